```python
import math
import jax
import jax.numpy as jnp
from jax import lax
import numpy as np

D_MODEL = 1024
BATCH = 2
SEQ = 8192
DEPTH = 2

MEM_LEN = 256
N_BRANCH = 4
BRANCH_WIDTH = 256
BLOCK_Q = 128
EPS = 1e-5
MAX_POS_OFFSET = 4096
NEG_BIG = -1e30

MLA_HEADS = 4
MLA_Q_LORA = 256
MLA_KV_LORA = 128
MLA_NOPE = 64
MLA_ROPE = 32
MLA_V = 64
ROPE_THETA = 10000.0

SWA_HEADS = 4
SWA_KV_HEADS = 2
SWA_HEAD_DIM = 64
SWA_WINDOW = 128

HGRN_HEADS = 4
HGRN_DK = 64
HGRN_DV = 64
HGRN_CHUNK = 64

SB_HEADS = 4
SB_HEAD_DIM = 64

REL_BUCKETS = 32
REL_MAX_DIST = 128

XA_HEADS = 4
XA_HEAD_DIM = 64

F_DENSE = 2816
N_EXPERTS = 8
TOP_K = 2
F_EXPERT = 3584
MOE_BLOCK = 256

DEEPNORM_ALPHA = (2 * DEPTH) ** 0.25
DEEPNORM_BETA = (8 * DEPTH) ** -0.25
N_DENSE = (DEPTH + 1) // 2
N_MOE = DEPTH // 2

IN_SPLITS = (
    ('mla_cq', MLA_Q_LORA),
    ('mla_ckv', MLA_KV_LORA),
    ('mla_kr', MLA_ROPE),
    ('swa_q', SWA_HEADS * SWA_HEAD_DIM),
    ('swa_k', SWA_KV_HEADS * SWA_HEAD_DIM),
    ('swa_v', SWA_KV_HEADS * SWA_HEAD_DIM),
    ('hgrn_q', HGRN_HEADS * HGRN_DK),
    ('hgrn_f', HGRN_HEADS * HGRN_DK),
    ('hgrn_i', HGRN_HEADS * HGRN_DV),
    ('hgrn_g', HGRN_HEADS * HGRN_DV),
    ('sb_q', SB_HEADS * SB_HEAD_DIM),
    ('sb_k', SB_HEADS * SB_HEAD_DIM),
    ('sb_v', SB_HEADS * SB_HEAD_DIM),
    ('gates', N_BRANCH * D_MODEL),
)
IN_COLS = sum(width for _, width in IN_SPLITS)

kernel_name = 'hybrid_mla_swa_hgrn2_stickbreak_moe_block'

F32 = jnp.float32


def _split_cols(h):
    out = {}
    off = 0
    for name, width in IN_SPLITS:
        out[name] = h[..., off:off + width]
        off += width
    return out


def _layernorm(x, g, b):
    xf = x.astype(F32)
    mu = jnp.mean(xf, axis=-1, keepdims=True)
    xc = xf - mu
    var = jnp.mean(xc * xc, axis=-1, keepdims=True)
    return (xc * lax.rsqrt(var + EPS) * g.astype(F32) + b.astype(F32)).astype(x.dtype)


def _rmsnorm(x, g):
    xf = x.astype(F32)
    return (xf * lax.rsqrt(jnp.mean(xf * xf, axis=-1, keepdims=True) + EPS) * g.astype(F32)).astype(x.dtype)


def _rope(x, positions):
    half = x.shape[-1] // 2
    inv_freq = ROPE_THETA ** (-jnp.arange(half, dtype=F32) / half)
    ang = positions.astype(F32)[..., None] * inv_freq
    ang = ang.reshape(ang.shape[:2] + (1,) * (x.ndim - 3) + (half,))
    cos, sin = jnp.cos(ang), jnp.sin(ang)
    x1, x2 = x[..., :half].astype(F32), x[..., half:].astype(F32)
    return jnp.concatenate([x1 * cos - x2 * sin, x1 * sin + x2 * cos], axis=-1).astype(x.dtype)


def _to_qblocks(t):
    b, s = t.shape[:2]
    return jnp.moveaxis(t.reshape((b, s // BLOCK_Q, BLOCK_Q) + t.shape[2:]), 1, 0)


def _from_qblocks(t):
    t = jnp.moveaxis(t, 0, 1)
    return t.reshape((t.shape[0], t.shape[1] * t.shape[2]) + t.shape[3:])


def _rel_bucket(dist):
    exact = REL_BUCKETS // 2
    n = jnp.maximum(dist, 0)
    nf = jnp.maximum(n, 1).astype(F32)
    large = exact + (jnp.log(nf / exact) / math.log(REL_MAX_DIST / exact) * (REL_BUCKETS - exact)).astype(jnp.int32)
    large = jnp.clip(large, 0, REL_BUCKETS - 1)
    return jnp.where(n < exact, n, large)


def _mla(cq, ckv, kr, positions, q_norm, w_uq, kv_norm, w_ukv):
    b, s, _ = cq.shape
    q = (_rmsnorm(cq, q_norm) @ w_uq).reshape(b, s, MLA_HEADS, MLA_NOPE + MLA_ROPE)
    q_nope = q[..., :MLA_NOPE]
    q_rope = _rope(q[..., MLA_NOPE:], positions)
    kv = (_rmsnorm(ckv, kv_norm) @ w_ukv).reshape(b, s, MLA_HEADS, MLA_NOPE + MLA_V)
    k_nope, v = kv[..., :MLA_NOPE], kv[..., MLA_NOPE:]
    k_rope = _rope(kr, positions)
    scale = (MLA_NOPE + MLA_ROPE) ** -0.5
    k_idx = jnp.arange(s)

    def block(args):
        qn, qr, q_idx = args
        logits = (jnp.einsum('bqhd,bkhd->bhqk', qn, k_nope).astype(F32)
                  + jnp.einsum('bqhr,bkr->bhqk', qr, k_rope).astype(F32)) * scale
        causal = k_idx[None, :] <= q_idx[:, None]
        logits = jnp.where(causal[None, None], logits, NEG_BIG)
        p = jax.nn.softmax(logits, axis=-1).astype(v.dtype)
        return jnp.einsum('bhqk,bkhd->bqhd', p, v)

    o = lax.map(block, (_to_qblocks(q_nope), _to_qblocks(q_rope), k_idx.reshape(-1, BLOCK_Q)))
    return _from_qblocks(o).reshape(b, s, MLA_HEADS * MLA_V)


def _swa(q, k, v, positions, sinks, rel_table):
    b, s, _ = q.shape
    w = SWA_WINDOW
    nb = s // w
    g = SWA_HEADS // SWA_KV_HEADS
    qb = q.reshape(b, nb, w, SWA_KV_HEADS, g, SWA_HEAD_DIM)

    def band(t):
        tb = t.reshape((b, nb, w) + t.shape[2:])
        prev = jnp.concatenate([jnp.zeros_like(tb[:, :1]), tb[:, :-1]], axis=1)
        return jnp.concatenate([prev, tb], axis=2)

    kb = band(k.reshape(b, s, SWA_KV_HEADS, SWA_HEAD_DIM))
    vb = band(v.reshape(b, s, SWA_KV_HEADS, SWA_HEAD_DIM))
    pq = positions.reshape(b, nb, w)
    pk = band(positions)
    logits = jnp.einsum('bnqhgd,bnkhd->bnhgqk', qb, kb).astype(F32) * SWA_HEAD_DIM ** -0.5
    bucket = _rel_bucket(pq[..., :, None] - pk[..., None, :])
    bias = jnp.moveaxis(rel_table[bucket].astype(F32), -1, 2)
    logits = logits + bias.reshape(b, nb, SWA_KV_HEADS, g, w, 2 * w)
    i = jnp.arange(w)[:, None]
    j = jnp.arange(2 * w)[None, :]
    in_window = (j >= i + 1) & (j <= i + w)
    valid = in_window[None] & ((jnp.arange(nb)[:, None, None] > 0) | (j >= w)[None])
    logits = jnp.where(valid[None, :, None, None], logits, NEG_BIG)
    sink = sinks.astype(F32).reshape(1, 1, SWA_KV_HEADS, g, 1, 1)
    m = jnp.maximum(jnp.max(logits, axis=-1, keepdims=True), sink)
    p = jnp.exp(logits - m)
    p = p / (jnp.sum(p, axis=-1, keepdims=True) + jnp.exp(sink - m))
    o = jnp.einsum('bnhgqk,bnkhd->bnqhgd', p.astype(v.dtype), vb)
    return o.reshape(b, s, SWA_HEADS * SWA_HEAD_DIM)


def _hgrn2(q, f, i, g, lower_bound, norm_w):
    dt = g.dtype
    b, s, _ = q.shape
    c = HGRN_CHUNK
    nc = s // c
    qf = jax.nn.silu(q.astype(F32))
    lb = lower_bound.astype(F32)
    forget = lb + (1.0 - lb) * jax.nn.sigmoid(f.astype(F32))
    log_f = jnp.log(forget)
    kf = 1.0 - forget
    vf = i.astype(F32)

    def chunks(t, d):
        return t.reshape(b, nc, c, HGRN_HEADS, d).transpose(1, 0, 3, 2, 4)

    causal = jnp.tril(jnp.ones((c, c), dtype=bool))

    def step(state, inp):
        qc, kc, vc, gc = inp
        bc = jnp.cumsum(gc, axis=2)
        o_inter = jnp.einsum('bhtk,bhkv->bhtv', qc * jnp.exp(bc), state)
        diff = bc[:, :, :, None, :] - bc[:, :, None, :, :]
        decay = jnp.exp(jnp.where(causal[:, :, None], diff, NEG_BIG))
        att = jnp.einsum('bhtk,bhtsk,bhsk->bhts', qc, decay, kc)
        o = o_inter + jnp.einsum('bhts,bhsv->bhtv', att, vc)
        b_last = bc[:, :, -1:, :]
        state = (jnp.exp(b_last[:, :, 0, :, None]) * state
                 + jnp.einsum('bhsk,bhsv->bhkv', kc * jnp.exp(b_last - bc), vc))
        return state, o

    s0 = jnp.zeros((b, HGRN_HEADS, HGRN_DK, HGRN_DV), F32)
    _, o = lax.scan(step, s0, (chunks(qf, HGRN_DK), chunks(kf, HGRN_DK), chunks(vf, HGRN_DV), chunks(log_f, HGRN_DK)))
    o = o.transpose(1, 0, 3, 2, 4).reshape(b, s, HGRN_HEADS, HGRN_DV)
    gate = jax.nn.silu(g.astype(F32).reshape(b, s, HGRN_HEADS, HGRN_DV))
    o = _rmsnorm(o, norm_w.reshape(HGRN_HEADS, HGRN_DV)) * gate
    return o.reshape(b, s, HGRN_HEADS * HGRN_DV).astype(dt)


def _stick_breaking(q, k, v):
    b, s, _ = q.shape
    q = q.reshape(b, s, SB_HEADS, SB_HEAD_DIM)
    k = k.reshape(b, s, SB_HEADS, SB_HEAD_DIM)
    v = v.reshape(b, s, SB_HEADS, SB_HEAD_DIM)
    k_idx = jnp.arange(s)

    def block(args):
        qb, q_idx = args
        z = jnp.einsum('bqhd,bkhd->bhqk', qb, k).astype(F32) * SB_HEAD_DIM ** -0.5
        strict = (k_idx[None, :] < q_idx[:, None])[None, None]
        log_1m_beta = jnp.where(strict, jax.nn.log_sigmoid(-z), 0.0)
        log_remain = lax.cumsum(log_1m_beta, axis=3, reverse=True) - log_1m_beta
        a = jnp.where(strict, jnp.exp(jax.nn.log_sigmoid(z) + log_remain), 0.0)
        return jnp.einsum('bhqk,bkhd->bqhd', a.astype(v.dtype), v)

    o = lax.map(block, (_to_qblocks(q), k_idx.reshape(-1, BLOCK_Q)))
    return _from_qblocks(o).reshape(b, s, SB_HEADS * SB_HEAD_DIM)


def _hybrid_mixer(x, positions, lower_bound, rel_table, w_in, mla_q_norm, mla_w_uq, mla_kv_norm,
                  mla_w_ukv, swa_sinks, hgrn_norm, w_branch, w_out):
    b, s, _ = x.shape
    p = _split_cols(x @ w_in)
    y_mla = _mla(p['mla_cq'], p['mla_ckv'], p['mla_kr'], positions, mla_q_norm, mla_w_uq, mla_kv_norm, mla_w_ukv)
    y_swa = _swa(p['swa_q'], p['swa_k'], p['swa_v'], positions, swa_sinks, rel_table)
    y_hgrn = _hgrn2(p['hgrn_q'], p['hgrn_f'], p['hgrn_i'], p['hgrn_g'], lower_bound, hgrn_norm)
    y_sb = _stick_breaking(p['sb_q'], p['sb_k'], p['sb_v'])
    branches = jnp.stack([y_mla, y_swa, y_hgrn, y_sb], axis=2)
    gates = jax.nn.sigmoid(p['gates'].reshape(b, s, N_BRANCH, D_MODEL))
    merged = jnp.sum(gates * jnp.einsum('bsnc,ncd->bsnd', branches, w_branch), axis=2)
    return merged @ w_out


def _mem_xattn(x, mem, wq, wkv, wo):
    b, s, _ = x.shape
    m = mem.shape[1]
    q = (x @ wq).reshape(b, s, XA_HEADS, XA_HEAD_DIM)
    kv = (mem @ wkv).reshape(b, m, 2, XA_HEADS, XA_HEAD_DIM)
    k, v = kv[:, :, 0], kv[:, :, 1]
    logits = jnp.einsum('bshd,bmhd->bhsm', q, k).astype(F32) * XA_HEAD_DIM ** -0.5
    p = jax.nn.softmax(logits, axis=-1).astype(v.dtype)
    o = jnp.einsum('bhsm,bmhd->bshd', p, v).reshape(b, s, XA_HEADS * XA_HEAD_DIM)
    return o @ wo


def _swiglu(x, w13, w2):
    a, gate = jnp.split(x @ w13, 2, axis=-1)
    return (jax.nn.silu(a) * gate) @ w2


def _moe(x, router, w13, w2):
    b, s, d = x.shape
    n = b * s
    a = n * TOP_K
    xf = x.reshape(n, d)
    logits = (xf @ router).astype(F32)
    top_val, top_idx = lax.top_k(logits, TOP_K)
    top_w = jax.nn.softmax(top_val, axis=-1)
    e_flat = top_idx.reshape(-1)
    t_flat = jnp.repeat(jnp.arange(n, dtype=jnp.int32), TOP_K)
    w_flat = top_w.reshape(-1)
    order = jnp.argsort(e_flat)
    e_s, t_s, w_s = e_flat[order], t_flat[order], w_flat[order]
    counts = jnp.bincount(e_flat, length=N_EXPERTS)
    padded = (counts + MOE_BLOCK - 1) // MOE_BLOCK * MOE_BLOCK
    group_start = jnp.cumsum(counts) - counts
    padded_end = jnp.cumsum(padded)
    padded_start = padded_end - padded
    dest = padded_start[e_s] + jnp.arange(a, dtype=jnp.int32) - group_start[e_s]
    cap = a + N_EXPERTS * MOE_BLOCK
    n_blk = cap // MOE_BLOCK
    slot_tok = jnp.full((cap,), n, dtype=jnp.int32).at[dest].set(t_s)
    slot_w = jnp.zeros((cap,), F32).at[dest].set(w_s)
    blk_exp = jnp.minimum(jnp.searchsorted(padded_end, jnp.arange(n_blk) * MOE_BLOCK, side='right'), N_EXPERTS - 1)
    x_pad = jnp.concatenate([xf, jnp.zeros((1, d), x.dtype)], axis=0)
    xb = x_pad[slot_tok].reshape(n_blk, MOE_BLOCK, d)

    def expert_block(args):
        xe, e = args
        return _swiglu(xe, w13[e], w2[e])

    yb = lax.map(expert_block, (xb, blk_exp)).reshape(cap, d)
    y = jnp.zeros((n + 1, d), x.dtype).at[slot_tok].add(yb * slot_w[:, None].astype(x.dtype))
    return y[:n].reshape(b, s, d)


def setup_inputs(seed: int = 0) -> dict:
    key = jax.random.key(seed)
    ks = list(jax.random.split(key, 24))

    def nrm(k, shape, scale):
        return jax.random.normal(k, shape, F32) * scale

    x = nrm(ks[0], (BATCH, SEQ, D_MODEL), 1.0)
    mem = nrm(ks[1], (BATCH, MEM_LEN, D_MODEL), 1.0)
    offset = jax.random.randint(ks[2], (BATCH, 1), 0, MAX_POS_OFFSET, dtype=jnp.int32)
    positions = (offset + jnp.arange(SEQ, dtype=jnp.int32)[None, :]).astype(jnp.int32)
    rel_bias_table = nrm(ks[3], (REL_BUCKETS, SWA_HEADS), 0.5)
    hgrn_lb_logits = nrm(ks[4], (DEPTH, HGRN_HEADS * HGRN_DK), 0.5)
    w_in = nrm(ks[5], (DEPTH, D_MODEL, IN_COLS), D_MODEL ** -0.5)
    mla_q_norm = 1.0 + nrm(ks[6], (DEPTH, MLA_Q_LORA), 0.02)
    mla_w_uq = nrm(ks[7], (DEPTH, MLA_Q_LORA, MLA_HEADS * (MLA_NOPE + MLA_ROPE)), MLA_Q_LORA ** -0.5)
    mla_kv_norm = 1.0 + nrm(ks[8], (DEPTH, MLA_KV_LORA), 0.02)
    mla_w_ukv = nrm(ks[9], (DEPTH, MLA_KV_LORA, MLA_HEADS * (MLA_NOPE + MLA_V)), MLA_KV_LORA ** -0.5)
    swa_sinks = nrm(ks[10], (DEPTH, SWA_HEADS), 1.0)
    hgrn_norm = 1.0 + nrm(ks[11], (DEPTH, HGRN_HEADS * HGRN_DV), 0.02)
    w_branch = nrm(ks[12], (DEPTH, N_BRANCH, BRANCH_WIDTH, D_MODEL), BRANCH_WIDTH ** -0.5)
    w_out = nrm(ks[13], (DEPTH, D_MODEL, D_MODEL), D_MODEL ** -0.5 * DEEPNORM_BETA)
    ln_g = 1.0 + nrm(ks[14], (DEPTH, 3, D_MODEL), 0.02)
    ln_b = nrm(ks[15], (DEPTH, 3, D_MODEL), 0.02)
    xa_wq = nrm(ks[16], (DEPTH, D_MODEL, XA_HEADS * XA_HEAD_DIM), D_MODEL ** -0.5)
    xa_wkv = nrm(ks[17], (DEPTH, D_MODEL, 2 * XA_HEADS * XA_HEAD_DIM), D_MODEL ** -0.5)
    xa_wo = nrm(ks[18], (DEPTH, XA_HEADS * XA_HEAD_DIM, D_MODEL), (XA_HEADS * XA_HEAD_DIM) ** -0.5 * DEEPNORM_BETA)
    ffn_w13 = nrm(ks[19], (N_DENSE, D_MODEL, 2 * F_DENSE), D_MODEL ** -0.5)
    ffn_w2 = nrm(ks[20], (N_DENSE, F_DENSE, D_MODEL), F_DENSE ** -0.5 * DEEPNORM_BETA)
    moe_router = nrm(ks[21], (N_MOE, D_MODEL, N_EXPERTS), D_MODEL ** -0.5)
    moe_w13 = nrm(ks[22], (N_MOE, N_EXPERTS, D_MODEL, 2 * F_EXPERT), D_MODEL ** -0.5)
    moe_w2 = nrm(ks[23], (N_MOE, N_EXPERTS, F_EXPERT, D_MODEL), F_EXPERT ** -0.5 * DEEPNORM_BETA)
    return {'x': x, 'mem': mem, 'positions': positions, 'rel_bias_table': rel_bias_table,
            'hgrn_lb_logits': hgrn_lb_logits, 'w_in': w_in, 'mla_q_norm': mla_q_norm, 'mla_w_uq': mla_w_uq,
            'mla_kv_norm': mla_kv_norm, 'mla_w_ukv': mla_w_ukv, 'swa_sinks': swa_sinks, 'hgrn_norm': hgrn_norm,
            'w_branch': w_branch, 'w_out': w_out, 'ln_g': ln_g, 'ln_b': ln_b, 'xa_wq': xa_wq, 'xa_wkv': xa_wkv,
            'xa_wo': xa_wo, 'ffn_w13': ffn_w13, 'ffn_w2': ffn_w2, 'moe_router': moe_router,
            'moe_w13': moe_w13, 'moe_w2': moe_w2}


def reference(x, mem, positions, rel_bias_table, hgrn_lb_logits, w_in, mla_q_norm, mla_w_uq, mla_kv_norm,
              mla_w_ukv, swa_sinks, hgrn_norm, w_branch, w_out, ln_g, ln_b, xa_wq, xa_wkv, xa_wo,
              ffn_w13, ffn_w2, moe_router, moe_w13, moe_w2):
    sm = jax.nn.softmax(hgrn_lb_logits.astype(F32), axis=0)
    lower_bounds = jnp.cumsum(sm, axis=0) - sm[0]
    for l in range(DEPTH):
        y = _hybrid_mixer(x, positions, lower_bounds[l], rel_bias_table, w_in[l], mla_q_norm[l], mla_w_uq[l],
                          mla_kv_norm[l], mla_w_ukv[l], swa_sinks[l], hgrn_norm[l], w_branch[l], w_out[l])
        x = _layernorm(DEEPNORM_ALPHA * x + y, ln_g[l, 0], ln_b[l, 0])
        y = _mem_xattn(x, mem, xa_wq[l], xa_wkv[l], xa_wo[l])
        x = _layernorm(DEEPNORM_ALPHA * x + y, ln_g[l, 1], ln_b[l, 1])
        if l % 2 == 0:
            y = _swiglu(x, ffn_w13[l // 2], ffn_w2[l // 2])
        else:
            y = _moe(x, moe_router[l // 2], moe_w13[l // 2], moe_w2[l // 2])
        x = _layernorm(DEEPNORM_ALPHA * x + y, ln_g[l, 2], ln_b[l, 2])
    return x
```

```python
import functools
import math

import jax
import jax.numpy as jnp
from jax import lax
from jax.experimental import pallas as pl
from jax.experimental.pallas import tpu as pltpu

F32 = jnp.float32
BF16 = jnp.bfloat16
I32 = jnp.int32

D_MODEL = 1024
DEPTH = 2
EPS = 1e-5
NEG_BIG = -1e30
LANES = 128
HEAD_DIM = 64
N_HEADS = 4
WIDTH = N_HEADS * HEAD_DIM

MLA_Q_LORA = 256
MLA_KV_LORA = 128
MLA_NOPE = 64
MLA_ROPE = 32
ROPE_THETA = 10000.0
MLA_SCALE = (MLA_NOPE + MLA_ROPE) ** -0.5
QK_SCALE = HEAD_DIM ** -0.5

SWA_WINDOW = 128
REL_BUCKETS = 32
REL_MAX_DIST = 128
HGRN_CHUNK = 64
N_EXPERTS = 8
F_DENSE = 2816
F_EXPERT = 3584
ALPHA = (2 * DEPTH) ** 0.25

_IN_SPLITS = (('mla_cq', 256), ('mla_ckv', 128), ('mla_kr', 32), ('swa_q', 256), ('swa_k', 128),
              ('swa_v', 128), ('hgrn', 1024), ('sb_q', 256), ('sb_k', 256), ('sb_v', 256), ('gates', 4096))
_IN_OFF = {}
_o = 0
for _n, _w in _IN_SPLITS:
    _IN_OFF[_n] = _o
    _o += _w

_A_SPLITS = (('cq', 256), ('ckv', 128), ('kra', 128), ('krb', 128), ('swa_q', 256), ('swa_k', 256),
             ('swa_v', 256), ('hgrn', 1024), ('sb_q', 256), ('sb_k', 256), ('sb_v', 256))
_A_OFF = {}
_o = 0
for _n, _w in _A_SPLITS:
    _A_OFF[_n] = (_o, _o + _w)
    _o += _w
A_COLS = _o

TM_A = 512
TQ_ATT = 256
HG_ROWS = 256
TM_FFN = 512
TF_FFN = 1408
MOE_TB = 512
MOE_TF = 512
TM_COMB = 256
VMEM_LIMIT = 56 * 1024 * 1024


def _cparams(sem, **kw):
    return pltpu.CompilerParams(dimension_semantics=sem, vmem_limit_bytes=VMEM_LIMIT, **kw)


def _const_spec(shape):
    nd = len(shape)
    return pl.BlockSpec(shape, lambda *_: (0,) * nd)


def _layernorm(v, g, b):
    mu = jnp.mean(v, axis=-1, keepdims=True)
    vc = v - mu
    var = jnp.mean(vc * vc, axis=-1, keepdims=True)
    return vc * lax.rsqrt(var + EPS) * g + b


def _dot(a, b):
    return jnp.dot(a, b, preferred_element_type=F32)


def _dot_nt(a, b):
    return lax.dot_general(a, b, (((1,), (1,)), ((), ())), preferred_element_type=F32)


def _split3(a):
    hi = a.astype(BF16)
    r = a - hi.astype(F32)
    mid = r.astype(BF16)
    lo = (r - mid.astype(F32)).astype(BF16)
    return hi, mid, lo


def _rope_kernel(pos_ref, freq_ref, c_ref, s_ref):
    lane = lax.broadcasted_iota(I32, pos_ref.shape, 1)
    ang = pos_ref[...] * freq_ref[...]
    rope = (lane >= MLA_NOPE) & (lane < MLA_NOPE + MLA_ROPE)
    first = lane < MLA_NOPE + MLA_ROPE // 2
    c_ref[...] = jnp.where(lane < MLA_NOPE, 1.0, jnp.where(rope, jnp.cos(ang), 0.0))
    sn = jnp.sin(ang)
    s_ref[...] = jnp.where(rope, jnp.where(first, -sn, sn), 0.0)


def _rope_tables(positions):
    n = positions.size
    half = MLA_ROPE // 2
    inv_freq = ROPE_THETA ** (-jnp.arange(half, dtype=F32) / half)
    freq = jnp.zeros((1, LANES), F32).at[0, MLA_NOPE:MLA_NOPE + MLA_ROPE].set(jnp.tile(inv_freq, 2))
    posb = jnp.broadcast_to(positions.reshape(n, 1).astype(F32), (n, LANES))
    tm = 1024
    return pl.pallas_call(
        _rope_kernel,
        grid=(n // tm,),
        in_specs=[pl.BlockSpec((tm, LANES), lambda i: (i, 0)), _const_spec((1, LANES))],
        out_specs=[pl.BlockSpec((tm, LANES), lambda i: (i, 0))] * 2,
        out_shape=[jax.ShapeDtypeStruct((n, LANES), F32)] * 2,
        compiler_params=_cparams(("parallel",)),
        name="rope_tables",
    )(posb, freq)


def _inproj_kernel(x_ref, w_ref, c_ref, s_ref, qn_ref, kvn_ref, wuqa_ref, wuqb_ref, wuk_ref, wuv_ref,
                   mq_ref, mk_ref, mv_ref, swq_ref, swk_ref, swv_ref, hg_ref, sbq_ref, sbk_ref, sbv_ref):
    h = _dot(x_ref[...].astype(BF16), w_ref[...])

    def cols(name):
        lo, hi = _A_OFF[name]
        return h[:, lo:hi]

    c = c_ref[...]
    s = s_ref[...]
    c4 = jnp.concatenate([c] * N_HEADS, axis=1)
    s4 = jnp.concatenate([s] * N_HEADS, axis=1)

    cq = cols('cq')
    cqn = (cq * lax.rsqrt(jnp.mean(cq * cq, axis=-1, keepdims=True) + EPS) * qn_ref[...]).astype(BF16)
    q = _dot(cqn, wuqa_ref[...]) * c4 + _dot(cqn, wuqb_ref[...]) * s4
    mq_ref[...] = (q * MLA_SCALE).astype(BF16)

    ckv = cols('ckv')
    ckvn = (ckv * lax.rsqrt(jnp.mean(ckv * ckv, axis=-1, keepdims=True) + EPS) * kvn_ref[...]).astype(BF16)
    krot = cols('kra') * c + cols('krb') * s
    mk_ref[...] = (_dot(ckvn, wuk_ref[...]) + jnp.concatenate([krot] * N_HEADS, axis=1)).astype(BF16)
    mv_ref[...] = _dot(ckvn, wuv_ref[...]).astype(BF16)

    swq_ref[...] = cols('swa_q').astype(BF16)
    swk_ref[...] = cols('swa_k').astype(BF16)
    swv_ref[...] = cols('swa_v').astype(BF16)
    hg_ref[...] = cols('hgrn')
    sbq_ref[...] = cols('sb_q').astype(BF16)
    sbk_ref[...] = cols('sb_k').astype(BF16)
    sbv_ref[...] = cols('sb_v').astype(BF16)


def _inproj_weights(w_in, w_uq, w_ukv):
    def seg(name, width):
        o = _IN_OFF[name]
        return w_in[:, o:o + width]

    kr = seg('mla_kr', MLA_ROPE)
    half = MLA_ROPE // 2
    z64 = jnp.zeros((D_MODEL, MLA_NOPE), F32)
    z32 = jnp.zeros((D_MODEL, LANES - MLA_NOPE - MLA_ROPE), F32)
    kra = jnp.concatenate([z64, kr, z32], axis=1)
    krb = jnp.concatenate([z64, kr[:, half:], kr[:, :half], z32], axis=1)
    swk = seg('swa_k', 128)
    swv = seg('swa_v', 128)
    dup = lambda t: jnp.concatenate([t[:, :64], t[:, :64], t[:, 64:], t[:, 64:]], axis=1)
    w_a = jnp.concatenate([
        seg('mla_cq', 256), seg('mla_ckv', 128), kra, krb,
        seg('swa_q', 256) * QK_SCALE, dup(swk), dup(swv),
        seg('hgrn', 1024), seg('sb_q', 256) * QK_SCALE, seg('sb_k', 256), seg('sb_v', 256)], axis=1)

    qd = MLA_NOPE + MLA_ROPE
    zq = jnp.zeros((MLA_Q_LORA, LANES - qd), F32)
    zn = jnp.zeros((MLA_Q_LORA, MLA_NOPE), F32)
    qa, qb = [], []
    for hh in range(N_HEADS):
        nope = w_uq[:, hh * qd: hh * qd + MLA_NOPE]
        rope = w_uq[:, hh * qd + MLA_NOPE: (hh + 1) * qd]
        qa += [nope, rope, zq]
        qb += [zn, rope[:, half:], rope[:, :half], zq]
    wuqa = jnp.concatenate(qa, axis=1)
    wuqb = jnp.concatenate(qb, axis=1)
    lane = jnp.arange(N_HEADS * LANES) % LANES
    wuk = jnp.where(lane[None, :] < MLA_NOPE, w_ukv, 0.0)
    wuv = jnp.concatenate([w_ukv[:, hh * LANES + MLA_NOPE:(hh + 1) * LANES] for hh in range(N_HEADS)], axis=1)
    return tuple(t.astype(BF16) for t in (w_a, wuqa, wuqb, wuk, wuv))


def _inproj(x2d, wts, ctab, stab, q_norm, kv_norm):
    n = x2d.shape[0]
    w_a, wuqa, wuqb, wuk, wuv = wts
    tm = TM_A
    row = lambda w: pl.BlockSpec((tm, w), lambda i: (i, 0))
    out_w = (512, 512, 256, 256, 256, 256, 1024, 256, 256, 256)
    out_dt = (BF16, BF16, BF16, BF16, BF16, BF16, F32, BF16, BF16, BF16)
    return pl.pallas_call(
        _inproj_kernel,
        grid=(n // tm,),
        in_specs=[row(D_MODEL), _const_spec(w_a.shape), row(LANES), row(LANES),
                  _const_spec((1, MLA_Q_LORA)), _const_spec((1, MLA_KV_LORA)),
                  _const_spec(wuqa.shape), _const_spec(wuqb.shape), _const_spec(wuk.shape),
                  _const_spec(wuv.shape)],
        out_specs=[row(w) for w in out_w],
        out_shape=[jax.ShapeDtypeStruct((n, w), d) for w, d in zip(out_w, out_dt)],
        compiler_params=_cparams(("parallel",)),
        name="inproj",
    )(x2d, w_a, ctab, stab, q_norm.reshape(1, -1), kv_norm.reshape(1, -1), wuqa, wuqb, wuk, wuv)


def _half_mask(half):
    lane = lax.broadcasted_iota(I32, (1, LANES), 1)
    return (lane < HEAD_DIM) if half == 0 else (lane >= HEAD_DIM)


def _mla_kernel(q_ref, k_ref, v_ref, o_ref):
    tq = q_ref.shape[0]
    i = pl.program_id(1)
    row = lax.broadcasted_iota(I32, (tq, tq), 0)
    col = lax.broadcasted_iota(I32, (tq, tq), 1)
    causal = col <= row
    outs = []
    for p in range(N_HEADS // 2):
        halves = []
        for half in range(2):
            hh = 2 * p + half
            qh = q_ref[:, hh * LANES:(hh + 1) * LANES]

            def step(j, carry, masked, qh=qh, hh=hh, p=p):
                m, l, acc = carry
                off = pl.multiple_of(j * tq, tq)
                kb = k_ref[pl.ds(off, tq), hh * LANES:(hh + 1) * LANES]
                s = _dot_nt(qh, kb)
                if masked:
                    s = jnp.where(causal, s, NEG_BIG)
                m_new = jnp.maximum(m, jnp.max(s, axis=-1, keepdims=True))
                a = jnp.exp(m - m_new)
                pm = jnp.exp(s - m_new)
                l = a * l + jnp.sum(pm, axis=-1, keepdims=True)
                vb = v_ref[pl.ds(off, tq), p * LANES:(p + 1) * LANES]
                acc = a * acc + _dot(pm.astype(BF16), vb)
                return m_new, l, acc

            init = (jnp.full((tq, 1), NEG_BIG, F32), jnp.zeros((tq, 1), F32), jnp.zeros((tq, LANES), F32))
            carry = lax.fori_loop(0, i, functools.partial(step, masked=False), init)
            _, l, acc = step(i, carry, True)
            halves.append(acc / l)
        outs.append(jnp.where(_half_mask(0), halves[0], halves[1]))
    o_ref[...] = jnp.concatenate(outs, axis=1).astype(o_ref.dtype)


def _mla_attention(q, k, v, batch, seq):
    tq = TQ_ATT
    q3, k3, v3 = (t.reshape(batch, seq, t.shape[-1]) for t in (q, k, v))
    out = pl.pallas_call(
        _mla_kernel,
        grid=(batch, seq // tq),
        in_specs=[pl.BlockSpec((None, tq, 512), lambda b, i: (b, i, 0)),
                  pl.BlockSpec((None, seq, 512), lambda b, i: (b, 0, 0)),
                  pl.BlockSpec((None, seq, WIDTH), lambda b, i: (b, 0, 0))],
        out_specs=pl.BlockSpec((None, tq, WIDTH), lambda b, i: (b, i, 0)),
        out_shape=jax.ShapeDtypeStruct((batch, seq, WIDTH), BF16),
        compiler_params=_cparams(("parallel", "arbitrary")),
        name="mla_attention",
    )(q3, k3, v3)
    return out.reshape(batch * seq, WIDTH)


def _sb_kernel(q_ref, k_ref, v_ref, o_ref):
    tq = q_ref.shape[0]
    i = pl.program_id(1)
    row = lax.broadcasted_iota(I32, (tq, tq), 0)
    col = lax.broadcasted_iota(I32, (tq, tq), 1)
    strict = col < row
    later = (row > col).astype(BF16)
    outs = []
    for p in range(N_HEADS // 2):
        halves = []
        qp = q_ref[:, p * LANES:(p + 1) * LANES]
        for half in range(2):
            qh = jnp.where(_half_mask(half), qp, jnp.zeros_like(qp))

            def block(j, carry, diag, qh=qh, p=p):
                run, acc = carry
                off = pl.multiple_of(j * tq, tq)
                kb = k_ref[pl.ds(off, tq), p * LANES:(p + 1) * LANES]
                z = _dot_nt(qh, kb)
                lsn = -(jnp.maximum(z, 0.0) + jnp.log(1.0 + jnp.exp(-jnp.abs(z))))
                if diag:
                    lsn = jnp.where(strict, lsn, 0.0)
                hi = lsn.astype(BF16)
                lo = (lsn - hi.astype(F32)).astype(BF16)
                rem = _dot(hi, later) + _dot(lo, later)
                arg = z + lsn + rem + run
                if diag:
                    arg = jnp.where(strict, arg, NEG_BIG)
                a = jnp.exp(arg)
                run = run + rem[:, 0:1] + lsn[:, 0:1]
                vb = v_ref[pl.ds(off, tq), p * LANES:(p + 1) * LANES]
                acc = acc + _dot(a.astype(BF16), vb)
                return run, acc

            carry = block(i, (jnp.zeros((tq, 1), F32), jnp.zeros((tq, LANES), F32)), True)
            _, acc = lax.fori_loop(0, i, lambda jj, c: block(i - 1 - jj, c, False), carry)
            halves.append(acc)
        outs.append(jnp.where(_half_mask(0), halves[0], halves[1]))
    o_ref[...] = jnp.concatenate(outs, axis=1).astype(o_ref.dtype)


def _sb_attention(q, k, v, batch, seq):
    tq = TQ_ATT
    q3, k3, v3 = (t.reshape(batch, seq, WIDTH) for t in (q, k, v))
    out = pl.pallas_call(
        _sb_kernel,
        grid=(batch, seq // tq),
        in_specs=[pl.BlockSpec((None, tq, WIDTH), lambda b, i: (b, i, 0)),
                  pl.BlockSpec((None, seq, WIDTH), lambda b, i: (b, 0, 0)),
                  pl.BlockSpec((None, seq, WIDTH), lambda b, i: (b, 0, 0))],
        out_specs=pl.BlockSpec((None, tq, WIDTH), lambda b, i: (b, i, 0)),
        out_shape=jax.ShapeDtypeStruct((batch, seq, WIDTH), BF16),
        compiler_params=_cparams(("parallel", "arbitrary")),
        name="stick_breaking",
    )(q3, k3, v3)
    return out.reshape(batch * seq, WIDTH)


def _rel_bucket(dist):
    exact = REL_BUCKETS // 2
    n = jnp.maximum(dist, 0)
    nf = jnp.maximum(n, 1).astype(F32)
    large = exact + (jnp.log(nf / exact) / math.log(REL_MAX_DIST / exact) * (REL_BUCKETS - exact)).astype(I32)
    large = jnp.clip(large, 0, REL_BUCKETS - 1)
    return jnp.where(n < exact, n, large)


def _swa_kernel(tab_ref, sink_ref, q_ref, kc_ref, kp_ref, vc_ref, vp_ref, pq_ref, pkc_ref, pkp_ref, o_ref):
    w = SWA_WINDOW
    nblk = pl.program_id(1)
    row = lax.broadcasted_iota(I32, (w, w), 0)
    col = lax.broadcasted_iota(I32, (w, w), 1)
    valid_c = col <= row
    valid_p = (col > row) & (nblk > 0)
    pq = pq_ref[...]
    bucket_c = _rel_bucket(pq - pkc_ref[...])
    bucket_p = _rel_bucket(pq - pkp_ref[...])
    bias_c = [jnp.zeros((w, w), F32) for _ in range(N_HEADS)]
    bias_p = [jnp.zeros((w, w), F32) for _ in range(N_HEADS)]
    for b in range(REL_BUCKETS):
        mc = bucket_c == b
        mp = bucket_p == b
        for hh in range(N_HEADS):
            t = tab_ref[b, hh]
            bias_c[hh] = jnp.where(mc, t, bias_c[hh])
            bias_p[hh] = jnp.where(mp, t, bias_p[hh])
    outs = []
    for p in range(N_HEADS // 2):
        sl = slice(p * LANES, (p + 1) * LANES)
        qp = q_ref[:, sl]
        kc, kp, vc, vp = kc_ref[:, sl], kp_ref[:, sl], vc_ref[:, sl], vp_ref[:, sl]
        halves = []
        for half in range(2):
            hh = 2 * p + half
            qh = jnp.where(_half_mask(half), qp, jnp.zeros_like(qp))
            lc = jnp.where(valid_c, _dot_nt(qh, kc) + bias_c[hh], NEG_BIG)
            lp = jnp.where(valid_p, _dot_nt(qh, kp) + bias_p[hh], NEG_BIG)
            sink = sink_ref[hh]
            m = jnp.maximum(jnp.maximum(jnp.max(lc, axis=-1, keepdims=True),
                                        jnp.max(lp, axis=-1, keepdims=True)), sink)
            ec = jnp.exp(lc - m)
            ep = jnp.exp(lp - m)
            den = jnp.sum(ec, axis=-1, keepdims=True) + jnp.sum(ep, axis=-1, keepdims=True) + jnp.exp(sink - m)
            halves.append(_dot((ec / den).astype(BF16), vc) + _dot((ep / den).astype(BF16), vp))
        outs.append(jnp.where(_half_mask(0), halves[0], halves[1]))
    o_ref[...] = jnp.concatenate(outs, axis=1).astype(o_ref.dtype)


def _swa_attention(q, k, v, positions, sinks, rel_table, batch, seq):
    w = SWA_WINDOW
    nb = seq // w
    q3, k3, v3 = (t.reshape(batch, seq, WIDTH) for t in (q, k, v))
    pcol = positions.reshape(batch, seq, 1)
    prow = positions.reshape(batch, 1, seq)
    cur = lambda b, n: (b, n, 0)
    prev = lambda b, n: (b, jnp.maximum(n - 1, 0), 0)
    smem = pl.BlockSpec(memory_space=pltpu.SMEM)
    out = pl.pallas_call(
        _swa_kernel,
        grid=(batch, nb),
        in_specs=[smem, smem,
                  pl.BlockSpec((None, w, WIDTH), cur),
                  pl.BlockSpec((None, w, WIDTH), cur), pl.BlockSpec((None, w, WIDTH), prev),
                  pl.BlockSpec((None, w, WIDTH), cur), pl.BlockSpec((None, w, WIDTH), prev),
                  pl.BlockSpec((None, w, 1), cur),
                  pl.BlockSpec((None, 1, w), lambda b, n: (b, 0, n)),
                  pl.BlockSpec((None, 1, w), lambda b, n: (b, 0, jnp.maximum(n - 1, 0)))],
        out_specs=pl.BlockSpec((None, w, WIDTH), cur),
        out_shape=jax.ShapeDtypeStruct((batch, seq, WIDTH), BF16),
        compiler_params=_cparams(("parallel", "arbitrary")),
        name="swa_attention",
    )(rel_table.astype(F32), sinks.astype(F32), q3, k3, k3, v3, v3, pcol, prow, prow)
    return out.reshape(batch * seq, WIDTH)


def _hgrn_kernel(hg_ref, lb_ref, nw_ref, o_ref, state_ref, w_ref):
    c = HGRN_CHUNK
    @pl.when(pl.program_id(1) == 0)
    def _():
        state_ref[...] = jnp.zeros_like(state_ref)

    r64 = lax.broadcasted_iota(I32, (c, c), 0)
    c64 = lax.broadcasted_iota(I32, (c, c), 1)
    incl = (c64 <= r64).astype(BF16)
    ra = lax.broadcasted_iota(I32, (WIDTH, WIDTH), 0) // HEAD_DIM
    ca = lax.broadcasted_iota(I32, (WIDTH, WIDTH), 1) // HEAD_DIM
    same_head = ra == ca
    seg = same_head.astype(BF16)
    ones_cols = jnp.ones((c, LANES), BF16)
    trow = lax.broadcasted_iota(I32, (c, WIDTH), 0)
    lb = lb_ref[...]
    nw = nw_ref[...]
    group = 8

    for ch in range(hg_ref.shape[0] // c):
        rows = slice(ch * c, (ch + 1) * c)
        qraw = hg_ref[rows, 0:WIDTH]
        fraw = hg_ref[rows, WIDTH:2 * WIDTH]
        v = hg_ref[rows, 2 * WIDTH:3 * WIDTH]
        graw = hg_ref[rows, 3 * WIDTH:4 * WIDTH]
        qf = qraw * jax.nn.sigmoid(qraw)
        forget = lb + (1.0 - lb) * jax.nn.sigmoid(fraw)
        lf = jnp.log(forget)
        kk = 1.0 - forget
        gate = graw * jax.nn.sigmoid(graw)

        lf3 = _split3(lf)
        bc = _dot(incl, lf3[0]) + _dot(incl, lf3[1]) + _dot(incl, lf3[2])
        b_last = bc[c - 1:c, :]
        dn0 = (((0,), (0,)), ((), ()))
        tot_col = sum(lax.dot_general(t, ones_cols, dn0, preferred_element_type=F32) for t in lf3)
        decay_col = jnp.exp(jnp.concatenate([tot_col, tot_col], axis=1))

        state = state_ref[...]
        o = _dot((qf * jnp.exp(bc)).astype(BF16), state.astype(BF16))

        vb = v
        for g0 in range(0, c, group):
            for s_i in range(g0, g0 + group):
                e = jnp.exp(jnp.minimum(bc - bc[s_i:s_i + 1, :], 0.0))
                wgt = jnp.where(trow >= s_i, qf * kk[s_i:s_i + 1, :] * e, 0.0)
                w_ref[(s_i - g0) * c:(s_i - g0 + 1) * c, :] = wgt.astype(BF16)
            att = _dot(w_ref[...], seg)
            for s_i in range(g0, g0 + group):
                o = o + att[(s_i - g0) * c:(s_i - g0 + 1) * c, :] * vb[s_i:s_i + 1, :]

        khat = (kk * jnp.exp(b_last - bc)).astype(BF16)
        upd = lax.dot_general(khat, v.astype(BF16), dn0, preferred_element_type=F32)
        state_ref[...] = decay_col * state + jnp.where(same_head, upd, 0.0)

        o2 = _split3(o * o)
        ms = (_dot(o2[0], seg) + _dot(o2[1], seg)) * (1.0 / HEAD_DIM)
        o_ref[rows, :] = (o * lax.rsqrt(ms + EPS) * nw * gate).astype(o_ref.dtype)


def _hgrn(hg, lower_bound, norm_w, batch, seq):
    rows = HG_ROWS
    hg3 = hg.reshape(batch, seq, 4 * WIDTH)
    out = pl.pallas_call(
        _hgrn_kernel,
        grid=(batch, seq // rows),
        in_specs=[pl.BlockSpec((None, rows, 4 * WIDTH), lambda b, i: (b, i, 0)),
                  _const_spec((1, WIDTH)), _const_spec((1, WIDTH))],
        out_specs=pl.BlockSpec((None, rows, WIDTH), lambda b, i: (b, i, 0)),
        out_shape=jax.ShapeDtypeStruct((batch, seq, WIDTH), BF16),
        scratch_shapes=[pltpu.VMEM((WIDTH, WIDTH), F32), pltpu.VMEM((8 * HGRN_CHUNK, WIDTH), BF16)],
        compiler_params=_cparams(("parallel", "arbitrary")),
        name="hgrn2",
    )(hg3, lower_bound.reshape(1, WIDTH).astype(F32), norm_w.reshape(1, WIDTH).astype(F32))
    return out.reshape(batch * seq, WIDTH)


def _merge_kernel(x_ref, y0_ref, y1_ref, y2_ref, y3_ref, wg_ref, wb_ref, wo_ref, g_ref, b_ref, o_ref):
    x = x_ref[...]
    xb = x.astype(BF16)
    merged = jnp.zeros(x.shape, F32)
    for nbr, y_ref in enumerate((y0_ref, y1_ref, y2_ref, y3_ref)):
        gate = jax.nn.sigmoid(_dot(xb, wg_ref[:, nbr * D_MODEL:(nbr + 1) * D_MODEL]))
        merged = merged + gate * _dot(y_ref[...], wb_ref[nbr])
    y = _dot(merged.astype(BF16), wo_ref[...])
    o_ref[...] = _layernorm(ALPHA * x + y, g_ref[...], b_ref[...])


def _merge(x2d, ys, wg, wb, wo, g, b):
    n = x2d.shape[0]
    tm = TM_A
    row = lambda w: pl.BlockSpec((tm, w), lambda i: (i, 0))
    return pl.pallas_call(
        _merge_kernel,
        grid=(n // tm,),
        in_specs=[row(D_MODEL)] + [row(WIDTH)] * 4 +
                 [_const_spec(wg.shape), _const_spec(wb.shape), _const_spec(wo.shape),
                  _const_spec((1, D_MODEL)), _const_spec((1, D_MODEL))],
        out_specs=row(D_MODEL),
        out_shape=jax.ShapeDtypeStruct((n, D_MODEL), F32),
        compiler_params=_cparams(("parallel",)),
        name="merge_outproj_ln",
    )(x2d, *ys, wg, wb, wo, g.reshape(1, -1), b.reshape(1, -1))


def _memkv_kernel(m_ref, w_ref, k_ref, v_ref):
    kv = _dot(m_ref[...].astype(BF16), w_ref[...])
    k_ref[...] = kv[:, :WIDTH].astype(BF16)
    v_ref[...] = kv[:, WIDTH:].astype(BF16)


def _memkv(mem, wkv):
    batch, m, _ = mem.shape
    return pl.pallas_call(
        _memkv_kernel,
        grid=(batch,),
        in_specs=[pl.BlockSpec((None, m, D_MODEL), lambda b: (b, 0, 0)), _const_spec(wkv.shape)],
        out_specs=[pl.BlockSpec((None, m, WIDTH), lambda b: (b, 0, 0))] * 2,
        out_shape=[jax.ShapeDtypeStruct((batch, m, WIDTH), BF16)] * 2,
        compiler_params=_cparams(("parallel",)),
        name="mem_kv",
    )(mem, wkv)


def _xattn_kernel(x_ref, wq_ref, k_ref, v_ref, wo_ref, g_ref, b_ref, o_ref):
    x = x_ref[...]
    q = _dot(x.astype(BF16), wq_ref[...]).astype(BF16)
    k = k_ref[...]
    v = v_ref[...]
    lane = lax.broadcasted_iota(I32, (1, WIDTH), 1) // HEAD_DIM
    o = jnp.zeros((x.shape[0], WIDTH), F32)
    for hh in range(N_HEADS):
        qh = jnp.where(lane == hh, q, jnp.zeros_like(q))
        s = _dot_nt(qh, k)
        e = jnp.exp(s - jnp.max(s, axis=-1, keepdims=True))
        p = e / jnp.sum(e, axis=-1, keepdims=True)
        o = o + jnp.where(lane == hh, _dot(p.astype(BF16), v), 0.0)
    y = _dot(o.astype(BF16), wo_ref[...])
    o_ref[...] = _layernorm(ALPHA * x + y, g_ref[...], b_ref[...])


def _xattn(x2d, wq, k, v, wo, g, b, batch, seq):
    tm = TM_A
    m = k.shape[1]
    x3 = x2d.reshape(batch, seq, D_MODEL)
    row = pl.BlockSpec((None, tm, D_MODEL), lambda bb, i: (bb, i, 0))
    kv_spec = pl.BlockSpec((None, m, WIDTH), lambda bb, i: (bb, 0, 0))
    out = pl.pallas_call(
        _xattn_kernel,
        grid=(batch, seq // tm),
        in_specs=[row, _const_spec(wq.shape), kv_spec, kv_spec, _const_spec(wo.shape),
                  _const_spec((1, D_MODEL)), _const_spec((1, D_MODEL))],
        out_specs=row,
        out_shape=jax.ShapeDtypeStruct((batch, seq, D_MODEL), F32),
        compiler_params=_cparams(("parallel", "parallel")),
        name="mem_xattn_ln",
    )(x3, wq, k, v, wo, g.reshape(1, -1), b.reshape(1, -1))
    return out.reshape(batch * seq, D_MODEL)


def _ffn_kernel(x_ref, w1_ref, w3_ref, w2_ref, g_ref, b_ref, o_ref, acc_ref):
    f = pl.program_id(1)
    xb = x_ref[...].astype(BF16)
    a = _dot(xb, w1_ref[...])
    gate = _dot(xb, w3_ref[...])
    part = _dot((a * jax.nn.sigmoid(a) * gate).astype(BF16), w2_ref[...])

    @pl.when(f == 0)
    def _():
        acc_ref[...] = part

    @pl.when(f > 0)
    def _():
        acc_ref[...] += part

    @pl.when(f == pl.num_programs(1) - 1)
    def _():
        o_ref[...] = _layernorm(ALPHA * x_ref[...] + acc_ref[...], g_ref[...], b_ref[...])


def _ffn(x2d, w13, w2, g, b):
    n = x2d.shape[0]
    tm, tf = TM_FFN, TF_FFN
    nf = F_DENSE // tf
    return pl.pallas_call(
        _ffn_kernel,
        grid=(n // tm, nf),
        in_specs=[pl.BlockSpec((tm, D_MODEL), lambda i, f: (i, 0)),
                  pl.BlockSpec((D_MODEL, tf), lambda i, f: (0, f)),
                  pl.BlockSpec((D_MODEL, tf), lambda i, f: (0, nf + f)),
                  pl.BlockSpec((tf, D_MODEL), lambda i, f: (f, 0)),
                  _const_spec((1, D_MODEL)), _const_spec((1, D_MODEL))],
        out_specs=pl.BlockSpec((tm, D_MODEL), lambda i, f: (i, 0)),
        out_shape=jax.ShapeDtypeStruct((n, D_MODEL), F32),
        scratch_shapes=[pltpu.VMEM((tm, D_MODEL), F32)],
        compiler_params=_cparams(("parallel", "arbitrary")),
        name="ffn_ln",
    )(x2d, w13, w13, w2, g.reshape(1, -1), b.reshape(1, -1))


def _router_kernel(x_ref, r_ref, info_ref, wts_ref, cnt_ref, carry_ref):
    tm = x_ref.shape[0]

    @pl.when(pl.program_id(0) == 0)
    def _():
        carry_ref[...] = jnp.zeros_like(carry_ref)

    logits = jnp.dot(x_ref[...], r_ref[...], precision=lax.Precision.HIGHEST, preferred_element_type=F32)
    lane = lax.broadcasted_iota(I32, (tm, LANES), 1)
    lg = jnp.where(lane < N_EXPERTS, logits, -jnp.inf)
    m1 = jnp.max(lg, axis=-1, keepdims=True)
    i1 = jnp.min(jnp.where(lg == m1, lane, LANES), axis=-1, keepdims=True)
    lg2 = jnp.where(lane == i1, -jnp.inf, lg)
    m2 = jnp.max(lg2, axis=-1, keepdims=True)
    i2 = jnp.min(jnp.where(lg2 == m2, lane, LANES), axis=-1, keepdims=True)
    e = jnp.exp(m2 - m1)
    w1 = 1.0 / (1.0 + e)
    w2 = e / (1.0 + e)
    sel1 = lane == i1
    sel2 = lane == i2
    chosen = jnp.where(sel1 | sel2, 1.0, 0.0)
    row = lax.broadcasted_iota(I32, (tm, tm), 0)
    col = lax.broadcasted_iota(I32, (tm, tm), 1)
    before = (col < row).astype(BF16)
    ranks = _dot(before, chosen.astype(BF16)) + carry_ref[...]
    r1 = jnp.sum(jnp.where(sel1, ranks, 0.0), axis=-1, keepdims=True).astype(I32)
    r2 = jnp.sum(jnp.where(sel2, ranks, 0.0), axis=-1, keepdims=True).astype(I32)
    carry_ref[...] = carry_ref[...] + jnp.sum(chosen, axis=0, keepdims=True)
    info_ref[...] = jnp.where(lane == 0, i1, jnp.where(lane == 1, i2, jnp.where(lane == 2, r1,
                              jnp.where(lane == 3, r2, 0))))
    wts_ref[...] = jnp.where(lane == 0, w1, jnp.where(lane == 1, w2, 0.0))
    cnt_ref[...] = carry_ref[...]


def _router(x2d, router):
    n = x2d.shape[0]
    tm = TM_A
    r_pad = jnp.zeros((D_MODEL, LANES), F32).at[:, :N_EXPERTS].set(router.astype(F32))
    row = pl.BlockSpec((tm, LANES), lambda i: (i, 0))
    return pl.pallas_call(
        _router_kernel,
        grid=(n // tm,),
        in_specs=[pl.BlockSpec((tm, D_MODEL), lambda i: (i, 0)), _const_spec(r_pad.shape)],
        out_specs=[row, row, _const_spec((1, LANES))],
        out_shape=[jax.ShapeDtypeStruct((n, LANES), I32), jax.ShapeDtypeStruct((n, LANES), F32),
                   jax.ShapeDtypeStruct((1, LANES), F32)],
        scratch_shapes=[pltpu.VMEM((1, LANES), F32)],
        compiler_params=_cparams(("arbitrary",)),
        name="moe_router",
    )(x2d, r_pad)


def _dispatch_kernel(nused_ref, tok_ref, x_hbm, o_ref, buf_ref, sem):
    tb = buf_ref.shape[0]

    @pl.when(pl.program_id(0) < nused_ref[0])
    def _():
        def issue(r, c):
            pltpu.make_async_copy(x_hbm.at[pl.ds(tok_ref[0, r], 1), :], buf_ref.at[pl.ds(r, 1), :], sem).start()
            return c
        lax.fori_loop(0, tb, issue, 0)
        pltpu.make_async_copy(x_hbm.at[pl.ds(0, tb), :], buf_ref, sem).wait()
        o_ref[...] = buf_ref[...].astype(o_ref.dtype)

    @pl.when(pl.program_id(0) >= nused_ref[0])
    def _():
        o_ref[...] = jnp.zeros_like(o_ref)


def _dispatch(x2d, slot_tok, nused, nblk):
    tb = MOE_TB
    grid_spec = pltpu.PrefetchScalarGridSpec(
        num_scalar_prefetch=1,
        grid=(nblk,),
        in_specs=[pl.BlockSpec((None, 1, tb), lambda i, nu: (i, 0, 0), memory_space=pltpu.SMEM),
                  pl.BlockSpec(memory_space=pl.ANY)],
        out_specs=pl.BlockSpec((tb, D_MODEL), lambda i, nu: (i, 0)),
        scratch_shapes=[pltpu.VMEM((tb, D_MODEL), F32), pltpu.SemaphoreType.DMA(())],
    )
    return pl.pallas_call(
        _dispatch_kernel,
        grid_spec=grid_spec,
        out_shape=jax.ShapeDtypeStruct((nblk * tb, D_MODEL), BF16),
        compiler_params=_cparams(("arbitrary",), disable_bounds_checks=True),
        name="moe_dispatch",
    )(nused, slot_tok.reshape(nblk, 1, tb), x2d)


def _expert_kernel(nused_ref, bexp_ref, x_ref, w1_ref, w3_ref, w2_ref, o_ref, acc_ref):
    f = pl.program_id(1)

    @pl.when(pl.program_id(0) < nused_ref[0])
    def _():
        xb = x_ref[...]
        a = _dot(xb, w1_ref[...])
        gate = _dot(xb, w3_ref[...])
        part = _dot((a * jax.nn.sigmoid(a) * gate).astype(BF16), w2_ref[...])

        @pl.when(f == 0)
        def _():
            acc_ref[...] = part

        @pl.when(f > 0)
        def _():
            acc_ref[...] += part

        @pl.when(f == pl.num_programs(1) - 1)
        def _():
            o_ref[...] = acc_ref[...]

    @pl.when(pl.program_id(0) >= nused_ref[0])
    def _():
        o_ref[...] = jnp.zeros_like(o_ref)


def _experts(xb, w13, w2, nused, blk_exp, nblk):
    tb, tf = MOE_TB, MOE_TF
    nf = F_EXPERT // tf

    def blk(i, nu):
        return jnp.minimum(i, nu[0] - 1)

    def ftile(i, f, nu):
        return jnp.where(i < nu[0], f, nf - 1)

    grid_spec = pltpu.PrefetchScalarGridSpec(
        num_scalar_prefetch=2,
        grid=(nblk, nf),
        in_specs=[pl.BlockSpec((tb, D_MODEL), lambda i, f, nu, be: (blk(i, nu), 0)),
                  pl.BlockSpec((None, D_MODEL, tf), lambda i, f, nu, be: (be[blk(i, nu)], 0, ftile(i, f, nu))),
                  pl.BlockSpec((None, D_MODEL, tf), lambda i, f, nu, be: (be[blk(i, nu)], 0, nf + ftile(i, f, nu))),
                  pl.BlockSpec((None, tf, D_MODEL), lambda i, f, nu, be: (be[blk(i, nu)], ftile(i, f, nu), 0))],
        out_specs=pl.BlockSpec((tb, D_MODEL), lambda i, f, nu, be: (i, 0)),
        scratch_shapes=[pltpu.VMEM((tb, D_MODEL), F32)],
    )
    return pl.pallas_call(
        _expert_kernel,
        grid_spec=grid_spec,
        out_shape=jax.ShapeDtypeStruct((nblk * tb, D_MODEL), F32),
        compiler_params=_cparams(("arbitrary", "arbitrary")),
        name="moe_experts",
    )(nused, blk_exp, xb, w13, w13, w2)


def _combine_kernel(dest_ref, y_hbm, x_ref, wts_ref, g_ref, b_ref, o_ref, buf_ref, sem):
    tm = x_ref.shape[0]

    def issue(r, c):
        for k in range(2):
            pltpu.make_async_copy(y_hbm.at[pl.ds(dest_ref[0, 2 * r + k], 1), :],
                                  buf_ref.at[k, pl.ds(r, 1), :], sem).start()
        return c
    lax.fori_loop(0, tm, issue, 0)
    for k in range(2):
        pltpu.make_async_copy(y_hbm.at[pl.ds(0, tm), :], buf_ref.at[k], sem).wait()
    wts = wts_ref[...]
    y = wts[:, 0:1] * buf_ref[0] + wts[:, 1:2] * buf_ref[1]
    o_ref[...] = _layernorm(ALPHA * x_ref[...] + y, g_ref[...], b_ref[...])


def _combine(yb, dest, x2d, wts, g, b):
    n = x2d.shape[0]
    tm = TM_COMB
    nt = n // tm
    row = lambda w: pl.BlockSpec((tm, w), lambda i: (i, 0))
    return pl.pallas_call(
        _combine_kernel,
        grid=(nt,),
        in_specs=[pl.BlockSpec((None, 1, 2 * tm), lambda i: (i, 0, 0), memory_space=pltpu.SMEM),
                  pl.BlockSpec(memory_space=pl.ANY), row(D_MODEL), row(LANES),
                  _const_spec((1, D_MODEL)), _const_spec((1, D_MODEL))],
        out_specs=row(D_MODEL),
        out_shape=jax.ShapeDtypeStruct((n, D_MODEL), F32),
        scratch_shapes=[pltpu.VMEM((2, tm, D_MODEL), F32), pltpu.SemaphoreType.DMA(())],
        compiler_params=_cparams(("arbitrary",), disable_bounds_checks=True),
        name="moe_combine_ln",
    )(dest.reshape(nt, 1, 2 * tm), yb, x2d, wts, g.reshape(1, -1), b.reshape(1, -1))


def _moe(x2d, router, w13, w2, g, b):
    n = x2d.shape[0]
    tb = MOE_TB
    info, wts, cnt = _router(x2d, router)
    idx = info[:, 0:2]
    rank = info[:, 2:4]
    counts = cnt[0, :N_EXPERTS].astype(I32)
    padded = (counts + tb - 1) // tb * tb
    pend = jnp.cumsum(padded)
    pstart = pend - padded
    dest = (pstart[idx] + rank).astype(I32)
    nblk = (2 * n) // tb + N_EXPERTS
    tok = jnp.repeat(jnp.arange(n, dtype=I32), 2)
    slot_tok = jnp.zeros((nblk * tb,), I32).at[dest.reshape(-1)].set(tok)
    nused = (pend[-1] // tb).astype(I32).reshape(1)
    blk_exp = jnp.minimum(jnp.searchsorted(pend, jnp.arange(nblk, dtype=I32) * tb, side='right'),
                          N_EXPERTS - 1).astype(I32)
    xb = _dispatch(x2d, slot_tok, nused, nblk)
    yb = _experts(xb, w13, w2, nused, blk_exp, nblk)
    return _combine(yb, dest.reshape(-1), x2d, wts, g, b)


def kernel(x, mem, positions, rel_bias_table, hgrn_lb_logits, w_in, mla_q_norm, mla_w_uq, mla_kv_norm, mla_w_ukv, swa_sinks, hgrn_norm, w_branch, w_out, ln_g, ln_b, xa_wq, xa_wkv, xa_wo, ffn_w13, ffn_w2, moe_router, moe_w13, moe_w2):
    batch, seq, _ = x.shape
    n = batch * seq
    sm = jax.nn.softmax(hgrn_lb_logits.astype(F32), axis=0)
    lower_bounds = jnp.cumsum(sm, axis=0) - sm[0]
    ctab, stab = _rope_tables(positions)
    xc = x.reshape(n, D_MODEL)
    for l in range(DEPTH):
        wts = _inproj_weights(w_in[l], mla_w_uq[l], mla_w_ukv[l])
        mq, mk, mv, swq, swk, swv, hg, sbq, sbk, sbv = _inproj(xc, wts, ctab, stab, mla_q_norm[l], mla_kv_norm[l])
        y_mla = _mla_attention(mq, mk, mv, batch, seq)
        y_swa = _swa_attention(swq, swk, swv, positions, swa_sinks[l], rel_bias_table, batch, seq)
        y_hg = _hgrn(hg, lower_bounds[l], hgrn_norm[l], batch, seq)
        y_sb = _sb_attention(sbq, sbk, sbv, batch, seq)
        go = _IN_OFF['gates']
        xc = _merge(xc, (y_mla, y_swa, y_hg, y_sb), w_in[l][:, go:].astype(BF16), w_branch[l].astype(BF16),
                    w_out[l].astype(BF16), ln_g[l, 0], ln_b[l, 0])
        mk_, mv_ = _memkv(mem, xa_wkv[l].astype(BF16))
        xc = _xattn(xc, (xa_wq[l] * QK_SCALE).astype(BF16), mk_, mv_, xa_wo[l].astype(BF16),
                    ln_g[l, 1], ln_b[l, 1], batch, seq)
        if l % 2 == 0:
            xc = _ffn(xc, ffn_w13[l // 2].astype(BF16), ffn_w2[l // 2].astype(BF16), ln_g[l, 2], ln_b[l, 2])
        else:
            xc = _moe(xc, moe_router[l // 2], moe_w13[l // 2].astype(BF16), moe_w2[l // 2].astype(BF16),
                      ln_g[l, 2], ln_b[l, 2])
    return xc.reshape(batch, seq, D_MODEL)
```

```python
import functools
import math

import jax
import jax.numpy as jnp
from jax import lax
from jax.experimental import pallas as pl
from jax.experimental.pallas import tpu as pltpu

F32 = jnp.float32
BF16 = jnp.bfloat16
I32 = jnp.int32

D_MODEL = 1024
DEPTH = 2
EPS = 1e-5
NEG_BIG = -1e30
LANES = 128
HEAD_DIM = 64
N_HEADS = 4
WIDTH = N_HEADS * HEAD_DIM

MLA_Q_LORA = 256
MLA_KV_LORA = 128
MLA_NOPE = 64
MLA_ROPE = 32
ROPE_THETA = 10000.0
MLA_SCALE = (MLA_NOPE + MLA_ROPE) ** -0.5
QK_SCALE = HEAD_DIM ** -0.5

SWA_WINDOW = 128
REL_BUCKETS = 32
REL_MAX_DIST = 128
HGRN_CHUNK = 64
N_EXPERTS = 8
F_DENSE = 2816
F_EXPERT = 3584
ALPHA = (2 * DEPTH) ** 0.25

_IN_SPLITS = (('mla_cq', 256), ('mla_ckv', 128), ('mla_kr', 32), ('swa_q', 256), ('swa_k', 128),
              ('swa_v', 128), ('hgrn', 1024), ('sb_q', 256), ('sb_k', 256), ('sb_v', 256), ('gates', 4096))
_IN_OFF = {}
_o = 0
for _n, _w in _IN_SPLITS:
    _IN_OFF[_n] = _o
    _o += _w

_A_SPLITS = (('cq', 256), ('ckv', 128), ('kra', 128), ('krb', 128), ('swa_q', 256), ('swa_k', 256),
             ('swa_v', 256), ('hgrn', 1024), ('sb_q', 256), ('sb_k', 256), ('sb_v', 256))
_A_OFF = {}
_o = 0
for _n, _w in _A_SPLITS:
    _A_OFF[_n] = (_o, _o + _w)
    _o += _w
A_COLS = _o

TM_A = 512
TQ_ATT = 256
HG_ROWS = 256
TM_FFN = 512
TF_FFN = 1408
MOE_TB = 512
MOE_TF = 512
TM_COMB = 256
VMEM_LIMIT = 56 * 1024 * 1024


def _cparams(sem, **kw):
    return pltpu.CompilerParams(dimension_semantics=sem, vmem_limit_bytes=VMEM_LIMIT, **kw)


def _const_spec(shape):
    nd = len(shape)
    return pl.BlockSpec(shape, lambda *_: (0,) * nd)


def _layernorm(v, g, b):
    mu = jnp.mean(v, axis=-1, keepdims=True)
    vc = v - mu
    var = jnp.mean(vc * vc, axis=-1, keepdims=True)
    return vc * lax.rsqrt(var + EPS) * g + b


def _dot(a, b):
    return jnp.dot(a, b, preferred_element_type=F32)


def _dot_nt(a, b):
    return lax.dot_general(a, b, (((1,), (1,)), ((), ())), preferred_element_type=F32)


def _split3(a):
    hi = a.astype(BF16)
    r = a - hi.astype(F32)
    mid = r.astype(BF16)
    lo = (r - mid.astype(F32)).astype(BF16)
    return hi, mid, lo


def _rope_kernel(pos_ref, freq_ref, c_ref, s_ref):
    lane = lax.broadcasted_iota(I32, pos_ref.shape, 1)
    ang = pos_ref[...] * freq_ref[...]
    rope = (lane >= MLA_NOPE) & (lane < MLA_NOPE + MLA_ROPE)
    first = lane < MLA_NOPE + MLA_ROPE // 2
    c_ref[...] = jnp.where(lane < MLA_NOPE, 1.0, jnp.where(rope, jnp.cos(ang), 0.0))
    sn = jnp.sin(ang)
    s_ref[...] = jnp.where(rope, jnp.where(first, -sn, sn), 0.0)


def _rope_tables(positions):
    n = positions.size
    half = MLA_ROPE // 2
    inv_freq = ROPE_THETA ** (-jnp.arange(half, dtype=F32) / half)
    freq = jnp.zeros((1, LANES), F32).at[0, MLA_NOPE:MLA_NOPE + MLA_ROPE].set(jnp.tile(inv_freq, 2))
    posb = jnp.broadcast_to(positions.reshape(n, 1).astype(F32), (n, LANES))
    tm = 1024
    return pl.pallas_call(
        _rope_kernel,
        grid=(n // tm,),
        in_specs=[pl.BlockSpec((tm, LANES), lambda i: (i, 0)), _const_spec((1, LANES))],
        out_specs=[pl.BlockSpec((tm, LANES), lambda i: (i, 0))] * 2,
        out_shape=[jax.ShapeDtypeStruct((n, LANES), F32)] * 2,
        compiler_params=_cparams(("parallel",)),
        name="rope_tables",
    )(posb, freq)


def _inproj_kernel(x_ref, w_ref, c_ref, s_ref, qn_ref, kvn_ref, wuqa_ref, wuqb_ref, wuk_ref, wuv_ref,
                   mq_ref, mk_ref, mv_ref, swq_ref, swk_ref, swv_ref, hg_ref, sbq_ref, sbk_ref, sbv_ref):
    h = _dot(x_ref[...].astype(BF16), w_ref[...])

    def cols(name):
        lo, hi = _A_OFF[name]
        return h[:, lo:hi]

    c = c_ref[...]
    s = s_ref[...]
    c4 = jnp.concatenate([c] * N_HEADS, axis=1)
    s4 = jnp.concatenate([s] * N_HEADS, axis=1)

    cq = cols('cq')
    cqn = (cq * lax.rsqrt(jnp.mean(cq * cq, axis=-1, keepdims=True) + EPS) * qn_ref[...]).astype(BF16)
    q = _dot(cqn, wuqa_ref[...]) * c4 + _dot(cqn, wuqb_ref[...]) * s4
    mq_ref[...] = (q * MLA_SCALE).astype(BF16)

    ckv = cols('ckv')
    ckvn = (ckv * lax.rsqrt(jnp.mean(ckv * ckv, axis=-1, keepdims=True) + EPS) * kvn_ref[...]).astype(BF16)
    krot = cols('kra') * c + cols('krb') * s
    mk_ref[...] = (_dot(ckvn, wuk_ref[...]) + jnp.concatenate([krot] * N_HEADS, axis=1)).astype(BF16)
    mv_ref[...] = _dot(ckvn, wuv_ref[...]).astype(BF16)

    swq_ref[...] = cols('swa_q').astype(BF16)
    swk_ref[...] = cols('swa_k').astype(BF16)
    swv_ref[...] = cols('swa_v').astype(BF16)
    hg_ref[...] = cols('hgrn')
    sbq_ref[...] = cols('sb_q').astype(BF16)
    sbk_ref[...] = cols('sb_k').astype(BF16)
    sbv_ref[...] = cols('sb_v').astype(BF16)


def _inproj_weights(w_in, w_uq, w_ukv):
    def seg(name, width):
        o = _IN_OFF[name]
        return w_in[:, o:o + width]

    kr = seg('mla_kr', MLA_ROPE)
    half = MLA_ROPE // 2
    z64 = jnp.zeros((D_MODEL, MLA_NOPE), F32)
    z32 = jnp.zeros((D_MODEL, LANES - MLA_NOPE - MLA_ROPE), F32)
    kra = jnp.concatenate([z64, kr, z32], axis=1)
    krb = jnp.concatenate([z64, kr[:, half:], kr[:, :half], z32], axis=1)
    swk = seg('swa_k', 128)
    swv = seg('swa_v', 128)
    dup = lambda t: jnp.concatenate([t[:, :64], t[:, :64], t[:, 64:], t[:, 64:]], axis=1)
    w_a = jnp.concatenate([
        seg('mla_cq', 256), seg('mla_ckv', 128), kra, krb,
        seg('swa_q', 256) * QK_SCALE, dup(swk), dup(swv),
        seg('hgrn', 1024), seg('sb_q', 256) * QK_SCALE, seg('sb_k', 256), seg('sb_v', 256)], axis=1)

    qd = MLA_NOPE + MLA_ROPE
    zq = jnp.zeros((MLA_Q_LORA, LANES - qd), F32)
    zn = jnp.zeros((MLA_Q_LORA, MLA_NOPE), F32)
    qa, qb = [], []
    for hh in range(N_HEADS):
        nope = w_uq[:, hh * qd: hh * qd + MLA_NOPE]
        rope = w_uq[:, hh * qd + MLA_NOPE: (hh + 1) * qd]
        qa += [nope, rope, zq]
        qb += [zn, rope[:, half:], rope[:, :half], zq]
    wuqa = jnp.concatenate(qa, axis=1)
    wuqb = jnp.concatenate(qb, axis=1)
    lane = jnp.arange(N_HEADS * LANES) % LANES
    wuk = jnp.where(lane[None, :] < MLA_NOPE, w_ukv, 0.0)
    wuv = jnp.concatenate([w_ukv[:, hh * LANES + MLA_NOPE:(hh + 1) * LANES] for hh in range(N_HEADS)], axis=1)
    return tuple(t.astype(BF16) for t in (w_a, wuqa, wuqb, wuk, wuv))


def _inproj(x2d, wts, ctab, stab, q_norm, kv_norm):
    n = x2d.shape[0]
    w_a, wuqa, wuqb, wuk, wuv = wts
    tm = TM_A
    row = lambda w: pl.BlockSpec((tm, w), lambda i: (i, 0))
    out_w = (512, 512, 256, 256, 256, 256, 1024, 256, 256, 256)
    out_dt = (BF16, BF16, BF16, BF16, BF16, BF16, F32, BF16, BF16, BF16)
    return pl.pallas_call(
        _inproj_kernel,
        grid=(n // tm,),
        in_specs=[row(D_MODEL), _const_spec(w_a.shape), row(LANES), row(LANES),
                  _const_spec((1, MLA_Q_LORA)), _const_spec((1, MLA_KV_LORA)),
                  _const_spec(wuqa.shape), _const_spec(wuqb.shape), _const_spec(wuk.shape),
                  _const_spec(wuv.shape)],
        out_specs=[row(w) for w in out_w],
        out_shape=[jax.ShapeDtypeStruct((n, w), d) for w, d in zip(out_w, out_dt)],
        compiler_params=_cparams(("parallel",)),
        name="inproj",
    )(x2d, w_a, ctab, stab, q_norm.reshape(1, -1), kv_norm.reshape(1, -1), wuqa, wuqb, wuk, wuv)


def _half_mask(half):
    lane = lax.broadcasted_iota(I32, (1, LANES), 1)
    return (lane < HEAD_DIM) if half == 0 else (lane >= HEAD_DIM)


def _mla_kernel(q_ref, k_ref, v_ref, o_ref):
    tq = q_ref.shape[0]
    i = pl.program_id(1)
    row = lax.broadcasted_iota(I32, (tq, tq), 0)
    col = lax.broadcasted_iota(I32, (tq, tq), 1)
    causal = col <= row
    ones = jnp.ones((1, LANES), BF16)

    def step(j, carry, masked):
        off = pl.multiple_of(j * tq, tq)
        heads = range(N_HEADS)
        ss = [_dot_nt(q_ref[:, hh * LANES:(hh + 1) * LANES], k_ref[pl.ds(off, tq), hh * LANES:(hh + 1) * LANES])
              for hh in heads]
        if masked:
            ss = [jnp.where(causal, s, NEG_BIG) for s in ss]
        ms = [jnp.maximum(carry[hh][0], jnp.max(ss[hh], axis=-1, keepdims=True)) for hh in heads]
        pms = [jnp.exp(ss[hh] - ms[hh]).astype(BF16) for hh in heads]
        new = []
        for hh in heads:
            vb = v_ref[pl.ds(off, tq), (hh // 2) * LANES:(hh // 2 + 1) * LANES]
            vb = jnp.where(_half_mask(hh % 2), vb, ones)
            m, acc = carry[hh]
            new.append((ms[hh], jnp.exp(m - ms[hh]) * acc + _dot(pms[hh], vb)))
        return tuple(new)

    init = tuple((jnp.full((tq, 1), NEG_BIG, F32), jnp.zeros((tq, LANES), F32)) for _ in range(N_HEADS))
    carry = lax.fori_loop(0, i, functools.partial(step, masked=False), init)
    carry = step(i, carry, True)
    outs = []
    for p in range(N_HEADS // 2):
        a0 = carry[2 * p][1]
        a1 = carry[2 * p + 1][1]
        outs.append(jnp.where(_half_mask(0), a0 / a0[:, HEAD_DIM:HEAD_DIM + 1], a1 / a1[:, 0:1]))
    o_ref[...] = jnp.concatenate(outs, axis=1).astype(o_ref.dtype)


def _mla_attention(q, k, v, batch, seq):
    tq = TQ_ATT
    q3, k3, v3 = (t.reshape(batch, seq, t.shape[-1]) for t in (q, k, v))
    out = pl.pallas_call(
        _mla_kernel,
        grid=(batch, seq // tq),
        in_specs=[pl.BlockSpec((None, tq, 512), lambda b, i: (b, i, 0)),
                  pl.BlockSpec((None, seq, 512), lambda b, i: (b, 0, 0)),
                  pl.BlockSpec((None, seq, WIDTH), lambda b, i: (b, 0, 0))],
        out_specs=pl.BlockSpec((None, tq, WIDTH), lambda b, i: (b, i, 0)),
        out_shape=jax.ShapeDtypeStruct((batch, seq, WIDTH), BF16),
        compiler_params=_cparams(("parallel", "arbitrary")),
        name="mla_attention",
    )(q3, k3, v3)
    return out.reshape(batch * seq, WIDTH)


def _sb_kernel(q_ref, k_ref, v_ref, o_ref):
    tq = q_ref.shape[0]
    i = pl.program_id(1)
    row = lax.broadcasted_iota(I32, (tq, tq), 0)
    col = lax.broadcasted_iota(I32, (tq, tq), 1)
    strict = col < row
    later = (row > col).astype(BF16)
    qs = []
    for hh in range(N_HEADS):
        qp = q_ref[:, (hh // 2) * LANES:(hh // 2 + 1) * LANES]
        qs.append(jnp.where(_half_mask(hh % 2), qp, jnp.zeros_like(qp)))

    def block(j, carry, diag):
        off = pl.multiple_of(j * tq, tq)
        runs, accs = carry
        heads = range(N_HEADS)
        zs = [_dot_nt(qs[hh], k_ref[pl.ds(off, tq), (hh // 2) * LANES:(hh // 2 + 1) * LANES]) for hh in heads]
        lsps = [jnp.minimum(z, 0.0) - jnp.log(1.0 + jnp.exp(-jnp.abs(z))) for z in zs]
        lsns = [lsp - z for lsp, z in zip(lsps, zs)]
        if diag:
            lsns = [jnp.where(strict, t, 0.0) for t in lsns]
        his = [t.astype(BF16) for t in lsns]
        los = [(t - hi.astype(F32)).astype(BF16) for t, hi in zip(lsns, his)]
        rems = [_dot(hi, later) + _dot(lo, later) for hi, lo in zip(his, los)]
        args = [lsps[hh] + rems[hh] + runs[hh] for hh in heads]
        if diag:
            args = [jnp.where(strict, t, NEG_BIG) for t in args]
        probs = [jnp.exp(t).astype(BF16) for t in args]
        new_runs = tuple(runs[hh] + rems[hh][:, 0:1] + lsns[hh][:, 0:1] for hh in heads)
        new_accs = list(accs)
        for hh in heads:
            p = hh // 2
            vb = v_ref[pl.ds(off, tq), p * LANES:(p + 1) * LANES]
            vb = jnp.where(_half_mask(hh % 2), vb, jnp.zeros_like(vb))
            new_accs[p] = new_accs[p] + _dot(probs[hh], vb)
        return new_runs, tuple(new_accs)

    init = (tuple(jnp.zeros((tq, 1), F32) for _ in range(N_HEADS)),
            tuple(jnp.zeros((tq, LANES), F32) for _ in range(N_HEADS // 2)))
    carry = block(i, init, True)
    _, accs = lax.fori_loop(0, i, lambda jj, c: block(i - 1 - jj, c, False), carry)
    o_ref[...] = jnp.concatenate(accs, axis=1).astype(o_ref.dtype)


def _sb_attention(q, k, v, batch, seq):
    tq = TQ_ATT
    q3, k3, v3 = (t.reshape(batch, seq, WIDTH) for t in (q, k, v))
    out = pl.pallas_call(
        _sb_kernel,
        grid=(batch, seq // tq),
        in_specs=[pl.BlockSpec((None, tq, WIDTH), lambda b, i: (b, i, 0)),
                  pl.BlockSpec((None, seq, WIDTH), lambda b, i: (b, 0, 0)),
                  pl.BlockSpec((None, seq, WIDTH), lambda b, i: (b, 0, 0))],
        out_specs=pl.BlockSpec((None, tq, WIDTH), lambda b, i: (b, i, 0)),
        out_shape=jax.ShapeDtypeStruct((batch, seq, WIDTH), BF16),
        compiler_params=_cparams(("parallel", "arbitrary")),
        name="stick_breaking",
    )(q3, k3, v3)
    return out.reshape(batch * seq, WIDTH)


def _rel_bucket(dist):
    exact = REL_BUCKETS // 2
    n = jnp.maximum(dist, 0)
    nf = jnp.maximum(n, 1).astype(F32)
    large = exact + (jnp.log(nf / exact) / math.log(REL_MAX_DIST / exact) * (REL_BUCKETS - exact)).astype(I32)
    large = jnp.clip(large, 0, REL_BUCKETS - 1)
    return jnp.where(n < exact, n, large)


def _swa_kernel(tab_ref, sink_ref, q_ref, kc_ref, kp_ref, vc_ref, vp_ref, pq_ref, pkc_ref, pkp_ref, o_ref):
    w = SWA_WINDOW
    nblk = pl.program_id(1)
    row = lax.broadcasted_iota(I32, (w, w), 0)
    col = lax.broadcasted_iota(I32, (w, w), 1)
    valid_c = col <= row
    valid_p = (col > row) & (nblk > 0)
    pq = pq_ref[...]
    bucket_c = _rel_bucket(pq - pkc_ref[...])
    bucket_p = _rel_bucket(pq - pkp_ref[...])
    bias_c = [jnp.zeros((w, w), F32) for _ in range(N_HEADS)]
    bias_p = [jnp.zeros((w, w), F32) for _ in range(N_HEADS)]
    for b in range(REL_BUCKETS):
        mc = bucket_c == b
        mp = bucket_p == b
        for hh in range(N_HEADS):
            t = tab_ref[b, hh]
            bias_c[hh] = jnp.where(mc, t, bias_c[hh])
            bias_p[hh] = jnp.where(mp, t, bias_p[hh])
    outs = []
    for p in range(N_HEADS // 2):
        sl = slice(p * LANES, (p + 1) * LANES)
        qp = q_ref[:, sl]
        kc, kp, vc, vp = kc_ref[:, sl], kp_ref[:, sl], vc_ref[:, sl], vp_ref[:, sl]
        halves = []
        for half in range(2):
            hh = 2 * p + half
            qh = jnp.where(_half_mask(half), qp, jnp.zeros_like(qp))
            lc = jnp.where(valid_c, _dot_nt(qh, kc) + bias_c[hh], NEG_BIG)
            lp = jnp.where(valid_p, _dot_nt(qh, kp) + bias_p[hh], NEG_BIG)
            sink = sink_ref[hh]
            m = jnp.maximum(jnp.maximum(jnp.max(lc, axis=-1, keepdims=True),
                                        jnp.max(lp, axis=-1, keepdims=True)), sink)
            ec = jnp.exp(lc - m)
            ep = jnp.exp(lp - m)
            den = jnp.sum(ec, axis=-1, keepdims=True) + jnp.sum(ep, axis=-1, keepdims=True) + jnp.exp(sink - m)
            halves.append(_dot((ec / den).astype(BF16), vc) + _dot((ep / den).astype(BF16), vp))
        outs.append(jnp.where(_half_mask(0), halves[0], halves[1]))
    o_ref[...] = jnp.concatenate(outs, axis=1).astype(o_ref.dtype)


def _swa_attention(q, k, v, positions, sinks, rel_table, batch, seq):
    w = SWA_WINDOW
    nb = seq // w
    q3, k3, v3 = (t.reshape(batch, seq, WIDTH) for t in (q, k, v))
    pcol = positions.reshape(batch, seq, 1)
    prow = positions.reshape(batch, 1, seq)
    cur = lambda b, n: (b, n, 0)
    prev = lambda b, n: (b, jnp.maximum(n - 1, 0), 0)
    smem = pl.BlockSpec(memory_space=pltpu.SMEM)
    out = pl.pallas_call(
        _swa_kernel,
        grid=(batch, nb),
        in_specs=[smem, smem,
                  pl.BlockSpec((None, w, WIDTH), cur),
                  pl.BlockSpec((None, w, WIDTH), cur), pl.BlockSpec((None, w, WIDTH), prev),
                  pl.BlockSpec((None, w, WIDTH), cur), pl.BlockSpec((None, w, WIDTH), prev),
                  pl.BlockSpec((None, w, 1), cur),
                  pl.BlockSpec((None, 1, w), lambda b, n: (b, 0, n)),
                  pl.BlockSpec((None, 1, w), lambda b, n: (b, 0, jnp.maximum(n - 1, 0)))],
        out_specs=pl.BlockSpec((None, w, WIDTH), cur),
        out_shape=jax.ShapeDtypeStruct((batch, seq, WIDTH), BF16),
        compiler_params=_cparams(("parallel", "arbitrary")),
        name="swa_attention",
    )(rel_table.astype(F32), sinks.astype(F32), q3, k3, k3, v3, v3, pcol, prow, prow)
    return out.reshape(batch * seq, WIDTH)


def _hgrn_kernel(hg_ref, lb_ref, nw_ref, o_ref, state_ref, w_ref):
    c = HGRN_CHUNK
    @pl.when(pl.program_id(1) == 0)
    def _():
        state_ref[...] = jnp.zeros_like(state_ref)

    r64 = lax.broadcasted_iota(I32, (c, c), 0)
    c64 = lax.broadcasted_iota(I32, (c, c), 1)
    incl = (c64 <= r64).astype(BF16)
    ra = lax.broadcasted_iota(I32, (WIDTH, WIDTH), 0) // HEAD_DIM
    ca = lax.broadcasted_iota(I32, (WIDTH, WIDTH), 1) // HEAD_DIM
    same_head = ra == ca
    seg = same_head.astype(BF16)
    ones_cols = jnp.ones((c, LANES), BF16)
    trow = lax.broadcasted_iota(I32, (c, WIDTH), 0)
    lb = lb_ref[...]
    nw = nw_ref[...]
    group = 8

    for ch in range(hg_ref.shape[0] // c):
        rows = slice(ch * c, (ch + 1) * c)
        qraw = hg_ref[rows, 0:WIDTH]
        fraw = hg_ref[rows, WIDTH:2 * WIDTH]
        v = hg_ref[rows, 2 * WIDTH:3 * WIDTH]
        graw = hg_ref[rows, 3 * WIDTH:4 * WIDTH]
        qf = qraw * jax.nn.sigmoid(qraw)
        forget = lb + (1.0 - lb) * jax.nn.sigmoid(fraw)
        lf = jnp.log(forget)
        kk = 1.0 - forget
        gate = graw * jax.nn.sigmoid(graw)

        lf3 = _split3(lf)
        bc = _dot(incl, lf3[0]) + _dot(incl, lf3[1]) + _dot(incl, lf3[2])
        b_last = bc[c - 1:c, :]
        dn0 = (((0,), (0,)), ((), ()))
        tot_col = sum(lax.dot_general(t, ones_cols, dn0, preferred_element_type=F32) for t in lf3)
        decay_col = jnp.exp(jnp.concatenate([tot_col, tot_col], axis=1))

        state = state_ref[...]
        o = _dot((qf * jnp.exp(bc)).astype(BF16), state.astype(BF16))

        vb = v
        for g0 in range(0, c, group):
            for s_i in range(g0, g0 + group):
                e = jnp.exp(jnp.minimum(bc - bc[s_i:s_i + 1, :], 0.0))
                wgt = jnp.where(trow >= s_i, qf * kk[s_i:s_i + 1, :] * e, 0.0)
                w_ref[(s_i - g0) * c:(s_i - g0 + 1) * c, :] = wgt.astype(BF16)
            att = _dot(w_ref[...], seg)
            for s_i in range(g0, g0 + group):
                o = o + att[(s_i - g0) * c:(s_i - g0 + 1) * c, :] * vb[s_i:s_i + 1, :]

        khat = (kk * jnp.exp(b_last - bc)).astype(BF16)
        upd = lax.dot_general(khat, v.astype(BF16), dn0, preferred_element_type=F32)
        state_ref[...] = decay_col * state + jnp.where(same_head, upd, 0.0)

        o2 = _split3(o * o)
        ms = (_dot(o2[0], seg) + _dot(o2[1], seg)) * (1.0 / HEAD_DIM)
        o_ref[rows, :] = (o * lax.rsqrt(ms + EPS) * nw * gate).astype(o_ref.dtype)


def _hgrn(hg, lower_bound, norm_w, batch, seq):
    rows = HG_ROWS
    hg3 = hg.reshape(batch, seq, 4 * WIDTH)
    out = pl.pallas_call(
        _hgrn_kernel,
        grid=(batch, seq // rows),
        in_specs=[pl.BlockSpec((None, rows, 4 * WIDTH), lambda b, i: (b, i, 0)),
                  _const_spec((1, WIDTH)), _const_spec((1, WIDTH))],
        out_specs=pl.BlockSpec((None, rows, WIDTH), lambda b, i: (b, i, 0)),
        out_shape=jax.ShapeDtypeStruct((batch, seq, WIDTH), BF16),
        scratch_shapes=[pltpu.VMEM((WIDTH, WIDTH), F32), pltpu.VMEM((8 * HGRN_CHUNK, WIDTH), BF16)],
        compiler_params=_cparams(("parallel", "arbitrary")),
        name="hgrn2",
    )(hg3, lower_bound.reshape(1, WIDTH).astype(F32), norm_w.reshape(1, WIDTH).astype(F32))
    return out.reshape(batch * seq, WIDTH)


def _merge_kernel(x_ref, y0_ref, y1_ref, y2_ref, y3_ref, wg_ref, wb_ref, wo_ref, g_ref, b_ref, o_ref):
    x = x_ref[...]
    xb = x.astype(BF16)
    merged = jnp.zeros(x.shape, F32)
    for nbr, y_ref in enumerate((y0_ref, y1_ref, y2_ref, y3_ref)):
        gate = jax.nn.sigmoid(_dot(xb, wg_ref[:, nbr * D_MODEL:(nbr + 1) * D_MODEL]))
        merged = merged + gate * _dot(y_ref[...], wb_ref[nbr])
    y = _dot(merged.astype(BF16), wo_ref[...])
    o_ref[...] = _layernorm(ALPHA * x + y, g_ref[...], b_ref[...])


def _merge(x2d, ys, wg, wb, wo, g, b):
    n = x2d.shape[0]
    tm = TM_A
    row = lambda w: pl.BlockSpec((tm, w), lambda i: (i, 0))
    return pl.pallas_call(
        _merge_kernel,
        grid=(n // tm,),
        in_specs=[row(D_MODEL)] + [row(WIDTH)] * 4 +
                 [_const_spec(wg.shape), _const_spec(wb.shape), _const_spec(wo.shape),
                  _const_spec((1, D_MODEL)), _const_spec((1, D_MODEL))],
        out_specs=row(D_MODEL),
        out_shape=jax.ShapeDtypeStruct((n, D_MODEL), F32),
        compiler_params=_cparams(("parallel",)),
        name="merge_outproj_ln",
    )(x2d, *ys, wg, wb, wo, g.reshape(1, -1), b.reshape(1, -1))


def _memkv_kernel(m_ref, w_ref, k_ref, v_ref):
    kv = _dot(m_ref[...].astype(BF16), w_ref[...])
    k_ref[...] = kv[:, :WIDTH].astype(BF16)
    v_ref[...] = kv[:, WIDTH:].astype(BF16)


def _memkv(mem, wkv):
    batch, m, _ = mem.shape
    return pl.pallas_call(
        _memkv_kernel,
        grid=(batch,),
        in_specs=[pl.BlockSpec((None, m, D_MODEL), lambda b: (b, 0, 0)), _const_spec(wkv.shape)],
        out_specs=[pl.BlockSpec((None, m, WIDTH), lambda b: (b, 0, 0))] * 2,
        out_shape=[jax.ShapeDtypeStruct((batch, m, WIDTH), BF16)] * 2,
        compiler_params=_cparams(("parallel",)),
        name="mem_kv",
    )(mem, wkv)


def _xattn_kernel(x_ref, wq_ref, k_ref, v_ref, wo_ref, g_ref, b_ref, o_ref):
    x = x_ref[...]
    q = _dot(x.astype(BF16), wq_ref[...]).astype(BF16)
    k = k_ref[...]
    v = v_ref[...]
    lane = lax.broadcasted_iota(I32, (1, WIDTH), 1) // HEAD_DIM
    o = jnp.zeros((x.shape[0], WIDTH), F32)
    for hh in range(N_HEADS):
        qh = jnp.where(lane == hh, q, jnp.zeros_like(q))
        s = _dot_nt(qh, k)
        e = jnp.exp(s - jnp.max(s, axis=-1, keepdims=True))
        p = e / jnp.sum(e, axis=-1, keepdims=True)
        o = o + jnp.where(lane == hh, _dot(p.astype(BF16), v), 0.0)
    y = _dot(o.astype(BF16), wo_ref[...])
    o_ref[...] = _layernorm(ALPHA * x + y, g_ref[...], b_ref[...])


def _xattn(x2d, wq, k, v, wo, g, b, batch, seq):
    tm = TM_A
    m = k.shape[1]
    x3 = x2d.reshape(batch, seq, D_MODEL)
    row = pl.BlockSpec((None, tm, D_MODEL), lambda bb, i: (bb, i, 0))
    kv_spec = pl.BlockSpec((None, m, WIDTH), lambda bb, i: (bb, 0, 0))
    out = pl.pallas_call(
        _xattn_kernel,
        grid=(batch, seq // tm),
        in_specs=[row, _const_spec(wq.shape), kv_spec, kv_spec, _const_spec(wo.shape),
                  _const_spec((1, D_MODEL)), _const_spec((1, D_MODEL))],
        out_specs=row,
        out_shape=jax.ShapeDtypeStruct((batch, seq, D_MODEL), F32),
        compiler_params=_cparams(("parallel", "parallel")),
        name="mem_xattn_ln",
    )(x3, wq, k, v, wo, g.reshape(1, -1), b.reshape(1, -1))
    return out.reshape(batch * seq, D_MODEL)


def _ffn_kernel(x_ref, w1_ref, w3_ref, w2_ref, g_ref, b_ref, o_ref, acc_ref):
    f = pl.program_id(1)
    xb = x_ref[...].astype(BF16)
    a = _dot(xb, w1_ref[...])
    gate = _dot(xb, w3_ref[...])
    part = _dot((a * jax.nn.sigmoid(a) * gate).astype(BF16), w2_ref[...])

    @pl.when(f == 0)
    def _():
        acc_ref[...] = part

    @pl.when(f > 0)
    def _():
        acc_ref[...] += part

    @pl.when(f == pl.num_programs(1) - 1)
    def _():
        o_ref[...] = _layernorm(ALPHA * x_ref[...] + acc_ref[...], g_ref[...], b_ref[...])


def _ffn(x2d, w13, w2, g, b):
    n = x2d.shape[0]
    tm, tf = TM_FFN, TF_FFN
    nf = F_DENSE // tf
    return pl.pallas_call(
        _ffn_kernel,
        grid=(n // tm, nf),
        in_specs=[pl.BlockSpec((tm, D_MODEL), lambda i, f: (i, 0)),
                  pl.BlockSpec((D_MODEL, tf), lambda i, f: (0, f)),
                  pl.BlockSpec((D_MODEL, tf), lambda i, f: (0, nf + f)),
                  pl.BlockSpec((tf, D_MODEL), lambda i, f: (f, 0)),
                  _const_spec((1, D_MODEL)), _const_spec((1, D_MODEL))],
        out_specs=pl.BlockSpec((tm, D_MODEL), lambda i, f: (i, 0)),
        out_shape=jax.ShapeDtypeStruct((n, D_MODEL), F32),
        scratch_shapes=[pltpu.VMEM((tm, D_MODEL), F32)],
        compiler_params=_cparams(("parallel", "arbitrary")),
        name="ffn_ln",
    )(x2d, w13, w13, w2, g.reshape(1, -1), b.reshape(1, -1))


def _router_kernel(x_ref, r_ref, info_ref, wts_ref, cnt_ref, carry_ref):
    tm = x_ref.shape[0]

    @pl.when(pl.program_id(0) == 0)
    def _():
        carry_ref[...] = jnp.zeros_like(carry_ref)

    logits = jnp.dot(x_ref[...], r_ref[...], precision=lax.Precision.HIGHEST, preferred_element_type=F32)
    lane = lax.broadcasted_iota(I32, (tm, LANES), 1)
    lg = jnp.where(lane < N_EXPERTS, logits, -jnp.inf)
    m1 = jnp.max(lg, axis=-1, keepdims=True)
    i1 = jnp.min(jnp.where(lg == m1, lane, LANES), axis=-1, keepdims=True)
    lg2 = jnp.where(lane == i1, -jnp.inf, lg)
    m2 = jnp.max(lg2, axis=-1, keepdims=True)
    i2 = jnp.min(jnp.where(lg2 == m2, lane, LANES), axis=-1, keepdims=True)
    e = jnp.exp(m2 - m1)
    w1 = 1.0 / (1.0 + e)
    w2 = e / (1.0 + e)
    sel1 = lane == i1
    sel2 = lane == i2
    chosen = jnp.where(sel1 | sel2, 1.0, 0.0)
    row = lax.broadcasted_iota(I32, (tm, tm), 0)
    col = lax.broadcasted_iota(I32, (tm, tm), 1)
    before = (col < row).astype(BF16)
    ranks = _dot(before, chosen.astype(BF16)) + carry_ref[...]
    r1 = jnp.sum(jnp.where(sel1, ranks, 0.0), axis=-1, keepdims=True).astype(I32)
    r2 = jnp.sum(jnp.where(sel2, ranks, 0.0), axis=-1, keepdims=True).astype(I32)
    carry_ref[...] = carry_ref[...] + jnp.sum(chosen, axis=0, keepdims=True)
    info_ref[...] = jnp.where(lane == 0, i1, jnp.where(lane == 1, i2, jnp.where(lane == 2, r1,
                              jnp.where(lane == 3, r2, 0))))
    wts_ref[...] = jnp.where(lane == 0, w1, jnp.where(lane == 1, w2, 0.0))
    cnt_ref[...] = carry_ref[...]


def _router(x2d, router):
    n = x2d.shape[0]
    tm = TM_A
    r_pad = jnp.zeros((D_MODEL, LANES), F32).at[:, :N_EXPERTS].set(router.astype(F32))
    row = pl.BlockSpec((tm, LANES), lambda i: (i, 0))
    return pl.pallas_call(
        _router_kernel,
        grid=(n // tm,),
        in_specs=[pl.BlockSpec((tm, D_MODEL), lambda i: (i, 0)), _const_spec(r_pad.shape)],
        out_specs=[row, row, _const_spec((1, LANES))],
        out_shape=[jax.ShapeDtypeStruct((n, LANES), I32), jax.ShapeDtypeStruct((n, LANES), F32),
                   jax.ShapeDtypeStruct((1, LANES), F32)],
        scratch_shapes=[pltpu.VMEM((1, LANES), F32)],
        compiler_params=_cparams(("arbitrary",)),
        name="moe_router",
    )(x2d, r_pad)


def _dispatch_kernel(nused_ref, tok_ref, x_hbm, o_ref, buf_ref, sem):
    tb = buf_ref.shape[0]

    @pl.when(pl.program_id(0) < nused_ref[0])
    def _():
        def issue(r, c):
            pltpu.make_async_copy(x_hbm.at[pl.ds(tok_ref[0, r], 1), :], buf_ref.at[pl.ds(r, 1), :], sem).start()
            return c
        lax.fori_loop(0, tb, issue, 0)
        pltpu.make_async_copy(x_hbm.at[pl.ds(0, tb), :], buf_ref, sem).wait()
        o_ref[...] = buf_ref[...].astype(o_ref.dtype)

    @pl.when(pl.program_id(0) >= nused_ref[0])
    def _():
        o_ref[...] = jnp.zeros_like(o_ref)


def _dispatch(x2d, slot_tok, nused, nblk):
    tb = MOE_TB
    grid_spec = pltpu.PrefetchScalarGridSpec(
        num_scalar_prefetch=1,
        grid=(nblk,),
        in_specs=[pl.BlockSpec((None, 1, tb), lambda i, nu: (i, 0, 0), memory_space=pltpu.SMEM),
                  pl.BlockSpec(memory_space=pl.ANY)],
        out_specs=pl.BlockSpec((tb, D_MODEL), lambda i, nu: (i, 0)),
        scratch_shapes=[pltpu.VMEM((tb, D_MODEL), F32), pltpu.SemaphoreType.DMA(())],
    )
    return pl.pallas_call(
        _dispatch_kernel,
        grid_spec=grid_spec,
        out_shape=jax.ShapeDtypeStruct((nblk * tb, D_MODEL), BF16),
        compiler_params=_cparams(("arbitrary",), disable_bounds_checks=True),
        name="moe_dispatch",
    )(nused, slot_tok.reshape(nblk, 1, tb), x2d)


def _expert_kernel(nused_ref, bexp_ref, x_ref, w1_ref, w3_ref, w2_ref, o_ref, acc_ref):
    f = pl.program_id(1)

    @pl.when(pl.program_id(0) < nused_ref[0])
    def _():
        xb = x_ref[...]
        a = _dot(xb, w1_ref[...])
        gate = _dot(xb, w3_ref[...])
        part = _dot((a * jax.nn.sigmoid(a) * gate).astype(BF16), w2_ref[...])

        @pl.when(f == 0)
        def _():
            acc_ref[...] = part

        @pl.when(f > 0)
        def _():
            acc_ref[...] += part

        @pl.when(f == pl.num_programs(1) - 1)
        def _():
            o_ref[...] = acc_ref[...]

    @pl.when(pl.program_id(0) >= nused_ref[0])
    def _():
        o_ref[...] = jnp.zeros_like(o_ref)


def _experts(xb, w13, w2, nused, blk_exp, nblk):
    tb, tf = MOE_TB, MOE_TF
    nf = F_EXPERT // tf

    def blk(i, nu):
        return jnp.minimum(i, nu[0] - 1)

    def ftile(i, f, nu):
        return jnp.where(i < nu[0], f, nf - 1)

    grid_spec = pltpu.PrefetchScalarGridSpec(
        num_scalar_prefetch=2,
        grid=(nblk, nf),
        in_specs=[pl.BlockSpec((tb, D_MODEL), lambda i, f, nu, be: (blk(i, nu), 0)),
                  pl.BlockSpec((None, D_MODEL, tf), lambda i, f, nu, be: (be[blk(i, nu)], 0, ftile(i, f, nu))),
                  pl.BlockSpec((None, D_MODEL, tf), lambda i, f, nu, be: (be[blk(i, nu)], 0, nf + ftile(i, f, nu))),
                  pl.BlockSpec((None, tf, D_MODEL), lambda i, f, nu, be: (be[blk(i, nu)], ftile(i, f, nu), 0))],
        out_specs=pl.BlockSpec((tb, D_MODEL), lambda i, f, nu, be: (i, 0)),
        scratch_shapes=[pltpu.VMEM((tb, D_MODEL), F32)],
    )
    return pl.pallas_call(
        _expert_kernel,
        grid_spec=grid_spec,
        out_shape=jax.ShapeDtypeStruct((nblk * tb, D_MODEL), F32),
        compiler_params=_cparams(("arbitrary", "arbitrary")),
        name="moe_experts",
    )(nused, blk_exp, xb, w13, w13, w2)


def _combine_kernel(dest_ref, y_hbm, x_ref, wts_ref, g_ref, b_ref, o_ref, buf_ref, sem):
    tm = x_ref.shape[0]

    def issue(r, c):
        for k in range(2):
            pltpu.make_async_copy(y_hbm.at[pl.ds(dest_ref[0, 2 * r + k], 1), :],
                                  buf_ref.at[k, pl.ds(r, 1), :], sem).start()
        return c
    lax.fori_loop(0, tm, issue, 0)
    for k in range(2):
        pltpu.make_async_copy(y_hbm.at[pl.ds(0, tm), :], buf_ref.at[k], sem).wait()
    wts = wts_ref[...]
    y = wts[:, 0:1] * buf_ref[0] + wts[:, 1:2] * buf_ref[1]
    o_ref[...] = _layernorm(ALPHA * x_ref[...] + y, g_ref[...], b_ref[...])


def _combine(yb, dest, x2d, wts, g, b):
    n = x2d.shape[0]
    tm = TM_COMB
    nt = n // tm
    row = lambda w: pl.BlockSpec((tm, w), lambda i: (i, 0))
    return pl.pallas_call(
        _combine_kernel,
        grid=(nt,),
        in_specs=[pl.BlockSpec((None, 1, 2 * tm), lambda i: (i, 0, 0), memory_space=pltpu.SMEM),
                  pl.BlockSpec(memory_space=pl.ANY), row(D_MODEL), row(LANES),
                  _const_spec((1, D_MODEL)), _const_spec((1, D_MODEL))],
        out_specs=row(D_MODEL),
        out_shape=jax.ShapeDtypeStruct((n, D_MODEL), F32),
        scratch_shapes=[pltpu.VMEM((2, tm, D_MODEL), F32), pltpu.SemaphoreType.DMA(())],
        compiler_params=_cparams(("arbitrary",), disable_bounds_checks=True),
        name="moe_combine_ln",
    )(dest.reshape(nt, 1, 2 * tm), yb, x2d, wts, g.reshape(1, -1), b.reshape(1, -1))


def _moe(x2d, router, w13, w2, g, b):
    n = x2d.shape[0]
    tb = MOE_TB
    info, wts, cnt = _router(x2d, router)
    idx = info[:, 0:2]
    rank = info[:, 2:4]
    counts = cnt[0, :N_EXPERTS].astype(I32)
    padded = (counts + tb - 1) // tb * tb
    pend = jnp.cumsum(padded)
    pstart = pend - padded
    dest = (pstart[idx] + rank).astype(I32)
    nblk = (2 * n) // tb + N_EXPERTS
    tok = jnp.repeat(jnp.arange(n, dtype=I32), 2)
    slot_tok = jnp.zeros((nblk * tb,), I32).at[dest.reshape(-1)].set(tok)
    nused = (pend[-1] // tb).astype(I32).reshape(1)
    blk_exp = jnp.minimum(jnp.searchsorted(pend, jnp.arange(nblk, dtype=I32) * tb, side='right'),
                          N_EXPERTS - 1).astype(I32)
    xb = _dispatch(x2d, slot_tok, nused, nblk)
    yb = _experts(xb, w13, w2, nused, blk_exp, nblk)
    return _combine(yb, dest.reshape(-1), x2d, wts, g, b)


def kernel(x, mem, positions, rel_bias_table, hgrn_lb_logits, w_in, mla_q_norm, mla_w_uq, mla_kv_norm, mla_w_ukv, swa_sinks, hgrn_norm, w_branch, w_out, ln_g, ln_b, xa_wq, xa_wkv, xa_wo, ffn_w13, ffn_w2, moe_router, moe_w13, moe_w2):
    batch, seq, _ = x.shape
    n = batch * seq
    sm = jax.nn.softmax(hgrn_lb_logits.astype(F32), axis=0)
    lower_bounds = jnp.cumsum(sm, axis=0) - sm[0]
    ctab, stab = _rope_tables(positions)
    xc = x.reshape(n, D_MODEL)
    for l in range(DEPTH):
        wts = _inproj_weights(w_in[l], mla_w_uq[l], mla_w_ukv[l])
        mq, mk, mv, swq, swk, swv, hg, sbq, sbk, sbv = _inproj(xc, wts, ctab, stab, mla_q_norm[l], mla_kv_norm[l])
        y_mla = _mla_attention(mq, mk, mv, batch, seq)
        y_swa = _swa_attention(swq, swk, swv, positions, swa_sinks[l], rel_bias_table, batch, seq)
        y_hg = _hgrn(hg, lower_bounds[l], hgrn_norm[l], batch, seq)
        y_sb = _sb_attention(sbq, sbk, sbv, batch, seq)
        go = _IN_OFF['gates']
        xc = _merge(xc, (y_mla, y_swa, y_hg, y_sb), w_in[l][:, go:].astype(BF16), w_branch[l].astype(BF16),
                    w_out[l].astype(BF16), ln_g[l, 0], ln_b[l, 0])
        mk_, mv_ = _memkv(mem, xa_wkv[l].astype(BF16))
        xc = _xattn(xc, (xa_wq[l] * QK_SCALE).astype(BF16), mk_, mv_, xa_wo[l].astype(BF16),
                    ln_g[l, 1], ln_b[l, 1], batch, seq)
        if l % 2 == 0:
            xc = _ffn(xc, ffn_w13[l // 2].astype(BF16), ffn_w2[l // 2].astype(BF16), ln_g[l, 2], ln_b[l, 2])
        else:
            xc = _moe(xc, moe_router[l // 2], moe_w13[l // 2].astype(BF16), moe_w2[l // 2].astype(BF16),
                      ln_g[l, 2], ln_b[l, 2])
    return xc.reshape(batch, seq, D_MODEL)
```

```python
import functools
import math

import jax
import jax.numpy as jnp
from jax import lax
from jax.experimental import pallas as pl
from jax.experimental.pallas import tpu as pltpu

F32 = jnp.float32
BF16 = jnp.bfloat16
I32 = jnp.int32

D_MODEL = 1024
DEPTH = 2
EPS = 1e-5
NEG_BIG = -1e30
LANES = 128
HEAD_DIM = 64
N_HEADS = 4
WIDTH = N_HEADS * HEAD_DIM

MLA_Q_LORA = 256
MLA_KV_LORA = 128
MLA_NOPE = 64
MLA_ROPE = 32
ROPE_THETA = 10000.0
MLA_SCALE = (MLA_NOPE + MLA_ROPE) ** -0.5
LOG2E = math.log2(math.e)
QK_SCALE = HEAD_DIM ** -0.5

SB_RUN_FLOOR = -104.0
SWA_WINDOW = 128
REL_BUCKETS = 32
REL_MAX_DIST = 128
HGRN_CHUNK = 64
N_EXPERTS = 8
F_DENSE = 2816
F_EXPERT = 3584
ALPHA = (2 * DEPTH) ** 0.25

_IN_SPLITS = (('mla_cq', 256), ('mla_ckv', 128), ('mla_kr', 32), ('swa_q', 256), ('swa_k', 128),
              ('swa_v', 128), ('hgrn', 1024), ('sb_q', 256), ('sb_k', 256), ('sb_v', 256), ('gates', 4096))
_IN_OFF = {}
_o = 0
for _n, _w in _IN_SPLITS:
    _IN_OFF[_n] = _o
    _o += _w

_A_SPLITS = (('cq', 256), ('ckv', 128), ('kra', 128), ('krb', 128), ('swa_q', 256), ('swa_k', 256),
             ('swa_v', 256), ('hgrn', 1024), ('sb_q', 256), ('sb_k', 256), ('sb_v', 256))
_A_OFF = {}
_o = 0
for _n, _w in _A_SPLITS:
    _A_OFF[_n] = (_o, _o + _w)
    _o += _w
A_COLS = _o

TM_A = 512
TQ_ATT = 256
HG_ROWS = 256
TM_FFN = 512
TF_FFN = 1408
MOE_TB = 512
MOE_TF = 512
TM_COMB = 256
VMEM_LIMIT = 56 * 1024 * 1024


def _cparams(sem, **kw):
    return pltpu.CompilerParams(dimension_semantics=sem, vmem_limit_bytes=VMEM_LIMIT, **kw)


def _const_spec(shape):
    nd = len(shape)
    return pl.BlockSpec(shape, lambda *_: (0,) * nd)


def _layernorm(v, g, b):
    mu = jnp.mean(v, axis=-1, keepdims=True)
    vc = v - mu
    var = jnp.mean(vc * vc, axis=-1, keepdims=True)
    return vc * lax.rsqrt(var + EPS) * g + b


def _dot(a, b):
    return jnp.dot(a, b, preferred_element_type=F32)


def _dot_nt(a, b):
    return lax.dot_general(a, b, (((1,), (1,)), ((), ())), preferred_element_type=F32)


def _split3(a):
    hi = a.astype(BF16)
    r = a - hi.astype(F32)
    mid = r.astype(BF16)
    lo = (r - mid.astype(F32)).astype(BF16)
    return hi, mid, lo


def _rope_kernel(pos_ref, freq_ref, c_ref, s_ref):
    lane = lax.broadcasted_iota(I32, pos_ref.shape, 1)
    ang = pos_ref[...] * freq_ref[...]
    rope = (lane >= MLA_NOPE) & (lane < MLA_NOPE + MLA_ROPE)
    first = lane < MLA_NOPE + MLA_ROPE // 2
    c_ref[...] = jnp.where(lane < MLA_NOPE, 1.0, jnp.where(rope, jnp.cos(ang), 0.0))
    sn = jnp.sin(ang)
    s_ref[...] = jnp.where(rope, jnp.where(first, -sn, sn), 0.0)


def _rope_tables(positions):
    n = positions.size
    half = MLA_ROPE // 2
    inv_freq = ROPE_THETA ** (-jnp.arange(half, dtype=F32) / half)
    freq = jnp.zeros((1, LANES), F32).at[0, MLA_NOPE:MLA_NOPE + MLA_ROPE].set(jnp.tile(inv_freq, 2))
    posb = jnp.broadcast_to(positions.reshape(n, 1).astype(F32), (n, LANES))
    tm = 1024
    return pl.pallas_call(
        _rope_kernel,
        grid=(n // tm,),
        in_specs=[pl.BlockSpec((tm, LANES), lambda i: (i, 0)), _const_spec((1, LANES))],
        out_specs=[pl.BlockSpec((tm, LANES), lambda i: (i, 0))] * 2,
        out_shape=[jax.ShapeDtypeStruct((n, LANES), F32)] * 2,
        compiler_params=_cparams(("parallel",)),
        name="rope_tables",
    )(posb, freq)


def _inproj_kernel(x_ref, w_ref, c_ref, s_ref, qn_ref, kvn_ref, wuqa_ref, wuqb_ref, wuk_ref, wuv_ref,
                   mq_ref, mk_ref, mv_ref, swq_ref, swk_ref, swv_ref, hg_ref, sbq_ref, sbk_ref, sbv_ref):
    h = _dot(x_ref[...].astype(BF16), w_ref[...])

    def cols(name):
        lo, hi = _A_OFF[name]
        return h[:, lo:hi]

    c = c_ref[...]
    s = s_ref[...]
    c4 = jnp.concatenate([c] * N_HEADS, axis=1)
    s4 = jnp.concatenate([s] * N_HEADS, axis=1)

    cq = cols('cq')
    cqn = (cq * lax.rsqrt(jnp.mean(cq * cq, axis=-1, keepdims=True) + EPS) * qn_ref[...]).astype(BF16)
    q = _dot(cqn, wuqa_ref[...]) * c4 + _dot(cqn, wuqb_ref[...]) * s4
    mq_ref[...] = (q * (MLA_SCALE * LOG2E)).astype(BF16)

    ckv = cols('ckv')
    ckvn = (ckv * lax.rsqrt(jnp.mean(ckv * ckv, axis=-1, keepdims=True) + EPS) * kvn_ref[...]).astype(BF16)
    krot = cols('kra') * c + cols('krb') * s
    mk_ref[...] = (_dot(ckvn, wuk_ref[...]) + jnp.concatenate([krot] * N_HEADS, axis=1)).astype(BF16)
    mv_ref[...] = _dot(ckvn, wuv_ref[...]).astype(BF16)

    swq_ref[...] = cols('swa_q').astype(BF16)
    swk_ref[...] = cols('swa_k').astype(BF16)
    swv_ref[...] = cols('swa_v').astype(BF16)
    hg_ref[...] = cols('hgrn')
    sbq_ref[...] = cols('sb_q').astype(BF16)
    sbk_ref[...] = cols('sb_k').astype(BF16)
    sbv_ref[...] = cols('sb_v').astype(BF16)


def _inproj_weights(w_in, w_uq, w_ukv):
    def seg(name, width):
        o = _IN_OFF[name]
        return w_in[:, o:o + width]

    kr = seg('mla_kr', MLA_ROPE)
    half = MLA_ROPE // 2
    z64 = jnp.zeros((D_MODEL, MLA_NOPE), F32)
    z32 = jnp.zeros((D_MODEL, LANES - MLA_NOPE - MLA_ROPE), F32)
    kra = jnp.concatenate([z64, kr, z32], axis=1)
    krb = jnp.concatenate([z64, kr[:, half:], kr[:, :half], z32], axis=1)
    swk = seg('swa_k', 128)
    swv = seg('swa_v', 128)
    dup = lambda t: jnp.concatenate([t[:, :64], t[:, :64], t[:, 64:], t[:, 64:]], axis=1)
    w_a = jnp.concatenate([
        seg('mla_cq', 256), seg('mla_ckv', 128), kra, krb,
        seg('swa_q', 256) * QK_SCALE, dup(swk), dup(swv),
        seg('hgrn', 1024), seg('sb_q', 256) * QK_SCALE, seg('sb_k', 256), seg('sb_v', 256)], axis=1)

    qd = MLA_NOPE + MLA_ROPE
    zq = jnp.zeros((MLA_Q_LORA, LANES - qd), F32)
    zn = jnp.zeros((MLA_Q_LORA, MLA_NOPE), F32)
    qa, qb = [], []
    for hh in range(N_HEADS):
        nope = w_uq[:, hh * qd: hh * qd + MLA_NOPE]
        rope = w_uq[:, hh * qd + MLA_NOPE: (hh + 1) * qd]
        qa += [nope, rope, zq]
        qb += [zn, rope[:, half:], rope[:, :half], zq]
    wuqa = jnp.concatenate(qa, axis=1)
    wuqb = jnp.concatenate(qb, axis=1)
    lane = jnp.arange(N_HEADS * LANES) % LANES
    wuk = jnp.where(lane[None, :] < MLA_NOPE, w_ukv, 0.0)
    wuv = jnp.concatenate([w_ukv[:, hh * LANES + MLA_NOPE:(hh + 1) * LANES] for hh in range(N_HEADS)], axis=1)
    return tuple(t.astype(BF16) for t in (w_a, wuqa, wuqb, wuk, wuv))


def _inproj(x2d, wts, ctab, stab, q_norm, kv_norm):
    n = x2d.shape[0]
    w_a, wuqa, wuqb, wuk, wuv = wts
    tm = TM_A
    row = lambda w: pl.BlockSpec((tm, w), lambda i: (i, 0))
    out_w = (512, 512, 256, 256, 256, 256, 1024, 256, 256, 256)
    out_dt = (BF16, BF16, BF16, BF16, BF16, BF16, F32, BF16, BF16, BF16)
    return pl.pallas_call(
        _inproj_kernel,
        grid=(n // tm,),
        in_specs=[row(D_MODEL), _const_spec(w_a.shape), row(LANES), row(LANES),
                  _const_spec((1, MLA_Q_LORA)), _const_spec((1, MLA_KV_LORA)),
                  _const_spec(wuqa.shape), _const_spec(wuqb.shape), _const_spec(wuk.shape),
                  _const_spec(wuv.shape)],
        out_specs=[row(w) for w in out_w],
        out_shape=[jax.ShapeDtypeStruct((n, w), d) for w, d in zip(out_w, out_dt)],
        compiler_params=_cparams(("parallel",)),
        name="inproj",
    )(x2d, w_a, ctab, stab, q_norm.reshape(1, -1), kv_norm.reshape(1, -1), wuqa, wuqb, wuk, wuv)


def _half_mask(half):
    lane = lax.broadcasted_iota(I32, (1, LANES), 1)
    return (lane < HEAD_DIM) if half == 0 else (lane >= HEAD_DIM)


def _mla_kernel(q_ref, k_ref, v_ref, o_ref):
    tq = q_ref.shape[0]
    i = pl.program_id(1)
    row = lax.broadcasted_iota(I32, (tq, tq), 0)
    col = lax.broadcasted_iota(I32, (tq, tq), 1)
    causal = col <= row
    ones = jnp.ones((1, LANES), BF16)

    heads = range(N_HEADS)

    def scores(j):
        off = pl.multiple_of(j * tq, tq)
        return tuple(_dot_nt(q_ref[:, hh * LANES:(hh + 1) * LANES],
                             k_ref[pl.ds(off, tq), hh * LANES:(hh + 1) * LANES]) for hh in heads)

    def update(j, ss, carry, masked):
        off = pl.multiple_of(j * tq, tq)
        if masked:
            ss = [jnp.where(causal, s, NEG_BIG) for s in ss]
        ms = [jnp.maximum(carry[hh][0], jnp.max(ss[hh], axis=-1, keepdims=True)) for hh in heads]
        pms = [jnp.exp2(ss[hh] - ms[hh]).astype(BF16) for hh in heads]
        new = []
        for hh in heads:
            vb = v_ref[pl.ds(off, tq), (hh // 2) * LANES:(hh // 2 + 1) * LANES]
            vb = jnp.where(_half_mask(hh % 2), vb, ones)
            m, acc = carry[hh]
            new.append((ms[hh], jnp.exp2(m - ms[hh]) * acc + _dot(pms[hh], vb)))
        return tuple(new)

    def step(j, c):
        ss, carry = c
        nxt = scores(j + 1)
        return nxt, update(j, ss, carry, False)

    init = tuple((jnp.full((tq, 1), NEG_BIG, F32), jnp.zeros((tq, LANES), F32)) for _ in heads)
    ss, carry = lax.fori_loop(0, i, step, (scores(0), init))
    carry = update(i, ss, carry, True)
    outs = []
    for p in range(N_HEADS // 2):
        a0 = carry[2 * p][1]
        a1 = carry[2 * p + 1][1]
        outs.append(jnp.where(_half_mask(0), a0 / a0[:, HEAD_DIM:HEAD_DIM + 1], a1 / a1[:, 0:1]))
    o_ref[...] = jnp.concatenate(outs, axis=1).astype(o_ref.dtype)


def _mla_attention(q, k, v, batch, seq):
    tq = TQ_ATT
    q3, k3, v3 = (t.reshape(batch, seq, t.shape[-1]) for t in (q, k, v))
    out = pl.pallas_call(
        _mla_kernel,
        grid=(batch, seq // tq),
        in_specs=[pl.BlockSpec((None, tq, 512), lambda b, i: (b, i, 0)),
                  pl.BlockSpec((None, seq, 512), lambda b, i: (b, 0, 0)),
                  pl.BlockSpec((None, seq, WIDTH), lambda b, i: (b, 0, 0))],
        out_specs=pl.BlockSpec((None, tq, WIDTH), lambda b, i: (b, i, 0)),
        out_shape=jax.ShapeDtypeStruct((batch, seq, WIDTH), BF16),
        compiler_params=_cparams(("parallel", "arbitrary")),
        name="mla_attention",
    )(q3, k3, v3)
    return out.reshape(batch * seq, WIDTH)


def _sb_kernel(q_ref, k_ref, v_ref, o_ref):
    tq = q_ref.shape[0]
    i = pl.program_id(1)
    row = lax.broadcasted_iota(I32, (tq, tq), 0)
    col = lax.broadcasted_iota(I32, (tq, tq), 1)
    strict = col < row
    later = (row > col).astype(BF16)
    qs = []
    for hh in range(N_HEADS):
        qp = q_ref[:, (hh // 2) * LANES:(hh // 2 + 1) * LANES]
        qs.append(jnp.where(_half_mask(hh % 2), qp, jnp.zeros_like(qp)))

    def block(j, carry, diag):
        off = pl.multiple_of(j * tq, tq)
        runs, accs = carry
        heads = range(N_HEADS)
        zs = [_dot_nt(qs[hh], k_ref[pl.ds(off, tq), (hh // 2) * LANES:(hh // 2 + 1) * LANES]) for hh in heads]
        lsps = [jnp.minimum(z, 0.0) - jnp.log(1.0 + jnp.exp(-jnp.abs(z))) for z in zs]
        lsns = [lsp - z for lsp, z in zip(lsps, zs)]
        if diag:
            lsns = [jnp.where(strict, t, 0.0) for t in lsns]
        his = [t.astype(BF16) for t in lsns]
        los = [(t - hi.astype(F32)).astype(BF16) for t, hi in zip(lsns, his)]
        rems = [_dot(hi, later) + _dot(lo, later) for hi, lo in zip(his, los)]
        args = [lsps[hh] + rems[hh] + runs[hh] for hh in heads]
        if diag:
            args = [jnp.where(strict, t, NEG_BIG) for t in args]
        probs = [jnp.exp(t).astype(BF16) for t in args]
        new_runs = tuple(runs[hh] + rems[hh][:, 0:1] + lsns[hh][:, 0:1] for hh in heads)
        new_accs = list(accs)
        for hh in heads:
            p = hh // 2
            vb = v_ref[pl.ds(off, tq), p * LANES:(p + 1) * LANES]
            vb = jnp.where(_half_mask(hh % 2), vb, jnp.zeros_like(vb))
            new_accs[p] = new_accs[p] + _dot(probs[hh], vb)
        return new_runs, tuple(new_accs)

    init = (tuple(jnp.zeros((tq, 1), F32) for _ in range(N_HEADS)),
            tuple(jnp.zeros((tq, LANES), F32) for _ in range(N_HEADS // 2)))
    def still_active(runs):
        top = functools.reduce(jnp.maximum, runs)
        return (jnp.max(top) > SB_RUN_FLOOR).astype(I32)

    runs, accs = block(i, init, True)

    def cond(c):
        return (c[0] < i) & (c[1] > 0)

    def body(c):
        jj, _, runs, accs = c
        runs, accs = block(i - 1 - jj, (runs, accs), False)
        return jj + 1, still_active(runs), runs, accs

    _, _, _, accs = lax.while_loop(cond, body, (jnp.int32(0), still_active(runs), runs, accs))
    o_ref[...] = jnp.concatenate(accs, axis=1).astype(o_ref.dtype)


def _sb_attention(q, k, v, batch, seq):
    tq = TQ_ATT
    q3, k3, v3 = (t.reshape(batch, seq, WIDTH) for t in (q, k, v))
    out = pl.pallas_call(
        _sb_kernel,
        grid=(batch, seq // tq),
        in_specs=[pl.BlockSpec((None, tq, WIDTH), lambda b, i: (b, i, 0)),
                  pl.BlockSpec((None, seq, WIDTH), lambda b, i: (b, 0, 0)),
                  pl.BlockSpec((None, seq, WIDTH), lambda b, i: (b, 0, 0))],
        out_specs=pl.BlockSpec((None, tq, WIDTH), lambda b, i: (b, i, 0)),
        out_shape=jax.ShapeDtypeStruct((batch, seq, WIDTH), BF16),
        compiler_params=_cparams(("parallel", "arbitrary")),
        name="stick_breaking",
    )(q3, k3, v3)
    return out.reshape(batch * seq, WIDTH)


def _rel_bucket(dist):
    exact = REL_BUCKETS // 2
    n = jnp.maximum(dist, 0)
    nf = jnp.maximum(n, 1).astype(F32)
    large = exact + (jnp.log(nf / exact) / math.log(REL_MAX_DIST / exact) * (REL_BUCKETS - exact)).astype(I32)
    large = jnp.clip(large, 0, REL_BUCKETS - 1)
    return jnp.where(n < exact, n, large)


def _swa_kernel(tab_ref, sink_ref, q_ref, kc_ref, kp_ref, vc_ref, vp_ref, pq_ref, pkc_ref, pkp_ref, o_ref):
    w = SWA_WINDOW
    nblk = pl.program_id(1)
    row = lax.broadcasted_iota(I32, (w, w), 0)
    col = lax.broadcasted_iota(I32, (w, w), 1)
    valid_c = col <= row
    valid_p = (col > row) & (nblk > 0)
    pq = pq_ref[...]
    bucket_c = _rel_bucket(pq - pkc_ref[...])
    bucket_p = _rel_bucket(pq - pkp_ref[...])
    bias_c = [jnp.zeros((w, w), F32) for _ in range(N_HEADS)]
    bias_p = [jnp.zeros((w, w), F32) for _ in range(N_HEADS)]
    for b in range(REL_BUCKETS):
        mc = bucket_c == b
        mp = bucket_p == b
        for hh in range(N_HEADS):
            t = tab_ref[b, hh]
            bias_c[hh] = jnp.where(mc, t, bias_c[hh])
            bias_p[hh] = jnp.where(mp, t, bias_p[hh])
    outs = []
    for p in range(N_HEADS // 2):
        sl = slice(p * LANES, (p + 1) * LANES)
        qp = q_ref[:, sl]
        kc, kp, vc, vp = kc_ref[:, sl], kp_ref[:, sl], vc_ref[:, sl], vp_ref[:, sl]
        halves = []
        for half in range(2):
            hh = 2 * p + half
            qh = jnp.where(_half_mask(half), qp, jnp.zeros_like(qp))
            lc = jnp.where(valid_c, _dot_nt(qh, kc) + bias_c[hh], NEG_BIG)
            lp = jnp.where(valid_p, _dot_nt(qh, kp) + bias_p[hh], NEG_BIG)
            sink = sink_ref[hh]
            m = jnp.maximum(jnp.maximum(jnp.max(lc, axis=-1, keepdims=True),
                                        jnp.max(lp, axis=-1, keepdims=True)), sink)
            ec = jnp.exp(lc - m)
            ep = jnp.exp(lp - m)
            den = jnp.sum(ec, axis=-1, keepdims=True) + jnp.sum(ep, axis=-1, keepdims=True) + jnp.exp(sink - m)
            halves.append(_dot((ec / den).astype(BF16), vc) + _dot((ep / den).astype(BF16), vp))
        outs.append(jnp.where(_half_mask(0), halves[0], halves[1]))
    o_ref[...] = jnp.concatenate(outs, axis=1).astype(o_ref.dtype)


def _swa_attention(q, k, v, positions, sinks, rel_table, batch, seq):
    w = SWA_WINDOW
    nb = seq // w
    q3, k3, v3 = (t.reshape(batch, seq, WIDTH) for t in (q, k, v))
    pcol = positions.reshape(batch, seq, 1)
    prow = positions.reshape(batch, 1, seq)
    cur = lambda b, n: (b, n, 0)
    prev = lambda b, n: (b, jnp.maximum(n - 1, 0), 0)
    smem = pl.BlockSpec(memory_space=pltpu.SMEM)
    out = pl.pallas_call(
        _swa_kernel,
        grid=(batch, nb),
        in_specs=[smem, smem,
                  pl.BlockSpec((None, w, WIDTH), cur),
                  pl.BlockSpec((None, w, WIDTH), cur), pl.BlockSpec((None, w, WIDTH), prev),
                  pl.BlockSpec((None, w, WIDTH), cur), pl.BlockSpec((None, w, WIDTH), prev),
                  pl.BlockSpec((None, w, 1), cur),
                  pl.BlockSpec((None, 1, w), lambda b, n: (b, 0, n)),
                  pl.BlockSpec((None, 1, w), lambda b, n: (b, 0, jnp.maximum(n - 1, 0)))],
        out_specs=pl.BlockSpec((None, w, WIDTH), cur),
        out_shape=jax.ShapeDtypeStruct((batch, seq, WIDTH), BF16),
        compiler_params=_cparams(("parallel", "arbitrary")),
        name="swa_attention",
    )(rel_table.astype(F32), sinks.astype(F32), q3, k3, k3, v3, v3, pcol, prow, prow)
    return out.reshape(batch * seq, WIDTH)


def _hgrn_kernel(hg_ref, lb_ref, nw_ref, o_ref, state_ref, w_ref):
    c = HGRN_CHUNK
    @pl.when(pl.program_id(1) == 0)
    def _():
        state_ref[...] = jnp.zeros_like(state_ref)

    r64 = lax.broadcasted_iota(I32, (c, c), 0)
    c64 = lax.broadcasted_iota(I32, (c, c), 1)
    incl = (c64 <= r64).astype(BF16)
    ra = lax.broadcasted_iota(I32, (WIDTH, WIDTH), 0) // HEAD_DIM
    ca = lax.broadcasted_iota(I32, (WIDTH, WIDTH), 1) // HEAD_DIM
    same_head = ra == ca
    seg = same_head.astype(BF16)
    ones_cols = jnp.ones((c, LANES), BF16)
    trow = lax.broadcasted_iota(I32, (c, WIDTH), 0)
    lb = lb_ref[...]
    nw = nw_ref[...]
    group = 8

    for ch in range(hg_ref.shape[0] // c):
        rows = slice(ch * c, (ch + 1) * c)
        qraw = hg_ref[rows, 0:WIDTH]
        fraw = hg_ref[rows, WIDTH:2 * WIDTH]
        v = hg_ref[rows, 2 * WIDTH:3 * WIDTH]
        graw = hg_ref[rows, 3 * WIDTH:4 * WIDTH]
        qf = qraw * jax.nn.sigmoid(qraw)
        forget = lb + (1.0 - lb) * jax.nn.sigmoid(fraw)
        lf = jnp.log(forget)
        kk = 1.0 - forget
        gate = graw * jax.nn.sigmoid(graw)

        lf3 = _split3(lf)
        bc = _dot(incl, lf3[0]) + _dot(incl, lf3[1]) + _dot(incl, lf3[2])
        b_last = bc[c - 1:c, :]
        dn0 = (((0,), (0,)), ((), ()))
        tot_col = sum(lax.dot_general(t, ones_cols, dn0, preferred_element_type=F32) for t in lf3)
        decay_col = jnp.exp(jnp.concatenate([tot_col, tot_col], axis=1))

        state = state_ref[...]
        o = _dot((qf * jnp.exp(bc)).astype(BF16), state.astype(BF16))

        vb = v
        for g0 in range(0, c, group):
            for s_i in range(g0, g0 + group):
                e = jnp.exp(jnp.minimum(bc - bc[s_i:s_i + 1, :], 0.0))
                wgt = jnp.where(trow >= s_i, qf * kk[s_i:s_i + 1, :] * e, 0.0)
                w_ref[(s_i - g0) * c:(s_i - g0 + 1) * c, :] = wgt.astype(BF16)
            att = _dot(w_ref[...], seg)
            for s_i in range(g0, g0 + group):
                o = o + att[(s_i - g0) * c:(s_i - g0 + 1) * c, :] * vb[s_i:s_i + 1, :]

        khat = (kk * jnp.exp(b_last - bc)).astype(BF16)
        upd = lax.dot_general(khat, v.astype(BF16), dn0, preferred_element_type=F32)
        state_ref[...] = decay_col * state + jnp.where(same_head, upd, 0.0)

        o2 = _split3(o * o)
        ms = (_dot(o2[0], seg) + _dot(o2[1], seg)) * (1.0 / HEAD_DIM)
        o_ref[rows, :] = (o * lax.rsqrt(ms + EPS) * nw * gate).astype(o_ref.dtype)


def _hgrn(hg, lower_bound, norm_w, batch, seq):
    rows = HG_ROWS
    hg3 = hg.reshape(batch, seq, 4 * WIDTH)
    out = pl.pallas_call(
        _hgrn_kernel,
        grid=(batch, seq // rows),
        in_specs=[pl.BlockSpec((None, rows, 4 * WIDTH), lambda b, i: (b, i, 0)),
                  _const_spec((1, WIDTH)), _const_spec((1, WIDTH))],
        out_specs=pl.BlockSpec((None, rows, WIDTH), lambda b, i: (b, i, 0)),
        out_shape=jax.ShapeDtypeStruct((batch, seq, WIDTH), BF16),
        scratch_shapes=[pltpu.VMEM((WIDTH, WIDTH), F32), pltpu.VMEM((8 * HGRN_CHUNK, WIDTH), BF16)],
        compiler_params=_cparams(("parallel", "arbitrary")),
        name="hgrn2",
    )(hg3, lower_bound.reshape(1, WIDTH).astype(F32), norm_w.reshape(1, WIDTH).astype(F32))
    return out.reshape(batch * seq, WIDTH)


def _merge_kernel(x_ref, y0_ref, y1_ref, y2_ref, y3_ref, wg_ref, wb_ref, wo_ref, g_ref, b_ref, o_ref):
    x = x_ref[...]
    xb = x.astype(BF16)
    merged = jnp.zeros(x.shape, F32)
    for nbr, y_ref in enumerate((y0_ref, y1_ref, y2_ref, y3_ref)):
        gate = jax.nn.sigmoid(_dot(xb, wg_ref[:, nbr * D_MODEL:(nbr + 1) * D_MODEL]))
        merged = merged + gate * _dot(y_ref[...], wb_ref[nbr])
    y = _dot(merged.astype(BF16), wo_ref[...])
    o_ref[...] = _layernorm(ALPHA * x + y, g_ref[...], b_ref[...])


def _merge(x2d, ys, wg, wb, wo, g, b):
    n = x2d.shape[0]
    tm = TM_A
    row = lambda w: pl.BlockSpec((tm, w), lambda i: (i, 0))
    return pl.pallas_call(
        _merge_kernel,
        grid=(n // tm,),
        in_specs=[row(D_MODEL)] + [row(WIDTH)] * 4 +
                 [_const_spec(wg.shape), _const_spec(wb.shape), _const_spec(wo.shape),
                  _const_spec((1, D_MODEL)), _const_spec((1, D_MODEL))],
        out_specs=row(D_MODEL),
        out_shape=jax.ShapeDtypeStruct((n, D_MODEL), F32),
        compiler_params=_cparams(("parallel",)),
        name="merge_outproj_ln",
    )(x2d, *ys, wg, wb, wo, g.reshape(1, -1), b.reshape(1, -1))


def _memkv_kernel(m_ref, w_ref, k_ref, v_ref):
    kv = _dot(m_ref[...].astype(BF16), w_ref[...])
    k_ref[...] = kv[:, :WIDTH].astype(BF16)
    v_ref[...] = kv[:, WIDTH:].astype(BF16)


def _memkv(mem, wkv):
    batch, m, _ = mem.shape
    return pl.pallas_call(
        _memkv_kernel,
        grid=(batch,),
        in_specs=[pl.BlockSpec((None, m, D_MODEL), lambda b: (b, 0, 0)), _const_spec(wkv.shape)],
        out_specs=[pl.BlockSpec((None, m, WIDTH), lambda b: (b, 0, 0))] * 2,
        out_shape=[jax.ShapeDtypeStruct((batch, m, WIDTH), BF16)] * 2,
        compiler_params=_cparams(("parallel",)),
        name="mem_kv",
    )(mem, wkv)


def _xattn_kernel(x_ref, wq_ref, k_ref, v_ref, wo_ref, g_ref, b_ref, o_ref):
    x = x_ref[...]
    q = _dot(x.astype(BF16), wq_ref[...]).astype(BF16)
    k = k_ref[...]
    v = v_ref[...]
    lane = lax.broadcasted_iota(I32, (1, WIDTH), 1) // HEAD_DIM
    o = jnp.zeros((x.shape[0], WIDTH), F32)
    for hh in range(N_HEADS):
        qh = jnp.where(lane == hh, q, jnp.zeros_like(q))
        s = _dot_nt(qh, k)
        e = jnp.exp(s - jnp.max(s, axis=-1, keepdims=True))
        p = e / jnp.sum(e, axis=-1, keepdims=True)
        o = o + jnp.where(lane == hh, _dot(p.astype(BF16), v), 0.0)
    y = _dot(o.astype(BF16), wo_ref[...])
    o_ref[...] = _layernorm(ALPHA * x + y, g_ref[...], b_ref[...])


def _xattn(x2d, wq, k, v, wo, g, b, batch, seq):
    tm = TM_A
    m = k.shape[1]
    x3 = x2d.reshape(batch, seq, D_MODEL)
    row = pl.BlockSpec((None, tm, D_MODEL), lambda bb, i: (bb, i, 0))
    kv_spec = pl.BlockSpec((None, m, WIDTH), lambda bb, i: (bb, 0, 0))
    out = pl.pallas_call(
        _xattn_kernel,
        grid=(batch, seq // tm),
        in_specs=[row, _const_spec(wq.shape), kv_spec, kv_spec, _const_spec(wo.shape),
                  _const_spec((1, D_MODEL)), _const_spec((1, D_MODEL))],
        out_specs=row,
        out_shape=jax.ShapeDtypeStruct((batch, seq, D_MODEL), F32),
        compiler_params=_cparams(("parallel", "parallel")),
        name="mem_xattn_ln",
    )(x3, wq, k, v, wo, g.reshape(1, -1), b.reshape(1, -1))
    return out.reshape(batch * seq, D_MODEL)


def _ffn_kernel(x_ref, w1_ref, w3_ref, w2_ref, g_ref, b_ref, o_ref, acc_ref):
    f = pl.program_id(1)
    xb = x_ref[...].astype(BF16)
    a = _dot(xb, w1_ref[...])
    gate = _dot(xb, w3_ref[...])
    part = _dot((a * jax.nn.sigmoid(a) * gate).astype(BF16), w2_ref[...])

    @pl.when(f == 0)
    def _():
        acc_ref[...] = part

    @pl.when(f > 0)
    def _():
        acc_ref[...] += part

    @pl.when(f == pl.num_programs(1) - 1)
    def _():
        o_ref[...] = _layernorm(ALPHA * x_ref[...] + acc_ref[...], g_ref[...], b_ref[...])


def _ffn(x2d, w13, w2, g, b):
    n = x2d.shape[0]
    tm, tf = TM_FFN, TF_FFN
    nf = F_DENSE // tf
    return pl.pallas_call(
        _ffn_kernel,
        grid=(n // tm, nf),
        in_specs=[pl.BlockSpec((tm, D_MODEL), lambda i, f: (i, 0)),
                  pl.BlockSpec((D_MODEL, tf), lambda i, f: (0, f)),
                  pl.BlockSpec((D_MODEL, tf), lambda i, f: (0, nf + f)),
                  pl.BlockSpec((tf, D_MODEL), lambda i, f: (f, 0)),
                  _const_spec((1, D_MODEL)), _const_spec((1, D_MODEL))],
        out_specs=pl.BlockSpec((tm, D_MODEL), lambda i, f: (i, 0)),
        out_shape=jax.ShapeDtypeStruct((n, D_MODEL), F32),
        scratch_shapes=[pltpu.VMEM((tm, D_MODEL), F32)],
        compiler_params=_cparams(("parallel", "arbitrary")),
        name="ffn_ln",
    )(x2d, w13, w13, w2, g.reshape(1, -1), b.reshape(1, -1))


def _router_kernel(x_ref, r_ref, info_ref, wts_ref, cnt_ref, carry_ref):
    tm = x_ref.shape[0]

    @pl.when(pl.program_id(0) == 0)
    def _():
        carry_ref[...] = jnp.zeros_like(carry_ref)

    logits = jnp.dot(x_ref[...], r_ref[...], precision=lax.Precision.HIGHEST, preferred_element_type=F32)
    lane = lax.broadcasted_iota(I32, (tm, LANES), 1)
    lg = jnp.where(lane < N_EXPERTS, logits, -jnp.inf)
    m1 = jnp.max(lg, axis=-1, keepdims=True)
    i1 = jnp.min(jnp.where(lg == m1, lane, LANES), axis=-1, keepdims=True)
    lg2 = jnp.where(lane == i1, -jnp.inf, lg)
    m2 = jnp.max(lg2, axis=-1, keepdims=True)
    i2 = jnp.min(jnp.where(lg2 == m2, lane, LANES), axis=-1, keepdims=True)
    e = jnp.exp(m2 - m1)
    w1 = 1.0 / (1.0 + e)
    w2 = e / (1.0 + e)
    sel1 = lane == i1
    sel2 = lane == i2
    chosen = jnp.where(sel1 | sel2, 1.0, 0.0)
    row = lax.broadcasted_iota(I32, (tm, tm), 0)
    col = lax.broadcasted_iota(I32, (tm, tm), 1)
    before = (col < row).astype(BF16)
    ranks = _dot(before, chosen.astype(BF16)) + carry_ref[...]
    r1 = jnp.sum(jnp.where(sel1, ranks, 0.0), axis=-1, keepdims=True).astype(I32)
    r2 = jnp.sum(jnp.where(sel2, ranks, 0.0), axis=-1, keepdims=True).astype(I32)
    carry_ref[...] = carry_ref[...] + jnp.sum(chosen, axis=0, keepdims=True)
    info_ref[...] = jnp.where(lane == 0, i1, jnp.where(lane == 1, i2, jnp.where(lane == 2, r1,
                              jnp.where(lane == 3, r2, 0))))
    wts_ref[...] = jnp.where(lane == 0, w1, jnp.where(lane == 1, w2, 0.0))
    cnt_ref[...] = carry_ref[...]


def _router(x2d, router):
    n = x2d.shape[0]
    tm = TM_A
    r_pad = jnp.zeros((D_MODEL, LANES), F32).at[:, :N_EXPERTS].set(router.astype(F32))
    row = pl.BlockSpec((tm, LANES), lambda i: (i, 0))
    return pl.pallas_call(
        _router_kernel,
        grid=(n // tm,),
        in_specs=[pl.BlockSpec((tm, D_MODEL), lambda i: (i, 0)), _const_spec(r_pad.shape)],
        out_specs=[row, row, _const_spec((1, LANES))],
        out_shape=[jax.ShapeDtypeStruct((n, LANES), I32), jax.ShapeDtypeStruct((n, LANES), F32),
                   jax.ShapeDtypeStruct((1, LANES), F32)],
        scratch_shapes=[pltpu.VMEM((1, LANES), F32)],
        compiler_params=_cparams(("arbitrary",)),
        name="moe_router",
    )(x2d, r_pad)


def _dispatch_kernel(nused_ref, tok_ref, x_hbm, o_ref, buf_ref, sem):
    tb = buf_ref.shape[0]

    @pl.when(pl.program_id(0) < nused_ref[0])
    def _():
        def issue(r, c):
            pltpu.make_async_copy(x_hbm.at[pl.ds(tok_ref[0, r], 1), :], buf_ref.at[pl.ds(r, 1), :], sem).start()
            return c
        lax.fori_loop(0, tb, issue, 0)
        pltpu.make_async_copy(x_hbm.at[pl.ds(0, tb), :], buf_ref, sem).wait()
        o_ref[...] = buf_ref[...].astype(o_ref.dtype)

    @pl.when(pl.program_id(0) >= nused_ref[0])
    def _():
        o_ref[...] = jnp.zeros_like(o_ref)


def _dispatch(x2d, slot_tok, nused, nblk):
    tb = MOE_TB
    grid_spec = pltpu.PrefetchScalarGridSpec(
        num_scalar_prefetch=1,
        grid=(nblk,),
        in_specs=[pl.BlockSpec((None, 1, tb), lambda i, nu: (i, 0, 0), memory_space=pltpu.SMEM),
                  pl.BlockSpec(memory_space=pl.ANY)],
        out_specs=pl.BlockSpec((tb, D_MODEL), lambda i, nu: (i, 0)),
        scratch_shapes=[pltpu.VMEM((tb, D_MODEL), F32), pltpu.SemaphoreType.DMA(())],
    )
    return pl.pallas_call(
        _dispatch_kernel,
        grid_spec=grid_spec,
        out_shape=jax.ShapeDtypeStruct((nblk * tb, D_MODEL), BF16),
        compiler_params=_cparams(("arbitrary",), disable_bounds_checks=True),
        name="moe_dispatch",
    )(nused, slot_tok.reshape(nblk, 1, tb), x2d)


def _expert_kernel(nused_ref, bexp_ref, x_ref, w1_ref, w3_ref, w2_ref, o_ref, acc_ref):
    f = pl.program_id(1)

    @pl.when(pl.program_id(0) < nused_ref[0])
    def _():
        xb = x_ref[...]
        a = _dot(xb, w1_ref[...])
        gate = _dot(xb, w3_ref[...])
        part = _dot((a * jax.nn.sigmoid(a) * gate).astype(BF16), w2_ref[...])

        @pl.when(f == 0)
        def _():
            acc_ref[...] = part

        @pl.when(f > 0)
        def _():
            acc_ref[...] += part

        @pl.when(f == pl.num_programs(1) - 1)
        def _():
            o_ref[...] = acc_ref[...]

    @pl.when(pl.program_id(0) >= nused_ref[0])
    def _():
        o_ref[...] = jnp.zeros_like(o_ref)


def _experts(xb, w13, w2, nused, blk_exp, nblk):
    tb, tf = MOE_TB, MOE_TF
    nf = F_EXPERT // tf

    def blk(i, nu):
        return jnp.maximum(jnp.minimum(i, nu[0] - 1), 0)

    def ftile(i, f, nu):
        return jnp.where(i < nu[0], f, nf - 1)

    grid_spec = pltpu.PrefetchScalarGridSpec(
        num_scalar_prefetch=2,
        grid=(nblk, nf),
        in_specs=[pl.BlockSpec((tb, D_MODEL), lambda i, f, nu, be: (blk(i, nu), 0)),
                  pl.BlockSpec((None, D_MODEL, tf), lambda i, f, nu, be: (be[blk(i, nu)], 0, ftile(i, f, nu))),
                  pl.BlockSpec((None, D_MODEL, tf), lambda i, f, nu, be: (be[blk(i, nu)], 0, nf + ftile(i, f, nu))),
                  pl.BlockSpec((None, tf, D_MODEL), lambda i, f, nu, be: (be[blk(i, nu)], ftile(i, f, nu), 0))],
        out_specs=pl.BlockSpec((tb, D_MODEL), lambda i, f, nu, be: (i, 0)),
        scratch_shapes=[pltpu.VMEM((tb, D_MODEL), F32)],
    )
    return pl.pallas_call(
        _expert_kernel,
        grid_spec=grid_spec,
        out_shape=jax.ShapeDtypeStruct((nblk * tb, D_MODEL), F32),
        compiler_params=_cparams(("arbitrary", "arbitrary")),
        name="moe_experts",
    )(nused, blk_exp, xb, w13, w13, w2)


def _combine_kernel(dest_ref, y_hbm, x_ref, wts_ref, g_ref, b_ref, o_ref, buf_ref, sem):
    tm = x_ref.shape[0]

    def issue(r, c):
        for k in range(2):
            pltpu.make_async_copy(y_hbm.at[pl.ds(dest_ref[0, 2 * r + k], 1), :],
                                  buf_ref.at[k, pl.ds(r, 1), :], sem).start()
        return c
    lax.fori_loop(0, tm, issue, 0)
    for k in range(2):
        pltpu.make_async_copy(y_hbm.at[pl.ds(0, tm), :], buf_ref.at[k], sem).wait()
    wts = wts_ref[...]
    y = wts[:, 0:1] * buf_ref[0] + wts[:, 1:2] * buf_ref[1]
    o_ref[...] = _layernorm(ALPHA * x_ref[...] + y, g_ref[...], b_ref[...])


def _combine(yb, dest, x2d, wts, g, b):
    n = x2d.shape[0]
    tm = TM_COMB
    nt = n // tm
    row = lambda w: pl.BlockSpec((tm, w), lambda i: (i, 0))
    return pl.pallas_call(
        _combine_kernel,
        grid=(nt,),
        in_specs=[pl.BlockSpec((None, 1, 2 * tm), lambda i: (i, 0, 0), memory_space=pltpu.SMEM),
                  pl.BlockSpec(memory_space=pl.ANY), row(D_MODEL), row(LANES),
                  _const_spec((1, D_MODEL)), _const_spec((1, D_MODEL))],
        out_specs=row(D_MODEL),
        out_shape=jax.ShapeDtypeStruct((n, D_MODEL), F32),
        scratch_shapes=[pltpu.VMEM((2, tm, D_MODEL), F32), pltpu.SemaphoreType.DMA(())],
        compiler_params=_cparams(("arbitrary",), disable_bounds_checks=True),
        name="moe_combine_ln",
    )(dest.reshape(nt, 1, 2 * tm), yb, x2d, wts, g.reshape(1, -1), b.reshape(1, -1))


def _moe(x2d, router, w13, w2, g, b):
    n = x2d.shape[0]
    tb = MOE_TB
    info, wts, cnt = _router(x2d, router)
    idx = info[:, 0:2]
    rank = info[:, 2:4]
    counts = cnt[0, :N_EXPERTS].astype(I32)
    padded = (counts + tb - 1) // tb * tb
    pend = jnp.cumsum(padded)
    pstart = pend - padded
    dest = (pstart[idx] + rank).astype(I32)
    nblk = (2 * n) // tb + N_EXPERTS
    tok = jnp.repeat(jnp.arange(n, dtype=I32), 2)
    slot_tok = jnp.zeros((nblk * tb,), I32).at[dest.reshape(-1)].set(tok)
    nused = (pend[-1] // tb).astype(I32).reshape(1)
    blk_exp = jnp.minimum(jnp.searchsorted(pend, jnp.arange(nblk, dtype=I32) * tb, side='right'),
                          N_EXPERTS - 1).astype(I32)
    xb = _dispatch(x2d, slot_tok, nused, nblk)
    yb = _experts(xb, w13, w2, nused, blk_exp, nblk)
    return _combine(yb, dest.reshape(-1), x2d, wts, g, b)


def kernel(x, mem, positions, rel_bias_table, hgrn_lb_logits, w_in, mla_q_norm, mla_w_uq, mla_kv_norm, mla_w_ukv, swa_sinks, hgrn_norm, w_branch, w_out, ln_g, ln_b, xa_wq, xa_wkv, xa_wo, ffn_w13, ffn_w2, moe_router, moe_w13, moe_w2):
    batch, seq, _ = x.shape
    n = batch * seq
    sm = jax.nn.softmax(hgrn_lb_logits.astype(F32), axis=0)
    lower_bounds = jnp.cumsum(sm, axis=0) - sm[0]
    ctab, stab = _rope_tables(positions)
    xc = x.reshape(n, D_MODEL)
    for l in range(DEPTH):
        wts = _inproj_weights(w_in[l], mla_w_uq[l], mla_w_ukv[l])
        mq, mk, mv, swq, swk, swv, hg, sbq, sbk, sbv = _inproj(xc, wts, ctab, stab, mla_q_norm[l], mla_kv_norm[l])
        y_mla = _mla_attention(mq, mk, mv, batch, seq)
        y_swa = _swa_attention(swq, swk, swv, positions, swa_sinks[l], rel_bias_table, batch, seq)
        y_hg = _hgrn(hg, lower_bounds[l], hgrn_norm[l], batch, seq)
        y_sb = _sb_attention(sbq, sbk, sbv, batch, seq)
        go = _IN_OFF['gates']
        xc = _merge(xc, (y_mla, y_swa, y_hg, y_sb), w_in[l][:, go:].astype(BF16), w_branch[l].astype(BF16),
                    w_out[l].astype(BF16), ln_g[l, 0], ln_b[l, 0])
        mk_, mv_ = _memkv(mem, xa_wkv[l].astype(BF16))
        xc = _xattn(xc, (xa_wq[l] * QK_SCALE).astype(BF16), mk_, mv_, xa_wo[l].astype(BF16),
                    ln_g[l, 1], ln_b[l, 1], batch, seq)
        if l % 2 == 0:
            xc = _ffn(xc, ffn_w13[l // 2].astype(BF16), ffn_w2[l // 2].astype(BF16), ln_g[l, 2], ln_b[l, 2])
        else:
            xc = _moe(xc, moe_router[l // 2], moe_w13[l // 2].astype(BF16), moe_w2[l // 2].astype(BF16),
                      ln_g[l, 2], ln_b[l, 2])
    return xc.reshape(batch, seq, D_MODEL)
```

```python
import functools
import math

import jax
import jax.numpy as jnp
from jax import lax
from jax.experimental import pallas as pl
from jax.experimental.pallas import tpu as pltpu

F32 = jnp.float32
BF16 = jnp.bfloat16
I32 = jnp.int32

D_MODEL = 1024
DEPTH = 2
EPS = 1e-5
NEG_BIG = -1e30
LANES = 128
HEAD_DIM = 64
N_HEADS = 4
WIDTH = N_HEADS * HEAD_DIM

MLA_Q_LORA = 256
MLA_KV_LORA = 128
MLA_NOPE = 64
MLA_ROPE = 32
ROPE_THETA = 10000.0
MLA_SCALE = (MLA_NOPE + MLA_ROPE) ** -0.5
LOG2E = math.log2(math.e)
QK_SCALE = HEAD_DIM ** -0.5

SB_RUN_FLOOR = -104.0
SWA_WINDOW = 128
REL_BUCKETS = 32
REL_MAX_DIST = 128
HGRN_CHUNK = 64
N_EXPERTS = 8
F_DENSE = 2816
F_EXPERT = 3584
ALPHA = (2 * DEPTH) ** 0.25

_IN_SPLITS = (('mla_cq', 256), ('mla_ckv', 128), ('mla_kr', 32), ('swa_q', 256), ('swa_k', 128),
              ('swa_v', 128), ('hgrn', 1024), ('sb_q', 256), ('sb_k', 256), ('sb_v', 256), ('gates', 4096))
_IN_OFF = {}
_o = 0
for _n, _w in _IN_SPLITS:
    _IN_OFF[_n] = _o
    _o += _w

_A_SPLITS = (('cq', 256), ('ckv', 128), ('kra', 128), ('krb', 128), ('swa_q', 256), ('swa_k', 256),
             ('swa_v', 256), ('hgrn', 1024), ('sb_q', 256), ('sb_k', 256), ('sb_v', 256))
_A_OFF = {}
_o = 0
for _n, _w in _A_SPLITS:
    _A_OFF[_n] = (_o, _o + _w)
    _o += _w
A_COLS = _o

TM_A = 512
TQ_ATT = 256
MLA_TQ = 512
MLA_TK = 512
MLA_GROUP = 4
HG_ROWS = 256
TM_FFN = 512
TF_FFN = 1408
MOE_TB = 512
MOE_TF = 1792
TM_COMB = 256
VMEM_LIMIT = 56 * 1024 * 1024


def _cparams(sem, **kw):
    return pltpu.CompilerParams(dimension_semantics=sem, vmem_limit_bytes=VMEM_LIMIT, **kw)


def _const_spec(shape):
    nd = len(shape)
    return pl.BlockSpec(shape, lambda *_: (0,) * nd)


def _layernorm(v, g, b):
    mu = jnp.mean(v, axis=-1, keepdims=True)
    vc = v - mu
    var = jnp.mean(vc * vc, axis=-1, keepdims=True)
    return vc * lax.rsqrt(var + EPS) * g + b


def _dot(a, b):
    return jnp.dot(a, b, preferred_element_type=F32)


def _dot_nt(a, b):
    return lax.dot_general(a, b, (((1,), (1,)), ((), ())), preferred_element_type=F32)


def _split3(a):
    hi = a.astype(BF16)
    r = a - hi.astype(F32)
    mid = r.astype(BF16)
    lo = (r - mid.astype(F32)).astype(BF16)
    return hi, mid, lo


def _rope_kernel(pos_ref, freq_ref, c_ref, s_ref):
    lane = lax.broadcasted_iota(I32, pos_ref.shape, 1)
    ang = pos_ref[...] * freq_ref[...]
    rope = (lane >= MLA_NOPE) & (lane < MLA_NOPE + MLA_ROPE)
    first = lane < MLA_NOPE + MLA_ROPE // 2
    c_ref[...] = jnp.where(lane < MLA_NOPE, 1.0, jnp.where(rope, jnp.cos(ang), 0.0))
    sn = jnp.sin(ang)
    s_ref[...] = jnp.where(rope, jnp.where(first, -sn, sn), 0.0)


def _rope_tables(positions):
    n = positions.size
    half = MLA_ROPE // 2
    inv_freq = ROPE_THETA ** (-jnp.arange(half, dtype=F32) / half)
    freq = jnp.zeros((1, LANES), F32).at[0, MLA_NOPE:MLA_NOPE + MLA_ROPE].set(jnp.tile(inv_freq, 2))
    posb = jnp.broadcast_to(positions.reshape(n, 1).astype(F32), (n, LANES))
    tm = 1024
    return pl.pallas_call(
        _rope_kernel,
        grid=(n // tm,),
        in_specs=[pl.BlockSpec((tm, LANES), lambda i: (i, 0)), _const_spec((1, LANES))],
        out_specs=[pl.BlockSpec((tm, LANES), lambda i: (i, 0))] * 2,
        out_shape=[jax.ShapeDtypeStruct((n, LANES), F32)] * 2,
        compiler_params=_cparams(("parallel",)),
        name="rope_tables",
    )(posb, freq)


def _inproj_kernel(x_ref, w_ref, c_ref, s_ref, qn_ref, kvn_ref, wuqa_ref, wuqb_ref, wuk_ref, wuv_ref,
                   mq_ref, mk_ref, mv_ref, swq_ref, swk_ref, swv_ref, hg_ref, sbq_ref, sbk_ref, sbv_ref):
    h = _dot(x_ref[...].astype(BF16), w_ref[...])

    def cols(name):
        lo, hi = _A_OFF[name]
        return h[:, lo:hi]

    c = c_ref[...]
    s = s_ref[...]
    c4 = jnp.concatenate([c] * N_HEADS, axis=1)
    s4 = jnp.concatenate([s] * N_HEADS, axis=1)

    cq = cols('cq')
    cqn = (cq * lax.rsqrt(jnp.mean(cq * cq, axis=-1, keepdims=True) + EPS) * qn_ref[...]).astype(BF16)
    q = _dot(cqn, wuqa_ref[...]) * c4 + _dot(cqn, wuqb_ref[...]) * s4
    mq_ref[...] = (q * (MLA_SCALE * LOG2E)).astype(BF16)

    ckv = cols('ckv')
    ckvn = (ckv * lax.rsqrt(jnp.mean(ckv * ckv, axis=-1, keepdims=True) + EPS) * kvn_ref[...]).astype(BF16)
    krot = cols('kra') * c + cols('krb') * s
    mk_ref[...] = (_dot(ckvn, wuk_ref[...]) + jnp.concatenate([krot] * N_HEADS, axis=1)).astype(BF16)
    mv_ref[...] = _dot(ckvn, wuv_ref[...]).astype(BF16)

    swq_ref[...] = cols('swa_q').astype(BF16)
    swk_ref[...] = cols('swa_k').astype(BF16)
    swv_ref[...] = cols('swa_v').astype(BF16)
    hg_ref[...] = cols('hgrn')
    sbq_ref[...] = cols('sb_q').astype(BF16)
    sbk_ref[...] = cols('sb_k').astype(BF16)
    sbv_ref[...] = cols('sb_v').astype(BF16)


def _inproj_weights(w_in, w_uq, w_ukv):
    def seg(name, width):
        o = _IN_OFF[name]
        return w_in[:, o:o + width]

    kr = seg('mla_kr', MLA_ROPE)
    half = MLA_ROPE // 2
    z64 = jnp.zeros((D_MODEL, MLA_NOPE), F32)
    z32 = jnp.zeros((D_MODEL, LANES - MLA_NOPE - MLA_ROPE), F32)
    kra = jnp.concatenate([z64, kr, z32], axis=1)
    krb = jnp.concatenate([z64, kr[:, half:], kr[:, :half], z32], axis=1)
    swk = seg('swa_k', 128)
    swv = seg('swa_v', 128)
    dup = lambda t: jnp.concatenate([t[:, :64], t[:, :64], t[:, 64:], t[:, 64:]], axis=1)
    w_a = jnp.concatenate([
        seg('mla_cq', 256), seg('mla_ckv', 128), kra, krb,
        seg('swa_q', 256) * QK_SCALE, dup(swk), dup(swv),
        seg('hgrn', 1024), seg('sb_q', 256) * QK_SCALE, seg('sb_k', 256), seg('sb_v', 256)], axis=1)

    qd = MLA_NOPE + MLA_ROPE
    zq = jnp.zeros((MLA_Q_LORA, LANES - qd), F32)
    zn = jnp.zeros((MLA_Q_LORA, MLA_NOPE), F32)
    qa, qb = [], []
    for hh in range(N_HEADS):
        nope = w_uq[:, hh * qd: hh * qd + MLA_NOPE]
        rope = w_uq[:, hh * qd + MLA_NOPE: (hh + 1) * qd]
        qa += [nope, rope, zq]
        qb += [zn, rope[:, half:], rope[:, :half], zq]
    wuqa = jnp.concatenate(qa, axis=1)
    wuqb = jnp.concatenate(qb, axis=1)
    lane = jnp.arange(N_HEADS * LANES) % LANES
    wuk = jnp.where(lane[None, :] < MLA_NOPE, w_ukv, 0.0)
    wuv = jnp.concatenate([w_ukv[:, hh * LANES + MLA_NOPE:(hh + 1) * LANES] for hh in range(N_HEADS)], axis=1)
    return tuple(t.astype(BF16) for t in (w_a, wuqa, wuqb, wuk, wuv))


def _inproj(x2d, wts, ctab, stab, q_norm, kv_norm):
    n = x2d.shape[0]
    w_a, wuqa, wuqb, wuk, wuv = wts
    tm = TM_A
    row = lambda w: pl.BlockSpec((tm, w), lambda i: (i, 0))
    out_w = (512, 512, 256, 256, 256, 256, 1024, 256, 256, 256)
    out_dt = (BF16, BF16, BF16, BF16, BF16, BF16, F32, BF16, BF16, BF16)
    return pl.pallas_call(
        _inproj_kernel,
        grid=(n // tm,),
        in_specs=[row(D_MODEL), _const_spec(w_a.shape), row(LANES), row(LANES),
                  _const_spec((1, MLA_Q_LORA)), _const_spec((1, MLA_KV_LORA)),
                  _const_spec(wuqa.shape), _const_spec(wuqb.shape), _const_spec(wuk.shape),
                  _const_spec(wuv.shape)],
        out_specs=[row(w) for w in out_w],
        out_shape=[jax.ShapeDtypeStruct((n, w), d) for w, d in zip(out_w, out_dt)],
        compiler_params=_cparams(("parallel",)),
        name="inproj",
    )(x2d, w_a, ctab, stab, q_norm.reshape(1, -1), kv_norm.reshape(1, -1), wuqa, wuqb, wuk, wuv)


def _half_mask(half):
    lane = lax.broadcasted_iota(I32, (1, LANES), 1)
    return (lane < HEAD_DIM) if half == 0 else (lane >= HEAD_DIM)


def _mla_kernel(q_ref, k_ref, v_ref, o_ref):
    tq = q_ref.shape[0]
    tk = MLA_TK
    nsub = tq // tk
    i = pl.program_id(1)
    row = lax.broadcasted_iota(I32, (tq, tk), 0)
    col = lax.broadcasted_iota(I32, (tq, tk), 1)
    ones = jnp.ones((1, LANES), BF16)

    def update(off, carry, heads, mask):
        ss = [_dot_nt(q_ref[:, hh * LANES:(hh + 1) * LANES], k_ref[pl.ds(off, tk), hh * LANES:(hh + 1) * LANES])
              for hh in heads]
        if mask is not None:
            ss = [jnp.where(mask, s, NEG_BIG) for s in ss]
        ms = [jnp.maximum(c[0], jnp.max(s, axis=-1, keepdims=True)) for c, s in zip(carry, ss)]
        pms = [jnp.exp2(s - m).astype(BF16) for s, m in zip(ss, ms)]
        new = []
        for n, hh in enumerate(heads):
            vb = v_ref[pl.ds(off, tk), (hh // 2) * LANES:(hh // 2 + 1) * LANES]
            vb = jnp.where(_half_mask(hh % 2), vb, ones)
            m, acc = carry[n]
            new.append((ms[n], jnp.exp2(m - ms[n]) * acc + _dot(pms[n], vb)))
        return tuple(new)

    accs = []
    for g in range(0, N_HEADS, MLA_GROUP):
        heads = tuple(range(g, g + MLA_GROUP))
        init = tuple((jnp.full((tq, 1), NEG_BIG, F32), jnp.zeros((tq, LANES), F32)) for _ in heads)
        carry = lax.fori_loop(0, i * nsub,
                              lambda j, c, heads=heads: update(pl.multiple_of(j * tk, tk), c, heads, None), init)
        for r in range(nsub):
            carry = update(pl.multiple_of(i * tq + r * tk, tk), carry, heads, col + r * tk <= row)
        accs += [c[1] for c in carry]
    outs = []
    for p in range(N_HEADS // 2):
        a0, a1 = accs[2 * p], accs[2 * p + 1]
        outs.append(jnp.where(_half_mask(0), a0 / a0[:, HEAD_DIM:HEAD_DIM + 1], a1 / a1[:, 0:1]))
    o_ref[...] = jnp.concatenate(outs, axis=1).astype(o_ref.dtype)


def _mla_attention(q, k, v, batch, seq):
    tq = MLA_TQ
    q3, k3, v3 = (t.reshape(batch, seq, t.shape[-1]) for t in (q, k, v))
    out = pl.pallas_call(
        _mla_kernel,
        grid=(batch, seq // tq),
        in_specs=[pl.BlockSpec((None, tq, 512), lambda b, i: (b, i, 0)),
                  pl.BlockSpec((None, seq, 512), lambda b, i: (b, 0, 0)),
                  pl.BlockSpec((None, seq, WIDTH), lambda b, i: (b, 0, 0))],
        out_specs=pl.BlockSpec((None, tq, WIDTH), lambda b, i: (b, i, 0)),
        out_shape=jax.ShapeDtypeStruct((batch, seq, WIDTH), BF16),
        compiler_params=_cparams(("parallel", "arbitrary")),
        name="mla_attention",
    )(q3, k3, v3)
    return out.reshape(batch * seq, WIDTH)


def _sb_kernel(q_ref, k_ref, v_ref, o_ref):
    tq = q_ref.shape[0]
    i = pl.program_id(1)
    row = lax.broadcasted_iota(I32, (tq, tq), 0)
    col = lax.broadcasted_iota(I32, (tq, tq), 1)
    strict = col < row
    later = (row > col).astype(BF16)
    qs = []
    for hh in range(N_HEADS):
        qp = q_ref[:, (hh // 2) * LANES:(hh // 2 + 1) * LANES]
        qs.append(jnp.where(_half_mask(hh % 2), qp, jnp.zeros_like(qp)))

    def block(j, carry, diag):
        off = pl.multiple_of(j * tq, tq)
        runs, accs = carry
        heads = range(N_HEADS)
        zs = [_dot_nt(qs[hh], k_ref[pl.ds(off, tq), (hh // 2) * LANES:(hh // 2 + 1) * LANES]) for hh in heads]
        lsps = [jnp.minimum(z, 0.0) - jnp.log(1.0 + jnp.exp(-jnp.abs(z))) for z in zs]
        lsns = [lsp - z for lsp, z in zip(lsps, zs)]
        if diag:
            lsns = [jnp.where(strict, t, 0.0) for t in lsns]
        his = [t.astype(BF16) for t in lsns]
        los = [(t - hi.astype(F32)).astype(BF16) for t, hi in zip(lsns, his)]
        rems = [_dot(hi, later) + _dot(lo, later) for hi, lo in zip(his, los)]
        args = [lsps[hh] + rems[hh] + runs[hh] for hh in heads]
        if diag:
            args = [jnp.where(strict, t, NEG_BIG) for t in args]
        probs = [jnp.exp(t).astype(BF16) for t in args]
        new_runs = tuple(runs[hh] + rems[hh][:, 0:1] + lsns[hh][:, 0:1] for hh in heads)
        new_accs = list(accs)
        for hh in heads:
            p = hh // 2
            vb = v_ref[pl.ds(off, tq), p * LANES:(p + 1) * LANES]
            vb = jnp.where(_half_mask(hh % 2), vb, jnp.zeros_like(vb))
            new_accs[p] = new_accs[p] + _dot(probs[hh], vb)
        return new_runs, tuple(new_accs)

    init = (tuple(jnp.zeros((tq, 1), F32) for _ in range(N_HEADS)),
            tuple(jnp.zeros((tq, LANES), F32) for _ in range(N_HEADS // 2)))
    def still_active(runs):
        top = functools.reduce(jnp.maximum, runs)
        return (jnp.max(top) > SB_RUN_FLOOR).astype(I32)

    runs, accs = block(i, init, True)

    def cond(c):
        return (c[0] < i) & (c[1] > 0)

    def body(c):
        jj, _, runs, accs = c
        runs, accs = block(i - 1 - jj, (runs, accs), False)
        return jj + 1, still_active(runs), runs, accs

    _, _, _, accs = lax.while_loop(cond, body, (jnp.int32(0), still_active(runs), runs, accs))
    o_ref[...] = jnp.concatenate(accs, axis=1).astype(o_ref.dtype)


def _sb_attention(q, k, v, batch, seq):
    tq = TQ_ATT
    q3, k3, v3 = (t.reshape(batch, seq, WIDTH) for t in (q, k, v))
    out = pl.pallas_call(
        _sb_kernel,
        grid=(batch, seq // tq),
        in_specs=[pl.BlockSpec((None, tq, WIDTH), lambda b, i: (b, i, 0)),
                  pl.BlockSpec((None, seq, WIDTH), lambda b, i: (b, 0, 0)),
                  pl.BlockSpec((None, seq, WIDTH), lambda b, i: (b, 0, 0))],
        out_specs=pl.BlockSpec((None, tq, WIDTH), lambda b, i: (b, i, 0)),
        out_shape=jax.ShapeDtypeStruct((batch, seq, WIDTH), BF16),
        compiler_params=_cparams(("parallel", "arbitrary")),
        name="stick_breaking",
    )(q3, k3, v3)
    return out.reshape(batch * seq, WIDTH)


def _rel_bucket(dist):
    exact = REL_BUCKETS // 2
    n = jnp.maximum(dist, 0)
    nf = jnp.maximum(n, 1).astype(F32)
    large = exact + (jnp.log(nf / exact) / math.log(REL_MAX_DIST / exact) * (REL_BUCKETS - exact)).astype(I32)
    large = jnp.clip(large, 0, REL_BUCKETS - 1)
    return jnp.where(n < exact, n, large)


def _swa_kernel(tab_ref, sink_ref, q_ref, kc_ref, kp_ref, vc_ref, vp_ref, pq_ref, pkc_ref, pkp_ref, o_ref):
    w = SWA_WINDOW
    nblk = pl.program_id(1)
    row = lax.broadcasted_iota(I32, (w, w), 0)
    col = lax.broadcasted_iota(I32, (w, w), 1)
    valid_c = col <= row
    valid_p = (col > row) & (nblk > 0)
    pq = pq_ref[...]
    bucket_c = _rel_bucket(pq - pkc_ref[...])
    bucket_p = _rel_bucket(pq - pkp_ref[...])
    bias_c = [jnp.zeros((w, w), F32) for _ in range(N_HEADS)]
    bias_p = [jnp.zeros((w, w), F32) for _ in range(N_HEADS)]
    for b in range(REL_BUCKETS):
        mc = bucket_c == b
        mp = bucket_p == b
        for hh in range(N_HEADS):
            t = tab_ref[b, hh]
            bias_c[hh] = jnp.where(mc, t, bias_c[hh])
            bias_p[hh] = jnp.where(mp, t, bias_p[hh])
    outs = []
    for p in range(N_HEADS // 2):
        sl = slice(p * LANES, (p + 1) * LANES)
        qp = q_ref[:, sl]
        kc, kp, vc, vp = kc_ref[:, sl], kp_ref[:, sl], vc_ref[:, sl], vp_ref[:, sl]
        halves = []
        for half in range(2):
            hh = 2 * p + half
            qh = jnp.where(_half_mask(half), qp, jnp.zeros_like(qp))
            lc = jnp.where(valid_c, _dot_nt(qh, kc) + bias_c[hh], NEG_BIG)
            lp = jnp.where(valid_p, _dot_nt(qh, kp) + bias_p[hh], NEG_BIG)
            sink = sink_ref[hh]
            m = jnp.maximum(jnp.maximum(jnp.max(lc, axis=-1, keepdims=True),
                                        jnp.max(lp, axis=-1, keepdims=True)), sink)
            ec = jnp.exp(lc - m)
            ep = jnp.exp(lp - m)
            den = jnp.sum(ec, axis=-1, keepdims=True) + jnp.sum(ep, axis=-1, keepdims=True) + jnp.exp(sink - m)
            halves.append(_dot((ec / den).astype(BF16), vc) + _dot((ep / den).astype(BF16), vp))
        outs.append(jnp.where(_half_mask(0), halves[0], halves[1]))
    o_ref[...] = jnp.concatenate(outs, axis=1).astype(o_ref.dtype)


def _swa_attention(q, k, v, positions, sinks, rel_table, batch, seq):
    w = SWA_WINDOW
    nb = seq // w
    q3, k3, v3 = (t.reshape(batch, seq, WIDTH) for t in (q, k, v))
    pcol = positions.reshape(batch, seq, 1)
    prow = positions.reshape(batch, 1, seq)
    cur = lambda b, n: (b, n, 0)
    prev = lambda b, n: (b, jnp.maximum(n - 1, 0), 0)
    smem = pl.BlockSpec(memory_space=pltpu.SMEM)
    out = pl.pallas_call(
        _swa_kernel,
        grid=(batch, nb),
        in_specs=[smem, smem,
                  pl.BlockSpec((None, w, WIDTH), cur),
                  pl.BlockSpec((None, w, WIDTH), cur), pl.BlockSpec((None, w, WIDTH), prev),
                  pl.BlockSpec((None, w, WIDTH), cur), pl.BlockSpec((None, w, WIDTH), prev),
                  pl.BlockSpec((None, w, 1), cur),
                  pl.BlockSpec((None, 1, w), lambda b, n: (b, 0, n)),
                  pl.BlockSpec((None, 1, w), lambda b, n: (b, 0, jnp.maximum(n - 1, 0)))],
        out_specs=pl.BlockSpec((None, w, WIDTH), cur),
        out_shape=jax.ShapeDtypeStruct((batch, seq, WIDTH), BF16),
        compiler_params=_cparams(("parallel", "arbitrary")),
        name="swa_attention",
    )(rel_table.astype(F32), sinks.astype(F32), q3, k3, k3, v3, v3, pcol, prow, prow)
    return out.reshape(batch * seq, WIDTH)


def _hgrn_kernel(hg_ref, lb_ref, nw_ref, o_ref, state_ref, w_ref):
    c = HGRN_CHUNK
    @pl.when(pl.program_id(1) == 0)
    def _():
        state_ref[...] = jnp.zeros_like(state_ref)

    r64 = lax.broadcasted_iota(I32, (c, c), 0)
    c64 = lax.broadcasted_iota(I32, (c, c), 1)
    incl = (c64 <= r64).astype(BF16)
    ra = lax.broadcasted_iota(I32, (WIDTH, WIDTH), 0) // HEAD_DIM
    ca = lax.broadcasted_iota(I32, (WIDTH, WIDTH), 1) // HEAD_DIM
    same_head = ra == ca
    seg = same_head.astype(BF16)
    ones_cols = jnp.ones((c, LANES), BF16)
    trow = lax.broadcasted_iota(I32, (c, WIDTH), 0)
    lb = lb_ref[...]
    nw = nw_ref[...]
    group = 8

    for ch in range(hg_ref.shape[0] // c):
        rows = slice(ch * c, (ch + 1) * c)
        qraw = hg_ref[rows, 0:WIDTH]
        fraw = hg_ref[rows, WIDTH:2 * WIDTH]
        v = hg_ref[rows, 2 * WIDTH:3 * WIDTH]
        graw = hg_ref[rows, 3 * WIDTH:4 * WIDTH]
        qf = qraw * jax.nn.sigmoid(qraw)
        forget = lb + (1.0 - lb) * jax.nn.sigmoid(fraw)
        lf = jnp.log(forget)
        kk = 1.0 - forget
        gate = graw * jax.nn.sigmoid(graw)

        lf3 = _split3(lf)
        bc = _dot(incl, lf3[0]) + _dot(incl, lf3[1]) + _dot(incl, lf3[2])
        b_last = bc[c - 1:c, :]
        dn0 = (((0,), (0,)), ((), ()))
        tot_col = sum(lax.dot_general(t, ones_cols, dn0, preferred_element_type=F32) for t in lf3)
        decay_col = jnp.exp(jnp.concatenate([tot_col, tot_col], axis=1))

        state = state_ref[...]
        o = _dot((qf * jnp.exp(bc)).astype(BF16), state.astype(BF16))

        vb = v
        for g0 in range(0, c, group):
            for s_i in range(g0, g0 + group):
                e = jnp.exp(jnp.minimum(bc - bc[s_i:s_i + 1, :], 0.0))
                wgt = jnp.where(trow >= s_i, qf * kk[s_i:s_i + 1, :] * e, 0.0)
                w_ref[(s_i - g0) * c:(s_i - g0 + 1) * c, :] = wgt.astype(BF16)
            att = _dot(w_ref[...], seg)
            for s_i in range(g0, g0 + group):
                o = o + att[(s_i - g0) * c:(s_i - g0 + 1) * c, :] * vb[s_i:s_i + 1, :]

        khat = (kk * jnp.exp(b_last - bc)).astype(BF16)
        upd = lax.dot_general(khat, v.astype(BF16), dn0, preferred_element_type=F32)
        state_ref[...] = decay_col * state + jnp.where(same_head, upd, 0.0)

        o2 = _split3(o * o)
        ms = (_dot(o2[0], seg) + _dot(o2[1], seg)) * (1.0 / HEAD_DIM)
        o_ref[rows, :] = (o * lax.rsqrt(ms + EPS) * nw * gate).astype(o_ref.dtype)


def _hgrn(hg, lower_bound, norm_w, batch, seq):
    rows = HG_ROWS
    hg3 = hg.reshape(batch, seq, 4 * WIDTH)
    out = pl.pallas_call(
        _hgrn_kernel,
        grid=(batch, seq // rows),
        in_specs=[pl.BlockSpec((None, rows, 4 * WIDTH), lambda b, i: (b, i, 0)),
                  _const_spec((1, WIDTH)), _const_spec((1, WIDTH))],
        out_specs=pl.BlockSpec((None, rows, WIDTH), lambda b, i: (b, i, 0)),
        out_shape=jax.ShapeDtypeStruct((batch, seq, WIDTH), BF16),
        scratch_shapes=[pltpu.VMEM((WIDTH, WIDTH), F32), pltpu.VMEM((8 * HGRN_CHUNK, WIDTH), BF16)],
        compiler_params=_cparams(("parallel", "arbitrary")),
        name="hgrn2",
    )(hg3, lower_bound.reshape(1, WIDTH).astype(F32), norm_w.reshape(1, WIDTH).astype(F32))
    return out.reshape(batch * seq, WIDTH)


def _merge_kernel(x_ref, y0_ref, y1_ref, y2_ref, y3_ref, wg_ref, wb_ref, wo_ref, g_ref, b_ref, o_ref):
    x = x_ref[...]
    xb = x.astype(BF16)
    merged = jnp.zeros(x.shape, F32)
    for nbr, y_ref in enumerate((y0_ref, y1_ref, y2_ref, y3_ref)):
        gate = jax.nn.sigmoid(_dot(xb, wg_ref[:, nbr * D_MODEL:(nbr + 1) * D_MODEL]))
        merged = merged + gate * _dot(y_ref[...], wb_ref[nbr])
    y = _dot(merged.astype(BF16), wo_ref[...])
    o_ref[...] = _layernorm(ALPHA * x + y, g_ref[...], b_ref[...])


def _merge(x2d, ys, wg, wb, wo, g, b):
    n = x2d.shape[0]
    tm = TM_A
    row = lambda w: pl.BlockSpec((tm, w), lambda i: (i, 0))
    return pl.pallas_call(
        _merge_kernel,
        grid=(n // tm,),
        in_specs=[row(D_MODEL)] + [row(WIDTH)] * 4 +
                 [_const_spec(wg.shape), _const_spec(wb.shape), _const_spec(wo.shape),
                  _const_spec((1, D_MODEL)), _const_spec((1, D_MODEL))],
        out_specs=row(D_MODEL),
        out_shape=jax.ShapeDtypeStruct((n, D_MODEL), F32),
        compiler_params=_cparams(("parallel",)),
        name="merge_outproj_ln",
    )(x2d, *ys, wg, wb, wo, g.reshape(1, -1), b.reshape(1, -1))


def _memkv_kernel(m_ref, w_ref, k_ref, v_ref):
    kv = _dot(m_ref[...].astype(BF16), w_ref[...])
    k_ref[...] = kv[:, :WIDTH].astype(BF16)
    v_ref[...] = kv[:, WIDTH:].astype(BF16)


def _memkv(mem, wkv):
    batch, m, _ = mem.shape
    return pl.pallas_call(
        _memkv_kernel,
        grid=(batch,),
        in_specs=[pl.BlockSpec((None, m, D_MODEL), lambda b: (b, 0, 0)), _const_spec(wkv.shape)],
        out_specs=[pl.BlockSpec((None, m, WIDTH), lambda b: (b, 0, 0))] * 2,
        out_shape=[jax.ShapeDtypeStruct((batch, m, WIDTH), BF16)] * 2,
        compiler_params=_cparams(("parallel",)),
        name="mem_kv",
    )(mem, wkv)


def _xattn_kernel(x_ref, wq_ref, k_ref, v_ref, wo_ref, g_ref, b_ref, o_ref):
    x = x_ref[...]
    q = _dot(x.astype(BF16), wq_ref[...]).astype(BF16)
    k = k_ref[...]
    v = v_ref[...]
    lane = lax.broadcasted_iota(I32, (1, WIDTH), 1) // HEAD_DIM
    o = jnp.zeros((x.shape[0], WIDTH), F32)
    for hh in range(N_HEADS):
        qh = jnp.where(lane == hh, q, jnp.zeros_like(q))
        s = _dot_nt(qh, k)
        e = jnp.exp(s - jnp.max(s, axis=-1, keepdims=True))
        p = e / jnp.sum(e, axis=-1, keepdims=True)
        o = o + jnp.where(lane == hh, _dot(p.astype(BF16), v), 0.0)
    y = _dot(o.astype(BF16), wo_ref[...])
    o_ref[...] = _layernorm(ALPHA * x + y, g_ref[...], b_ref[...])


def _xattn(x2d, wq, k, v, wo, g, b, batch, seq):
    tm = TM_A
    m = k.shape[1]
    x3 = x2d.reshape(batch, seq, D_MODEL)
    row = pl.BlockSpec((None, tm, D_MODEL), lambda bb, i: (bb, i, 0))
    kv_spec = pl.BlockSpec((None, m, WIDTH), lambda bb, i: (bb, 0, 0))
    out = pl.pallas_call(
        _xattn_kernel,
        grid=(batch, seq // tm),
        in_specs=[row, _const_spec(wq.shape), kv_spec, kv_spec, _const_spec(wo.shape),
                  _const_spec((1, D_MODEL)), _const_spec((1, D_MODEL))],
        out_specs=row,
        out_shape=jax.ShapeDtypeStruct((batch, seq, D_MODEL), F32),
        compiler_params=_cparams(("parallel", "parallel")),
        name="mem_xattn_ln",
    )(x3, wq, k, v, wo, g.reshape(1, -1), b.reshape(1, -1))
    return out.reshape(batch * seq, D_MODEL)


def _ffn_kernel(x_ref, w1_ref, w3_ref, w2_ref, g_ref, b_ref, o_ref, acc_ref):
    f = pl.program_id(1)
    xb = x_ref[...].astype(BF16)
    a = _dot(xb, w1_ref[...])
    gate = _dot(xb, w3_ref[...])
    part = _dot((a * jax.nn.sigmoid(a) * gate).astype(BF16), w2_ref[...])

    @pl.when(f == 0)
    def _():
        acc_ref[...] = part

    @pl.when(f > 0)
    def _():
        acc_ref[...] += part

    @pl.when(f == pl.num_programs(1) - 1)
    def _():
        o_ref[...] = _layernorm(ALPHA * x_ref[...] + acc_ref[...], g_ref[...], b_ref[...])


def _ffn(x2d, w13, w2, g, b):
    n = x2d.shape[0]
    tm, tf = TM_FFN, TF_FFN
    nf = F_DENSE // tf
    return pl.pallas_call(
        _ffn_kernel,
        grid=(n // tm, nf),
        in_specs=[pl.BlockSpec((tm, D_MODEL), lambda i, f: (i, 0)),
                  pl.BlockSpec((D_MODEL, tf), lambda i, f: (0, f)),
                  pl.BlockSpec((D_MODEL, tf), lambda i, f: (0, nf + f)),
                  pl.BlockSpec((tf, D_MODEL), lambda i, f: (f, 0)),
                  _const_spec((1, D_MODEL)), _const_spec((1, D_MODEL))],
        out_specs=pl.BlockSpec((tm, D_MODEL), lambda i, f: (i, 0)),
        out_shape=jax.ShapeDtypeStruct((n, D_MODEL), F32),
        scratch_shapes=[pltpu.VMEM((tm, D_MODEL), F32)],
        compiler_params=_cparams(("parallel", "arbitrary")),
        name="ffn_ln",
    )(x2d, w13, w13, w2, g.reshape(1, -1), b.reshape(1, -1))


def _router_kernel(x_ref, r_ref, info_ref, wts_ref, cnt_ref, carry_ref):
    tm = x_ref.shape[0]

    @pl.when(pl.program_id(0) == 0)
    def _():
        carry_ref[...] = jnp.zeros_like(carry_ref)

    logits = jnp.dot(x_ref[...], r_ref[...], precision=lax.Precision.HIGHEST, preferred_element_type=F32)
    lane = lax.broadcasted_iota(I32, (tm, LANES), 1)
    lg = jnp.where(lane < N_EXPERTS, logits, -jnp.inf)
    m1 = jnp.max(lg, axis=-1, keepdims=True)
    i1 = jnp.min(jnp.where(lg == m1, lane, LANES), axis=-1, keepdims=True)
    lg2 = jnp.where(lane == i1, -jnp.inf, lg)
    m2 = jnp.max(lg2, axis=-1, keepdims=True)
    i2 = jnp.min(jnp.where(lg2 == m2, lane, LANES), axis=-1, keepdims=True)
    e = jnp.exp(m2 - m1)
    w1 = 1.0 / (1.0 + e)
    w2 = e / (1.0 + e)
    sel1 = lane == i1
    sel2 = lane == i2
    chosen = jnp.where(sel1 | sel2, 1.0, 0.0)
    row = lax.broadcasted_iota(I32, (tm, tm), 0)
    col = lax.broadcasted_iota(I32, (tm, tm), 1)
    before = (col < row).astype(BF16)
    ranks = _dot(before, chosen.astype(BF16)) + carry_ref[...]
    r1 = jnp.sum(jnp.where(sel1, ranks, 0.0), axis=-1, keepdims=True).astype(I32)
    r2 = jnp.sum(jnp.where(sel2, ranks, 0.0), axis=-1, keepdims=True).astype(I32)
    carry_ref[...] = carry_ref[...] + jnp.sum(chosen, axis=0, keepdims=True)
    info_ref[...] = jnp.where(lane == 0, i1, jnp.where(lane == 1, i2, jnp.where(lane == 2, r1,
                              jnp.where(lane == 3, r2, 0))))
    wts_ref[...] = jnp.where(lane == 0, w1, jnp.where(lane == 1, w2, 0.0))
    cnt_ref[...] = carry_ref[...]


def _router(x2d, router):
    n = x2d.shape[0]
    tm = TM_A
    r_pad = jnp.zeros((D_MODEL, LANES), F32).at[:, :N_EXPERTS].set(router.astype(F32))
    row = pl.BlockSpec((tm, LANES), lambda i: (i, 0))
    return pl.pallas_call(
        _router_kernel,
        grid=(n // tm,),
        in_specs=[pl.BlockSpec((tm, D_MODEL), lambda i: (i, 0)), _const_spec(r_pad.shape)],
        out_specs=[row, row, _const_spec((1, LANES))],
        out_shape=[jax.ShapeDtypeStruct((n, LANES), I32), jax.ShapeDtypeStruct((n, LANES), F32),
                   jax.ShapeDtypeStruct((1, LANES), F32)],
        scratch_shapes=[pltpu.VMEM((1, LANES), F32)],
        compiler_params=_cparams(("arbitrary",)),
        name="moe_router",
    )(x2d, r_pad)


def _dispatch_kernel(nused_ref, tok_ref, x_hbm, o_ref, buf_ref, sem):
    tb = buf_ref.shape[0]

    @pl.when(pl.program_id(0) < nused_ref[0])
    def _():
        def issue(r, c):
            pltpu.make_async_copy(x_hbm.at[pl.ds(tok_ref[0, r], 1), :], buf_ref.at[pl.ds(r, 1), :], sem).start()
            return c
        lax.fori_loop(0, tb, issue, 0)
        pltpu.make_async_copy(x_hbm.at[pl.ds(0, tb), :], buf_ref, sem).wait()
        o_ref[...] = buf_ref[...].astype(o_ref.dtype)

    @pl.when(pl.program_id(0) >= nused_ref[0])
    def _():
        o_ref[...] = jnp.zeros_like(o_ref)


def _dispatch(x2d, slot_tok, nused, nblk):
    tb = MOE_TB
    grid_spec = pltpu.PrefetchScalarGridSpec(
        num_scalar_prefetch=1,
        grid=(nblk,),
        in_specs=[pl.BlockSpec((None, 1, tb), lambda i, nu: (i, 0, 0), memory_space=pltpu.SMEM),
                  pl.BlockSpec(memory_space=pl.ANY)],
        out_specs=pl.BlockSpec((tb, D_MODEL), lambda i, nu: (i, 0)),
        scratch_shapes=[pltpu.VMEM((tb, D_MODEL), F32), pltpu.SemaphoreType.DMA(())],
    )
    return pl.pallas_call(
        _dispatch_kernel,
        grid_spec=grid_spec,
        out_shape=jax.ShapeDtypeStruct((nblk * tb, D_MODEL), BF16),
        compiler_params=_cparams(("arbitrary",), disable_bounds_checks=True),
        name="moe_dispatch",
    )(nused, slot_tok.reshape(nblk, 1, tb), x2d)


def _expert_kernel(nused_ref, bexp_ref, x_ref, w1_ref, w3_ref, w2_ref, o_ref, acc_ref):
    f = pl.program_id(1)

    @pl.when(pl.program_id(0) < nused_ref[0])
    def _():
        xb = x_ref[...]
        a = _dot(xb, w1_ref[...])
        gate = _dot(xb, w3_ref[...])
        part = _dot((a * jax.nn.sigmoid(a) * gate).astype(BF16), w2_ref[...])

        @pl.when(f == 0)
        def _():
            acc_ref[...] = part

        @pl.when(f > 0)
        def _():
            acc_ref[...] += part

        @pl.when(f == pl.num_programs(1) - 1)
        def _():
            o_ref[...] = acc_ref[...]

    @pl.when(pl.program_id(0) >= nused_ref[0])
    def _():
        o_ref[...] = jnp.zeros_like(o_ref)


def _experts(xb, w13, w2, nused, blk_exp, nblk):
    tb, tf = MOE_TB, MOE_TF
    nf = F_EXPERT // tf

    def blk(i, nu):
        return jnp.maximum(jnp.minimum(i, nu[0] - 1), 0)

    def ftile(i, f, nu):
        return jnp.where(i < nu[0], f, nf - 1)

    grid_spec = pltpu.PrefetchScalarGridSpec(
        num_scalar_prefetch=2,
        grid=(nblk, nf),
        in_specs=[pl.BlockSpec((tb, D_MODEL), lambda i, f, nu, be: (blk(i, nu), 0)),
                  pl.BlockSpec((None, D_MODEL, tf), lambda i, f, nu, be: (be[blk(i, nu)], 0, ftile(i, f, nu))),
                  pl.BlockSpec((None, D_MODEL, tf), lambda i, f, nu, be: (be[blk(i, nu)], 0, nf + ftile(i, f, nu))),
                  pl.BlockSpec((None, tf, D_MODEL), lambda i, f, nu, be: (be[blk(i, nu)], ftile(i, f, nu), 0))],
        out_specs=pl.BlockSpec((tb, D_MODEL), lambda i, f, nu, be: (i, 0)),
        scratch_shapes=[pltpu.VMEM((tb, D_MODEL), F32)],
    )
    return pl.pallas_call(
        _expert_kernel,
        grid_spec=grid_spec,
        out_shape=jax.ShapeDtypeStruct((nblk * tb, D_MODEL), F32),
        compiler_params=_cparams(("arbitrary", "arbitrary")),
        name="moe_experts",
    )(nused, blk_exp, xb, w13, w13, w2)


def _combine_kernel(dest_ref, y_hbm, x_ref, wts_ref, g_ref, b_ref, o_ref, buf_ref, sem):
    tm = x_ref.shape[0]

    def issue(r, c):
        for k in range(2):
            pltpu.make_async_copy(y_hbm.at[pl.ds(dest_ref[0, 2 * r + k], 1), :],
                                  buf_ref.at[k, pl.ds(r, 1), :], sem).start()
        return c
    lax.fori_loop(0, tm, issue, 0)
    for k in range(2):
        pltpu.make_async_copy(y_hbm.at[pl.ds(0, tm), :], buf_ref.at[k], sem).wait()
    wts = wts_ref[...]
    y = wts[:, 0:1] * buf_ref[0] + wts[:, 1:2] * buf_ref[1]
    o_ref[...] = _layernorm(ALPHA * x_ref[...] + y, g_ref[...], b_ref[...])


def _combine(yb, dest, x2d, wts, g, b):
    n = x2d.shape[0]
    tm = TM_COMB
    nt = n // tm
    row = lambda w: pl.BlockSpec((tm, w), lambda i: (i, 0))
    return pl.pallas_call(
        _combine_kernel,
        grid=(nt,),
        in_specs=[pl.BlockSpec((None, 1, 2 * tm), lambda i: (i, 0, 0), memory_space=pltpu.SMEM),
                  pl.BlockSpec(memory_space=pl.ANY), row(D_MODEL), row(LANES),
                  _const_spec((1, D_MODEL)), _const_spec((1, D_MODEL))],
        out_specs=row(D_MODEL),
        out_shape=jax.ShapeDtypeStruct((n, D_MODEL), F32),
        scratch_shapes=[pltpu.VMEM((2, tm, D_MODEL), F32), pltpu.SemaphoreType.DMA(())],
        compiler_params=_cparams(("arbitrary",), disable_bounds_checks=True),
        name="moe_combine_ln",
    )(dest.reshape(nt, 1, 2 * tm), yb, x2d, wts, g.reshape(1, -1), b.reshape(1, -1))


def _moe(x2d, router, w13, w2, g, b):
    n = x2d.shape[0]
    tb = MOE_TB
    info, wts, cnt = _router(x2d, router)
    idx = info[:, 0:2]
    rank = info[:, 2:4]
    counts = cnt[0, :N_EXPERTS].astype(I32)
    padded = (counts + tb - 1) // tb * tb
    pend = jnp.cumsum(padded)
    pstart = pend - padded
    dest = (pstart[idx] + rank).astype(I32)
    nblk = (2 * n) // tb + N_EXPERTS
    tok = jnp.repeat(jnp.arange(n, dtype=I32), 2)
    slot_tok = jnp.zeros((nblk * tb,), I32).at[dest.reshape(-1)].set(tok)
    nused = (pend[-1] // tb).astype(I32).reshape(1)
    blk_exp = jnp.minimum(jnp.searchsorted(pend, jnp.arange(nblk, dtype=I32) * tb, side='right'),
                          N_EXPERTS - 1).astype(I32)
    xb = _dispatch(x2d, slot_tok, nused, nblk)
    yb = _experts(xb, w13, w2, nused, blk_exp, nblk)
    return _combine(yb, dest.reshape(-1), x2d, wts, g, b)


def kernel(x, mem, positions, rel_bias_table, hgrn_lb_logits, w_in, mla_q_norm, mla_w_uq, mla_kv_norm, mla_w_ukv, swa_sinks, hgrn_norm, w_branch, w_out, ln_g, ln_b, xa_wq, xa_wkv, xa_wo, ffn_w13, ffn_w2, moe_router, moe_w13, moe_w2):
    batch, seq, _ = x.shape
    n = batch * seq
    sm = jax.nn.softmax(hgrn_lb_logits.astype(F32), axis=0)
    lower_bounds = jnp.cumsum(sm, axis=0) - sm[0]
    ctab, stab = _rope_tables(positions)
    xc = x.reshape(n, D_MODEL)
    for l in range(DEPTH):
        wts = _inproj_weights(w_in[l], mla_w_uq[l], mla_w_ukv[l])
        mq, mk, mv, swq, swk, swv, hg, sbq, sbk, sbv = _inproj(xc, wts, ctab, stab, mla_q_norm[l], mla_kv_norm[l])
        y_mla = _mla_attention(mq, mk, mv, batch, seq)
        y_swa = _swa_attention(swq, swk, swv, positions, swa_sinks[l], rel_bias_table, batch, seq)
        y_hg = _hgrn(hg, lower_bounds[l], hgrn_norm[l], batch, seq)
        y_sb = _sb_attention(sbq, sbk, sbv, batch, seq)
        go = _IN_OFF['gates']
        xc = _merge(xc, (y_mla, y_swa, y_hg, y_sb), w_in[l][:, go:].astype(BF16), w_branch[l].astype(BF16),
                    w_out[l].astype(BF16), ln_g[l, 0], ln_b[l, 0])
        mk_, mv_ = _memkv(mem, xa_wkv[l].astype(BF16))
        xc = _xattn(xc, (xa_wq[l] * QK_SCALE).astype(BF16), mk_, mv_, xa_wo[l].astype(BF16),
                    ln_g[l, 1], ln_b[l, 1], batch, seq)
        if l % 2 == 0:
            xc = _ffn(xc, ffn_w13[l // 2].astype(BF16), ffn_w2[l // 2].astype(BF16), ln_g[l, 2], ln_b[l, 2])
        else:
            xc = _moe(xc, moe_router[l // 2], moe_w13[l // 2].astype(BF16), moe_w2[l // 2].astype(BF16),
                      ln_g[l, 2], ln_b[l, 2])
    return xc.reshape(batch, seq, D_MODEL)
```

```python
import functools
import math

import jax
import jax.numpy as jnp
from jax import lax
from jax.experimental import pallas as pl
from jax.experimental.pallas import tpu as pltpu

F32 = jnp.float32
BF16 = jnp.bfloat16
I32 = jnp.int32

D_MODEL = 1024
DEPTH = 2
EPS = 1e-5
NEG_BIG = -1e30
LANES = 128
HEAD_DIM = 64
N_HEADS = 4
WIDTH = N_HEADS * HEAD_DIM

MLA_Q_LORA = 256
MLA_KV_LORA = 128
MLA_NOPE = 64
MLA_ROPE = 32
ROPE_THETA = 10000.0
MLA_SCALE = (MLA_NOPE + MLA_ROPE) ** -0.5
LOG2E = math.log2(math.e)
QK_SCALE = HEAD_DIM ** -0.5

SB_RUN_FLOOR = -104.0
SWA_WINDOW = 128
REL_BUCKETS = 32
REL_MAX_DIST = 128
HGRN_CHUNK = 64
N_EXPERTS = 8
F_DENSE = 2816
F_EXPERT = 3584
ALPHA = (2 * DEPTH) ** 0.25

_IN_SPLITS = (('mla_cq', 256), ('mla_ckv', 128), ('mla_kr', 32), ('swa_q', 256), ('swa_k', 128),
              ('swa_v', 128), ('hgrn', 1024), ('sb_q', 256), ('sb_k', 256), ('sb_v', 256), ('gates', 4096))
_IN_OFF = {}
_o = 0
for _n, _w in _IN_SPLITS:
    _IN_OFF[_n] = _o
    _o += _w

_A_SPLITS = (('cq', 256), ('ckv', 128), ('kra', 128), ('krb', 128), ('swa_q', 256), ('swa_k', 256),
             ('swa_v', 256), ('hgrn', 1024), ('sb_q', 256), ('sb_k', 256), ('sb_v', 256))
_A_OFF = {}
_o = 0
for _n, _w in _A_SPLITS:
    _A_OFF[_n] = (_o, _o + _w)
    _o += _w
A_COLS = _o

TM_A = 512
TQ_ATT = 256
MLA_TQ = 512
MLA_TK = 512
MLA_GROUP = 4
SWA_TQ = 512
HG_ROWS = 256
TM_FFN = 512
TF_FFN = 1408
MOE_TB = 512
MOE_TF = 1792
TM_COMB = 256
TM_DISP = 512
VMEM_LIMIT = 56 * 1024 * 1024


def _cparams(sem, **kw):
    return pltpu.CompilerParams(dimension_semantics=sem, vmem_limit_bytes=VMEM_LIMIT, **kw)


def _const_spec(shape):
    nd = len(shape)
    return pl.BlockSpec(shape, lambda *_: (0,) * nd)


def _layernorm(v, g, b):
    mu = jnp.mean(v, axis=-1, keepdims=True)
    vc = v - mu
    var = jnp.mean(vc * vc, axis=-1, keepdims=True)
    return vc * lax.rsqrt(var + EPS) * g + b


def _dot(a, b):
    return jnp.dot(a, b, preferred_element_type=F32)


def _dot_nt(a, b):
    return lax.dot_general(a, b, (((1,), (1,)), ((), ())), preferred_element_type=F32)


def _split3(a):
    hi = a.astype(BF16)
    r = a - hi.astype(F32)
    mid = r.astype(BF16)
    lo = (r - mid.astype(F32)).astype(BF16)
    return hi, mid, lo


def _rope_kernel(pos_ref, freq_ref, c_ref, s_ref):
    lane = lax.broadcasted_iota(I32, pos_ref.shape, 1)
    ang = pos_ref[...] * freq_ref[...]
    rope = (lane >= MLA_NOPE) & (lane < MLA_NOPE + MLA_ROPE)
    first = lane < MLA_NOPE + MLA_ROPE // 2
    c_ref[...] = jnp.where(lane < MLA_NOPE, 1.0, jnp.where(rope, jnp.cos(ang), 0.0))
    sn = jnp.sin(ang)
    s_ref[...] = jnp.where(rope, jnp.where(first, -sn, sn), 0.0)


def _rope_tables(positions):
    n = positions.size
    half = MLA_ROPE // 2
    inv_freq = ROPE_THETA ** (-jnp.arange(half, dtype=F32) / half)
    freq = jnp.zeros((1, LANES), F32).at[0, MLA_NOPE:MLA_NOPE + MLA_ROPE].set(jnp.tile(inv_freq, 2))
    posb = jnp.broadcast_to(positions.reshape(n, 1).astype(F32), (n, LANES))
    tm = 1024
    return pl.pallas_call(
        _rope_kernel,
        grid=(n // tm,),
        in_specs=[pl.BlockSpec((tm, LANES), lambda i: (i, 0)), _const_spec((1, LANES))],
        out_specs=[pl.BlockSpec((tm, LANES), lambda i: (i, 0))] * 2,
        out_shape=[jax.ShapeDtypeStruct((n, LANES), F32)] * 2,
        compiler_params=_cparams(("parallel",)),
        name="rope_tables",
    )(posb, freq)


def _inproj_kernel(x_ref, w_ref, c_ref, s_ref, qn_ref, kvn_ref, wuqa_ref, wuqb_ref, wuk_ref, wuv_ref,
                   mq_ref, mk_ref, mv_ref, swq_ref, swk_ref, swv_ref, hg_ref, sbq_ref, sbk_ref, sbv_ref):
    h = _dot(x_ref[...].astype(BF16), w_ref[...])

    def cols(name):
        lo, hi = _A_OFF[name]
        return h[:, lo:hi]

    c = c_ref[...]
    s = s_ref[...]
    c4 = jnp.concatenate([c] * N_HEADS, axis=1)
    s4 = jnp.concatenate([s] * N_HEADS, axis=1)

    cq = cols('cq')
    cqn = (cq * lax.rsqrt(jnp.mean(cq * cq, axis=-1, keepdims=True) + EPS) * qn_ref[...]).astype(BF16)
    q = _dot(cqn, wuqa_ref[...]) * c4 + _dot(cqn, wuqb_ref[...]) * s4
    mq_ref[...] = (q * (MLA_SCALE * LOG2E)).astype(BF16)

    ckv = cols('ckv')
    ckvn = (ckv * lax.rsqrt(jnp.mean(ckv * ckv, axis=-1, keepdims=True) + EPS) * kvn_ref[...]).astype(BF16)
    krot = cols('kra') * c + cols('krb') * s
    mk_ref[...] = (_dot(ckvn, wuk_ref[...]) + jnp.concatenate([krot] * N_HEADS, axis=1)).astype(BF16)
    mv_ref[...] = _dot(ckvn, wuv_ref[...]).astype(BF16)

    swq_ref[...] = cols('swa_q').astype(BF16)
    swk_ref[...] = cols('swa_k').astype(BF16)
    swv_ref[...] = cols('swa_v').astype(BF16)
    hg_ref[...] = cols('hgrn')
    sbq_ref[...] = cols('sb_q').astype(BF16)
    sbk_ref[...] = cols('sb_k').astype(BF16)
    sbv_ref[...] = cols('sb_v').astype(BF16)


def _inproj_weights(w_in, w_uq, w_ukv):
    def seg(name, width):
        o = _IN_OFF[name]
        return w_in[:, o:o + width]

    kr = seg('mla_kr', MLA_ROPE)
    half = MLA_ROPE // 2
    z64 = jnp.zeros((D_MODEL, MLA_NOPE), F32)
    z32 = jnp.zeros((D_MODEL, LANES - MLA_NOPE - MLA_ROPE), F32)
    kra = jnp.concatenate([z64, kr, z32], axis=1)
    krb = jnp.concatenate([z64, kr[:, half:], kr[:, :half], z32], axis=1)
    swk = seg('swa_k', 128)
    swv = seg('swa_v', 128)
    dup = lambda t: jnp.concatenate([t[:, :64], t[:, :64], t[:, 64:], t[:, 64:]], axis=1)
    w_a = jnp.concatenate([
        seg('mla_cq', 256), seg('mla_ckv', 128), kra, krb,
        seg('swa_q', 256) * QK_SCALE, dup(swk), dup(swv),
        seg('hgrn', 1024), seg('sb_q', 256) * QK_SCALE, seg('sb_k', 256), seg('sb_v', 256)], axis=1)

    qd = MLA_NOPE + MLA_ROPE
    zq = jnp.zeros((MLA_Q_LORA, LANES - qd), F32)
    zn = jnp.zeros((MLA_Q_LORA, MLA_NOPE), F32)
    qa, qb = [], []
    for hh in range(N_HEADS):
        nope = w_uq[:, hh * qd: hh * qd + MLA_NOPE]
        rope = w_uq[:, hh * qd + MLA_NOPE: (hh + 1) * qd]
        qa += [nope, rope, zq]
        qb += [zn, rope[:, half:], rope[:, :half], zq]
    wuqa = jnp.concatenate(qa, axis=1)
    wuqb = jnp.concatenate(qb, axis=1)
    lane = jnp.arange(N_HEADS * LANES) % LANES
    wuk = jnp.where(lane[None, :] < MLA_NOPE, w_ukv, 0.0)
    wuv = jnp.concatenate([w_ukv[:, hh * LANES + MLA_NOPE:(hh + 1) * LANES] for hh in range(N_HEADS)], axis=1)
    return tuple(t.astype(BF16) for t in (w_a, wuqa, wuqb, wuk, wuv))


def _inproj(x2d, wts, ctab, stab, q_norm, kv_norm):
    n = x2d.shape[0]
    w_a, wuqa, wuqb, wuk, wuv = wts
    tm = TM_A
    row = lambda w: pl.BlockSpec((tm, w), lambda i: (i, 0))
    out_w = (512, 512, 256, 256, 256, 256, 1024, 256, 256, 256)
    out_dt = (BF16, BF16, BF16, BF16, BF16, BF16, F32, BF16, BF16, BF16)
    return pl.pallas_call(
        _inproj_kernel,
        grid=(n // tm,),
        in_specs=[row(D_MODEL), _const_spec(w_a.shape), row(LANES), row(LANES),
                  _const_spec((1, MLA_Q_LORA)), _const_spec((1, MLA_KV_LORA)),
                  _const_spec(wuqa.shape), _const_spec(wuqb.shape), _const_spec(wuk.shape),
                  _const_spec(wuv.shape)],
        out_specs=[row(w) for w in out_w],
        out_shape=[jax.ShapeDtypeStruct((n, w), d) for w, d in zip(out_w, out_dt)],
        compiler_params=_cparams(("parallel",)),
        name="inproj",
    )(x2d, w_a, ctab, stab, q_norm.reshape(1, -1), kv_norm.reshape(1, -1), wuqa, wuqb, wuk, wuv)


def _half_mask(half):
    lane = lax.broadcasted_iota(I32, (1, LANES), 1)
    return (lane < HEAD_DIM) if half == 0 else (lane >= HEAD_DIM)


def _mla_kernel(q_ref, k_ref, v_ref, o_ref):
    tq = q_ref.shape[0]
    tk = MLA_TK
    nsub = tq // tk
    i = pl.program_id(1)
    row = lax.broadcasted_iota(I32, (tq, tk), 0)
    col = lax.broadcasted_iota(I32, (tq, tk), 1)
    ones = jnp.ones((1, LANES), BF16)

    def update(off, carry, heads, mask):
        ss = [_dot_nt(q_ref[:, hh * LANES:(hh + 1) * LANES], k_ref[pl.ds(off, tk), hh * LANES:(hh + 1) * LANES])
              for hh in heads]
        if mask is not None:
            ss = [jnp.where(mask, s, NEG_BIG) for s in ss]
        ms = [jnp.maximum(c[0], jnp.max(s, axis=-1, keepdims=True)) for c, s in zip(carry, ss)]
        pms = [jnp.exp2(s - m).astype(BF16) for s, m in zip(ss, ms)]
        new = []
        for n, hh in enumerate(heads):
            vb = v_ref[pl.ds(off, tk), (hh // 2) * LANES:(hh // 2 + 1) * LANES]
            vb = jnp.where(_half_mask(hh % 2), vb, ones)
            m, acc = carry[n]
            new.append((ms[n], jnp.exp2(m - ms[n]) * acc + _dot(pms[n], vb)))
        return tuple(new)

    accs = []
    for g in range(0, N_HEADS, MLA_GROUP):
        heads = tuple(range(g, g + MLA_GROUP))
        init = tuple((jnp.full((tq, 1), NEG_BIG, F32), jnp.zeros((tq, LANES), F32)) for _ in heads)
        carry = lax.fori_loop(0, i * nsub,
                              lambda j, c, heads=heads: update(pl.multiple_of(j * tk, tk), c, heads, None), init)
        for r in range(nsub):
            carry = update(pl.multiple_of(i * tq + r * tk, tk), carry, heads, col + r * tk <= row)
        accs += [c[1] for c in carry]
    outs = []
    for p in range(N_HEADS // 2):
        a0, a1 = accs[2 * p], accs[2 * p + 1]
        outs.append(jnp.where(_half_mask(0), a0 / a0[:, HEAD_DIM:HEAD_DIM + 1], a1 / a1[:, 0:1]))
    o_ref[...] = jnp.concatenate(outs, axis=1).astype(o_ref.dtype)


def _mla_attention(q, k, v, batch, seq):
    tq = MLA_TQ
    q3, k3, v3 = (t.reshape(batch, seq, t.shape[-1]) for t in (q, k, v))
    out = pl.pallas_call(
        _mla_kernel,
        grid=(batch, seq // tq),
        in_specs=[pl.BlockSpec((None, tq, 512), lambda b, i: (b, i, 0)),
                  pl.BlockSpec((None, seq, 512), lambda b, i: (b, 0, 0)),
                  pl.BlockSpec((None, seq, WIDTH), lambda b, i: (b, 0, 0))],
        out_specs=pl.BlockSpec((None, tq, WIDTH), lambda b, i: (b, i, 0)),
        out_shape=jax.ShapeDtypeStruct((batch, seq, WIDTH), BF16),
        compiler_params=_cparams(("parallel", "arbitrary")),
        name="mla_attention",
    )(q3, k3, v3)
    return out.reshape(batch * seq, WIDTH)


def _sb_kernel(q_ref, k_ref, v_ref, o_ref):
    tq = q_ref.shape[0]
    i = pl.program_id(1)
    row = lax.broadcasted_iota(I32, (tq, tq), 0)
    col = lax.broadcasted_iota(I32, (tq, tq), 1)
    strict = col < row
    later = (row > col).astype(BF16)
    qs = []
    for hh in range(N_HEADS):
        qp = q_ref[:, (hh // 2) * LANES:(hh // 2 + 1) * LANES]
        qs.append(jnp.where(_half_mask(hh % 2), qp, jnp.zeros_like(qp)))

    def block(j, carry, diag):
        off = pl.multiple_of(j * tq, tq)
        runs, accs = carry
        heads = range(N_HEADS)
        zs = [_dot_nt(qs[hh], k_ref[pl.ds(off, tq), (hh // 2) * LANES:(hh // 2 + 1) * LANES]) for hh in heads]
        lsps = [jnp.minimum(z, 0.0) - jnp.log(1.0 + jnp.exp(-jnp.abs(z))) for z in zs]
        lsns = [lsp - z for lsp, z in zip(lsps, zs)]
        if diag:
            lsns = [jnp.where(strict, t, 0.0) for t in lsns]
        his = [t.astype(BF16) for t in lsns]
        los = [(t - hi.astype(F32)).astype(BF16) for t, hi in zip(lsns, his)]
        rems = [_dot(hi, later) + _dot(lo, later) for hi, lo in zip(his, los)]
        args = [lsps[hh] + rems[hh] + runs[hh] for hh in heads]
        if diag:
            args = [jnp.where(strict, t, NEG_BIG) for t in args]
        probs = [jnp.exp(t).astype(BF16) for t in args]
        new_runs = tuple(runs[hh] + rems[hh][:, 0:1] + lsns[hh][:, 0:1] for hh in heads)
        new_accs = list(accs)
        for hh in heads:
            p = hh // 2
            vb = v_ref[pl.ds(off, tq), p * LANES:(p + 1) * LANES]
            vb = jnp.where(_half_mask(hh % 2), vb, jnp.zeros_like(vb))
            new_accs[p] = new_accs[p] + _dot(probs[hh], vb)
        return new_runs, tuple(new_accs)

    init = (tuple(jnp.zeros((tq, 1), F32) for _ in range(N_HEADS)),
            tuple(jnp.zeros((tq, LANES), F32) for _ in range(N_HEADS // 2)))
    def still_active(runs):
        top = functools.reduce(jnp.maximum, runs)
        return (jnp.max(top) > SB_RUN_FLOOR).astype(I32)

    runs, accs = block(i, init, True)

    def cond(c):
        return (c[0] < i) & (c[1] > 0)

    def body(c):
        jj, _, runs, accs = c
        runs, accs = block(i - 1 - jj, (runs, accs), False)
        return jj + 1, still_active(runs), runs, accs

    _, _, _, accs = lax.while_loop(cond, body, (jnp.int32(0), still_active(runs), runs, accs))
    o_ref[...] = jnp.concatenate(accs, axis=1).astype(o_ref.dtype)


def _sb_attention(q, k, v, batch, seq):
    tq = TQ_ATT
    q3, k3, v3 = (t.reshape(batch, seq, WIDTH) for t in (q, k, v))
    out = pl.pallas_call(
        _sb_kernel,
        grid=(batch, seq // tq),
        in_specs=[pl.BlockSpec((None, tq, WIDTH), lambda b, i: (b, i, 0)),
                  pl.BlockSpec((None, seq, WIDTH), lambda b, i: (b, 0, 0)),
                  pl.BlockSpec((None, seq, WIDTH), lambda b, i: (b, 0, 0))],
        out_specs=pl.BlockSpec((None, tq, WIDTH), lambda b, i: (b, i, 0)),
        out_shape=jax.ShapeDtypeStruct((batch, seq, WIDTH), BF16),
        compiler_params=_cparams(("parallel", "arbitrary")),
        name="stick_breaking",
    )(q3, k3, v3)
    return out.reshape(batch * seq, WIDTH)


def _rel_bucket(dist):
    exact = REL_BUCKETS // 2
    n = jnp.maximum(dist, 0)
    nf = jnp.maximum(n, 1).astype(F32)
    large = exact + (jnp.log(nf / exact) / math.log(REL_MAX_DIST / exact) * (REL_BUCKETS - exact)).astype(I32)
    large = jnp.clip(large, 0, REL_BUCKETS - 1)
    return jnp.where(n < exact, n, large)


def _swa_kernel(sink_ref, tab_ref, q_ref, kc_ref, kh_ref, vc_ref, vh_ref, pq_ref, pkc_ref, pkh_ref, o_ref):
    w = SWA_WINDOW
    step = pl.program_id(1)
    row = lax.broadcasted_iota(I32, (w, w), 0)
    col = lax.broadcasted_iota(I32, (w, w), 1)
    valid_c = col <= row
    valid_p = col > row
    tabs = [jnp.broadcast_to(tab_ref[hh:hh + 1, :], (w, LANES)) for hh in range(N_HEADS)]
    ones = jnp.ones((1, LANES), BF16)
    nsub = q_ref.shape[0] // w
    chains = [(r, hh) for r in range(nsub) for hh in range(N_HEADS)]

    def keys(ref, halo_ref, r, hh):
        sl = slice((hh // 2) * LANES, (hh // 2 + 1) * LANES)
        cur = ref[r * w:(r + 1) * w, sl]
        prev = ref[(r - 1) * w:r * w, sl] if r else halo_ref[:, sl]
        return cur, prev

    buckets = []
    for r in range(nsub):
        pq = pq_ref[r * w:(r + 1) * w, :]
        pk_prev = pkc_ref[:, (r - 1) * w:r * w] if r else pkh_ref[...]
        buckets.append((_rel_bucket(pq - pkc_ref[:, r * w:(r + 1) * w]), _rel_bucket(pq - pk_prev)))
    logits = []
    for r, hh in chains:
        qp = q_ref[r * w:(r + 1) * w, (hh // 2) * LANES:(hh // 2 + 1) * LANES]
        qh = jnp.where(_half_mask(hh % 2), qp, jnp.zeros_like(qp))
        kc, kp = keys(kc_ref, kh_ref, r, hh)
        logits.append((_dot_nt(qh, kc), _dot_nt(qh, kp)))
    masked = []
    for (r, hh), (lc, lp) in zip(chains, logits):
        lc = jnp.where(valid_c, lc + jnp.take_along_axis(tabs[hh], buckets[r][0], axis=1), NEG_BIG)
        lp = lp + jnp.take_along_axis(tabs[hh], buckets[r][1], axis=1)
        lp = jnp.where(valid_p if r else valid_p & (step > 0), lp, NEG_BIG)
        masked.append((lc, lp))
    maxes = [jnp.maximum(jnp.maximum(jnp.max(lc, axis=-1, keepdims=True), jnp.max(lp, axis=-1, keepdims=True)),
                         sink_ref[hh]) for (r, hh), (lc, lp) in zip(chains, masked)]
    probs = [(jnp.exp(lc - m).astype(BF16), jnp.exp(lp - m).astype(BF16)) for (lc, lp), m in zip(masked, maxes)]
    outs = {}
    for (r, hh), (ec, ep), m in zip(chains, probs, maxes):
        vc, vp = keys(vc_ref, vh_ref, r, hh)
        mine = _half_mask(hh % 2)
        acc = _dot(ec, jnp.where(mine, vc, ones)) + _dot(ep, jnp.where(mine, vp, ones))
        den = (acc[:, 0:1] if hh % 2 else acc[:, HEAD_DIM:HEAD_DIM + 1]) + jnp.exp(sink_ref[hh] - m)
        outs[(r, hh)] = acc / den
    for r in range(nsub):
        pairs = [jnp.where(_half_mask(0), outs[(r, 2 * p)], outs[(r, 2 * p + 1)]) for p in range(N_HEADS // 2)]
        o_ref[r * w:(r + 1) * w, :] = jnp.concatenate(pairs, axis=1).astype(o_ref.dtype)


def _swa_attention(q, k, v, positions, sinks, rel_table, batch, seq):
    w = SWA_WINDOW
    tq = SWA_TQ
    per = tq // w
    q3, k3, v3 = (t.reshape(batch, seq, WIDTH) for t in (q, k, v))
    pcol = positions.reshape(batch, seq, 1)
    prow = positions.reshape(batch, 1, seq)
    tab = jnp.zeros((N_HEADS, LANES), F32).at[:, :REL_BUCKETS].set(rel_table.astype(F32).T)
    cur = lambda b, n: (b, n, 0)
    halo = lambda b, n: (b, jnp.maximum(n * per - 1, 0), 0)
    out = pl.pallas_call(
        _swa_kernel,
        grid=(batch, seq // tq),
        in_specs=[pl.BlockSpec(memory_space=pltpu.SMEM), _const_spec((N_HEADS, LANES)),
                  pl.BlockSpec((None, tq, WIDTH), cur),
                  pl.BlockSpec((None, tq, WIDTH), cur), pl.BlockSpec((None, w, WIDTH), halo),
                  pl.BlockSpec((None, tq, WIDTH), cur), pl.BlockSpec((None, w, WIDTH), halo),
                  pl.BlockSpec((None, tq, 1), cur),
                  pl.BlockSpec((None, 1, tq), lambda b, n: (b, 0, n)),
                  pl.BlockSpec((None, 1, w), lambda b, n: (b, 0, jnp.maximum(n * per - 1, 0)))],
        out_specs=pl.BlockSpec((None, tq, WIDTH), cur),
        out_shape=jax.ShapeDtypeStruct((batch, seq, WIDTH), BF16),
        compiler_params=_cparams(("parallel", "arbitrary")),
        name="swa_attention",
    )(sinks.astype(F32), tab, q3, k3, k3, v3, v3, pcol, prow, prow)
    return out.reshape(batch * seq, WIDTH)


def _hgrn_kernel(hg_ref, lb_ref, nw_ref, o_ref, state_ref, w_ref):
    c = HGRN_CHUNK
    @pl.when(pl.program_id(1) == 0)
    def _():
        state_ref[...] = jnp.zeros_like(state_ref)

    r64 = lax.broadcasted_iota(I32, (c, c), 0)
    c64 = lax.broadcasted_iota(I32, (c, c), 1)
    incl = (c64 <= r64).astype(BF16)
    ra = lax.broadcasted_iota(I32, (WIDTH, WIDTH), 0) // HEAD_DIM
    ca = lax.broadcasted_iota(I32, (WIDTH, WIDTH), 1) // HEAD_DIM
    same_head = ra == ca
    seg = same_head.astype(BF16)
    ones_cols = jnp.ones((c, LANES), BF16)
    trow = lax.broadcasted_iota(I32, (c, WIDTH), 0)
    lb = lb_ref[...]
    nw = nw_ref[...]
    group = 8

    for ch in range(hg_ref.shape[0] // c):
        rows = slice(ch * c, (ch + 1) * c)
        qraw = hg_ref[rows, 0:WIDTH]
        fraw = hg_ref[rows, WIDTH:2 * WIDTH]
        v = hg_ref[rows, 2 * WIDTH:3 * WIDTH]
        graw = hg_ref[rows, 3 * WIDTH:4 * WIDTH]
        qf = qraw * jax.nn.sigmoid(qraw)
        forget = lb + (1.0 - lb) * jax.nn.sigmoid(fraw)
        lf = jnp.log(forget)
        kk = 1.0 - forget
        gate = graw * jax.nn.sigmoid(graw)

        lf3 = _split3(lf)
        bc = _dot(incl, lf3[0]) + _dot(incl, lf3[1]) + _dot(incl, lf3[2])
        b_last = bc[c - 1:c, :]
        dn0 = (((0,), (0,)), ((), ()))
        tot_col = sum(lax.dot_general(t, ones_cols, dn0, preferred_element_type=F32) for t in lf3)
        decay_col = jnp.exp(jnp.concatenate([tot_col, tot_col], axis=1))

        state = state_ref[...]
        o = _dot((qf * jnp.exp(bc)).astype(BF16), state.astype(BF16))

        vb = v
        for g0 in range(0, c, group):
            for s_i in range(g0, g0 + group):
                e = jnp.exp(jnp.minimum(bc - bc[s_i:s_i + 1, :], 0.0))
                wgt = jnp.where(trow >= s_i, qf * kk[s_i:s_i + 1, :] * e, 0.0)
                w_ref[(s_i - g0) * c:(s_i - g0 + 1) * c, :] = wgt.astype(BF16)
            att = _dot(w_ref[...], seg)
            for s_i in range(g0, g0 + group):
                o = o + att[(s_i - g0) * c:(s_i - g0 + 1) * c, :] * vb[s_i:s_i + 1, :]

        khat = (kk * jnp.exp(b_last - bc)).astype(BF16)
        upd = lax.dot_general(khat, v.astype(BF16), dn0, preferred_element_type=F32)
        state_ref[...] = decay_col * state + jnp.where(same_head, upd, 0.0)

        o2 = _split3(o * o)
        ms = (_dot(o2[0], seg) + _dot(o2[1], seg)) * (1.0 / HEAD_DIM)
        o_ref[rows, :] = (o * lax.rsqrt(ms + EPS) * nw * gate).astype(o_ref.dtype)


def _hgrn(hg, lower_bound, norm_w, batch, seq):
    rows = HG_ROWS
    hg3 = hg.reshape(batch, seq, 4 * WIDTH)
    out = pl.pallas_call(
        _hgrn_kernel,
        grid=(batch, seq // rows),
        in_specs=[pl.BlockSpec((None, rows, 4 * WIDTH), lambda b, i: (b, i, 0)),
                  _const_spec((1, WIDTH)), _const_spec((1, WIDTH))],
        out_specs=pl.BlockSpec((None, rows, WIDTH), lambda b, i: (b, i, 0)),
        out_shape=jax.ShapeDtypeStruct((batch, seq, WIDTH), BF16),
        scratch_shapes=[pltpu.VMEM((WIDTH, WIDTH), F32), pltpu.VMEM((8 * HGRN_CHUNK, WIDTH), BF16)],
        compiler_params=_cparams(("parallel", "arbitrary")),
        name="hgrn2",
    )(hg3, lower_bound.reshape(1, WIDTH).astype(F32), norm_w.reshape(1, WIDTH).astype(F32))
    return out.reshape(batch * seq, WIDTH)


def _merge_kernel(x_ref, y0_ref, y1_ref, y2_ref, y3_ref, wg_ref, wb_ref, wo_ref, g_ref, b_ref, o_ref):
    x = x_ref[...]
    xb = x.astype(BF16)
    merged = jnp.zeros(x.shape, F32)
    for nbr, y_ref in enumerate((y0_ref, y1_ref, y2_ref, y3_ref)):
        gate = jax.nn.sigmoid(_dot(xb, wg_ref[:, nbr * D_MODEL:(nbr + 1) * D_MODEL]))
        merged = merged + gate * _dot(y_ref[...], wb_ref[nbr])
    y = _dot(merged.astype(BF16), wo_ref[...])
    o_ref[...] = _layernorm(ALPHA * x + y, g_ref[...], b_ref[...])


def _merge(x2d, ys, wg, wb, wo, g, b):
    n = x2d.shape[0]
    tm = TM_A
    row = lambda w: pl.BlockSpec((tm, w), lambda i: (i, 0))
    return pl.pallas_call(
        _merge_kernel,
        grid=(n // tm,),
        in_specs=[row(D_MODEL)] + [row(WIDTH)] * 4 +
                 [_const_spec(wg.shape), _const_spec(wb.shape), _const_spec(wo.shape),
                  _const_spec((1, D_MODEL)), _const_spec((1, D_MODEL))],
        out_specs=row(D_MODEL),
        out_shape=jax.ShapeDtypeStruct((n, D_MODEL), F32),
        compiler_params=_cparams(("parallel",)),
        name="merge_outproj_ln",
    )(x2d, *ys, wg, wb, wo, g.reshape(1, -1), b.reshape(1, -1))


def _memkv_kernel(m_ref, w_ref, k_ref, v_ref):
    kv = _dot(m_ref[...].astype(BF16), w_ref[...])
    k_ref[...] = kv[:, :WIDTH].astype(BF16)
    v_ref[...] = kv[:, WIDTH:].astype(BF16)


def _memkv(mem, wkv):
    batch, m, _ = mem.shape
    return pl.pallas_call(
        _memkv_kernel,
        grid=(batch,),
        in_specs=[pl.BlockSpec((None, m, D_MODEL), lambda b: (b, 0, 0)), _const_spec(wkv.shape)],
        out_specs=[pl.BlockSpec((None, m, WIDTH), lambda b: (b, 0, 0))] * 2,
        out_shape=[jax.ShapeDtypeStruct((batch, m, WIDTH), BF16)] * 2,
        compiler_params=_cparams(("parallel",)),
        name="mem_kv",
    )(mem, wkv)


def _xattn_kernel(x_ref, wq_ref, k_ref, v_ref, wo_ref, g_ref, b_ref, o_ref):
    x = x_ref[...]
    q = _dot(x.astype(BF16), wq_ref[...]).astype(BF16)
    k = k_ref[...]
    v = v_ref[...]
    lane = lax.broadcasted_iota(I32, (1, WIDTH), 1) // HEAD_DIM
    o = jnp.zeros((x.shape[0], WIDTH), F32)
    for hh in range(N_HEADS):
        qh = jnp.where(lane == hh, q, jnp.zeros_like(q))
        s = _dot_nt(qh, k)
        e = jnp.exp(s - jnp.max(s, axis=-1, keepdims=True))
        p = e / jnp.sum(e, axis=-1, keepdims=True)
        o = o + jnp.where(lane == hh, _dot(p.astype(BF16), v), 0.0)
    y = _dot(o.astype(BF16), wo_ref[...])
    o_ref[...] = _layernorm(ALPHA * x + y, g_ref[...], b_ref[...])


def _xattn(x2d, wq, k, v, wo, g, b, batch, seq):
    tm = TM_A
    m = k.shape[1]
    x3 = x2d.reshape(batch, seq, D_MODEL)
    row = pl.BlockSpec((None, tm, D_MODEL), lambda bb, i: (bb, i, 0))
    kv_spec = pl.BlockSpec((None, m, WIDTH), lambda bb, i: (bb, 0, 0))
    out = pl.pallas_call(
        _xattn_kernel,
        grid=(batch, seq // tm),
        in_specs=[row, _const_spec(wq.shape), kv_spec, kv_spec, _const_spec(wo.shape),
                  _const_spec((1, D_MODEL)), _const_spec((1, D_MODEL))],
        out_specs=row,
        out_shape=jax.ShapeDtypeStruct((batch, seq, D_MODEL), F32),
        compiler_params=_cparams(("parallel", "parallel")),
        name="mem_xattn_ln",
    )(x3, wq, k, v, wo, g.reshape(1, -1), b.reshape(1, -1))
    return out.reshape(batch * seq, D_MODEL)


def _ffn_kernel(x_ref, w1_ref, w3_ref, w2_ref, g_ref, b_ref, o_ref, acc_ref):
    f = pl.program_id(1)
    xb = x_ref[...].astype(BF16)
    a = _dot(xb, w1_ref[...])
    gate = _dot(xb, w3_ref[...])
    part = _dot((a * jax.nn.sigmoid(a) * gate).astype(BF16), w2_ref[...])

    @pl.when(f == 0)
    def _():
        acc_ref[...] = part

    @pl.when(f > 0)
    def _():
        acc_ref[...] += part

    @pl.when(f == pl.num_programs(1) - 1)
    def _():
        o_ref[...] = _layernorm(ALPHA * x_ref[...] + acc_ref[...], g_ref[...], b_ref[...])


def _ffn(x2d, w13, w2, g, b):
    n = x2d.shape[0]
    tm, tf = TM_FFN, TF_FFN
    nf = F_DENSE // tf
    return pl.pallas_call(
        _ffn_kernel,
        grid=(n // tm, nf),
        in_specs=[pl.BlockSpec((tm, D_MODEL), lambda i, f: (i, 0)),
                  pl.BlockSpec((D_MODEL, tf), lambda i, f: (0, f)),
                  pl.BlockSpec((D_MODEL, tf), lambda i, f: (0, nf + f)),
                  pl.BlockSpec((tf, D_MODEL), lambda i, f: (f, 0)),
                  _const_spec((1, D_MODEL)), _const_spec((1, D_MODEL))],
        out_specs=pl.BlockSpec((tm, D_MODEL), lambda i, f: (i, 0)),
        out_shape=jax.ShapeDtypeStruct((n, D_MODEL), F32),
        scratch_shapes=[pltpu.VMEM((tm, D_MODEL), F32)],
        compiler_params=_cparams(("parallel", "arbitrary")),
        name="ffn_ln",
    )(x2d, w13, w13, w2, g.reshape(1, -1), b.reshape(1, -1))


def _router_kernel(x_ref, r_ref, info_ref, wts_ref, cnt_ref, carry_ref):
    tm = x_ref.shape[0]

    @pl.when(pl.program_id(0) == 0)
    def _():
        carry_ref[...] = jnp.zeros_like(carry_ref)

    logits = jnp.dot(x_ref[...], r_ref[...], precision=lax.Precision.HIGHEST, preferred_element_type=F32)
    lane = lax.broadcasted_iota(I32, (tm, LANES), 1)
    lg = jnp.where(lane < N_EXPERTS, logits, -jnp.inf)
    m1 = jnp.max(lg, axis=-1, keepdims=True)
    i1 = jnp.min(jnp.where(lg == m1, lane, LANES), axis=-1, keepdims=True)
    lg2 = jnp.where(lane == i1, -jnp.inf, lg)
    m2 = jnp.max(lg2, axis=-1, keepdims=True)
    i2 = jnp.min(jnp.where(lg2 == m2, lane, LANES), axis=-1, keepdims=True)
    e = jnp.exp(m2 - m1)
    w1 = 1.0 / (1.0 + e)
    w2 = e / (1.0 + e)
    sel1 = lane == i1
    sel2 = lane == i2
    chosen = jnp.where(sel1 | sel2, 1.0, 0.0)
    row = lax.broadcasted_iota(I32, (tm, tm), 0)
    col = lax.broadcasted_iota(I32, (tm, tm), 1)
    before = (col < row).astype(BF16)
    ranks = _dot(before, chosen.astype(BF16)) + carry_ref[...]
    r1 = jnp.sum(jnp.where(sel1, ranks, 0.0), axis=-1, keepdims=True).astype(I32)
    r2 = jnp.sum(jnp.where(sel2, ranks, 0.0), axis=-1, keepdims=True).astype(I32)
    carry_ref[...] = carry_ref[...] + jnp.sum(chosen, axis=0, keepdims=True)
    info_ref[...] = jnp.where(lane == 0, i1, jnp.where(lane == 1, i2, jnp.where(lane == 2, r1,
                              jnp.where(lane == 3, r2, 0))))
    wts_ref[...] = jnp.where(lane == 0, w1, jnp.where(lane == 1, w2, 0.0))
    cnt_ref[...] = carry_ref[...]


def _router(x2d, router):
    n = x2d.shape[0]
    tm = TM_A
    r_pad = jnp.zeros((D_MODEL, LANES), F32).at[:, :N_EXPERTS].set(router.astype(F32))
    row = pl.BlockSpec((tm, LANES), lambda i: (i, 0))
    return pl.pallas_call(
        _router_kernel,
        grid=(n // tm,),
        in_specs=[pl.BlockSpec((tm, D_MODEL), lambda i: (i, 0)), _const_spec(r_pad.shape)],
        out_specs=[row, row, _const_spec((1, LANES))],
        out_shape=[jax.ShapeDtypeStruct((n, LANES), I32), jax.ShapeDtypeStruct((n, LANES), F32),
                   jax.ShapeDtypeStruct((1, LANES), F32)],
        scratch_shapes=[pltpu.VMEM((1, LANES), F32)],
        compiler_params=_cparams(("arbitrary",)),
        name="moe_router",
    )(x2d, r_pad)


def _dispatch_kernel(pad_ref, dest_ref, x_hbm, xb_hbm, sems):
    tm = dest_ref.shape[-1] // 2
    i = pl.program_id(0)
    slot = i % 2

    def issue(r, c):
        for k in range(2):
            pltpu.make_async_copy(x_hbm.at[pl.ds(i * tm + r, 1), :],
                                  xb_hbm.at[pl.ds(dest_ref[0, 2 * r + k], 1), :], sems.at[slot]).start()
        return c
    lax.fori_loop(0, tm, issue, 0)

    def wait_step(s):
        pltpu.make_async_copy(x_hbm.at[pl.ds(0, 2 * tm), :], xb_hbm.at[pl.ds(0, 2 * tm), :], sems.at[s]).wait()

    @pl.when(i > 0)
    def _():
        wait_step(1 - slot)

    @pl.when(i == pl.num_programs(0) - 1)
    def _():
        wait_step(slot)

        def fill(e, c):
            def one(s, c2):
                pltpu.make_async_copy(x_hbm.at[pl.ds(0, 1), :], xb_hbm.at[pl.ds(s, 1), :], sems.at[2]).start()
                return c2

            def done(s, c2):
                pltpu.make_async_copy(x_hbm.at[pl.ds(0, 1), :], xb_hbm.at[pl.ds(0, 1), :], sems.at[2]).wait()
                return c2
            lax.fori_loop(pad_ref[0, e], pad_ref[1, e], one, 0)
            lax.fori_loop(pad_ref[0, e], pad_ref[1, e], done, 0)
            return c
        lax.fori_loop(0, pad_ref.shape[1], fill, 0)


def _dispatch(x2d, dest, pads, nblk):
    n = x2d.shape[0]
    tm = TM_DISP
    nt = n // tm
    grid_spec = pltpu.PrefetchScalarGridSpec(
        num_scalar_prefetch=1,
        grid=(nt,),
        in_specs=[pl.BlockSpec((None, 1, 2 * tm), lambda i, pads: (i, 0, 0), memory_space=pltpu.SMEM),
                  pl.BlockSpec(memory_space=pl.ANY)],
        out_specs=pl.BlockSpec(memory_space=pl.ANY),
        scratch_shapes=[pltpu.SemaphoreType.DMA((3,))],
    )
    return pl.pallas_call(
        _dispatch_kernel,
        grid_spec=grid_spec,
        out_shape=jax.ShapeDtypeStruct((nblk * MOE_TB, D_MODEL), F32),
        compiler_params=_cparams(("arbitrary",), disable_bounds_checks=True),
        name="moe_dispatch",
    )(pads, dest.reshape(nt, 1, 2 * tm), x2d)


def _expert_kernel(nused_ref, bexp_ref, x_ref, w1_ref, w3_ref, w2_ref, o_ref, acc_ref):
    f = pl.program_id(1)

    @pl.when(pl.program_id(0) < nused_ref[0])
    def _():
        xb = x_ref[...].astype(BF16)
        a = _dot(xb, w1_ref[...])
        gate = _dot(xb, w3_ref[...])
        part = _dot((a * jax.nn.sigmoid(a) * gate).astype(BF16), w2_ref[...])

        @pl.when(f == 0)
        def _():
            acc_ref[...] = part

        @pl.when(f > 0)
        def _():
            acc_ref[...] += part

        @pl.when(f == pl.num_programs(1) - 1)
        def _():
            o_ref[...] = acc_ref[...]

    @pl.when(pl.program_id(0) >= nused_ref[0])
    def _():
        o_ref[...] = jnp.zeros_like(o_ref)


def _experts(xb, w13, w2, nused, blk_exp, nblk):
    tb, tf = MOE_TB, MOE_TF
    nf = F_EXPERT // tf

    def blk(i, nu):
        return jnp.maximum(jnp.minimum(i, nu[0] - 1), 0)

    def ftile(i, f, nu):
        return jnp.where(i < nu[0], f, nf - 1)

    grid_spec = pltpu.PrefetchScalarGridSpec(
        num_scalar_prefetch=2,
        grid=(nblk, nf),
        in_specs=[pl.BlockSpec((tb, D_MODEL), lambda i, f, nu, be: (blk(i, nu), 0)),
                  pl.BlockSpec((None, D_MODEL, tf), lambda i, f, nu, be: (be[blk(i, nu)], 0, ftile(i, f, nu))),
                  pl.BlockSpec((None, D_MODEL, tf), lambda i, f, nu, be: (be[blk(i, nu)], 0, nf + ftile(i, f, nu))),
                  pl.BlockSpec((None, tf, D_MODEL), lambda i, f, nu, be: (be[blk(i, nu)], ftile(i, f, nu), 0))],
        out_specs=pl.BlockSpec((tb, D_MODEL), lambda i, f, nu, be: (i, 0)),
        scratch_shapes=[pltpu.VMEM((tb, D_MODEL), F32)],
    )
    return pl.pallas_call(
        _expert_kernel,
        grid_spec=grid_spec,
        out_shape=jax.ShapeDtypeStruct((nblk * tb, D_MODEL), F32),
        compiler_params=_cparams(("arbitrary", "arbitrary")),
        name="moe_experts",
    )(nused, blk_exp, xb, w13, w13, w2)


def _combine_kernel(dest_ref, nxt_ref, y_hbm, x_ref, wts_ref, g_ref, b_ref, o_ref, buf_ref, sems):
    tm = x_ref.shape[0]
    i = pl.program_id(0)
    slot = i % 2

    def gather(idx_ref, s):
        def issue(r, c):
            for k in range(2):
                pltpu.make_async_copy(y_hbm.at[pl.ds(idx_ref[0, 2 * r + k], 1), :],
                                      buf_ref.at[s, k, pl.ds(r, 1), :], sems.at[s]).start()
            return c
        lax.fori_loop(0, tm, issue, 0)

    @pl.when(i == 0)
    def _():
        gather(dest_ref, slot)

    @pl.when(i + 1 < pl.num_programs(0))
    def _():
        gather(nxt_ref, 1 - slot)

    for k in range(2):
        pltpu.make_async_copy(y_hbm.at[pl.ds(0, tm), :], buf_ref.at[slot, k], sems.at[slot]).wait()
    wts = wts_ref[...]
    y = wts[:, 0:1] * buf_ref[slot, 0] + wts[:, 1:2] * buf_ref[slot, 1]
    o_ref[...] = _layernorm(ALPHA * x_ref[...] + y, g_ref[...], b_ref[...])


def _combine(yb, dest, x2d, wts, g, b):
    n = x2d.shape[0]
    tm = TM_COMB
    nt = n // tm
    row = lambda w: pl.BlockSpec((tm, w), lambda i: (i, 0))
    dest3 = dest.reshape(nt, 1, 2 * tm)
    return pl.pallas_call(
        _combine_kernel,
        grid=(nt,),
        in_specs=[pl.BlockSpec((None, 1, 2 * tm), lambda i: (i, 0, 0), memory_space=pltpu.SMEM),
                  pl.BlockSpec((None, 1, 2 * tm), lambda i: (jnp.minimum(i + 1, nt - 1), 0, 0),
                               memory_space=pltpu.SMEM),
                  pl.BlockSpec(memory_space=pl.ANY), row(D_MODEL), row(LANES),
                  _const_spec((1, D_MODEL)), _const_spec((1, D_MODEL))],
        out_specs=row(D_MODEL),
        out_shape=jax.ShapeDtypeStruct((n, D_MODEL), F32),
        scratch_shapes=[pltpu.VMEM((2, 2, tm, D_MODEL), F32), pltpu.SemaphoreType.DMA((2,))],
        compiler_params=_cparams(("arbitrary",), disable_bounds_checks=True),
        name="moe_combine_ln",
    )(dest3, dest3, yb, x2d, wts, g.reshape(1, -1), b.reshape(1, -1))


def _moe(x2d, router, w13, w2, g, b):
    n = x2d.shape[0]
    tb = MOE_TB
    info, wts, cnt = _router(x2d, router)
    idx = info[:, 0:2]
    rank = info[:, 2:4]
    counts = cnt[0, :N_EXPERTS].astype(I32)
    padded = (counts + tb - 1) // tb * tb
    pend = jnp.cumsum(padded)
    pstart = pend - padded
    dest = (pstart[idx] + rank).astype(I32).reshape(-1)
    nblk = (2 * n) // tb + N_EXPERTS
    pads = jnp.stack([jnp.append(pstart + counts, pend[-1]), jnp.append(pend, nblk * tb)]).astype(I32)
    nused = (pend[-1] // tb).astype(I32).reshape(1)
    blk_exp = jnp.minimum(jnp.searchsorted(pend, jnp.arange(nblk, dtype=I32) * tb, side='right'),
                          N_EXPERTS - 1).astype(I32)
    xb = _dispatch(x2d, dest, pads, nblk)
    yb = _experts(xb, w13, w2, nused, blk_exp, nblk)
    return _combine(yb, dest, x2d, wts, g, b)


def kernel(x, mem, positions, rel_bias_table, hgrn_lb_logits, w_in, mla_q_norm, mla_w_uq, mla_kv_norm, mla_w_ukv, swa_sinks, hgrn_norm, w_branch, w_out, ln_g, ln_b, xa_wq, xa_wkv, xa_wo, ffn_w13, ffn_w2, moe_router, moe_w13, moe_w2):
    batch, seq, _ = x.shape
    n = batch * seq
    sm = jax.nn.softmax(hgrn_lb_logits.astype(F32), axis=0)
    lower_bounds = jnp.cumsum(sm, axis=0) - sm[0]
    ctab, stab = _rope_tables(positions)
    xc = x.reshape(n, D_MODEL)
    for l in range(DEPTH):
        wts = _inproj_weights(w_in[l], mla_w_uq[l], mla_w_ukv[l])
        mq, mk, mv, swq, swk, swv, hg, sbq, sbk, sbv = _inproj(xc, wts, ctab, stab, mla_q_norm[l], mla_kv_norm[l])
        y_mla = _mla_attention(mq, mk, mv, batch, seq)
        y_swa = _swa_attention(swq, swk, swv, positions, swa_sinks[l], rel_bias_table, batch, seq)
        y_hg = _hgrn(hg, lower_bounds[l], hgrn_norm[l], batch, seq)
        y_sb = _sb_attention(sbq, sbk, sbv, batch, seq)
        go = _IN_OFF['gates']
        xc = _merge(xc, (y_mla, y_swa, y_hg, y_sb), w_in[l][:, go:].astype(BF16), w_branch[l].astype(BF16),
                    w_out[l].astype(BF16), ln_g[l, 0], ln_b[l, 0])
        mk_, mv_ = _memkv(mem, xa_wkv[l].astype(BF16))
        xc = _xattn(xc, (xa_wq[l] * QK_SCALE).astype(BF16), mk_, mv_, xa_wo[l].astype(BF16),
                    ln_g[l, 1], ln_b[l, 1], batch, seq)
        if l % 2 == 0:
            xc = _ffn(xc, ffn_w13[l // 2].astype(BF16), ffn_w2[l // 2].astype(BF16), ln_g[l, 2], ln_b[l, 2])
        else:
            xc = _moe(xc, moe_router[l // 2], moe_w13[l // 2].astype(BF16), moe_w2[l // 2].astype(BF16),
                      ln_g[l, 2], ln_b[l, 2])
    return xc.reshape(batch, seq, D_MODEL)
```

```python
import functools
import math

import jax
import jax.numpy as jnp
from jax import lax
from jax.experimental import pallas as pl
from jax.experimental.pallas import tpu as pltpu

F32 = jnp.float32
BF16 = jnp.bfloat16
I32 = jnp.int32

D_MODEL = 1024
DEPTH = 2
EPS = 1e-5
NEG_BIG = -1e30
LANES = 128
HEAD_DIM = 64
N_HEADS = 4
WIDTH = N_HEADS * HEAD_DIM

MLA_Q_LORA = 256
MLA_KV_LORA = 128
MLA_NOPE = 64
MLA_ROPE = 32
ROPE_THETA = 10000.0
MLA_SCALE = (MLA_NOPE + MLA_ROPE) ** -0.5
LOG2E = math.log2(math.e)
QK_SCALE = HEAD_DIM ** -0.5

SB_RUN_FLOOR = -104.0
SWA_WINDOW = 128
REL_BUCKETS = 32
REL_MAX_DIST = 128
HGRN_CHUNK = 64
N_EXPERTS = 8
F_DENSE = 2816
F_EXPERT = 3584
ALPHA = (2 * DEPTH) ** 0.25

_IN_SPLITS = (('mla_cq', 256), ('mla_ckv', 128), ('mla_kr', 32), ('swa_q', 256), ('swa_k', 128),
              ('swa_v', 128), ('hgrn', 1024), ('sb_q', 256), ('sb_k', 256), ('sb_v', 256), ('gates', 4096))
_IN_OFF = {}
_o = 0
for _n, _w in _IN_SPLITS:
    _IN_OFF[_n] = _o
    _o += _w

_A_SPLITS = (('cq', 256), ('ckv', 128), ('kra', 128), ('krb', 128), ('swa_q', 256), ('swa_k', 256),
             ('swa_v', 256), ('hgrn', 1024), ('sb_q', 256), ('sb_k', 256), ('sb_v', 256))
_A_OFF = {}
_o = 0
for _n, _w in _A_SPLITS:
    _A_OFF[_n] = (_o, _o + _w)
    _o += _w
A_COLS = _o

TM_A = 512
TQ_ATT = 256
MLA_TQ = 512
MLA_TK = 512
MLA_GROUP = 4
SWA_TQ = 512
HG_ROWS = 256
TM_FFN = 512
TF_FFN = 1408
MOE_TB = 512
MOE_TF = 1792
TM_COMB = 256
TM_DISP = 512
VMEM_LIMIT = 56 * 1024 * 1024


def _cparams(sem, **kw):
    return pltpu.CompilerParams(dimension_semantics=sem, vmem_limit_bytes=VMEM_LIMIT, **kw)


def _const_spec(shape):
    nd = len(shape)
    return pl.BlockSpec(shape, lambda *_: (0,) * nd)


def _layernorm(v, g, b):
    mu = jnp.mean(v, axis=-1, keepdims=True)
    vc = v - mu
    var = jnp.mean(vc * vc, axis=-1, keepdims=True)
    return vc * lax.rsqrt(var + EPS) * g + b


def _dot(a, b):
    return jnp.dot(a, b, preferred_element_type=F32)


def _dot_nt(a, b):
    return lax.dot_general(a, b, (((1,), (1,)), ((), ())), preferred_element_type=F32)


def _split3(a):
    hi = a.astype(BF16)
    r = a - hi.astype(F32)
    mid = r.astype(BF16)
    lo = (r - mid.astype(F32)).astype(BF16)
    return hi, mid, lo


def _rope_kernel(pos_ref, freq_ref, c_ref, s_ref):
    lane = lax.broadcasted_iota(I32, pos_ref.shape, 1)
    ang = pos_ref[...] * freq_ref[...]
    rope = (lane >= MLA_NOPE) & (lane < MLA_NOPE + MLA_ROPE)
    first = lane < MLA_NOPE + MLA_ROPE // 2
    c_ref[...] = jnp.where(lane < MLA_NOPE, 1.0, jnp.where(rope, jnp.cos(ang), 0.0))
    sn = jnp.sin(ang)
    s_ref[...] = jnp.where(rope, jnp.where(first, -sn, sn), 0.0)


def _rope_tables(positions):
    n = positions.size
    half = MLA_ROPE // 2
    inv_freq = ROPE_THETA ** (-jnp.arange(half, dtype=F32) / half)
    freq = jnp.zeros((1, LANES), F32).at[0, MLA_NOPE:MLA_NOPE + MLA_ROPE].set(jnp.tile(inv_freq, 2))
    posb = jnp.broadcast_to(positions.reshape(n, 1).astype(F32), (n, LANES))
    tm = 1024
    return pl.pallas_call(
        _rope_kernel,
        grid=(n // tm,),
        in_specs=[pl.BlockSpec((tm, LANES), lambda i: (i, 0)), _const_spec((1, LANES))],
        out_specs=[pl.BlockSpec((tm, LANES), lambda i: (i, 0))] * 2,
        out_shape=[jax.ShapeDtypeStruct((n, LANES), F32)] * 2,
        compiler_params=_cparams(("parallel",)),
        name="rope_tables",
    )(posb, freq)


def _inproj_kernel(x_ref, w_ref, c_ref, s_ref, qn_ref, kvn_ref, wuqa_ref, wuqb_ref, wuk_ref, wuv_ref,
                   mq_ref, mk_ref, mv_ref, swq_ref, swk_ref, swv_ref, hg_ref, sbq_ref, sbk_ref, sbv_ref):
    h = _dot(x_ref[...].astype(BF16), w_ref[...])

    def cols(name):
        lo, hi = _A_OFF[name]
        return h[:, lo:hi]

    c = c_ref[...]
    s = s_ref[...]
    c4 = jnp.concatenate([c] * N_HEADS, axis=1)
    s4 = jnp.concatenate([s] * N_HEADS, axis=1)

    cq = cols('cq')
    cqn = (cq * lax.rsqrt(jnp.mean(cq * cq, axis=-1, keepdims=True) + EPS) * qn_ref[...]).astype(BF16)
    q = _dot(cqn, wuqa_ref[...]) * c4 + _dot(cqn, wuqb_ref[...]) * s4
    mq_ref[...] = (q * (MLA_SCALE * LOG2E)).astype(BF16)

    ckv = cols('ckv')
    ckvn = (ckv * lax.rsqrt(jnp.mean(ckv * ckv, axis=-1, keepdims=True) + EPS) * kvn_ref[...]).astype(BF16)
    krot = cols('kra') * c + cols('krb') * s
    mk_ref[...] = (_dot(ckvn, wuk_ref[...]) + jnp.concatenate([krot] * N_HEADS, axis=1)).astype(BF16)
    mv_ref[...] = _dot(ckvn, wuv_ref[...]).astype(BF16)

    swq_ref[...] = cols('swa_q').astype(BF16)
    swk_ref[...] = cols('swa_k').astype(BF16)
    swv_ref[...] = cols('swa_v').astype(BF16)
    hg_ref[...] = cols('hgrn')
    sbq_ref[...] = cols('sb_q').astype(BF16)
    sbk_ref[...] = cols('sb_k').astype(BF16)
    sbv_ref[...] = cols('sb_v').astype(BF16)


def _inproj_weights(w_in, w_uq, w_ukv):
    def seg(name, width):
        o = _IN_OFF[name]
        return w_in[:, o:o + width]

    kr = seg('mla_kr', MLA_ROPE)
    half = MLA_ROPE // 2
    z64 = jnp.zeros((D_MODEL, MLA_NOPE), F32)
    z32 = jnp.zeros((D_MODEL, LANES - MLA_NOPE - MLA_ROPE), F32)
    kra = jnp.concatenate([z64, kr, z32], axis=1)
    krb = jnp.concatenate([z64, kr[:, half:], kr[:, :half], z32], axis=1)
    swk = seg('swa_k', 128)
    swv = seg('swa_v', 128)
    dup = lambda t: jnp.concatenate([t[:, :64], t[:, :64], t[:, 64:], t[:, 64:]], axis=1)
    w_a = jnp.concatenate([
        seg('mla_cq', 256), seg('mla_ckv', 128), kra, krb,
        seg('swa_q', 256) * QK_SCALE, dup(swk), dup(swv),
        seg('hgrn', 1024), seg('sb_q', 256) * QK_SCALE, seg('sb_k', 256), seg('sb_v', 256)], axis=1)

    qd = MLA_NOPE + MLA_ROPE
    zq = jnp.zeros((MLA_Q_LORA, LANES - qd), F32)
    zn = jnp.zeros((MLA_Q_LORA, MLA_NOPE), F32)
    qa, qb = [], []
    for hh in range(N_HEADS):
        nope = w_uq[:, hh * qd: hh * qd + MLA_NOPE]
        rope = w_uq[:, hh * qd + MLA_NOPE: (hh + 1) * qd]
        qa += [nope, rope, zq]
        qb += [zn, rope[:, half:], rope[:, :half], zq]
    wuqa = jnp.concatenate(qa, axis=1)
    wuqb = jnp.concatenate(qb, axis=1)
    lane = jnp.arange(N_HEADS * LANES) % LANES
    wuk = jnp.where(lane[None, :] < MLA_NOPE, w_ukv, 0.0)
    wuv = jnp.concatenate([w_ukv[:, hh * LANES + MLA_NOPE:(hh + 1) * LANES] for hh in range(N_HEADS)], axis=1)
    return tuple(t.astype(BF16) for t in (w_a, wuqa, wuqb, wuk, wuv))


def _inproj(x2d, wts, ctab, stab, q_norm, kv_norm):
    n = x2d.shape[0]
    w_a, wuqa, wuqb, wuk, wuv = wts
    tm = TM_A
    row = lambda w: pl.BlockSpec((tm, w), lambda i: (i, 0))
    out_w = (512, 512, 256, 256, 256, 256, 1024, 256, 256, 256)
    out_dt = (BF16, BF16, BF16, BF16, BF16, BF16, F32, BF16, BF16, BF16)
    return pl.pallas_call(
        _inproj_kernel,
        grid=(n // tm,),
        in_specs=[row(D_MODEL), _const_spec(w_a.shape), row(LANES), row(LANES),
                  _const_spec((1, MLA_Q_LORA)), _const_spec((1, MLA_KV_LORA)),
                  _const_spec(wuqa.shape), _const_spec(wuqb.shape), _const_spec(wuk.shape),
                  _const_spec(wuv.shape)],
        out_specs=[row(w) for w in out_w],
        out_shape=[jax.ShapeDtypeStruct((n, w), d) for w, d in zip(out_w, out_dt)],
        compiler_params=_cparams(("parallel",)),
        name="inproj",
    )(x2d, w_a, ctab, stab, q_norm.reshape(1, -1), kv_norm.reshape(1, -1), wuqa, wuqb, wuk, wuv)


def _half_mask(half):
    lane = lax.broadcasted_iota(I32, (1, LANES), 1)
    return (lane < HEAD_DIM) if half == 0 else (lane >= HEAD_DIM)


def _mla_kernel(q_ref, k_ref, v_ref, o_ref):
    tq = q_ref.shape[0]
    tk = MLA_TK
    nsub = tq // tk
    i = pl.program_id(1)
    row = lax.broadcasted_iota(I32, (tq, tk), 0)
    col = lax.broadcasted_iota(I32, (tq, tk), 1)
    ones = jnp.ones((1, LANES), BF16)

    def update(off, carry, heads, mask):
        ss = [_dot_nt(q_ref[:, hh * LANES:(hh + 1) * LANES], k_ref[pl.ds(off, tk), hh * LANES:(hh + 1) * LANES])
              for hh in heads]
        if mask is not None:
            ss = [jnp.where(mask, s, NEG_BIG) for s in ss]
        ms = [jnp.maximum(c[0], jnp.max(s, axis=-1, keepdims=True)) for c, s in zip(carry, ss)]
        pms = [jnp.exp2(s - m).astype(BF16) for s, m in zip(ss, ms)]
        new = []
        for n, hh in enumerate(heads):
            vb = v_ref[pl.ds(off, tk), (hh // 2) * LANES:(hh // 2 + 1) * LANES]
            vb = jnp.where(_half_mask(hh % 2), vb, ones)
            m, acc = carry[n]
            new.append((ms[n], jnp.exp2(m - ms[n]) * acc + _dot(pms[n], vb)))
        return tuple(new)

    accs = []
    for g in range(0, N_HEADS, MLA_GROUP):
        heads = tuple(range(g, g + MLA_GROUP))
        init = tuple((jnp.full((tq, 1), NEG_BIG, F32), jnp.zeros((tq, LANES), F32)) for _ in heads)
        carry = lax.fori_loop(0, i * nsub,
                              lambda j, c, heads=heads: update(pl.multiple_of(j * tk, tk), c, heads, None), init)
        for r in range(nsub):
            carry = update(pl.multiple_of(i * tq + r * tk, tk), carry, heads, col + r * tk <= row)
        accs += [c[1] for c in carry]
    outs = []
    for p in range(N_HEADS // 2):
        a0, a1 = accs[2 * p], accs[2 * p + 1]
        outs.append(jnp.where(_half_mask(0), a0 / a0[:, HEAD_DIM:HEAD_DIM + 1], a1 / a1[:, 0:1]))
    o_ref[...] = jnp.concatenate(outs, axis=1).astype(o_ref.dtype)


def _mla_attention(q, k, v, batch, seq):
    tq = MLA_TQ
    q3, k3, v3 = (t.reshape(batch, seq, t.shape[-1]) for t in (q, k, v))
    out = pl.pallas_call(
        _mla_kernel,
        grid=(batch, seq // tq),
        in_specs=[pl.BlockSpec((None, tq, 512), lambda b, i: (b, i, 0)),
                  pl.BlockSpec((None, seq, 512), lambda b, i: (b, 0, 0)),
                  pl.BlockSpec((None, seq, WIDTH), lambda b, i: (b, 0, 0))],
        out_specs=pl.BlockSpec((None, tq, WIDTH), lambda b, i: (b, i, 0)),
        out_shape=jax.ShapeDtypeStruct((batch, seq, WIDTH), BF16),
        compiler_params=_cparams(("parallel", "arbitrary")),
        name="mla_attention",
    )(q3, k3, v3)
    return out.reshape(batch * seq, WIDTH)


def _sb_kernel(q_ref, k_ref, v_ref, o_ref):
    tq = q_ref.shape[0]
    i = pl.program_id(1)
    row = lax.broadcasted_iota(I32, (tq, tq), 0)
    col = lax.broadcasted_iota(I32, (tq, tq), 1)
    strict = col < row
    later = (row > col).astype(BF16)
    qs = []
    for hh in range(N_HEADS):
        qp = q_ref[:, (hh // 2) * LANES:(hh // 2 + 1) * LANES]
        qs.append(jnp.where(_half_mask(hh % 2), qp, jnp.zeros_like(qp)))

    def block(j, carry, diag):
        off = pl.multiple_of(j * tq, tq)
        runs, accs = carry
        heads = range(N_HEADS)
        zs = [_dot_nt(qs[hh], k_ref[pl.ds(off, tq), (hh // 2) * LANES:(hh // 2 + 1) * LANES]) for hh in heads]
        lsps = [jnp.minimum(z, 0.0) - jnp.log(1.0 + jnp.exp(-jnp.abs(z))) for z in zs]
        lsns = [lsp - z for lsp, z in zip(lsps, zs)]
        if diag:
            lsns = [jnp.where(strict, t, 0.0) for t in lsns]
        his = [t.astype(BF16) for t in lsns]
        los = [(t - hi.astype(F32)).astype(BF16) for t, hi in zip(lsns, his)]
        rems = [_dot(hi, later) + _dot(lo, later) for hi, lo in zip(his, los)]
        args = [lsps[hh] + rems[hh] + runs[hh] for hh in heads]
        if diag:
            args = [jnp.where(strict, t, NEG_BIG) for t in args]
        probs = [jnp.exp(t).astype(BF16) for t in args]
        new_runs = tuple(runs[hh] + rems[hh][:, 0:1] + lsns[hh][:, 0:1] for hh in heads)
        new_accs = list(accs)
        for hh in heads:
            p = hh // 2
            vb = v_ref[pl.ds(off, tq), p * LANES:(p + 1) * LANES]
            vb = jnp.where(_half_mask(hh % 2), vb, jnp.zeros_like(vb))
            new_accs[p] = new_accs[p] + _dot(probs[hh], vb)
        return new_runs, tuple(new_accs)

    init = (tuple(jnp.zeros((tq, 1), F32) for _ in range(N_HEADS)),
            tuple(jnp.zeros((tq, LANES), F32) for _ in range(N_HEADS // 2)))
    def still_active(runs):
        top = functools.reduce(jnp.maximum, runs)
        return (jnp.max(top) > SB_RUN_FLOOR).astype(I32)

    runs, accs = block(i, init, True)

    def cond(c):
        return (c[0] < i) & (c[1] > 0)

    def body(c):
        jj, _, runs, accs = c
        runs, accs = block(i - 1 - jj, (runs, accs), False)
        return jj + 1, still_active(runs), runs, accs

    _, _, _, accs = lax.while_loop(cond, body, (jnp.int32(0), still_active(runs), runs, accs))
    o_ref[...] = jnp.concatenate(accs, axis=1).astype(o_ref.dtype)


def _sb_attention(q, k, v, batch, seq):
    tq = TQ_ATT
    q3, k3, v3 = (t.reshape(batch, seq, WIDTH) for t in (q, k, v))
    out = pl.pallas_call(
        _sb_kernel,
        grid=(batch, seq // tq),
        in_specs=[pl.BlockSpec((None, tq, WIDTH), lambda b, i: (b, i, 0)),
                  pl.BlockSpec((None, seq, WIDTH), lambda b, i: (b, 0, 0)),
                  pl.BlockSpec((None, seq, WIDTH), lambda b, i: (b, 0, 0))],
        out_specs=pl.BlockSpec((None, tq, WIDTH), lambda b, i: (b, i, 0)),
        out_shape=jax.ShapeDtypeStruct((batch, seq, WIDTH), BF16),
        compiler_params=_cparams(("parallel", "arbitrary")),
        name="stick_breaking",
    )(q3, k3, v3)
    return out.reshape(batch * seq, WIDTH)


def _rel_bucket(dist):
    exact = REL_BUCKETS // 2
    n = jnp.maximum(dist, 0)
    nf = jnp.maximum(n, 1).astype(F32)
    large = exact + (jnp.log(nf / exact) / math.log(REL_MAX_DIST / exact) * (REL_BUCKETS - exact)).astype(I32)
    large = jnp.clip(large, 0, REL_BUCKETS - 1)
    return jnp.where(n < exact, n, large)


def _swa_kernel(sink_ref, tab_ref, q_ref, kc_ref, kh_ref, vc_ref, vh_ref, pq_ref, pkc_ref, pkh_ref, o_ref):
    w = SWA_WINDOW
    step = pl.program_id(1)
    row = lax.broadcasted_iota(I32, (w, w), 0)
    col = lax.broadcasted_iota(I32, (w, w), 1)
    valid_c = col <= row
    valid_p = col > row
    tabs = [jnp.broadcast_to(tab_ref[hh:hh + 1, :], (w, LANES)) for hh in range(N_HEADS)]
    ones = jnp.ones((1, LANES), BF16)
    nsub = q_ref.shape[0] // w
    chains = [(r, hh) for r in range(nsub) for hh in range(N_HEADS)]

    def keys(ref, halo_ref, r, hh):
        sl = slice((hh // 2) * LANES, (hh // 2 + 1) * LANES)
        cur = ref[r * w:(r + 1) * w, sl]
        prev = ref[(r - 1) * w:r * w, sl] if r else halo_ref[:, sl]
        return cur, prev

    buckets = []
    for r in range(nsub):
        pq = pq_ref[r * w:(r + 1) * w, :]
        pk_prev = pkc_ref[:, (r - 1) * w:r * w] if r else pkh_ref[...]
        buckets.append((_rel_bucket(pq - pkc_ref[:, r * w:(r + 1) * w]), _rel_bucket(pq - pk_prev)))
    logits = []
    for r, hh in chains:
        qp = q_ref[r * w:(r + 1) * w, (hh // 2) * LANES:(hh // 2 + 1) * LANES]
        qh = jnp.where(_half_mask(hh % 2), qp, jnp.zeros_like(qp))
        kc, kp = keys(kc_ref, kh_ref, r, hh)
        logits.append((_dot_nt(qh, kc), _dot_nt(qh, kp)))
    masked = []
    for (r, hh), (lc, lp) in zip(chains, logits):
        lc = jnp.where(valid_c, lc + jnp.take_along_axis(tabs[hh], buckets[r][0], axis=1), NEG_BIG)
        lp = lp + jnp.take_along_axis(tabs[hh], buckets[r][1], axis=1)
        lp = jnp.where(valid_p if r else valid_p & (step > 0), lp, NEG_BIG)
        masked.append((lc, lp))
    maxes = [jnp.maximum(jnp.maximum(jnp.max(lc, axis=-1, keepdims=True), jnp.max(lp, axis=-1, keepdims=True)),
                         sink_ref[hh]) for (r, hh), (lc, lp) in zip(chains, masked)]
    probs = [(jnp.exp(lc - m).astype(BF16), jnp.exp(lp - m).astype(BF16)) for (lc, lp), m in zip(masked, maxes)]
    outs = {}
    for (r, hh), (ec, ep), m in zip(chains, probs, maxes):
        vc, vp = keys(vc_ref, vh_ref, r, hh)
        mine = _half_mask(hh % 2)
        acc = _dot(ec, jnp.where(mine, vc, ones)) + _dot(ep, jnp.where(mine, vp, ones))
        den = (acc[:, 0:1] if hh % 2 else acc[:, HEAD_DIM:HEAD_DIM + 1]) + jnp.exp(sink_ref[hh] - m)
        outs[(r, hh)] = acc / den
    for r in range(nsub):
        pairs = [jnp.where(_half_mask(0), outs[(r, 2 * p)], outs[(r, 2 * p + 1)]) for p in range(N_HEADS // 2)]
        o_ref[r * w:(r + 1) * w, :] = jnp.concatenate(pairs, axis=1).astype(o_ref.dtype)


def _swa_attention(q, k, v, positions, sinks, rel_table, batch, seq):
    w = SWA_WINDOW
    tq = SWA_TQ
    per = tq // w
    q3, k3, v3 = (t.reshape(batch, seq, WIDTH) for t in (q, k, v))
    pcol = positions.reshape(batch, seq, 1)
    prow = positions.reshape(batch, 1, seq)
    tab = jnp.zeros((N_HEADS, LANES), F32).at[:, :REL_BUCKETS].set(rel_table.astype(F32).T)
    cur = lambda b, n: (b, n, 0)
    halo = lambda b, n: (b, jnp.maximum(n * per - 1, 0), 0)
    out = pl.pallas_call(
        _swa_kernel,
        grid=(batch, seq // tq),
        in_specs=[pl.BlockSpec(memory_space=pltpu.SMEM), _const_spec((N_HEADS, LANES)),
                  pl.BlockSpec((None, tq, WIDTH), cur),
                  pl.BlockSpec((None, tq, WIDTH), cur), pl.BlockSpec((None, w, WIDTH), halo),
                  pl.BlockSpec((None, tq, WIDTH), cur), pl.BlockSpec((None, w, WIDTH), halo),
                  pl.BlockSpec((None, tq, 1), cur),
                  pl.BlockSpec((None, 1, tq), lambda b, n: (b, 0, n)),
                  pl.BlockSpec((None, 1, w), lambda b, n: (b, 0, jnp.maximum(n * per - 1, 0)))],
        out_specs=pl.BlockSpec((None, tq, WIDTH), cur),
        out_shape=jax.ShapeDtypeStruct((batch, seq, WIDTH), BF16),
        compiler_params=_cparams(("parallel", "arbitrary")),
        name="swa_attention",
    )(sinks.astype(F32), tab, q3, k3, k3, v3, v3, pcol, prow, prow)
    return out.reshape(batch * seq, WIDTH)


def _hgrn_kernel(hg_ref, lb_ref, nw_ref, o_ref, state_ref, w_ref):
    c = HGRN_CHUNK
    @pl.when(pl.program_id(1) == 0)
    def _():
        state_ref[...] = jnp.zeros_like(state_ref)

    r64 = lax.broadcasted_iota(I32, (c, c), 0)
    c64 = lax.broadcasted_iota(I32, (c, c), 1)
    incl = (c64 <= r64).astype(BF16)
    ra = lax.broadcasted_iota(I32, (WIDTH, WIDTH), 0) // HEAD_DIM
    ca = lax.broadcasted_iota(I32, (WIDTH, WIDTH), 1) // HEAD_DIM
    same_head = ra == ca
    seg = same_head.astype(BF16)
    ones_cols = jnp.ones((c, LANES), BF16)
    trow = lax.broadcasted_iota(I32, (c, WIDTH), 0)
    lb = lb_ref[...]
    nw = nw_ref[...]
    group = 8

    for ch in range(hg_ref.shape[0] // c):
        rows = slice(ch * c, (ch + 1) * c)
        qraw = hg_ref[rows, 0:WIDTH]
        fraw = hg_ref[rows, WIDTH:2 * WIDTH]
        v = hg_ref[rows, 2 * WIDTH:3 * WIDTH]
        graw = hg_ref[rows, 3 * WIDTH:4 * WIDTH]
        qf = qraw * jax.nn.sigmoid(qraw)
        forget = lb + (1.0 - lb) * jax.nn.sigmoid(fraw)
        lf = jnp.log(forget)
        kk = 1.0 - forget
        gate = graw * jax.nn.sigmoid(graw)

        lf3 = _split3(lf)
        bc = _dot(incl, lf3[0]) + _dot(incl, lf3[1]) + _dot(incl, lf3[2])
        b_last = bc[c - 1:c, :]
        dn0 = (((0,), (0,)), ((), ()))
        tot_col = sum(lax.dot_general(t, ones_cols, dn0, preferred_element_type=F32) for t in lf3)
        decay_col = jnp.exp(jnp.concatenate([tot_col, tot_col], axis=1))

        state = state_ref[...]
        o = _dot((qf * jnp.exp(bc)).astype(BF16), state.astype(BF16))

        vb = v
        for g0 in range(0, c, group):
            for s_i in range(g0, g0 + group):
                e = jnp.exp(jnp.minimum(bc - bc[s_i:s_i + 1, :], 0.0))
                wgt = jnp.where(trow >= s_i, qf * kk[s_i:s_i + 1, :] * e, 0.0)
                w_ref[(s_i - g0) * c:(s_i - g0 + 1) * c, :] = wgt.astype(BF16)
            att = _dot(w_ref[...], seg)
            for s_i in range(g0, g0 + group):
                o = o + att[(s_i - g0) * c:(s_i - g0 + 1) * c, :] * vb[s_i:s_i + 1, :]

        khat = (kk * jnp.exp(b_last - bc)).astype(BF16)
        upd = lax.dot_general(khat, v.astype(BF16), dn0, preferred_element_type=F32)
        state_ref[...] = decay_col * state + jnp.where(same_head, upd, 0.0)

        o2 = _split3(o * o)
        ms = (_dot(o2[0], seg) + _dot(o2[1], seg)) * (1.0 / HEAD_DIM)
        o_ref[rows, :] = (o * lax.rsqrt(ms + EPS) * nw * gate).astype(o_ref.dtype)


def _hgrn(hg, lower_bound, norm_w, batch, seq):
    rows = HG_ROWS
    hg3 = hg.reshape(batch, seq, 4 * WIDTH)
    out = pl.pallas_call(
        _hgrn_kernel,
        grid=(batch, seq // rows),
        in_specs=[pl.BlockSpec((None, rows, 4 * WIDTH), lambda b, i: (b, i, 0)),
                  _const_spec((1, WIDTH)), _const_spec((1, WIDTH))],
        out_specs=pl.BlockSpec((None, rows, WIDTH), lambda b, i: (b, i, 0)),
        out_shape=jax.ShapeDtypeStruct((batch, seq, WIDTH), BF16),
        scratch_shapes=[pltpu.VMEM((WIDTH, WIDTH), F32), pltpu.VMEM((8 * HGRN_CHUNK, WIDTH), BF16)],
        compiler_params=_cparams(("parallel", "arbitrary")),
        name="hgrn2",
    )(hg3, lower_bound.reshape(1, WIDTH).astype(F32), norm_w.reshape(1, WIDTH).astype(F32))
    return out.reshape(batch * seq, WIDTH)


def _merge_kernel(x_ref, y0_ref, y1_ref, y2_ref, y3_ref, wg_ref, wb_ref, wo_ref, g_ref, b_ref, o_ref):
    x = x_ref[...]
    xb = x.astype(BF16)
    merged = jnp.zeros(x.shape, F32)
    for nbr, y_ref in enumerate((y0_ref, y1_ref, y2_ref, y3_ref)):
        gate = jax.nn.sigmoid(_dot(xb, wg_ref[:, nbr * D_MODEL:(nbr + 1) * D_MODEL]))
        merged = merged + gate * _dot(y_ref[...], wb_ref[nbr])
    y = _dot(merged.astype(BF16), wo_ref[...])
    o_ref[...] = _layernorm(ALPHA * x + y, g_ref[...], b_ref[...])


def _merge(x2d, ys, wg, wb, wo, g, b):
    n = x2d.shape[0]
    tm = TM_A
    row = lambda w: pl.BlockSpec((tm, w), lambda i: (i, 0))
    return pl.pallas_call(
        _merge_kernel,
        grid=(n // tm,),
        in_specs=[row(D_MODEL)] + [row(WIDTH)] * 4 +
                 [_const_spec(wg.shape), _const_spec(wb.shape), _const_spec(wo.shape),
                  _const_spec((1, D_MODEL)), _const_spec((1, D_MODEL))],
        out_specs=row(D_MODEL),
        out_shape=jax.ShapeDtypeStruct((n, D_MODEL), F32),
        compiler_params=_cparams(("parallel",)),
        name="merge_outproj_ln",
    )(x2d, *ys, wg, wb, wo, g.reshape(1, -1), b.reshape(1, -1))


def _memkv_kernel(m_ref, w_ref, k_ref, v_ref):
    kv = _dot(m_ref[...].astype(BF16), w_ref[...])
    k_ref[...] = kv[:, :WIDTH].astype(BF16)
    v_ref[...] = kv[:, WIDTH:].astype(BF16)


def _memkv(mem, wkv):
    batch, m, _ = mem.shape
    return pl.pallas_call(
        _memkv_kernel,
        grid=(batch,),
        in_specs=[pl.BlockSpec((None, m, D_MODEL), lambda b: (b, 0, 0)), _const_spec(wkv.shape)],
        out_specs=[pl.BlockSpec((None, m, WIDTH), lambda b: (b, 0, 0))] * 2,
        out_shape=[jax.ShapeDtypeStruct((batch, m, WIDTH), BF16)] * 2,
        compiler_params=_cparams(("parallel",)),
        name="mem_kv",
    )(mem, wkv)


def _xattn_kernel(x_ref, wq_ref, k_ref, v_ref, wo_ref, g_ref, b_ref, o_ref):
    x = x_ref[...]
    q = _dot(x.astype(BF16), wq_ref[...]).astype(BF16)
    k = k_ref[...]
    v = v_ref[...]
    lane = lax.broadcasted_iota(I32, (1, WIDTH), 1) // HEAD_DIM
    o = jnp.zeros((x.shape[0], WIDTH), F32)
    for hh in range(N_HEADS):
        qh = jnp.where(lane == hh, q, jnp.zeros_like(q))
        s = _dot_nt(qh, k)
        e = jnp.exp(s - jnp.max(s, axis=-1, keepdims=True))
        p = e / jnp.sum(e, axis=-1, keepdims=True)
        o = o + jnp.where(lane == hh, _dot(p.astype(BF16), v), 0.0)
    y = _dot(o.astype(BF16), wo_ref[...])
    o_ref[...] = _layernorm(ALPHA * x + y, g_ref[...], b_ref[...])


def _xattn(x2d, wq, k, v, wo, g, b, batch, seq):
    tm = TM_A
    m = k.shape[1]
    x3 = x2d.reshape(batch, seq, D_MODEL)
    row = pl.BlockSpec((None, tm, D_MODEL), lambda bb, i: (bb, i, 0))
    kv_spec = pl.BlockSpec((None, m, WIDTH), lambda bb, i: (bb, 0, 0))
    out = pl.pallas_call(
        _xattn_kernel,
        grid=(batch, seq // tm),
        in_specs=[row, _const_spec(wq.shape), kv_spec, kv_spec, _const_spec(wo.shape),
                  _const_spec((1, D_MODEL)), _const_spec((1, D_MODEL))],
        out_specs=row,
        out_shape=jax.ShapeDtypeStruct((batch, seq, D_MODEL), F32),
        compiler_params=_cparams(("parallel", "parallel")),
        name="mem_xattn_ln",
    )(x3, wq, k, v, wo, g.reshape(1, -1), b.reshape(1, -1))
    return out.reshape(batch * seq, D_MODEL)


def _ffn_kernel(x_ref, w1_ref, w3_ref, w2_ref, g_ref, b_ref, o_ref, acc_ref):
    f = pl.program_id(1)
    xb = x_ref[...].astype(BF16)
    a = _dot(xb, w1_ref[...])
    gate = _dot(xb, w3_ref[...])
    part = _dot((a * jax.nn.sigmoid(a) * gate).astype(BF16), w2_ref[...])

    @pl.when(f == 0)
    def _():
        acc_ref[...] = part

    @pl.when(f > 0)
    def _():
        acc_ref[...] += part

    @pl.when(f == pl.num_programs(1) - 1)
    def _():
        o_ref[...] = _layernorm(ALPHA * x_ref[...] + acc_ref[...], g_ref[...], b_ref[...])


def _ffn(x2d, w13, w2, g, b):
    n = x2d.shape[0]
    tm, tf = TM_FFN, TF_FFN
    nf = F_DENSE // tf
    return pl.pallas_call(
        _ffn_kernel,
        grid=(n // tm, nf),
        in_specs=[pl.BlockSpec((tm, D_MODEL), lambda i, f: (i, 0)),
                  pl.BlockSpec((D_MODEL, tf), lambda i, f: (0, f)),
                  pl.BlockSpec((D_MODEL, tf), lambda i, f: (0, nf + f)),
                  pl.BlockSpec((tf, D_MODEL), lambda i, f: (f, 0)),
                  _const_spec((1, D_MODEL)), _const_spec((1, D_MODEL))],
        out_specs=pl.BlockSpec((tm, D_MODEL), lambda i, f: (i, 0)),
        out_shape=jax.ShapeDtypeStruct((n, D_MODEL), F32),
        scratch_shapes=[pltpu.VMEM((tm, D_MODEL), F32)],
        compiler_params=_cparams(("parallel", "arbitrary")),
        name="ffn_ln",
    )(x2d, w13, w13, w2, g.reshape(1, -1), b.reshape(1, -1))


def _router_kernel(x_ref, r_ref, info_ref, wts_ref, cnt_ref, carry_ref):
    tm = x_ref.shape[0]

    @pl.when(pl.program_id(0) == 0)
    def _():
        carry_ref[...] = jnp.zeros_like(carry_ref)

    logits = jnp.dot(x_ref[...], r_ref[...], precision=lax.Precision.HIGHEST, preferred_element_type=F32)
    lane = lax.broadcasted_iota(I32, (tm, LANES), 1)
    lg = jnp.where(lane < N_EXPERTS, logits, -jnp.inf)
    m1 = jnp.max(lg, axis=-1, keepdims=True)
    i1 = jnp.min(jnp.where(lg == m1, lane, LANES), axis=-1, keepdims=True)
    lg2 = jnp.where(lane == i1, -jnp.inf, lg)
    m2 = jnp.max(lg2, axis=-1, keepdims=True)
    i2 = jnp.min(jnp.where(lg2 == m2, lane, LANES), axis=-1, keepdims=True)
    e = jnp.exp(m2 - m1)
    w1 = 1.0 / (1.0 + e)
    w2 = e / (1.0 + e)
    sel1 = lane == i1
    sel2 = lane == i2
    chosen = jnp.where(sel1 | sel2, 1.0, 0.0)
    row = lax.broadcasted_iota(I32, (tm, tm), 0)
    col = lax.broadcasted_iota(I32, (tm, tm), 1)
    before = (col < row).astype(BF16)
    ranks = _dot(before, chosen.astype(BF16)) + carry_ref[...]
    r1 = jnp.sum(jnp.where(sel1, ranks, 0.0), axis=-1, keepdims=True).astype(I32)
    r2 = jnp.sum(jnp.where(sel2, ranks, 0.0), axis=-1, keepdims=True).astype(I32)
    carry_ref[...] = carry_ref[...] + jnp.sum(chosen, axis=0, keepdims=True)
    info_ref[...] = jnp.where(lane == 0, i1, jnp.where(lane == 1, i2, jnp.where(lane == 2, r1,
                              jnp.where(lane == 3, r2, 0))))
    wts_ref[...] = jnp.where(lane == 0, w1, jnp.where(lane == 1, w2, 0.0))
    cnt_ref[...] = carry_ref[...]


def _router(x2d, router):
    n = x2d.shape[0]
    tm = TM_A
    r_pad = jnp.zeros((D_MODEL, LANES), F32).at[:, :N_EXPERTS].set(router.astype(F32))
    row = pl.BlockSpec((tm, LANES), lambda i: (i, 0))
    return pl.pallas_call(
        _router_kernel,
        grid=(n // tm,),
        in_specs=[pl.BlockSpec((tm, D_MODEL), lambda i: (i, 0)), _const_spec(r_pad.shape)],
        out_specs=[row, row, _const_spec((1, LANES))],
        out_shape=[jax.ShapeDtypeStruct((n, LANES), I32), jax.ShapeDtypeStruct((n, LANES), F32),
                   jax.ShapeDtypeStruct((1, LANES), F32)],
        scratch_shapes=[pltpu.VMEM((1, LANES), F32)],
        compiler_params=_cparams(("arbitrary",)),
        name="moe_router",
    )(x2d, r_pad)


def _dispatch_kernel(pad_ref, dest_ref, x_ref, xb_hbm, stage_ref, sems):
    tm = x_ref.shape[0]
    i = pl.program_id(0)
    last = pl.num_programs(0) - 1
    slot = i % 2

    def wait_step(s):
        for _ in range(2):
            pltpu.make_async_copy(stage_ref.at[s], xb_hbm.at[pl.ds(0, tm), :], sems.at[s]).wait()

    @pl.when(i >= 2)
    def _():
        wait_step(slot)

    stage_ref[slot] = x_ref[...]

    def issue(r, c):
        for k in range(2):
            pltpu.make_async_copy(stage_ref.at[slot, pl.ds(r, 1), :],
                                  xb_hbm.at[pl.ds(dest_ref[0, 2 * r + k], 1), :], sems.at[slot]).start()
        return c
    lax.fori_loop(0, tm, issue, 0)

    @pl.when(i == last)
    def _():
        def fill(e, c):
            def one(s, c2):
                pltpu.make_async_copy(stage_ref.at[slot, pl.ds(0, 1), :], xb_hbm.at[pl.ds(s, 1), :],
                                      sems.at[2]).start()
                return c2

            def done(s, c2):
                pltpu.make_async_copy(stage_ref.at[slot, pl.ds(0, 1), :], xb_hbm.at[pl.ds(0, 1), :],
                                      sems.at[2]).wait()
                return c2
            lax.fori_loop(pad_ref[0, e], pad_ref[1, e], one, 0)
            lax.fori_loop(pad_ref[0, e], pad_ref[1, e], done, 0)
            return c
        lax.fori_loop(0, pad_ref.shape[1], fill, 0)
        wait_step(slot)

        @pl.when(last >= 1)
        def _():
            wait_step(1 - slot)


def _dispatch(x2d, dest, pads, nblk):
    n = x2d.shape[0]
    tm = TM_DISP
    nt = n // tm
    grid_spec = pltpu.PrefetchScalarGridSpec(
        num_scalar_prefetch=1,
        grid=(nt,),
        in_specs=[pl.BlockSpec((None, 1, 2 * tm), lambda i, pads: (i, 0, 0), memory_space=pltpu.SMEM),
                  pl.BlockSpec((tm, D_MODEL), lambda i, pads: (i, 0))],
        out_specs=pl.BlockSpec(memory_space=pl.ANY),
        scratch_shapes=[pltpu.VMEM((2, tm, D_MODEL), F32), pltpu.SemaphoreType.DMA((3,))],
    )
    return pl.pallas_call(
        _dispatch_kernel,
        grid_spec=grid_spec,
        out_shape=jax.ShapeDtypeStruct((nblk * MOE_TB, D_MODEL), F32),
        compiler_params=_cparams(("arbitrary",), disable_bounds_checks=True),
        name="moe_dispatch",
    )(pads, dest.reshape(nt, 1, 2 * tm), x2d)


def _expert_kernel(nused_ref, bexp_ref, x_ref, w1_ref, w3_ref, w2_ref, o_ref, acc_ref):
    f = pl.program_id(1)

    @pl.when(pl.program_id(0) < nused_ref[0])
    def _():
        xb = x_ref[...].astype(BF16)
        a = _dot(xb, w1_ref[...])
        gate = _dot(xb, w3_ref[...])
        part = _dot((a * jax.nn.sigmoid(a) * gate).astype(BF16), w2_ref[...])

        @pl.when(f == 0)
        def _():
            acc_ref[...] = part

        @pl.when(f > 0)
        def _():
            acc_ref[...] += part

        @pl.when(f == pl.num_programs(1) - 1)
        def _():
            o_ref[...] = acc_ref[...]

    @pl.when(pl.program_id(0) >= nused_ref[0])
    def _():
        o_ref[...] = jnp.zeros_like(o_ref)


def _experts(xb, w13, w2, nused, blk_exp, nblk):
    tb, tf = MOE_TB, MOE_TF
    nf = F_EXPERT // tf

    def blk(i, nu):
        return jnp.maximum(jnp.minimum(i, nu[0] - 1), 0)

    def ftile(i, f, nu):
        return jnp.where(i < nu[0], f, nf - 1)

    grid_spec = pltpu.PrefetchScalarGridSpec(
        num_scalar_prefetch=2,
        grid=(nblk, nf),
        in_specs=[pl.BlockSpec((tb, D_MODEL), lambda i, f, nu, be: (blk(i, nu), 0)),
                  pl.BlockSpec((None, D_MODEL, tf), lambda i, f, nu, be: (be[blk(i, nu)], 0, ftile(i, f, nu))),
                  pl.BlockSpec((None, D_MODEL, tf), lambda i, f, nu, be: (be[blk(i, nu)], 0, nf + ftile(i, f, nu))),
                  pl.BlockSpec((None, tf, D_MODEL), lambda i, f, nu, be: (be[blk(i, nu)], ftile(i, f, nu), 0))],
        out_specs=pl.BlockSpec((tb, D_MODEL), lambda i, f, nu, be: (i, 0)),
        scratch_shapes=[pltpu.VMEM((tb, D_MODEL), F32)],
    )
    return pl.pallas_call(
        _expert_kernel,
        grid_spec=grid_spec,
        out_shape=jax.ShapeDtypeStruct((nblk * tb, D_MODEL), F32),
        compiler_params=_cparams(("arbitrary", "arbitrary")),
        name="moe_experts",
    )(nused, blk_exp, xb, w13, w13, w2)


def _combine_kernel(dest_ref, nxt_ref, y_hbm, x_ref, wts_ref, g_ref, b_ref, o_ref, buf_ref, sems):
    tm = x_ref.shape[0]
    i = pl.program_id(0)
    slot = i % 2

    def gather(idx_ref, s):
        def issue(r, c):
            for k in range(2):
                pltpu.make_async_copy(y_hbm.at[pl.ds(idx_ref[0, 2 * r + k], 1), :],
                                      buf_ref.at[s, k, pl.ds(r, 1), :], sems.at[s]).start()
            return c
        lax.fori_loop(0, tm, issue, 0)

    @pl.when(i == 0)
    def _():
        gather(dest_ref, slot)

    @pl.when(i + 1 < pl.num_programs(0))
    def _():
        gather(nxt_ref, 1 - slot)

    for k in range(2):
        pltpu.make_async_copy(y_hbm.at[pl.ds(0, tm), :], buf_ref.at[slot, k], sems.at[slot]).wait()
    wts = wts_ref[...]
    y = wts[:, 0:1] * buf_ref[slot, 0] + wts[:, 1:2] * buf_ref[slot, 1]
    o_ref[...] = _layernorm(ALPHA * x_ref[...] + y, g_ref[...], b_ref[...])


def _combine(yb, dest, x2d, wts, g, b):
    n = x2d.shape[0]
    tm = TM_COMB
    nt = n // tm
    row = lambda w: pl.BlockSpec((tm, w), lambda i: (i, 0))
    dest3 = dest.reshape(nt, 1, 2 * tm)
    return pl.pallas_call(
        _combine_kernel,
        grid=(nt,),
        in_specs=[pl.BlockSpec((None, 1, 2 * tm), lambda i: (i, 0, 0), memory_space=pltpu.SMEM),
                  pl.BlockSpec((None, 1, 2 * tm), lambda i: (jnp.minimum(i + 1, nt - 1), 0, 0),
                               memory_space=pltpu.SMEM),
                  pl.BlockSpec(memory_space=pl.ANY), row(D_MODEL), row(LANES),
                  _const_spec((1, D_MODEL)), _const_spec((1, D_MODEL))],
        out_specs=row(D_MODEL),
        out_shape=jax.ShapeDtypeStruct((n, D_MODEL), F32),
        scratch_shapes=[pltpu.VMEM((2, 2, tm, D_MODEL), F32), pltpu.SemaphoreType.DMA((2,))],
        compiler_params=_cparams(("arbitrary",), disable_bounds_checks=True),
        name="moe_combine_ln",
    )(dest3, dest3, yb, x2d, wts, g.reshape(1, -1), b.reshape(1, -1))


def _moe(x2d, router, w13, w2, g, b):
    n = x2d.shape[0]
    tb = MOE_TB
    info, wts, cnt = _router(x2d, router)
    idx = info[:, 0:2]
    rank = info[:, 2:4]
    counts = cnt[0, :N_EXPERTS].astype(I32)
    padded = (counts + tb - 1) // tb * tb
    pend = jnp.cumsum(padded)
    pstart = pend - padded
    dest = (pstart[idx] + rank).astype(I32).reshape(-1)
    nblk = (2 * n) // tb + N_EXPERTS
    pads = jnp.stack([jnp.append(pstart + counts, pend[-1]), jnp.append(pend, nblk * tb)]).astype(I32)
    nused = (pend[-1] // tb).astype(I32).reshape(1)
    blk_exp = jnp.minimum(jnp.searchsorted(pend, jnp.arange(nblk, dtype=I32) * tb, side='right'),
                          N_EXPERTS - 1).astype(I32)
    xb = _dispatch(x2d, dest, pads, nblk)
    yb = _experts(xb, w13, w2, nused, blk_exp, nblk)
    return _combine(yb, dest, x2d, wts, g, b)


def kernel(x, mem, positions, rel_bias_table, hgrn_lb_logits, w_in, mla_q_norm, mla_w_uq, mla_kv_norm, mla_w_ukv, swa_sinks, hgrn_norm, w_branch, w_out, ln_g, ln_b, xa_wq, xa_wkv, xa_wo, ffn_w13, ffn_w2, moe_router, moe_w13, moe_w2):
    batch, seq, _ = x.shape
    n = batch * seq
    sm = jax.nn.softmax(hgrn_lb_logits.astype(F32), axis=0)
    lower_bounds = jnp.cumsum(sm, axis=0) - sm[0]
    ctab, stab = _rope_tables(positions)
    xc = x.reshape(n, D_MODEL)
    for l in range(DEPTH):
        wts = _inproj_weights(w_in[l], mla_w_uq[l], mla_w_ukv[l])
        mq, mk, mv, swq, swk, swv, hg, sbq, sbk, sbv = _inproj(xc, wts, ctab, stab, mla_q_norm[l], mla_kv_norm[l])
        y_mla = _mla_attention(mq, mk, mv, batch, seq)
        y_swa = _swa_attention(swq, swk, swv, positions, swa_sinks[l], rel_bias_table, batch, seq)
        y_hg = _hgrn(hg, lower_bounds[l], hgrn_norm[l], batch, seq)
        y_sb = _sb_attention(sbq, sbk, sbv, batch, seq)
        go = _IN_OFF['gates']
        xc = _merge(xc, (y_mla, y_swa, y_hg, y_sb), w_in[l][:, go:].astype(BF16), w_branch[l].astype(BF16),
                    w_out[l].astype(BF16), ln_g[l, 0], ln_b[l, 0])
        mk_, mv_ = _memkv(mem, xa_wkv[l].astype(BF16))
        xc = _xattn(xc, (xa_wq[l] * QK_SCALE).astype(BF16), mk_, mv_, xa_wo[l].astype(BF16),
                    ln_g[l, 1], ln_b[l, 1], batch, seq)
        if l % 2 == 0:
            xc = _ffn(xc, ffn_w13[l // 2].astype(BF16), ffn_w2[l // 2].astype(BF16), ln_g[l, 2], ln_b[l, 2])
        else:
            xc = _moe(xc, moe_router[l // 2], moe_w13[l // 2].astype(BF16), moe_w2[l // 2].astype(BF16),
                      ln_g[l, 2], ln_b[l, 2])
    return xc.reshape(batch, seq, D_MODEL)
```

```python
import functools
import math

import jax
import jax.numpy as jnp
from jax import lax
from jax.experimental import pallas as pl
from jax.experimental.pallas import tpu as pltpu

F32 = jnp.float32
BF16 = jnp.bfloat16
I32 = jnp.int32

D_MODEL = 1024
DEPTH = 2
EPS = 1e-5
NEG_BIG = -1e30
LANES = 128
HEAD_DIM = 64
N_HEADS = 4
WIDTH = N_HEADS * HEAD_DIM

MLA_Q_LORA = 256
MLA_KV_LORA = 128
MLA_NOPE = 64
MLA_ROPE = 32
ROPE_THETA = 10000.0
MLA_SCALE = (MLA_NOPE + MLA_ROPE) ** -0.5
LOG2E = math.log2(math.e)
QK_SCALE = HEAD_DIM ** -0.5

SB_RUN_FLOOR = -104.0
SWA_WINDOW = 128
REL_BUCKETS = 32
REL_MAX_DIST = 128
HGRN_CHUNK = 64
HGRN_BLOCK = 16
N_EXPERTS = 8
F_DENSE = 2816
F_EXPERT = 3584
ALPHA = (2 * DEPTH) ** 0.25

_IN_SPLITS = (('mla_cq', 256), ('mla_ckv', 128), ('mla_kr', 32), ('swa_q', 256), ('swa_k', 128),
              ('swa_v', 128), ('hgrn', 1024), ('sb_q', 256), ('sb_k', 256), ('sb_v', 256), ('gates', 4096))
_IN_OFF = {}
_o = 0
for _n, _w in _IN_SPLITS:
    _IN_OFF[_n] = _o
    _o += _w

_A_SPLITS = (('cq', 256), ('ckv', 128), ('kra', 128), ('krb', 128), ('swa_q', 256), ('swa_k', 256),
             ('swa_v', 256), ('hgrn', 1024), ('sb_q', 256), ('sb_k', 256), ('sb_v', 256))
_A_OFF = {}
_o = 0
for _n, _w in _A_SPLITS:
    _A_OFF[_n] = (_o, _o + _w)
    _o += _w
A_COLS = _o

TM_A = 512
TQ_ATT = 256
MLA_TQ = 512
MLA_TK = 512
MLA_GROUP = 4
SWA_TQ = 512
HG_ROWS = 256
TM_FFN = 512
TF_FFN = 1408
MOE_TB = 512
MOE_TF = 1792
TM_COMB = 256
TM_DISP = 512
VMEM_LIMIT = 56 * 1024 * 1024


def _cparams(sem, **kw):
    return pltpu.CompilerParams(dimension_semantics=sem, vmem_limit_bytes=VMEM_LIMIT, **kw)


def _const_spec(shape):
    nd = len(shape)
    return pl.BlockSpec(shape, lambda *_: (0,) * nd)


def _layernorm(v, g, b):
    mu = jnp.mean(v, axis=-1, keepdims=True)
    vc = v - mu
    var = jnp.mean(vc * vc, axis=-1, keepdims=True)
    return vc * lax.rsqrt(var + EPS) * g + b


def _dot(a, b):
    return jnp.dot(a, b, preferred_element_type=F32)


def _dot_nt(a, b):
    return lax.dot_general(a, b, (((1,), (1,)), ((), ())), preferred_element_type=F32)


def _split3(a):
    hi = a.astype(BF16)
    r = a - hi.astype(F32)
    mid = r.astype(BF16)
    lo = (r - mid.astype(F32)).astype(BF16)
    return hi, mid, lo


def _rope_kernel(pos_ref, freq_ref, c_ref, s_ref):
    lane = lax.broadcasted_iota(I32, pos_ref.shape, 1)
    ang = pos_ref[...] * freq_ref[...]
    rope = (lane >= MLA_NOPE) & (lane < MLA_NOPE + MLA_ROPE)
    first = lane < MLA_NOPE + MLA_ROPE // 2
    c_ref[...] = jnp.where(lane < MLA_NOPE, 1.0, jnp.where(rope, jnp.cos(ang), 0.0))
    sn = jnp.sin(ang)
    s_ref[...] = jnp.where(rope, jnp.where(first, -sn, sn), 0.0)


def _rope_tables(positions):
    n = positions.size
    half = MLA_ROPE // 2
    inv_freq = ROPE_THETA ** (-jnp.arange(half, dtype=F32) / half)
    freq = jnp.zeros((1, LANES), F32).at[0, MLA_NOPE:MLA_NOPE + MLA_ROPE].set(jnp.tile(inv_freq, 2))
    posb = jnp.broadcast_to(positions.reshape(n, 1).astype(F32), (n, LANES))
    tm = 1024
    return pl.pallas_call(
        _rope_kernel,
        grid=(n // tm,),
        in_specs=[pl.BlockSpec((tm, LANES), lambda i: (i, 0)), _const_spec((1, LANES))],
        out_specs=[pl.BlockSpec((tm, LANES), lambda i: (i, 0))] * 2,
        out_shape=[jax.ShapeDtypeStruct((n, LANES), F32)] * 2,
        compiler_params=_cparams(("parallel",)),
        name="rope_tables",
    )(posb, freq)


def _inproj_kernel(x_ref, w_ref, c_ref, s_ref, qn_ref, kvn_ref, wuqa_ref, wuqb_ref, wuk_ref, wuv_ref,
                   mq_ref, mk_ref, mv_ref, swq_ref, swk_ref, swv_ref, hg_ref, sbq_ref, sbk_ref, sbv_ref):
    h = _dot(x_ref[...].astype(BF16), w_ref[...])

    def cols(name):
        lo, hi = _A_OFF[name]
        return h[:, lo:hi]

    c = c_ref[...]
    s = s_ref[...]
    c4 = jnp.concatenate([c] * N_HEADS, axis=1)
    s4 = jnp.concatenate([s] * N_HEADS, axis=1)

    cq = cols('cq')
    cqn = (cq * lax.rsqrt(jnp.mean(cq * cq, axis=-1, keepdims=True) + EPS) * qn_ref[...]).astype(BF16)
    q = _dot(cqn, wuqa_ref[...]) * c4 + _dot(cqn, wuqb_ref[...]) * s4
    mq_ref[...] = (q * (MLA_SCALE * LOG2E)).astype(BF16)

    ckv = cols('ckv')
    ckvn = (ckv * lax.rsqrt(jnp.mean(ckv * ckv, axis=-1, keepdims=True) + EPS) * kvn_ref[...]).astype(BF16)
    krot = cols('kra') * c + cols('krb') * s
    mk_ref[...] = (_dot(ckvn, wuk_ref[...]) + jnp.concatenate([krot] * N_HEADS, axis=1)).astype(BF16)
    mv_ref[...] = _dot(ckvn, wuv_ref[...]).astype(BF16)

    swq_ref[...] = cols('swa_q').astype(BF16)
    swk_ref[...] = cols('swa_k').astype(BF16)
    swv_ref[...] = cols('swa_v').astype(BF16)
    hg_ref[...] = cols('hgrn')
    sbq_ref[...] = cols('sb_q').astype(BF16)
    sbk_ref[...] = cols('sb_k').astype(BF16)
    sbv_ref[...] = cols('sb_v').astype(BF16)


def _inproj_weights(w_in, w_uq, w_ukv):
    def seg(name, width):
        o = _IN_OFF[name]
        return w_in[:, o:o + width]

    kr = seg('mla_kr', MLA_ROPE)
    half = MLA_ROPE // 2
    z64 = jnp.zeros((D_MODEL, MLA_NOPE), F32)
    z32 = jnp.zeros((D_MODEL, LANES - MLA_NOPE - MLA_ROPE), F32)
    kra = jnp.concatenate([z64, kr, z32], axis=1)
    krb = jnp.concatenate([z64, kr[:, half:], kr[:, :half], z32], axis=1)
    swk = seg('swa_k', 128)
    swv = seg('swa_v', 128)
    dup = lambda t: jnp.concatenate([t[:, :64], t[:, :64], t[:, 64:], t[:, 64:]], axis=1)
    w_a = jnp.concatenate([
        seg('mla_cq', 256), seg('mla_ckv', 128), kra, krb,
        seg('swa_q', 256) * QK_SCALE, dup(swk), dup(swv),
        seg('hgrn', 1024), seg('sb_q', 256) * QK_SCALE, seg('sb_k', 256), seg('sb_v', 256)], axis=1)

    qd = MLA_NOPE + MLA_ROPE
    zq = jnp.zeros((MLA_Q_LORA, LANES - qd), F32)
    zn = jnp.zeros((MLA_Q_LORA, MLA_NOPE), F32)
    qa, qb = [], []
    for hh in range(N_HEADS):
        nope = w_uq[:, hh * qd: hh * qd + MLA_NOPE]
        rope = w_uq[:, hh * qd + MLA_NOPE: (hh + 1) * qd]
        qa += [nope, rope, zq]
        qb += [zn, rope[:, half:], rope[:, :half], zq]
    wuqa = jnp.concatenate(qa, axis=1)
    wuqb = jnp.concatenate(qb, axis=1)
    lane = jnp.arange(N_HEADS * LANES) % LANES
    wuk = jnp.where(lane[None, :] < MLA_NOPE, w_ukv, 0.0)
    wuv = jnp.concatenate([w_ukv[:, hh * LANES + MLA_NOPE:(hh + 1) * LANES] for hh in range(N_HEADS)], axis=1)
    return tuple(t.astype(BF16) for t in (w_a, wuqa, wuqb, wuk, wuv))


def _inproj(x2d, wts, ctab, stab, q_norm, kv_norm):
    n = x2d.shape[0]
    w_a, wuqa, wuqb, wuk, wuv = wts
    tm = TM_A
    row = lambda w: pl.BlockSpec((tm, w), lambda i: (i, 0))
    out_w = (512, 512, 256, 256, 256, 256, 1024, 256, 256, 256)
    out_dt = (BF16, BF16, BF16, BF16, BF16, BF16, F32, BF16, BF16, BF16)
    return pl.pallas_call(
        _inproj_kernel,
        grid=(n // tm,),
        in_specs=[row(D_MODEL), _const_spec(w_a.shape), row(LANES), row(LANES),
                  _const_spec((1, MLA_Q_LORA)), _const_spec((1, MLA_KV_LORA)),
                  _const_spec(wuqa.shape), _const_spec(wuqb.shape), _const_spec(wuk.shape),
                  _const_spec(wuv.shape)],
        out_specs=[row(w) for w in out_w],
        out_shape=[jax.ShapeDtypeStruct((n, w), d) for w, d in zip(out_w, out_dt)],
        compiler_params=_cparams(("parallel",)),
        name="inproj",
    )(x2d, w_a, ctab, stab, q_norm.reshape(1, -1), kv_norm.reshape(1, -1), wuqa, wuqb, wuk, wuv)


def _half_mask(half):
    lane = lax.broadcasted_iota(I32, (1, LANES), 1)
    return (lane < HEAD_DIM) if half == 0 else (lane >= HEAD_DIM)


def _mla_kernel(q_ref, k_ref, v_ref, o_ref):
    tq = q_ref.shape[0]
    tk = MLA_TK
    nsub = tq // tk
    i = pl.program_id(1)
    row = lax.broadcasted_iota(I32, (tq, tk), 0)
    col = lax.broadcasted_iota(I32, (tq, tk), 1)
    ones = jnp.ones((1, LANES), BF16)

    def update(off, carry, heads, mask):
        ss = [_dot_nt(q_ref[:, hh * LANES:(hh + 1) * LANES], k_ref[pl.ds(off, tk), hh * LANES:(hh + 1) * LANES])
              for hh in heads]
        if mask is not None:
            ss = [jnp.where(mask, s, NEG_BIG) for s in ss]
        ms = [jnp.maximum(c[0], jnp.max(s, axis=-1, keepdims=True)) for c, s in zip(carry, ss)]
        pms = [jnp.exp2(s - m).astype(BF16) for s, m in zip(ss, ms)]
        new = []
        for n, hh in enumerate(heads):
            vb = v_ref[pl.ds(off, tk), (hh // 2) * LANES:(hh // 2 + 1) * LANES]
            vb = jnp.where(_half_mask(hh % 2), vb, ones)
            m, acc = carry[n]
            new.append((ms[n], jnp.exp2(m - ms[n]) * acc + _dot(pms[n], vb)))
        return tuple(new)

    accs = []
    for g in range(0, N_HEADS, MLA_GROUP):
        heads = tuple(range(g, g + MLA_GROUP))
        init = tuple((jnp.full((tq, 1), NEG_BIG, F32), jnp.zeros((tq, LANES), F32)) for _ in heads)
        carry = lax.fori_loop(0, i * nsub,
                              lambda j, c, heads=heads: update(pl.multiple_of(j * tk, tk), c, heads, None), init)
        for r in range(nsub):
            carry = update(pl.multiple_of(i * tq + r * tk, tk), carry, heads, col + r * tk <= row)
        accs += [c[1] for c in carry]
    outs = []
    for p in range(N_HEADS // 2):
        a0, a1 = accs[2 * p], accs[2 * p + 1]
        outs.append(jnp.where(_half_mask(0), a0 / a0[:, HEAD_DIM:HEAD_DIM + 1], a1 / a1[:, 0:1]))
    o_ref[...] = jnp.concatenate(outs, axis=1).astype(o_ref.dtype)


def _mla_attention(q, k, v, batch, seq):
    tq = MLA_TQ
    q3, k3, v3 = (t.reshape(batch, seq, t.shape[-1]) for t in (q, k, v))
    out = pl.pallas_call(
        _mla_kernel,
        grid=(batch, seq // tq),
        in_specs=[pl.BlockSpec((None, tq, 512), lambda b, i: (b, i, 0)),
                  pl.BlockSpec((None, seq, 512), lambda b, i: (b, 0, 0)),
                  pl.BlockSpec((None, seq, WIDTH), lambda b, i: (b, 0, 0))],
        out_specs=pl.BlockSpec((None, tq, WIDTH), lambda b, i: (b, i, 0)),
        out_shape=jax.ShapeDtypeStruct((batch, seq, WIDTH), BF16),
        compiler_params=_cparams(("parallel", "arbitrary")),
        name="mla_attention",
    )(q3, k3, v3)
    return out.reshape(batch * seq, WIDTH)


def _sb_kernel(q_ref, k_ref, v_ref, o_ref):
    tq = q_ref.shape[0]
    i = pl.program_id(1)
    row = lax.broadcasted_iota(I32, (tq, tq), 0)
    col = lax.broadcasted_iota(I32, (tq, tq), 1)
    strict = col < row
    later = (row > col).astype(BF16)
    qs = []
    for hh in range(N_HEADS):
        qp = q_ref[:, (hh // 2) * LANES:(hh // 2 + 1) * LANES]
        qs.append(jnp.where(_half_mask(hh % 2), qp, jnp.zeros_like(qp)))

    def block(j, carry, diag):
        off = pl.multiple_of(j * tq, tq)
        runs, accs = carry
        heads = range(N_HEADS)
        zs = [_dot_nt(qs[hh], k_ref[pl.ds(off, tq), (hh // 2) * LANES:(hh // 2 + 1) * LANES]) for hh in heads]
        lsps = [jnp.minimum(z, 0.0) - jnp.log(1.0 + jnp.exp(-jnp.abs(z))) for z in zs]
        lsns = [lsp - z for lsp, z in zip(lsps, zs)]
        if diag:
            lsns = [jnp.where(strict, t, 0.0) for t in lsns]
        his = [t.astype(BF16) for t in lsns]
        los = [(t - hi.astype(F32)).astype(BF16) for t, hi in zip(lsns, his)]
        rems = [_dot(hi, later) + _dot(lo, later) for hi, lo in zip(his, los)]
        args = [lsps[hh] + rems[hh] + runs[hh] for hh in heads]
        if diag:
            args = [jnp.where(strict, t, NEG_BIG) for t in args]
        probs = [jnp.exp(t).astype(BF16) for t in args]
        new_runs = tuple(runs[hh] + rems[hh][:, 0:1] + lsns[hh][:, 0:1] for hh in heads)
        new_accs = list(accs)
        for hh in heads:
            p = hh // 2
            vb = v_ref[pl.ds(off, tq), p * LANES:(p + 1) * LANES]
            vb = jnp.where(_half_mask(hh % 2), vb, jnp.zeros_like(vb))
            new_accs[p] = new_accs[p] + _dot(probs[hh], vb)
        return new_runs, tuple(new_accs)

    init = (tuple(jnp.zeros((tq, 1), F32) for _ in range(N_HEADS)),
            tuple(jnp.zeros((tq, LANES), F32) for _ in range(N_HEADS // 2)))
    def still_active(runs):
        top = functools.reduce(jnp.maximum, runs)
        return (jnp.max(top) > SB_RUN_FLOOR).astype(I32)

    runs, accs = block(i, init, True)

    def cond(c):
        return (c[0] < i) & (c[1] > 0)

    def body(c):
        jj, _, runs, accs = c
        runs, accs = block(i - 1 - jj, (runs, accs), False)
        return jj + 1, still_active(runs), runs, accs

    _, _, _, accs = lax.while_loop(cond, body, (jnp.int32(0), still_active(runs), runs, accs))
    o_ref[...] = jnp.concatenate(accs, axis=1).astype(o_ref.dtype)


def _sb_attention(q, k, v, batch, seq):
    tq = TQ_ATT
    q3, k3, v3 = (t.reshape(batch, seq, WIDTH) for t in (q, k, v))
    out = pl.pallas_call(
        _sb_kernel,
        grid=(batch, seq // tq),
        in_specs=[pl.BlockSpec((None, tq, WIDTH), lambda b, i: (b, i, 0)),
                  pl.BlockSpec((None, seq, WIDTH), lambda b, i: (b, 0, 0)),
                  pl.BlockSpec((None, seq, WIDTH), lambda b, i: (b, 0, 0))],
        out_specs=pl.BlockSpec((None, tq, WIDTH), lambda b, i: (b, i, 0)),
        out_shape=jax.ShapeDtypeStruct((batch, seq, WIDTH), BF16),
        compiler_params=_cparams(("parallel", "arbitrary")),
        name="stick_breaking",
    )(q3, k3, v3)
    return out.reshape(batch * seq, WIDTH)


def _rel_bucket(dist):
    exact = REL_BUCKETS // 2
    n = jnp.maximum(dist, 0)
    nf = jnp.maximum(n, 1).astype(F32)
    large = exact + (jnp.log(nf / exact) / math.log(REL_MAX_DIST / exact) * (REL_BUCKETS - exact)).astype(I32)
    large = jnp.clip(large, 0, REL_BUCKETS - 1)
    return jnp.where(n < exact, n, large)


def _swa_kernel(sink_ref, tab_ref, q_ref, kc_ref, kh_ref, vc_ref, vh_ref, pq_ref, pkc_ref, pkh_ref, o_ref):
    w = SWA_WINDOW
    step = pl.program_id(1)
    row = lax.broadcasted_iota(I32, (w, w), 0)
    col = lax.broadcasted_iota(I32, (w, w), 1)
    valid_c = col <= row
    valid_p = col > row
    tabs = [jnp.broadcast_to(tab_ref[hh:hh + 1, :], (w, LANES)) for hh in range(N_HEADS)]
    ones = jnp.ones((1, LANES), BF16)
    nsub = q_ref.shape[0] // w
    chains = [(r, hh) for r in range(nsub) for hh in range(N_HEADS)]

    def keys(ref, halo_ref, r, hh):
        sl = slice((hh // 2) * LANES, (hh // 2 + 1) * LANES)
        cur = ref[r * w:(r + 1) * w, sl]
        prev = ref[(r - 1) * w:r * w, sl] if r else halo_ref[:, sl]
        return cur, prev

    buckets = []
    for r in range(nsub):
        pq = pq_ref[r * w:(r + 1) * w, :]
        pk_prev = pkc_ref[:, (r - 1) * w:r * w] if r else pkh_ref[...]
        buckets.append((_rel_bucket(pq - pkc_ref[:, r * w:(r + 1) * w]), _rel_bucket(pq - pk_prev)))
    logits = []
    for r, hh in chains:
        qp = q_ref[r * w:(r + 1) * w, (hh // 2) * LANES:(hh // 2 + 1) * LANES]
        qh = jnp.where(_half_mask(hh % 2), qp, jnp.zeros_like(qp))
        kc, kp = keys(kc_ref, kh_ref, r, hh)
        logits.append((_dot_nt(qh, kc), _dot_nt(qh, kp)))
    masked = []
    for (r, hh), (lc, lp) in zip(chains, logits):
        lc = jnp.where(valid_c, lc + jnp.take_along_axis(tabs[hh], buckets[r][0], axis=1), NEG_BIG)
        lp = lp + jnp.take_along_axis(tabs[hh], buckets[r][1], axis=1)
        lp = jnp.where(valid_p if r else valid_p & (step > 0), lp, NEG_BIG)
        masked.append((lc, lp))
    maxes = [jnp.maximum(jnp.maximum(jnp.max(lc, axis=-1, keepdims=True), jnp.max(lp, axis=-1, keepdims=True)),
                         sink_ref[hh]) for (r, hh), (lc, lp) in zip(chains, masked)]
    probs = [(jnp.exp(lc - m).astype(BF16), jnp.exp(lp - m).astype(BF16)) for (lc, lp), m in zip(masked, maxes)]
    outs = {}
    for (r, hh), (ec, ep), m in zip(chains, probs, maxes):
        vc, vp = keys(vc_ref, vh_ref, r, hh)
        mine = _half_mask(hh % 2)
        acc = _dot(ec, jnp.where(mine, vc, ones)) + _dot(ep, jnp.where(mine, vp, ones))
        den = (acc[:, 0:1] if hh % 2 else acc[:, HEAD_DIM:HEAD_DIM + 1]) + jnp.exp(sink_ref[hh] - m)
        outs[(r, hh)] = acc / den
    for r in range(nsub):
        pairs = [jnp.where(_half_mask(0), outs[(r, 2 * p)], outs[(r, 2 * p + 1)]) for p in range(N_HEADS // 2)]
        o_ref[r * w:(r + 1) * w, :] = jnp.concatenate(pairs, axis=1).astype(o_ref.dtype)


def _swa_attention(q, k, v, positions, sinks, rel_table, batch, seq):
    w = SWA_WINDOW
    tq = SWA_TQ
    per = tq // w
    q3, k3, v3 = (t.reshape(batch, seq, WIDTH) for t in (q, k, v))
    pcol = positions.reshape(batch, seq, 1)
    prow = positions.reshape(batch, 1, seq)
    tab = jnp.zeros((N_HEADS, LANES), F32).at[:, :REL_BUCKETS].set(rel_table.astype(F32).T)
    cur = lambda b, n: (b, n, 0)
    halo = lambda b, n: (b, jnp.maximum(n * per - 1, 0), 0)
    out = pl.pallas_call(
        _swa_kernel,
        grid=(batch, seq // tq),
        in_specs=[pl.BlockSpec(memory_space=pltpu.SMEM), _const_spec((N_HEADS, LANES)),
                  pl.BlockSpec((None, tq, WIDTH), cur),
                  pl.BlockSpec((None, tq, WIDTH), cur), pl.BlockSpec((None, w, WIDTH), halo),
                  pl.BlockSpec((None, tq, WIDTH), cur), pl.BlockSpec((None, w, WIDTH), halo),
                  pl.BlockSpec((None, tq, 1), cur),
                  pl.BlockSpec((None, 1, tq), lambda b, n: (b, 0, n)),
                  pl.BlockSpec((None, 1, w), lambda b, n: (b, 0, jnp.maximum(n * per - 1, 0)))],
        out_specs=pl.BlockSpec((None, tq, WIDTH), cur),
        out_shape=jax.ShapeDtypeStruct((batch, seq, WIDTH), BF16),
        compiler_params=_cparams(("parallel", "arbitrary")),
        name="swa_attention",
    )(sinks.astype(F32), tab, q3, k3, k3, v3, v3, pcol, prow, prow)
    return out.reshape(batch * seq, WIDTH)


def _hgrn_kernel(hg_ref, lb_ref, nw_ref, o_ref, state_ref):
    c = HGRN_CHUNK
    blk = HGRN_BLOCK

    @pl.when(pl.program_id(1) == 0)
    def _():
        state_ref[...] = jnp.zeros_like(state_ref)

    r64 = lax.broadcasted_iota(I32, (c, c), 0)
    c64 = lax.broadcasted_iota(I32, (c, c), 1)
    incl = (c64 <= r64).astype(BF16)
    ra = lax.broadcasted_iota(I32, (WIDTH, WIDTH), 0) // HEAD_DIM
    ca = lax.broadcasted_iota(I32, (WIDTH, WIDTH), 1) // HEAD_DIM
    same_head = ra == ca
    seg = same_head.astype(BF16)
    ones_cols = jnp.ones((c, LANES), BF16)
    trow = lax.broadcasted_iota(I32, (blk, WIDTH), 0)
    caps = [jnp.where(trow >= s_i, 0.0, NEG_BIG) for s_i in range(blk)]
    lane_head = lax.broadcasted_iota(I32, (1, WIDTH), 1) // HEAD_DIM
    lb = lb_ref[...]
    nw = nw_ref[...]
    dn0 = (((0,), (0,)), ((), ()))

    for ch in range(hg_ref.shape[0] // c):
        rows = slice(ch * c, (ch + 1) * c)
        qraw = hg_ref[rows, 0:WIDTH]
        fraw = hg_ref[rows, WIDTH:2 * WIDTH]
        v = hg_ref[rows, 2 * WIDTH:3 * WIDTH]
        graw = hg_ref[rows, 3 * WIDTH:4 * WIDTH]
        qf = qraw * jax.nn.sigmoid(qraw)
        forget = lb + (1.0 - lb) * jax.nn.sigmoid(fraw)
        lf = jnp.log(forget)
        kk = 1.0 - forget
        gate = graw * jax.nn.sigmoid(graw)
        vb = v.astype(BF16)

        lf3 = _split3(lf)
        bc = _dot(incl, lf3[0]) + _dot(incl, lf3[1]) + _dot(incl, lf3[2])
        b_last = bc[c - 1:c, :]
        tot_col = sum(lax.dot_general(t, ones_cols, dn0, preferred_element_type=F32) for t in lf3)
        decay_col = jnp.exp(jnp.concatenate([tot_col, tot_col], axis=1))

        state = state_ref[...]
        o_inter = _dot((qf * jnp.exp(bc)).astype(BF16), state.astype(BF16))

        def before(qa, qb, ka, kb):
            ref = bc[kb - 1:kb, :]
            qt = qf[qa:qb] * jnp.exp(bc[qa:qb] - ref)
            kt = (kk[ka:kb] * jnp.exp(ref - bc[ka:kb])).astype(BF16)
            qs = jnp.concatenate([jnp.where(lane_head == hh, qt, 0.0) for hh in range(N_HEADS)], axis=0)
            att = _dot_nt(qs.astype(BF16), kt)
            mix = _dot(att.astype(BF16), vb[ka:kb])
            nq = qb - qa
            return sum(jnp.where(lane_head == hh, mix[hh * nq:(hh + 1) * nq], 0.0) for hh in range(N_HEADS))

        bc2 = bc * LOG2E

        def inside(a):
            b2 = bc2[a:a + blk]
            qb_ = qf[a:a + blk]
            ws = []
            for s_i in range(blk):
                e = jnp.exp2(jnp.minimum(b2 - b2[s_i:s_i + 1, :], caps[s_i]))
                ws.append((qb_ * kk[a + s_i:a + s_i + 1, :] * e).astype(BF16))
            att = _dot(jnp.concatenate(ws, axis=0), seg)
            return sum(att[s_i * blk:(s_i + 1) * blk] * v[a + s_i:a + s_i + 1, :] for s_i in range(blk))

        half = c // 2
        far = before(half, c, 0, half)
        intra = [inside(0),
                 inside(blk) + before(blk, half, 0, blk),
                 inside(half) + far[:blk],
                 inside(half + blk) + far[blk:] + before(half + blk, c, half, half + blk)]
        o = o_inter + jnp.concatenate(intra, axis=0)

        khat = (kk * jnp.exp(b_last - bc)).astype(BF16)
        upd = lax.dot_general(khat, vb, dn0, preferred_element_type=F32)
        state_ref[...] = decay_col * state + jnp.where(same_head, upd, 0.0)

        o2 = _split3(o * o)
        ms = (_dot(o2[0], seg) + _dot(o2[1], seg)) * (1.0 / HEAD_DIM)
        o_ref[rows, :] = (o * lax.rsqrt(ms + EPS) * nw * gate).astype(o_ref.dtype)


def _hgrn(hg, lower_bound, norm_w, batch, seq):
    rows = HG_ROWS
    hg3 = hg.reshape(batch, seq, 4 * WIDTH)
    out = pl.pallas_call(
        _hgrn_kernel,
        grid=(batch, seq // rows),
        in_specs=[pl.BlockSpec((None, rows, 4 * WIDTH), lambda b, i: (b, i, 0)),
                  _const_spec((1, WIDTH)), _const_spec((1, WIDTH))],
        out_specs=pl.BlockSpec((None, rows, WIDTH), lambda b, i: (b, i, 0)),
        out_shape=jax.ShapeDtypeStruct((batch, seq, WIDTH), BF16),
        scratch_shapes=[pltpu.VMEM((WIDTH, WIDTH), F32)],
        compiler_params=_cparams(("parallel", "arbitrary")),
        name="hgrn2",
    )(hg3, lower_bound.reshape(1, WIDTH).astype(F32), norm_w.reshape(1, WIDTH).astype(F32))
    return out.reshape(batch * seq, WIDTH)


def _merge_kernel(x_ref, y0_ref, y1_ref, y2_ref, y3_ref, wg_ref, wb_ref, wo_ref, g_ref, b_ref, o_ref):
    x = x_ref[...]
    xb = x.astype(BF16)
    merged = jnp.zeros(x.shape, F32)
    for nbr, y_ref in enumerate((y0_ref, y1_ref, y2_ref, y3_ref)):
        gate = jax.nn.sigmoid(_dot(xb, wg_ref[:, nbr * D_MODEL:(nbr + 1) * D_MODEL]))
        merged = merged + gate * _dot(y_ref[...], wb_ref[nbr])
    y = _dot(merged.astype(BF16), wo_ref[...])
    o_ref[...] = _layernorm(ALPHA * x + y, g_ref[...], b_ref[...])


def _merge(x2d, ys, wg, wb, wo, g, b):
    n = x2d.shape[0]
    tm = TM_A
    row = lambda w: pl.BlockSpec((tm, w), lambda i: (i, 0))
    return pl.pallas_call(
        _merge_kernel,
        grid=(n // tm,),
        in_specs=[row(D_MODEL)] + [row(WIDTH)] * 4 +
                 [_const_spec(wg.shape), _const_spec(wb.shape), _const_spec(wo.shape),
                  _const_spec((1, D_MODEL)), _const_spec((1, D_MODEL))],
        out_specs=row(D_MODEL),
        out_shape=jax.ShapeDtypeStruct((n, D_MODEL), F32),
        compiler_params=_cparams(("parallel",)),
        name="merge_outproj_ln",
    )(x2d, *ys, wg, wb, wo, g.reshape(1, -1), b.reshape(1, -1))


def _memkv_kernel(m_ref, w_ref, k_ref, v_ref):
    kv = _dot(m_ref[...].astype(BF16), w_ref[...])
    k_ref[...] = kv[:, :WIDTH].astype(BF16)
    v_ref[...] = kv[:, WIDTH:].astype(BF16)


def _memkv(mem, wkv):
    batch, m, _ = mem.shape
    return pl.pallas_call(
        _memkv_kernel,
        grid=(batch,),
        in_specs=[pl.BlockSpec((None, m, D_MODEL), lambda b: (b, 0, 0)), _const_spec(wkv.shape)],
        out_specs=[pl.BlockSpec((None, m, WIDTH), lambda b: (b, 0, 0))] * 2,
        out_shape=[jax.ShapeDtypeStruct((batch, m, WIDTH), BF16)] * 2,
        compiler_params=_cparams(("parallel",)),
        name="mem_kv",
    )(mem, wkv)


def _xattn_kernel(x_ref, wq_ref, k_ref, v_ref, wo_ref, g_ref, b_ref, o_ref):
    x = x_ref[...]
    q = _dot(x.astype(BF16), wq_ref[...]).astype(BF16)
    k = k_ref[...]
    v = v_ref[...]
    lane = lax.broadcasted_iota(I32, (1, WIDTH), 1) // HEAD_DIM
    o = jnp.zeros((x.shape[0], WIDTH), F32)
    for hh in range(N_HEADS):
        qh = jnp.where(lane == hh, q, jnp.zeros_like(q))
        s = _dot_nt(qh, k)
        e = jnp.exp(s - jnp.max(s, axis=-1, keepdims=True))
        p = e / jnp.sum(e, axis=-1, keepdims=True)
        o = o + jnp.where(lane == hh, _dot(p.astype(BF16), v), 0.0)
    y = _dot(o.astype(BF16), wo_ref[...])
    o_ref[...] = _layernorm(ALPHA * x + y, g_ref[...], b_ref[...])


def _xattn(x2d, wq, k, v, wo, g, b, batch, seq):
    tm = TM_A
    m = k.shape[1]
    x3 = x2d.reshape(batch, seq, D_MODEL)
    row = pl.BlockSpec((None, tm, D_MODEL), lambda bb, i: (bb, i, 0))
    kv_spec = pl.BlockSpec((None, m, WIDTH), lambda bb, i: (bb, 0, 0))
    out = pl.pallas_call(
        _xattn_kernel,
        grid=(batch, seq // tm),
        in_specs=[row, _const_spec(wq.shape), kv_spec, kv_spec, _const_spec(wo.shape),
                  _const_spec((1, D_MODEL)), _const_spec((1, D_MODEL))],
        out_specs=row,
        out_shape=jax.ShapeDtypeStruct((batch, seq, D_MODEL), F32),
        compiler_params=_cparams(("parallel", "parallel")),
        name="mem_xattn_ln",
    )(x3, wq, k, v, wo, g.reshape(1, -1), b.reshape(1, -1))
    return out.reshape(batch * seq, D_MODEL)


def _ffn_kernel(x_ref, w1_ref, w3_ref, w2_ref, g_ref, b_ref, o_ref, acc_ref):
    f = pl.program_id(1)
    xb = x_ref[...].astype(BF16)
    a = _dot(xb, w1_ref[...])
    gate = _dot(xb, w3_ref[...])
    part = _dot((a * jax.nn.sigmoid(a) * gate).astype(BF16), w2_ref[...])

    @pl.when(f == 0)
    def _():
        acc_ref[...] = part

    @pl.when(f > 0)
    def _():
        acc_ref[...] += part

    @pl.when(f == pl.num_programs(1) - 1)
    def _():
        o_ref[...] = _layernorm(ALPHA * x_ref[...] + acc_ref[...], g_ref[...], b_ref[...])


def _ffn(x2d, w13, w2, g, b):
    n = x2d.shape[0]
    tm, tf = TM_FFN, TF_FFN
    nf = F_DENSE // tf
    return pl.pallas_call(
        _ffn_kernel,
        grid=(n // tm, nf),
        in_specs=[pl.BlockSpec((tm, D_MODEL), lambda i, f: (i, 0)),
                  pl.BlockSpec((D_MODEL, tf), lambda i, f: (0, f)),
                  pl.BlockSpec((D_MODEL, tf), lambda i, f: (0, nf + f)),
                  pl.BlockSpec((tf, D_MODEL), lambda i, f: (f, 0)),
                  _const_spec((1, D_MODEL)), _const_spec((1, D_MODEL))],
        out_specs=pl.BlockSpec((tm, D_MODEL), lambda i, f: (i, 0)),
        out_shape=jax.ShapeDtypeStruct((n, D_MODEL), F32),
        scratch_shapes=[pltpu.VMEM((tm, D_MODEL), F32)],
        compiler_params=_cparams(("parallel", "arbitrary")),
        name="ffn_ln",
    )(x2d, w13, w13, w2, g.reshape(1, -1), b.reshape(1, -1))


def _router_kernel(x_ref, r_ref, info_ref, wts_ref, cnt_ref, carry_ref):
    tm = x_ref.shape[0]

    @pl.when(pl.program_id(0) == 0)
    def _():
        carry_ref[...] = jnp.zeros_like(carry_ref)

    logits = jnp.dot(x_ref[...], r_ref[...], precision=lax.Precision.HIGHEST, preferred_element_type=F32)
    lane = lax.broadcasted_iota(I32, (tm, LANES), 1)
    lg = jnp.where(lane < N_EXPERTS, logits, -jnp.inf)
    m1 = jnp.max(lg, axis=-1, keepdims=True)
    i1 = jnp.min(jnp.where(lg == m1, lane, LANES), axis=-1, keepdims=True)
    lg2 = jnp.where(lane == i1, -jnp.inf, lg)
    m2 = jnp.max(lg2, axis=-1, keepdims=True)
    i2 = jnp.min(jnp.where(lg2 == m2, lane, LANES), axis=-1, keepdims=True)
    e = jnp.exp(m2 - m1)
    w1 = 1.0 / (1.0 + e)
    w2 = e / (1.0 + e)
    sel1 = lane == i1
    sel2 = lane == i2
    chosen = jnp.where(sel1 | sel2, 1.0, 0.0)
    row = lax.broadcasted_iota(I32, (tm, tm), 0)
    col = lax.broadcasted_iota(I32, (tm, tm), 1)
    before = (col < row).astype(BF16)
    ranks = _dot(before, chosen.astype(BF16)) + carry_ref[...]
    r1 = jnp.sum(jnp.where(sel1, ranks, 0.0), axis=-1, keepdims=True).astype(I32)
    r2 = jnp.sum(jnp.where(sel2, ranks, 0.0), axis=-1, keepdims=True).astype(I32)
    carry_ref[...] = carry_ref[...] + jnp.sum(chosen, axis=0, keepdims=True)
    info_ref[...] = jnp.where(lane == 0, i1, jnp.where(lane == 1, i2, jnp.where(lane == 2, r1,
                              jnp.where(lane == 3, r2, 0))))
    wts_ref[...] = jnp.where(lane == 0, w1, jnp.where(lane == 1, w2, 0.0))
    cnt_ref[...] = carry_ref[...]


def _router(x2d, router):
    n = x2d.shape[0]
    tm = TM_A
    r_pad = jnp.zeros((D_MODEL, LANES), F32).at[:, :N_EXPERTS].set(router.astype(F32))
    row = pl.BlockSpec((tm, LANES), lambda i: (i, 0))
    return pl.pallas_call(
        _router_kernel,
        grid=(n // tm,),
        in_specs=[pl.BlockSpec((tm, D_MODEL), lambda i: (i, 0)), _const_spec(r_pad.shape)],
        out_specs=[row, row, _const_spec((1, LANES))],
        out_shape=[jax.ShapeDtypeStruct((n, LANES), I32), jax.ShapeDtypeStruct((n, LANES), F32),
                   jax.ShapeDtypeStruct((1, LANES), F32)],
        scratch_shapes=[pltpu.VMEM((1, LANES), F32)],
        compiler_params=_cparams(("arbitrary",)),
        name="moe_router",
    )(x2d, r_pad)


def _dispatch_kernel(pad_ref, dest_ref, x_ref, xb_hbm, stage_ref, sems):
    tm = x_ref.shape[0]
    i = pl.program_id(0)
    last = pl.num_programs(0) - 1
    slot = i % 2

    def wait_step(s):
        for _ in range(2):
            pltpu.make_async_copy(stage_ref.at[s], xb_hbm.at[pl.ds(0, tm), :], sems.at[s]).wait()

    @pl.when(i >= 2)
    def _():
        wait_step(slot)

    stage_ref[slot] = x_ref[...]

    def issue(r, c):
        for k in range(2):
            pltpu.make_async_copy(stage_ref.at[slot, pl.ds(r, 1), :],
                                  xb_hbm.at[pl.ds(dest_ref[0, 2 * r + k], 1), :], sems.at[slot]).start()
        return c
    lax.fori_loop(0, tm, issue, 0)

    @pl.when(i == last)
    def _():
        def fill(e, c):
            def one(s, c2):
                pltpu.make_async_copy(stage_ref.at[slot, pl.ds(0, 1), :], xb_hbm.at[pl.ds(s, 1), :],
                                      sems.at[2]).start()
                return c2

            def done(s, c2):
                pltpu.make_async_copy(stage_ref.at[slot, pl.ds(0, 1), :], xb_hbm.at[pl.ds(0, 1), :],
                                      sems.at[2]).wait()
                return c2
            lax.fori_loop(pad_ref[0, e], pad_ref[1, e], one, 0)
            lax.fori_loop(pad_ref[0, e], pad_ref[1, e], done, 0)
            return c
        lax.fori_loop(0, pad_ref.shape[1], fill, 0)
        wait_step(slot)

        @pl.when(last >= 1)
        def _():
            wait_step(1 - slot)


def _dispatch(x2d, dest, pads, nblk):
    n = x2d.shape[0]
    tm = TM_DISP
    nt = n // tm
    grid_spec = pltpu.PrefetchScalarGridSpec(
        num_scalar_prefetch=1,
        grid=(nt,),
        in_specs=[pl.BlockSpec((None, 1, 2 * tm), lambda i, pads: (i, 0, 0), memory_space=pltpu.SMEM),
                  pl.BlockSpec((tm, D_MODEL), lambda i, pads: (i, 0))],
        out_specs=pl.BlockSpec(memory_space=pl.ANY),
        scratch_shapes=[pltpu.VMEM((2, tm, D_MODEL), F32), pltpu.SemaphoreType.DMA((3,))],
    )
    return pl.pallas_call(
        _dispatch_kernel,
        grid_spec=grid_spec,
        out_shape=jax.ShapeDtypeStruct((nblk * MOE_TB, D_MODEL), F32),
        compiler_params=_cparams(("arbitrary",), disable_bounds_checks=True),
        name="moe_dispatch",
    )(pads, dest.reshape(nt, 1, 2 * tm), x2d)


def _expert_kernel(nused_ref, bexp_ref, x_ref, w1_ref, w3_ref, w2_ref, o_ref, acc_ref):
    f = pl.program_id(1)

    @pl.when(pl.program_id(0) < nused_ref[0])
    def _():
        xb = x_ref[...].astype(BF16)
        a = _dot(xb, w1_ref[...])
        gate = _dot(xb, w3_ref[...])
        part = _dot((a * jax.nn.sigmoid(a) * gate).astype(BF16), w2_ref[...])

        @pl.when(f == 0)
        def _():
            acc_ref[...] = part

        @pl.when(f > 0)
        def _():
            acc_ref[...] += part

        @pl.when(f == pl.num_programs(1) - 1)
        def _():
            o_ref[...] = acc_ref[...]

    @pl.when(pl.program_id(0) >= nused_ref[0])
    def _():
        o_ref[...] = jnp.zeros_like(o_ref)


def _experts(xb, w13, w2, nused, blk_exp, nblk):
    tb, tf = MOE_TB, MOE_TF
    nf = F_EXPERT // tf

    def blk(i, nu):
        return jnp.maximum(jnp.minimum(i, nu[0] - 1), 0)

    def ftile(i, f, nu):
        return jnp.where(i < nu[0], f, nf - 1)

    grid_spec = pltpu.PrefetchScalarGridSpec(
        num_scalar_prefetch=2,
        grid=(nblk, nf),
        in_specs=[pl.BlockSpec((tb, D_MODEL), lambda i, f, nu, be: (blk(i, nu), 0)),
                  pl.BlockSpec((None, D_MODEL, tf), lambda i, f, nu, be: (be[blk(i, nu)], 0, ftile(i, f, nu))),
                  pl.BlockSpec((None, D_MODEL, tf), lambda i, f, nu, be: (be[blk(i, nu)], 0, nf + ftile(i, f, nu))),
                  pl.BlockSpec((None, tf, D_MODEL), lambda i, f, nu, be: (be[blk(i, nu)], ftile(i, f, nu), 0))],
        out_specs=pl.BlockSpec((tb, D_MODEL), lambda i, f, nu, be: (i, 0)),
        scratch_shapes=[pltpu.VMEM((tb, D_MODEL), F32)],
    )
    return pl.pallas_call(
        _expert_kernel,
        grid_spec=grid_spec,
        out_shape=jax.ShapeDtypeStruct((nblk * tb, D_MODEL), F32),
        compiler_params=_cparams(("arbitrary", "arbitrary")),
        name="moe_experts",
    )(nused, blk_exp, xb, w13, w13, w2)


def _combine_kernel(dest_ref, nxt_ref, y_hbm, x_ref, wts_ref, g_ref, b_ref, o_ref, buf_ref, sems):
    tm = x_ref.shape[0]
    i = pl.program_id(0)
    slot = i % 2

    def gather(idx_ref, s):
        def issue(r, c):
            for k in range(2):
                pltpu.make_async_copy(y_hbm.at[pl.ds(idx_ref[0, 2 * r + k], 1), :],
                                      buf_ref.at[s, k, pl.ds(r, 1), :], sems.at[s]).start()
            return c
        lax.fori_loop(0, tm, issue, 0)

    @pl.when(i == 0)
    def _():
        gather(dest_ref, slot)

    @pl.when(i + 1 < pl.num_programs(0))
    def _():
        gather(nxt_ref, 1 - slot)

    for k in range(2):
        pltpu.make_async_copy(y_hbm.at[pl.ds(0, tm), :], buf_ref.at[slot, k], sems.at[slot]).wait()
    wts = wts_ref[...]
    y = wts[:, 0:1] * buf_ref[slot, 0] + wts[:, 1:2] * buf_ref[slot, 1]
    o_ref[...] = _layernorm(ALPHA * x_ref[...] + y, g_ref[...], b_ref[...])


def _combine(yb, dest, x2d, wts, g, b):
    n = x2d.shape[0]
    tm = TM_COMB
    nt = n // tm
    row = lambda w: pl.BlockSpec((tm, w), lambda i: (i, 0))
    dest3 = dest.reshape(nt, 1, 2 * tm)
    return pl.pallas_call(
        _combine_kernel,
        grid=(nt,),
        in_specs=[pl.BlockSpec((None, 1, 2 * tm), lambda i: (i, 0, 0), memory_space=pltpu.SMEM),
                  pl.BlockSpec((None, 1, 2 * tm), lambda i: (jnp.minimum(i + 1, nt - 1), 0, 0),
                               memory_space=pltpu.SMEM),
                  pl.BlockSpec(memory_space=pl.ANY), row(D_MODEL), row(LANES),
                  _const_spec((1, D_MODEL)), _const_spec((1, D_MODEL))],
        out_specs=row(D_MODEL),
        out_shape=jax.ShapeDtypeStruct((n, D_MODEL), F32),
        scratch_shapes=[pltpu.VMEM((2, 2, tm, D_MODEL), F32), pltpu.SemaphoreType.DMA((2,))],
        compiler_params=_cparams(("arbitrary",), disable_bounds_checks=True),
        name="moe_combine_ln",
    )(dest3, dest3, yb, x2d, wts, g.reshape(1, -1), b.reshape(1, -1))


def _moe(x2d, router, w13, w2, g, b):
    n = x2d.shape[0]
    tb = MOE_TB
    info, wts, cnt = _router(x2d, router)
    idx = info[:, 0:2]
    rank = info[:, 2:4]
    counts = cnt[0, :N_EXPERTS].astype(I32)
    padded = (counts + tb - 1) // tb * tb
    pend = jnp.cumsum(padded)
    pstart = pend - padded
    dest = (pstart[idx] + rank).astype(I32).reshape(-1)
    nblk = (2 * n) // tb + N_EXPERTS
    pads = jnp.stack([jnp.append(pstart + counts, pend[-1]), jnp.append(pend, nblk * tb)]).astype(I32)
    nused = (pend[-1] // tb).astype(I32).reshape(1)
    blk_exp = jnp.minimum(jnp.searchsorted(pend, jnp.arange(nblk, dtype=I32) * tb, side='right'),
                          N_EXPERTS - 1).astype(I32)
    xb = _dispatch(x2d, dest, pads, nblk)
    yb = _experts(xb, w13, w2, nused, blk_exp, nblk)
    return _combine(yb, dest, x2d, wts, g, b)


def kernel(x, mem, positions, rel_bias_table, hgrn_lb_logits, w_in, mla_q_norm, mla_w_uq, mla_kv_norm, mla_w_ukv, swa_sinks, hgrn_norm, w_branch, w_out, ln_g, ln_b, xa_wq, xa_wkv, xa_wo, ffn_w13, ffn_w2, moe_router, moe_w13, moe_w2):
    batch, seq, _ = x.shape
    n = batch * seq
    sm = jax.nn.softmax(hgrn_lb_logits.astype(F32), axis=0)
    lower_bounds = jnp.cumsum(sm, axis=0) - sm[0]
    ctab, stab = _rope_tables(positions)
    xc = x.reshape(n, D_MODEL)
    for l in range(DEPTH):
        wts = _inproj_weights(w_in[l], mla_w_uq[l], mla_w_ukv[l])
        mq, mk, mv, swq, swk, swv, hg, sbq, sbk, sbv = _inproj(xc, wts, ctab, stab, mla_q_norm[l], mla_kv_norm[l])
        y_mla = _mla_attention(mq, mk, mv, batch, seq)
        y_swa = _swa_attention(swq, swk, swv, positions, swa_sinks[l], rel_bias_table, batch, seq)
        y_hg = _hgrn(hg, lower_bounds[l], hgrn_norm[l], batch, seq)
        y_sb = _sb_attention(sbq, sbk, sbv, batch, seq)
        go = _IN_OFF['gates']
        xc = _merge(xc, (y_mla, y_swa, y_hg, y_sb), w_in[l][:, go:].astype(BF16), w_branch[l].astype(BF16),
                    w_out[l].astype(BF16), ln_g[l, 0], ln_b[l, 0])
        mk_, mv_ = _memkv(mem, xa_wkv[l].astype(BF16))
        xc = _xattn(xc, (xa_wq[l] * QK_SCALE).astype(BF16), mk_, mv_, xa_wo[l].astype(BF16),
                    ln_g[l, 1], ln_b[l, 1], batch, seq)
        if l % 2 == 0:
            xc = _ffn(xc, ffn_w13[l // 2].astype(BF16), ffn_w2[l // 2].astype(BF16), ln_g[l, 2], ln_b[l, 2])
        else:
            xc = _moe(xc, moe_router[l // 2], moe_w13[l // 2].astype(BF16), moe_w2[l // 2].astype(BF16),
                      ln_g[l, 2], ln_b[l, 2])
    return xc.reshape(batch, seq, D_MODEL)
```

```python
import functools
import math

import jax
import jax.numpy as jnp
from jax import lax
from jax.experimental import pallas as pl
from jax.experimental.pallas import tpu as pltpu

F32 = jnp.float32
BF16 = jnp.bfloat16
I32 = jnp.int32

D_MODEL = 1024
DEPTH = 2
EPS = 1e-5
NEG_BIG = -1e30
LANES = 128
HEAD_DIM = 64
N_HEADS = 4
WIDTH = N_HEADS * HEAD_DIM

MLA_Q_LORA = 256
MLA_KV_LORA = 128
MLA_NOPE = 64
MLA_ROPE = 32
ROPE_THETA = 10000.0
MLA_SCALE = (MLA_NOPE + MLA_ROPE) ** -0.5
LOG2E = math.log2(math.e)
QK_SCALE = HEAD_DIM ** -0.5

SB_RUN_FLOOR = -150.0
SWA_WINDOW = 128
REL_BUCKETS = 32
REL_MAX_DIST = 128
HGRN_CHUNK = 64
HGRN_BLOCK = 16
N_EXPERTS = 8
F_DENSE = 2816
F_EXPERT = 3584
ALPHA = (2 * DEPTH) ** 0.25

_IN_SPLITS = (('mla_cq', 256), ('mla_ckv', 128), ('mla_kr', 32), ('swa_q', 256), ('swa_k', 128),
              ('swa_v', 128), ('hgrn', 1024), ('sb_q', 256), ('sb_k', 256), ('sb_v', 256), ('gates', 4096))
_IN_OFF = {}
_o = 0
for _n, _w in _IN_SPLITS:
    _IN_OFF[_n] = _o
    _o += _w

_A_SPLITS = (('cq', 256), ('ckv', 128), ('kra', 128), ('krb', 128), ('swa_q', 256), ('swa_k', 256),
             ('swa_v', 256), ('hgrn', 1024), ('sb_q', 256), ('sb_k', 256), ('sb_v', 256))
_A_OFF = {}
_o = 0
for _n, _w in _A_SPLITS:
    _A_OFF[_n] = (_o, _o + _w)
    _o += _w
A_COLS = _o

TM_A = 512
TQ_ATT = 256
MLA_TQ = 512
MLA_TK = 512
MLA_GROUP = 4
SWA_TQ = 512
HG_ROWS = 256
TM_FFN = 512
TF_FFN = 1408
MOE_TB = 512
MOE_TF = 1792
TM_COMB = 256
TM_DISP = 512
VMEM_LIMIT = 56 * 1024 * 1024


def _cparams(sem, **kw):
    return pltpu.CompilerParams(dimension_semantics=sem, vmem_limit_bytes=VMEM_LIMIT, **kw)


def _const_spec(shape):
    nd = len(shape)
    return pl.BlockSpec(shape, lambda *_: (0,) * nd, pipeline_mode=pl.Buffered(1))


def _layernorm(v, g, b):
    mu = jnp.mean(v, axis=-1, keepdims=True)
    vc = v - mu
    var = jnp.mean(vc * vc, axis=-1, keepdims=True)
    return vc * lax.rsqrt(var + EPS) * g + b


def _dot(a, b):
    return jnp.dot(a, b, preferred_element_type=F32)


def _dot_nt(a, b):
    return lax.dot_general(a, b, (((1,), (1,)), ((), ())), preferred_element_type=F32)


def _split3(a):
    hi = a.astype(BF16)
    r = a - hi.astype(F32)
    mid = r.astype(BF16)
    lo = (r - mid.astype(F32)).astype(BF16)
    return hi, mid, lo


def _rope_kernel(pos_ref, freq_ref, c_ref, s_ref):
    lane = lax.broadcasted_iota(I32, pos_ref.shape, 1)
    ang = pos_ref[...] * freq_ref[...]
    rope = (lane >= MLA_NOPE) & (lane < MLA_NOPE + MLA_ROPE)
    first = lane < MLA_NOPE + MLA_ROPE // 2
    c_ref[...] = jnp.where(lane < MLA_NOPE, 1.0, jnp.where(rope, jnp.cos(ang), 0.0))
    sn = jnp.sin(ang)
    s_ref[...] = jnp.where(rope, jnp.where(first, -sn, sn), 0.0)


def _rope_tables(positions):
    n = positions.size
    half = MLA_ROPE // 2
    inv_freq = ROPE_THETA ** (-jnp.arange(half, dtype=F32) / half)
    freq = jnp.zeros((1, LANES), F32).at[0, MLA_NOPE:MLA_NOPE + MLA_ROPE].set(jnp.tile(inv_freq, 2))
    posb = jnp.broadcast_to(positions.reshape(n, 1).astype(F32), (n, LANES))
    tm = 1024
    return pl.pallas_call(
        _rope_kernel,
        grid=(n // tm,),
        in_specs=[pl.BlockSpec((tm, LANES), lambda i: (i, 0)), _const_spec((1, LANES))],
        out_specs=[pl.BlockSpec((tm, LANES), lambda i: (i, 0))] * 2,
        out_shape=[jax.ShapeDtypeStruct((n, LANES), F32)] * 2,
        compiler_params=_cparams(("parallel",)),
        name="rope_tables",
    )(posb, freq)


def _inproj_kernel(x_ref, w_ref, c_ref, s_ref, qn_ref, kvn_ref, wuqa_ref, wuqb_ref, wuk_ref, wuv_ref,
                   mq_ref, mk_ref, mv_ref, swq_ref, swk_ref, swv_ref, hg_ref, sbq_ref, sbk_ref, sbv_ref):
    h = _dot(x_ref[...].astype(BF16), w_ref[...])

    def cols(name):
        lo, hi = _A_OFF[name]
        return h[:, lo:hi]

    c = c_ref[...]
    s = s_ref[...]
    c4 = jnp.concatenate([c] * N_HEADS, axis=1)
    s4 = jnp.concatenate([s] * N_HEADS, axis=1)

    cq = cols('cq')
    cqn = (cq * lax.rsqrt(jnp.mean(cq * cq, axis=-1, keepdims=True) + EPS) * qn_ref[...]).astype(BF16)
    q = _dot(cqn, wuqa_ref[...]) * c4 + _dot(cqn, wuqb_ref[...]) * s4
    mq_ref[...] = (q * (MLA_SCALE * LOG2E)).astype(BF16)

    ckv = cols('ckv')
    ckvn = (ckv * lax.rsqrt(jnp.mean(ckv * ckv, axis=-1, keepdims=True) + EPS) * kvn_ref[...]).astype(BF16)
    krot = cols('kra') * c + cols('krb') * s
    mk_ref[...] = (_dot(ckvn, wuk_ref[...]) + jnp.concatenate([krot] * N_HEADS, axis=1)).astype(BF16)
    mv_ref[...] = _dot(ckvn, wuv_ref[...]).astype(BF16)

    swq_ref[...] = cols('swa_q').astype(BF16)
    swk_ref[...] = cols('swa_k').astype(BF16)
    swv_ref[...] = cols('swa_v').astype(BF16)
    hg_ref[...] = cols('hgrn')
    sbq_ref[...] = cols('sb_q').astype(BF16)
    sbk_ref[...] = cols('sb_k').astype(BF16)
    sbv_ref[...] = cols('sb_v').astype(BF16)


def _inproj_weights(w_in, w_uq, w_ukv):
    def seg(name, width):
        o = _IN_OFF[name]
        return w_in[:, o:o + width]

    kr = seg('mla_kr', MLA_ROPE)
    half = MLA_ROPE // 2
    z64 = jnp.zeros((D_MODEL, MLA_NOPE), F32)
    z32 = jnp.zeros((D_MODEL, LANES - MLA_NOPE - MLA_ROPE), F32)
    kra = jnp.concatenate([z64, kr, z32], axis=1)
    krb = jnp.concatenate([z64, kr[:, half:], kr[:, :half], z32], axis=1)
    swk = seg('swa_k', 128)
    swv = seg('swa_v', 128)
    dup = lambda t: jnp.concatenate([t[:, :64], t[:, :64], t[:, 64:], t[:, 64:]], axis=1)
    w_a = jnp.concatenate([
        seg('mla_cq', 256), seg('mla_ckv', 128), kra, krb,
        seg('swa_q', 256) * QK_SCALE, dup(swk), dup(swv),
        seg('hgrn', 1024), seg('sb_q', 256) * (QK_SCALE * LOG2E), seg('sb_k', 256), seg('sb_v', 256)], axis=1)

    qd = MLA_NOPE + MLA_ROPE
    zq = jnp.zeros((MLA_Q_LORA, LANES - qd), F32)
    zn = jnp.zeros((MLA_Q_LORA, MLA_NOPE), F32)
    qa, qb = [], []
    for hh in range(N_HEADS):
        nope = w_uq[:, hh * qd: hh * qd + MLA_NOPE]
        rope = w_uq[:, hh * qd + MLA_NOPE: (hh + 1) * qd]
        qa += [nope, rope, zq]
        qb += [zn, rope[:, half:], rope[:, :half], zq]
    wuqa = jnp.concatenate(qa, axis=1)
    wuqb = jnp.concatenate(qb, axis=1)
    lane = jnp.arange(N_HEADS * LANES) % LANES
    wuk = jnp.where(lane[None, :] < MLA_NOPE, w_ukv, 0.0)
    wuv = jnp.concatenate([w_ukv[:, hh * LANES + MLA_NOPE:(hh + 1) * LANES] for hh in range(N_HEADS)], axis=1)
    return tuple(t.astype(BF16) for t in (w_a, wuqa, wuqb, wuk, wuv))


def _inproj(x2d, wts, ctab, stab, q_norm, kv_norm):
    n = x2d.shape[0]
    w_a, wuqa, wuqb, wuk, wuv = wts
    tm = TM_A
    row = lambda w: pl.BlockSpec((tm, w), lambda i: (i, 0))
    out_w = (512, 512, 256, 256, 256, 256, 1024, 256, 256, 256)
    out_dt = (BF16, BF16, BF16, BF16, BF16, BF16, F32, BF16, BF16, BF16)
    return pl.pallas_call(
        _inproj_kernel,
        grid=(n // tm,),
        in_specs=[row(D_MODEL), _const_spec(w_a.shape), row(LANES), row(LANES),
                  _const_spec((1, MLA_Q_LORA)), _const_spec((1, MLA_KV_LORA)),
                  _const_spec(wuqa.shape), _const_spec(wuqb.shape), _const_spec(wuk.shape),
                  _const_spec(wuv.shape)],
        out_specs=[row(w) for w in out_w],
        out_shape=[jax.ShapeDtypeStruct((n, w), d) for w, d in zip(out_w, out_dt)],
        compiler_params=_cparams(("parallel",)),
        name="inproj",
    )(x2d, w_a, ctab, stab, q_norm.reshape(1, -1), kv_norm.reshape(1, -1), wuqa, wuqb, wuk, wuv)


def _half_mask(half):
    lane = lax.broadcasted_iota(I32, (1, LANES), 1)
    return (lane < HEAD_DIM) if half == 0 else (lane >= HEAD_DIM)


def _mla_kernel(q_ref, k_ref, v_ref, o_ref):
    tq = q_ref.shape[0]
    tk = MLA_TK
    nsub = tq // tk
    i = pl.program_id(1)
    row = lax.broadcasted_iota(I32, (tq, tk), 0)
    col = lax.broadcasted_iota(I32, (tq, tk), 1)
    ones = jnp.ones((1, LANES), BF16)

    def update(off, carry, heads, mask):
        ss = [_dot_nt(q_ref[:, hh * LANES:(hh + 1) * LANES], k_ref[pl.ds(off, tk), hh * LANES:(hh + 1) * LANES])
              for hh in heads]
        if mask is not None:
            ss = [jnp.where(mask, s, NEG_BIG) for s in ss]
        ms = [jnp.maximum(c[0], jnp.max(s, axis=-1, keepdims=True)) for c, s in zip(carry, ss)]
        pms = [jnp.exp2(s - m).astype(BF16) for s, m in zip(ss, ms)]
        new = []
        for n, hh in enumerate(heads):
            vb = v_ref[pl.ds(off, tk), (hh // 2) * LANES:(hh // 2 + 1) * LANES]
            vb = jnp.where(_half_mask(hh % 2), vb, ones)
            m, acc = carry[n]
            new.append((ms[n], jnp.exp2(m - ms[n]) * acc + _dot(pms[n], vb)))
        return tuple(new)

    accs = []
    for g in range(0, N_HEADS, MLA_GROUP):
        heads = tuple(range(g, g + MLA_GROUP))
        init = tuple((jnp.full((tq, 1), NEG_BIG, F32), jnp.zeros((tq, LANES), F32)) for _ in heads)
        carry = lax.fori_loop(0, i * nsub,
                              lambda j, c, heads=heads: update(pl.multiple_of(j * tk, tk), c, heads, None), init)
        for r in range(nsub):
            carry = update(pl.multiple_of(i * tq + r * tk, tk), carry, heads, col + r * tk <= row)
        accs += [c[1] for c in carry]
    outs = []
    for p in range(N_HEADS // 2):
        a0, a1 = accs[2 * p], accs[2 * p + 1]
        outs.append(jnp.where(_half_mask(0), a0 / a0[:, HEAD_DIM:HEAD_DIM + 1], a1 / a1[:, 0:1]))
    o_ref[...] = jnp.concatenate(outs, axis=1).astype(o_ref.dtype)


def _mla_attention(q, k, v, batch, seq):
    tq = MLA_TQ
    q3, k3, v3 = (t.reshape(batch, seq, t.shape[-1]) for t in (q, k, v))
    out = pl.pallas_call(
        _mla_kernel,
        grid=(batch, seq // tq),
        in_specs=[pl.BlockSpec((None, tq, 512), lambda b, i: (b, i, 0)),
                  pl.BlockSpec((None, seq, 512), lambda b, i: (b, 0, 0)),
                  pl.BlockSpec((None, seq, WIDTH), lambda b, i: (b, 0, 0))],
        out_specs=pl.BlockSpec((None, tq, WIDTH), lambda b, i: (b, i, 0)),
        out_shape=jax.ShapeDtypeStruct((batch, seq, WIDTH), BF16),
        compiler_params=_cparams(("parallel", "arbitrary")),
        name="mla_attention",
    )(q3, k3, v3)
    return out.reshape(batch * seq, WIDTH)


def _sb_kernel(q_ref, k_ref, v_ref, o_ref):
    tq = q_ref.shape[0]
    i = pl.program_id(1)
    row = lax.broadcasted_iota(I32, (tq, tq), 0)
    col = lax.broadcasted_iota(I32, (tq, tq), 1)
    strict = col < row
    later = (row > col).astype(BF16)
    qs = []
    for hh in range(N_HEADS):
        qp = q_ref[:, (hh // 2) * LANES:(hh // 2 + 1) * LANES]
        qs.append(jnp.where(_half_mask(hh % 2), qp, jnp.zeros_like(qp)))

    def block(j, carry, diag):
        off = pl.multiple_of(j * tq, tq)
        runs, accs = carry
        heads = range(N_HEADS)
        zs = [_dot_nt(qs[hh], k_ref[pl.ds(off, tq), (hh // 2) * LANES:(hh // 2 + 1) * LANES]) for hh in heads]
        lsps = [jnp.minimum(z, 0.0) - jnp.log2(1.0 + jnp.exp2(-jnp.abs(z))) for z in zs]
        lsns = [lsp - z for lsp, z in zip(lsps, zs)]
        if diag:
            lsns = [jnp.where(strict, t, 0.0) for t in lsns]
        his = [t.astype(BF16) for t in lsns]
        los = [(t - hi.astype(F32)).astype(BF16) for t, hi in zip(lsns, his)]
        rems = [_dot(hi, later) + _dot(lo, later) for hi, lo in zip(his, los)]
        args = [lsps[hh] + rems[hh] + runs[hh] for hh in heads]
        if diag:
            args = [jnp.where(strict, t, NEG_BIG) for t in args]
        probs = [jnp.exp2(t).astype(BF16) for t in args]
        new_runs = tuple(runs[hh] + rems[hh][:, 0:1] + lsns[hh][:, 0:1] for hh in heads)
        new_accs = list(accs)
        for hh in heads:
            p = hh // 2
            vb = v_ref[pl.ds(off, tq), p * LANES:(p + 1) * LANES]
            vb = jnp.where(_half_mask(hh % 2), vb, jnp.zeros_like(vb))
            new_accs[p] = new_accs[p] + _dot(probs[hh], vb)
        return new_runs, tuple(new_accs)

    init = (tuple(jnp.zeros((tq, 1), F32) for _ in range(N_HEADS)),
            tuple(jnp.zeros((tq, LANES), F32) for _ in range(N_HEADS // 2)))
    def still_active(runs):
        top = functools.reduce(jnp.maximum, runs)
        return (jnp.max(top) > SB_RUN_FLOOR).astype(I32)

    runs, accs = block(i, init, True)

    def cond(c):
        return (c[0] < i) & (c[1] > 0)

    def body(c):
        jj, _, runs, accs = c
        runs, accs = block(i - 1 - jj, (runs, accs), False)
        return jj + 1, still_active(runs), runs, accs

    _, _, _, accs = lax.while_loop(cond, body, (jnp.int32(0), still_active(runs), runs, accs))
    o_ref[...] = jnp.concatenate(accs, axis=1).astype(o_ref.dtype)


def _sb_attention(q, k, v, batch, seq):
    tq = TQ_ATT
    q3, k3, v3 = (t.reshape(batch, seq, WIDTH) for t in (q, k, v))
    out = pl.pallas_call(
        _sb_kernel,
        grid=(batch, seq // tq),
        in_specs=[pl.BlockSpec((None, tq, WIDTH), lambda b, i: (b, i, 0)),
                  pl.BlockSpec((None, seq, WIDTH), lambda b, i: (b, 0, 0)),
                  pl.BlockSpec((None, seq, WIDTH), lambda b, i: (b, 0, 0))],
        out_specs=pl.BlockSpec((None, tq, WIDTH), lambda b, i: (b, i, 0)),
        out_shape=jax.ShapeDtypeStruct((batch, seq, WIDTH), BF16),
        compiler_params=_cparams(("parallel", "arbitrary")),
        name="stick_breaking",
    )(q3, k3, v3)
    return out.reshape(batch * seq, WIDTH)


def _rel_bucket(dist):
    exact = REL_BUCKETS // 2
    n = jnp.maximum(dist, 0)
    nf = jnp.maximum(n, 1).astype(F32)
    large = exact + (jnp.log(nf / exact) / math.log(REL_MAX_DIST / exact) * (REL_BUCKETS - exact)).astype(I32)
    large = jnp.clip(large, 0, REL_BUCKETS - 1)
    return jnp.where(n < exact, n, large)


def _swa_kernel(sink_ref, tab_ref, q_ref, kc_ref, kh_ref, vc_ref, vh_ref, pq_ref, pkc_ref, pkh_ref, o_ref):
    w = SWA_WINDOW
    step = pl.program_id(1)
    row = lax.broadcasted_iota(I32, (w, w), 0)
    col = lax.broadcasted_iota(I32, (w, w), 1)
    valid_c = col <= row
    valid_p = col > row
    tabs = [jnp.broadcast_to(tab_ref[hh:hh + 1, :], (w, LANES)) for hh in range(N_HEADS)]
    ones = jnp.ones((1, LANES), BF16)
    nsub = q_ref.shape[0] // w
    chains = [(r, hh) for r in range(nsub) for hh in range(N_HEADS)]

    def keys(ref, halo_ref, r, hh):
        sl = slice((hh // 2) * LANES, (hh // 2 + 1) * LANES)
        cur = ref[r * w:(r + 1) * w, sl]
        prev = ref[(r - 1) * w:r * w, sl] if r else halo_ref[:, sl]
        return cur, prev

    buckets = []
    for r in range(nsub):
        pq = pq_ref[r * w:(r + 1) * w, :]
        pk_prev = pkc_ref[:, (r - 1) * w:r * w] if r else pkh_ref[...]
        buckets.append((_rel_bucket(pq - pkc_ref[:, r * w:(r + 1) * w]), _rel_bucket(pq - pk_prev)))
    logits = []
    for r, hh in chains:
        qp = q_ref[r * w:(r + 1) * w, (hh // 2) * LANES:(hh // 2 + 1) * LANES]
        qh = jnp.where(_half_mask(hh % 2), qp, jnp.zeros_like(qp))
        kc, kp = keys(kc_ref, kh_ref, r, hh)
        logits.append((_dot_nt(qh, kc), _dot_nt(qh, kp)))
    masked = []
    for (r, hh), (lc, lp) in zip(chains, logits):
        lc = jnp.where(valid_c, lc + jnp.take_along_axis(tabs[hh], buckets[r][0], axis=1), NEG_BIG)
        lp = lp + jnp.take_along_axis(tabs[hh], buckets[r][1], axis=1)
        lp = jnp.where(valid_p if r else valid_p & (step > 0), lp, NEG_BIG)
        masked.append((lc, lp))
    maxes = [jnp.maximum(jnp.maximum(jnp.max(lc, axis=-1, keepdims=True), jnp.max(lp, axis=-1, keepdims=True)),
                         sink_ref[hh]) for (r, hh), (lc, lp) in zip(chains, masked)]
    probs = [(jnp.exp(lc - m).astype(BF16), jnp.exp(lp - m).astype(BF16)) for (lc, lp), m in zip(masked, maxes)]
    outs = {}
    for (r, hh), (ec, ep), m in zip(chains, probs, maxes):
        vc, vp = keys(vc_ref, vh_ref, r, hh)
        mine = _half_mask(hh % 2)
        acc = _dot(ec, jnp.where(mine, vc, ones)) + _dot(ep, jnp.where(mine, vp, ones))
        den = (acc[:, 0:1] if hh % 2 else acc[:, HEAD_DIM:HEAD_DIM + 1]) + jnp.exp(sink_ref[hh] - m)
        outs[(r, hh)] = acc / den
    for r in range(nsub):
        pairs = [jnp.where(_half_mask(0), outs[(r, 2 * p)], outs[(r, 2 * p + 1)]) for p in range(N_HEADS // 2)]
        o_ref[r * w:(r + 1) * w, :] = jnp.concatenate(pairs, axis=1).astype(o_ref.dtype)


def _swa_attention(q, k, v, positions, sinks, rel_table, batch, seq):
    w = SWA_WINDOW
    tq = SWA_TQ
    per = tq // w
    q3, k3, v3 = (t.reshape(batch, seq, WIDTH) for t in (q, k, v))
    pcol = positions.reshape(batch, seq, 1)
    prow = positions.reshape(batch, 1, seq)
    tab = jnp.zeros((N_HEADS, LANES), F32).at[:, :REL_BUCKETS].set(rel_table.astype(F32).T)
    cur = lambda b, n: (b, n, 0)
    halo = lambda b, n: (b, jnp.maximum(n * per - 1, 0), 0)
    out = pl.pallas_call(
        _swa_kernel,
        grid=(batch, seq // tq),
        in_specs=[pl.BlockSpec(memory_space=pltpu.SMEM), _const_spec((N_HEADS, LANES)),
                  pl.BlockSpec((None, tq, WIDTH), cur),
                  pl.BlockSpec((None, tq, WIDTH), cur), pl.BlockSpec((None, w, WIDTH), halo),
                  pl.BlockSpec((None, tq, WIDTH), cur), pl.BlockSpec((None, w, WIDTH), halo),
                  pl.BlockSpec((None, tq, 1), cur),
                  pl.BlockSpec((None, 1, tq), lambda b, n: (b, 0, n)),
                  pl.BlockSpec((None, 1, w), lambda b, n: (b, 0, jnp.maximum(n * per - 1, 0)))],
        out_specs=pl.BlockSpec((None, tq, WIDTH), cur),
        out_shape=jax.ShapeDtypeStruct((batch, seq, WIDTH), BF16),
        compiler_params=_cparams(("parallel", "arbitrary")),
        name="swa_attention",
    )(sinks.astype(F32), tab, q3, k3, k3, v3, v3, pcol, prow, prow)
    return out.reshape(batch * seq, WIDTH)


def _hgrn_kernel(hg_ref, lb_ref, nw_ref, o_ref, state_ref):
    c = HGRN_CHUNK
    blk = HGRN_BLOCK

    @pl.when(pl.program_id(1) == 0)
    def _():
        state_ref[...] = jnp.zeros_like(state_ref)

    r64 = lax.broadcasted_iota(I32, (c, c), 0)
    c64 = lax.broadcasted_iota(I32, (c, c), 1)
    incl = (c64 <= r64).astype(BF16)
    ra = lax.broadcasted_iota(I32, (WIDTH, WIDTH), 0) // HEAD_DIM
    ca = lax.broadcasted_iota(I32, (WIDTH, WIDTH), 1) // HEAD_DIM
    same_head = ra == ca
    seg = same_head.astype(BF16)
    ones_cols = jnp.ones((c, LANES), BF16)
    trow = lax.broadcasted_iota(I32, (blk, WIDTH), 0)
    caps = [jnp.where(trow >= s_i, 0.0, NEG_BIG) for s_i in range(blk)]
    lane_head = lax.broadcasted_iota(I32, (1, WIDTH), 1) // HEAD_DIM
    lb = lb_ref[...]
    nw = nw_ref[...]
    dn0 = (((0,), (0,)), ((), ()))

    for ch in range(hg_ref.shape[0] // c):
        rows = slice(ch * c, (ch + 1) * c)
        qraw = hg_ref[rows, 0:WIDTH]
        fraw = hg_ref[rows, WIDTH:2 * WIDTH]
        v = hg_ref[rows, 2 * WIDTH:3 * WIDTH]
        graw = hg_ref[rows, 3 * WIDTH:4 * WIDTH]
        qf = qraw * jax.nn.sigmoid(qraw)
        forget = lb + (1.0 - lb) * jax.nn.sigmoid(fraw)
        lf = jnp.log(forget)
        kk = 1.0 - forget
        gate = graw * jax.nn.sigmoid(graw)
        vb = v.astype(BF16)

        lf3 = _split3(lf)
        bc = _dot(incl, lf3[0]) + _dot(incl, lf3[1]) + _dot(incl, lf3[2])
        b_last = bc[c - 1:c, :]
        tot_col = sum(lax.dot_general(t, ones_cols, dn0, preferred_element_type=F32) for t in lf3)
        decay_col = jnp.exp(jnp.concatenate([tot_col, tot_col], axis=1))

        state = state_ref[...]
        o_inter = _dot((qf * jnp.exp(bc)).astype(BF16), state.astype(BF16))

        def before(qa, qb, ka, kb):
            ref = bc[kb - 1:kb, :]
            qt = qf[qa:qb] * jnp.exp(bc[qa:qb] - ref)
            kt = (kk[ka:kb] * jnp.exp(ref - bc[ka:kb])).astype(BF16)
            qs = jnp.concatenate([jnp.where(lane_head == hh, qt, 0.0) for hh in range(N_HEADS)], axis=0)
            att = _dot_nt(qs.astype(BF16), kt)
            mix = _dot(att.astype(BF16), vb[ka:kb])
            nq = qb - qa
            return sum(jnp.where(lane_head == hh, mix[hh * nq:(hh + 1) * nq], 0.0) for hh in range(N_HEADS))

        bc2 = bc * LOG2E

        def inside(a):
            b2 = bc2[a:a + blk]
            qb_ = qf[a:a + blk]
            ws = []
            for s_i in range(blk):
                e = jnp.exp2(jnp.minimum(b2 - b2[s_i:s_i + 1, :], caps[s_i]))
                ws.append((qb_ * kk[a + s_i:a + s_i + 1, :] * e).astype(BF16))
            att = _dot(jnp.concatenate(ws, axis=0), seg)
            return sum(att[s_i * blk:(s_i + 1) * blk] * v[a + s_i:a + s_i + 1, :] for s_i in range(blk))

        half = c // 2
        far = before(half, c, 0, half)
        intra = [inside(0),
                 inside(blk) + before(blk, half, 0, blk),
                 inside(half) + far[:blk],
                 inside(half + blk) + far[blk:] + before(half + blk, c, half, half + blk)]
        o = o_inter + jnp.concatenate(intra, axis=0)

        khat = (kk * jnp.exp(b_last - bc)).astype(BF16)
        upd = lax.dot_general(khat, vb, dn0, preferred_element_type=F32)
        state_ref[...] = decay_col * state + jnp.where(same_head, upd, 0.0)

        o2 = _split3(o * o)
        ms = (_dot(o2[0], seg) + _dot(o2[1], seg)) * (1.0 / HEAD_DIM)
        o_ref[rows, :] = (o * lax.rsqrt(ms + EPS) * nw * gate).astype(o_ref.dtype)


def _hgrn(hg, lower_bound, norm_w, batch, seq):
    rows = HG_ROWS
    hg3 = hg.reshape(batch, seq, 4 * WIDTH)
    out = pl.pallas_call(
        _hgrn_kernel,
        grid=(batch, seq // rows),
        in_specs=[pl.BlockSpec((None, rows, 4 * WIDTH), lambda b, i: (b, i, 0)),
                  _const_spec((1, WIDTH)), _const_spec((1, WIDTH))],
        out_specs=pl.BlockSpec((None, rows, WIDTH), lambda b, i: (b, i, 0)),
        out_shape=jax.ShapeDtypeStruct((batch, seq, WIDTH), BF16),
        scratch_shapes=[pltpu.VMEM((WIDTH, WIDTH), F32)],
        compiler_params=_cparams(("parallel", "arbitrary")),
        name="hgrn2",
    )(hg3, lower_bound.reshape(1, WIDTH).astype(F32), norm_w.reshape(1, WIDTH).astype(F32))
    return out.reshape(batch * seq, WIDTH)


def _merge_kernel(x_ref, y0_ref, y1_ref, y2_ref, y3_ref, wg_ref, wb_ref, wo_ref, g_ref, b_ref, o_ref):
    x = x_ref[...]
    xb = x.astype(BF16)
    merged = jnp.zeros(x.shape, F32)
    for nbr, y_ref in enumerate((y0_ref, y1_ref, y2_ref, y3_ref)):
        gate = jax.nn.sigmoid(_dot(xb, wg_ref[:, nbr * D_MODEL:(nbr + 1) * D_MODEL]))
        merged = merged + gate * _dot(y_ref[...], wb_ref[nbr])
    y = _dot(merged.astype(BF16), wo_ref[...])
    o_ref[...] = _layernorm(ALPHA * x + y, g_ref[...], b_ref[...])


def _merge(x2d, ys, wg, wb, wo, g, b):
    n = x2d.shape[0]
    tm = TM_A
    row = lambda w: pl.BlockSpec((tm, w), lambda i: (i, 0))
    return pl.pallas_call(
        _merge_kernel,
        grid=(n // tm,),
        in_specs=[row(D_MODEL)] + [row(WIDTH)] * 4 +
                 [_const_spec(wg.shape), _const_spec(wb.shape), _const_spec(wo.shape),
                  _const_spec((1, D_MODEL)), _const_spec((1, D_MODEL))],
        out_specs=row(D_MODEL),
        out_shape=jax.ShapeDtypeStruct((n, D_MODEL), F32),
        compiler_params=_cparams(("parallel",)),
        name="merge_outproj_ln",
    )(x2d, *ys, wg, wb, wo, g.reshape(1, -1), b.reshape(1, -1))


def _memkv_kernel(m_ref, w_ref, k_ref, v_ref):
    kv = _dot(m_ref[...].astype(BF16), w_ref[...])
    k_ref[...] = kv[:, :WIDTH].astype(BF16)
    v_ref[...] = kv[:, WIDTH:].astype(BF16)


def _memkv(mem, wkv):
    batch, m, _ = mem.shape
    return pl.pallas_call(
        _memkv_kernel,
        grid=(batch,),
        in_specs=[pl.BlockSpec((None, m, D_MODEL), lambda b: (b, 0, 0)), _const_spec(wkv.shape)],
        out_specs=[pl.BlockSpec((None, m, WIDTH), lambda b: (b, 0, 0))] * 2,
        out_shape=[jax.ShapeDtypeStruct((batch, m, WIDTH), BF16)] * 2,
        compiler_params=_cparams(("parallel",)),
        name="mem_kv",
    )(mem, wkv)


def _xattn_kernel(x_ref, wq_ref, k_ref, v_ref, wo_ref, g_ref, b_ref, o_ref):
    x = x_ref[...]
    q = _dot(x.astype(BF16), wq_ref[...]).astype(BF16)
    k = k_ref[...]
    v = v_ref[...]
    lane = lax.broadcasted_iota(I32, (1, WIDTH), 1) // HEAD_DIM
    heads = range(N_HEADS)
    ss = [_dot_nt(jnp.where(lane == hh, q, jnp.zeros_like(q)), k) for hh in heads]
    es = [jnp.exp(s - jnp.max(s, axis=-1, keepdims=True)) for s in ss]
    ps = [(e / jnp.sum(e, axis=-1, keepdims=True)).astype(BF16) for e in es]
    o = jnp.zeros((x.shape[0], WIDTH), F32)
    for hh in heads:
        o = o + jnp.where(lane == hh, _dot(ps[hh], v), 0.0)
    y = _dot(o.astype(BF16), wo_ref[...])
    o_ref[...] = _layernorm(ALPHA * x + y, g_ref[...], b_ref[...])


def _xattn(x2d, wq, k, v, wo, g, b, batch, seq):
    tm = TM_A
    m = k.shape[1]
    x3 = x2d.reshape(batch, seq, D_MODEL)
    row = pl.BlockSpec((None, tm, D_MODEL), lambda bb, i: (bb, i, 0))
    kv_spec = pl.BlockSpec((None, m, WIDTH), lambda bb, i: (bb, 0, 0))
    out = pl.pallas_call(
        _xattn_kernel,
        grid=(batch, seq // tm),
        in_specs=[row, _const_spec(wq.shape), kv_spec, kv_spec, _const_spec(wo.shape),
                  _const_spec((1, D_MODEL)), _const_spec((1, D_MODEL))],
        out_specs=row,
        out_shape=jax.ShapeDtypeStruct((batch, seq, D_MODEL), F32),
        compiler_params=_cparams(("parallel", "parallel")),
        name="mem_xattn_ln",
    )(x3, wq, k, v, wo, g.reshape(1, -1), b.reshape(1, -1))
    return out.reshape(batch * seq, D_MODEL)


def _ffn_kernel(x_ref, w13_ref, w2_ref, g_ref, b_ref, o_ref):
    x = x_ref[...]
    xb = x.astype(BF16)
    y = None
    for h in range(F_DENSE // TF_FFN):
        lo = h * TF_FFN
        a = _dot(xb, w13_ref[:, lo:lo + TF_FFN])
        gate = _dot(xb, w13_ref[:, F_DENSE + lo:F_DENSE + lo + TF_FFN])
        part = _dot((a * jax.nn.sigmoid(a) * gate).astype(BF16), w2_ref[lo:lo + TF_FFN, :])
        y = part if y is None else y + part
    o_ref[...] = _layernorm(ALPHA * x + y, g_ref[...], b_ref[...])


def _ffn(x2d, w13, w2, g, b):
    n = x2d.shape[0]
    tm = TM_FFN
    return pl.pallas_call(
        _ffn_kernel,
        grid=(n // tm,),
        in_specs=[pl.BlockSpec((tm, D_MODEL), lambda i: (i, 0)),
                  _const_spec(w13.shape), _const_spec(w2.shape),
                  _const_spec((1, D_MODEL)), _const_spec((1, D_MODEL))],
        out_specs=pl.BlockSpec((tm, D_MODEL), lambda i: (i, 0)),
        out_shape=jax.ShapeDtypeStruct((n, D_MODEL), F32),
        compiler_params=_cparams(("parallel",)),
        name="ffn_ln",
    )(x2d, w13, w2, g.reshape(1, -1), b.reshape(1, -1))


def _router_kernel(x_ref, r_ref, info_ref, wts_ref, cnt_ref, carry_ref):
    tm = x_ref.shape[0]

    @pl.when(pl.program_id(0) == 0)
    def _():
        carry_ref[...] = jnp.zeros_like(carry_ref)

    logits = jnp.dot(x_ref[...], r_ref[...], precision=lax.Precision.HIGHEST, preferred_element_type=F32)
    lane = lax.broadcasted_iota(I32, (tm, LANES), 1)
    lg = jnp.where(lane < N_EXPERTS, logits, -jnp.inf)
    m1 = jnp.max(lg, axis=-1, keepdims=True)
    i1 = jnp.min(jnp.where(lg == m1, lane, LANES), axis=-1, keepdims=True)
    lg2 = jnp.where(lane == i1, -jnp.inf, lg)
    m2 = jnp.max(lg2, axis=-1, keepdims=True)
    i2 = jnp.min(jnp.where(lg2 == m2, lane, LANES), axis=-1, keepdims=True)
    e = jnp.exp(m2 - m1)
    w1 = 1.0 / (1.0 + e)
    w2 = e / (1.0 + e)
    sel1 = lane == i1
    sel2 = lane == i2
    chosen = jnp.where(sel1 | sel2, 1.0, 0.0)
    row = lax.broadcasted_iota(I32, (tm, tm), 0)
    col = lax.broadcasted_iota(I32, (tm, tm), 1)
    before = (col < row).astype(BF16)
    ranks = _dot(before, chosen.astype(BF16)) + carry_ref[...]
    r1 = jnp.sum(jnp.where(sel1, ranks, 0.0), axis=-1, keepdims=True).astype(I32)
    r2 = jnp.sum(jnp.where(sel2, ranks, 0.0), axis=-1, keepdims=True).astype(I32)
    carry_ref[...] = carry_ref[...] + jnp.sum(chosen, axis=0, keepdims=True)
    info_ref[...] = jnp.where(lane == 0, i1, jnp.where(lane == 1, i2, jnp.where(lane == 2, r1,
                              jnp.where(lane == 3, r2, 0))))
    wts_ref[...] = jnp.where(lane == 0, w1, jnp.where(lane == 1, w2, 0.0))
    cnt_ref[...] = carry_ref[...]


def _router(x2d, router):
    n = x2d.shape[0]
    tm = TM_A
    r_pad = jnp.zeros((D_MODEL, LANES), F32).at[:, :N_EXPERTS].set(router.astype(F32))
    row = pl.BlockSpec((tm, LANES), lambda i: (i, 0))
    return pl.pallas_call(
        _router_kernel,
        grid=(n // tm,),
        in_specs=[pl.BlockSpec((tm, D_MODEL), lambda i: (i, 0)), _const_spec(r_pad.shape)],
        out_specs=[row, row, pl.BlockSpec((1, LANES), lambda i: (0, 0))],
        out_shape=[jax.ShapeDtypeStruct((n, LANES), I32), jax.ShapeDtypeStruct((n, LANES), F32),
                   jax.ShapeDtypeStruct((1, LANES), F32)],
        scratch_shapes=[pltpu.VMEM((1, LANES), F32)],
        compiler_params=_cparams(("arbitrary",)),
        name="moe_router",
    )(x2d, r_pad)


def _dispatch_kernel(pad_ref, dest_ref, x_ref, xb_hbm, stage_ref, sems):
    tm = x_ref.shape[0]
    i = pl.program_id(0)
    last = pl.num_programs(0) - 1
    slot = i % 2

    def wait_step(s):
        for _ in range(2):
            pltpu.make_async_copy(stage_ref.at[s], xb_hbm.at[pl.ds(0, tm), :], sems.at[s]).wait()

    @pl.when(i >= 2)
    def _():
        wait_step(slot)

    stage_ref[slot] = x_ref[...]

    def issue(r, c):
        for k in range(2):
            pltpu.make_async_copy(stage_ref.at[slot, pl.ds(r, 1), :],
                                  xb_hbm.at[pl.ds(dest_ref[0, 2 * r + k], 1), :], sems.at[slot]).start()
        return c
    lax.fori_loop(0, tm, issue, 0)

    @pl.when(i == last)
    def _():
        def fill(e, c):
            def one(s, c2):
                pltpu.make_async_copy(stage_ref.at[slot, pl.ds(0, 1), :], xb_hbm.at[pl.ds(s, 1), :],
                                      sems.at[2]).start()
                return c2

            def done(s, c2):
                pltpu.make_async_copy(stage_ref.at[slot, pl.ds(0, 1), :], xb_hbm.at[pl.ds(0, 1), :],
                                      sems.at[2]).wait()
                return c2
            lax.fori_loop(pad_ref[0, e], pad_ref[1, e], one, 0)
            lax.fori_loop(pad_ref[0, e], pad_ref[1, e], done, 0)
            return c
        lax.fori_loop(0, pad_ref.shape[1], fill, 0)
        wait_step(slot)

        @pl.when(last >= 1)
        def _():
            wait_step(1 - slot)


def _dispatch(x2d, dest, pads, nblk):
    n = x2d.shape[0]
    tm = TM_DISP
    nt = n // tm
    grid_spec = pltpu.PrefetchScalarGridSpec(
        num_scalar_prefetch=1,
        grid=(nt,),
        in_specs=[pl.BlockSpec((None, 1, 2 * tm), lambda i, pads: (i, 0, 0), memory_space=pltpu.SMEM),
                  pl.BlockSpec((tm, D_MODEL), lambda i, pads: (i, 0))],
        out_specs=pl.BlockSpec(memory_space=pl.ANY),
        scratch_shapes=[pltpu.VMEM((2, tm, D_MODEL), F32), pltpu.SemaphoreType.DMA((3,))],
    )
    return pl.pallas_call(
        _dispatch_kernel,
        grid_spec=grid_spec,
        out_shape=jax.ShapeDtypeStruct((nblk * MOE_TB, D_MODEL), F32),
        compiler_params=_cparams(("arbitrary",), disable_bounds_checks=True),
        name="moe_dispatch",
    )(pads, dest.reshape(nt, 1, 2 * tm), x2d)


def _expert_kernel(nused_ref, bexp_ref, x_ref, w1_ref, w3_ref, w2_ref, o_ref, acc_ref):
    f = pl.program_id(1)

    @pl.when(pl.program_id(0) < nused_ref[0])
    def _():
        xb = x_ref[...].astype(BF16)
        a = _dot(xb, w1_ref[...])
        gate = _dot(xb, w3_ref[...])
        part = _dot((a * jax.nn.sigmoid(a) * gate).astype(BF16), w2_ref[...])

        @pl.when(f == 0)
        def _():
            acc_ref[...] = part

        @pl.when(f > 0)
        def _():
            acc_ref[...] += part

        @pl.when(f == pl.num_programs(1) - 1)
        def _():
            o_ref[...] = acc_ref[...]

    @pl.when(pl.program_id(0) >= nused_ref[0])
    def _():
        o_ref[...] = jnp.zeros_like(o_ref)


def _experts(xb, w13, w2, nused, blk_exp, nblk):
    tb, tf = MOE_TB, MOE_TF
    nf = F_EXPERT // tf

    def blk(i, nu):
        return jnp.maximum(jnp.minimum(i, nu[0] - 1), 0)

    def ftile(i, f, nu):
        return jnp.where(i < nu[0], f, nf - 1)

    grid_spec = pltpu.PrefetchScalarGridSpec(
        num_scalar_prefetch=2,
        grid=(nblk, nf),
        in_specs=[pl.BlockSpec((tb, D_MODEL), lambda i, f, nu, be: (blk(i, nu), 0)),
                  pl.BlockSpec((None, D_MODEL, tf), lambda i, f, nu, be: (be[blk(i, nu)], 0, ftile(i, f, nu))),
                  pl.BlockSpec((None, D_MODEL, tf), lambda i, f, nu, be: (be[blk(i, nu)], 0, nf + ftile(i, f, nu))),
                  pl.BlockSpec((None, tf, D_MODEL), lambda i, f, nu, be: (be[blk(i, nu)], ftile(i, f, nu), 0))],
        out_specs=pl.BlockSpec((tb, D_MODEL), lambda i, f, nu, be: (i, 0)),
        scratch_shapes=[pltpu.VMEM((tb, D_MODEL), F32)],
    )
    return pl.pallas_call(
        _expert_kernel,
        grid_spec=grid_spec,
        out_shape=jax.ShapeDtypeStruct((nblk * tb, D_MODEL), F32),
        compiler_params=_cparams(("arbitrary", "arbitrary")),
        name="moe_experts",
    )(nused, blk_exp, xb, w13, w13, w2)


def _combine_kernel(dest_ref, nxt_ref, y_hbm, x_ref, wts_ref, g_ref, b_ref, o_ref, buf_ref, sems):
    tm = x_ref.shape[0]
    i = pl.program_id(0)
    slot = i % 2

    def gather(idx_ref, s):
        def issue(r, c):
            for k in range(2):
                pltpu.make_async_copy(y_hbm.at[pl.ds(idx_ref[0, 2 * r + k], 1), :],
                                      buf_ref.at[s, k, pl.ds(r, 1), :], sems.at[s]).start()
            return c
        lax.fori_loop(0, tm, issue, 0)

    @pl.when(i == 0)
    def _():
        gather(dest_ref, slot)

    @pl.when(i + 1 < pl.num_programs(0))
    def _():
        gather(nxt_ref, 1 - slot)

    for k in range(2):
        pltpu.make_async_copy(y_hbm.at[pl.ds(0, tm), :], buf_ref.at[slot, k], sems.at[slot]).wait()
    wts = wts_ref[...]
    y = wts[:, 0:1] * buf_ref[slot, 0] + wts[:, 1:2] * buf_ref[slot, 1]
    o_ref[...] = _layernorm(ALPHA * x_ref[...] + y, g_ref[...], b_ref[...])


def _combine(yb, dest, x2d, wts, g, b):
    n = x2d.shape[0]
    tm = TM_COMB
    nt = n // tm
    row = lambda w: pl.BlockSpec((tm, w), lambda i: (i, 0))
    dest3 = dest.reshape(nt, 1, 2 * tm)
    return pl.pallas_call(
        _combine_kernel,
        grid=(nt,),
        in_specs=[pl.BlockSpec((None, 1, 2 * tm), lambda i: (i, 0, 0), memory_space=pltpu.SMEM),
                  pl.BlockSpec((None, 1, 2 * tm), lambda i: (jnp.minimum(i + 1, nt - 1), 0, 0),
                               memory_space=pltpu.SMEM),
                  pl.BlockSpec(memory_space=pl.ANY), row(D_MODEL), row(LANES),
                  _const_spec((1, D_MODEL)), _const_spec((1, D_MODEL))],
        out_specs=row(D_MODEL),
        out_shape=jax.ShapeDtypeStruct((n, D_MODEL), F32),
        scratch_shapes=[pltpu.VMEM((2, 2, tm, D_MODEL), F32), pltpu.SemaphoreType.DMA((2,))],
        compiler_params=_cparams(("arbitrary",), disable_bounds_checks=True),
        name="moe_combine_ln",
    )(dest3, dest3, yb, x2d, wts, g.reshape(1, -1), b.reshape(1, -1))


def _moe(x2d, router, w13, w2, g, b):
    n = x2d.shape[0]
    tb = MOE_TB
    info, wts, cnt = _router(x2d, router)
    idx = info[:, 0:2]
    rank = info[:, 2:4]
    counts = cnt[0, :N_EXPERTS].astype(I32)
    padded = (counts + tb - 1) // tb * tb
    pend = jnp.cumsum(padded)
    pstart = pend - padded
    dest = (pstart[idx] + rank).astype(I32).reshape(-1)
    nblk = (2 * n) // tb + N_EXPERTS
    pads = jnp.stack([jnp.append(pstart + counts, pend[-1]), jnp.append(pend, nblk * tb)]).astype(I32)
    nused = (pend[-1] // tb).astype(I32).reshape(1)
    blk_exp = jnp.minimum(jnp.searchsorted(pend, jnp.arange(nblk, dtype=I32) * tb, side='right'),
                          N_EXPERTS - 1).astype(I32)
    xb = _dispatch(x2d, dest, pads, nblk)
    yb = _experts(xb, w13, w2, nused, blk_exp, nblk)
    return _combine(yb, dest, x2d, wts, g, b)


def kernel(x, mem, positions, rel_bias_table, hgrn_lb_logits, w_in, mla_q_norm, mla_w_uq, mla_kv_norm, mla_w_ukv, swa_sinks, hgrn_norm, w_branch, w_out, ln_g, ln_b, xa_wq, xa_wkv, xa_wo, ffn_w13, ffn_w2, moe_router, moe_w13, moe_w2):
    batch, seq, _ = x.shape
    n = batch * seq
    sm = jax.nn.softmax(hgrn_lb_logits.astype(F32), axis=0)
    lower_bounds = jnp.cumsum(sm, axis=0) - sm[0]
    ctab, stab = _rope_tables(positions)
    xc = x.reshape(n, D_MODEL)
    for l in range(DEPTH):
        wts = _inproj_weights(w_in[l], mla_w_uq[l], mla_w_ukv[l])
        mq, mk, mv, swq, swk, swv, hg, sbq, sbk, sbv = _inproj(xc, wts, ctab, stab, mla_q_norm[l], mla_kv_norm[l])
        y_mla = _mla_attention(mq, mk, mv, batch, seq)
        y_swa = _swa_attention(swq, swk, swv, positions, swa_sinks[l], rel_bias_table, batch, seq)
        y_hg = _hgrn(hg, lower_bounds[l], hgrn_norm[l], batch, seq)
        y_sb = _sb_attention(sbq, sbk, sbv, batch, seq)
        go = _IN_OFF['gates']
        xc = _merge(xc, (y_mla, y_swa, y_hg, y_sb), w_in[l][:, go:].astype(BF16), w_branch[l].astype(BF16),
                    w_out[l].astype(BF16), ln_g[l, 0], ln_b[l, 0])
        mk_, mv_ = _memkv(mem, xa_wkv[l].astype(BF16))
        xc = _xattn(xc, (xa_wq[l] * QK_SCALE).astype(BF16), mk_, mv_, xa_wo[l].astype(BF16),
                    ln_g[l, 1], ln_b[l, 1], batch, seq)
        if l % 2 == 0:
            xc = _ffn(xc, ffn_w13[l // 2].astype(BF16), ffn_w2[l // 2].astype(BF16), ln_g[l, 2], ln_b[l, 2])
        else:
            xc = _moe(xc, moe_router[l // 2], moe_w13[l // 2].astype(BF16), moe_w2[l // 2].astype(BF16),
                      ln_g[l, 2], ln_b[l, 2])
    return xc.reshape(batch, seq, D_MODEL)
```

```python
import functools
import math

import jax
import jax.numpy as jnp
from jax import lax
from jax.experimental import pallas as pl
from jax.experimental.pallas import tpu as pltpu

F32 = jnp.float32
BF16 = jnp.bfloat16
I32 = jnp.int32

D_MODEL = 1024
DEPTH = 2
EPS = 1e-5
NEG_BIG = -1e30
LANES = 128
HEAD_DIM = 64
N_HEADS = 4
WIDTH = N_HEADS * HEAD_DIM

MLA_Q_LORA = 256
MLA_KV_LORA = 128
MLA_NOPE = 64
MLA_ROPE = 32
ROPE_THETA = 10000.0
MLA_SCALE = (MLA_NOPE + MLA_ROPE) ** -0.5
LOG2E = math.log2(math.e)
QK_SCALE = HEAD_DIM ** -0.5

SB_RUN_FLOOR = -150.0
SWA_WINDOW = 128
REL_BUCKETS = 32
REL_MAX_DIST = 128
HGRN_CHUNK = 64
HGRN_BLOCK = 16
N_EXPERTS = 8
F_DENSE = 2816
F_EXPERT = 3584
ALPHA = (2 * DEPTH) ** 0.25

_IN_SPLITS = (('mla_cq', 256), ('mla_ckv', 128), ('mla_kr', 32), ('swa_q', 256), ('swa_k', 128),
              ('swa_v', 128), ('hgrn', 1024), ('sb_q', 256), ('sb_k', 256), ('sb_v', 256), ('gates', 4096))
_IN_OFF = {}
_o = 0
for _n, _w in _IN_SPLITS:
    _IN_OFF[_n] = _o
    _o += _w

_A_SPLITS = (('cq', 256), ('ckv', 128), ('kra', 128), ('krb', 128), ('swa_q', 256), ('swa_k', 256),
             ('swa_v', 256), ('hgrn', 1024), ('sb_q', 256), ('sb_k', 256), ('sb_v', 256))
_A_OFF = {}
_o = 0
for _n, _w in _A_SPLITS:
    _A_OFF[_n] = (_o, _o + _w)
    _o += _w
A_COLS = _o

TM_A = 512
TQ_ATT = 256
MLA_TQ = 512
MLA_TK = 512
MLA_GROUP = 4
SWA_TQ = 512
HG_ROWS = 256
TM_FFN = 512
TF_FFN = 1408
MOE_TB = 512
MOE_TF = 1792
TM_COMB = 256
TM_DISP = 512
VMEM_LIMIT = 56 * 1024 * 1024


def _cparams(sem, **kw):
    return pltpu.CompilerParams(dimension_semantics=sem, vmem_limit_bytes=VMEM_LIMIT, **kw)


def _const_spec(shape):
    nd = len(shape)
    return pl.BlockSpec(shape, lambda *_: (0,) * nd, pipeline_mode=pl.Buffered(1))


def _layernorm(v, g, b):
    mu = jnp.mean(v, axis=-1, keepdims=True)
    vc = v - mu
    var = jnp.mean(vc * vc, axis=-1, keepdims=True)
    return vc * lax.rsqrt(var + EPS) * g + b


def _dot(a, b):
    return jnp.dot(a, b, preferred_element_type=F32)


def _dot_nt(a, b):
    return lax.dot_general(a, b, (((1,), (1,)), ((), ())), preferred_element_type=F32)


def _split3(a):
    hi = a.astype(BF16)
    r = a - hi.astype(F32)
    mid = r.astype(BF16)
    lo = (r - mid.astype(F32)).astype(BF16)
    return hi, mid, lo


def _rope_kernel(pos_ref, freq_ref, c_ref, s_ref):
    lane = lax.broadcasted_iota(I32, pos_ref.shape, 1)
    ang = pos_ref[...] * freq_ref[...]
    rope = (lane >= MLA_NOPE) & (lane < MLA_NOPE + MLA_ROPE)
    first = lane < MLA_NOPE + MLA_ROPE // 2
    c_ref[...] = jnp.where(lane < MLA_NOPE, 1.0, jnp.where(rope, jnp.cos(ang), 0.0))
    sn = jnp.sin(ang)
    s_ref[...] = jnp.where(rope, jnp.where(first, -sn, sn), 0.0)


def _rope_tables(positions):
    n = positions.size
    half = MLA_ROPE // 2
    inv_freq = ROPE_THETA ** (-jnp.arange(half, dtype=F32) / half)
    freq = jnp.zeros((1, LANES), F32).at[0, MLA_NOPE:MLA_NOPE + MLA_ROPE].set(jnp.tile(inv_freq, 2))
    posb = jnp.broadcast_to(positions.reshape(n, 1).astype(F32), (n, LANES))
    tm = 1024
    return pl.pallas_call(
        _rope_kernel,
        grid=(n // tm,),
        in_specs=[pl.BlockSpec((tm, LANES), lambda i: (i, 0)), _const_spec((1, LANES))],
        out_specs=[pl.BlockSpec((tm, LANES), lambda i: (i, 0))] * 2,
        out_shape=[jax.ShapeDtypeStruct((n, LANES), F32)] * 2,
        compiler_params=_cparams(("parallel",)),
        name="rope_tables",
    )(posb, freq)


def _inproj_kernel(x_ref, w_ref, c_ref, s_ref, qn_ref, kvn_ref, wuqa_ref, wuqb_ref, wuk_ref, wuv_ref,
                   mq_ref, mk_ref, mv_ref, swq_ref, swk_ref, swv_ref, hg_ref, sbq_ref, sbk_ref, sbv_ref):
    h = _dot(x_ref[...].astype(BF16), w_ref[...])

    def cols(name):
        lo, hi = _A_OFF[name]
        return h[:, lo:hi]

    c = c_ref[...]
    s = s_ref[...]
    c4 = jnp.concatenate([c] * N_HEADS, axis=1)
    s4 = jnp.concatenate([s] * N_HEADS, axis=1)

    cq = cols('cq')
    cqn = (cq * lax.rsqrt(jnp.mean(cq * cq, axis=-1, keepdims=True) + EPS) * qn_ref[...]).astype(BF16)
    q = _dot(cqn, wuqa_ref[...]) * c4 + _dot(cqn, wuqb_ref[...]) * s4
    mq_ref[...] = (q * (MLA_SCALE * LOG2E)).astype(BF16)

    ckv = cols('ckv')
    ckvn = (ckv * lax.rsqrt(jnp.mean(ckv * ckv, axis=-1, keepdims=True) + EPS) * kvn_ref[...]).astype(BF16)
    krot = cols('kra') * c + cols('krb') * s
    mk_ref[...] = (_dot(ckvn, wuk_ref[...]) + jnp.concatenate([krot] * N_HEADS, axis=1)).astype(BF16)
    mv_ref[...] = _dot(ckvn, wuv_ref[...]).astype(BF16)

    swq_ref[...] = cols('swa_q').astype(BF16)
    swk_ref[...] = cols('swa_k').astype(BF16)
    swv_ref[...] = cols('swa_v').astype(BF16)
    hg_ref[...] = cols('hgrn')
    sbq_ref[...] = cols('sb_q').astype(BF16)
    sbk_ref[...] = cols('sb_k').astype(BF16)
    sbv_ref[...] = cols('sb_v').astype(BF16)


def _inproj_weights(w_in, w_uq, w_ukv):
    def seg(name, width):
        o = _IN_OFF[name]
        return w_in[:, o:o + width]

    kr = seg('mla_kr', MLA_ROPE)
    half = MLA_ROPE // 2
    z64 = jnp.zeros((D_MODEL, MLA_NOPE), F32)
    z32 = jnp.zeros((D_MODEL, LANES - MLA_NOPE - MLA_ROPE), F32)
    kra = jnp.concatenate([z64, kr, z32], axis=1)
    krb = jnp.concatenate([z64, kr[:, half:], kr[:, :half], z32], axis=1)
    swk = seg('swa_k', 128)
    swv = seg('swa_v', 128)
    dup = lambda t: jnp.concatenate([t[:, :64], t[:, :64], t[:, 64:], t[:, 64:]], axis=1)
    w_a = jnp.concatenate([
        seg('mla_cq', 256), seg('mla_ckv', 128), kra, krb,
        seg('swa_q', 256) * QK_SCALE, dup(swk), dup(swv),
        seg('hgrn', 1024), seg('sb_q', 256) * (QK_SCALE * LOG2E), seg('sb_k', 256), seg('sb_v', 256)], axis=1)

    qd = MLA_NOPE + MLA_ROPE
    zq = jnp.zeros((MLA_Q_LORA, LANES - qd), F32)
    zn = jnp.zeros((MLA_Q_LORA, MLA_NOPE), F32)
    qa, qb = [], []
    for hh in range(N_HEADS):
        nope = w_uq[:, hh * qd: hh * qd + MLA_NOPE]
        rope = w_uq[:, hh * qd + MLA_NOPE: (hh + 1) * qd]
        qa += [nope, rope, zq]
        qb += [zn, rope[:, half:], rope[:, :half], zq]
    wuqa = jnp.concatenate(qa, axis=1)
    wuqb = jnp.concatenate(qb, axis=1)
    lane = jnp.arange(N_HEADS * LANES) % LANES
    wuk = jnp.where(lane[None, :] < MLA_NOPE, w_ukv, 0.0)
    wuv = jnp.concatenate([w_ukv[:, hh * LANES + MLA_NOPE:(hh + 1) * LANES] for hh in range(N_HEADS)], axis=1)
    return tuple(t.astype(BF16) for t in (w_a, wuqa, wuqb, wuk, wuv))


def _inproj(x2d, wts, ctab, stab, q_norm, kv_norm):
    n = x2d.shape[0]
    w_a, wuqa, wuqb, wuk, wuv = wts
    tm = TM_A
    row = lambda w: pl.BlockSpec((tm, w), lambda i: (i, 0))
    out_w = (512, 512, 256, 256, 256, 256, 1024, 256, 256, 256)
    out_dt = (BF16, BF16, BF16, BF16, BF16, BF16, F32, BF16, BF16, BF16)
    return pl.pallas_call(
        _inproj_kernel,
        grid=(n // tm,),
        in_specs=[row(D_MODEL), _const_spec(w_a.shape), row(LANES), row(LANES),
                  _const_spec((1, MLA_Q_LORA)), _const_spec((1, MLA_KV_LORA)),
                  _const_spec(wuqa.shape), _const_spec(wuqb.shape), _const_spec(wuk.shape),
                  _const_spec(wuv.shape)],
        out_specs=[row(w) for w in out_w],
        out_shape=[jax.ShapeDtypeStruct((n, w), d) for w, d in zip(out_w, out_dt)],
        compiler_params=_cparams(("parallel",)),
        name="inproj",
    )(x2d, w_a, ctab, stab, q_norm.reshape(1, -1), kv_norm.reshape(1, -1), wuqa, wuqb, wuk, wuv)


def _half_mask(half):
    lane = lax.broadcasted_iota(I32, (1, LANES), 1)
    return (lane < HEAD_DIM) if half == 0 else (lane >= HEAD_DIM)


def _mla_kernel(q_ref, k_ref, v_ref, o_ref):
    tq = q_ref.shape[0]
    tk = MLA_TK
    nsub = tq // tk
    i = pl.program_id(1)
    row = lax.broadcasted_iota(I32, (tq, tk), 0)
    col = lax.broadcasted_iota(I32, (tq, tk), 1)
    ones = jnp.ones((1, LANES), BF16)

    def update(off, carry, heads, mask, width=tk):
        ss = [_dot_nt(q_ref[:, hh * LANES:(hh + 1) * LANES],
                      k_ref[pl.ds(off, width), hh * LANES:(hh + 1) * LANES]) for hh in heads]
        if mask is not None:
            ss = [jnp.where(mask, s, NEG_BIG) for s in ss]
        ms = [jnp.maximum(c[0], jnp.max(s, axis=-1, keepdims=True)) for c, s in zip(carry, ss)]
        pms = [jnp.exp2(s - m).astype(BF16) for s, m in zip(ss, ms)]
        new = []
        for n, hh in enumerate(heads):
            vb = v_ref[pl.ds(off, width), (hh // 2) * LANES:(hh // 2 + 1) * LANES]
            vb = jnp.where(_half_mask(hh % 2), vb, ones)
            m, acc = carry[n]
            new.append((ms[n], jnp.exp2(m - ms[n]) * acc + _dot(pms[n], vb)))
        return tuple(new)

    accs = []
    for g in range(0, N_HEADS, MLA_GROUP):
        heads = tuple(range(g, g + MLA_GROUP))
        init = tuple((jnp.full((tq, 1), NEG_BIG, F32), jnp.zeros((tq, LANES), F32)) for _ in heads)
        nkb = i * nsub
        carry = lax.fori_loop(
            0, nkb // 2,
            lambda j, c, heads=heads: update(pl.multiple_of(j * 2 * tk, 2 * tk), c, heads, None, 2 * tk), init)
        carry = lax.cond(
            nkb % 2 == 1,
            lambda c, heads=heads: update(pl.multiple_of((nkb - 1) * tk, tk), c, heads, None),
            lambda c: c, carry)
        for r in range(nsub):
            carry = update(pl.multiple_of(i * tq + r * tk, tk), carry, heads, col + r * tk <= row)
        accs += [c[1] for c in carry]
    outs = []
    for p in range(N_HEADS // 2):
        a0, a1 = accs[2 * p], accs[2 * p + 1]
        outs.append(jnp.where(_half_mask(0), a0 / a0[:, HEAD_DIM:HEAD_DIM + 1], a1 / a1[:, 0:1]))
    o_ref[...] = jnp.concatenate(outs, axis=1).astype(o_ref.dtype)


def _mla_attention(q, k, v, batch, seq):
    tq = MLA_TQ
    q3, k3, v3 = (t.reshape(batch, seq, t.shape[-1]) for t in (q, k, v))
    out = pl.pallas_call(
        _mla_kernel,
        grid=(batch, seq // tq),
        in_specs=[pl.BlockSpec((None, tq, 512), lambda b, i: (b, i, 0)),
                  pl.BlockSpec((None, seq, 512), lambda b, i: (b, 0, 0)),
                  pl.BlockSpec((None, seq, WIDTH), lambda b, i: (b, 0, 0))],
        out_specs=pl.BlockSpec((None, tq, WIDTH), lambda b, i: (b, i, 0)),
        out_shape=jax.ShapeDtypeStruct((batch, seq, WIDTH), BF16),
        compiler_params=_cparams(("parallel", "arbitrary")),
        name="mla_attention",
    )(q3, k3, v3)
    return out.reshape(batch * seq, WIDTH)


def _sb_kernel(q_ref, k_ref, v_ref, o_ref):
    tq = q_ref.shape[0]
    i = pl.program_id(1)
    row = lax.broadcasted_iota(I32, (tq, tq), 0)
    col = lax.broadcasted_iota(I32, (tq, tq), 1)
    strict = col < row
    later = (row > col).astype(BF16)
    qs = []
    for hh in range(N_HEADS):
        qp = q_ref[:, (hh // 2) * LANES:(hh // 2 + 1) * LANES]
        qs.append(jnp.where(_half_mask(hh % 2), qp, jnp.zeros_like(qp)))

    def block(j, carry, diag):
        off = pl.multiple_of(j * tq, tq)
        runs, accs = carry
        heads = range(N_HEADS)
        zs = [_dot_nt(qs[hh], k_ref[pl.ds(off, tq), (hh // 2) * LANES:(hh // 2 + 1) * LANES]) for hh in heads]
        lsps = [jnp.minimum(z, 0.0) - jnp.log2(1.0 + jnp.exp2(-jnp.abs(z))) for z in zs]
        lsns = [lsp - z for lsp, z in zip(lsps, zs)]
        if diag:
            lsns = [jnp.where(strict, t, 0.0) for t in lsns]
        his = [t.astype(BF16) for t in lsns]
        los = [(t - hi.astype(F32)).astype(BF16) for t, hi in zip(lsns, his)]
        rems = [_dot(hi, later) + _dot(lo, later) for hi, lo in zip(his, los)]
        args = [lsps[hh] + rems[hh] + runs[hh] for hh in heads]
        if diag:
            args = [jnp.where(strict, t, NEG_BIG) for t in args]
        probs = [jnp.exp2(t).astype(BF16) for t in args]
        new_runs = tuple(runs[hh] + rems[hh][:, 0:1] + lsns[hh][:, 0:1] for hh in heads)
        new_accs = list(accs)
        for hh in heads:
            p = hh // 2
            vb = v_ref[pl.ds(off, tq), p * LANES:(p + 1) * LANES]
            vb = jnp.where(_half_mask(hh % 2), vb, jnp.zeros_like(vb))
            new_accs[p] = new_accs[p] + _dot(probs[hh], vb)
        return new_runs, tuple(new_accs)

    init = (tuple(jnp.zeros((tq, 1), F32) for _ in range(N_HEADS)),
            tuple(jnp.zeros((tq, LANES), F32) for _ in range(N_HEADS // 2)))
    def still_active(runs):
        top = functools.reduce(jnp.maximum, runs)
        return (jnp.max(top) > SB_RUN_FLOOR).astype(I32)

    runs, accs = block(i, init, True)

    def cond(c):
        return (c[0] < i) & (c[1] > 0)

    def body(c):
        jj, _, runs, accs = c
        runs, accs = block(i - 1 - jj, (runs, accs), False)
        return jj + 1, still_active(runs), runs, accs

    _, _, _, accs = lax.while_loop(cond, body, (jnp.int32(0), still_active(runs), runs, accs))
    o_ref[...] = jnp.concatenate(accs, axis=1).astype(o_ref.dtype)


def _sb_attention(q, k, v, batch, seq):
    tq = TQ_ATT
    q3, k3, v3 = (t.reshape(batch, seq, WIDTH) for t in (q, k, v))
    out = pl.pallas_call(
        _sb_kernel,
        grid=(batch, seq // tq),
        in_specs=[pl.BlockSpec((None, tq, WIDTH), lambda b, i: (b, i, 0)),
                  pl.BlockSpec((None, seq, WIDTH), lambda b, i: (b, 0, 0)),
                  pl.BlockSpec((None, seq, WIDTH), lambda b, i: (b, 0, 0))],
        out_specs=pl.BlockSpec((None, tq, WIDTH), lambda b, i: (b, i, 0)),
        out_shape=jax.ShapeDtypeStruct((batch, seq, WIDTH), BF16),
        compiler_params=_cparams(("parallel", "arbitrary")),
        name="stick_breaking",
    )(q3, k3, v3)
    return out.reshape(batch * seq, WIDTH)


def _rel_bucket(dist):
    exact = REL_BUCKETS // 2
    n = jnp.maximum(dist, 0)
    nf = jnp.maximum(n, 1).astype(F32)
    large = exact + (jnp.log(nf / exact) / math.log(REL_MAX_DIST / exact) * (REL_BUCKETS - exact)).astype(I32)
    large = jnp.clip(large, 0, REL_BUCKETS - 1)
    return jnp.where(n < exact, n, large)


def _swa_kernel(sink_ref, tab_ref, q_ref, kc_ref, kh_ref, vc_ref, vh_ref, pq_ref, pkc_ref, pkh_ref, o_ref):
    w = SWA_WINDOW
    step = pl.program_id(1)
    row = lax.broadcasted_iota(I32, (w, w), 0)
    col = lax.broadcasted_iota(I32, (w, w), 1)
    valid_c = col <= row
    valid_p = col > row
    tabs = [jnp.broadcast_to(tab_ref[hh:hh + 1, :], (w, LANES)) for hh in range(N_HEADS)]
    ones = jnp.ones((1, LANES), BF16)
    nsub = q_ref.shape[0] // w
    chains = [(r, hh) for r in range(nsub) for hh in range(N_HEADS)]

    def keys(ref, halo_ref, r, hh):
        sl = slice((hh // 2) * LANES, (hh // 2 + 1) * LANES)
        cur = ref[r * w:(r + 1) * w, sl]
        prev = ref[(r - 1) * w:r * w, sl] if r else halo_ref[:, sl]
        return cur, prev

    buckets = []
    for r in range(nsub):
        pq = pq_ref[r * w:(r + 1) * w, :]
        pk_prev = pkc_ref[:, (r - 1) * w:r * w] if r else pkh_ref[...]
        buckets.append((_rel_bucket(pq - pkc_ref[:, r * w:(r + 1) * w]), _rel_bucket(pq - pk_prev)))
    logits = []
    for r, hh in chains:
        qp = q_ref[r * w:(r + 1) * w, (hh // 2) * LANES:(hh // 2 + 1) * LANES]
        qh = jnp.where(_half_mask(hh % 2), qp, jnp.zeros_like(qp))
        kc, kp = keys(kc_ref, kh_ref, r, hh)
        logits.append((_dot_nt(qh, kc), _dot_nt(qh, kp)))
    masked = []
    for (r, hh), (lc, lp) in zip(chains, logits):
        lc = jnp.where(valid_c, lc + jnp.take_along_axis(tabs[hh], buckets[r][0], axis=1), NEG_BIG)
        lp = lp + jnp.take_along_axis(tabs[hh], buckets[r][1], axis=1)
        lp = jnp.where(valid_p if r else valid_p & (step > 0), lp, NEG_BIG)
        masked.append((lc, lp))
    maxes = [jnp.maximum(jnp.maximum(jnp.max(lc, axis=-1, keepdims=True), jnp.max(lp, axis=-1, keepdims=True)),
                         sink_ref[hh]) for (r, hh), (lc, lp) in zip(chains, masked)]
    probs = [(jnp.exp(lc - m).astype(BF16), jnp.exp(lp - m).astype(BF16)) for (lc, lp), m in zip(masked, maxes)]
    outs = {}
    for (r, hh), (ec, ep), m in zip(chains, probs, maxes):
        vc, vp = keys(vc_ref, vh_ref, r, hh)
        mine = _half_mask(hh % 2)
        acc = _dot(ec, jnp.where(mine, vc, ones)) + _dot(ep, jnp.where(mine, vp, ones))
        den = (acc[:, 0:1] if hh % 2 else acc[:, HEAD_DIM:HEAD_DIM + 1]) + jnp.exp(sink_ref[hh] - m)
        outs[(r, hh)] = acc / den
    for r in range(nsub):
        pairs = [jnp.where(_half_mask(0), outs[(r, 2 * p)], outs[(r, 2 * p + 1)]) for p in range(N_HEADS // 2)]
        o_ref[r * w:(r + 1) * w, :] = jnp.concatenate(pairs, axis=1).astype(o_ref.dtype)


def _swa_attention(q, k, v, positions, sinks, rel_table, batch, seq):
    w = SWA_WINDOW
    tq = SWA_TQ
    per = tq // w
    q3, k3, v3 = (t.reshape(batch, seq, WIDTH) for t in (q, k, v))
    pcol = positions.reshape(batch, seq, 1)
    prow = positions.reshape(batch, 1, seq)
    tab = jnp.zeros((N_HEADS, LANES), F32).at[:, :REL_BUCKETS].set(rel_table.astype(F32).T)
    cur = lambda b, n: (b, n, 0)
    halo = lambda b, n: (b, jnp.maximum(n * per - 1, 0), 0)
    out = pl.pallas_call(
        _swa_kernel,
        grid=(batch, seq // tq),
        in_specs=[pl.BlockSpec(memory_space=pltpu.SMEM), _const_spec((N_HEADS, LANES)),
                  pl.BlockSpec((None, tq, WIDTH), cur),
                  pl.BlockSpec((None, tq, WIDTH), cur), pl.BlockSpec((None, w, WIDTH), halo),
                  pl.BlockSpec((None, tq, WIDTH), cur), pl.BlockSpec((None, w, WIDTH), halo),
                  pl.BlockSpec((None, tq, 1), cur),
                  pl.BlockSpec((None, 1, tq), lambda b, n: (b, 0, n)),
                  pl.BlockSpec((None, 1, w), lambda b, n: (b, 0, jnp.maximum(n * per - 1, 0)))],
        out_specs=pl.BlockSpec((None, tq, WIDTH), cur),
        out_shape=jax.ShapeDtypeStruct((batch, seq, WIDTH), BF16),
        compiler_params=_cparams(("parallel", "arbitrary")),
        name="swa_attention",
    )(sinks.astype(F32), tab, q3, k3, k3, v3, v3, pcol, prow, prow)
    return out.reshape(batch * seq, WIDTH)


def _hgrn_kernel(hg_ref, lb_ref, nw_ref, o_ref, state_ref):
    c = HGRN_CHUNK
    blk = HGRN_BLOCK

    @pl.when(pl.program_id(1) == 0)
    def _():
        state_ref[...] = jnp.zeros_like(state_ref)

    r64 = lax.broadcasted_iota(I32, (c, c), 0)
    c64 = lax.broadcasted_iota(I32, (c, c), 1)
    incl = (c64 <= r64).astype(BF16)
    ra = lax.broadcasted_iota(I32, (WIDTH, WIDTH), 0) // HEAD_DIM
    ca = lax.broadcasted_iota(I32, (WIDTH, WIDTH), 1) // HEAD_DIM
    same_head = ra == ca
    seg = same_head.astype(BF16)
    ones_cols = jnp.ones((c, LANES), BF16)
    trow = lax.broadcasted_iota(I32, (blk, WIDTH), 0)
    caps = [jnp.where(trow >= s_i, 0.0, NEG_BIG) for s_i in range(blk)]
    lane_head = lax.broadcasted_iota(I32, (1, WIDTH), 1) // HEAD_DIM
    lb = lb_ref[...]
    nw = nw_ref[...]
    dn0 = (((0,), (0,)), ((), ()))

    for ch in range(hg_ref.shape[0] // c):
        rows = slice(ch * c, (ch + 1) * c)
        qraw = hg_ref[rows, 0:WIDTH]
        fraw = hg_ref[rows, WIDTH:2 * WIDTH]
        v = hg_ref[rows, 2 * WIDTH:3 * WIDTH]
        graw = hg_ref[rows, 3 * WIDTH:4 * WIDTH]
        qf = qraw * jax.nn.sigmoid(qraw)
        forget = lb + (1.0 - lb) * jax.nn.sigmoid(fraw)
        lf = jnp.log(forget)
        kk = 1.0 - forget
        gate = graw * jax.nn.sigmoid(graw)
        vb = v.astype(BF16)

        lf3 = _split3(lf)
        bc = _dot(incl, lf3[0]) + _dot(incl, lf3[1]) + _dot(incl, lf3[2])
        b_last = bc[c - 1:c, :]
        tot_col = sum(lax.dot_general(t, ones_cols, dn0, preferred_element_type=F32) for t in lf3)
        decay_col = jnp.exp(jnp.concatenate([tot_col, tot_col], axis=1))

        state = state_ref[...]
        o_inter = _dot((qf * jnp.exp(bc)).astype(BF16), state.astype(BF16))

        def before(qa, qb, ka, kb):
            ref = bc[kb - 1:kb, :]
            qt = qf[qa:qb] * jnp.exp(bc[qa:qb] - ref)
            kt = (kk[ka:kb] * jnp.exp(ref - bc[ka:kb])).astype(BF16)
            qs = jnp.concatenate([jnp.where(lane_head == hh, qt, 0.0) for hh in range(N_HEADS)], axis=0)
            att = _dot_nt(qs.astype(BF16), kt)
            mix = _dot(att.astype(BF16), vb[ka:kb])
            nq = qb - qa
            return sum(jnp.where(lane_head == hh, mix[hh * nq:(hh + 1) * nq], 0.0) for hh in range(N_HEADS))

        bc2 = bc * LOG2E

        def inside(a):
            b2 = bc2[a:a + blk]
            qb_ = qf[a:a + blk]
            ws = []
            for s_i in range(blk):
                e = jnp.exp2(jnp.minimum(b2 - b2[s_i:s_i + 1, :], caps[s_i]))
                ws.append((qb_ * kk[a + s_i:a + s_i + 1, :] * e).astype(BF16))
            att = _dot(jnp.concatenate(ws, axis=0), seg)
            return sum(att[s_i * blk:(s_i + 1) * blk] * v[a + s_i:a + s_i + 1, :] for s_i in range(blk))

        half = c // 2
        far = before(half, c, 0, half)
        intra = [inside(0),
                 inside(blk) + before(blk, half, 0, blk),
                 inside(half) + far[:blk],
                 inside(half + blk) + far[blk:] + before(half + blk, c, half, half + blk)]
        o = o_inter + jnp.concatenate(intra, axis=0)

        khat = (kk * jnp.exp(b_last - bc)).astype(BF16)
        upd = lax.dot_general(khat, vb, dn0, preferred_element_type=F32)
        state_ref[...] = decay_col * state + jnp.where(same_head, upd, 0.0)

        o2 = _split3(o * o)
        ms = (_dot(o2[0], seg) + _dot(o2[1], seg)) * (1.0 / HEAD_DIM)
        o_ref[rows, :] = (o * lax.rsqrt(ms + EPS) * nw * gate).astype(o_ref.dtype)


def _hgrn(hg, lower_bound, norm_w, batch, seq):
    rows = HG_ROWS
    hg3 = hg.reshape(batch, seq, 4 * WIDTH)
    out = pl.pallas_call(
        _hgrn_kernel,
        grid=(batch, seq // rows),
        in_specs=[pl.BlockSpec((None, rows, 4 * WIDTH), lambda b, i: (b, i, 0)),
                  _const_spec((1, WIDTH)), _const_spec((1, WIDTH))],
        out_specs=pl.BlockSpec((None, rows, WIDTH), lambda b, i: (b, i, 0)),
        out_shape=jax.ShapeDtypeStruct((batch, seq, WIDTH), BF16),
        scratch_shapes=[pltpu.VMEM((WIDTH, WIDTH), F32)],
        compiler_params=_cparams(("parallel", "arbitrary")),
        name="hgrn2",
    )(hg3, lower_bound.reshape(1, WIDTH).astype(F32), norm_w.reshape(1, WIDTH).astype(F32))
    return out.reshape(batch * seq, WIDTH)


def _merge_kernel(x_ref, y0_ref, y1_ref, y2_ref, y3_ref, wg_ref, wb_ref, wo_ref, g_ref, b_ref, o_ref):
    x = x_ref[...]
    xb = x.astype(BF16)
    merged = jnp.zeros(x.shape, F32)
    for nbr, y_ref in enumerate((y0_ref, y1_ref, y2_ref, y3_ref)):
        gate = jax.nn.sigmoid(_dot(xb, wg_ref[:, nbr * D_MODEL:(nbr + 1) * D_MODEL]))
        merged = merged + gate * _dot(y_ref[...], wb_ref[nbr])
    y = _dot(merged.astype(BF16), wo_ref[...])
    o_ref[...] = _layernorm(ALPHA * x + y, g_ref[...], b_ref[...])


def _merge(x2d, ys, wg, wb, wo, g, b):
    n = x2d.shape[0]
    tm = TM_A
    row = lambda w: pl.BlockSpec((tm, w), lambda i: (i, 0))
    return pl.pallas_call(
        _merge_kernel,
        grid=(n // tm,),
        in_specs=[row(D_MODEL)] + [row(WIDTH)] * 4 +
                 [_const_spec(wg.shape), _const_spec(wb.shape), _const_spec(wo.shape),
                  _const_spec((1, D_MODEL)), _const_spec((1, D_MODEL))],
        out_specs=row(D_MODEL),
        out_shape=jax.ShapeDtypeStruct((n, D_MODEL), F32),
        compiler_params=_cparams(("parallel",)),
        name="merge_outproj_ln",
    )(x2d, *ys, wg, wb, wo, g.reshape(1, -1), b.reshape(1, -1))


def _memkv_kernel(m_ref, w_ref, k_ref, v_ref):
    kv = _dot(m_ref[...].astype(BF16), w_ref[...])
    k_ref[...] = kv[:, :WIDTH].astype(BF16)
    v_ref[...] = kv[:, WIDTH:].astype(BF16)


def _memkv(mem, wkv):
    batch, m, _ = mem.shape
    return pl.pallas_call(
        _memkv_kernel,
        grid=(batch,),
        in_specs=[pl.BlockSpec((None, m, D_MODEL), lambda b: (b, 0, 0)), _const_spec(wkv.shape)],
        out_specs=[pl.BlockSpec((None, m, WIDTH), lambda b: (b, 0, 0))] * 2,
        out_shape=[jax.ShapeDtypeStruct((batch, m, WIDTH), BF16)] * 2,
        compiler_params=_cparams(("parallel",)),
        name="mem_kv",
    )(mem, wkv)


def _xattn_kernel(x_ref, wq_ref, k_ref, v_ref, wo_ref, g_ref, b_ref, o_ref):
    x = x_ref[...]
    q = _dot(x.astype(BF16), wq_ref[...]).astype(BF16)
    k = k_ref[...]
    v = v_ref[...]
    lane = lax.broadcasted_iota(I32, (1, WIDTH), 1) // HEAD_DIM
    heads = range(N_HEADS)
    ss = [_dot_nt(jnp.where(lane == hh, q, jnp.zeros_like(q)), k) for hh in heads]
    es = [jnp.exp(s - jnp.max(s, axis=-1, keepdims=True)) for s in ss]
    ps = [(e / jnp.sum(e, axis=-1, keepdims=True)).astype(BF16) for e in es]
    o = jnp.zeros((x.shape[0], WIDTH), F32)
    for hh in heads:
        o = o + jnp.where(lane == hh, _dot(ps[hh], v), 0.0)
    y = _dot(o.astype(BF16), wo_ref[...])
    o_ref[...] = _layernorm(ALPHA * x + y, g_ref[...], b_ref[...])


def _xattn(x2d, wq, k, v, wo, g, b, batch, seq):
    tm = TM_A
    m = k.shape[1]
    x3 = x2d.reshape(batch, seq, D_MODEL)
    row = pl.BlockSpec((None, tm, D_MODEL), lambda bb, i: (bb, i, 0))
    kv_spec = pl.BlockSpec((None, m, WIDTH), lambda bb, i: (bb, 0, 0))
    out = pl.pallas_call(
        _xattn_kernel,
        grid=(batch, seq // tm),
        in_specs=[row, _const_spec(wq.shape), kv_spec, kv_spec, _const_spec(wo.shape),
                  _const_spec((1, D_MODEL)), _const_spec((1, D_MODEL))],
        out_specs=row,
        out_shape=jax.ShapeDtypeStruct((batch, seq, D_MODEL), F32),
        compiler_params=_cparams(("parallel", "parallel")),
        name="mem_xattn_ln",
    )(x3, wq, k, v, wo, g.reshape(1, -1), b.reshape(1, -1))
    return out.reshape(batch * seq, D_MODEL)


def _ffn_kernel(x_ref, w13_ref, w2_ref, g_ref, b_ref, o_ref):
    x = x_ref[...]
    xb = x.astype(BF16)
    y = None
    for h in range(F_DENSE // TF_FFN):
        lo = h * TF_FFN
        a = _dot(xb, w13_ref[:, lo:lo + TF_FFN])
        gate = _dot(xb, w13_ref[:, F_DENSE + lo:F_DENSE + lo + TF_FFN])
        part = _dot((a * jax.nn.sigmoid(a) * gate).astype(BF16), w2_ref[lo:lo + TF_FFN, :])
        y = part if y is None else y + part
    o_ref[...] = _layernorm(ALPHA * x + y, g_ref[...], b_ref[...])


def _ffn(x2d, w13, w2, g, b):
    n = x2d.shape[0]
    tm = TM_FFN
    return pl.pallas_call(
        _ffn_kernel,
        grid=(n // tm,),
        in_specs=[pl.BlockSpec((tm, D_MODEL), lambda i: (i, 0)),
                  _const_spec(w13.shape), _const_spec(w2.shape),
                  _const_spec((1, D_MODEL)), _const_spec((1, D_MODEL))],
        out_specs=pl.BlockSpec((tm, D_MODEL), lambda i: (i, 0)),
        out_shape=jax.ShapeDtypeStruct((n, D_MODEL), F32),
        compiler_params=_cparams(("parallel",)),
        name="ffn_ln",
    )(x2d, w13, w2, g.reshape(1, -1), b.reshape(1, -1))


def _router_kernel(x_ref, r_ref, info_ref, wts_ref, cnt_ref, carry_ref):
    tm = x_ref.shape[0]

    @pl.when(pl.program_id(0) == 0)
    def _():
        carry_ref[...] = jnp.zeros_like(carry_ref)

    logits = jnp.dot(x_ref[...], r_ref[...], precision=lax.Precision.HIGHEST, preferred_element_type=F32)
    lane = lax.broadcasted_iota(I32, (tm, LANES), 1)
    lg = jnp.where(lane < N_EXPERTS, logits, -jnp.inf)
    m1 = jnp.max(lg, axis=-1, keepdims=True)
    i1 = jnp.min(jnp.where(lg == m1, lane, LANES), axis=-1, keepdims=True)
    lg2 = jnp.where(lane == i1, -jnp.inf, lg)
    m2 = jnp.max(lg2, axis=-1, keepdims=True)
    i2 = jnp.min(jnp.where(lg2 == m2, lane, LANES), axis=-1, keepdims=True)
    e = jnp.exp(m2 - m1)
    w1 = 1.0 / (1.0 + e)
    w2 = e / (1.0 + e)
    sel1 = lane == i1
    sel2 = lane == i2
    chosen = jnp.where(sel1 | sel2, 1.0, 0.0)
    row = lax.broadcasted_iota(I32, (tm, tm), 0)
    col = lax.broadcasted_iota(I32, (tm, tm), 1)
    before = (col < row).astype(BF16)
    ranks = _dot(before, chosen.astype(BF16)) + carry_ref[...]
    r1 = jnp.sum(jnp.where(sel1, ranks, 0.0), axis=-1, keepdims=True).astype(I32)
    r2 = jnp.sum(jnp.where(sel2, ranks, 0.0), axis=-1, keepdims=True).astype(I32)
    carry_ref[...] = carry_ref[...] + jnp.sum(chosen, axis=0, keepdims=True)
    info_ref[...] = jnp.where(lane == 0, i1, jnp.where(lane == 1, i2, jnp.where(lane == 2, r1,
                              jnp.where(lane == 3, r2, 0))))
    wts_ref[...] = jnp.where(lane == 0, w1, jnp.where(lane == 1, w2, 0.0))
    cnt_ref[...] = carry_ref[...]


def _router(x2d, router):
    n = x2d.shape[0]
    tm = TM_A
    r_pad = jnp.zeros((D_MODEL, LANES), F32).at[:, :N_EXPERTS].set(router.astype(F32))
    row = pl.BlockSpec((tm, LANES), lambda i: (i, 0))
    return pl.pallas_call(
        _router_kernel,
        grid=(n // tm,),
        in_specs=[pl.BlockSpec((tm, D_MODEL), lambda i: (i, 0)), _const_spec(r_pad.shape)],
        out_specs=[row, row, pl.BlockSpec((1, LANES), lambda i: (0, 0))],
        out_shape=[jax.ShapeDtypeStruct((n, LANES), I32), jax.ShapeDtypeStruct((n, LANES), F32),
                   jax.ShapeDtypeStruct((1, LANES), F32)],
        scratch_shapes=[pltpu.VMEM((1, LANES), F32)],
        compiler_params=_cparams(("arbitrary",)),
        name="moe_router",
    )(x2d, r_pad)


def _dispatch_kernel(pad_ref, dest_ref, x_ref, xb_hbm, stage_ref, sems):
    tm = x_ref.shape[0]
    i = pl.program_id(0)
    last = pl.num_programs(0) - 1
    slot = i % 2

    def wait_step(s):
        for _ in range(2):
            pltpu.make_async_copy(stage_ref.at[s], xb_hbm.at[pl.ds(0, tm), :], sems.at[s]).wait()

    @pl.when(i >= 2)
    def _():
        wait_step(slot)

    stage_ref[slot] = x_ref[...]

    def issue(r, c):
        for k in range(2):
            pltpu.make_async_copy(stage_ref.at[slot, pl.ds(r, 1), :],
                                  xb_hbm.at[pl.ds(dest_ref[0, 2 * r + k], 1), :], sems.at[slot]).start()
        return c
    lax.fori_loop(0, tm, issue, 0, unroll=8)

    @pl.when(i == last)
    def _():
        def fill(e, c):
            def one(s, c2):
                pltpu.make_async_copy(stage_ref.at[slot, pl.ds(0, 1), :], xb_hbm.at[pl.ds(s, 1), :],
                                      sems.at[2]).start()
                return c2

            def done(s, c2):
                pltpu.make_async_copy(stage_ref.at[slot, pl.ds(0, 1), :], xb_hbm.at[pl.ds(0, 1), :],
                                      sems.at[2]).wait()
                return c2
            lax.fori_loop(pad_ref[0, e], pad_ref[1, e], one, 0)
            lax.fori_loop(pad_ref[0, e], pad_ref[1, e], done, 0)
            return c
        lax.fori_loop(0, pad_ref.shape[1], fill, 0)
        wait_step(slot)

        @pl.when(last >= 1)
        def _():
            wait_step(1 - slot)


def _dispatch(x2d, dest, pads, nblk):
    n = x2d.shape[0]
    tm = TM_DISP
    nt = n // tm
    grid_spec = pltpu.PrefetchScalarGridSpec(
        num_scalar_prefetch=1,
        grid=(nt,),
        in_specs=[pl.BlockSpec((None, 1, 2 * tm), lambda i, pads: (i, 0, 0), memory_space=pltpu.SMEM),
                  pl.BlockSpec((tm, D_MODEL), lambda i, pads: (i, 0))],
        out_specs=pl.BlockSpec(memory_space=pl.ANY),
        scratch_shapes=[pltpu.VMEM((2, tm, D_MODEL), F32), pltpu.SemaphoreType.DMA((3,))],
    )
    return pl.pallas_call(
        _dispatch_kernel,
        grid_spec=grid_spec,
        out_shape=jax.ShapeDtypeStruct((nblk * MOE_TB, D_MODEL), F32),
        compiler_params=_cparams(("arbitrary",), disable_bounds_checks=True),
        name="moe_dispatch",
    )(pads, dest.reshape(nt, 1, 2 * tm), x2d)


def _expert_kernel(nused_ref, bexp_ref, x_ref, w1_ref, w3_ref, w2_ref, o_ref, acc_ref):
    f = pl.program_id(1)

    @pl.when(pl.program_id(0) < nused_ref[0])
    def _():
        xb = x_ref[...].astype(BF16)
        a = _dot(xb, w1_ref[...])
        gate = _dot(xb, w3_ref[...])
        part = _dot((a * jax.nn.sigmoid(a) * gate).astype(BF16), w2_ref[...])

        @pl.when(f == 0)
        def _():
            acc_ref[...] = part

        @pl.when(f > 0)
        def _():
            acc_ref[...] += part

        @pl.when(f == pl.num_programs(1) - 1)
        def _():
            o_ref[...] = acc_ref[...]

    @pl.when(pl.program_id(0) >= nused_ref[0])
    def _():
        o_ref[...] = jnp.zeros_like(o_ref)


def _experts(xb, w13, w2, nused, blk_exp, nblk):
    tb, tf = MOE_TB, MOE_TF
    nf = F_EXPERT // tf

    def blk(i, nu):
        return jnp.maximum(jnp.minimum(i, nu[0] - 1), 0)

    def ftile(i, f, nu):
        return jnp.where(i < nu[0], f, nf - 1)

    grid_spec = pltpu.PrefetchScalarGridSpec(
        num_scalar_prefetch=2,
        grid=(nblk, nf),
        in_specs=[pl.BlockSpec((tb, D_MODEL), lambda i, f, nu, be: (blk(i, nu), 0)),
                  pl.BlockSpec((None, D_MODEL, tf), lambda i, f, nu, be: (be[blk(i, nu)], 0, ftile(i, f, nu))),
                  pl.BlockSpec((None, D_MODEL, tf), lambda i, f, nu, be: (be[blk(i, nu)], 0, nf + ftile(i, f, nu))),
                  pl.BlockSpec((None, tf, D_MODEL), lambda i, f, nu, be: (be[blk(i, nu)], ftile(i, f, nu), 0))],
        out_specs=pl.BlockSpec((tb, D_MODEL), lambda i, f, nu, be: (i, 0)),
        scratch_shapes=[pltpu.VMEM((tb, D_MODEL), F32)],
    )
    return pl.pallas_call(
        _expert_kernel,
        grid_spec=grid_spec,
        out_shape=jax.ShapeDtypeStruct((nblk * tb, D_MODEL), F32),
        compiler_params=_cparams(("arbitrary", "arbitrary")),
        name="moe_experts",
    )(nused, blk_exp, xb, w13, w13, w2)


def _combine_kernel(dest_ref, nxt_ref, y_hbm, x_ref, wts_ref, g_ref, b_ref, o_ref, buf_ref, sems):
    tm = x_ref.shape[0]
    i = pl.program_id(0)
    slot = i % 2

    def gather(idx_ref, s):
        def issue(r, c):
            for k in range(2):
                pltpu.make_async_copy(y_hbm.at[pl.ds(idx_ref[0, 2 * r + k], 1), :],
                                      buf_ref.at[s, k, pl.ds(r, 1), :], sems.at[s]).start()
            return c
        lax.fori_loop(0, tm, issue, 0, unroll=8)

    @pl.when(i == 0)
    def _():
        gather(dest_ref, slot)

    @pl.when(i + 1 < pl.num_programs(0))
    def _():
        gather(nxt_ref, 1 - slot)

    for k in range(2):
        pltpu.make_async_copy(y_hbm.at[pl.ds(0, tm), :], buf_ref.at[slot, k], sems.at[slot]).wait()
    wts = wts_ref[...]
    y = wts[:, 0:1] * buf_ref[slot, 0] + wts[:, 1:2] * buf_ref[slot, 1]
    o_ref[...] = _layernorm(ALPHA * x_ref[...] + y, g_ref[...], b_ref[...])


def _combine(yb, dest, x2d, wts, g, b):
    n = x2d.shape[0]
    tm = TM_COMB
    nt = n // tm
    row = lambda w: pl.BlockSpec((tm, w), lambda i: (i, 0))
    dest3 = dest.reshape(nt, 1, 2 * tm)
    return pl.pallas_call(
        _combine_kernel,
        grid=(nt,),
        in_specs=[pl.BlockSpec((None, 1, 2 * tm), lambda i: (i, 0, 0), memory_space=pltpu.SMEM),
                  pl.BlockSpec((None, 1, 2 * tm), lambda i: (jnp.minimum(i + 1, nt - 1), 0, 0),
                               memory_space=pltpu.SMEM),
                  pl.BlockSpec(memory_space=pl.ANY), row(D_MODEL), row(LANES),
                  _const_spec((1, D_MODEL)), _const_spec((1, D_MODEL))],
        out_specs=row(D_MODEL),
        out_shape=jax.ShapeDtypeStruct((n, D_MODEL), F32),
        scratch_shapes=[pltpu.VMEM((2, 2, tm, D_MODEL), F32), pltpu.SemaphoreType.DMA((2,))],
        compiler_params=_cparams(("arbitrary",), disable_bounds_checks=True),
        name="moe_combine_ln",
    )(dest3, dest3, yb, x2d, wts, g.reshape(1, -1), b.reshape(1, -1))


def _moe(x2d, router, w13, w2, g, b):
    n = x2d.shape[0]
    tb = MOE_TB
    info, wts, cnt = _router(x2d, router)
    idx = info[:, 0:2]
    rank = info[:, 2:4]
    counts = cnt[0, :N_EXPERTS].astype(I32)
    padded = (counts + tb - 1) // tb * tb
    pend = jnp.cumsum(padded)
    pstart = pend - padded
    dest = (pstart[idx] + rank).astype(I32).reshape(-1)
    nblk = (2 * n) // tb + N_EXPERTS
    pads = jnp.stack([jnp.append(pstart + counts, pend[-1]), jnp.append(pend, nblk * tb)]).astype(I32)
    nused = (pend[-1] // tb).astype(I32).reshape(1)
    blk_exp = jnp.minimum(jnp.searchsorted(pend, jnp.arange(nblk, dtype=I32) * tb, side='right'),
                          N_EXPERTS - 1).astype(I32)
    xb = _dispatch(x2d, dest, pads, nblk)
    yb = _experts(xb, w13, w2, nused, blk_exp, nblk)
    return _combine(yb, dest, x2d, wts, g, b)


def kernel(x, mem, positions, rel_bias_table, hgrn_lb_logits, w_in, mla_q_norm, mla_w_uq, mla_kv_norm, mla_w_ukv, swa_sinks, hgrn_norm, w_branch, w_out, ln_g, ln_b, xa_wq, xa_wkv, xa_wo, ffn_w13, ffn_w2, moe_router, moe_w13, moe_w2):
    batch, seq, _ = x.shape
    n = batch * seq
    sm = jax.nn.softmax(hgrn_lb_logits.astype(F32), axis=0)
    lower_bounds = jnp.cumsum(sm, axis=0) - sm[0]
    ctab, stab = _rope_tables(positions)
    xc = x.reshape(n, D_MODEL)
    for l in range(DEPTH):
        wts = _inproj_weights(w_in[l], mla_w_uq[l], mla_w_ukv[l])
        mq, mk, mv, swq, swk, swv, hg, sbq, sbk, sbv = _inproj(xc, wts, ctab, stab, mla_q_norm[l], mla_kv_norm[l])
        y_mla = _mla_attention(mq, mk, mv, batch, seq)
        y_swa = _swa_attention(swq, swk, swv, positions, swa_sinks[l], rel_bias_table, batch, seq)
        y_hg = _hgrn(hg, lower_bounds[l], hgrn_norm[l], batch, seq)
        y_sb = _sb_attention(sbq, sbk, sbv, batch, seq)
        go = _IN_OFF['gates']
        xc = _merge(xc, (y_mla, y_swa, y_hg, y_sb), w_in[l][:, go:].astype(BF16), w_branch[l].astype(BF16),
                    w_out[l].astype(BF16), ln_g[l, 0], ln_b[l, 0])
        mk_, mv_ = _memkv(mem, xa_wkv[l].astype(BF16))
        xc = _xattn(xc, (xa_wq[l] * QK_SCALE).astype(BF16), mk_, mv_, xa_wo[l].astype(BF16),
                    ln_g[l, 1], ln_b[l, 1], batch, seq)
        if l % 2 == 0:
            xc = _ffn(xc, ffn_w13[l // 2].astype(BF16), ffn_w2[l // 2].astype(BF16), ln_g[l, 2], ln_b[l, 2])
        else:
            xc = _moe(xc, moe_router[l // 2], moe_w13[l // 2].astype(BF16), moe_w2[l // 2].astype(BF16),
                      ln_g[l, 2], ln_b[l, 2])
    return xc.reshape(batch, seq, D_MODEL)
```

```python
import functools
import math

import jax
import jax.numpy as jnp
from jax import lax
from jax.experimental import pallas as pl
from jax.experimental.pallas import tpu as pltpu

F32 = jnp.float32
BF16 = jnp.bfloat16
I32 = jnp.int32

D_MODEL = 1024
DEPTH = 2
EPS = 1e-5
NEG_BIG = -1e30
LANES = 128
HEAD_DIM = 64
N_HEADS = 4
WIDTH = N_HEADS * HEAD_DIM

MLA_Q_LORA = 256
MLA_KV_LORA = 128
MLA_NOPE = 64
MLA_ROPE = 32
ROPE_THETA = 10000.0
MLA_SCALE = (MLA_NOPE + MLA_ROPE) ** -0.5
LOG2E = math.log2(math.e)
QK_SCALE = HEAD_DIM ** -0.5

SB_RUN_FLOOR = -150.0
SWA_WINDOW = 128
REL_BUCKETS = 32
REL_MAX_DIST = 128
HGRN_CHUNK = 64
HGRN_BLOCK = 16
N_EXPERTS = 8
F_DENSE = 2816
F_EXPERT = 3584
ALPHA = (2 * DEPTH) ** 0.25

_IN_SPLITS = (('mla_cq', 256), ('mla_ckv', 128), ('mla_kr', 32), ('swa_q', 256), ('swa_k', 128),
              ('swa_v', 128), ('hgrn', 1024), ('sb_q', 256), ('sb_k', 256), ('sb_v', 256), ('gates', 4096))
_IN_OFF = {}
_o = 0
for _n, _w in _IN_SPLITS:
    _IN_OFF[_n] = _o
    _o += _w

_A_SPLITS = (('cq', 256), ('ckv', 128), ('kra', 128), ('krb', 128), ('swa_q', 256), ('swa_k', 256),
             ('swa_v', 256), ('hgrn', 1024), ('sb_q', 256), ('sb_k', 256), ('sb_v', 256))
_A_OFF = {}
_o = 0
for _n, _w in _A_SPLITS:
    _A_OFF[_n] = (_o, _o + _w)
    _o += _w
A_COLS = _o

TM_A = 512
TQ_ATT = 256
MLA_TQ = 512
MLA_TK = 512
MLA_GROUP = 4
SWA_TQ = 512
HG_ROWS = 256
TM_FFN = 512
TF_FFN = 1408
MOE_TB = 768
MOE_TF = 512
TM_COMB = 256
TM_DISP = 512
VMEM_LIMIT = 56 * 1024 * 1024


def _cparams(sem, **kw):
    return pltpu.CompilerParams(dimension_semantics=sem, vmem_limit_bytes=VMEM_LIMIT, **kw)


def _const_spec(shape):
    nd = len(shape)
    return pl.BlockSpec(shape, lambda *_: (0,) * nd, pipeline_mode=pl.Buffered(1))


def _layernorm(v, g, b):
    mu = jnp.mean(v, axis=-1, keepdims=True)
    vc = v - mu
    var = jnp.mean(vc * vc, axis=-1, keepdims=True)
    return vc * lax.rsqrt(var + EPS) * g + b


def _dot(a, b):
    return jnp.dot(a, b, preferred_element_type=F32)


def _dot_nt(a, b):
    return lax.dot_general(a, b, (((1,), (1,)), ((), ())), preferred_element_type=F32)


def _split3(a):
    hi = a.astype(BF16)
    r = a - hi.astype(F32)
    mid = r.astype(BF16)
    lo = (r - mid.astype(F32)).astype(BF16)
    return hi, mid, lo


def _rope_kernel(pos_ref, freq_ref, c_ref, s_ref):
    lane = lax.broadcasted_iota(I32, pos_ref.shape, 1)
    ang = pos_ref[...] * freq_ref[...]
    rope = (lane >= MLA_NOPE) & (lane < MLA_NOPE + MLA_ROPE)
    first = lane < MLA_NOPE + MLA_ROPE // 2
    c_ref[...] = jnp.where(lane < MLA_NOPE, 1.0, jnp.where(rope, jnp.cos(ang), 0.0))
    sn = jnp.sin(ang)
    s_ref[...] = jnp.where(rope, jnp.where(first, -sn, sn), 0.0)


def _rope_tables(positions):
    n = positions.size
    half = MLA_ROPE // 2
    inv_freq = ROPE_THETA ** (-jnp.arange(half, dtype=F32) / half)
    freq = jnp.zeros((1, LANES), F32).at[0, MLA_NOPE:MLA_NOPE + MLA_ROPE].set(jnp.tile(inv_freq, 2))
    posb = jnp.broadcast_to(positions.reshape(n, 1).astype(F32), (n, LANES))
    tm = 1024
    return pl.pallas_call(
        _rope_kernel,
        grid=(n // tm,),
        in_specs=[pl.BlockSpec((tm, LANES), lambda i: (i, 0)), _const_spec((1, LANES))],
        out_specs=[pl.BlockSpec((tm, LANES), lambda i: (i, 0))] * 2,
        out_shape=[jax.ShapeDtypeStruct((n, LANES), F32)] * 2,
        compiler_params=_cparams(("parallel",)),
        name="rope_tables",
    )(posb, freq)


def _inproj_kernel(x_ref, w_ref, c_ref, s_ref, qn_ref, kvn_ref, wuqa_ref, wuqb_ref, wuk_ref, wuv_ref,
                   mq_ref, mk_ref, mv_ref, swq_ref, swk_ref, swv_ref, hg_ref, sbq_ref, sbk_ref, sbv_ref):
    h = _dot(x_ref[...].astype(BF16), w_ref[...])

    def cols(name):
        lo, hi = _A_OFF[name]
        return h[:, lo:hi]

    c = c_ref[...]
    s = s_ref[...]
    c4 = jnp.concatenate([c] * N_HEADS, axis=1)
    s4 = jnp.concatenate([s] * N_HEADS, axis=1)

    cq = cols('cq')
    cqn = (cq * lax.rsqrt(jnp.mean(cq * cq, axis=-1, keepdims=True) + EPS) * qn_ref[...]).astype(BF16)
    q = _dot(cqn, wuqa_ref[...]) * c4 + _dot(cqn, wuqb_ref[...]) * s4
    mq_ref[...] = (q * (MLA_SCALE * LOG2E)).astype(BF16)

    ckv = cols('ckv')
    ckvn = (ckv * lax.rsqrt(jnp.mean(ckv * ckv, axis=-1, keepdims=True) + EPS) * kvn_ref[...]).astype(BF16)
    krot = cols('kra') * c + cols('krb') * s
    mk_ref[...] = (_dot(ckvn, wuk_ref[...]) + jnp.concatenate([krot] * N_HEADS, axis=1)).astype(BF16)
    mv_ref[...] = _dot(ckvn, wuv_ref[...]).astype(BF16)

    swq_ref[...] = cols('swa_q').astype(BF16)
    swk_ref[...] = cols('swa_k').astype(BF16)
    swv_ref[...] = cols('swa_v').astype(BF16)
    hg_ref[...] = cols('hgrn')
    sbq_ref[...] = cols('sb_q').astype(BF16)
    sbk_ref[...] = cols('sb_k').astype(BF16)
    sbv_ref[...] = cols('sb_v').astype(BF16)


def _inproj_weights(w_in, w_uq, w_ukv):
    def seg(name, width):
        o = _IN_OFF[name]
        return w_in[:, o:o + width]

    kr = seg('mla_kr', MLA_ROPE)
    half = MLA_ROPE // 2
    z64 = jnp.zeros((D_MODEL, MLA_NOPE), F32)
    z32 = jnp.zeros((D_MODEL, LANES - MLA_NOPE - MLA_ROPE), F32)
    kra = jnp.concatenate([z64, kr, z32], axis=1)
    krb = jnp.concatenate([z64, kr[:, half:], kr[:, :half], z32], axis=1)
    swk = seg('swa_k', 128)
    swv = seg('swa_v', 128)
    dup = lambda t: jnp.concatenate([t[:, :64], t[:, :64], t[:, 64:], t[:, 64:]], axis=1)
    w_a = jnp.concatenate([
        seg('mla_cq', 256), seg('mla_ckv', 128), kra, krb,
        seg('swa_q', 256) * QK_SCALE, dup(swk), dup(swv),
        seg('hgrn', 1024), seg('sb_q', 256) * (QK_SCALE * LOG2E), seg('sb_k', 256), seg('sb_v', 256)], axis=1)

    qd = MLA_NOPE + MLA_ROPE
    zq = jnp.zeros((MLA_Q_LORA, LANES - qd), F32)
    zn = jnp.zeros((MLA_Q_LORA, MLA_NOPE), F32)
    qa, qb = [], []
    for hh in range(N_HEADS):
        nope = w_uq[:, hh * qd: hh * qd + MLA_NOPE]
        rope = w_uq[:, hh * qd + MLA_NOPE: (hh + 1) * qd]
        qa += [nope, rope, zq]
        qb += [zn, rope[:, half:], rope[:, :half], zq]
    wuqa = jnp.concatenate(qa, axis=1)
    wuqb = jnp.concatenate(qb, axis=1)
    lane = jnp.arange(N_HEADS * LANES) % LANES
    wuk = jnp.where(lane[None, :] < MLA_NOPE, w_ukv, 0.0)
    wuv = jnp.concatenate([w_ukv[:, hh * LANES + MLA_NOPE:(hh + 1) * LANES] for hh in range(N_HEADS)], axis=1)
    return tuple(t.astype(BF16) for t in (w_a, wuqa, wuqb, wuk, wuv))


def _inproj(x2d, wts, ctab, stab, q_norm, kv_norm):
    n = x2d.shape[0]
    w_a, wuqa, wuqb, wuk, wuv = wts
    tm = TM_A
    row = lambda w: pl.BlockSpec((tm, w), lambda i: (i, 0))
    out_w = (512, 512, 256, 256, 256, 256, 1024, 256, 256, 256)
    out_dt = (BF16, BF16, BF16, BF16, BF16, BF16, F32, BF16, BF16, BF16)
    return pl.pallas_call(
        _inproj_kernel,
        grid=(n // tm,),
        in_specs=[row(D_MODEL), _const_spec(w_a.shape), row(LANES), row(LANES),
                  _const_spec((1, MLA_Q_LORA)), _const_spec((1, MLA_KV_LORA)),
                  _const_spec(wuqa.shape), _const_spec(wuqb.shape), _const_spec(wuk.shape),
                  _const_spec(wuv.shape)],
        out_specs=[row(w) for w in out_w],
        out_shape=[jax.ShapeDtypeStruct((n, w), d) for w, d in zip(out_w, out_dt)],
        compiler_params=_cparams(("parallel",)),
        name="inproj",
    )(x2d, w_a, ctab, stab, q_norm.reshape(1, -1), kv_norm.reshape(1, -1), wuqa, wuqb, wuk, wuv)


def _half_mask(half):
    lane = lax.broadcasted_iota(I32, (1, LANES), 1)
    return (lane < HEAD_DIM) if half == 0 else (lane >= HEAD_DIM)


def _mla_kernel(q_ref, k_ref, v_ref, o_ref):
    tq = q_ref.shape[0]
    tk = MLA_TK
    nsub = tq // tk
    i = pl.program_id(1)
    row = lax.broadcasted_iota(I32, (tq, tk), 0)
    col = lax.broadcasted_iota(I32, (tq, tk), 1)
    ones = jnp.ones((1, LANES), BF16)

    def update(off, carry, heads, mask, width=tk):
        ss = [_dot_nt(q_ref[:, hh * LANES:(hh + 1) * LANES],
                      k_ref[pl.ds(off, width), hh * LANES:(hh + 1) * LANES]) for hh in heads]
        if mask is not None:
            ss = [jnp.where(mask, s, NEG_BIG) for s in ss]
        ms = [jnp.maximum(c[0], jnp.max(s, axis=-1, keepdims=True)) for c, s in zip(carry, ss)]
        pms = [jnp.exp2(s - m).astype(BF16) for s, m in zip(ss, ms)]
        new = []
        for n, hh in enumerate(heads):
            vb = v_ref[pl.ds(off, width), (hh // 2) * LANES:(hh // 2 + 1) * LANES]
            vb = jnp.where(_half_mask(hh % 2), vb, ones)
            m, acc = carry[n]
            new.append((ms[n], jnp.exp2(m - ms[n]) * acc + _dot(pms[n], vb)))
        return tuple(new)

    accs = []
    for g in range(0, N_HEADS, MLA_GROUP):
        heads = tuple(range(g, g + MLA_GROUP))
        init = tuple((jnp.full((tq, 1), NEG_BIG, F32), jnp.zeros((tq, LANES), F32)) for _ in heads)
        nkb = i * nsub
        carry = lax.fori_loop(
            0, nkb // 2,
            lambda j, c, heads=heads: update(pl.multiple_of(j * 2 * tk, 2 * tk), c, heads, None, 2 * tk), init)
        carry = lax.cond(
            nkb % 2 == 1,
            lambda c, heads=heads: update(pl.multiple_of((nkb - 1) * tk, tk), c, heads, None),
            lambda c: c, carry)
        for r in range(nsub):
            carry = update(pl.multiple_of(i * tq + r * tk, tk), carry, heads, col + r * tk <= row)
        accs += [c[1] for c in carry]
    outs = []
    for p in range(N_HEADS // 2):
        a0, a1 = accs[2 * p], accs[2 * p + 1]
        outs.append(jnp.where(_half_mask(0), a0 / a0[:, HEAD_DIM:HEAD_DIM + 1], a1 / a1[:, 0:1]))
    o_ref[...] = jnp.concatenate(outs, axis=1).astype(o_ref.dtype)


def _mla_attention(q, k, v, batch, seq):
    tq = MLA_TQ
    q3, k3, v3 = (t.reshape(batch, seq, t.shape[-1]) for t in (q, k, v))
    out = pl.pallas_call(
        _mla_kernel,
        grid=(batch, seq // tq),
        in_specs=[pl.BlockSpec((None, tq, 512), lambda b, i: (b, i, 0)),
                  pl.BlockSpec((None, seq, 512), lambda b, i: (b, 0, 0)),
                  pl.BlockSpec((None, seq, WIDTH), lambda b, i: (b, 0, 0))],
        out_specs=pl.BlockSpec((None, tq, WIDTH), lambda b, i: (b, i, 0)),
        out_shape=jax.ShapeDtypeStruct((batch, seq, WIDTH), BF16),
        compiler_params=_cparams(("parallel", "arbitrary")),
        name="mla_attention",
    )(q3, k3, v3)
    return out.reshape(batch * seq, WIDTH)


def _sb_kernel(q_ref, k_ref, v_ref, o_ref):
    tq = q_ref.shape[0]
    i = pl.program_id(1)
    row = lax.broadcasted_iota(I32, (tq, tq), 0)
    col = lax.broadcasted_iota(I32, (tq, tq), 1)
    strict = col < row
    later = (row > col).astype(BF16)
    qs = []
    for hh in range(N_HEADS):
        qp = q_ref[:, (hh // 2) * LANES:(hh // 2 + 1) * LANES]
        qs.append(jnp.where(_half_mask(hh % 2), qp, jnp.zeros_like(qp)))

    def block(j, carry, diag):
        off = pl.multiple_of(j * tq, tq)
        runs, accs = carry
        heads = range(N_HEADS)
        zs = [_dot_nt(qs[hh], k_ref[pl.ds(off, tq), (hh // 2) * LANES:(hh // 2 + 1) * LANES]) for hh in heads]
        lsps = [jnp.minimum(z, 0.0) - jnp.log2(1.0 + jnp.exp2(-jnp.abs(z))) for z in zs]
        lsns = [lsp - z for lsp, z in zip(lsps, zs)]
        if diag:
            lsns = [jnp.where(strict, t, 0.0) for t in lsns]
        his = [t.astype(BF16) for t in lsns]
        los = [(t - hi.astype(F32)).astype(BF16) for t, hi in zip(lsns, his)]
        rems = [_dot(hi, later) + _dot(lo, later) for hi, lo in zip(his, los)]
        args = [lsps[hh] + rems[hh] + runs[hh] for hh in heads]
        if diag:
            args = [jnp.where(strict, t, NEG_BIG) for t in args]
        probs = [jnp.exp2(t).astype(BF16) for t in args]
        new_runs = tuple(runs[hh] + rems[hh][:, 0:1] + lsns[hh][:, 0:1] for hh in heads)
        new_accs = list(accs)
        for hh in heads:
            p = hh // 2
            vb = v_ref[pl.ds(off, tq), p * LANES:(p + 1) * LANES]
            vb = jnp.where(_half_mask(hh % 2), vb, jnp.zeros_like(vb))
            new_accs[p] = new_accs[p] + _dot(probs[hh], vb)
        return new_runs, tuple(new_accs)

    init = (tuple(jnp.zeros((tq, 1), F32) for _ in range(N_HEADS)),
            tuple(jnp.zeros((tq, LANES), F32) for _ in range(N_HEADS // 2)))
    def still_active(runs):
        top = functools.reduce(jnp.maximum, runs)
        return (jnp.max(top) > SB_RUN_FLOOR).astype(I32)

    runs, accs = block(i, init, True)

    def cond(c):
        return (c[0] < i) & (c[1] > 0)

    def body(c):
        jj, _, runs, accs = c
        runs, accs = block(i - 1 - jj, (runs, accs), False)
        return jj + 1, still_active(runs), runs, accs

    _, _, _, accs = lax.while_loop(cond, body, (jnp.int32(0), still_active(runs), runs, accs))
    o_ref[...] = jnp.concatenate(accs, axis=1).astype(o_ref.dtype)


def _sb_attention(q, k, v, batch, seq):
    tq = TQ_ATT
    q3, k3, v3 = (t.reshape(batch, seq, WIDTH) for t in (q, k, v))
    out = pl.pallas_call(
        _sb_kernel,
        grid=(batch, seq // tq),
        in_specs=[pl.BlockSpec((None, tq, WIDTH), lambda b, i: (b, i, 0)),
                  pl.BlockSpec((None, seq, WIDTH), lambda b, i: (b, 0, 0)),
                  pl.BlockSpec((None, seq, WIDTH), lambda b, i: (b, 0, 0))],
        out_specs=pl.BlockSpec((None, tq, WIDTH), lambda b, i: (b, i, 0)),
        out_shape=jax.ShapeDtypeStruct((batch, seq, WIDTH), BF16),
        compiler_params=_cparams(("parallel", "arbitrary")),
        name="stick_breaking",
    )(q3, k3, v3)
    return out.reshape(batch * seq, WIDTH)


def _rel_bucket(dist):
    exact = REL_BUCKETS // 2
    n = jnp.maximum(dist, 0)
    nf = jnp.maximum(n, 1).astype(F32)
    large = exact + (jnp.log(nf / exact) / math.log(REL_MAX_DIST / exact) * (REL_BUCKETS - exact)).astype(I32)
    large = jnp.clip(large, 0, REL_BUCKETS - 1)
    return jnp.where(n < exact, n, large)


def _swa_kernel(sink_ref, tab_ref, q_ref, kc_ref, kh_ref, vc_ref, vh_ref, pq_ref, pkc_ref, pkh_ref, o_ref):
    w = SWA_WINDOW
    step = pl.program_id(1)
    row = lax.broadcasted_iota(I32, (w, w), 0)
    col = lax.broadcasted_iota(I32, (w, w), 1)
    valid_c = col <= row
    valid_p = col > row
    tabs = [jnp.broadcast_to(tab_ref[hh:hh + 1, :], (w, LANES)) for hh in range(N_HEADS)]
    ones = jnp.ones((1, LANES), BF16)
    nsub = q_ref.shape[0] // w
    chains = [(r, hh) for r in range(nsub) for hh in range(N_HEADS)]

    def keys(ref, halo_ref, r, hh):
        sl = slice((hh // 2) * LANES, (hh // 2 + 1) * LANES)
        cur = ref[r * w:(r + 1) * w, sl]
        prev = ref[(r - 1) * w:r * w, sl] if r else halo_ref[:, sl]
        return cur, prev

    buckets = []
    for r in range(nsub):
        pq = pq_ref[r * w:(r + 1) * w, :]
        pk_prev = pkc_ref[:, (r - 1) * w:r * w] if r else pkh_ref[...]
        buckets.append((_rel_bucket(pq - pkc_ref[:, r * w:(r + 1) * w]), _rel_bucket(pq - pk_prev)))
    logits = []
    for r, hh in chains:
        qp = q_ref[r * w:(r + 1) * w, (hh // 2) * LANES:(hh // 2 + 1) * LANES]
        qh = jnp.where(_half_mask(hh % 2), qp, jnp.zeros_like(qp))
        kc, kp = keys(kc_ref, kh_ref, r, hh)
        logits.append((_dot_nt(qh, kc), _dot_nt(qh, kp)))
    masked = []
    for (r, hh), (lc, lp) in zip(chains, logits):
        lc = jnp.where(valid_c, lc + jnp.take_along_axis(tabs[hh], buckets[r][0], axis=1), NEG_BIG)
        lp = lp + jnp.take_along_axis(tabs[hh], buckets[r][1], axis=1)
        lp = jnp.where(valid_p if r else valid_p & (step > 0), lp, NEG_BIG)
        masked.append((lc, lp))
    maxes = [jnp.maximum(jnp.maximum(jnp.max(lc, axis=-1, keepdims=True), jnp.max(lp, axis=-1, keepdims=True)),
                         sink_ref[hh]) for (r, hh), (lc, lp) in zip(chains, masked)]
    probs = [(jnp.exp(lc - m).astype(BF16), jnp.exp(lp - m).astype(BF16)) for (lc, lp), m in zip(masked, maxes)]
    outs = {}
    for (r, hh), (ec, ep), m in zip(chains, probs, maxes):
        vc, vp = keys(vc_ref, vh_ref, r, hh)
        mine = _half_mask(hh % 2)
        acc = _dot(ec, jnp.where(mine, vc, ones)) + _dot(ep, jnp.where(mine, vp, ones))
        den = (acc[:, 0:1] if hh % 2 else acc[:, HEAD_DIM:HEAD_DIM + 1]) + jnp.exp(sink_ref[hh] - m)
        outs[(r, hh)] = acc / den
    for r in range(nsub):
        pairs = [jnp.where(_half_mask(0), outs[(r, 2 * p)], outs[(r, 2 * p + 1)]) for p in range(N_HEADS // 2)]
        o_ref[r * w:(r + 1) * w, :] = jnp.concatenate(pairs, axis=1).astype(o_ref.dtype)


def _swa_attention(q, k, v, positions, sinks, rel_table, batch, seq):
    w = SWA_WINDOW
    tq = SWA_TQ
    per = tq // w
    q3, k3, v3 = (t.reshape(batch, seq, WIDTH) for t in (q, k, v))
    pcol = positions.reshape(batch, seq, 1)
    prow = positions.reshape(batch, 1, seq)
    tab = jnp.zeros((N_HEADS, LANES), F32).at[:, :REL_BUCKETS].set(rel_table.astype(F32).T)
    cur = lambda b, n: (b, n, 0)
    halo = lambda b, n: (b, jnp.maximum(n * per - 1, 0), 0)
    out = pl.pallas_call(
        _swa_kernel,
        grid=(batch, seq // tq),
        in_specs=[pl.BlockSpec(memory_space=pltpu.SMEM), _const_spec((N_HEADS, LANES)),
                  pl.BlockSpec((None, tq, WIDTH), cur),
                  pl.BlockSpec((None, tq, WIDTH), cur), pl.BlockSpec((None, w, WIDTH), halo),
                  pl.BlockSpec((None, tq, WIDTH), cur), pl.BlockSpec((None, w, WIDTH), halo),
                  pl.BlockSpec((None, tq, 1), cur),
                  pl.BlockSpec((None, 1, tq), lambda b, n: (b, 0, n)),
                  pl.BlockSpec((None, 1, w), lambda b, n: (b, 0, jnp.maximum(n * per - 1, 0)))],
        out_specs=pl.BlockSpec((None, tq, WIDTH), cur),
        out_shape=jax.ShapeDtypeStruct((batch, seq, WIDTH), BF16),
        compiler_params=_cparams(("parallel", "arbitrary")),
        name="swa_attention",
    )(sinks.astype(F32), tab, q3, k3, k3, v3, v3, pcol, prow, prow)
    return out.reshape(batch * seq, WIDTH)


def _hgrn_kernel(hg_ref, lb_ref, nw_ref, o_ref, state_ref):
    c = HGRN_CHUNK
    blk = HGRN_BLOCK

    @pl.when(pl.program_id(1) == 0)
    def _():
        state_ref[...] = jnp.zeros_like(state_ref)

    r64 = lax.broadcasted_iota(I32, (c, c), 0)
    c64 = lax.broadcasted_iota(I32, (c, c), 1)
    incl = (c64 <= r64).astype(BF16)
    ra = lax.broadcasted_iota(I32, (WIDTH, WIDTH), 0) // HEAD_DIM
    ca = lax.broadcasted_iota(I32, (WIDTH, WIDTH), 1) // HEAD_DIM
    same_head = ra == ca
    seg = same_head.astype(BF16)
    ones_cols = jnp.ones((c, LANES), BF16)
    trow = lax.broadcasted_iota(I32, (blk, WIDTH), 0)
    caps = [jnp.where(trow >= s_i, 0.0, NEG_BIG) for s_i in range(blk)]
    lane_head = lax.broadcasted_iota(I32, (1, WIDTH), 1) // HEAD_DIM
    lb = lb_ref[...]
    nw = nw_ref[...]
    dn0 = (((0,), (0,)), ((), ()))

    for ch in range(hg_ref.shape[0] // c):
        rows = slice(ch * c, (ch + 1) * c)
        qraw = hg_ref[rows, 0:WIDTH]
        fraw = hg_ref[rows, WIDTH:2 * WIDTH]
        v = hg_ref[rows, 2 * WIDTH:3 * WIDTH]
        graw = hg_ref[rows, 3 * WIDTH:4 * WIDTH]
        qf = qraw * jax.nn.sigmoid(qraw)
        forget = lb + (1.0 - lb) * jax.nn.sigmoid(fraw)
        lf = jnp.log(forget)
        kk = 1.0 - forget
        gate = graw * jax.nn.sigmoid(graw)
        vb = v.astype(BF16)

        lf3 = _split3(lf)
        bc = _dot(incl, lf3[0]) + _dot(incl, lf3[1]) + _dot(incl, lf3[2])
        b_last = bc[c - 1:c, :]
        tot_col = sum(lax.dot_general(t, ones_cols, dn0, preferred_element_type=F32) for t in lf3)
        decay_col = jnp.exp(jnp.concatenate([tot_col, tot_col], axis=1))

        state = state_ref[...]
        o_inter = _dot((qf * jnp.exp(bc)).astype(BF16), state.astype(BF16))

        def before(qa, qb, ka, kb):
            ref = bc[kb - 1:kb, :]
            qt = qf[qa:qb] * jnp.exp(bc[qa:qb] - ref)
            kt = (kk[ka:kb] * jnp.exp(ref - bc[ka:kb])).astype(BF16)
            qs = jnp.concatenate([jnp.where(lane_head == hh, qt, 0.0) for hh in range(N_HEADS)], axis=0)
            att = _dot_nt(qs.astype(BF16), kt)
            mix = _dot(att.astype(BF16), vb[ka:kb])
            nq = qb - qa
            return sum(jnp.where(lane_head == hh, mix[hh * nq:(hh + 1) * nq], 0.0) for hh in range(N_HEADS))

        bc2 = bc * LOG2E

        def inside(a):
            b2 = bc2[a:a + blk]
            qb_ = qf[a:a + blk]
            ws = []
            for s_i in range(blk):
                e = jnp.exp2(jnp.minimum(b2 - b2[s_i:s_i + 1, :], caps[s_i]))
                ws.append((qb_ * kk[a + s_i:a + s_i + 1, :] * e).astype(BF16))
            att = _dot(jnp.concatenate(ws, axis=0), seg)
            return sum(att[s_i * blk:(s_i + 1) * blk] * v[a + s_i:a + s_i + 1, :] for s_i in range(blk))

        half = c // 2
        far = before(half, c, 0, half)
        intra = [inside(0),
                 inside(blk) + before(blk, half, 0, blk),
                 inside(half) + far[:blk],
                 inside(half + blk) + far[blk:] + before(half + blk, c, half, half + blk)]
        o = o_inter + jnp.concatenate(intra, axis=0)

        khat = (kk * jnp.exp(b_last - bc)).astype(BF16)
        upd = lax.dot_general(khat, vb, dn0, preferred_element_type=F32)
        state_ref[...] = decay_col * state + jnp.where(same_head, upd, 0.0)

        o2 = _split3(o * o)
        ms = (_dot(o2[0], seg) + _dot(o2[1], seg)) * (1.0 / HEAD_DIM)
        o_ref[rows, :] = (o * lax.rsqrt(ms + EPS) * nw * gate).astype(o_ref.dtype)


def _hgrn(hg, lower_bound, norm_w, batch, seq):
    rows = HG_ROWS
    hg3 = hg.reshape(batch, seq, 4 * WIDTH)
    out = pl.pallas_call(
        _hgrn_kernel,
        grid=(batch, seq // rows),
        in_specs=[pl.BlockSpec((None, rows, 4 * WIDTH), lambda b, i: (b, i, 0)),
                  _const_spec((1, WIDTH)), _const_spec((1, WIDTH))],
        out_specs=pl.BlockSpec((None, rows, WIDTH), lambda b, i: (b, i, 0)),
        out_shape=jax.ShapeDtypeStruct((batch, seq, WIDTH), BF16),
        scratch_shapes=[pltpu.VMEM((WIDTH, WIDTH), F32)],
        compiler_params=_cparams(("parallel", "arbitrary")),
        name="hgrn2",
    )(hg3, lower_bound.reshape(1, WIDTH).astype(F32), norm_w.reshape(1, WIDTH).astype(F32))
    return out.reshape(batch * seq, WIDTH)


def _merge_kernel(x_ref, y0_ref, y1_ref, y2_ref, y3_ref, wg_ref, wb_ref, wo_ref, g_ref, b_ref, o_ref):
    x = x_ref[...]
    xb = x.astype(BF16)
    merged = jnp.zeros(x.shape, F32)
    for nbr, y_ref in enumerate((y0_ref, y1_ref, y2_ref, y3_ref)):
        gate = jax.nn.sigmoid(_dot(xb, wg_ref[:, nbr * D_MODEL:(nbr + 1) * D_MODEL]))
        merged = merged + gate * _dot(y_ref[...], wb_ref[nbr])
    y = _dot(merged.astype(BF16), wo_ref[...])
    o_ref[...] = _layernorm(ALPHA * x + y, g_ref[...], b_ref[...])


def _merge(x2d, ys, wg, wb, wo, g, b):
    n = x2d.shape[0]
    tm = TM_A
    row = lambda w: pl.BlockSpec((tm, w), lambda i: (i, 0))
    return pl.pallas_call(
        _merge_kernel,
        grid=(n // tm,),
        in_specs=[row(D_MODEL)] + [row(WIDTH)] * 4 +
                 [_const_spec(wg.shape), _const_spec(wb.shape), _const_spec(wo.shape),
                  _const_spec((1, D_MODEL)), _const_spec((1, D_MODEL))],
        out_specs=row(D_MODEL),
        out_shape=jax.ShapeDtypeStruct((n, D_MODEL), F32),
        compiler_params=_cparams(("parallel",)),
        name="merge_outproj_ln",
    )(x2d, *ys, wg, wb, wo, g.reshape(1, -1), b.reshape(1, -1))


def _memkv_kernel(m_ref, w_ref, k_ref, v_ref):
    kv = _dot(m_ref[...].astype(BF16), w_ref[...])
    k_ref[...] = kv[:, :WIDTH].astype(BF16)
    v_ref[...] = kv[:, WIDTH:].astype(BF16)


def _memkv(mem, wkv):
    batch, m, _ = mem.shape
    return pl.pallas_call(
        _memkv_kernel,
        grid=(batch,),
        in_specs=[pl.BlockSpec((None, m, D_MODEL), lambda b: (b, 0, 0)), _const_spec(wkv.shape)],
        out_specs=[pl.BlockSpec((None, m, WIDTH), lambda b: (b, 0, 0))] * 2,
        out_shape=[jax.ShapeDtypeStruct((batch, m, WIDTH), BF16)] * 2,
        compiler_params=_cparams(("parallel",)),
        name="mem_kv",
    )(mem, wkv)


def _xattn_kernel(x_ref, wq_ref, k_ref, v_ref, wo_ref, g_ref, b_ref, o_ref):
    x = x_ref[...]
    q = _dot(x.astype(BF16), wq_ref[...]).astype(BF16)
    k = k_ref[...]
    v = v_ref[...]
    lane = lax.broadcasted_iota(I32, (1, WIDTH), 1) // HEAD_DIM
    heads = range(N_HEADS)
    ss = [_dot_nt(jnp.where(lane == hh, q, jnp.zeros_like(q)), k) for hh in heads]
    es = [jnp.exp(s - jnp.max(s, axis=-1, keepdims=True)) for s in ss]
    ps = [(e / jnp.sum(e, axis=-1, keepdims=True)).astype(BF16) for e in es]
    o = jnp.zeros((x.shape[0], WIDTH), F32)
    for hh in heads:
        o = o + jnp.where(lane == hh, _dot(ps[hh], v), 0.0)
    y = _dot(o.astype(BF16), wo_ref[...])
    o_ref[...] = _layernorm(ALPHA * x + y, g_ref[...], b_ref[...])


def _xattn(x2d, wq, k, v, wo, g, b, batch, seq):
    tm = TM_A
    m = k.shape[1]
    x3 = x2d.reshape(batch, seq, D_MODEL)
    row = pl.BlockSpec((None, tm, D_MODEL), lambda bb, i: (bb, i, 0))
    kv_spec = pl.BlockSpec((None, m, WIDTH), lambda bb, i: (bb, 0, 0))
    out = pl.pallas_call(
        _xattn_kernel,
        grid=(batch, seq // tm),
        in_specs=[row, _const_spec(wq.shape), kv_spec, kv_spec, _const_spec(wo.shape),
                  _const_spec((1, D_MODEL)), _const_spec((1, D_MODEL))],
        out_specs=row,
        out_shape=jax.ShapeDtypeStruct((batch, seq, D_MODEL), F32),
        compiler_params=_cparams(("parallel", "parallel")),
        name="mem_xattn_ln",
    )(x3, wq, k, v, wo, g.reshape(1, -1), b.reshape(1, -1))
    return out.reshape(batch * seq, D_MODEL)


def _ffn_kernel(x_ref, w13_ref, w2_ref, g_ref, b_ref, o_ref):
    x = x_ref[...]
    xb = x.astype(BF16)
    y = None
    for h in range(F_DENSE // TF_FFN):
        lo = h * TF_FFN
        a = _dot(xb, w13_ref[:, lo:lo + TF_FFN])
        gate = _dot(xb, w13_ref[:, F_DENSE + lo:F_DENSE + lo + TF_FFN])
        part = _dot((a * jax.nn.sigmoid(a) * gate).astype(BF16), w2_ref[lo:lo + TF_FFN, :])
        y = part if y is None else y + part
    o_ref[...] = _layernorm(ALPHA * x + y, g_ref[...], b_ref[...])


def _ffn(x2d, w13, w2, g, b):
    n = x2d.shape[0]
    tm = TM_FFN
    return pl.pallas_call(
        _ffn_kernel,
        grid=(n // tm,),
        in_specs=[pl.BlockSpec((tm, D_MODEL), lambda i: (i, 0)),
                  _const_spec(w13.shape), _const_spec(w2.shape),
                  _const_spec((1, D_MODEL)), _const_spec((1, D_MODEL))],
        out_specs=pl.BlockSpec((tm, D_MODEL), lambda i: (i, 0)),
        out_shape=jax.ShapeDtypeStruct((n, D_MODEL), F32),
        compiler_params=_cparams(("parallel",)),
        name="ffn_ln",
    )(x2d, w13, w2, g.reshape(1, -1), b.reshape(1, -1))


def _router_kernel(x_ref, r_ref, info_ref, wts_ref, cnt_ref, carry_ref):
    tm = x_ref.shape[0]

    @pl.when(pl.program_id(0) == 0)
    def _():
        carry_ref[...] = jnp.zeros_like(carry_ref)

    logits = jnp.dot(x_ref[...], r_ref[...], precision=lax.Precision.HIGHEST, preferred_element_type=F32)
    lane = lax.broadcasted_iota(I32, (tm, LANES), 1)
    lg = jnp.where(lane < N_EXPERTS, logits, -jnp.inf)
    m1 = jnp.max(lg, axis=-1, keepdims=True)
    i1 = jnp.min(jnp.where(lg == m1, lane, LANES), axis=-1, keepdims=True)
    lg2 = jnp.where(lane == i1, -jnp.inf, lg)
    m2 = jnp.max(lg2, axis=-1, keepdims=True)
    i2 = jnp.min(jnp.where(lg2 == m2, lane, LANES), axis=-1, keepdims=True)
    e = jnp.exp(m2 - m1)
    w1 = 1.0 / (1.0 + e)
    w2 = e / (1.0 + e)
    sel1 = lane == i1
    sel2 = lane == i2
    chosen = jnp.where(sel1 | sel2, 1.0, 0.0)
    row = lax.broadcasted_iota(I32, (tm, tm), 0)
    col = lax.broadcasted_iota(I32, (tm, tm), 1)
    before = (col < row).astype(BF16)
    ranks = _dot(before, chosen.astype(BF16)) + carry_ref[...]
    r1 = jnp.sum(jnp.where(sel1, ranks, 0.0), axis=-1, keepdims=True).astype(I32)
    r2 = jnp.sum(jnp.where(sel2, ranks, 0.0), axis=-1, keepdims=True).astype(I32)
    carry_ref[...] = carry_ref[...] + jnp.sum(chosen, axis=0, keepdims=True)
    info_ref[...] = jnp.where(lane == 0, i1, jnp.where(lane == 1, i2, jnp.where(lane == 2, r1,
                              jnp.where(lane == 3, r2, 0))))
    wts_ref[...] = jnp.where(lane == 0, w1, jnp.where(lane == 1, w2, 0.0))
    cnt_ref[...] = carry_ref[...]


def _router(x2d, router):
    n = x2d.shape[0]
    tm = TM_A
    r_pad = jnp.zeros((D_MODEL, LANES), F32).at[:, :N_EXPERTS].set(router.astype(F32))
    row = pl.BlockSpec((tm, LANES), lambda i: (i, 0))
    return pl.pallas_call(
        _router_kernel,
        grid=(n // tm,),
        in_specs=[pl.BlockSpec((tm, D_MODEL), lambda i: (i, 0)), _const_spec(r_pad.shape)],
        out_specs=[row, row, pl.BlockSpec((1, LANES), lambda i: (0, 0))],
        out_shape=[jax.ShapeDtypeStruct((n, LANES), I32), jax.ShapeDtypeStruct((n, LANES), F32),
                   jax.ShapeDtypeStruct((1, LANES), F32)],
        scratch_shapes=[pltpu.VMEM((1, LANES), F32)],
        compiler_params=_cparams(("arbitrary",)),
        name="moe_router",
    )(x2d, r_pad)


def _dispatch_kernel(pad_ref, dest_ref, x_ref, xb_hbm, stage_ref, sems):
    tm = x_ref.shape[0]
    i = pl.program_id(0)
    last = pl.num_programs(0) - 1
    slot = i % 2

    def wait_step(s):
        for _ in range(2):
            pltpu.make_async_copy(stage_ref.at[s], xb_hbm.at[pl.ds(0, tm), :], sems.at[s]).wait()

    @pl.when(i >= 2)
    def _():
        wait_step(slot)

    stage_ref[slot] = x_ref[...]

    def issue(r, c):
        for k in range(2):
            pltpu.make_async_copy(stage_ref.at[slot, pl.ds(r, 1), :],
                                  xb_hbm.at[pl.ds(dest_ref[0, 2 * r + k], 1), :], sems.at[slot]).start()
        return c
    lax.fori_loop(0, tm, issue, 0, unroll=8)

    @pl.when(i == last)
    def _():
        def fill(e, c):
            def one(s, c2):
                pltpu.make_async_copy(stage_ref.at[slot, pl.ds(0, 1), :], xb_hbm.at[pl.ds(s, 1), :],
                                      sems.at[2]).start()
                return c2

            def done(s, c2):
                pltpu.make_async_copy(stage_ref.at[slot, pl.ds(0, 1), :], xb_hbm.at[pl.ds(0, 1), :],
                                      sems.at[2]).wait()
                return c2
            lax.fori_loop(pad_ref[0, e], pad_ref[1, e], one, 0)
            lax.fori_loop(pad_ref[0, e], pad_ref[1, e], done, 0)
            return c
        lax.fori_loop(0, pad_ref.shape[1], fill, 0)
        wait_step(slot)

        @pl.when(last >= 1)
        def _():
            wait_step(1 - slot)


def _dispatch(x2d, dest, pads, nblk):
    n = x2d.shape[0]
    tm = TM_DISP
    nt = n // tm
    grid_spec = pltpu.PrefetchScalarGridSpec(
        num_scalar_prefetch=1,
        grid=(nt,),
        in_specs=[pl.BlockSpec((None, 1, 2 * tm), lambda i, pads: (i, 0, 0), memory_space=pltpu.SMEM),
                  pl.BlockSpec((tm, D_MODEL), lambda i, pads: (i, 0))],
        out_specs=pl.BlockSpec(memory_space=pl.ANY),
        scratch_shapes=[pltpu.VMEM((2, tm, D_MODEL), F32), pltpu.SemaphoreType.DMA((3,))],
    )
    return pl.pallas_call(
        _dispatch_kernel,
        grid_spec=grid_spec,
        out_shape=jax.ShapeDtypeStruct((nblk * MOE_TB, D_MODEL), F32),
        compiler_params=_cparams(("arbitrary",), disable_bounds_checks=True),
        name="moe_dispatch",
    )(pads, dest.reshape(nt, 1, 2 * tm), x2d)


def _expert_kernel(nused_ref, bexp_ref, x_ref, w1_ref, w3_ref, w2_ref, o_ref, acc_ref):
    f = pl.program_id(1)

    @pl.when(pl.program_id(0) < nused_ref[0])
    def _():
        xb = x_ref[...].astype(BF16)
        a = _dot(xb, w1_ref[...].astype(BF16))
        gate = _dot(xb, w3_ref[...].astype(BF16))
        part = _dot((a * jax.nn.sigmoid(a) * gate).astype(BF16), w2_ref[...].astype(BF16))

        @pl.when(f == 0)
        def _():
            acc_ref[...] = part

        @pl.when(f > 0)
        def _():
            acc_ref[...] += part

        @pl.when(f == pl.num_programs(1) - 1)
        def _():
            o_ref[...] = acc_ref[...]

    @pl.when(pl.program_id(0) >= nused_ref[0])
    def _():
        o_ref[...] = jnp.zeros_like(o_ref)


def _experts(xb, w13, w2, nused, blk_exp, nblk):
    tb, tf = MOE_TB, MOE_TF
    nf = F_EXPERT // tf

    def blk(i, nu):
        return jnp.maximum(jnp.minimum(i, nu[0] - 1), 0)

    def ftile(i, f, nu):
        return jnp.where(i < nu[0], f, nf - 1)

    grid_spec = pltpu.PrefetchScalarGridSpec(
        num_scalar_prefetch=2,
        grid=(nblk, nf),
        in_specs=[pl.BlockSpec((tb, D_MODEL), lambda i, f, nu, be: (blk(i, nu), 0)),
                  pl.BlockSpec((None, D_MODEL, tf), lambda i, f, nu, be: (be[blk(i, nu)], 0, ftile(i, f, nu))),
                  pl.BlockSpec((None, D_MODEL, tf), lambda i, f, nu, be: (be[blk(i, nu)], 0, nf + ftile(i, f, nu))),
                  pl.BlockSpec((None, tf, D_MODEL), lambda i, f, nu, be: (be[blk(i, nu)], ftile(i, f, nu), 0))],
        out_specs=pl.BlockSpec((tb, D_MODEL), lambda i, f, nu, be: (i, 0)),
        scratch_shapes=[pltpu.VMEM((tb, D_MODEL), F32)],
    )
    return pl.pallas_call(
        _expert_kernel,
        grid_spec=grid_spec,
        out_shape=jax.ShapeDtypeStruct((nblk * tb, D_MODEL), F32),
        compiler_params=_cparams(("arbitrary", "arbitrary")),
        name="moe_experts",
    )(nused, blk_exp, xb, w13, w13, w2)


def _combine_kernel(dest_ref, nxt_ref, y_hbm, x_ref, wts_ref, g_ref, b_ref, o_ref, buf_ref, sems):
    tm = x_ref.shape[0]
    i = pl.program_id(0)
    slot = i % 2

    def gather(idx_ref, s):
        def issue(r, c):
            for k in range(2):
                pltpu.make_async_copy(y_hbm.at[pl.ds(idx_ref[0, 2 * r + k], 1), :],
                                      buf_ref.at[s, k, pl.ds(r, 1), :], sems.at[s]).start()
            return c
        lax.fori_loop(0, tm, issue, 0, unroll=8)

    @pl.when(i == 0)
    def _():
        gather(dest_ref, slot)

    @pl.when(i + 1 < pl.num_programs(0))
    def _():
        gather(nxt_ref, 1 - slot)

    for k in range(2):
        pltpu.make_async_copy(y_hbm.at[pl.ds(0, tm), :], buf_ref.at[slot, k], sems.at[slot]).wait()
    wts = wts_ref[...]
    y = wts[:, 0:1] * buf_ref[slot, 0] + wts[:, 1:2] * buf_ref[slot, 1]
    o_ref[...] = _layernorm(ALPHA * x_ref[...] + y, g_ref[...], b_ref[...])


def _combine(yb, dest, x2d, wts, g, b):
    n = x2d.shape[0]
    tm = TM_COMB
    nt = n // tm
    row = lambda w: pl.BlockSpec((tm, w), lambda i: (i, 0))
    dest3 = dest.reshape(nt, 1, 2 * tm)
    return pl.pallas_call(
        _combine_kernel,
        grid=(nt,),
        in_specs=[pl.BlockSpec((None, 1, 2 * tm), lambda i: (i, 0, 0), memory_space=pltpu.SMEM),
                  pl.BlockSpec((None, 1, 2 * tm), lambda i: (jnp.minimum(i + 1, nt - 1), 0, 0),
                               memory_space=pltpu.SMEM),
                  pl.BlockSpec(memory_space=pl.ANY), row(D_MODEL), row(LANES),
                  _const_spec((1, D_MODEL)), _const_spec((1, D_MODEL))],
        out_specs=row(D_MODEL),
        out_shape=jax.ShapeDtypeStruct((n, D_MODEL), F32),
        scratch_shapes=[pltpu.VMEM((2, 2, tm, D_MODEL), F32), pltpu.SemaphoreType.DMA((2,))],
        compiler_params=_cparams(("arbitrary",), disable_bounds_checks=True),
        name="moe_combine_ln",
    )(dest3, dest3, yb, x2d, wts, g.reshape(1, -1), b.reshape(1, -1))


def _moe(x2d, router, w13, w2, g, b):
    n = x2d.shape[0]
    tb = MOE_TB
    info, wts, cnt = _router(x2d, router)
    idx = info[:, 0:2]
    rank = info[:, 2:4]
    counts = cnt[0, :N_EXPERTS].astype(I32)
    padded = (counts + tb - 1) // tb * tb
    pend = jnp.cumsum(padded)
    pstart = pend - padded
    dest = (pstart[idx] + rank).astype(I32).reshape(-1)
    nblk = -(-(2 * n + N_EXPERTS * (tb - 1)) // tb)
    pads = jnp.stack([jnp.append(pstart + counts, pend[-1]), jnp.append(pend, nblk * tb)]).astype(I32)
    nused = (pend[-1] // tb).astype(I32).reshape(1)
    blk_exp = jnp.minimum(jnp.searchsorted(pend, jnp.arange(nblk, dtype=I32) * tb, side='right'),
                          N_EXPERTS - 1).astype(I32)
    xb = _dispatch(x2d, dest, pads, nblk)
    yb = _experts(xb, w13, w2, nused, blk_exp, nblk)
    return _combine(yb, dest, x2d, wts, g, b)


def kernel(x, mem, positions, rel_bias_table, hgrn_lb_logits, w_in, mla_q_norm, mla_w_uq, mla_kv_norm, mla_w_ukv, swa_sinks, hgrn_norm, w_branch, w_out, ln_g, ln_b, xa_wq, xa_wkv, xa_wo, ffn_w13, ffn_w2, moe_router, moe_w13, moe_w2):
    batch, seq, _ = x.shape
    n = batch * seq
    sm = jax.nn.softmax(hgrn_lb_logits.astype(F32), axis=0)
    lower_bounds = jnp.cumsum(sm, axis=0) - sm[0]
    ctab, stab = _rope_tables(positions)
    xc = x.reshape(n, D_MODEL)
    for l in range(DEPTH):
        wts = _inproj_weights(w_in[l], mla_w_uq[l], mla_w_ukv[l])
        mq, mk, mv, swq, swk, swv, hg, sbq, sbk, sbv = _inproj(xc, wts, ctab, stab, mla_q_norm[l], mla_kv_norm[l])
        y_mla = _mla_attention(mq, mk, mv, batch, seq)
        y_swa = _swa_attention(swq, swk, swv, positions, swa_sinks[l], rel_bias_table, batch, seq)
        y_hg = _hgrn(hg, lower_bounds[l], hgrn_norm[l], batch, seq)
        y_sb = _sb_attention(sbq, sbk, sbv, batch, seq)
        go = _IN_OFF['gates']
        xc = _merge(xc, (y_mla, y_swa, y_hg, y_sb), w_in[l][:, go:].astype(BF16), w_branch[l].astype(BF16),
                    w_out[l].astype(BF16), ln_g[l, 0], ln_b[l, 0])
        mk_, mv_ = _memkv(mem, xa_wkv[l].astype(BF16))
        xc = _xattn(xc, (xa_wq[l] * QK_SCALE).astype(BF16), mk_, mv_, xa_wo[l].astype(BF16),
                    ln_g[l, 1], ln_b[l, 1], batch, seq)
        if l % 2 == 0:
            xc = _ffn(xc, ffn_w13[l // 2].astype(BF16), ffn_w2[l // 2].astype(BF16), ln_g[l, 2], ln_b[l, 2])
        else:
            xc = _moe(xc, moe_router[l // 2], moe_w13[l // 2], moe_w2[l // 2],
                      ln_g[l, 2], ln_b[l, 2])
    return xc.reshape(batch, seq, D_MODEL)
```

```python
import functools
import math

import jax
import jax.numpy as jnp
from jax import lax
from jax.experimental import pallas as pl
from jax.experimental.pallas import tpu as pltpu

F32 = jnp.float32
BF16 = jnp.bfloat16
I32 = jnp.int32

D_MODEL = 1024
DEPTH = 2
EPS = 1e-5
NEG_BIG = -1e30
LANES = 128
HEAD_DIM = 64
N_HEADS = 4
WIDTH = N_HEADS * HEAD_DIM

MLA_Q_LORA = 256
MLA_KV_LORA = 128
MLA_NOPE = 64
MLA_ROPE = 32
ROPE_THETA = 10000.0
MLA_SCALE = (MLA_NOPE + MLA_ROPE) ** -0.5
LOG2E = math.log2(math.e)
QK_SCALE = HEAD_DIM ** -0.5

SB_RUN_FLOOR = -150.0
SWA_WINDOW = 128
REL_BUCKETS = 32
REL_MAX_DIST = 128
HGRN_CHUNK = 64
HGRN_BLOCK = 16
N_EXPERTS = 8
F_DENSE = 2816
F_EXPERT = 3584
ALPHA = (2 * DEPTH) ** 0.25

_IN_SPLITS = (('mla_cq', 256), ('mla_ckv', 128), ('mla_kr', 32), ('swa_q', 256), ('swa_k', 128),
              ('swa_v', 128), ('hgrn', 1024), ('sb_q', 256), ('sb_k', 256), ('sb_v', 256), ('gates', 4096))
_IN_OFF = {}
_o = 0
for _n, _w in _IN_SPLITS:
    _IN_OFF[_n] = _o
    _o += _w

_A_SPLITS = (('cq', 256), ('ckv', 128), ('kra', 128), ('krb', 128), ('swa_q', 256), ('swa_k', 256),
             ('swa_v', 256), ('hgrn', 1024), ('sb_q', 256), ('sb_k', 256), ('sb_v', 256))
_A_OFF = {}
_o = 0
for _n, _w in _A_SPLITS:
    _A_OFF[_n] = (_o, _o + _w)
    _o += _w
A_COLS = _o

TM_A = 512
TQ_ATT = 256
MLA_TQ = 512
MLA_TK = 512
MLA_GROUP = 4
SWA_TQ = 512
HG_ROWS = 256
TM_FFN = 512
TF_FFN = 1408
MOE_TB = 768
MOE_TF = 896
TM_COMB = 256
TM_DISP = 512
VMEM_LIMIT = 56 * 1024 * 1024


def _cparams(sem, **kw):
    return pltpu.CompilerParams(dimension_semantics=sem, vmem_limit_bytes=VMEM_LIMIT, **kw)


def _const_spec(shape):
    nd = len(shape)
    return pl.BlockSpec(shape, lambda *_: (0,) * nd, pipeline_mode=pl.Buffered(1))


def _layernorm(v, g, b):
    mu = jnp.mean(v, axis=-1, keepdims=True)
    vc = v - mu
    var = jnp.mean(vc * vc, axis=-1, keepdims=True)
    return vc * lax.rsqrt(var + EPS) * g + b


def _dot(a, b):
    return jnp.dot(a, b, preferred_element_type=F32)


def _dot_nt(a, b):
    return lax.dot_general(a, b, (((1,), (1,)), ((), ())), preferred_element_type=F32)


def _split3(a):
    hi = a.astype(BF16)
    r = a - hi.astype(F32)
    mid = r.astype(BF16)
    lo = (r - mid.astype(F32)).astype(BF16)
    return hi, mid, lo


def _rope_kernel(pos_ref, freq_ref, c_ref, s_ref):
    lane = lax.broadcasted_iota(I32, pos_ref.shape, 1)
    ang = pos_ref[...] * freq_ref[...]
    rope = (lane >= MLA_NOPE) & (lane < MLA_NOPE + MLA_ROPE)
    first = lane < MLA_NOPE + MLA_ROPE // 2
    c_ref[...] = jnp.where(lane < MLA_NOPE, 1.0, jnp.where(rope, jnp.cos(ang), 0.0))
    sn = jnp.sin(ang)
    s_ref[...] = jnp.where(rope, jnp.where(first, -sn, sn), 0.0)


def _rope_tables(positions):
    n = positions.size
    half = MLA_ROPE // 2
    inv_freq = ROPE_THETA ** (-jnp.arange(half, dtype=F32) / half)
    freq = jnp.zeros((1, LANES), F32).at[0, MLA_NOPE:MLA_NOPE + MLA_ROPE].set(jnp.tile(inv_freq, 2))
    posb = jnp.broadcast_to(positions.reshape(n, 1).astype(F32), (n, LANES))
    tm = 1024
    return pl.pallas_call(
        _rope_kernel,
        grid=(n // tm,),
        in_specs=[pl.BlockSpec((tm, LANES), lambda i: (i, 0)), _const_spec((1, LANES))],
        out_specs=[pl.BlockSpec((tm, LANES), lambda i: (i, 0))] * 2,
        out_shape=[jax.ShapeDtypeStruct((n, LANES), F32)] * 2,
        compiler_params=_cparams(("parallel",)),
        name="rope_tables",
    )(posb, freq)


def _inproj_kernel(x_ref, w_ref, c_ref, s_ref, qn_ref, kvn_ref, wuqa_ref, wuqb_ref, wuk_ref, wuv_ref,
                   mq_ref, mk_ref, mv_ref, swq_ref, swk_ref, swv_ref, hg_ref, sbq_ref, sbk_ref, sbv_ref):
    h = _dot(x_ref[...].astype(BF16), w_ref[...])

    def cols(name):
        lo, hi = _A_OFF[name]
        return h[:, lo:hi]

    c = c_ref[...]
    s = s_ref[...]
    c4 = jnp.concatenate([c] * N_HEADS, axis=1)
    s4 = jnp.concatenate([s] * N_HEADS, axis=1)

    cq = cols('cq')
    cqn = (cq * lax.rsqrt(jnp.mean(cq * cq, axis=-1, keepdims=True) + EPS) * qn_ref[...]).astype(BF16)
    q = _dot(cqn, wuqa_ref[...]) * c4 + _dot(cqn, wuqb_ref[...]) * s4
    mq_ref[...] = (q * (MLA_SCALE * LOG2E)).astype(BF16)

    ckv = cols('ckv')
    ckvn = (ckv * lax.rsqrt(jnp.mean(ckv * ckv, axis=-1, keepdims=True) + EPS) * kvn_ref[...]).astype(BF16)
    krot = cols('kra') * c + cols('krb') * s
    mk_ref[...] = (_dot(ckvn, wuk_ref[...]) + jnp.concatenate([krot] * N_HEADS, axis=1)).astype(BF16)
    mv_ref[...] = _dot(ckvn, wuv_ref[...]).astype(BF16)

    swq_ref[...] = cols('swa_q').astype(BF16)
    swk_ref[...] = cols('swa_k').astype(BF16)
    swv_ref[...] = cols('swa_v').astype(BF16)
    hg_ref[...] = cols('hgrn')
    sbq_ref[...] = cols('sb_q').astype(BF16)
    sbk_ref[...] = cols('sb_k').astype(BF16)
    sbv_ref[...] = cols('sb_v').astype(BF16)


def _inproj_weights(w_in, w_uq, w_ukv):
    def seg(name, width):
        o = _IN_OFF[name]
        return w_in[:, o:o + width]

    kr = seg('mla_kr', MLA_ROPE)
    half = MLA_ROPE // 2
    z64 = jnp.zeros((D_MODEL, MLA_NOPE), F32)
    z32 = jnp.zeros((D_MODEL, LANES - MLA_NOPE - MLA_ROPE), F32)
    kra = jnp.concatenate([z64, kr, z32], axis=1)
    krb = jnp.concatenate([z64, kr[:, half:], kr[:, :half], z32], axis=1)
    swk = seg('swa_k', 128)
    swv = seg('swa_v', 128)
    dup = lambda t: jnp.concatenate([t[:, :64], t[:, :64], t[:, 64:], t[:, 64:]], axis=1)
    w_a = jnp.concatenate([
        seg('mla_cq', 256), seg('mla_ckv', 128), kra, krb,
        seg('swa_q', 256) * QK_SCALE, dup(swk), dup(swv),
        seg('hgrn', 1024), seg('sb_q', 256) * (QK_SCALE * LOG2E), seg('sb_k', 256), seg('sb_v', 256)], axis=1)

    qd = MLA_NOPE + MLA_ROPE
    zq = jnp.zeros((MLA_Q_LORA, LANES - qd), F32)
    zn = jnp.zeros((MLA_Q_LORA, MLA_NOPE), F32)
    qa, qb = [], []
    for hh in range(N_HEADS):
        nope = w_uq[:, hh * qd: hh * qd + MLA_NOPE]
        rope = w_uq[:, hh * qd + MLA_NOPE: (hh + 1) * qd]
        qa += [nope, rope, zq]
        qb += [zn, rope[:, half:], rope[:, :half], zq]
    wuqa = jnp.concatenate(qa, axis=1)
    wuqb = jnp.concatenate(qb, axis=1)
    lane = jnp.arange(N_HEADS * LANES) % LANES
    wuk = jnp.where(lane[None, :] < MLA_NOPE, w_ukv, 0.0)
    wuv = jnp.concatenate([w_ukv[:, hh * LANES + MLA_NOPE:(hh + 1) * LANES] for hh in range(N_HEADS)], axis=1)
    return tuple(t.astype(BF16) for t in (w_a, wuqa, wuqb, wuk, wuv))


def _inproj(x2d, wts, ctab, stab, q_norm, kv_norm):
    n = x2d.shape[0]
    w_a, wuqa, wuqb, wuk, wuv = wts
    tm = TM_A
    row = lambda w: pl.BlockSpec((tm, w), lambda i: (i, 0))
    out_w = (512, 512, 256, 256, 256, 256, 1024, 256, 256, 256)
    out_dt = (BF16, BF16, BF16, BF16, BF16, BF16, F32, BF16, BF16, BF16)
    return pl.pallas_call(
        _inproj_kernel,
        grid=(n // tm,),
        in_specs=[row(D_MODEL), _const_spec(w_a.shape), row(LANES), row(LANES),
                  _const_spec((1, MLA_Q_LORA)), _const_spec((1, MLA_KV_LORA)),
                  _const_spec(wuqa.shape), _const_spec(wuqb.shape), _const_spec(wuk.shape),
                  _const_spec(wuv.shape)],
        out_specs=[row(w) for w in out_w],
        out_shape=[jax.ShapeDtypeStruct((n, w), d) for w, d in zip(out_w, out_dt)],
        compiler_params=_cparams(("parallel",)),
        name="inproj",
    )(x2d, w_a, ctab, stab, q_norm.reshape(1, -1), kv_norm.reshape(1, -1), wuqa, wuqb, wuk, wuv)


def _half_mask(half):
    lane = lax.broadcasted_iota(I32, (1, LANES), 1)
    return (lane < HEAD_DIM) if half == 0 else (lane >= HEAD_DIM)


def _mla_kernel(q_ref, k_ref, v_ref, o_ref):
    tq = q_ref.shape[0]
    tk = MLA_TK
    nsub = tq // tk
    i = pl.program_id(1)
    row = lax.broadcasted_iota(I32, (tq, tk), 0)
    col = lax.broadcasted_iota(I32, (tq, tk), 1)
    ones = jnp.ones((1, LANES), BF16)

    def update(off, carry, heads, mask, width=tk):
        ss = [_dot_nt(q_ref[:, hh * LANES:(hh + 1) * LANES],
                      k_ref[pl.ds(off, width), hh * LANES:(hh + 1) * LANES]) for hh in heads]
        if mask is not None:
            ss = [jnp.where(mask, s, NEG_BIG) for s in ss]
        ms = [jnp.maximum(c[0], jnp.max(s, axis=-1, keepdims=True)) for c, s in zip(carry, ss)]
        pms = [jnp.exp2(s - m).astype(BF16) for s, m in zip(ss, ms)]
        new = []
        for n, hh in enumerate(heads):
            vb = v_ref[pl.ds(off, width), (hh // 2) * LANES:(hh // 2 + 1) * LANES]
            vb = jnp.where(_half_mask(hh % 2), vb, ones)
            m, acc = carry[n]
            new.append((ms[n], jnp.exp2(m - ms[n]) * acc + _dot(pms[n], vb)))
        return tuple(new)

    accs = []
    for g in range(0, N_HEADS, MLA_GROUP):
        heads = tuple(range(g, g + MLA_GROUP))
        init = tuple((jnp.full((tq, 1), NEG_BIG, F32), jnp.zeros((tq, LANES), F32)) for _ in heads)
        nkb = i * nsub
        carry = lax.fori_loop(
            0, nkb // 2,
            lambda j, c, heads=heads: update(pl.multiple_of(j * 2 * tk, 2 * tk), c, heads, None, 2 * tk), init)
        carry = lax.cond(
            nkb % 2 == 1,
            lambda c, heads=heads: update(pl.multiple_of((nkb - 1) * tk, tk), c, heads, None),
            lambda c: c, carry)
        for r in range(nsub):
            carry = update(pl.multiple_of(i * tq + r * tk, tk), carry, heads, col + r * tk <= row)
        accs += [c[1] for c in carry]
    outs = []
    for p in range(N_HEADS // 2):
        a0, a1 = accs[2 * p], accs[2 * p + 1]
        outs.append(jnp.where(_half_mask(0), a0 / a0[:, HEAD_DIM:HEAD_DIM + 1], a1 / a1[:, 0:1]))
    o_ref[...] = jnp.concatenate(outs, axis=1).astype(o_ref.dtype)


def _mla_attention(q, k, v, batch, seq):
    tq = MLA_TQ
    q3, k3, v3 = (t.reshape(batch, seq, t.shape[-1]) for t in (q, k, v))
    out = pl.pallas_call(
        _mla_kernel,
        grid=(batch, seq // tq),
        in_specs=[pl.BlockSpec((None, tq, 512), lambda b, i: (b, i, 0)),
                  pl.BlockSpec((None, seq, 512), lambda b, i: (b, 0, 0)),
                  pl.BlockSpec((None, seq, WIDTH), lambda b, i: (b, 0, 0))],
        out_specs=pl.BlockSpec((None, tq, WIDTH), lambda b, i: (b, i, 0)),
        out_shape=jax.ShapeDtypeStruct((batch, seq, WIDTH), BF16),
        compiler_params=_cparams(("parallel", "arbitrary")),
        name="mla_attention",
    )(q3, k3, v3)
    return out.reshape(batch * seq, WIDTH)


def _sb_kernel(q_ref, k_ref, v_ref, o_ref):
    tq = q_ref.shape[0]
    i = pl.program_id(1)
    row = lax.broadcasted_iota(I32, (tq, tq), 0)
    col = lax.broadcasted_iota(I32, (tq, tq), 1)
    strict = col < row
    later = (row > col).astype(BF16)
    qs = []
    for hh in range(N_HEADS):
        qp = q_ref[:, (hh // 2) * LANES:(hh // 2 + 1) * LANES]
        qs.append(jnp.where(_half_mask(hh % 2), qp, jnp.zeros_like(qp)))

    def block(j, carry, diag):
        off = pl.multiple_of(j * tq, tq)
        runs, accs = carry
        heads = range(N_HEADS)
        zs = [_dot_nt(qs[hh], k_ref[pl.ds(off, tq), (hh // 2) * LANES:(hh // 2 + 1) * LANES]) for hh in heads]
        lsps = [jnp.minimum(z, 0.0) - jnp.log2(1.0 + jnp.exp2(-jnp.abs(z))) for z in zs]
        lsns = [lsp - z for lsp, z in zip(lsps, zs)]
        if diag:
            lsns = [jnp.where(strict, t, 0.0) for t in lsns]
        his = [t.astype(BF16) for t in lsns]
        los = [(t - hi.astype(F32)).astype(BF16) for t, hi in zip(lsns, his)]
        rems = [_dot(hi, later) + _dot(lo, later) for hi, lo in zip(his, los)]
        args = [lsps[hh] + rems[hh] + runs[hh] for hh in heads]
        if diag:
            args = [jnp.where(strict, t, NEG_BIG) for t in args]
        probs = [jnp.exp2(t).astype(BF16) for t in args]
        new_runs = tuple(runs[hh] + rems[hh][:, 0:1] + lsns[hh][:, 0:1] for hh in heads)
        new_accs = list(accs)
        for hh in heads:
            p = hh // 2
            vb = v_ref[pl.ds(off, tq), p * LANES:(p + 1) * LANES]
            vb = jnp.where(_half_mask(hh % 2), vb, jnp.zeros_like(vb))
            new_accs[p] = new_accs[p] + _dot(probs[hh], vb)
        return new_runs, tuple(new_accs)

    init = (tuple(jnp.zeros((tq, 1), F32) for _ in range(N_HEADS)),
            tuple(jnp.zeros((tq, LANES), F32) for _ in range(N_HEADS // 2)))
    def still_active(runs):
        top = functools.reduce(jnp.maximum, runs)
        return (jnp.max(top) > SB_RUN_FLOOR).astype(I32)

    runs, accs = block(i, init, True)

    def cond(c):
        return (c[0] < i) & (c[1] > 0)

    def body(c):
        jj, _, runs, accs = c
        runs, accs = block(i - 1 - jj, (runs, accs), False)
        return jj + 1, still_active(runs), runs, accs

    _, _, _, accs = lax.while_loop(cond, body, (jnp.int32(0), still_active(runs), runs, accs))
    o_ref[...] = jnp.concatenate(accs, axis=1).astype(o_ref.dtype)


def _sb_attention(q, k, v, batch, seq):
    tq = TQ_ATT
    q3, k3, v3 = (t.reshape(batch, seq, WIDTH) for t in (q, k, v))
    out = pl.pallas_call(
        _sb_kernel,
        grid=(batch, seq // tq),
        in_specs=[pl.BlockSpec((None, tq, WIDTH), lambda b, i: (b, i, 0)),
                  pl.BlockSpec((None, seq, WIDTH), lambda b, i: (b, 0, 0)),
                  pl.BlockSpec((None, seq, WIDTH), lambda b, i: (b, 0, 0))],
        out_specs=pl.BlockSpec((None, tq, WIDTH), lambda b, i: (b, i, 0)),
        out_shape=jax.ShapeDtypeStruct((batch, seq, WIDTH), BF16),
        compiler_params=_cparams(("parallel", "arbitrary")),
        name="stick_breaking",
    )(q3, k3, v3)
    return out.reshape(batch * seq, WIDTH)


def _rel_bucket(dist):
    exact = REL_BUCKETS // 2
    n = jnp.maximum(dist, 0)
    nf = jnp.maximum(n, 1).astype(F32)
    large = exact + (jnp.log(nf / exact) / math.log(REL_MAX_DIST / exact) * (REL_BUCKETS - exact)).astype(I32)
    large = jnp.clip(large, 0, REL_BUCKETS - 1)
    return jnp.where(n < exact, n, large)


def _swa_kernel(sink_ref, tab_ref, q_ref, kc_ref, kh_ref, vc_ref, vh_ref, pq_ref, pkc_ref, pkh_ref, o_ref):
    w = SWA_WINDOW
    step = pl.program_id(1)
    row = lax.broadcasted_iota(I32, (w, w), 0)
    col = lax.broadcasted_iota(I32, (w, w), 1)
    valid_c = col <= row
    valid_p = col > row
    tabs = [jnp.broadcast_to(tab_ref[hh:hh + 1, :], (w, LANES)) for hh in range(N_HEADS)]
    ones = jnp.ones((1, LANES), BF16)
    nsub = q_ref.shape[0] // w
    chains = [(r, hh) for r in range(nsub) for hh in range(N_HEADS)]

    def keys(ref, halo_ref, r, hh):
        sl = slice((hh // 2) * LANES, (hh // 2 + 1) * LANES)
        cur = ref[r * w:(r + 1) * w, sl]
        prev = ref[(r - 1) * w:r * w, sl] if r else halo_ref[:, sl]
        return cur, prev

    buckets = []
    for r in range(nsub):
        pq = pq_ref[r * w:(r + 1) * w, :]
        pk_prev = pkc_ref[:, (r - 1) * w:r * w] if r else pkh_ref[...]
        buckets.append((_rel_bucket(pq - pkc_ref[:, r * w:(r + 1) * w]), _rel_bucket(pq - pk_prev)))
    logits = []
    for r, hh in chains:
        qp = q_ref[r * w:(r + 1) * w, (hh // 2) * LANES:(hh // 2 + 1) * LANES]
        qh = jnp.where(_half_mask(hh % 2), qp, jnp.zeros_like(qp))
        kc, kp = keys(kc_ref, kh_ref, r, hh)
        logits.append((_dot_nt(qh, kc), _dot_nt(qh, kp)))
    masked = []
    for (r, hh), (lc, lp) in zip(chains, logits):
        lc = jnp.where(valid_c, lc + jnp.take_along_axis(tabs[hh], buckets[r][0], axis=1), NEG_BIG)
        lp = lp + jnp.take_along_axis(tabs[hh], buckets[r][1], axis=1)
        lp = jnp.where(valid_p if r else valid_p & (step > 0), lp, NEG_BIG)
        masked.append((lc, lp))
    maxes = [jnp.maximum(jnp.maximum(jnp.max(lc, axis=-1, keepdims=True), jnp.max(lp, axis=-1, keepdims=True)),
                         sink_ref[hh]) for (r, hh), (lc, lp) in zip(chains, masked)]
    probs = [(jnp.exp(lc - m).astype(BF16), jnp.exp(lp - m).astype(BF16)) for (lc, lp), m in zip(masked, maxes)]
    outs = {}
    for (r, hh), (ec, ep), m in zip(chains, probs, maxes):
        vc, vp = keys(vc_ref, vh_ref, r, hh)
        mine = _half_mask(hh % 2)
        acc = _dot(ec, jnp.where(mine, vc, ones)) + _dot(ep, jnp.where(mine, vp, ones))
        den = (acc[:, 0:1] if hh % 2 else acc[:, HEAD_DIM:HEAD_DIM + 1]) + jnp.exp(sink_ref[hh] - m)
        outs[(r, hh)] = acc / den
    for r in range(nsub):
        pairs = [jnp.where(_half_mask(0), outs[(r, 2 * p)], outs[(r, 2 * p + 1)]) for p in range(N_HEADS // 2)]
        o_ref[r * w:(r + 1) * w, :] = jnp.concatenate(pairs, axis=1).astype(o_ref.dtype)


def _swa_attention(q, k, v, positions, sinks, rel_table, batch, seq):
    w = SWA_WINDOW
    tq = SWA_TQ
    per = tq // w
    q3, k3, v3 = (t.reshape(batch, seq, WIDTH) for t in (q, k, v))
    pcol = positions.reshape(batch, seq, 1)
    prow = positions.reshape(batch, 1, seq)
    tab = jnp.zeros((N_HEADS, LANES), F32).at[:, :REL_BUCKETS].set(rel_table.astype(F32).T)
    cur = lambda b, n: (b, n, 0)
    halo = lambda b, n: (b, jnp.maximum(n * per - 1, 0), 0)
    out = pl.pallas_call(
        _swa_kernel,
        grid=(batch, seq // tq),
        in_specs=[pl.BlockSpec(memory_space=pltpu.SMEM), _const_spec((N_HEADS, LANES)),
                  pl.BlockSpec((None, tq, WIDTH), cur),
                  pl.BlockSpec((None, tq, WIDTH), cur), pl.BlockSpec((None, w, WIDTH), halo),
                  pl.BlockSpec((None, tq, WIDTH), cur), pl.BlockSpec((None, w, WIDTH), halo),
                  pl.BlockSpec((None, tq, 1), cur),
                  pl.BlockSpec((None, 1, tq), lambda b, n: (b, 0, n)),
                  pl.BlockSpec((None, 1, w), lambda b, n: (b, 0, jnp.maximum(n * per - 1, 0)))],
        out_specs=pl.BlockSpec((None, tq, WIDTH), cur),
        out_shape=jax.ShapeDtypeStruct((batch, seq, WIDTH), BF16),
        compiler_params=_cparams(("parallel", "arbitrary")),
        name="swa_attention",
    )(sinks.astype(F32), tab, q3, k3, k3, v3, v3, pcol, prow, prow)
    return out.reshape(batch * seq, WIDTH)


def _hgrn_kernel(hg_ref, lb_ref, nw_ref, o_ref, state_ref):
    c = HGRN_CHUNK
    blk = HGRN_BLOCK

    @pl.when(pl.program_id(1) == 0)
    def _():
        state_ref[...] = jnp.zeros_like(state_ref)

    r64 = lax.broadcasted_iota(I32, (c, c), 0)
    c64 = lax.broadcasted_iota(I32, (c, c), 1)
    incl = (c64 <= r64).astype(BF16)
    ra = lax.broadcasted_iota(I32, (WIDTH, WIDTH), 0) // HEAD_DIM
    ca = lax.broadcasted_iota(I32, (WIDTH, WIDTH), 1) // HEAD_DIM
    same_head = ra == ca
    seg = same_head.astype(BF16)
    ones_cols = jnp.ones((c, LANES), BF16)
    trow = lax.broadcasted_iota(I32, (blk, WIDTH), 0)
    caps = [jnp.where(trow >= s_i, 0.0, NEG_BIG) for s_i in range(blk)]
    lane_head = lax.broadcasted_iota(I32, (1, WIDTH), 1) // HEAD_DIM
    lb = lb_ref[...]
    nw = nw_ref[...]
    dn0 = (((0,), (0,)), ((), ()))

    for ch in range(hg_ref.shape[0] // c):
        rows = slice(ch * c, (ch + 1) * c)
        qraw = hg_ref[rows, 0:WIDTH]
        fraw = hg_ref[rows, WIDTH:2 * WIDTH]
        v = hg_ref[rows, 2 * WIDTH:3 * WIDTH]
        graw = hg_ref[rows, 3 * WIDTH:4 * WIDTH]
        qf = qraw * jax.nn.sigmoid(qraw)
        forget = lb + (1.0 - lb) * jax.nn.sigmoid(fraw)
        lf = jnp.log(forget)
        kk = 1.0 - forget
        gate = graw * jax.nn.sigmoid(graw)
        vb = v.astype(BF16)

        lf3 = _split3(lf)
        bc = _dot(incl, lf3[0]) + _dot(incl, lf3[1]) + _dot(incl, lf3[2])
        b_last = bc[c - 1:c, :]
        tot_col = sum(lax.dot_general(t, ones_cols, dn0, preferred_element_type=F32) for t in lf3)
        decay_col = jnp.exp(jnp.concatenate([tot_col, tot_col], axis=1))

        state = state_ref[...]
        o_inter = _dot((qf * jnp.exp(bc)).astype(BF16), state.astype(BF16))

        def before(qa, qb, ka, kb):
            ref = bc[kb - 1:kb, :]
            qt = qf[qa:qb] * jnp.exp(bc[qa:qb] - ref)
            kt = (kk[ka:kb] * jnp.exp(ref - bc[ka:kb])).astype(BF16)
            qs = jnp.concatenate([jnp.where(lane_head == hh, qt, 0.0) for hh in range(N_HEADS)], axis=0)
            att = _dot_nt(qs.astype(BF16), kt)
            mix = _dot(att.astype(BF16), vb[ka:kb])
            nq = qb - qa
            return sum(jnp.where(lane_head == hh, mix[hh * nq:(hh + 1) * nq], 0.0) for hh in range(N_HEADS))

        bc2 = bc * LOG2E

        def inside(a):
            b2 = bc2[a:a + blk]
            qb_ = qf[a:a + blk]
            ws = []
            for s_i in range(blk):
                e = jnp.exp2(jnp.minimum(b2 - b2[s_i:s_i + 1, :], caps[s_i]))
                ws.append((qb_ * kk[a + s_i:a + s_i + 1, :] * e).astype(BF16))
            att = _dot(jnp.concatenate(ws, axis=0), seg)
            return sum(att[s_i * blk:(s_i + 1) * blk] * v[a + s_i:a + s_i + 1, :] for s_i in range(blk))

        half = c // 2
        far = before(half, c, 0, half)
        intra = [inside(0),
                 inside(blk) + before(blk, half, 0, blk),
                 inside(half) + far[:blk],
                 inside(half + blk) + far[blk:] + before(half + blk, c, half, half + blk)]
        o = o_inter + jnp.concatenate(intra, axis=0)

        khat = (kk * jnp.exp(b_last - bc)).astype(BF16)
        upd = lax.dot_general(khat, vb, dn0, preferred_element_type=F32)
        state_ref[...] = decay_col * state + jnp.where(same_head, upd, 0.0)

        o2 = _split3(o * o)
        ms = (_dot(o2[0], seg) + _dot(o2[1], seg)) * (1.0 / HEAD_DIM)
        o_ref[rows, :] = (o * lax.rsqrt(ms + EPS) * nw * gate).astype(o_ref.dtype)


def _hgrn(hg, lower_bound, norm_w, batch, seq):
    rows = HG_ROWS
    hg3 = hg.reshape(batch, seq, 4 * WIDTH)
    out = pl.pallas_call(
        _hgrn_kernel,
        grid=(batch, seq // rows),
        in_specs=[pl.BlockSpec((None, rows, 4 * WIDTH), lambda b, i: (b, i, 0)),
                  _const_spec((1, WIDTH)), _const_spec((1, WIDTH))],
        out_specs=pl.BlockSpec((None, rows, WIDTH), lambda b, i: (b, i, 0)),
        out_shape=jax.ShapeDtypeStruct((batch, seq, WIDTH), BF16),
        scratch_shapes=[pltpu.VMEM((WIDTH, WIDTH), F32)],
        compiler_params=_cparams(("parallel", "arbitrary")),
        name="hgrn2",
    )(hg3, lower_bound.reshape(1, WIDTH).astype(F32), norm_w.reshape(1, WIDTH).astype(F32))
    return out.reshape(batch * seq, WIDTH)


def _merge_kernel(x_ref, y0_ref, y1_ref, y2_ref, y3_ref, wg_ref, wb_ref, wo_ref, g_ref, b_ref, o_ref):
    x = x_ref[...]
    xb = x.astype(BF16)
    merged = jnp.zeros(x.shape, F32)
    for nbr, y_ref in enumerate((y0_ref, y1_ref, y2_ref, y3_ref)):
        gate = jax.nn.sigmoid(_dot(xb, wg_ref[:, nbr * D_MODEL:(nbr + 1) * D_MODEL]))
        merged = merged + gate * _dot(y_ref[...], wb_ref[nbr])
    y = _dot(merged.astype(BF16), wo_ref[...])
    o_ref[...] = _layernorm(ALPHA * x + y, g_ref[...], b_ref[...])


def _merge(x2d, ys, wg, wb, wo, g, b):
    n = x2d.shape[0]
    tm = TM_A
    row = lambda w: pl.BlockSpec((tm, w), lambda i: (i, 0))
    return pl.pallas_call(
        _merge_kernel,
        grid=(n // tm,),
        in_specs=[row(D_MODEL)] + [row(WIDTH)] * 4 +
                 [_const_spec(wg.shape), _const_spec(wb.shape), _const_spec(wo.shape),
                  _const_spec((1, D_MODEL)), _const_spec((1, D_MODEL))],
        out_specs=row(D_MODEL),
        out_shape=jax.ShapeDtypeStruct((n, D_MODEL), F32),
        compiler_params=_cparams(("parallel",)),
        name="merge_outproj_ln",
    )(x2d, *ys, wg, wb, wo, g.reshape(1, -1), b.reshape(1, -1))


def _memkv_kernel(m_ref, w_ref, k_ref, v_ref):
    kv = _dot(m_ref[...].astype(BF16), w_ref[...])
    k_ref[...] = kv[:, :WIDTH].astype(BF16)
    v_ref[...] = kv[:, WIDTH:].astype(BF16)


def _memkv(mem, wkv):
    batch, m, _ = mem.shape
    return pl.pallas_call(
        _memkv_kernel,
        grid=(batch,),
        in_specs=[pl.BlockSpec((None, m, D_MODEL), lambda b: (b, 0, 0)), _const_spec(wkv.shape)],
        out_specs=[pl.BlockSpec((None, m, WIDTH), lambda b: (b, 0, 0))] * 2,
        out_shape=[jax.ShapeDtypeStruct((batch, m, WIDTH), BF16)] * 2,
        compiler_params=_cparams(("parallel",)),
        name="mem_kv",
    )(mem, wkv)


def _xattn_kernel(x_ref, wq_ref, k_ref, v_ref, wo_ref, g_ref, b_ref, o_ref):
    x = x_ref[...]
    q = _dot(x.astype(BF16), wq_ref[...]).astype(BF16)
    k = k_ref[...]
    v = v_ref[...]
    lane = lax.broadcasted_iota(I32, (1, WIDTH), 1) // HEAD_DIM
    heads = range(N_HEADS)
    ss = [_dot_nt(jnp.where(lane == hh, q, jnp.zeros_like(q)), k) for hh in heads]
    es = [jnp.exp(s - jnp.max(s, axis=-1, keepdims=True)) for s in ss]
    ps = [(e / jnp.sum(e, axis=-1, keepdims=True)).astype(BF16) for e in es]
    o = jnp.zeros((x.shape[0], WIDTH), F32)
    for hh in heads:
        o = o + jnp.where(lane == hh, _dot(ps[hh], v), 0.0)
    y = _dot(o.astype(BF16), wo_ref[...])
    o_ref[...] = _layernorm(ALPHA * x + y, g_ref[...], b_ref[...])


def _xattn(x2d, wq, k, v, wo, g, b, batch, seq):
    tm = TM_A
    m = k.shape[1]
    x3 = x2d.reshape(batch, seq, D_MODEL)
    row = pl.BlockSpec((None, tm, D_MODEL), lambda bb, i: (bb, i, 0))
    kv_spec = pl.BlockSpec((None, m, WIDTH), lambda bb, i: (bb, 0, 0))
    out = pl.pallas_call(
        _xattn_kernel,
        grid=(batch, seq // tm),
        in_specs=[row, _const_spec(wq.shape), kv_spec, kv_spec, _const_spec(wo.shape),
                  _const_spec((1, D_MODEL)), _const_spec((1, D_MODEL))],
        out_specs=row,
        out_shape=jax.ShapeDtypeStruct((batch, seq, D_MODEL), F32),
        compiler_params=_cparams(("parallel", "parallel")),
        name="mem_xattn_ln",
    )(x3, wq, k, v, wo, g.reshape(1, -1), b.reshape(1, -1))
    return out.reshape(batch * seq, D_MODEL)


def _ffn_kernel(x_ref, w13_ref, w2_ref, g_ref, b_ref, o_ref):
    x = x_ref[...]
    xb = x.astype(BF16)
    y = None
    for h in range(F_DENSE // TF_FFN):
        lo = h * TF_FFN
        a = _dot(xb, w13_ref[:, lo:lo + TF_FFN])
        gate = _dot(xb, w13_ref[:, F_DENSE + lo:F_DENSE + lo + TF_FFN])
        part = _dot((a * jax.nn.sigmoid(a) * gate).astype(BF16), w2_ref[lo:lo + TF_FFN, :])
        y = part if y is None else y + part
    o_ref[...] = _layernorm(ALPHA * x + y, g_ref[...], b_ref[...])


def _ffn(x2d, w13, w2, g, b):
    n = x2d.shape[0]
    tm = TM_FFN
    return pl.pallas_call(
        _ffn_kernel,
        grid=(n // tm,),
        in_specs=[pl.BlockSpec((tm, D_MODEL), lambda i: (i, 0)),
                  _const_spec(w13.shape), _const_spec(w2.shape),
                  _const_spec((1, D_MODEL)), _const_spec((1, D_MODEL))],
        out_specs=pl.BlockSpec((tm, D_MODEL), lambda i: (i, 0)),
        out_shape=jax.ShapeDtypeStruct((n, D_MODEL), F32),
        compiler_params=_cparams(("parallel",)),
        name="ffn_ln",
    )(x2d, w13, w2, g.reshape(1, -1), b.reshape(1, -1))


def _router_kernel(x_ref, r_ref, info_ref, wts_ref, cnt_ref, carry_ref):
    tm = x_ref.shape[0]

    @pl.when(pl.program_id(0) == 0)
    def _():
        carry_ref[...] = jnp.zeros_like(carry_ref)

    logits = jnp.dot(x_ref[...], r_ref[...], precision=lax.Precision.HIGHEST, preferred_element_type=F32)
    lane = lax.broadcasted_iota(I32, (tm, LANES), 1)
    lg = jnp.where(lane < N_EXPERTS, logits, -jnp.inf)
    m1 = jnp.max(lg, axis=-1, keepdims=True)
    i1 = jnp.min(jnp.where(lg == m1, lane, LANES), axis=-1, keepdims=True)
    lg2 = jnp.where(lane == i1, -jnp.inf, lg)
    m2 = jnp.max(lg2, axis=-1, keepdims=True)
    i2 = jnp.min(jnp.where(lg2 == m2, lane, LANES), axis=-1, keepdims=True)
    e = jnp.exp(m2 - m1)
    w1 = 1.0 / (1.0 + e)
    w2 = e / (1.0 + e)
    sel1 = lane == i1
    sel2 = lane == i2
    chosen = jnp.where(sel1 | sel2, 1.0, 0.0)
    row = lax.broadcasted_iota(I32, (tm, tm), 0)
    col = lax.broadcasted_iota(I32, (tm, tm), 1)
    before = (col < row).astype(BF16)
    ranks = _dot(before, chosen.astype(BF16)) + carry_ref[...]
    r1 = jnp.sum(jnp.where(sel1, ranks, 0.0), axis=-1, keepdims=True).astype(I32)
    r2 = jnp.sum(jnp.where(sel2, ranks, 0.0), axis=-1, keepdims=True).astype(I32)
    carry_ref[...] = carry_ref[...] + jnp.sum(chosen, axis=0, keepdims=True)
    info_ref[...] = jnp.where(lane == 0, i1, jnp.where(lane == 1, i2, jnp.where(lane == 2, r1,
                              jnp.where(lane == 3, r2, 0))))
    wts_ref[...] = jnp.where(lane == 0, w1, jnp.where(lane == 1, w2, 0.0))
    cnt_ref[...] = carry_ref[...]


def _router(x2d, router):
    n = x2d.shape[0]
    tm = TM_A
    r_pad = jnp.zeros((D_MODEL, LANES), F32).at[:, :N_EXPERTS].set(router.astype(F32))
    row = pl.BlockSpec((tm, LANES), lambda i: (i, 0))
    return pl.pallas_call(
        _router_kernel,
        grid=(n // tm,),
        in_specs=[pl.BlockSpec((tm, D_MODEL), lambda i: (i, 0)), _const_spec(r_pad.shape)],
        out_specs=[row, row, pl.BlockSpec((1, LANES), lambda i: (0, 0))],
        out_shape=[jax.ShapeDtypeStruct((n, LANES), I32), jax.ShapeDtypeStruct((n, LANES), F32),
                   jax.ShapeDtypeStruct((1, LANES), F32)],
        scratch_shapes=[pltpu.VMEM((1, LANES), F32)],
        compiler_params=_cparams(("arbitrary",)),
        name="moe_router",
    )(x2d, r_pad)


def _dispatch_kernel(pad_ref, dest_ref, x_ref, xb_hbm, stage_ref, sems):
    tm = x_ref.shape[0]
    i = pl.program_id(0)
    last = pl.num_programs(0) - 1
    slot = i % 2

    def wait_step(s):
        for _ in range(2):
            pltpu.make_async_copy(stage_ref.at[s], xb_hbm.at[pl.ds(0, tm), :], sems.at[s]).wait()

    @pl.when(i >= 2)
    def _():
        wait_step(slot)

    stage_ref[slot] = x_ref[...]

    def issue(r, c):
        for k in range(2):
            pltpu.make_async_copy(stage_ref.at[slot, pl.ds(r, 1), :],
                                  xb_hbm.at[pl.ds(dest_ref[0, 2 * r + k], 1), :], sems.at[slot]).start()
        return c
    lax.fori_loop(0, tm, issue, 0, unroll=8)

    @pl.when(i == last)
    def _():
        def fill(e, c):
            def one(s, c2):
                pltpu.make_async_copy(stage_ref.at[slot, pl.ds(0, 1), :], xb_hbm.at[pl.ds(s, 1), :],
                                      sems.at[2]).start()
                return c2

            def done(s, c2):
                pltpu.make_async_copy(stage_ref.at[slot, pl.ds(0, 1), :], xb_hbm.at[pl.ds(0, 1), :],
                                      sems.at[2]).wait()
                return c2
            lax.fori_loop(pad_ref[0, e], pad_ref[1, e], one, 0)
            lax.fori_loop(pad_ref[0, e], pad_ref[1, e], done, 0)
            return c
        lax.fori_loop(0, pad_ref.shape[1], fill, 0)
        wait_step(slot)

        @pl.when(last >= 1)
        def _():
            wait_step(1 - slot)


def _dispatch(x2d, dest, pads, nblk):
    n = x2d.shape[0]
    tm = TM_DISP
    nt = n // tm
    grid_spec = pltpu.PrefetchScalarGridSpec(
        num_scalar_prefetch=1,
        grid=(nt,),
        in_specs=[pl.BlockSpec((None, 1, 2 * tm), lambda i, pads: (i, 0, 0), memory_space=pltpu.SMEM),
                  pl.BlockSpec((tm, D_MODEL), lambda i, pads: (i, 0))],
        out_specs=pl.BlockSpec(memory_space=pl.ANY),
        scratch_shapes=[pltpu.VMEM((2, tm, D_MODEL), F32), pltpu.SemaphoreType.DMA((3,))],
    )
    return pl.pallas_call(
        _dispatch_kernel,
        grid_spec=grid_spec,
        out_shape=jax.ShapeDtypeStruct((nblk * MOE_TB, D_MODEL), F32),
        compiler_params=_cparams(("arbitrary",), disable_bounds_checks=True),
        name="moe_dispatch",
    )(pads, dest.reshape(nt, 1, 2 * tm), x2d)


def _expert_kernel(nused_ref, bexp_ref, x_ref, w1_ref, w3_ref, w2_ref, o_ref, acc_ref):
    f = pl.program_id(1)

    @pl.when(pl.program_id(0) < nused_ref[0])
    def _():
        xb = x_ref[...].astype(BF16)
        a = _dot(xb, w1_ref[...].astype(BF16))
        gate = _dot(xb, w3_ref[...].astype(BF16))
        part = _dot((a * jax.nn.sigmoid(a) * gate).astype(BF16), w2_ref[...].astype(BF16))

        @pl.when(f == 0)
        def _():
            acc_ref[...] = part

        @pl.when(f > 0)
        def _():
            acc_ref[...] += part

        @pl.when(f == pl.num_programs(1) - 1)
        def _():
            o_ref[...] = acc_ref[...]

    @pl.when(pl.program_id(0) >= nused_ref[0])
    def _():
        o_ref[...] = jnp.zeros_like(o_ref)


def _experts(xb, w13, w2, nused, blk_exp, nblk):
    tb, tf = MOE_TB, MOE_TF
    nf = F_EXPERT // tf

    def blk(i, nu):
        return jnp.maximum(jnp.minimum(i, nu[0] - 1), 0)

    def ftile(i, f, nu):
        return jnp.where(i < nu[0], f, nf - 1)

    grid_spec = pltpu.PrefetchScalarGridSpec(
        num_scalar_prefetch=2,
        grid=(nblk, nf),
        in_specs=[pl.BlockSpec((tb, D_MODEL), lambda i, f, nu, be: (blk(i, nu), 0)),
                  pl.BlockSpec((None, D_MODEL, tf), lambda i, f, nu, be: (be[blk(i, nu)], 0, ftile(i, f, nu))),
                  pl.BlockSpec((None, D_MODEL, tf), lambda i, f, nu, be: (be[blk(i, nu)], 0, nf + ftile(i, f, nu))),
                  pl.BlockSpec((None, tf, D_MODEL), lambda i, f, nu, be: (be[blk(i, nu)], ftile(i, f, nu), 0))],
        out_specs=pl.BlockSpec((tb, D_MODEL), lambda i, f, nu, be: (i, 0)),
        scratch_shapes=[pltpu.VMEM((tb, D_MODEL), F32)],
    )
    return pl.pallas_call(
        _expert_kernel,
        grid_spec=grid_spec,
        out_shape=jax.ShapeDtypeStruct((nblk * tb, D_MODEL), F32),
        compiler_params=_cparams(("arbitrary", "arbitrary")),
        name="moe_experts",
    )(nused, blk_exp, xb, w13, w13, w2)


def _combine_kernel(dest_ref, nxt_ref, y_hbm, x_ref, wts_ref, g_ref, b_ref, o_ref, buf_ref, sems):
    tm = x_ref.shape[0]
    i = pl.program_id(0)
    slot = i % 2

    def gather(idx_ref, s):
        def issue(r, c):
            for k in range(2):
                pltpu.make_async_copy(y_hbm.at[pl.ds(idx_ref[0, 2 * r + k], 1), :],
                                      buf_ref.at[s, k, pl.ds(r, 1), :], sems.at[s]).start()
            return c
        lax.fori_loop(0, tm, issue, 0, unroll=8)

    @pl.when(i == 0)
    def _():
        gather(dest_ref, slot)

    @pl.when(i + 1 < pl.num_programs(0))
    def _():
        gather(nxt_ref, 1 - slot)

    for k in range(2):
        pltpu.make_async_copy(y_hbm.at[pl.ds(0, tm), :], buf_ref.at[slot, k], sems.at[slot]).wait()
    wts = wts_ref[...]
    y = wts[:, 0:1] * buf_ref[slot, 0] + wts[:, 1:2] * buf_ref[slot, 1]
    o_ref[...] = _layernorm(ALPHA * x_ref[...] + y, g_ref[...], b_ref[...])


def _combine(yb, dest, x2d, wts, g, b):
    n = x2d.shape[0]
    tm = TM_COMB
    nt = n // tm
    row = lambda w: pl.BlockSpec((tm, w), lambda i: (i, 0))
    dest3 = dest.reshape(nt, 1, 2 * tm)
    return pl.pallas_call(
        _combine_kernel,
        grid=(nt,),
        in_specs=[pl.BlockSpec((None, 1, 2 * tm), lambda i: (i, 0, 0), memory_space=pltpu.SMEM),
                  pl.BlockSpec((None, 1, 2 * tm), lambda i: (jnp.minimum(i + 1, nt - 1), 0, 0),
                               memory_space=pltpu.SMEM),
                  pl.BlockSpec(memory_space=pl.ANY), row(D_MODEL), row(LANES),
                  _const_spec((1, D_MODEL)), _const_spec((1, D_MODEL))],
        out_specs=row(D_MODEL),
        out_shape=jax.ShapeDtypeStruct((n, D_MODEL), F32),
        scratch_shapes=[pltpu.VMEM((2, 2, tm, D_MODEL), F32), pltpu.SemaphoreType.DMA((2,))],
        compiler_params=_cparams(("arbitrary",), disable_bounds_checks=True),
        name="moe_combine_ln",
    )(dest3, dest3, yb, x2d, wts, g.reshape(1, -1), b.reshape(1, -1))


def _moe(x2d, router, w13, w2, g, b):
    n = x2d.shape[0]
    tb = MOE_TB
    info, wts, cnt = _router(x2d, router)
    idx = info[:, 0:2]
    rank = info[:, 2:4]
    counts = cnt[0, :N_EXPERTS].astype(I32)
    padded = (counts + tb - 1) // tb * tb
    pend = jnp.cumsum(padded)
    pstart = pend - padded
    dest = (pstart[idx] + rank).astype(I32).reshape(-1)
    nblk = -(-(2 * n + N_EXPERTS * (tb - 1)) // tb)
    pads = jnp.stack([jnp.append(pstart + counts, pend[-1]), jnp.append(pend, nblk * tb)]).astype(I32)
    nused = (pend[-1] // tb).astype(I32).reshape(1)
    blk_exp = jnp.minimum(jnp.searchsorted(pend, jnp.arange(nblk, dtype=I32) * tb, side='right'),
                          N_EXPERTS - 1).astype(I32)
    xb = _dispatch(x2d, dest, pads, nblk)
    yb = _experts(xb, w13, w2, nused, blk_exp, nblk)
    return _combine(yb, dest, x2d, wts, g, b)


def kernel(x, mem, positions, rel_bias_table, hgrn_lb_logits, w_in, mla_q_norm, mla_w_uq, mla_kv_norm, mla_w_ukv, swa_sinks, hgrn_norm, w_branch, w_out, ln_g, ln_b, xa_wq, xa_wkv, xa_wo, ffn_w13, ffn_w2, moe_router, moe_w13, moe_w2):
    batch, seq, _ = x.shape
    n = batch * seq
    sm = jax.nn.softmax(hgrn_lb_logits.astype(F32), axis=0)
    lower_bounds = jnp.cumsum(sm, axis=0) - sm[0]
    ctab, stab = _rope_tables(positions)
    xc = x.reshape(n, D_MODEL)
    for l in range(DEPTH):
        wts = _inproj_weights(w_in[l], mla_w_uq[l], mla_w_ukv[l])
        mq, mk, mv, swq, swk, swv, hg, sbq, sbk, sbv = _inproj(xc, wts, ctab, stab, mla_q_norm[l], mla_kv_norm[l])
        y_mla = _mla_attention(mq, mk, mv, batch, seq)
        y_swa = _swa_attention(swq, swk, swv, positions, swa_sinks[l], rel_bias_table, batch, seq)
        y_hg = _hgrn(hg, lower_bounds[l], hgrn_norm[l], batch, seq)
        y_sb = _sb_attention(sbq, sbk, sbv, batch, seq)
        go = _IN_OFF['gates']
        xc = _merge(xc, (y_mla, y_swa, y_hg, y_sb), w_in[l][:, go:].astype(BF16), w_branch[l].astype(BF16),
                    w_out[l].astype(BF16), ln_g[l, 0], ln_b[l, 0])
        mk_, mv_ = _memkv(mem, xa_wkv[l].astype(BF16))
        xc = _xattn(xc, (xa_wq[l] * QK_SCALE).astype(BF16), mk_, mv_, xa_wo[l].astype(BF16),
                    ln_g[l, 1], ln_b[l, 1], batch, seq)
        if l % 2 == 0:
            xc = _ffn(xc, ffn_w13[l // 2].astype(BF16), ffn_w2[l // 2].astype(BF16), ln_g[l, 2], ln_b[l, 2])
        else:
            xc = _moe(xc, moe_router[l // 2], moe_w13[l // 2], moe_w2[l // 2],
                      ln_g[l, 2], ln_b[l, 2])
    return xc.reshape(batch, seq, D_MODEL)
```

```python
import functools
import math

import jax
import jax.numpy as jnp
from jax import lax
from jax.experimental import pallas as pl
from jax.experimental.pallas import tpu as pltpu

F32 = jnp.float32
BF16 = jnp.bfloat16
I32 = jnp.int32

D_MODEL = 1024
DEPTH = 2
EPS = 1e-5
NEG_BIG = -1e30
LANES = 128
HEAD_DIM = 64
N_HEADS = 4
WIDTH = N_HEADS * HEAD_DIM

MLA_Q_LORA = 256
MLA_KV_LORA = 128
MLA_NOPE = 64
MLA_ROPE = 32
ROPE_THETA = 10000.0
MLA_SCALE = (MLA_NOPE + MLA_ROPE) ** -0.5
LOG2E = math.log2(math.e)
QK_SCALE = HEAD_DIM ** -0.5

SB_RUN_FLOOR = -150.0
SWA_WINDOW = 128
REL_BUCKETS = 32
REL_MAX_DIST = 128
HGRN_CHUNK = 64
HGRN_BLOCK = 16
N_EXPERTS = 8
F_DENSE = 2816
F_EXPERT = 3584
ALPHA = (2 * DEPTH) ** 0.25

_IN_SPLITS = (('mla_cq', 256), ('mla_ckv', 128), ('mla_kr', 32), ('swa_q', 256), ('swa_k', 128),
              ('swa_v', 128), ('hgrn', 1024), ('sb_q', 256), ('sb_k', 256), ('sb_v', 256), ('gates', 4096))
_IN_OFF = {}
_o = 0
for _n, _w in _IN_SPLITS:
    _IN_OFF[_n] = _o
    _o += _w

_A_SPLITS = (('cq', 256), ('ckv', 128), ('kra', 128), ('krb', 128), ('swa_q', 256), ('swa_k', 256),
             ('swa_v', 256), ('hgrn', 1024), ('sb_q', 256), ('sb_k', 256), ('sb_v', 256))
_A_OFF = {}
_o = 0
for _n, _w in _A_SPLITS:
    _A_OFF[_n] = (_o, _o + _w)
    _o += _w
A_COLS = _o

TM_A = 512
TQ_ATT = 256
MLA_TQ = 512
MLA_TK = 512
MLA_WIDE = 4
MLA_GROUP = 4
SWA_TQ = 512
HG_ROWS = 256
TM_FFN = 512
TF_FFN = 1408
MOE_TB = 512
MOE_TF = 1792
TM_COMB = 256
TM_DISP = 512
VMEM_LIMIT = 56 * 1024 * 1024


def _cparams(sem, **kw):
    return pltpu.CompilerParams(dimension_semantics=sem, vmem_limit_bytes=VMEM_LIMIT, **kw)


def _const_spec(shape):
    nd = len(shape)
    return pl.BlockSpec(shape, lambda *_: (0,) * nd, pipeline_mode=pl.Buffered(1))


def _layernorm(v, g, b):
    mu = jnp.mean(v, axis=-1, keepdims=True)
    vc = v - mu
    var = jnp.mean(vc * vc, axis=-1, keepdims=True)
    return vc * lax.rsqrt(var + EPS) * g + b


def _dot(a, b):
    return jnp.dot(a, b, preferred_element_type=F32)


def _dot_nt(a, b):
    return lax.dot_general(a, b, (((1,), (1,)), ((), ())), preferred_element_type=F32)


def _split3(a):
    hi = a.astype(BF16)
    r = a - hi.astype(F32)
    mid = r.astype(BF16)
    lo = (r - mid.astype(F32)).astype(BF16)
    return hi, mid, lo


def _rope_kernel(pos_ref, freq_ref, c_ref, s_ref):
    lane = lax.broadcasted_iota(I32, pos_ref.shape, 1)
    ang = pos_ref[...] * freq_ref[...]
    rope = (lane >= MLA_NOPE) & (lane < MLA_NOPE + MLA_ROPE)
    first = lane < MLA_NOPE + MLA_ROPE // 2
    c_ref[...] = jnp.where(lane < MLA_NOPE, 1.0, jnp.where(rope, jnp.cos(ang), 0.0))
    sn = jnp.sin(ang)
    s_ref[...] = jnp.where(rope, jnp.where(first, -sn, sn), 0.0)


def _rope_tables(positions):
    n = positions.size
    half = MLA_ROPE // 2
    inv_freq = ROPE_THETA ** (-jnp.arange(half, dtype=F32) / half)
    freq = jnp.zeros((1, LANES), F32).at[0, MLA_NOPE:MLA_NOPE + MLA_ROPE].set(jnp.tile(inv_freq, 2))
    posb = jnp.broadcast_to(positions.reshape(n, 1).astype(F32), (n, LANES))
    tm = 1024
    return pl.pallas_call(
        _rope_kernel,
        grid=(n // tm,),
        in_specs=[pl.BlockSpec((tm, LANES), lambda i: (i, 0)), _const_spec((1, LANES))],
        out_specs=[pl.BlockSpec((tm, LANES), lambda i: (i, 0))] * 2,
        out_shape=[jax.ShapeDtypeStruct((n, LANES), F32)] * 2,
        compiler_params=_cparams(("parallel",)),
        name="rope_tables",
    )(posb, freq)


def _inproj_kernel(x_ref, w_ref, c_ref, s_ref, qn_ref, kvn_ref, wuqa_ref, wuqb_ref, wuk_ref, wuv_ref,
                   mq_ref, mk_ref, mv_ref, swq_ref, swk_ref, swv_ref, hg_ref, sbq_ref, sbk_ref, sbv_ref):
    h = _dot(x_ref[...].astype(BF16), w_ref[...])

    def cols(name):
        lo, hi = _A_OFF[name]
        return h[:, lo:hi]

    c = c_ref[...]
    s = s_ref[...]
    c4 = jnp.concatenate([c] * N_HEADS, axis=1)
    s4 = jnp.concatenate([s] * N_HEADS, axis=1)

    cq = cols('cq')
    cqn = (cq * lax.rsqrt(jnp.mean(cq * cq, axis=-1, keepdims=True) + EPS) * qn_ref[...]).astype(BF16)
    q = _dot(cqn, wuqa_ref[...]) * c4 + _dot(cqn, wuqb_ref[...]) * s4
    mq_ref[...] = (q * (MLA_SCALE * LOG2E)).astype(BF16)

    ckv = cols('ckv')
    ckvn = (ckv * lax.rsqrt(jnp.mean(ckv * ckv, axis=-1, keepdims=True) + EPS) * kvn_ref[...]).astype(BF16)
    krot = cols('kra') * c + cols('krb') * s
    mk_ref[...] = (_dot(ckvn, wuk_ref[...]) + jnp.concatenate([krot] * N_HEADS, axis=1)).astype(BF16)
    mv_ref[...] = _dot(ckvn, wuv_ref[...]).astype(BF16)

    swq_ref[...] = cols('swa_q').astype(BF16)
    swk_ref[...] = cols('swa_k').astype(BF16)
    swv_ref[...] = cols('swa_v').astype(BF16)
    hg_ref[...] = cols('hgrn')
    sbq_ref[...] = cols('sb_q').astype(BF16)
    sbk_ref[...] = cols('sb_k').astype(BF16)
    sbv_ref[...] = cols('sb_v').astype(BF16)


def _inproj_weights(w_in, w_uq, w_ukv):
    def seg(name, width):
        o = _IN_OFF[name]
        return w_in[:, o:o + width]

    kr = seg('mla_kr', MLA_ROPE)
    half = MLA_ROPE // 2
    z64 = jnp.zeros((D_MODEL, MLA_NOPE), F32)
    z32 = jnp.zeros((D_MODEL, LANES - MLA_NOPE - MLA_ROPE), F32)
    kra = jnp.concatenate([z64, kr, z32], axis=1)
    krb = jnp.concatenate([z64, kr[:, half:], kr[:, :half], z32], axis=1)
    swk = seg('swa_k', 128)
    swv = seg('swa_v', 128)
    dup = lambda t: jnp.concatenate([t[:, :64], t[:, :64], t[:, 64:], t[:, 64:]], axis=1)
    w_a = jnp.concatenate([
        seg('mla_cq', 256), seg('mla_ckv', 128), kra, krb,
        seg('swa_q', 256) * QK_SCALE, dup(swk), dup(swv),
        seg('hgrn', 1024), seg('sb_q', 256) * (QK_SCALE * LOG2E), seg('sb_k', 256), seg('sb_v', 256)], axis=1)

    qd = MLA_NOPE + MLA_ROPE
    zq = jnp.zeros((MLA_Q_LORA, LANES - qd), F32)
    zn = jnp.zeros((MLA_Q_LORA, MLA_NOPE), F32)
    qa, qb = [], []
    for hh in range(N_HEADS):
        nope = w_uq[:, hh * qd: hh * qd + MLA_NOPE]
        rope = w_uq[:, hh * qd + MLA_NOPE: (hh + 1) * qd]
        qa += [nope, rope, zq]
        qb += [zn, rope[:, half:], rope[:, :half], zq]
    wuqa = jnp.concatenate(qa, axis=1)
    wuqb = jnp.concatenate(qb, axis=1)
    lane = jnp.arange(N_HEADS * LANES) % LANES
    wuk = jnp.where(lane[None, :] < MLA_NOPE, w_ukv, 0.0)
    wuv = jnp.concatenate([w_ukv[:, hh * LANES + MLA_NOPE:(hh + 1) * LANES] for hh in range(N_HEADS)], axis=1)
    return tuple(t.astype(BF16) for t in (w_a, wuqa, wuqb, wuk, wuv))


def _inproj(x2d, wts, ctab, stab, q_norm, kv_norm):
    n = x2d.shape[0]
    w_a, wuqa, wuqb, wuk, wuv = wts
    tm = TM_A
    row = lambda w: pl.BlockSpec((tm, w), lambda i: (i, 0))
    out_w = (512, 512, 256, 256, 256, 256, 1024, 256, 256, 256)
    out_dt = (BF16, BF16, BF16, BF16, BF16, BF16, F32, BF16, BF16, BF16)
    return pl.pallas_call(
        _inproj_kernel,
        grid=(n // tm,),
        in_specs=[row(D_MODEL), _const_spec(w_a.shape), row(LANES), row(LANES),
                  _const_spec((1, MLA_Q_LORA)), _const_spec((1, MLA_KV_LORA)),
                  _const_spec(wuqa.shape), _const_spec(wuqb.shape), _const_spec(wuk.shape),
                  _const_spec(wuv.shape)],
        out_specs=[row(w) for w in out_w],
        out_shape=[jax.ShapeDtypeStruct((n, w), d) for w, d in zip(out_w, out_dt)],
        compiler_params=_cparams(("parallel",)),
        name="inproj",
    )(x2d, w_a, ctab, stab, q_norm.reshape(1, -1), kv_norm.reshape(1, -1), wuqa, wuqb, wuk, wuv)


def _half_mask(half):
    lane = lax.broadcasted_iota(I32, (1, LANES), 1)
    return (lane < HEAD_DIM) if half == 0 else (lane >= HEAD_DIM)


def _mla_kernel(q_ref, k_ref, v_ref, o_ref):
    tq = q_ref.shape[0]
    tk = MLA_TK
    nsub = tq // tk
    i = pl.program_id(1)
    row = lax.broadcasted_iota(I32, (tq, tk), 0)
    col = lax.broadcasted_iota(I32, (tq, tk), 1)
    ones = jnp.ones((1, LANES), BF16)

    def update(off, carry, heads, mask, width=tk):
        ss = [_dot_nt(q_ref[:, hh * LANES:(hh + 1) * LANES],
                      k_ref[pl.ds(off, width), hh * LANES:(hh + 1) * LANES]) for hh in heads]
        if mask is not None:
            ss = [jnp.where(mask, s, NEG_BIG) for s in ss]
        ms = [jnp.maximum(c[0], jnp.max(s, axis=-1, keepdims=True)) for c, s in zip(carry, ss)]
        pms = [jnp.exp2(s - m).astype(BF16) for s, m in zip(ss, ms)]
        new = []
        for n, hh in enumerate(heads):
            vb = v_ref[pl.ds(off, width), (hh // 2) * LANES:(hh // 2 + 1) * LANES]
            vb = jnp.where(_half_mask(hh % 2), vb, ones)
            m, acc = carry[n]
            new.append((ms[n], jnp.exp2(m - ms[n]) * acc + _dot(pms[n], vb)))
        return tuple(new)

    accs = []
    for g in range(0, N_HEADS, MLA_GROUP):
        heads = tuple(range(g, g + MLA_GROUP))
        init = tuple((jnp.full((tq, 1), NEG_BIG, F32), jnp.zeros((tq, LANES), F32)) for _ in heads)
        nkb = i * nsub
        wide = MLA_WIDE * tk
        carry = lax.fori_loop(
            0, nkb // MLA_WIDE,
            lambda j, c, heads=heads: update(pl.multiple_of(j * wide, wide), c, heads, None, wide), init)
        done = nkb // MLA_WIDE * MLA_WIDE
        rest = nkb - done
        carry = lax.cond(
            rest >= 2,
            lambda c, heads=heads: update(pl.multiple_of(done * tk, 2 * tk), c, heads, None, 2 * tk),
            lambda c: c, carry)
        carry = lax.cond(
            rest % 2 == 1,
            lambda c, heads=heads: update(pl.multiple_of((nkb - 1) * tk, tk), c, heads, None),
            lambda c: c, carry)
        for r in range(nsub):
            carry = update(pl.multiple_of(i * tq + r * tk, tk), carry, heads, col + r * tk <= row)
        accs += [c[1] for c in carry]
    outs = []
    for p in range(N_HEADS // 2):
        a0, a1 = accs[2 * p], accs[2 * p + 1]
        outs.append(jnp.where(_half_mask(0), a0 / a0[:, HEAD_DIM:HEAD_DIM + 1], a1 / a1[:, 0:1]))
    o_ref[...] = jnp.concatenate(outs, axis=1).astype(o_ref.dtype)


def _mla_attention(q, k, v, batch, seq):
    tq = MLA_TQ
    q3, k3, v3 = (t.reshape(batch, seq, t.shape[-1]) for t in (q, k, v))
    out = pl.pallas_call(
        _mla_kernel,
        grid=(batch, seq // tq),
        in_specs=[pl.BlockSpec((None, tq, 512), lambda b, i: (b, i, 0)),
                  pl.BlockSpec((None, seq, 512), lambda b, i: (b, 0, 0), pipeline_mode=pl.Buffered(1)),
                  pl.BlockSpec((None, seq, WIDTH), lambda b, i: (b, 0, 0), pipeline_mode=pl.Buffered(1))],
        out_specs=pl.BlockSpec((None, tq, WIDTH), lambda b, i: (b, i, 0)),
        out_shape=jax.ShapeDtypeStruct((batch, seq, WIDTH), BF16),
        compiler_params=_cparams(("parallel", "arbitrary")),
        name="mla_attention",
    )(q3, k3, v3)
    return out.reshape(batch * seq, WIDTH)


def _sb_kernel(q_ref, k_ref, v_ref, o_ref):
    tq = q_ref.shape[0]
    i = pl.program_id(1)
    row = lax.broadcasted_iota(I32, (tq, tq), 0)
    col = lax.broadcasted_iota(I32, (tq, tq), 1)
    strict = col < row
    later = (row > col).astype(BF16)
    qs = []
    for hh in range(N_HEADS):
        qp = q_ref[:, (hh // 2) * LANES:(hh // 2 + 1) * LANES]
        qs.append(jnp.where(_half_mask(hh % 2), qp, jnp.zeros_like(qp)))

    def block(j, carry, diag):
        off = pl.multiple_of(j * tq, tq)
        runs, accs = carry
        heads = range(N_HEADS)
        zs = [_dot_nt(qs[hh], k_ref[pl.ds(off, tq), (hh // 2) * LANES:(hh // 2 + 1) * LANES]) for hh in heads]
        lsps = [jnp.minimum(z, 0.0) - jnp.log2(1.0 + jnp.exp2(-jnp.abs(z))) for z in zs]
        lsns = [lsp - z for lsp, z in zip(lsps, zs)]
        if diag:
            lsns = [jnp.where(strict, t, 0.0) for t in lsns]
        his = [t.astype(BF16) for t in lsns]
        los = [(t - hi.astype(F32)).astype(BF16) for t, hi in zip(lsns, his)]
        rems = [_dot(hi, later) + _dot(lo, later) for hi, lo in zip(his, los)]
        args = [lsps[hh] + rems[hh] + runs[hh] for hh in heads]
        if diag:
            args = [jnp.where(strict, t, NEG_BIG) for t in args]
        probs = [jnp.exp2(t).astype(BF16) for t in args]
        new_runs = tuple(runs[hh] + rems[hh][:, 0:1] + lsns[hh][:, 0:1] for hh in heads)
        new_accs = list(accs)
        for hh in heads:
            p = hh // 2
            vb = v_ref[pl.ds(off, tq), p * LANES:(p + 1) * LANES]
            vb = jnp.where(_half_mask(hh % 2), vb, jnp.zeros_like(vb))
            new_accs[p] = new_accs[p] + _dot(probs[hh], vb)
        return new_runs, tuple(new_accs)

    init = (tuple(jnp.zeros((tq, 1), F32) for _ in range(N_HEADS)),
            tuple(jnp.zeros((tq, LANES), F32) for _ in range(N_HEADS // 2)))
    def still_active(runs):
        top = functools.reduce(jnp.maximum, runs)
        return (jnp.max(top) > SB_RUN_FLOOR).astype(I32)

    runs, accs = block(i, init, True)

    def cond(c):
        return (c[0] < i) & (c[1] > 0)

    def body(c):
        jj, _, runs, accs = c
        runs, accs = block(i - 1 - jj, (runs, accs), False)
        return jj + 1, still_active(runs), runs, accs

    _, _, _, accs = lax.while_loop(cond, body, (jnp.int32(0), still_active(runs), runs, accs))
    o_ref[...] = jnp.concatenate(accs, axis=1).astype(o_ref.dtype)


def _sb_attention(q, k, v, batch, seq):
    tq = TQ_ATT
    q3, k3, v3 = (t.reshape(batch, seq, WIDTH) for t in (q, k, v))
    out = pl.pallas_call(
        _sb_kernel,
        grid=(batch, seq // tq),
        in_specs=[pl.BlockSpec((None, tq, WIDTH), lambda b, i: (b, i, 0)),
                  pl.BlockSpec((None, seq, WIDTH), lambda b, i: (b, 0, 0)),
                  pl.BlockSpec((None, seq, WIDTH), lambda b, i: (b, 0, 0))],
        out_specs=pl.BlockSpec((None, tq, WIDTH), lambda b, i: (b, i, 0)),
        out_shape=jax.ShapeDtypeStruct((batch, seq, WIDTH), BF16),
        compiler_params=_cparams(("parallel", "arbitrary")),
        name="stick_breaking",
    )(q3, k3, v3)
    return out.reshape(batch * seq, WIDTH)


def _rel_bucket(dist):
    exact = REL_BUCKETS // 2
    n = jnp.maximum(dist, 0)
    nf = jnp.maximum(n, 1).astype(F32)
    large = exact + (jnp.log(nf / exact) / math.log(REL_MAX_DIST / exact) * (REL_BUCKETS - exact)).astype(I32)
    large = jnp.clip(large, 0, REL_BUCKETS - 1)
    return jnp.where(n < exact, n, large)


def _swa_kernel(sink_ref, tab_ref, q_ref, kc_ref, kh_ref, vc_ref, vh_ref, pq_ref, pkc_ref, pkh_ref, o_ref):
    w = SWA_WINDOW
    step = pl.program_id(1)
    row = lax.broadcasted_iota(I32, (w, w), 0)
    col = lax.broadcasted_iota(I32, (w, w), 1)
    valid_c = col <= row
    valid_p = col > row
    tabs = [jnp.broadcast_to(tab_ref[hh:hh + 1, :], (w, LANES)) for hh in range(N_HEADS)]
    ones = jnp.ones((1, LANES), BF16)
    nsub = q_ref.shape[0] // w
    chains = [(r, hh) for r in range(nsub) for hh in range(N_HEADS)]

    def keys(ref, halo_ref, r, hh):
        sl = slice((hh // 2) * LANES, (hh // 2 + 1) * LANES)
        cur = ref[r * w:(r + 1) * w, sl]
        prev = ref[(r - 1) * w:r * w, sl] if r else halo_ref[:, sl]
        return cur, prev

    buckets = []
    for r in range(nsub):
        pq = pq_ref[r * w:(r + 1) * w, :]
        pk_prev = pkc_ref[:, (r - 1) * w:r * w] if r else pkh_ref[...]
        buckets.append((_rel_bucket(pq - pkc_ref[:, r * w:(r + 1) * w]), _rel_bucket(pq - pk_prev)))
    logits = []
    for r, hh in chains:
        qp = q_ref[r * w:(r + 1) * w, (hh // 2) * LANES:(hh // 2 + 1) * LANES]
        qh = jnp.where(_half_mask(hh % 2), qp, jnp.zeros_like(qp))
        kc, kp = keys(kc_ref, kh_ref, r, hh)
        logits.append((_dot_nt(qh, kc), _dot_nt(qh, kp)))
    masked = []
    for (r, hh), (lc, lp) in zip(chains, logits):
        lc = jnp.where(valid_c, lc + jnp.take_along_axis(tabs[hh], buckets[r][0], axis=1), NEG_BIG)
        lp = lp + jnp.take_along_axis(tabs[hh], buckets[r][1], axis=1)
        lp = jnp.where(valid_p if r else valid_p & (step > 0), lp, NEG_BIG)
        masked.append((lc, lp))
    maxes = [jnp.maximum(jnp.maximum(jnp.max(lc, axis=-1, keepdims=True), jnp.max(lp, axis=-1, keepdims=True)),
                         sink_ref[hh]) for (r, hh), (lc, lp) in zip(chains, masked)]
    probs = [(jnp.exp(lc - m).astype(BF16), jnp.exp(lp - m).astype(BF16)) for (lc, lp), m in zip(masked, maxes)]
    outs = {}
    for (r, hh), (ec, ep), m in zip(chains, probs, maxes):
        vc, vp = keys(vc_ref, vh_ref, r, hh)
        mine = _half_mask(hh % 2)
        acc = _dot(ec, jnp.where(mine, vc, ones)) + _dot(ep, jnp.where(mine, vp, ones))
        den = (acc[:, 0:1] if hh % 2 else acc[:, HEAD_DIM:HEAD_DIM + 1]) + jnp.exp(sink_ref[hh] - m)
        outs[(r, hh)] = acc / den
    for r in range(nsub):
        pairs = [jnp.where(_half_mask(0), outs[(r, 2 * p)], outs[(r, 2 * p + 1)]) for p in range(N_HEADS // 2)]
        o_ref[r * w:(r + 1) * w, :] = jnp.concatenate(pairs, axis=1).astype(o_ref.dtype)


def _swa_attention(q, k, v, positions, sinks, rel_table, batch, seq):
    w = SWA_WINDOW
    tq = SWA_TQ
    per = tq // w
    q3, k3, v3 = (t.reshape(batch, seq, WIDTH) for t in (q, k, v))
    pcol = positions.reshape(batch, seq, 1)
    prow = positions.reshape(batch, 1, seq)
    tab = jnp.zeros((N_HEADS, LANES), F32).at[:, :REL_BUCKETS].set(rel_table.astype(F32).T)
    cur = lambda b, n: (b, n, 0)
    halo = lambda b, n: (b, jnp.maximum(n * per - 1, 0), 0)
    out = pl.pallas_call(
        _swa_kernel,
        grid=(batch, seq // tq),
        in_specs=[pl.BlockSpec(memory_space=pltpu.SMEM), _const_spec((N_HEADS, LANES)),
                  pl.BlockSpec((None, tq, WIDTH), cur),
                  pl.BlockSpec((None, tq, WIDTH), cur), pl.BlockSpec((None, w, WIDTH), halo),
                  pl.BlockSpec((None, tq, WIDTH), cur), pl.BlockSpec((None, w, WIDTH), halo),
                  pl.BlockSpec((None, tq, 1), cur),
                  pl.BlockSpec((None, 1, tq), lambda b, n: (b, 0, n)),
                  pl.BlockSpec((None, 1, w), lambda b, n: (b, 0, jnp.maximum(n * per - 1, 0)))],
        out_specs=pl.BlockSpec((None, tq, WIDTH), cur),
        out_shape=jax.ShapeDtypeStruct((batch, seq, WIDTH), BF16),
        compiler_params=_cparams(("parallel", "arbitrary")),
        name="swa_attention",
    )(sinks.astype(F32), tab, q3, k3, k3, v3, v3, pcol, prow, prow)
    return out.reshape(batch * seq, WIDTH)


def _hgrn_kernel(hg_ref, lb_ref, nw_ref, o_ref, state_ref):
    c = HGRN_CHUNK
    blk = HGRN_BLOCK

    @pl.when(pl.program_id(1) == 0)
    def _():
        state_ref[...] = jnp.zeros_like(state_ref)

    r64 = lax.broadcasted_iota(I32, (c, c), 0)
    c64 = lax.broadcasted_iota(I32, (c, c), 1)
    incl = (c64 <= r64).astype(BF16)
    ra = lax.broadcasted_iota(I32, (WIDTH, WIDTH), 0) // HEAD_DIM
    ca = lax.broadcasted_iota(I32, (WIDTH, WIDTH), 1) // HEAD_DIM
    same_head = ra == ca
    seg = same_head.astype(BF16)
    ones_cols = jnp.ones((c, LANES), BF16)
    trow = lax.broadcasted_iota(I32, (blk, WIDTH), 0)
    caps = [jnp.where(trow >= s_i, 0.0, NEG_BIG) for s_i in range(blk)]
    lane_head = lax.broadcasted_iota(I32, (1, WIDTH), 1) // HEAD_DIM
    lb = lb_ref[...]
    nw = nw_ref[...]
    dn0 = (((0,), (0,)), ((), ()))

    for ch in range(hg_ref.shape[0] // c):
        rows = slice(ch * c, (ch + 1) * c)
        qraw = hg_ref[rows, 0:WIDTH]
        fraw = hg_ref[rows, WIDTH:2 * WIDTH]
        v = hg_ref[rows, 2 * WIDTH:3 * WIDTH]
        graw = hg_ref[rows, 3 * WIDTH:4 * WIDTH]
        qf = qraw * jax.nn.sigmoid(qraw)
        forget = lb + (1.0 - lb) * jax.nn.sigmoid(fraw)
        lf = jnp.log(forget)
        kk = 1.0 - forget
        gate = graw * jax.nn.sigmoid(graw)
        vb = v.astype(BF16)

        lf3 = _split3(lf)
        bc = _dot(incl, lf3[0]) + _dot(incl, lf3[1]) + _dot(incl, lf3[2])
        b_last = bc[c - 1:c, :]
        tot_col = sum(lax.dot_general(t, ones_cols, dn0, preferred_element_type=F32) for t in lf3)
        decay_col = jnp.exp(jnp.concatenate([tot_col, tot_col], axis=1))

        state = state_ref[...]
        o_inter = _dot((qf * jnp.exp(bc)).astype(BF16), state.astype(BF16))

        def before(qa, qb, ka, kb):
            ref = bc[kb - 1:kb, :]
            qt = qf[qa:qb] * jnp.exp(bc[qa:qb] - ref)
            kt = (kk[ka:kb] * jnp.exp(ref - bc[ka:kb])).astype(BF16)
            qs = jnp.concatenate([jnp.where(lane_head == hh, qt, 0.0) for hh in range(N_HEADS)], axis=0)
            att = _dot_nt(qs.astype(BF16), kt)
            mix = _dot(att.astype(BF16), vb[ka:kb])
            nq = qb - qa
            return sum(jnp.where(lane_head == hh, mix[hh * nq:(hh + 1) * nq], 0.0) for hh in range(N_HEADS))

        bc2 = bc * LOG2E

        def inside(a):
            b2 = bc2[a:a + blk]
            qb_ = qf[a:a + blk]
            ws = []
            for s_i in range(blk):
                e = jnp.exp2(jnp.minimum(b2 - b2[s_i:s_i + 1, :], caps[s_i]))
                ws.append((qb_ * kk[a + s_i:a + s_i + 1, :] * e).astype(BF16))
            att = _dot(jnp.concatenate(ws, axis=0), seg)
            return sum(att[s_i * blk:(s_i + 1) * blk] * v[a + s_i:a + s_i + 1, :] for s_i in range(blk))

        half = c // 2
        far = before(half, c, 0, half)
        intra = [inside(0),
                 inside(blk) + before(blk, half, 0, blk),
                 inside(half) + far[:blk],
                 inside(half + blk) + far[blk:] + before(half + blk, c, half, half + blk)]
        o = o_inter + jnp.concatenate(intra, axis=0)

        khat = (kk * jnp.exp(b_last - bc)).astype(BF16)
        upd = lax.dot_general(khat, vb, dn0, preferred_element_type=F32)
        state_ref[...] = decay_col * state + jnp.where(same_head, upd, 0.0)

        o2 = _split3(o * o)
        ms = (_dot(o2[0], seg) + _dot(o2[1], seg)) * (1.0 / HEAD_DIM)
        o_ref[rows, :] = (o * lax.rsqrt(ms + EPS) * nw * gate).astype(o_ref.dtype)


def _hgrn(hg, lower_bound, norm_w, batch, seq):
    rows = HG_ROWS
    hg3 = hg.reshape(batch, seq, 4 * WIDTH)
    out = pl.pallas_call(
        _hgrn_kernel,
        grid=(batch, seq // rows),
        in_specs=[pl.BlockSpec((None, rows, 4 * WIDTH), lambda b, i: (b, i, 0)),
                  _const_spec((1, WIDTH)), _const_spec((1, WIDTH))],
        out_specs=pl.BlockSpec((None, rows, WIDTH), lambda b, i: (b, i, 0)),
        out_shape=jax.ShapeDtypeStruct((batch, seq, WIDTH), BF16),
        scratch_shapes=[pltpu.VMEM((WIDTH, WIDTH), F32)],
        compiler_params=_cparams(("parallel", "arbitrary")),
        name="hgrn2",
    )(hg3, lower_bound.reshape(1, WIDTH).astype(F32), norm_w.reshape(1, WIDTH).astype(F32))
    return out.reshape(batch * seq, WIDTH)


def _merge_kernel(x_ref, y0_ref, y1_ref, y2_ref, y3_ref, wg_ref, wb_ref, wo_ref, g_ref, b_ref, o_ref):
    x = x_ref[...]
    xb = x.astype(BF16)
    merged = jnp.zeros(x.shape, F32)
    for nbr, y_ref in enumerate((y0_ref, y1_ref, y2_ref, y3_ref)):
        gate = jax.nn.sigmoid(_dot(xb, wg_ref[:, nbr * D_MODEL:(nbr + 1) * D_MODEL]))
        merged = merged + gate * _dot(y_ref[...], wb_ref[nbr])
    y = _dot(merged.astype(BF16), wo_ref[...])
    o_ref[...] = _layernorm(ALPHA * x + y, g_ref[...], b_ref[...])


def _merge(x2d, ys, wg, wb, wo, g, b):
    n = x2d.shape[0]
    tm = TM_A
    row = lambda w: pl.BlockSpec((tm, w), lambda i: (i, 0))
    return pl.pallas_call(
        _merge_kernel,
        grid=(n // tm,),
        in_specs=[row(D_MODEL)] + [row(WIDTH)] * 4 +
                 [_const_spec(wg.shape), _const_spec(wb.shape), _const_spec(wo.shape),
                  _const_spec((1, D_MODEL)), _const_spec((1, D_MODEL))],
        out_specs=row(D_MODEL),
        out_shape=jax.ShapeDtypeStruct((n, D_MODEL), F32),
        compiler_params=_cparams(("parallel",)),
        name="merge_outproj_ln",
    )(x2d, *ys, wg, wb, wo, g.reshape(1, -1), b.reshape(1, -1))


def _memkv_kernel(m_ref, w_ref, k_ref, v_ref):
    kv = _dot(m_ref[...].astype(BF16), w_ref[...])
    k_ref[...] = kv[:, :WIDTH].astype(BF16)
    v_ref[...] = kv[:, WIDTH:].astype(BF16)


def _memkv(mem, wkv):
    batch, m, _ = mem.shape
    return pl.pallas_call(
        _memkv_kernel,
        grid=(batch,),
        in_specs=[pl.BlockSpec((None, m, D_MODEL), lambda b: (b, 0, 0)), _const_spec(wkv.shape)],
        out_specs=[pl.BlockSpec((None, m, WIDTH), lambda b: (b, 0, 0))] * 2,
        out_shape=[jax.ShapeDtypeStruct((batch, m, WIDTH), BF16)] * 2,
        compiler_params=_cparams(("parallel",)),
        name="mem_kv",
    )(mem, wkv)


def _xattn_kernel(x_ref, wq_ref, k_ref, v_ref, wo_ref, g_ref, b_ref, o_ref):
    x = x_ref[...]
    q = _dot(x.astype(BF16), wq_ref[...]).astype(BF16)
    k = k_ref[...]
    v = v_ref[...]
    lane = lax.broadcasted_iota(I32, (1, WIDTH), 1) // HEAD_DIM
    heads = range(N_HEADS)
    ss = [_dot_nt(jnp.where(lane == hh, q, jnp.zeros_like(q)), k) for hh in heads]
    es = [jnp.exp(s - jnp.max(s, axis=-1, keepdims=True)) for s in ss]
    ps = [(e / jnp.sum(e, axis=-1, keepdims=True)).astype(BF16) for e in es]
    o = jnp.zeros((x.shape[0], WIDTH), F32)
    for hh in heads:
        o = o + jnp.where(lane == hh, _dot(ps[hh], v), 0.0)
    y = _dot(o.astype(BF16), wo_ref[...])
    o_ref[...] = _layernorm(ALPHA * x + y, g_ref[...], b_ref[...])


def _xattn(x2d, wq, k, v, wo, g, b, batch, seq):
    tm = TM_A
    m = k.shape[1]
    x3 = x2d.reshape(batch, seq, D_MODEL)
    row = pl.BlockSpec((None, tm, D_MODEL), lambda bb, i: (bb, i, 0))
    kv_spec = pl.BlockSpec((None, m, WIDTH), lambda bb, i: (bb, 0, 0))
    out = pl.pallas_call(
        _xattn_kernel,
        grid=(batch, seq // tm),
        in_specs=[row, _const_spec(wq.shape), kv_spec, kv_spec, _const_spec(wo.shape),
                  _const_spec((1, D_MODEL)), _const_spec((1, D_MODEL))],
        out_specs=row,
        out_shape=jax.ShapeDtypeStruct((batch, seq, D_MODEL), F32),
        compiler_params=_cparams(("parallel", "parallel")),
        name="mem_xattn_ln",
    )(x3, wq, k, v, wo, g.reshape(1, -1), b.reshape(1, -1))
    return out.reshape(batch * seq, D_MODEL)


def _ffn_kernel(x_ref, w13_ref, w2_ref, g_ref, b_ref, o_ref):
    x = x_ref[...]
    xb = x.astype(BF16)
    y = None
    for h in range(F_DENSE // TF_FFN):
        lo = h * TF_FFN
        a = _dot(xb, w13_ref[:, lo:lo + TF_FFN])
        gate = _dot(xb, w13_ref[:, F_DENSE + lo:F_DENSE + lo + TF_FFN])
        part = _dot((a * jax.nn.sigmoid(a) * gate).astype(BF16), w2_ref[lo:lo + TF_FFN, :])
        y = part if y is None else y + part
    o_ref[...] = _layernorm(ALPHA * x + y, g_ref[...], b_ref[...])


def _ffn(x2d, w13, w2, g, b):
    n = x2d.shape[0]
    tm = TM_FFN
    return pl.pallas_call(
        _ffn_kernel,
        grid=(n // tm,),
        in_specs=[pl.BlockSpec((tm, D_MODEL), lambda i: (i, 0)),
                  _const_spec(w13.shape), _const_spec(w2.shape),
                  _const_spec((1, D_MODEL)), _const_spec((1, D_MODEL))],
        out_specs=pl.BlockSpec((tm, D_MODEL), lambda i: (i, 0)),
        out_shape=jax.ShapeDtypeStruct((n, D_MODEL), F32),
        compiler_params=_cparams(("parallel",)),
        name="ffn_ln",
    )(x2d, w13, w2, g.reshape(1, -1), b.reshape(1, -1))


def _router_kernel(x_ref, r_ref, info_ref, wts_ref, cnt_ref, carry_ref):
    tm = x_ref.shape[0]

    @pl.when(pl.program_id(0) == 0)
    def _():
        carry_ref[...] = jnp.zeros_like(carry_ref)

    logits = jnp.dot(x_ref[...], r_ref[...], precision=lax.Precision.HIGHEST, preferred_element_type=F32)
    lane = lax.broadcasted_iota(I32, (tm, LANES), 1)
    lg = jnp.where(lane < N_EXPERTS, logits, -jnp.inf)
    m1 = jnp.max(lg, axis=-1, keepdims=True)
    i1 = jnp.min(jnp.where(lg == m1, lane, LANES), axis=-1, keepdims=True)
    lg2 = jnp.where(lane == i1, -jnp.inf, lg)
    m2 = jnp.max(lg2, axis=-1, keepdims=True)
    i2 = jnp.min(jnp.where(lg2 == m2, lane, LANES), axis=-1, keepdims=True)
    e = jnp.exp(m2 - m1)
    w1 = 1.0 / (1.0 + e)
    w2 = e / (1.0 + e)
    sel1 = lane == i1
    sel2 = lane == i2
    chosen = jnp.where(sel1 | sel2, 1.0, 0.0)
    row = lax.broadcasted_iota(I32, (tm, tm), 0)
    col = lax.broadcasted_iota(I32, (tm, tm), 1)
    before = (col < row).astype(BF16)
    ranks = _dot(before, chosen.astype(BF16)) + carry_ref[...]
    r1 = jnp.sum(jnp.where(sel1, ranks, 0.0), axis=-1, keepdims=True).astype(I32)
    r2 = jnp.sum(jnp.where(sel2, ranks, 0.0), axis=-1, keepdims=True).astype(I32)
    carry_ref[...] = carry_ref[...] + jnp.sum(chosen, axis=0, keepdims=True)
    info_ref[...] = jnp.where(lane == 0, i1, jnp.where(lane == 1, i2, jnp.where(lane == 2, r1,
                              jnp.where(lane == 3, r2, 0))))
    wts_ref[...] = jnp.where(lane == 0, w1, jnp.where(lane == 1, w2, 0.0))
    cnt_ref[...] = carry_ref[...]


def _router(x2d, router):
    n = x2d.shape[0]
    tm = TM_A
    r_pad = jnp.zeros((D_MODEL, LANES), F32).at[:, :N_EXPERTS].set(router.astype(F32))
    row = pl.BlockSpec((tm, LANES), lambda i: (i, 0))
    return pl.pallas_call(
        _router_kernel,
        grid=(n // tm,),
        in_specs=[pl.BlockSpec((tm, D_MODEL), lambda i: (i, 0)), _const_spec(r_pad.shape)],
        out_specs=[row, row, pl.BlockSpec((1, LANES), lambda i: (0, 0))],
        out_shape=[jax.ShapeDtypeStruct((n, LANES), I32), jax.ShapeDtypeStruct((n, LANES), F32),
                   jax.ShapeDtypeStruct((1, LANES), F32)],
        scratch_shapes=[pltpu.VMEM((1, LANES), F32)],
        compiler_params=_cparams(("arbitrary",)),
        name="moe_router",
    )(x2d, r_pad)


def _dispatch_kernel(pad_ref, dest_ref, x_ref, xb_hbm, stage_ref, sems):
    tm = x_ref.shape[0]
    i = pl.program_id(0)
    last = pl.num_programs(0) - 1
    slot = i % 2

    def wait_step(s):
        for _ in range(2):
            pltpu.make_async_copy(stage_ref.at[s], xb_hbm.at[pl.ds(0, tm), :], sems.at[s]).wait()

    @pl.when(i >= 2)
    def _():
        wait_step(slot)

    stage_ref[slot] = x_ref[...]

    def issue(r, c):
        for k in range(2):
            pltpu.make_async_copy(stage_ref.at[slot, pl.ds(r, 1), :],
                                  xb_hbm.at[pl.ds(dest_ref[0, 2 * r + k], 1), :], sems.at[slot]).start()
        return c
    lax.fori_loop(0, tm, issue, 0, unroll=8)

    @pl.when(i == last)
    def _():
        def fill(e, c):
            def one(s, c2):
                pltpu.make_async_copy(stage_ref.at[slot, pl.ds(0, 1), :], xb_hbm.at[pl.ds(s, 1), :],
                                      sems.at[2]).start()
                return c2

            def done(s, c2):
                pltpu.make_async_copy(stage_ref.at[slot, pl.ds(0, 1), :], xb_hbm.at[pl.ds(0, 1), :],
                                      sems.at[2]).wait()
                return c2
            lax.fori_loop(pad_ref[0, e], pad_ref[1, e], one, 0)
            lax.fori_loop(pad_ref[0, e], pad_ref[1, e], done, 0)
            return c
        lax.fori_loop(0, pad_ref.shape[1], fill, 0)
        wait_step(slot)

        @pl.when(last >= 1)
        def _():
            wait_step(1 - slot)


def _dispatch(x2d, dest, pads, nblk):
    n = x2d.shape[0]
    tm = TM_DISP
    nt = n // tm
    grid_spec = pltpu.PrefetchScalarGridSpec(
        num_scalar_prefetch=1,
        grid=(nt,),
        in_specs=[pl.BlockSpec((None, 1, 2 * tm), lambda i, pads: (i, 0, 0), memory_space=pltpu.SMEM),
                  pl.BlockSpec((tm, D_MODEL), lambda i, pads: (i, 0))],
        out_specs=pl.BlockSpec(memory_space=pl.ANY),
        scratch_shapes=[pltpu.VMEM((2, tm, D_MODEL), F32), pltpu.SemaphoreType.DMA((3,))],
    )
    return pl.pallas_call(
        _dispatch_kernel,
        grid_spec=grid_spec,
        out_shape=jax.ShapeDtypeStruct((nblk * MOE_TB, D_MODEL), F32),
        compiler_params=_cparams(("arbitrary",), disable_bounds_checks=True),
        name="moe_dispatch",
    )(pads, dest.reshape(nt, 1, 2 * tm), x2d)


def _expert_kernel(nused_ref, bexp_ref, x_ref, w1_ref, w3_ref, w2_ref, o_ref, acc_ref):
    f = pl.program_id(1)

    @pl.when(pl.program_id(0) < nused_ref[0])
    def _():
        xb = x_ref[...].astype(BF16)
        a = _dot(xb, w1_ref[...])
        gate = _dot(xb, w3_ref[...])
        part = _dot((a * jax.nn.sigmoid(a) * gate).astype(BF16), w2_ref[...])

        @pl.when(f == 0)
        def _():
            acc_ref[...] = part

        @pl.when(f > 0)
        def _():
            acc_ref[...] += part

        @pl.when(f == pl.num_programs(1) - 1)
        def _():
            o_ref[...] = acc_ref[...]

    @pl.when(pl.program_id(0) >= nused_ref[0])
    def _():
        o_ref[...] = jnp.zeros_like(o_ref)


def _experts(xb, w13, w2, nused, blk_exp, nblk):
    tb, tf = MOE_TB, MOE_TF
    nf = F_EXPERT // tf

    def blk(i, nu):
        return jnp.maximum(jnp.minimum(i, nu[0] - 1), 0)

    def ftile(i, f, nu):
        return jnp.where(i < nu[0], f, nf - 1)

    grid_spec = pltpu.PrefetchScalarGridSpec(
        num_scalar_prefetch=2,
        grid=(nblk, nf),
        in_specs=[pl.BlockSpec((tb, D_MODEL), lambda i, f, nu, be: (blk(i, nu), 0)),
                  pl.BlockSpec((None, D_MODEL, tf), lambda i, f, nu, be: (be[blk(i, nu)], 0, ftile(i, f, nu))),
                  pl.BlockSpec((None, D_MODEL, tf), lambda i, f, nu, be: (be[blk(i, nu)], 0, nf + ftile(i, f, nu))),
                  pl.BlockSpec((None, tf, D_MODEL), lambda i, f, nu, be: (be[blk(i, nu)], ftile(i, f, nu), 0))],
        out_specs=pl.BlockSpec((tb, D_MODEL), lambda i, f, nu, be: (i, 0)),
        scratch_shapes=[pltpu.VMEM((tb, D_MODEL), F32)],
    )
    return pl.pallas_call(
        _expert_kernel,
        grid_spec=grid_spec,
        out_shape=jax.ShapeDtypeStruct((nblk * tb, D_MODEL), F32),
        compiler_params=_cparams(("arbitrary", "arbitrary")),
        name="moe_experts",
    )(nused, blk_exp, xb, w13, w13, w2)


def _combine_kernel(dest_ref, nxt_ref, y_hbm, x_ref, wts_ref, g_ref, b_ref, o_ref, buf_ref, sems):
    tm = x_ref.shape[0]
    i = pl.program_id(0)
    slot = i % 2

    def gather(idx_ref, s):
        def issue(r, c):
            for k in range(2):
                pltpu.make_async_copy(y_hbm.at[pl.ds(idx_ref[0, 2 * r + k], 1), :],
                                      buf_ref.at[s, k, pl.ds(r, 1), :], sems.at[s]).start()
            return c
        lax.fori_loop(0, tm, issue, 0, unroll=8)

    @pl.when(i == 0)
    def _():
        gather(dest_ref, slot)

    @pl.when(i + 1 < pl.num_programs(0))
    def _():
        gather(nxt_ref, 1 - slot)

    for k in range(2):
        pltpu.make_async_copy(y_hbm.at[pl.ds(0, tm), :], buf_ref.at[slot, k], sems.at[slot]).wait()
    wts = wts_ref[...]
    y = wts[:, 0:1] * buf_ref[slot, 0] + wts[:, 1:2] * buf_ref[slot, 1]
    o_ref[...] = _layernorm(ALPHA * x_ref[...] + y, g_ref[...], b_ref[...])


def _combine(yb, dest, x2d, wts, g, b):
    n = x2d.shape[0]
    tm = TM_COMB
    nt = n // tm
    row = lambda w: pl.BlockSpec((tm, w), lambda i: (i, 0))
    dest3 = dest.reshape(nt, 1, 2 * tm)
    return pl.pallas_call(
        _combine_kernel,
        grid=(nt,),
        in_specs=[pl.BlockSpec((None, 1, 2 * tm), lambda i: (i, 0, 0), memory_space=pltpu.SMEM),
                  pl.BlockSpec((None, 1, 2 * tm), lambda i: (jnp.minimum(i + 1, nt - 1), 0, 0),
                               memory_space=pltpu.SMEM),
                  pl.BlockSpec(memory_space=pl.ANY), row(D_MODEL), row(LANES),
                  _const_spec((1, D_MODEL)), _const_spec((1, D_MODEL))],
        out_specs=row(D_MODEL),
        out_shape=jax.ShapeDtypeStruct((n, D_MODEL), F32),
        scratch_shapes=[pltpu.VMEM((2, 2, tm, D_MODEL), F32), pltpu.SemaphoreType.DMA((2,))],
        compiler_params=_cparams(("arbitrary",), disable_bounds_checks=True),
        name="moe_combine_ln",
    )(dest3, dest3, yb, x2d, wts, g.reshape(1, -1), b.reshape(1, -1))


def _moe(x2d, router, w13, w2, g, b):
    n = x2d.shape[0]
    tb = MOE_TB
    info, wts, cnt = _router(x2d, router)
    idx = info[:, 0:2]
    rank = info[:, 2:4]
    counts = cnt[0, :N_EXPERTS].astype(I32)
    padded = (counts + tb - 1) // tb * tb
    pend = jnp.cumsum(padded)
    pstart = pend - padded
    dest = (pstart[idx] + rank).astype(I32).reshape(-1)
    nblk = -(-(2 * n + N_EXPERTS * (tb - 1)) // tb)
    pads = jnp.stack([jnp.append(pstart + counts, pend[-1]), jnp.append(pend, nblk * tb)]).astype(I32)
    nused = (pend[-1] // tb).astype(I32).reshape(1)
    first_row = jnp.arange(nblk, dtype=I32) * tb
    blk_exp = jnp.minimum(jnp.sum(pend[None, :] <= first_row[:, None], axis=1), N_EXPERTS - 1).astype(I32)
    xb = _dispatch(x2d, dest, pads, nblk)
    yb = _experts(xb, w13, w2, nused, blk_exp, nblk)
    return _combine(yb, dest, x2d, wts, g, b)


def kernel(x, mem, positions, rel_bias_table, hgrn_lb_logits, w_in, mla_q_norm, mla_w_uq, mla_kv_norm, mla_w_ukv, swa_sinks, hgrn_norm, w_branch, w_out, ln_g, ln_b, xa_wq, xa_wkv, xa_wo, ffn_w13, ffn_w2, moe_router, moe_w13, moe_w2):
    batch, seq, _ = x.shape
    n = batch * seq
    sm = jax.nn.softmax(hgrn_lb_logits.astype(F32), axis=0)
    lower_bounds = jnp.cumsum(sm, axis=0) - sm[0]
    ctab, stab = _rope_tables(positions)
    xc = x.reshape(n, D_MODEL)
    for l in range(DEPTH):
        wts = _inproj_weights(w_in[l], mla_w_uq[l], mla_w_ukv[l])
        mq, mk, mv, swq, swk, swv, hg, sbq, sbk, sbv = _inproj(xc, wts, ctab, stab, mla_q_norm[l], mla_kv_norm[l])
        y_mla = _mla_attention(mq, mk, mv, batch, seq)
        y_swa = _swa_attention(swq, swk, swv, positions, swa_sinks[l], rel_bias_table, batch, seq)
        y_hg = _hgrn(hg, lower_bounds[l], hgrn_norm[l], batch, seq)
        y_sb = _sb_attention(sbq, sbk, sbv, batch, seq)
        go = _IN_OFF['gates']
        xc = _merge(xc, (y_mla, y_swa, y_hg, y_sb), w_in[l][:, go:].astype(BF16), w_branch[l].astype(BF16),
                    w_out[l].astype(BF16), ln_g[l, 0], ln_b[l, 0])
        mk_, mv_ = _memkv(mem, xa_wkv[l].astype(BF16))
        xc = _xattn(xc, (xa_wq[l] * QK_SCALE).astype(BF16), mk_, mv_, xa_wo[l].astype(BF16),
                    ln_g[l, 1], ln_b[l, 1], batch, seq)
        if l % 2 == 0:
            xc = _ffn(xc, ffn_w13[l // 2].astype(BF16), ffn_w2[l // 2].astype(BF16), ln_g[l, 2], ln_b[l, 2])
        else:
            xc = _moe(xc, moe_router[l // 2], moe_w13[l // 2].astype(BF16), moe_w2[l // 2].astype(BF16),
                      ln_g[l, 2], ln_b[l, 2])
    return xc.reshape(batch, seq, D_MODEL)
```

```python
import functools
import math

import jax
import jax.numpy as jnp
from jax import lax
from jax.experimental import pallas as pl
from jax.experimental.pallas import tpu as pltpu

F32 = jnp.float32
BF16 = jnp.bfloat16
I32 = jnp.int32

D_MODEL = 1024
DEPTH = 2
EPS = 1e-5
NEG_BIG = -1e30
LANES = 128
HEAD_DIM = 64
N_HEADS = 4
WIDTH = N_HEADS * HEAD_DIM

MLA_Q_LORA = 256
MLA_KV_LORA = 128
MLA_NOPE = 64
MLA_ROPE = 32
ROPE_THETA = 10000.0
MLA_SCALE = (MLA_NOPE + MLA_ROPE) ** -0.5
LOG2E = math.log2(math.e)
QK_SCALE = HEAD_DIM ** -0.5

SB_RUN_FLOOR = -150.0
SWA_WINDOW = 128
REL_BUCKETS = 32
REL_MAX_DIST = 128
HGRN_CHUNK = 64
HGRN_BLOCK = 16
N_EXPERTS = 8
F_DENSE = 2816
F_EXPERT = 3584
ALPHA = (2 * DEPTH) ** 0.25

_IN_SPLITS = (('mla_cq', 256), ('mla_ckv', 128), ('mla_kr', 32), ('swa_q', 256), ('swa_k', 128),
              ('swa_v', 128), ('hgrn', 1024), ('sb_q', 256), ('sb_k', 256), ('sb_v', 256), ('gates', 4096))
_IN_OFF = {}
_o = 0
for _n, _w in _IN_SPLITS:
    _IN_OFF[_n] = _o
    _o += _w

_A_SPLITS = (('cq', 256), ('ckv', 128), ('kra', 128), ('krb', 128), ('swa_q', 256), ('swa_k', 256),
             ('swa_v', 256), ('hgrn', 1024), ('sb_q', 256), ('sb_k', 256), ('sb_v', 256))
_A_OFF = {}
_o = 0
for _n, _w in _A_SPLITS:
    _A_OFF[_n] = (_o, _o + _w)
    _o += _w
A_COLS = _o

TM_A = 512
TQ_ATT = 256
MLA_TQ = 512
MLA_TK = 512
MLA_WIDE = 4
MLA_GROUP = 4
SWA_TQ = 512
HG_ROWS = 256
TM_FFN = 512
TF_FFN = 1408
MOE_TB = 512
MOE_TF = 1792
TM_COMB = 256
TM_DISP = 512
VMEM_LIMIT = 56 * 1024 * 1024


def _cparams(sem, **kw):
    return pltpu.CompilerParams(dimension_semantics=sem, vmem_limit_bytes=VMEM_LIMIT, **kw)


def _const_spec(shape):
    nd = len(shape)
    return pl.BlockSpec(shape, lambda *_: (0,) * nd, pipeline_mode=pl.Buffered(1))


def _layernorm(v, g, b):
    mu = jnp.mean(v, axis=-1, keepdims=True)
    vc = v - mu
    var = jnp.mean(vc * vc, axis=-1, keepdims=True)
    return vc * lax.rsqrt(var + EPS) * g + b


def _dot(a, b):
    return jnp.dot(a, b, preferred_element_type=F32)


def _dot_nt(a, b):
    return lax.dot_general(a, b, (((1,), (1,)), ((), ())), preferred_element_type=F32)


def _split3(a):
    hi = a.astype(BF16)
    r = a - hi.astype(F32)
    mid = r.astype(BF16)
    lo = (r - mid.astype(F32)).astype(BF16)
    return hi, mid, lo


def _rope_kernel(pos_ref, freq_ref, c_ref, s_ref):
    lane = lax.broadcasted_iota(I32, pos_ref.shape, 1)
    ang = pos_ref[...] * freq_ref[...]
    rope = (lane >= MLA_NOPE) & (lane < MLA_NOPE + MLA_ROPE)
    first = lane < MLA_NOPE + MLA_ROPE // 2
    c_ref[...] = jnp.where(lane < MLA_NOPE, 1.0, jnp.where(rope, jnp.cos(ang), 0.0))
    sn = jnp.sin(ang)
    s_ref[...] = jnp.where(rope, jnp.where(first, -sn, sn), 0.0)


def _rope_tables(positions):
    n = positions.size
    half = MLA_ROPE // 2
    inv_freq = ROPE_THETA ** (-jnp.arange(half, dtype=F32) / half)
    freq = jnp.zeros((1, LANES), F32).at[0, MLA_NOPE:MLA_NOPE + MLA_ROPE].set(jnp.tile(inv_freq, 2))
    posb = jnp.broadcast_to(positions.reshape(n, 1).astype(F32), (n, LANES))
    tm = 1024
    return pl.pallas_call(
        _rope_kernel,
        grid=(n // tm,),
        in_specs=[pl.BlockSpec((tm, LANES), lambda i: (i, 0)), _const_spec((1, LANES))],
        out_specs=[pl.BlockSpec((tm, LANES), lambda i: (i, 0))] * 2,
        out_shape=[jax.ShapeDtypeStruct((n, LANES), F32)] * 2,
        compiler_params=_cparams(("parallel",)),
        name="rope_tables",
    )(posb, freq)


def _inproj_kernel(x_ref, w_ref, c_ref, s_ref, qn_ref, kvn_ref, wuqa_ref, wuqb_ref, wuk_ref, wuv_ref,
                   mq_ref, mk_ref, mv_ref, swq_ref, swk_ref, swv_ref, hg_ref, sbq_ref, sbk_ref, sbv_ref):
    h = _dot(x_ref[...].astype(BF16), w_ref[...])

    def cols(name):
        lo, hi = _A_OFF[name]
        return h[:, lo:hi]

    c = c_ref[...]
    s = s_ref[...]
    c4 = jnp.concatenate([c] * N_HEADS, axis=1)
    s4 = jnp.concatenate([s] * N_HEADS, axis=1)

    cq = cols('cq')
    cqn = (cq * lax.rsqrt(jnp.mean(cq * cq, axis=-1, keepdims=True) + EPS) * qn_ref[...]).astype(BF16)
    q = _dot(cqn, wuqa_ref[...]) * c4 + _dot(cqn, wuqb_ref[...]) * s4
    mq_ref[...] = (q * (MLA_SCALE * LOG2E)).astype(BF16)

    ckv = cols('ckv')
    ckvn = (ckv * lax.rsqrt(jnp.mean(ckv * ckv, axis=-1, keepdims=True) + EPS) * kvn_ref[...]).astype(BF16)
    krot = cols('kra') * c + cols('krb') * s
    mk_ref[...] = (_dot(ckvn, wuk_ref[...]) + jnp.concatenate([krot] * N_HEADS, axis=1)).astype(BF16)
    mv_ref[...] = _dot(ckvn, wuv_ref[...]).astype(BF16)

    swq_ref[...] = cols('swa_q').astype(BF16)
    swk_ref[...] = cols('swa_k').astype(BF16)
    swv_ref[...] = cols('swa_v').astype(BF16)
    hg_ref[...] = cols('hgrn')
    sbq_ref[...] = cols('sb_q').astype(BF16)
    sbk_ref[...] = cols('sb_k').astype(BF16)
    sbv_ref[...] = cols('sb_v').astype(BF16)


def _inproj_weights(w_in, w_uq, w_ukv):
    def seg(name, width):
        o = _IN_OFF[name]
        return w_in[:, o:o + width]

    kr = seg('mla_kr', MLA_ROPE)
    half = MLA_ROPE // 2
    z64 = jnp.zeros((D_MODEL, MLA_NOPE), F32)
    z32 = jnp.zeros((D_MODEL, LANES - MLA_NOPE - MLA_ROPE), F32)
    kra = jnp.concatenate([z64, kr, z32], axis=1)
    krb = jnp.concatenate([z64, kr[:, half:], kr[:, :half], z32], axis=1)
    swk = seg('swa_k', 128)
    swv = seg('swa_v', 128)
    dup = lambda t: jnp.concatenate([t[:, :64], t[:, :64], t[:, 64:], t[:, 64:]], axis=1)
    w_a = jnp.concatenate([
        seg('mla_cq', 256), seg('mla_ckv', 128), kra, krb,
        seg('swa_q', 256) * QK_SCALE, dup(swk), dup(swv),
        seg('hgrn', 1024), seg('sb_q', 256) * (QK_SCALE * LOG2E), seg('sb_k', 256), seg('sb_v', 256)], axis=1)

    qd = MLA_NOPE + MLA_ROPE
    zq = jnp.zeros((MLA_Q_LORA, LANES - qd), F32)
    zn = jnp.zeros((MLA_Q_LORA, MLA_NOPE), F32)
    qa, qb = [], []
    for hh in range(N_HEADS):
        nope = w_uq[:, hh * qd: hh * qd + MLA_NOPE]
        rope = w_uq[:, hh * qd + MLA_NOPE: (hh + 1) * qd]
        qa += [nope, rope, zq]
        qb += [zn, rope[:, half:], rope[:, :half], zq]
    wuqa = jnp.concatenate(qa, axis=1)
    wuqb = jnp.concatenate(qb, axis=1)
    lane = jnp.arange(N_HEADS * LANES) % LANES
    wuk = jnp.where(lane[None, :] < MLA_NOPE, w_ukv, 0.0)
    wuv = jnp.concatenate([w_ukv[:, hh * LANES + MLA_NOPE:(hh + 1) * LANES] for hh in range(N_HEADS)], axis=1)
    return tuple(t.astype(BF16) for t in (w_a, wuqa, wuqb, wuk, wuv))


def _inproj(x2d, wts, ctab, stab, q_norm, kv_norm):
    n = x2d.shape[0]
    w_a, wuqa, wuqb, wuk, wuv = wts
    tm = TM_A
    row = lambda w: pl.BlockSpec((tm, w), lambda i: (i, 0))
    out_w = (512, 512, 256, 256, 256, 256, 1024, 256, 256, 256)
    out_dt = (BF16, BF16, BF16, BF16, BF16, BF16, F32, BF16, BF16, BF16)
    return pl.pallas_call(
        _inproj_kernel,
        grid=(n // tm,),
        in_specs=[row(D_MODEL), _const_spec(w_a.shape), row(LANES), row(LANES),
                  _const_spec((1, MLA_Q_LORA)), _const_spec((1, MLA_KV_LORA)),
                  _const_spec(wuqa.shape), _const_spec(wuqb.shape), _const_spec(wuk.shape),
                  _const_spec(wuv.shape)],
        out_specs=[row(w) for w in out_w],
        out_shape=[jax.ShapeDtypeStruct((n, w), d) for w, d in zip(out_w, out_dt)],
        compiler_params=_cparams(("parallel",)),
        name="inproj",
    )(x2d, w_a, ctab, stab, q_norm.reshape(1, -1), kv_norm.reshape(1, -1), wuqa, wuqb, wuk, wuv)


def _half_mask(half):
    lane = lax.broadcasted_iota(I32, (1, LANES), 1)
    return (lane < HEAD_DIM) if half == 0 else (lane >= HEAD_DIM)


def _mla_kernel(q_ref, k_ref, v_ref, o_ref):
    tq = q_ref.shape[0]
    tk = MLA_TK
    nsub = tq // tk
    i = pl.program_id(1)
    row = lax.broadcasted_iota(I32, (tq, tk), 0)
    col = lax.broadcasted_iota(I32, (tq, tk), 1)
    ones = jnp.ones((1, LANES), BF16)

    def update(off, carry, heads, mask, width=tk):
        ss = [_dot_nt(q_ref[:, hh * LANES:(hh + 1) * LANES],
                      k_ref[pl.ds(off, width), hh * LANES:(hh + 1) * LANES]) for hh in heads]
        if mask is not None:
            ss = [jnp.where(mask, s, NEG_BIG) for s in ss]
        ms = [jnp.maximum(c[0], jnp.max(s, axis=-1, keepdims=True)) for c, s in zip(carry, ss)]
        pms = [jnp.exp2(s - m).astype(BF16) for s, m in zip(ss, ms)]
        new = []
        for n, hh in enumerate(heads):
            vb = v_ref[pl.ds(off, width), (hh // 2) * LANES:(hh // 2 + 1) * LANES]
            vb = jnp.where(_half_mask(hh % 2), vb, ones)
            m, acc = carry[n]
            new.append((ms[n], jnp.exp2(m - ms[n]) * acc + _dot(pms[n], vb)))
        return tuple(new)

    accs = []
    for g in range(0, N_HEADS, MLA_GROUP):
        heads = tuple(range(g, g + MLA_GROUP))
        init = tuple((jnp.full((tq, 1), NEG_BIG, F32), jnp.zeros((tq, LANES), F32)) for _ in heads)
        nkb = i * nsub
        wide = MLA_WIDE * tk
        carry = lax.fori_loop(
            0, nkb // MLA_WIDE,
            lambda j, c, heads=heads: update(pl.multiple_of(j * wide, wide), c, heads, None, wide), init)
        done = nkb // MLA_WIDE * MLA_WIDE
        rest = nkb - done
        carry = lax.cond(
            rest >= 2,
            lambda c, heads=heads: update(pl.multiple_of(done * tk, 2 * tk), c, heads, None, 2 * tk),
            lambda c: c, carry)
        carry = lax.cond(
            rest % 2 == 1,
            lambda c, heads=heads: update(pl.multiple_of((nkb - 1) * tk, tk), c, heads, None),
            lambda c: c, carry)
        for r in range(nsub):
            carry = update(pl.multiple_of(i * tq + r * tk, tk), carry, heads, col + r * tk <= row)
        accs += [c[1] for c in carry]
    outs = []
    for p in range(N_HEADS // 2):
        a0, a1 = accs[2 * p], accs[2 * p + 1]
        outs.append(jnp.where(_half_mask(0), a0 / a0[:, HEAD_DIM:HEAD_DIM + 1], a1 / a1[:, 0:1]))
    o_ref[...] = jnp.concatenate(outs, axis=1).astype(o_ref.dtype)


def _mla_attention(q, k, v, batch, seq):
    tq = MLA_TQ
    q3, k3, v3 = (t.reshape(batch, seq, t.shape[-1]) for t in (q, k, v))
    out = pl.pallas_call(
        _mla_kernel,
        grid=(batch, seq // tq),
        in_specs=[pl.BlockSpec((None, tq, 512), lambda b, i: (b, i, 0)),
                  pl.BlockSpec((None, seq, 512), lambda b, i: (b, 0, 0), pipeline_mode=pl.Buffered(1)),
                  pl.BlockSpec((None, seq, WIDTH), lambda b, i: (b, 0, 0), pipeline_mode=pl.Buffered(1))],
        out_specs=pl.BlockSpec((None, tq, WIDTH), lambda b, i: (b, i, 0)),
        out_shape=jax.ShapeDtypeStruct((batch, seq, WIDTH), BF16),
        compiler_params=_cparams(("parallel", "arbitrary")),
        name="mla_attention",
    )(q3, k3, v3)
    return out.reshape(batch * seq, WIDTH)


def _sb_kernel(q_ref, k_ref, v_ref, o_ref):
    tq = q_ref.shape[0]
    i = pl.program_id(1)
    row = lax.broadcasted_iota(I32, (tq, tq), 0)
    col = lax.broadcasted_iota(I32, (tq, tq), 1)
    strict = col < row
    later = (row > col).astype(BF16)
    qs = []
    for hh in range(N_HEADS):
        qp = q_ref[:, (hh // 2) * LANES:(hh // 2 + 1) * LANES]
        qs.append(jnp.where(_half_mask(hh % 2), qp, jnp.zeros_like(qp)))

    def block(j, carry, diag):
        off = pl.multiple_of(j * tq, tq)
        runs, accs = carry
        heads = range(N_HEADS)
        zs = [_dot_nt(qs[hh], k_ref[pl.ds(off, tq), (hh // 2) * LANES:(hh // 2 + 1) * LANES]) for hh in heads]
        lsps = [jnp.minimum(z, 0.0) - jnp.log2(1.0 + jnp.exp2(-jnp.abs(z))) for z in zs]
        lsns = [lsp - z for lsp, z in zip(lsps, zs)]
        if diag:
            lsns = [jnp.where(strict, t, 0.0) for t in lsns]
        his = [t.astype(BF16) for t in lsns]
        los = [(t - hi.astype(F32)).astype(BF16) for t, hi in zip(lsns, his)]
        rems = [_dot(hi, later) + _dot(lo, later) for hi, lo in zip(his, los)]
        args = [lsps[hh] + rems[hh] + runs[hh] for hh in heads]
        if diag:
            args = [jnp.where(strict, t, NEG_BIG) for t in args]
        probs = [jnp.exp2(t).astype(BF16) for t in args]
        new_runs = tuple(runs[hh] + rems[hh][:, 0:1] + lsns[hh][:, 0:1] for hh in heads)
        new_accs = list(accs)
        for hh in heads:
            p = hh // 2
            vb = v_ref[pl.ds(off, tq), p * LANES:(p + 1) * LANES]
            vb = jnp.where(_half_mask(hh % 2), vb, jnp.zeros_like(vb))
            new_accs[p] = new_accs[p] + _dot(probs[hh], vb)
        return new_runs, tuple(new_accs)

    init = (tuple(jnp.zeros((tq, 1), F32) for _ in range(N_HEADS)),
            tuple(jnp.zeros((tq, LANES), F32) for _ in range(N_HEADS // 2)))
    def still_active(runs):
        top = functools.reduce(jnp.maximum, runs)
        return (jnp.max(top) > SB_RUN_FLOOR).astype(I32)

    runs, accs = block(i, init, True)

    def cond(c):
        return (c[0] < i) & (c[1] > 0)

    def body(c):
        jj, _, runs, accs = c
        runs, accs = block(i - 1 - jj, (runs, accs), False)
        return jj + 1, still_active(runs), runs, accs

    _, _, _, accs = lax.while_loop(cond, body, (jnp.int32(0), still_active(runs), runs, accs))
    o_ref[...] = jnp.concatenate(accs, axis=1).astype(o_ref.dtype)


def _sb_attention(q, k, v, batch, seq):
    tq = TQ_ATT
    q3, k3, v3 = (t.reshape(batch, seq, WIDTH) for t in (q, k, v))
    out = pl.pallas_call(
        _sb_kernel,
        grid=(batch, seq // tq),
        in_specs=[pl.BlockSpec((None, tq, WIDTH), lambda b, i: (b, i, 0)),
                  pl.BlockSpec((None, seq, WIDTH), lambda b, i: (b, 0, 0)),
                  pl.BlockSpec((None, seq, WIDTH), lambda b, i: (b, 0, 0))],
        out_specs=pl.BlockSpec((None, tq, WIDTH), lambda b, i: (b, i, 0)),
        out_shape=jax.ShapeDtypeStruct((batch, seq, WIDTH), BF16),
        compiler_params=_cparams(("parallel", "arbitrary")),
        name="stick_breaking",
    )(q3, k3, v3)
    return out.reshape(batch * seq, WIDTH)


def _rel_bucket(dist):
    exact = REL_BUCKETS // 2
    n = jnp.maximum(dist, 0)
    nf = jnp.maximum(n, 1).astype(F32)
    large = exact + (jnp.log(nf / exact) / math.log(REL_MAX_DIST / exact) * (REL_BUCKETS - exact)).astype(I32)
    large = jnp.clip(large, 0, REL_BUCKETS - 1)
    return jnp.where(n < exact, n, large)


def _swa_kernel(sink_ref, tab_ref, q_ref, kc_ref, kh_ref, vc_ref, vh_ref, pq_ref, pkc_ref, pkh_ref, o_ref):
    w = SWA_WINDOW
    step = pl.program_id(1)
    row = lax.broadcasted_iota(I32, (w, w), 0)
    col = lax.broadcasted_iota(I32, (w, w), 1)
    valid_c = col <= row
    valid_p = col > row
    tabs = [jnp.broadcast_to(tab_ref[hh:hh + 1, :], (w, LANES)) for hh in range(N_HEADS)]
    ones = jnp.ones((1, LANES), BF16)
    nsub = q_ref.shape[0] // w
    chains = [(r, hh) for r in range(nsub) for hh in range(N_HEADS)]

    def keys(ref, halo_ref, r, hh):
        sl = slice((hh // 2) * LANES, (hh // 2 + 1) * LANES)
        cur = ref[r * w:(r + 1) * w, sl]
        prev = ref[(r - 1) * w:r * w, sl] if r else halo_ref[:, sl]
        return cur, prev

    buckets = []
    for r in range(nsub):
        pq = pq_ref[r * w:(r + 1) * w, :]
        pk_prev = pkc_ref[:, (r - 1) * w:r * w] if r else pkh_ref[...]
        buckets.append((_rel_bucket(pq - pkc_ref[:, r * w:(r + 1) * w]), _rel_bucket(pq - pk_prev)))
    logits = []
    for r, hh in chains:
        qp = q_ref[r * w:(r + 1) * w, (hh // 2) * LANES:(hh // 2 + 1) * LANES]
        qh = jnp.where(_half_mask(hh % 2), qp, jnp.zeros_like(qp))
        kc, kp = keys(kc_ref, kh_ref, r, hh)
        logits.append((_dot_nt(qh, kc), _dot_nt(qh, kp)))
    masked = []
    for (r, hh), (lc, lp) in zip(chains, logits):
        lc = jnp.where(valid_c, lc + jnp.take_along_axis(tabs[hh], buckets[r][0], axis=1), NEG_BIG)
        lp = lp + jnp.take_along_axis(tabs[hh], buckets[r][1], axis=1)
        lp = jnp.where(valid_p if r else valid_p & (step > 0), lp, NEG_BIG)
        masked.append((lc, lp))
    maxes = [jnp.maximum(jnp.maximum(jnp.max(lc, axis=-1, keepdims=True), jnp.max(lp, axis=-1, keepdims=True)),
                         sink_ref[hh]) for (r, hh), (lc, lp) in zip(chains, masked)]
    probs = [(jnp.exp(lc - m).astype(BF16), jnp.exp(lp - m).astype(BF16)) for (lc, lp), m in zip(masked, maxes)]
    outs = {}
    for (r, hh), (ec, ep), m in zip(chains, probs, maxes):
        vc, vp = keys(vc_ref, vh_ref, r, hh)
        mine = _half_mask(hh % 2)
        acc = _dot(ec, jnp.where(mine, vc, ones)) + _dot(ep, jnp.where(mine, vp, ones))
        den = (acc[:, 0:1] if hh % 2 else acc[:, HEAD_DIM:HEAD_DIM + 1]) + jnp.exp(sink_ref[hh] - m)
        outs[(r, hh)] = acc / den
    for r in range(nsub):
        pairs = [jnp.where(_half_mask(0), outs[(r, 2 * p)], outs[(r, 2 * p + 1)]) for p in range(N_HEADS // 2)]
        o_ref[r * w:(r + 1) * w, :] = jnp.concatenate(pairs, axis=1).astype(o_ref.dtype)


def _swa_attention(q, k, v, positions, sinks, rel_table, batch, seq):
    w = SWA_WINDOW
    tq = SWA_TQ
    per = tq // w
    q3, k3, v3 = (t.reshape(batch, seq, WIDTH) for t in (q, k, v))
    pcol = positions.reshape(batch, seq, 1)
    prow = positions.reshape(batch, 1, seq)
    tab = jnp.zeros((N_HEADS, LANES), F32).at[:, :REL_BUCKETS].set(rel_table.astype(F32).T)
    cur = lambda b, n: (b, n, 0)
    halo = lambda b, n: (b, jnp.maximum(n * per - 1, 0), 0)
    out = pl.pallas_call(
        _swa_kernel,
        grid=(batch, seq // tq),
        in_specs=[pl.BlockSpec(memory_space=pltpu.SMEM), _const_spec((N_HEADS, LANES)),
                  pl.BlockSpec((None, tq, WIDTH), cur),
                  pl.BlockSpec((None, tq, WIDTH), cur), pl.BlockSpec((None, w, WIDTH), halo),
                  pl.BlockSpec((None, tq, WIDTH), cur), pl.BlockSpec((None, w, WIDTH), halo),
                  pl.BlockSpec((None, tq, 1), cur),
                  pl.BlockSpec((None, 1, tq), lambda b, n: (b, 0, n)),
                  pl.BlockSpec((None, 1, w), lambda b, n: (b, 0, jnp.maximum(n * per - 1, 0)))],
        out_specs=pl.BlockSpec((None, tq, WIDTH), cur),
        out_shape=jax.ShapeDtypeStruct((batch, seq, WIDTH), BF16),
        compiler_params=_cparams(("parallel", "arbitrary")),
        name="swa_attention",
    )(sinks.astype(F32), tab, q3, k3, k3, v3, v3, pcol, prow, prow)
    return out.reshape(batch * seq, WIDTH)


def _hgrn_kernel(hg_ref, lb_ref, nw_ref, o_ref, state_ref):
    c = HGRN_CHUNK
    blk = HGRN_BLOCK

    @pl.when(pl.program_id(1) == 0)
    def _():
        state_ref[...] = jnp.zeros_like(state_ref)

    r64 = lax.broadcasted_iota(I32, (c, c), 0)
    c64 = lax.broadcasted_iota(I32, (c, c), 1)
    incl = (c64 <= r64).astype(BF16)
    ra = lax.broadcasted_iota(I32, (WIDTH, WIDTH), 0) // HEAD_DIM
    ca = lax.broadcasted_iota(I32, (WIDTH, WIDTH), 1) // HEAD_DIM
    same_head = ra == ca
    seg = same_head.astype(BF16)
    ones_cols = jnp.ones((c, LANES), BF16)
    trow = lax.broadcasted_iota(I32, (blk, WIDTH), 0)
    caps = [jnp.where(trow >= s_i, 0.0, NEG_BIG) for s_i in range(blk)]
    lane_head = lax.broadcasted_iota(I32, (1, WIDTH), 1) // HEAD_DIM
    lb = lb_ref[...]
    nw = nw_ref[...]
    dn0 = (((0,), (0,)), ((), ()))

    for ch in range(hg_ref.shape[0] // c):
        rows = slice(ch * c, (ch + 1) * c)
        qraw = hg_ref[rows, 0:WIDTH]
        fraw = hg_ref[rows, WIDTH:2 * WIDTH]
        v = hg_ref[rows, 2 * WIDTH:3 * WIDTH]
        graw = hg_ref[rows, 3 * WIDTH:4 * WIDTH]
        qf = qraw * jax.nn.sigmoid(qraw)
        forget = lb + (1.0 - lb) * jax.nn.sigmoid(fraw)
        lf = jnp.log(forget)
        kk = 1.0 - forget
        gate = graw * jax.nn.sigmoid(graw)
        vb = v.astype(BF16)

        lf3 = _split3(lf)
        bc = _dot(incl, lf3[0]) + _dot(incl, lf3[1]) + _dot(incl, lf3[2])
        b_last = bc[c - 1:c, :]
        tot_col = sum(lax.dot_general(t, ones_cols, dn0, preferred_element_type=F32) for t in lf3)
        decay_col = jnp.exp(jnp.concatenate([tot_col, tot_col], axis=1))

        state = state_ref[...]
        o_inter = _dot((qf * jnp.exp(bc)).astype(BF16), state.astype(BF16))

        def before(qa, qb, ka, kb):
            ref = bc[kb - 1:kb, :]
            qt = qf[qa:qb] * jnp.exp(bc[qa:qb] - ref)
            kt = (kk[ka:kb] * jnp.exp(ref - bc[ka:kb])).astype(BF16)
            qs = jnp.concatenate([jnp.where(lane_head == hh, qt, 0.0) for hh in range(N_HEADS)], axis=0)
            att = _dot_nt(qs.astype(BF16), kt)
            mix = _dot(att.astype(BF16), vb[ka:kb])
            nq = qb - qa
            return sum(jnp.where(lane_head == hh, mix[hh * nq:(hh + 1) * nq], 0.0) for hh in range(N_HEADS))

        bc2 = bc * LOG2E

        def inside(a):
            b2 = bc2[a:a + blk]
            qb_ = qf[a:a + blk]
            ws = []
            for s_i in range(blk):
                e = jnp.exp2(jnp.minimum(b2 - b2[s_i:s_i + 1, :], caps[s_i]))
                ws.append(qb_ * kk[a + s_i:a + s_i + 1, :] * e)
            att = _dot(jnp.concatenate(ws, axis=0).astype(BF16), seg)
            return sum(att[s_i * blk:(s_i + 1) * blk] * v[a + s_i:a + s_i + 1, :] for s_i in range(blk))

        pieces = {a: [] for a in range(0, c, blk)}

        def cover(a, b):
            if b - a == blk:
                pieces[a].append(inside(a))
                return
            mid = (a + b) // 2
            cover(a, mid)
            cover(mid, b)
            res = before(mid, b, a, mid)
            for off in range(0, b - mid, blk):
                pieces[mid + off].append(res[off:off + blk])

        cover(0, c)
        o = o_inter + jnp.concatenate([sum(pieces[a]) for a in range(0, c, blk)], axis=0)

        khat = (kk * jnp.exp(b_last - bc)).astype(BF16)
        upd = lax.dot_general(khat, vb, dn0, preferred_element_type=F32)
        state_ref[...] = decay_col * state + jnp.where(same_head, upd, 0.0)

        o2 = _split3(o * o)
        ms = (_dot(o2[0], seg) + _dot(o2[1], seg)) * (1.0 / HEAD_DIM)
        o_ref[rows, :] = (o * lax.rsqrt(ms + EPS) * nw * gate).astype(o_ref.dtype)


def _hgrn(hg, lower_bound, norm_w, batch, seq):
    rows = HG_ROWS
    hg3 = hg.reshape(batch, seq, 4 * WIDTH)
    out = pl.pallas_call(
        _hgrn_kernel,
        grid=(batch, seq // rows),
        in_specs=[pl.BlockSpec((None, rows, 4 * WIDTH), lambda b, i: (b, i, 0)),
                  _const_spec((1, WIDTH)), _const_spec((1, WIDTH))],
        out_specs=pl.BlockSpec((None, rows, WIDTH), lambda b, i: (b, i, 0)),
        out_shape=jax.ShapeDtypeStruct((batch, seq, WIDTH), BF16),
        scratch_shapes=[pltpu.VMEM((WIDTH, WIDTH), F32)],
        compiler_params=_cparams(("parallel", "arbitrary")),
        name="hgrn2",
    )(hg3, lower_bound.reshape(1, WIDTH).astype(F32), norm_w.reshape(1, WIDTH).astype(F32))
    return out.reshape(batch * seq, WIDTH)


def _merge_kernel(x_ref, y0_ref, y1_ref, y2_ref, y3_ref, wg_ref, wb_ref, wo_ref, g_ref, b_ref, o_ref):
    x = x_ref[...]
    xb = x.astype(BF16)
    merged = jnp.zeros(x.shape, F32)
    for nbr, y_ref in enumerate((y0_ref, y1_ref, y2_ref, y3_ref)):
        gate = jax.nn.sigmoid(_dot(xb, wg_ref[:, nbr * D_MODEL:(nbr + 1) * D_MODEL]))
        merged = merged + gate * _dot(y_ref[...], wb_ref[nbr])
    y = _dot(merged.astype(BF16), wo_ref[...])
    o_ref[...] = _layernorm(ALPHA * x + y, g_ref[...], b_ref[...])


def _merge(x2d, ys, wg, wb, wo, g, b):
    n = x2d.shape[0]
    tm = TM_A
    row = lambda w: pl.BlockSpec((tm, w), lambda i: (i, 0))
    return pl.pallas_call(
        _merge_kernel,
        grid=(n // tm,),
        in_specs=[row(D_MODEL)] + [row(WIDTH)] * 4 +
                 [_const_spec(wg.shape), _const_spec(wb.shape), _const_spec(wo.shape),
                  _const_spec((1, D_MODEL)), _const_spec((1, D_MODEL))],
        out_specs=row(D_MODEL),
        out_shape=jax.ShapeDtypeStruct((n, D_MODEL), F32),
        compiler_params=_cparams(("parallel",)),
        name="merge_outproj_ln",
    )(x2d, *ys, wg, wb, wo, g.reshape(1, -1), b.reshape(1, -1))


def _memkv_kernel(m_ref, w_ref, k_ref, v_ref):
    kv = _dot(m_ref[...].astype(BF16), w_ref[...])
    k_ref[...] = kv[:, :WIDTH].astype(BF16)
    v_ref[...] = kv[:, WIDTH:].astype(BF16)


def _memkv(mem, wkv):
    batch, m, _ = mem.shape
    return pl.pallas_call(
        _memkv_kernel,
        grid=(batch,),
        in_specs=[pl.BlockSpec((None, m, D_MODEL), lambda b: (b, 0, 0)), _const_spec(wkv.shape)],
        out_specs=[pl.BlockSpec((None, m, WIDTH), lambda b: (b, 0, 0))] * 2,
        out_shape=[jax.ShapeDtypeStruct((batch, m, WIDTH), BF16)] * 2,
        compiler_params=_cparams(("parallel",)),
        name="mem_kv",
    )(mem, wkv)


def _xattn_kernel(x_ref, wq_ref, k_ref, v_ref, wo_ref, g_ref, b_ref, o_ref):
    x = x_ref[...]
    q = _dot(x.astype(BF16), wq_ref[...]).astype(BF16)
    k = k_ref[...]
    v = v_ref[...]
    lane = lax.broadcasted_iota(I32, (1, WIDTH), 1) // HEAD_DIM
    heads = range(N_HEADS)
    ss = [_dot_nt(jnp.where(lane == hh, q, jnp.zeros_like(q)), k) for hh in heads]
    es = [jnp.exp(s - jnp.max(s, axis=-1, keepdims=True)) for s in ss]
    ps = [(e / jnp.sum(e, axis=-1, keepdims=True)).astype(BF16) for e in es]
    o = jnp.zeros((x.shape[0], WIDTH), F32)
    for hh in heads:
        o = o + jnp.where(lane == hh, _dot(ps[hh], v), 0.0)
    y = _dot(o.astype(BF16), wo_ref[...])
    o_ref[...] = _layernorm(ALPHA * x + y, g_ref[...], b_ref[...])


def _xattn(x2d, wq, k, v, wo, g, b, batch, seq):
    tm = TM_A
    m = k.shape[1]
    x3 = x2d.reshape(batch, seq, D_MODEL)
    row = pl.BlockSpec((None, tm, D_MODEL), lambda bb, i: (bb, i, 0))
    kv_spec = pl.BlockSpec((None, m, WIDTH), lambda bb, i: (bb, 0, 0))
    out = pl.pallas_call(
        _xattn_kernel,
        grid=(batch, seq // tm),
        in_specs=[row, _const_spec(wq.shape), kv_spec, kv_spec, _const_spec(wo.shape),
                  _const_spec((1, D_MODEL)), _const_spec((1, D_MODEL))],
        out_specs=row,
        out_shape=jax.ShapeDtypeStruct((batch, seq, D_MODEL), F32),
        compiler_params=_cparams(("parallel", "parallel")),
        name="mem_xattn_ln",
    )(x3, wq, k, v, wo, g.reshape(1, -1), b.reshape(1, -1))
    return out.reshape(batch * seq, D_MODEL)


def _ffn_kernel(x_ref, w13_ref, w2_ref, g_ref, b_ref, o_ref):
    x = x_ref[...]
    xb = x.astype(BF16)
    y = None
    for h in range(F_DENSE // TF_FFN):
        lo = h * TF_FFN
        a = _dot(xb, w13_ref[:, lo:lo + TF_FFN])
        gate = _dot(xb, w13_ref[:, F_DENSE + lo:F_DENSE + lo + TF_FFN])
        part = _dot((a * jax.nn.sigmoid(a) * gate).astype(BF16), w2_ref[lo:lo + TF_FFN, :])
        y = part if y is None else y + part
    o_ref[...] = _layernorm(ALPHA * x + y, g_ref[...], b_ref[...])


def _ffn(x2d, w13, w2, g, b):
    n = x2d.shape[0]
    tm = TM_FFN
    return pl.pallas_call(
        _ffn_kernel,
        grid=(n // tm,),
        in_specs=[pl.BlockSpec((tm, D_MODEL), lambda i: (i, 0)),
                  _const_spec(w13.shape), _const_spec(w2.shape),
                  _const_spec((1, D_MODEL)), _const_spec((1, D_MODEL))],
        out_specs=pl.BlockSpec((tm, D_MODEL), lambda i: (i, 0)),
        out_shape=jax.ShapeDtypeStruct((n, D_MODEL), F32),
        compiler_params=_cparams(("parallel",)),
        name="ffn_ln",
    )(x2d, w13, w2, g.reshape(1, -1), b.reshape(1, -1))


def _router_kernel(x_ref, r_ref, info_ref, wts_ref, cnt_ref, carry_ref):
    tm = x_ref.shape[0]

    @pl.when(pl.program_id(0) == 0)
    def _():
        carry_ref[...] = jnp.zeros_like(carry_ref)

    logits = jnp.dot(x_ref[...], r_ref[...], precision=lax.Precision.HIGHEST, preferred_element_type=F32)
    lane = lax.broadcasted_iota(I32, (tm, LANES), 1)
    lg = jnp.where(lane < N_EXPERTS, logits, -jnp.inf)
    m1 = jnp.max(lg, axis=-1, keepdims=True)
    i1 = jnp.min(jnp.where(lg == m1, lane, LANES), axis=-1, keepdims=True)
    lg2 = jnp.where(lane == i1, -jnp.inf, lg)
    m2 = jnp.max(lg2, axis=-1, keepdims=True)
    i2 = jnp.min(jnp.where(lg2 == m2, lane, LANES), axis=-1, keepdims=True)
    e = jnp.exp(m2 - m1)
    w1 = 1.0 / (1.0 + e)
    w2 = e / (1.0 + e)
    sel1 = lane == i1
    sel2 = lane == i2
    chosen = jnp.where(sel1 | sel2, 1.0, 0.0)
    row = lax.broadcasted_iota(I32, (tm, tm), 0)
    col = lax.broadcasted_iota(I32, (tm, tm), 1)
    before = (col < row).astype(BF16)
    ranks = _dot(before, chosen.astype(BF16)) + carry_ref[...]
    r1 = jnp.sum(jnp.where(sel1, ranks, 0.0), axis=-1, keepdims=True).astype(I32)
    r2 = jnp.sum(jnp.where(sel2, ranks, 0.0), axis=-1, keepdims=True).astype(I32)
    carry_ref[...] = carry_ref[...] + jnp.sum(chosen, axis=0, keepdims=True)
    info_ref[...] = jnp.where(lane == 0, i1, jnp.where(lane == 1, i2, jnp.where(lane == 2, r1,
                              jnp.where(lane == 3, r2, 0))))
    wts_ref[...] = jnp.where(lane == 0, w1, jnp.where(lane == 1, w2, 0.0))
    cnt_ref[...] = carry_ref[...]


def _router(x2d, router):
    n = x2d.shape[0]
    tm = TM_A
    r_pad = jnp.zeros((D_MODEL, LANES), F32).at[:, :N_EXPERTS].set(router.astype(F32))
    row = pl.BlockSpec((tm, LANES), lambda i: (i, 0))
    return pl.pallas_call(
        _router_kernel,
        grid=(n // tm,),
        in_specs=[pl.BlockSpec((tm, D_MODEL), lambda i: (i, 0)), _const_spec(r_pad.shape)],
        out_specs=[row, row, pl.BlockSpec((1, LANES), lambda i: (0, 0))],
        out_shape=[jax.ShapeDtypeStruct((n, LANES), I32), jax.ShapeDtypeStruct((n, LANES), F32),
                   jax.ShapeDtypeStruct((1, LANES), F32)],
        scratch_shapes=[pltpu.VMEM((1, LANES), F32)],
        compiler_params=_cparams(("arbitrary",)),
        name="moe_router",
    )(x2d, r_pad)


def _dispatch_kernel(pad_ref, dest_ref, x_ref, xb_hbm, stage_ref, sems):
    tm = x_ref.shape[0]
    i = pl.program_id(0)
    last = pl.num_programs(0) - 1
    slot = i % 2

    def wait_step(s):
        for _ in range(2):
            pltpu.make_async_copy(stage_ref.at[s], xb_hbm.at[pl.ds(0, tm), :], sems.at[s]).wait()

    @pl.when(i >= 2)
    def _():
        wait_step(slot)

    for s in range(2):
        @pl.when(slot == s)
        def _():
            stage_ref[s] = x_ref[...]

            def issue(r, c):
                for k in range(2):
                    pltpu.make_async_copy(stage_ref.at[s, pl.ds(r, 1), :],
                                          xb_hbm.at[pl.ds(dest_ref[0, 2 * r + k], 1), :], sems.at[s]).start()
                return c
            lax.fori_loop(0, tm, issue, 0, unroll=8)

    @pl.when(i == last)
    def _():
        def fill(e, c):
            def one(s, c2):
                pltpu.make_async_copy(stage_ref.at[slot, pl.ds(0, 1), :], xb_hbm.at[pl.ds(s, 1), :],
                                      sems.at[2]).start()
                return c2

            def done(s, c2):
                pltpu.make_async_copy(stage_ref.at[slot, pl.ds(0, 1), :], xb_hbm.at[pl.ds(0, 1), :],
                                      sems.at[2]).wait()
                return c2
            lax.fori_loop(pad_ref[0, e], pad_ref[1, e], one, 0)
            lax.fori_loop(pad_ref[0, e], pad_ref[1, e], done, 0)
            return c
        lax.fori_loop(0, pad_ref.shape[1], fill, 0)
        wait_step(slot)

        @pl.when(last >= 1)
        def _():
            wait_step(1 - slot)


def _dispatch(x2d, dest, pads, nblk):
    n = x2d.shape[0]
    tm = TM_DISP
    nt = n // tm
    grid_spec = pltpu.PrefetchScalarGridSpec(
        num_scalar_prefetch=1,
        grid=(nt,),
        in_specs=[pl.BlockSpec((None, 1, 2 * tm), lambda i, pads: (i, 0, 0), memory_space=pltpu.SMEM),
                  pl.BlockSpec((tm, D_MODEL), lambda i, pads: (i, 0))],
        out_specs=pl.BlockSpec(memory_space=pl.ANY),
        scratch_shapes=[pltpu.VMEM((2, tm, D_MODEL), F32), pltpu.SemaphoreType.DMA((3,))],
    )
    return pl.pallas_call(
        _dispatch_kernel,
        grid_spec=grid_spec,
        out_shape=jax.ShapeDtypeStruct((nblk * MOE_TB, D_MODEL), F32),
        compiler_params=_cparams(("arbitrary",), disable_bounds_checks=True),
        name="moe_dispatch",
    )(pads, dest.reshape(nt, 1, 2 * tm), x2d)


def _expert_kernel(nused_ref, bexp_ref, x_ref, w1_ref, w3_ref, w2_ref, o_ref, acc_ref):
    f = pl.program_id(1)

    @pl.when(pl.program_id(0) < nused_ref[0])
    def _():
        xb = x_ref[...].astype(BF16)
        a = _dot(xb, w1_ref[...])
        gate = _dot(xb, w3_ref[...])
        part = _dot((a * jax.nn.sigmoid(a) * gate).astype(BF16), w2_ref[...])

        @pl.when(f == 0)
        def _():
            acc_ref[...] = part

        @pl.when(f > 0)
        def _():
            acc_ref[...] += part

        @pl.when(f == pl.num_programs(1) - 1)
        def _():
            o_ref[...] = acc_ref[...]

    @pl.when(pl.program_id(0) >= nused_ref[0])
    def _():
        o_ref[...] = jnp.zeros_like(o_ref)


def _experts(xb, w13, w2, nused, blk_exp, nblk):
    tb, tf = MOE_TB, MOE_TF
    nf = F_EXPERT // tf

    def blk(i, nu):
        return jnp.maximum(jnp.minimum(i, nu[0] - 1), 0)

    def ftile(i, f, nu):
        return jnp.where(i < nu[0], f, nf - 1)

    grid_spec = pltpu.PrefetchScalarGridSpec(
        num_scalar_prefetch=2,
        grid=(nblk, nf),
        in_specs=[pl.BlockSpec((tb, D_MODEL), lambda i, f, nu, be: (blk(i, nu), 0)),
                  pl.BlockSpec((None, D_MODEL, tf), lambda i, f, nu, be: (be[blk(i, nu)], 0, ftile(i, f, nu))),
                  pl.BlockSpec((None, D_MODEL, tf), lambda i, f, nu, be: (be[blk(i, nu)], 0, nf + ftile(i, f, nu))),
                  pl.BlockSpec((None, tf, D_MODEL), lambda i, f, nu, be: (be[blk(i, nu)], ftile(i, f, nu), 0))],
        out_specs=pl.BlockSpec((tb, D_MODEL), lambda i, f, nu, be: (i, 0)),
        scratch_shapes=[pltpu.VMEM((tb, D_MODEL), F32)],
    )
    return pl.pallas_call(
        _expert_kernel,
        grid_spec=grid_spec,
        out_shape=jax.ShapeDtypeStruct((nblk * tb, D_MODEL), F32),
        compiler_params=_cparams(("arbitrary", "arbitrary")),
        name="moe_experts",
    )(nused, blk_exp, xb, w13, w13, w2)


def _combine_kernel(dest_ref, nxt_ref, y_hbm, x_ref, wts_ref, g_ref, b_ref, o_ref, buf_ref, sems):
    tm = x_ref.shape[0]
    i = pl.program_id(0)
    slot = i % 2

    def gather(idx_ref, s):
        def issue(r, c):
            for k in range(2):
                pltpu.make_async_copy(y_hbm.at[pl.ds(idx_ref[0, 2 * r + k], 1), :],
                                      buf_ref.at[s, k, pl.ds(r, 1), :], sems.at[s]).start()
            return c
        lax.fori_loop(0, tm, issue, 0, unroll=8)

    @pl.when(i == 0)
    def _():
        gather(dest_ref, 0)

    for s in range(2):
        @pl.when((i + 1 < pl.num_programs(0)) & (slot != s))
        def _():
            gather(nxt_ref, s)

    for k in range(2):
        pltpu.make_async_copy(y_hbm.at[pl.ds(0, tm), :], buf_ref.at[slot, k], sems.at[slot]).wait()
    wts = wts_ref[...]
    y = wts[:, 0:1] * buf_ref[slot, 0] + wts[:, 1:2] * buf_ref[slot, 1]
    o_ref[...] = _layernorm(ALPHA * x_ref[...] + y, g_ref[...], b_ref[...])


def _combine(yb, dest, x2d, wts, g, b):
    n = x2d.shape[0]
    tm = TM_COMB
    nt = n // tm
    row = lambda w: pl.BlockSpec((tm, w), lambda i: (i, 0))
    dest3 = dest.reshape(nt, 1, 2 * tm)
    return pl.pallas_call(
        _combine_kernel,
        grid=(nt,),
        in_specs=[pl.BlockSpec((None, 1, 2 * tm), lambda i: (i, 0, 0), memory_space=pltpu.SMEM),
                  pl.BlockSpec((None, 1, 2 * tm), lambda i: (jnp.minimum(i + 1, nt - 1), 0, 0),
                               memory_space=pltpu.SMEM),
                  pl.BlockSpec(memory_space=pl.ANY), row(D_MODEL), row(LANES),
                  _const_spec((1, D_MODEL)), _const_spec((1, D_MODEL))],
        out_specs=row(D_MODEL),
        out_shape=jax.ShapeDtypeStruct((n, D_MODEL), F32),
        scratch_shapes=[pltpu.VMEM((2, 2, tm, D_MODEL), F32), pltpu.SemaphoreType.DMA((2,))],
        compiler_params=_cparams(("arbitrary",), disable_bounds_checks=True),
        name="moe_combine_ln",
    )(dest3, dest3, yb, x2d, wts, g.reshape(1, -1), b.reshape(1, -1))


def _moe(x2d, router, w13, w2, g, b):
    n = x2d.shape[0]
    tb = MOE_TB
    info, wts, cnt = _router(x2d, router)
    idx = info[:, 0:2]
    rank = info[:, 2:4]
    counts = cnt[0, :N_EXPERTS].astype(I32)
    padded = (counts + tb - 1) // tb * tb
    pend = jnp.cumsum(padded)
    pstart = pend - padded
    dest = (pstart[idx] + rank).astype(I32).reshape(-1)
    nblk = -(-(2 * n + N_EXPERTS * (tb - 1)) // tb)
    pads = jnp.stack([jnp.append(pstart + counts, pend[-1]), jnp.append(pend, nblk * tb)]).astype(I32)
    nused = (pend[-1] // tb).astype(I32).reshape(1)
    first_row = jnp.arange(nblk, dtype=I32) * tb
    blk_exp = jnp.minimum(jnp.sum(pend[None, :] <= first_row[:, None], axis=1), N_EXPERTS - 1).astype(I32)
    xb = _dispatch(x2d, dest, pads, nblk)
    yb = _experts(xb, w13, w2, nused, blk_exp, nblk)
    return _combine(yb, dest, x2d, wts, g, b)


def kernel(x, mem, positions, rel_bias_table, hgrn_lb_logits, w_in, mla_q_norm, mla_w_uq, mla_kv_norm, mla_w_ukv, swa_sinks, hgrn_norm, w_branch, w_out, ln_g, ln_b, xa_wq, xa_wkv, xa_wo, ffn_w13, ffn_w2, moe_router, moe_w13, moe_w2):
    batch, seq, _ = x.shape
    n = batch * seq
    sm = jax.nn.softmax(hgrn_lb_logits.astype(F32), axis=0)
    lower_bounds = jnp.cumsum(sm, axis=0) - sm[0]
    ctab, stab = _rope_tables(positions)
    xc = x.reshape(n, D_MODEL)
    for l in range(DEPTH):
        wts = _inproj_weights(w_in[l], mla_w_uq[l], mla_w_ukv[l])
        mq, mk, mv, swq, swk, swv, hg, sbq, sbk, sbv = _inproj(xc, wts, ctab, stab, mla_q_norm[l], mla_kv_norm[l])
        y_mla = _mla_attention(mq, mk, mv, batch, seq)
        y_swa = _swa_attention(swq, swk, swv, positions, swa_sinks[l], rel_bias_table, batch, seq)
        y_hg = _hgrn(hg, lower_bounds[l], hgrn_norm[l], batch, seq)
        y_sb = _sb_attention(sbq, sbk, sbv, batch, seq)
        go = _IN_OFF['gates']
        xc = _merge(xc, (y_mla, y_swa, y_hg, y_sb), w_in[l][:, go:].astype(BF16), w_branch[l].astype(BF16),
                    w_out[l].astype(BF16), ln_g[l, 0], ln_b[l, 0])
        mk_, mv_ = _memkv(mem, xa_wkv[l].astype(BF16))
        xc = _xattn(xc, (xa_wq[l] * QK_SCALE).astype(BF16), mk_, mv_, xa_wo[l].astype(BF16),
                    ln_g[l, 1], ln_b[l, 1], batch, seq)
        if l % 2 == 0:
            xc = _ffn(xc, ffn_w13[l // 2].astype(BF16), ffn_w2[l // 2].astype(BF16), ln_g[l, 2], ln_b[l, 2])
        else:
            xc = _moe(xc, moe_router[l // 2], moe_w13[l // 2].astype(BF16), moe_w2[l // 2].astype(BF16),
                      ln_g[l, 2], ln_b[l, 2])
    return xc.reshape(batch, seq, D_MODEL)
```

```python
import functools
import math

import jax
import jax.numpy as jnp
from jax import lax
from jax.experimental import pallas as pl
from jax.experimental.pallas import tpu as pltpu

F32 = jnp.float32
BF16 = jnp.bfloat16
I32 = jnp.int32

D_MODEL = 1024
DEPTH = 2
EPS = 1e-5
NEG_BIG = -1e30
LANES = 128
HEAD_DIM = 64
N_HEADS = 4
WIDTH = N_HEADS * HEAD_DIM

MLA_Q_LORA = 256
MLA_KV_LORA = 128
MLA_NOPE = 64
MLA_ROPE = 32
ROPE_THETA = 10000.0
MLA_SCALE = (MLA_NOPE + MLA_ROPE) ** -0.5
LOG2E = math.log2(math.e)
QK_SCALE = HEAD_DIM ** -0.5

SB_RUN_FLOOR = -150.0
SWA_WINDOW = 128
REL_BUCKETS = 32
REL_MAX_DIST = 128
HGRN_CHUNK = 64
HGRN_BLOCK = 16
N_EXPERTS = 8
F_DENSE = 2816
F_EXPERT = 3584
ALPHA = (2 * DEPTH) ** 0.25

_IN_SPLITS = (('mla_cq', 256), ('mla_ckv', 128), ('mla_kr', 32), ('swa_q', 256), ('swa_k', 128),
              ('swa_v', 128), ('hgrn', 1024), ('sb_q', 256), ('sb_k', 256), ('sb_v', 256), ('gates', 4096))
_IN_OFF = {}
_o = 0
for _n, _w in _IN_SPLITS:
    _IN_OFF[_n] = _o
    _o += _w

_A_SPLITS = (('cq', 256), ('ckv', 128), ('kra', 128), ('krb', 128), ('swa_q', 256), ('swa_k', 256),
             ('swa_v', 256), ('hgrn', 1024), ('sb_q', 256), ('sb_k', 256), ('sb_v', 256))
_A_OFF = {}
_o = 0
for _n, _w in _A_SPLITS:
    _A_OFF[_n] = (_o, _o + _w)
    _o += _w
A_COLS = _o

TM_A = 512
TQ_ATT = 256
MLA_TQ = 512
MLA_TK = 512
MLA_WIDE = 4
MLA_GROUP = 4
SWA_TQ = 512
HG_ROWS = 256
TM_FFN = 512
TF_FFN = 1408
MOE_TB = 512
MOE_TF = 1792
TM_COMB = 256
TM_DISP = 512
VMEM_LIMIT = 56 * 1024 * 1024


def _cparams(sem, **kw):
    return pltpu.CompilerParams(dimension_semantics=sem, vmem_limit_bytes=VMEM_LIMIT, **kw)


def _const_spec(shape):
    nd = len(shape)
    return pl.BlockSpec(shape, lambda *_: (0,) * nd, pipeline_mode=pl.Buffered(1))


def _layernorm(v, g, b):
    mu = jnp.mean(v, axis=-1, keepdims=True)
    vc = v - mu
    var = jnp.mean(vc * vc, axis=-1, keepdims=True)
    return vc * lax.rsqrt(var + EPS) * g + b


def _dot(a, b):
    return jnp.dot(a, b, preferred_element_type=F32)


def _dot_nt(a, b):
    return lax.dot_general(a, b, (((1,), (1,)), ((), ())), preferred_element_type=F32)


def _split3(a):
    hi = a.astype(BF16)
    r = a - hi.astype(F32)
    mid = r.astype(BF16)
    lo = (r - mid.astype(F32)).astype(BF16)
    return hi, mid, lo


def _rope_kernel(pos_ref, freq_ref, c_ref, s_ref):
    lane = lax.broadcasted_iota(I32, pos_ref.shape, 1)
    ang = pos_ref[...] * freq_ref[...]
    rope = (lane >= MLA_NOPE) & (lane < MLA_NOPE + MLA_ROPE)
    first = lane < MLA_NOPE + MLA_ROPE // 2
    c_ref[...] = jnp.where(lane < MLA_NOPE, 1.0, jnp.where(rope, jnp.cos(ang), 0.0))
    sn = jnp.sin(ang)
    s_ref[...] = jnp.where(rope, jnp.where(first, -sn, sn), 0.0)


def _rope_tables(positions):
    n = positions.size
    half = MLA_ROPE // 2
    inv_freq = ROPE_THETA ** (-jnp.arange(half, dtype=F32) / half)
    freq = jnp.zeros((1, LANES), F32).at[0, MLA_NOPE:MLA_NOPE + MLA_ROPE].set(jnp.tile(inv_freq, 2))
    posb = jnp.broadcast_to(positions.reshape(n, 1).astype(F32), (n, LANES))
    tm = 1024
    return pl.pallas_call(
        _rope_kernel,
        grid=(n // tm,),
        in_specs=[pl.BlockSpec((tm, LANES), lambda i: (i, 0)), _const_spec((1, LANES))],
        out_specs=[pl.BlockSpec((tm, LANES), lambda i: (i, 0))] * 2,
        out_shape=[jax.ShapeDtypeStruct((n, LANES), F32)] * 2,
        compiler_params=_cparams(("parallel",)),
        name="rope_tables",
    )(posb, freq)


def _inproj_kernel(x_ref, w_ref, c_ref, s_ref, qn_ref, kvn_ref, wuqa_ref, wuqb_ref, wuk_ref, wuv_ref,
                   mq_ref, mk_ref, mv_ref, swq_ref, swk_ref, swv_ref, hg_ref, sbq_ref, sbk_ref, sbv_ref):
    h = _dot(x_ref[...].astype(BF16), w_ref[...])

    def cols(name):
        lo, hi = _A_OFF[name]
        return h[:, lo:hi]

    c = c_ref[...]
    s = s_ref[...]
    c4 = jnp.concatenate([c] * N_HEADS, axis=1)
    s4 = jnp.concatenate([s] * N_HEADS, axis=1)

    cq = cols('cq')
    cqn = (cq * lax.rsqrt(jnp.mean(cq * cq, axis=-1, keepdims=True) + EPS) * qn_ref[...]).astype(BF16)
    q = _dot(cqn, wuqa_ref[...]) * c4 + _dot(cqn, wuqb_ref[...]) * s4
    mq_ref[...] = (q * (MLA_SCALE * LOG2E)).astype(BF16)

    ckv = cols('ckv')
    ckvn = (ckv * lax.rsqrt(jnp.mean(ckv * ckv, axis=-1, keepdims=True) + EPS) * kvn_ref[...]).astype(BF16)
    krot = cols('kra') * c + cols('krb') * s
    mk_ref[...] = (_dot(ckvn, wuk_ref[...]) + jnp.concatenate([krot] * N_HEADS, axis=1)).astype(BF16)
    mv_ref[...] = _dot(ckvn, wuv_ref[...]).astype(BF16)

    swq_ref[...] = cols('swa_q').astype(BF16)
    swk_ref[...] = cols('swa_k').astype(BF16)
    swv_ref[...] = cols('swa_v').astype(BF16)
    hg_ref[...] = cols('hgrn')
    sbq_ref[...] = cols('sb_q').astype(BF16)
    sbk_ref[...] = cols('sb_k').astype(BF16)
    sbv_ref[...] = cols('sb_v').astype(BF16)


def _inproj_weights(w_in, w_uq, w_ukv):
    def seg(name, width):
        o = _IN_OFF[name]
        return w_in[:, o:o + width]

    kr = seg('mla_kr', MLA_ROPE)
    half = MLA_ROPE // 2
    z64 = jnp.zeros((D_MODEL, MLA_NOPE), F32)
    z32 = jnp.zeros((D_MODEL, LANES - MLA_NOPE - MLA_ROPE), F32)
    kra = jnp.concatenate([z64, kr, z32], axis=1)
    krb = jnp.concatenate([z64, kr[:, half:], kr[:, :half], z32], axis=1)
    swk = seg('swa_k', 128)
    swv = seg('swa_v', 128)
    dup = lambda t: jnp.concatenate([t[:, :64], t[:, :64], t[:, 64:], t[:, 64:]], axis=1)
    w_a = jnp.concatenate([
        seg('mla_cq', 256), seg('mla_ckv', 128), kra, krb,
        seg('swa_q', 256) * QK_SCALE, dup(swk), dup(swv),
        seg('hgrn', 1024), seg('sb_q', 256) * (QK_SCALE * LOG2E), seg('sb_k', 256), seg('sb_v', 256)], axis=1)

    qd = MLA_NOPE + MLA_ROPE
    zq = jnp.zeros((MLA_Q_LORA, LANES - qd), F32)
    zn = jnp.zeros((MLA_Q_LORA, MLA_NOPE), F32)
    qa, qb = [], []
    for hh in range(N_HEADS):
        nope = w_uq[:, hh * qd: hh * qd + MLA_NOPE]
        rope = w_uq[:, hh * qd + MLA_NOPE: (hh + 1) * qd]
        qa += [nope, rope, zq]
        qb += [zn, rope[:, half:], rope[:, :half], zq]
    wuqa = jnp.concatenate(qa, axis=1)
    wuqb = jnp.concatenate(qb, axis=1)
    lane = jnp.arange(N_HEADS * LANES) % LANES
    wuk = jnp.where(lane[None, :] < MLA_NOPE, w_ukv, 0.0)
    wuv = jnp.concatenate([w_ukv[:, hh * LANES + MLA_NOPE:(hh + 1) * LANES] for hh in range(N_HEADS)], axis=1)
    return tuple(t.astype(BF16) for t in (w_a, wuqa, wuqb, wuk, wuv))


def _inproj(x2d, wts, ctab, stab, q_norm, kv_norm):
    n = x2d.shape[0]
    w_a, wuqa, wuqb, wuk, wuv = wts
    tm = TM_A
    row = lambda w: pl.BlockSpec((tm, w), lambda i: (i, 0))
    out_w = (512, 512, 256, 256, 256, 256, 1024, 256, 256, 256)
    out_dt = (BF16, BF16, BF16, BF16, BF16, BF16, F32, BF16, BF16, BF16)
    return pl.pallas_call(
        _inproj_kernel,
        grid=(n // tm,),
        in_specs=[row(D_MODEL), _const_spec(w_a.shape), row(LANES), row(LANES),
                  _const_spec((1, MLA_Q_LORA)), _const_spec((1, MLA_KV_LORA)),
                  _const_spec(wuqa.shape), _const_spec(wuqb.shape), _const_spec(wuk.shape),
                  _const_spec(wuv.shape)],
        out_specs=[row(w) for w in out_w],
        out_shape=[jax.ShapeDtypeStruct((n, w), d) for w, d in zip(out_w, out_dt)],
        compiler_params=_cparams(("parallel",)),
        name="inproj",
    )(x2d, w_a, ctab, stab, q_norm.reshape(1, -1), kv_norm.reshape(1, -1), wuqa, wuqb, wuk, wuv)


def _half_mask(half):
    lane = lax.broadcasted_iota(I32, (1, LANES), 1)
    return (lane < HEAD_DIM) if half == 0 else (lane >= HEAD_DIM)


def _mla_kernel(q_ref, k_ref, v_ref, o_ref):
    tq = q_ref.shape[0]
    tk = MLA_TK
    nsub = tq // tk
    i = pl.program_id(1)
    row = lax.broadcasted_iota(I32, (tq, tk), 0)
    col = lax.broadcasted_iota(I32, (tq, tk), 1)
    ones = jnp.ones((1, LANES), BF16)

    def update(off, carry, heads, mask, width=tk):
        ss = [_dot_nt(q_ref[:, hh * LANES:(hh + 1) * LANES],
                      k_ref[pl.ds(off, width), hh * LANES:(hh + 1) * LANES]) for hh in heads]
        if mask is not None:
            ss = [jnp.where(mask, s, NEG_BIG) for s in ss]
        ms = [jnp.maximum(c[0], jnp.max(s, axis=-1, keepdims=True)) for c, s in zip(carry, ss)]
        pms = [jnp.exp2(s - m).astype(BF16) for s, m in zip(ss, ms)]
        new = []
        for n, hh in enumerate(heads):
            vb = v_ref[pl.ds(off, width), (hh // 2) * LANES:(hh // 2 + 1) * LANES]
            vb = jnp.where(_half_mask(hh % 2), vb, ones)
            m, acc = carry[n]
            new.append((ms[n], jnp.exp2(m - ms[n]) * acc + _dot(pms[n], vb)))
        return tuple(new)

    accs = []
    for g in range(0, N_HEADS, MLA_GROUP):
        heads = tuple(range(g, g + MLA_GROUP))
        init = tuple((jnp.full((tq, 1), NEG_BIG, F32), jnp.zeros((tq, LANES), F32)) for _ in heads)
        nkb = i * nsub
        wide = MLA_WIDE * tk
        carry = lax.fori_loop(
            0, nkb // MLA_WIDE,
            lambda j, c, heads=heads: update(pl.multiple_of(j * wide, wide), c, heads, None, wide), init)
        done = nkb // MLA_WIDE * MLA_WIDE
        rest = nkb - done
        carry = lax.cond(
            rest >= 2,
            lambda c, heads=heads: update(pl.multiple_of(done * tk, 2 * tk), c, heads, None, 2 * tk),
            lambda c: c, carry)
        carry = lax.cond(
            rest % 2 == 1,
            lambda c, heads=heads: update(pl.multiple_of((nkb - 1) * tk, tk), c, heads, None),
            lambda c: c, carry)
        for r in range(nsub):
            carry = update(pl.multiple_of(i * tq + r * tk, tk), carry, heads, col + r * tk <= row)
        accs += [c[1] for c in carry]
    outs = []
    for p in range(N_HEADS // 2):
        a0, a1 = accs[2 * p], accs[2 * p + 1]
        outs.append(jnp.where(_half_mask(0), a0 / a0[:, HEAD_DIM:HEAD_DIM + 1], a1 / a1[:, 0:1]))
    o_ref[...] = jnp.concatenate(outs, axis=1).astype(o_ref.dtype)


def _mla_attention(q, k, v, batch, seq):
    tq = MLA_TQ
    q3, k3, v3 = (t.reshape(batch, seq, t.shape[-1]) for t in (q, k, v))
    out = pl.pallas_call(
        _mla_kernel,
        grid=(batch, seq // tq),
        in_specs=[pl.BlockSpec((None, tq, 512), lambda b, i: (b, i, 0)),
                  pl.BlockSpec((None, seq, 512), lambda b, i: (b, 0, 0), pipeline_mode=pl.Buffered(1)),
                  pl.BlockSpec((None, seq, WIDTH), lambda b, i: (b, 0, 0), pipeline_mode=pl.Buffered(1))],
        out_specs=pl.BlockSpec((None, tq, WIDTH), lambda b, i: (b, i, 0)),
        out_shape=jax.ShapeDtypeStruct((batch, seq, WIDTH), BF16),
        compiler_params=_cparams(("parallel", "arbitrary")),
        name="mla_attention",
    )(q3, k3, v3)
    return out.reshape(batch * seq, WIDTH)


def _sb_kernel(q_ref, k_ref, v_ref, o_ref):
    tq = q_ref.shape[0]
    i = pl.program_id(1)
    row = lax.broadcasted_iota(I32, (tq, tq), 0)
    col = lax.broadcasted_iota(I32, (tq, tq), 1)
    strict = col < row
    later = (row > col).astype(BF16)
    qs = []
    for hh in range(N_HEADS):
        qp = q_ref[:, (hh // 2) * LANES:(hh // 2 + 1) * LANES]
        qs.append(jnp.where(_half_mask(hh % 2), qp, jnp.zeros_like(qp)))

    def block(j, carry, diag):
        off = pl.multiple_of(j * tq, tq)
        runs, accs = carry
        heads = range(N_HEADS)
        zs = [_dot_nt(qs[hh], k_ref[pl.ds(off, tq), (hh // 2) * LANES:(hh // 2 + 1) * LANES]) for hh in heads]
        lsps = [jnp.minimum(z, 0.0) - jnp.log2(1.0 + jnp.exp2(-jnp.abs(z))) for z in zs]
        lsns = [lsp - z for lsp, z in zip(lsps, zs)]
        if diag:
            lsns = [jnp.where(strict, t, 0.0) for t in lsns]
        his = [t.astype(BF16) for t in lsns]
        los = [(t - hi.astype(F32)).astype(BF16) for t, hi in zip(lsns, his)]
        rems = [_dot(hi, later) + _dot(lo, later) for hi, lo in zip(his, los)]
        args = [lsps[hh] + rems[hh] + runs[hh] for hh in heads]
        if diag:
            args = [jnp.where(strict, t, NEG_BIG) for t in args]
        probs = [jnp.exp2(t).astype(BF16) for t in args]
        new_runs = tuple(runs[hh] + rems[hh][:, 0:1] + lsns[hh][:, 0:1] for hh in heads)
        new_accs = list(accs)
        for hh in heads:
            p = hh // 2
            vb = v_ref[pl.ds(off, tq), p * LANES:(p + 1) * LANES]
            vb = jnp.where(_half_mask(hh % 2), vb, jnp.zeros_like(vb))
            new_accs[p] = new_accs[p] + _dot(probs[hh], vb)
        return new_runs, tuple(new_accs)

    init = (tuple(jnp.zeros((tq, 1), F32) for _ in range(N_HEADS)),
            tuple(jnp.zeros((tq, LANES), F32) for _ in range(N_HEADS // 2)))
    def still_active(runs):
        top = functools.reduce(jnp.maximum, runs)
        return (jnp.max(top) > SB_RUN_FLOOR).astype(I32)

    runs, accs = block(i, init, True)

    def cond(c):
        return (c[0] < i) & (c[1] > 0)

    def body(c):
        jj, _, runs, accs = c
        runs, accs = block(i - 1 - jj, (runs, accs), False)
        return jj + 1, still_active(runs), runs, accs

    _, _, _, accs = lax.while_loop(cond, body, (jnp.int32(0), still_active(runs), runs, accs))
    o_ref[...] = jnp.concatenate(accs, axis=1).astype(o_ref.dtype)


def _sb_attention(q, k, v, batch, seq):
    tq = TQ_ATT
    q3, k3, v3 = (t.reshape(batch, seq, WIDTH) for t in (q, k, v))
    out = pl.pallas_call(
        _sb_kernel,
        grid=(batch, seq // tq),
        in_specs=[pl.BlockSpec((None, tq, WIDTH), lambda b, i: (b, i, 0)),
                  pl.BlockSpec((None, seq, WIDTH), lambda b, i: (b, 0, 0)),
                  pl.BlockSpec((None, seq, WIDTH), lambda b, i: (b, 0, 0))],
        out_specs=pl.BlockSpec((None, tq, WIDTH), lambda b, i: (b, i, 0)),
        out_shape=jax.ShapeDtypeStruct((batch, seq, WIDTH), BF16),
        compiler_params=_cparams(("parallel", "arbitrary")),
        name="stick_breaking",
    )(q3, k3, v3)
    return out.reshape(batch * seq, WIDTH)


def _rel_bucket(dist):
    exact = REL_BUCKETS // 2
    n = jnp.maximum(dist, 0)
    nf = jnp.maximum(n, 1).astype(F32)
    large = exact + (jnp.log(nf / exact) / math.log(REL_MAX_DIST / exact) * (REL_BUCKETS - exact)).astype(I32)
    large = jnp.clip(large, 0, REL_BUCKETS - 1)
    return jnp.where(n < exact, n, large)


def _swa_kernel(sink_ref, tab_ref, q_ref, kc_ref, kh_ref, vc_ref, vh_ref, pq_ref, pkc_ref, pkh_ref, o_ref):
    w = SWA_WINDOW
    step = pl.program_id(1)
    row = lax.broadcasted_iota(I32, (w, w), 0)
    col = lax.broadcasted_iota(I32, (w, w), 1)
    valid_c = col <= row
    valid_p = col > row
    tabs = [jnp.broadcast_to(tab_ref[hh:hh + 1, :], (w, LANES)) for hh in range(N_HEADS)]
    ones = jnp.ones((1, LANES), BF16)
    nsub = q_ref.shape[0] // w
    chains = [(r, hh) for r in range(nsub) for hh in range(N_HEADS)]

    def keys(ref, halo_ref, r, hh):
        sl = slice((hh // 2) * LANES, (hh // 2 + 1) * LANES)
        cur = ref[r * w:(r + 1) * w, sl]
        prev = ref[(r - 1) * w:r * w, sl] if r else halo_ref[:, sl]
        return cur, prev

    buckets = []
    for r in range(nsub):
        pq = pq_ref[r * w:(r + 1) * w, :]
        pk_prev = pkc_ref[:, (r - 1) * w:r * w] if r else pkh_ref[...]
        buckets.append((_rel_bucket(pq - pkc_ref[:, r * w:(r + 1) * w]), _rel_bucket(pq - pk_prev)))
    logits = []
    for r, hh in chains:
        qp = q_ref[r * w:(r + 1) * w, (hh // 2) * LANES:(hh // 2 + 1) * LANES]
        qh = jnp.where(_half_mask(hh % 2), qp, jnp.zeros_like(qp))
        kc, kp = keys(kc_ref, kh_ref, r, hh)
        logits.append((_dot_nt(qh, kc), _dot_nt(qh, kp)))
    masked = []
    for (r, hh), (lc, lp) in zip(chains, logits):
        lc = jnp.where(valid_c, lc + jnp.take_along_axis(tabs[hh], buckets[r][0], axis=1), NEG_BIG)
        lp = lp + jnp.take_along_axis(tabs[hh], buckets[r][1], axis=1)
        lp = jnp.where(valid_p if r else valid_p & (step > 0), lp, NEG_BIG)
        masked.append((lc, lp))
    maxes = [jnp.maximum(jnp.maximum(jnp.max(lc, axis=-1, keepdims=True), jnp.max(lp, axis=-1, keepdims=True)),
                         sink_ref[hh]) for (r, hh), (lc, lp) in zip(chains, masked)]
    probs = [(jnp.exp(lc - m).astype(BF16), jnp.exp(lp - m).astype(BF16)) for (lc, lp), m in zip(masked, maxes)]
    outs = {}
    for (r, hh), (ec, ep), m in zip(chains, probs, maxes):
        vc, vp = keys(vc_ref, vh_ref, r, hh)
        mine = _half_mask(hh % 2)
        acc = _dot(ec, jnp.where(mine, vc, ones)) + _dot(ep, jnp.where(mine, vp, ones))
        den = (acc[:, 0:1] if hh % 2 else acc[:, HEAD_DIM:HEAD_DIM + 1]) + jnp.exp(sink_ref[hh] - m)
        outs[(r, hh)] = acc / den
    for r in range(nsub):
        pairs = [jnp.where(_half_mask(0), outs[(r, 2 * p)], outs[(r, 2 * p + 1)]) for p in range(N_HEADS // 2)]
        o_ref[r * w:(r + 1) * w, :] = jnp.concatenate(pairs, axis=1).astype(o_ref.dtype)


def _swa_attention(q, k, v, positions, sinks, rel_table, batch, seq):
    w = SWA_WINDOW
    tq = SWA_TQ
    per = tq // w
    q3, k3, v3 = (t.reshape(batch, seq, WIDTH) for t in (q, k, v))
    pcol = positions.reshape(batch, seq, 1)
    prow = positions.reshape(batch, 1, seq)
    tab = jnp.zeros((N_HEADS, LANES), F32).at[:, :REL_BUCKETS].set(rel_table.astype(F32).T)
    cur = lambda b, n: (b, n, 0)
    halo = lambda b, n: (b, jnp.maximum(n * per - 1, 0), 0)
    out = pl.pallas_call(
        _swa_kernel,
        grid=(batch, seq // tq),
        in_specs=[pl.BlockSpec(memory_space=pltpu.SMEM), _const_spec((N_HEADS, LANES)),
                  pl.BlockSpec((None, tq, WIDTH), cur),
                  pl.BlockSpec((None, tq, WIDTH), cur), pl.BlockSpec((None, w, WIDTH), halo),
                  pl.BlockSpec((None, tq, WIDTH), cur), pl.BlockSpec((None, w, WIDTH), halo),
                  pl.BlockSpec((None, tq, 1), cur),
                  pl.BlockSpec((None, 1, tq), lambda b, n: (b, 0, n)),
                  pl.BlockSpec((None, 1, w), lambda b, n: (b, 0, jnp.maximum(n * per - 1, 0)))],
        out_specs=pl.BlockSpec((None, tq, WIDTH), cur),
        out_shape=jax.ShapeDtypeStruct((batch, seq, WIDTH), BF16),
        compiler_params=_cparams(("parallel", "arbitrary")),
        name="swa_attention",
    )(sinks.astype(F32), tab, q3, k3, k3, v3, v3, pcol, prow, prow)
    return out.reshape(batch * seq, WIDTH)


def _hgrn_kernel(hg_ref, lb_ref, nw_ref, o_ref, state_ref):
    c = HGRN_CHUNK
    blk = HGRN_BLOCK

    @pl.when(pl.program_id(1) == 0)
    def _():
        state_ref[...] = jnp.zeros_like(state_ref)

    r64 = lax.broadcasted_iota(I32, (c, c), 0)
    c64 = lax.broadcasted_iota(I32, (c, c), 1)
    incl = (c64 <= r64).astype(BF16)
    ra = lax.broadcasted_iota(I32, (WIDTH, WIDTH), 0) // HEAD_DIM
    ca = lax.broadcasted_iota(I32, (WIDTH, WIDTH), 1) // HEAD_DIM
    same_head = ra == ca
    seg = same_head.astype(BF16)
    ones_cols = jnp.ones((c, LANES), BF16)
    trow = lax.broadcasted_iota(I32, (blk, WIDTH), 0)
    caps = [jnp.where(trow >= s_i, 0.0, NEG_BIG) for s_i in range(blk)]
    lane_head = lax.broadcasted_iota(I32, (1, WIDTH), 1) // HEAD_DIM
    lb = lb_ref[...]
    nw = nw_ref[...]
    dn0 = (((0,), (0,)), ((), ()))

    for ch in range(hg_ref.shape[0] // c):
        rows = slice(ch * c, (ch + 1) * c)
        qraw = hg_ref[rows, 0:WIDTH]
        fraw = hg_ref[rows, WIDTH:2 * WIDTH]
        v = hg_ref[rows, 2 * WIDTH:3 * WIDTH]
        graw = hg_ref[rows, 3 * WIDTH:4 * WIDTH]
        qf = qraw * jax.nn.sigmoid(qraw)
        forget = lb + (1.0 - lb) * jax.nn.sigmoid(fraw)
        lf = jnp.log(forget)
        kk = 1.0 - forget
        gate = graw * jax.nn.sigmoid(graw)
        vb = v.astype(BF16)

        lf3 = _split3(lf)
        bc = _dot(incl, lf3[0]) + _dot(incl, lf3[1]) + _dot(incl, lf3[2])
        b_last = bc[c - 1:c, :]
        tot_col = sum(lax.dot_general(t, ones_cols, dn0, preferred_element_type=F32) for t in lf3)
        decay_col = jnp.exp(jnp.concatenate([tot_col, tot_col], axis=1))

        state = state_ref[...]
        o_inter = _dot((qf * jnp.exp(bc)).astype(BF16), state.astype(BF16))

        def before(qa, qb, ka, kb):
            ref = bc[kb - 1:kb, :]
            qt = qf[qa:qb] * jnp.exp(bc[qa:qb] - ref)
            kt = (kk[ka:kb] * jnp.exp(ref - bc[ka:kb])).astype(BF16)
            qs = jnp.concatenate([jnp.where(lane_head == hh, qt, 0.0) for hh in range(N_HEADS)], axis=0)
            att = _dot_nt(qs.astype(BF16), kt)
            mix = _dot(att.astype(BF16), vb[ka:kb])
            nq = qb - qa
            return sum(jnp.where(lane_head == hh, mix[hh * nq:(hh + 1) * nq], 0.0) for hh in range(N_HEADS))

        bc2 = bc * LOG2E

        def inside(a):
            b2 = bc2[a:a + blk]
            qb_ = qf[a:a + blk]
            ws = []
            for s_i in range(blk):
                e = jnp.exp2(jnp.minimum(b2 - b2[s_i:s_i + 1, :], caps[s_i]))
                ws.append(qb_ * kk[a + s_i:a + s_i + 1, :] * e)
            att = _dot(jnp.concatenate(ws, axis=0).astype(BF16), seg)
            return sum(att[s_i * blk:(s_i + 1) * blk] * v[a + s_i:a + s_i + 1, :] for s_i in range(blk))

        pieces = {a: [] for a in range(0, c, blk)}

        def cover(a, b):
            if b - a == blk:
                pieces[a].append(inside(a))
                return
            mid = (a + b) // 2
            cover(a, mid)
            cover(mid, b)
            res = before(mid, b, a, mid)
            for off in range(0, b - mid, blk):
                pieces[mid + off].append(res[off:off + blk])

        cover(0, c)
        o = o_inter + jnp.concatenate([sum(pieces[a]) for a in range(0, c, blk)], axis=0)

        khat = (kk * jnp.exp(b_last - bc)).astype(BF16)
        upd = lax.dot_general(khat, vb, dn0, preferred_element_type=F32)
        state_ref[...] = decay_col * state + jnp.where(same_head, upd, 0.0)

        o2 = _split3(o * o)
        ms = (_dot(o2[0], seg) + _dot(o2[1], seg)) * (1.0 / HEAD_DIM)
        o_ref[rows, :] = (o * lax.rsqrt(ms + EPS) * nw * gate).astype(o_ref.dtype)


def _hgrn(hg, lower_bound, norm_w, batch, seq):
    rows = HG_ROWS
    hg3 = hg.reshape(batch, seq, 4 * WIDTH)
    out = pl.pallas_call(
        _hgrn_kernel,
        grid=(batch, seq // rows),
        in_specs=[pl.BlockSpec((None, rows, 4 * WIDTH), lambda b, i: (b, i, 0)),
                  _const_spec((1, WIDTH)), _const_spec((1, WIDTH))],
        out_specs=pl.BlockSpec((None, rows, WIDTH), lambda b, i: (b, i, 0)),
        out_shape=jax.ShapeDtypeStruct((batch, seq, WIDTH), BF16),
        scratch_shapes=[pltpu.VMEM((WIDTH, WIDTH), F32)],
        compiler_params=_cparams(("parallel", "arbitrary")),
        name="hgrn2",
    )(hg3, lower_bound.reshape(1, WIDTH).astype(F32), norm_w.reshape(1, WIDTH).astype(F32))
    return out.reshape(batch * seq, WIDTH)


def _merge_body(x, y_refs, wg_ref, wb_ref, wo_ref, g, b):
    xb = x.astype(BF16)
    merged = jnp.zeros(x.shape, F32)
    for nbr, y_ref in enumerate(y_refs):
        gate = jax.nn.sigmoid(_dot(xb, wg_ref[:, nbr * D_MODEL:(nbr + 1) * D_MODEL]))
        merged = merged + gate * _dot(y_ref[...], wb_ref[nbr])
    y = _dot(merged.astype(BF16), wo_ref[...])
    return _layernorm(ALPHA * x + y, g, b)


def _memkv_kernel(m_ref, w_ref, k_ref, v_ref):
    kv = _dot(m_ref[...].astype(BF16), w_ref[...])
    k_ref[...] = kv[:, :WIDTH].astype(BF16)
    v_ref[...] = kv[:, WIDTH:].astype(BF16)


def _memkv(mem, wkv):
    batch, m, _ = mem.shape
    return pl.pallas_call(
        _memkv_kernel,
        grid=(batch,),
        in_specs=[pl.BlockSpec((None, m, D_MODEL), lambda b: (b, 0, 0)), _const_spec(wkv.shape)],
        out_specs=[pl.BlockSpec((None, m, WIDTH), lambda b: (b, 0, 0))] * 2,
        out_shape=[jax.ShapeDtypeStruct((batch, m, WIDTH), BF16)] * 2,
        compiler_params=_cparams(("parallel",)),
        name="mem_kv",
    )(mem, wkv)


def _xattn_body(x, wq_ref, k, v, wo_ref, g, b):
    q = _dot(x.astype(BF16), wq_ref[...]).astype(BF16)
    lane = lax.broadcasted_iota(I32, (1, WIDTH), 1) // HEAD_DIM
    heads = range(N_HEADS)
    ss = [_dot_nt(jnp.where(lane == hh, q, jnp.zeros_like(q)), k) for hh in heads]
    es = [jnp.exp(s - jnp.max(s, axis=-1, keepdims=True)) for s in ss]
    ps = [(e / jnp.sum(e, axis=-1, keepdims=True)).astype(BF16) for e in es]
    o = jnp.zeros((x.shape[0], WIDTH), F32)
    for hh in heads:
        o = o + jnp.where(lane == hh, _dot(ps[hh], v), 0.0)
    y = _dot(o.astype(BF16), wo_ref[...])
    return _layernorm(ALPHA * x + y, g, b)


def _merge_xattn_kernel(x_ref, y0_ref, y1_ref, y2_ref, y3_ref, wg_ref, wb_ref, wo_ref, g1_ref, b1_ref,
                        wq_ref, k_ref, v_ref, xwo_ref, g2_ref, b2_ref, o_ref):
    x1 = _merge_body(x_ref[...], (y0_ref, y1_ref, y2_ref, y3_ref), wg_ref, wb_ref, wo_ref,
                     g1_ref[...], b1_ref[...])
    o_ref[...] = _xattn_body(x1, wq_ref, k_ref[...], v_ref[...], xwo_ref, g2_ref[...], b2_ref[...])


def _merge_xattn(x2d, ys, wg, wb, wo, g1, b1, wq, k, v, xwo, g2, b2, batch, seq):
    tm = TM_A
    m = k.shape[1]
    per = seq // tm
    row = lambda w: pl.BlockSpec((tm, w), lambda bb, i: (bb * per + i, 0))
    kv_spec = pl.BlockSpec((None, m, WIDTH), lambda bb, i: (bb, 0, 0))
    vec = _const_spec((1, D_MODEL))
    return pl.pallas_call(
        _merge_xattn_kernel,
        grid=(batch, per),
        in_specs=[row(D_MODEL)] + [row(WIDTH)] * 4 +
                 [_const_spec(wg.shape), _const_spec(wb.shape), _const_spec(wo.shape), vec, vec,
                  _const_spec(wq.shape), kv_spec, kv_spec, _const_spec(xwo.shape), vec, vec],
        out_specs=row(D_MODEL),
        out_shape=jax.ShapeDtypeStruct((batch * seq, D_MODEL), F32),
        compiler_params=_cparams(("parallel", "parallel")),
        name="merge_xattn_ln",
    )(x2d, *ys, wg, wb, wo, g1.reshape(1, -1), b1.reshape(1, -1),
      wq, k, v, xwo, g2.reshape(1, -1), b2.reshape(1, -1))


def _ffn_kernel(x_ref, w13_ref, w2_ref, g_ref, b_ref, o_ref):
    x = x_ref[...]
    xb = x.astype(BF16)
    y = None
    for h in range(F_DENSE // TF_FFN):
        lo = h * TF_FFN
        a = _dot(xb, w13_ref[:, lo:lo + TF_FFN])
        gate = _dot(xb, w13_ref[:, F_DENSE + lo:F_DENSE + lo + TF_FFN])
        part = _dot((a * jax.nn.sigmoid(a) * gate).astype(BF16), w2_ref[lo:lo + TF_FFN, :])
        y = part if y is None else y + part
    o_ref[...] = _layernorm(ALPHA * x + y, g_ref[...], b_ref[...])


def _ffn(x2d, w13, w2, g, b):
    n = x2d.shape[0]
    tm = TM_FFN
    return pl.pallas_call(
        _ffn_kernel,
        grid=(n // tm,),
        in_specs=[pl.BlockSpec((tm, D_MODEL), lambda i: (i, 0)),
                  _const_spec(w13.shape), _const_spec(w2.shape),
                  _const_spec((1, D_MODEL)), _const_spec((1, D_MODEL))],
        out_specs=pl.BlockSpec((tm, D_MODEL), lambda i: (i, 0)),
        out_shape=jax.ShapeDtypeStruct((n, D_MODEL), F32),
        compiler_params=_cparams(("parallel",)),
        name="ffn_ln",
    )(x2d, w13, w2, g.reshape(1, -1), b.reshape(1, -1))


def _router_kernel(x_ref, r_ref, info_ref, wts_ref, cnt_ref, carry_ref):
    tm = x_ref.shape[0]

    @pl.when(pl.program_id(0) == 0)
    def _():
        carry_ref[...] = jnp.zeros_like(carry_ref)

    logits = jnp.dot(x_ref[...], r_ref[...], precision=lax.Precision.HIGHEST, preferred_element_type=F32)
    lane = lax.broadcasted_iota(I32, (tm, LANES), 1)
    lg = jnp.where(lane < N_EXPERTS, logits, -jnp.inf)
    m1 = jnp.max(lg, axis=-1, keepdims=True)
    i1 = jnp.min(jnp.where(lg == m1, lane, LANES), axis=-1, keepdims=True)
    lg2 = jnp.where(lane == i1, -jnp.inf, lg)
    m2 = jnp.max(lg2, axis=-1, keepdims=True)
    i2 = jnp.min(jnp.where(lg2 == m2, lane, LANES), axis=-1, keepdims=True)
    e = jnp.exp(m2 - m1)
    w1 = 1.0 / (1.0 + e)
    w2 = e / (1.0 + e)
    sel1 = lane == i1
    sel2 = lane == i2
    chosen = jnp.where(sel1 | sel2, 1.0, 0.0)
    row = lax.broadcasted_iota(I32, (tm, tm), 0)
    col = lax.broadcasted_iota(I32, (tm, tm), 1)
    before = (col < row).astype(BF16)
    ranks = _dot(before, chosen.astype(BF16)) + carry_ref[...]
    r1 = jnp.sum(jnp.where(sel1, ranks, 0.0), axis=-1, keepdims=True).astype(I32)
    r2 = jnp.sum(jnp.where(sel2, ranks, 0.0), axis=-1, keepdims=True).astype(I32)
    carry_ref[...] = carry_ref[...] + jnp.sum(chosen, axis=0, keepdims=True)
    info_ref[...] = jnp.where(lane == 0, i1, jnp.where(lane == 1, i2, jnp.where(lane == 2, r1,
                              jnp.where(lane == 3, r2, 0))))
    wts_ref[...] = jnp.where(lane == 0, w1, jnp.where(lane == 1, w2, 0.0))
    cnt_ref[...] = carry_ref[...]


def _router(x2d, router):
    n = x2d.shape[0]
    tm = TM_A
    r_pad = jnp.zeros((D_MODEL, LANES), F32).at[:, :N_EXPERTS].set(router.astype(F32))
    row = pl.BlockSpec((tm, LANES), lambda i: (i, 0))
    return pl.pallas_call(
        _router_kernel,
        grid=(n // tm,),
        in_specs=[pl.BlockSpec((tm, D_MODEL), lambda i: (i, 0)), _const_spec(r_pad.shape)],
        out_specs=[row, row, pl.BlockSpec((1, LANES), lambda i: (0, 0))],
        out_shape=[jax.ShapeDtypeStruct((n, LANES), I32), jax.ShapeDtypeStruct((n, LANES), F32),
                   jax.ShapeDtypeStruct((1, LANES), F32)],
        scratch_shapes=[pltpu.VMEM((1, LANES), F32)],
        compiler_params=_cparams(("arbitrary",)),
        name="moe_router",
    )(x2d, r_pad)


def _dispatch_kernel(pad_ref, dest_ref, x_ref, xb_hbm, stage_ref, sems):
    tm = x_ref.shape[0]
    i = pl.program_id(0)
    last = pl.num_programs(0) - 1
    slot = i % 2

    def wait_step(s):
        for _ in range(2):
            pltpu.make_async_copy(stage_ref.at[s], xb_hbm.at[pl.ds(0, tm), :], sems.at[s]).wait()

    @pl.when(i >= 2)
    def _():
        wait_step(slot)

    for s in range(2):
        @pl.when(slot == s)
        def _():
            stage_ref[s] = x_ref[...]

            def issue(r, c):
                for k in range(2):
                    pltpu.make_async_copy(stage_ref.at[s, pl.ds(r, 1), :],
                                          xb_hbm.at[pl.ds(dest_ref[0, 2 * r + k], 1), :], sems.at[s]).start()
                return c
            lax.fori_loop(0, tm, issue, 0, unroll=8)

    @pl.when(i == last)
    def _():
        def fill(e, c):
            def one(s, c2):
                pltpu.make_async_copy(stage_ref.at[slot, pl.ds(0, 1), :], xb_hbm.at[pl.ds(s, 1), :],
                                      sems.at[2]).start()
                return c2

            def done(s, c2):
                pltpu.make_async_copy(stage_ref.at[slot, pl.ds(0, 1), :], xb_hbm.at[pl.ds(0, 1), :],
                                      sems.at[2]).wait()
                return c2
            lax.fori_loop(pad_ref[0, e], pad_ref[1, e], one, 0)
            lax.fori_loop(pad_ref[0, e], pad_ref[1, e], done, 0)
            return c
        lax.fori_loop(0, pad_ref.shape[1], fill, 0)
        wait_step(slot)

        @pl.when(last >= 1)
        def _():
            wait_step(1 - slot)


def _dispatch(x2d, dest, pads, nblk):
    n = x2d.shape[0]
    tm = TM_DISP
    nt = n // tm
    grid_spec = pltpu.PrefetchScalarGridSpec(
        num_scalar_prefetch=1,
        grid=(nt,),
        in_specs=[pl.BlockSpec((None, 1, 2 * tm), lambda i, pads: (i, 0, 0), memory_space=pltpu.SMEM),
                  pl.BlockSpec((tm, D_MODEL), lambda i, pads: (i, 0))],
        out_specs=pl.BlockSpec(memory_space=pl.ANY),
        scratch_shapes=[pltpu.VMEM((2, tm, D_MODEL), F32), pltpu.SemaphoreType.DMA((3,))],
    )
    return pl.pallas_call(
        _dispatch_kernel,
        grid_spec=grid_spec,
        out_shape=jax.ShapeDtypeStruct((nblk * MOE_TB, D_MODEL), F32),
        compiler_params=_cparams(("arbitrary",), disable_bounds_checks=True),
        name="moe_dispatch",
    )(pads, dest.reshape(nt, 1, 2 * tm), x2d)


def _expert_kernel(nused_ref, bexp_ref, x_ref, w1_ref, w3_ref, w2_ref, o_ref, acc_ref):
    f = pl.program_id(1)

    @pl.when(pl.program_id(0) < nused_ref[0])
    def _():
        xb = x_ref[...].astype(BF16)
        a = _dot(xb, w1_ref[...])
        gate = _dot(xb, w3_ref[...])
        part = _dot((a * jax.nn.sigmoid(a) * gate).astype(BF16), w2_ref[...])

        @pl.when(f == 0)
        def _():
            acc_ref[...] = part

        @pl.when(f > 0)
        def _():
            acc_ref[...] += part

        @pl.when(f == pl.num_programs(1) - 1)
        def _():
            o_ref[...] = acc_ref[...]

    @pl.when(pl.program_id(0) >= nused_ref[0])
    def _():
        o_ref[...] = jnp.zeros_like(o_ref)


def _experts(xb, w13, w2, nused, blk_exp, nblk):
    tb, tf = MOE_TB, MOE_TF
    nf = F_EXPERT // tf
    w13t = w13.reshape(N_EXPERTS, D_MODEL, 2 * nf, tf).transpose(0, 2, 1, 3).astype(BF16)

    def blk(i, nu):
        return jnp.maximum(jnp.minimum(i, nu[0] - 1), 0)

    def ftile(i, f, nu):
        return jnp.where(i < nu[0], f, nf - 1)

    grid_spec = pltpu.PrefetchScalarGridSpec(
        num_scalar_prefetch=2,
        grid=(nblk, nf),
        in_specs=[pl.BlockSpec((tb, D_MODEL), lambda i, f, nu, be: (blk(i, nu), 0)),
                  pl.BlockSpec((None, None, D_MODEL, tf),
                               lambda i, f, nu, be: (be[blk(i, nu)], ftile(i, f, nu), 0, 0)),
                  pl.BlockSpec((None, None, D_MODEL, tf),
                               lambda i, f, nu, be: (be[blk(i, nu)], nf + ftile(i, f, nu), 0, 0)),
                  pl.BlockSpec((None, tf, D_MODEL), lambda i, f, nu, be: (be[blk(i, nu)], ftile(i, f, nu), 0))],
        out_specs=pl.BlockSpec((tb, D_MODEL), lambda i, f, nu, be: (i, 0)),
        scratch_shapes=[pltpu.VMEM((tb, D_MODEL), F32)],
    )
    return pl.pallas_call(
        _expert_kernel,
        grid_spec=grid_spec,
        out_shape=jax.ShapeDtypeStruct((nblk * tb, D_MODEL), F32),
        compiler_params=_cparams(("arbitrary", "arbitrary")),
        name="moe_experts",
    )(nused, blk_exp, xb, w13t, w13t, w2.astype(BF16))


def _combine_kernel(dest_ref, nxt_ref, y_hbm, x_ref, wts_ref, g_ref, b_ref, o_ref, buf_ref, sems):
    tm = x_ref.shape[0]
    i = pl.program_id(0)
    slot = i % 2

    def gather(idx_ref, s):
        def issue(r, c):
            for k in range(2):
                pltpu.make_async_copy(y_hbm.at[pl.ds(idx_ref[0, 2 * r + k], 1), :],
                                      buf_ref.at[s, k, pl.ds(r, 1), :], sems.at[s]).start()
            return c
        lax.fori_loop(0, tm, issue, 0, unroll=8)

    @pl.when(i == 0)
    def _():
        gather(dest_ref, 0)

    for s in range(2):
        @pl.when((i + 1 < pl.num_programs(0)) & (slot != s))
        def _():
            gather(nxt_ref, s)

    for k in range(2):
        pltpu.make_async_copy(y_hbm.at[pl.ds(0, tm), :], buf_ref.at[slot, k], sems.at[slot]).wait()
    wts = wts_ref[...]
    y = wts[:, 0:1] * buf_ref[slot, 0] + wts[:, 1:2] * buf_ref[slot, 1]
    o_ref[...] = _layernorm(ALPHA * x_ref[...] + y, g_ref[...], b_ref[...])


def _combine(yb, dest, x2d, wts, g, b):
    n = x2d.shape[0]
    tm = TM_COMB
    nt = n // tm
    row = lambda w: pl.BlockSpec((tm, w), lambda i: (i, 0))
    dest3 = dest.reshape(nt, 1, 2 * tm)
    return pl.pallas_call(
        _combine_kernel,
        grid=(nt,),
        in_specs=[pl.BlockSpec((None, 1, 2 * tm), lambda i: (i, 0, 0), memory_space=pltpu.SMEM),
                  pl.BlockSpec((None, 1, 2 * tm), lambda i: (jnp.minimum(i + 1, nt - 1), 0, 0),
                               memory_space=pltpu.SMEM),
                  pl.BlockSpec(memory_space=pl.ANY), row(D_MODEL), row(LANES),
                  _const_spec((1, D_MODEL)), _const_spec((1, D_MODEL))],
        out_specs=row(D_MODEL),
        out_shape=jax.ShapeDtypeStruct((n, D_MODEL), F32),
        scratch_shapes=[pltpu.VMEM((2, 2, tm, D_MODEL), F32), pltpu.SemaphoreType.DMA((2,))],
        compiler_params=_cparams(("arbitrary",), disable_bounds_checks=True),
        name="moe_combine_ln",
    )(dest3, dest3, yb, x2d, wts, g.reshape(1, -1), b.reshape(1, -1))


def _moe(x2d, router, w13, w2, g, b):
    n = x2d.shape[0]
    tb = MOE_TB
    info, wts, cnt = _router(x2d, router)
    idx = info[:, 0:2]
    rank = info[:, 2:4]
    counts = cnt[0, :N_EXPERTS].astype(I32)
    padded = (counts + tb - 1) // tb * tb
    pend = jnp.cumsum(padded)
    pstart = pend - padded
    dest = (pstart[idx] + rank).astype(I32).reshape(-1)
    nblk = -(-(2 * n + N_EXPERTS * (tb - 1)) // tb)
    pads = jnp.stack([jnp.append(pstart + counts, pend[-1]), jnp.append(pend, nblk * tb)]).astype(I32)
    nused = (pend[-1] // tb).astype(I32).reshape(1)
    first_row = jnp.arange(nblk, dtype=I32) * tb
    blk_exp = jnp.minimum(jnp.sum(pend[None, :] <= first_row[:, None], axis=1), N_EXPERTS - 1).astype(I32)
    xb = _dispatch(x2d, dest, pads, nblk)
    yb = _experts(xb, w13, w2, nused, blk_exp, nblk)
    return _combine(yb, dest, x2d, wts, g, b)


def kernel(x, mem, positions, rel_bias_table, hgrn_lb_logits, w_in, mla_q_norm, mla_w_uq, mla_kv_norm, mla_w_ukv, swa_sinks, hgrn_norm, w_branch, w_out, ln_g, ln_b, xa_wq, xa_wkv, xa_wo, ffn_w13, ffn_w2, moe_router, moe_w13, moe_w2):
    batch, seq, _ = x.shape
    n = batch * seq
    sm = jax.nn.softmax(hgrn_lb_logits.astype(F32), axis=0)
    lower_bounds = jnp.cumsum(sm, axis=0) - sm[0]
    ctab, stab = _rope_tables(positions)
    xc = x.reshape(n, D_MODEL)
    for l in range(DEPTH):
        wts = _inproj_weights(w_in[l], mla_w_uq[l], mla_w_ukv[l])
        mq, mk, mv, swq, swk, swv, hg, sbq, sbk, sbv = _inproj(xc, wts, ctab, stab, mla_q_norm[l], mla_kv_norm[l])
        y_mla = _mla_attention(mq, mk, mv, batch, seq)
        y_swa = _swa_attention(swq, swk, swv, positions, swa_sinks[l], rel_bias_table, batch, seq)
        y_hg = _hgrn(hg, lower_bounds[l], hgrn_norm[l], batch, seq)
        y_sb = _sb_attention(sbq, sbk, sbv, batch, seq)
        go = _IN_OFF['gates']
        mk_, mv_ = _memkv(mem, xa_wkv[l].astype(BF16))
        xc = _merge_xattn(xc, (y_mla, y_swa, y_hg, y_sb), w_in[l][:, go:].astype(BF16), w_branch[l].astype(BF16),
                          w_out[l].astype(BF16), ln_g[l, 0], ln_b[l, 0],
                          (xa_wq[l] * QK_SCALE).astype(BF16), mk_, mv_, xa_wo[l].astype(BF16),
                          ln_g[l, 1], ln_b[l, 1], batch, seq)
        if l % 2 == 0:
            xc = _ffn(xc, ffn_w13[l // 2].astype(BF16), ffn_w2[l // 2].astype(BF16), ln_g[l, 2], ln_b[l, 2])
        else:
            xc = _moe(xc, moe_router[l // 2], moe_w13[l // 2], moe_w2[l // 2],
                      ln_g[l, 2], ln_b[l, 2])
    return xc.reshape(batch, seq, D_MODEL)
```

```python
import functools
import math

import jax
import jax.numpy as jnp
from jax import lax
from jax.experimental import pallas as pl
from jax.experimental.pallas import tpu as pltpu

F32 = jnp.float32
BF16 = jnp.bfloat16
I32 = jnp.int32

D_MODEL = 1024
DEPTH = 2
EPS = 1e-5
NEG_BIG = -1e30
LANES = 128
HEAD_DIM = 64
N_HEADS = 4
WIDTH = N_HEADS * HEAD_DIM

MLA_Q_LORA = 256
MLA_KV_LORA = 128
MLA_NOPE = 64
MLA_ROPE = 32
ROPE_THETA = 10000.0
MLA_SCALE = (MLA_NOPE + MLA_ROPE) ** -0.5
LOG2E = math.log2(math.e)
QK_SCALE = HEAD_DIM ** -0.5

SB_RUN_FLOOR = -150.0
SWA_WINDOW = 128
REL_BUCKETS = 32
REL_MAX_DIST = 128
HGRN_CHUNK = 64
HGRN_BLOCK = 16
N_EXPERTS = 8
F_DENSE = 2816
F_EXPERT = 3584
ALPHA = (2 * DEPTH) ** 0.25

_IN_SPLITS = (('mla_cq', 256), ('mla_ckv', 128), ('mla_kr', 32), ('swa_q', 256), ('swa_k', 128),
              ('swa_v', 128), ('hgrn', 1024), ('sb_q', 256), ('sb_k', 256), ('sb_v', 256), ('gates', 4096))
_IN_OFF = {}
_o = 0
for _n, _w in _IN_SPLITS:
    _IN_OFF[_n] = _o
    _o += _w

_A_SPLITS = (('cq', 256), ('ckv', 128), ('kra', 128), ('krb', 128), ('swa_q', 256), ('swa_k', 256),
             ('swa_v', 256), ('hgrn', 1024), ('sb_q', 256), ('sb_k', 256), ('sb_v', 256))
_A_OFF = {}
_o = 0
for _n, _w in _A_SPLITS:
    _A_OFF[_n] = (_o, _o + _w)
    _o += _w
A_COLS = _o

TM_A = 512
TQ_ATT = 256
MLA_TQ = 512
MLA_TK = 512
MLA_WIDE = 4
MLA_GROUP = 4
SWA_TQ = 512
HG_ROWS = 256
TM_FFN = 512
TF_FFN = 1408
MOE_TB = 512
MOE_TF = 1792
TM_COMB = 256
TM_DISP = 512
VMEM_LIMIT = 56 * 1024 * 1024


def _cparams(sem, **kw):
    return pltpu.CompilerParams(dimension_semantics=sem, vmem_limit_bytes=VMEM_LIMIT, **kw)


def _const_spec(shape):
    nd = len(shape)
    return pl.BlockSpec(shape, lambda *_: (0,) * nd, pipeline_mode=pl.Buffered(1))


def _layernorm(v, g, b):
    mu = jnp.mean(v, axis=-1, keepdims=True)
    vc = v - mu
    var = jnp.mean(vc * vc, axis=-1, keepdims=True)
    return vc * lax.rsqrt(var + EPS) * g + b


def _dot(a, b):
    return jnp.dot(a, b, preferred_element_type=F32)


def _dot_nt(a, b):
    return lax.dot_general(a, b, (((1,), (1,)), ((), ())), preferred_element_type=F32)


def _split3(a):
    hi = a.astype(BF16)
    r = a - hi.astype(F32)
    mid = r.astype(BF16)
    lo = (r - mid.astype(F32)).astype(BF16)
    return hi, mid, lo


def _rope_kernel(pos_ref, freq_ref, c_ref, s_ref):
    lane = lax.broadcasted_iota(I32, pos_ref.shape, 1)
    ang = pos_ref[...] * freq_ref[...]
    rope = (lane >= MLA_NOPE) & (lane < MLA_NOPE + MLA_ROPE)
    first = lane < MLA_NOPE + MLA_ROPE // 2
    c_ref[...] = jnp.where(lane < MLA_NOPE, 1.0, jnp.where(rope, jnp.cos(ang), 0.0))
    sn = jnp.sin(ang)
    s_ref[...] = jnp.where(rope, jnp.where(first, -sn, sn), 0.0)


def _rope_tables(positions):
    n = positions.size
    half = MLA_ROPE // 2
    inv_freq = ROPE_THETA ** (-jnp.arange(half, dtype=F32) / half)
    freq = jnp.zeros((1, LANES), F32).at[0, MLA_NOPE:MLA_NOPE + MLA_ROPE].set(jnp.tile(inv_freq, 2))
    posb = jnp.broadcast_to(positions.reshape(n, 1).astype(F32), (n, LANES))
    tm = 1024
    return pl.pallas_call(
        _rope_kernel,
        grid=(n // tm,),
        in_specs=[pl.BlockSpec((tm, LANES), lambda i: (i, 0)), _const_spec((1, LANES))],
        out_specs=[pl.BlockSpec((tm, LANES), lambda i: (i, 0))] * 2,
        out_shape=[jax.ShapeDtypeStruct((n, LANES), F32)] * 2,
        compiler_params=_cparams(("parallel",)),
        name="rope_tables",
    )(posb, freq)


def _inproj_kernel(x_ref, w_ref, c_ref, s_ref, qn_ref, kvn_ref, wuqa_ref, wuqb_ref, wuk_ref, wuv_ref,
                   mq_ref, mk_ref, mv_ref, swq_ref, swk_ref, swv_ref, hg_ref, sbq_ref, sbk_ref, sbv_ref):
    h = _dot(x_ref[...].astype(BF16), w_ref[...])

    def cols(name):
        lo, hi = _A_OFF[name]
        return h[:, lo:hi]

    c = c_ref[...]
    s = s_ref[...]
    c4 = jnp.concatenate([c] * N_HEADS, axis=1)
    s4 = jnp.concatenate([s] * N_HEADS, axis=1)

    cq = cols('cq')
    cqn = (cq * lax.rsqrt(jnp.mean(cq * cq, axis=-1, keepdims=True) + EPS) * qn_ref[...]).astype(BF16)
    q = _dot(cqn, wuqa_ref[...]) * c4 + _dot(cqn, wuqb_ref[...]) * s4
    mq_ref[...] = (q * (MLA_SCALE * LOG2E)).astype(BF16)

    ckv = cols('ckv')
    ckvn = (ckv * lax.rsqrt(jnp.mean(ckv * ckv, axis=-1, keepdims=True) + EPS) * kvn_ref[...]).astype(BF16)
    krot = cols('kra') * c + cols('krb') * s
    mk_ref[...] = (_dot(ckvn, wuk_ref[...]) + jnp.concatenate([krot] * N_HEADS, axis=1)).astype(BF16)
    mv_ref[...] = _dot(ckvn, wuv_ref[...]).astype(BF16)

    swq_ref[...] = cols('swa_q').astype(BF16)
    swk_ref[...] = cols('swa_k').astype(BF16)
    swv_ref[...] = cols('swa_v').astype(BF16)
    hg_ref[...] = cols('hgrn')
    sbq_ref[...] = cols('sb_q').astype(BF16)
    sbk_ref[...] = cols('sb_k').astype(BF16)
    sbv_ref[...] = cols('sb_v').astype(BF16)


def _inproj_weights(w_in, w_uq, w_ukv):
    def seg(name, width):
        o = _IN_OFF[name]
        return w_in[:, o:o + width]

    kr = seg('mla_kr', MLA_ROPE)
    half = MLA_ROPE // 2
    z64 = jnp.zeros((D_MODEL, MLA_NOPE), F32)
    z32 = jnp.zeros((D_MODEL, LANES - MLA_NOPE - MLA_ROPE), F32)
    kra = jnp.concatenate([z64, kr, z32], axis=1)
    krb = jnp.concatenate([z64, kr[:, half:], kr[:, :half], z32], axis=1)
    swk = seg('swa_k', 128)
    swv = seg('swa_v', 128)
    dup = lambda t: jnp.concatenate([t[:, :64], t[:, :64], t[:, 64:], t[:, 64:]], axis=1)
    w_a = jnp.concatenate([
        seg('mla_cq', 256), seg('mla_ckv', 128), kra, krb,
        seg('swa_q', 256) * QK_SCALE, dup(swk), dup(swv),
        seg('hgrn', 1024), seg('sb_q', 256) * (QK_SCALE * LOG2E), seg('sb_k', 256), seg('sb_v', 256)], axis=1)

    qd = MLA_NOPE + MLA_ROPE
    zq = jnp.zeros((MLA_Q_LORA, LANES - qd), F32)
    zn = jnp.zeros((MLA_Q_LORA, MLA_NOPE), F32)
    qa, qb = [], []
    for hh in range(N_HEADS):
        nope = w_uq[:, hh * qd: hh * qd + MLA_NOPE]
        rope = w_uq[:, hh * qd + MLA_NOPE: (hh + 1) * qd]
        qa += [nope, rope, zq]
        qb += [zn, rope[:, half:], rope[:, :half], zq]
    wuqa = jnp.concatenate(qa, axis=1)
    wuqb = jnp.concatenate(qb, axis=1)
    lane = jnp.arange(N_HEADS * LANES) % LANES
    wuk = jnp.where(lane[None, :] < MLA_NOPE, w_ukv, 0.0)
    wuv = jnp.concatenate([w_ukv[:, hh * LANES + MLA_NOPE:(hh + 1) * LANES] for hh in range(N_HEADS)], axis=1)
    return tuple(t.astype(BF16) for t in (w_a, wuqa, wuqb, wuk, wuv))


def _inproj(x2d, wts, ctab, stab, q_norm, kv_norm):
    n = x2d.shape[0]
    w_a, wuqa, wuqb, wuk, wuv = wts
    tm = TM_A
    row = lambda w: pl.BlockSpec((tm, w), lambda i: (i, 0))
    out_w = (512, 512, 256, 256, 256, 256, 1024, 256, 256, 256)
    out_dt = (BF16, BF16, BF16, BF16, BF16, BF16, F32, BF16, BF16, BF16)
    return pl.pallas_call(
        _inproj_kernel,
        grid=(n // tm,),
        in_specs=[row(D_MODEL), _const_spec(w_a.shape), row(LANES), row(LANES),
                  _const_spec((1, MLA_Q_LORA)), _const_spec((1, MLA_KV_LORA)),
                  _const_spec(wuqa.shape), _const_spec(wuqb.shape), _const_spec(wuk.shape),
                  _const_spec(wuv.shape)],
        out_specs=[row(w) for w in out_w],
        out_shape=[jax.ShapeDtypeStruct((n, w), d) for w, d in zip(out_w, out_dt)],
        compiler_params=_cparams(("parallel",)),
        name="inproj",
    )(x2d, w_a, ctab, stab, q_norm.reshape(1, -1), kv_norm.reshape(1, -1), wuqa, wuqb, wuk, wuv)


def _half_mask(half):
    lane = lax.broadcasted_iota(I32, (1, LANES), 1)
    return (lane < HEAD_DIM) if half == 0 else (lane >= HEAD_DIM)


def _mla_kernel(q_ref, k_ref, v_ref, o_ref):
    tq = q_ref.shape[0]
    tk = MLA_TK
    nsub = tq // tk
    i = pl.program_id(1)
    row = lax.broadcasted_iota(I32, (tq, tk), 0)
    col = lax.broadcasted_iota(I32, (tq, tk), 1)
    ones = jnp.ones((1, LANES), BF16)

    def update(off, carry, heads, mask, width=tk):
        ss = [_dot_nt(q_ref[:, hh * LANES:(hh + 1) * LANES],
                      k_ref[pl.ds(off, width), hh * LANES:(hh + 1) * LANES]) for hh in heads]
        if mask is not None:
            ss = [jnp.where(mask, s, NEG_BIG) for s in ss]
        ms = [jnp.maximum(c[0], jnp.max(s, axis=-1, keepdims=True)) for c, s in zip(carry, ss)]
        pms = [jnp.exp2(s - m).astype(BF16) for s, m in zip(ss, ms)]
        new = []
        for n, hh in enumerate(heads):
            vb = v_ref[pl.ds(off, width), (hh // 2) * LANES:(hh // 2 + 1) * LANES]
            vb = jnp.where(_half_mask(hh % 2), vb, ones)
            m, acc = carry[n]
            new.append((ms[n], jnp.exp2(m - ms[n]) * acc + _dot(pms[n], vb)))
        return tuple(new)

    accs = []
    for g in range(0, N_HEADS, MLA_GROUP):
        heads = tuple(range(g, g + MLA_GROUP))
        init = tuple((jnp.full((tq, 1), NEG_BIG, F32), jnp.zeros((tq, LANES), F32)) for _ in heads)
        nkb = i * nsub
        wide = MLA_WIDE * tk
        carry = lax.fori_loop(
            0, nkb // MLA_WIDE,
            lambda j, c, heads=heads: update(pl.multiple_of(j * wide, wide), c, heads, None, wide), init)
        done = nkb // MLA_WIDE * MLA_WIDE
        rest = nkb - done
        carry = lax.cond(
            rest >= 2,
            lambda c, heads=heads: update(pl.multiple_of(done * tk, 2 * tk), c, heads, None, 2 * tk),
            lambda c: c, carry)
        carry = lax.cond(
            rest % 2 == 1,
            lambda c, heads=heads: update(pl.multiple_of((nkb - 1) * tk, tk), c, heads, None),
            lambda c: c, carry)
        for r in range(nsub):
            carry = update(pl.multiple_of(i * tq + r * tk, tk), carry, heads, col + r * tk <= row)
        accs += [c[1] for c in carry]
    outs = []
    for p in range(N_HEADS // 2):
        a0, a1 = accs[2 * p], accs[2 * p + 1]
        outs.append(jnp.where(_half_mask(0), a0 / a0[:, HEAD_DIM:HEAD_DIM + 1], a1 / a1[:, 0:1]))
    o_ref[...] = jnp.concatenate(outs, axis=1).astype(o_ref.dtype)


def _mla_attention(q, k, v, batch, seq):
    tq = MLA_TQ
    q3, k3, v3 = (t.reshape(batch, seq, t.shape[-1]) for t in (q, k, v))
    out = pl.pallas_call(
        _mla_kernel,
        grid=(batch, seq // tq),
        in_specs=[pl.BlockSpec((None, tq, 512), lambda b, i: (b, i, 0)),
                  pl.BlockSpec((None, seq, 512), lambda b, i: (b, 0, 0), pipeline_mode=pl.Buffered(1)),
                  pl.BlockSpec((None, seq, WIDTH), lambda b, i: (b, 0, 0), pipeline_mode=pl.Buffered(1))],
        out_specs=pl.BlockSpec((None, tq, WIDTH), lambda b, i: (b, i, 0)),
        out_shape=jax.ShapeDtypeStruct((batch, seq, WIDTH), BF16),
        compiler_params=_cparams(("parallel", "arbitrary")),
        name="mla_attention",
    )(q3, k3, v3)
    return out.reshape(batch * seq, WIDTH)


def _sb_kernel(q_ref, k_ref, v_ref, o_ref):
    tq = q_ref.shape[0]
    i = pl.program_id(1)
    row = lax.broadcasted_iota(I32, (tq, tq), 0)
    col = lax.broadcasted_iota(I32, (tq, tq), 1)
    strict = col < row
    later = (row > col).astype(BF16)
    qs = []
    for hh in range(N_HEADS):
        qp = q_ref[:, (hh // 2) * LANES:(hh // 2 + 1) * LANES]
        qs.append(jnp.where(_half_mask(hh % 2), qp, jnp.zeros_like(qp)))

    def block(j, carry, diag):
        off = pl.multiple_of(j * tq, tq)
        runs, accs = carry
        heads = range(N_HEADS)
        zs = [_dot_nt(qs[hh], k_ref[pl.ds(off, tq), (hh // 2) * LANES:(hh // 2 + 1) * LANES]) for hh in heads]
        lsps = [jnp.minimum(z, 0.0) - jnp.log2(1.0 + jnp.exp2(-jnp.abs(z))) for z in zs]
        lsns = [lsp - z for lsp, z in zip(lsps, zs)]
        if diag:
            lsns = [jnp.where(strict, t, 0.0) for t in lsns]
        his = [t.astype(BF16) for t in lsns]
        los = [(t - hi.astype(F32)).astype(BF16) for t, hi in zip(lsns, his)]
        rems = [_dot(hi, later) + _dot(lo, later) for hi, lo in zip(his, los)]
        args = [lsps[hh] + rems[hh] + runs[hh] for hh in heads]
        if diag:
            args = [jnp.where(strict, t, NEG_BIG) for t in args]
        probs = [jnp.exp2(t).astype(BF16) for t in args]
        new_runs = tuple(runs[hh] + rems[hh][:, 0:1] + lsns[hh][:, 0:1] for hh in heads)
        new_accs = list(accs)
        for hh in heads:
            p = hh // 2
            vb = v_ref[pl.ds(off, tq), p * LANES:(p + 1) * LANES]
            vb = jnp.where(_half_mask(hh % 2), vb, jnp.zeros_like(vb))
            new_accs[p] = new_accs[p] + _dot(probs[hh], vb)
        return new_runs, tuple(new_accs)

    init = (tuple(jnp.zeros((tq, 1), F32) for _ in range(N_HEADS)),
            tuple(jnp.zeros((tq, LANES), F32) for _ in range(N_HEADS // 2)))
    def still_active(runs):
        top = functools.reduce(jnp.maximum, runs)
        return (jnp.max(top) > SB_RUN_FLOOR).astype(I32)

    runs, accs = block(i, init, True)

    def cond(c):
        return (c[0] < i) & (c[1] > 0)

    def body(c):
        jj, _, runs, accs = c
        runs, accs = block(i - 1 - jj, (runs, accs), False)
        return jj + 1, still_active(runs), runs, accs

    _, _, _, accs = lax.while_loop(cond, body, (jnp.int32(0), still_active(runs), runs, accs))
    o_ref[...] = jnp.concatenate(accs, axis=1).astype(o_ref.dtype)


def _sb_attention(q, k, v, batch, seq):
    tq = TQ_ATT
    q3, k3, v3 = (t.reshape(batch, seq, WIDTH) for t in (q, k, v))
    out = pl.pallas_call(
        _sb_kernel,
        grid=(batch, seq // tq),
        in_specs=[pl.BlockSpec((None, tq, WIDTH), lambda b, i: (b, i, 0)),
                  pl.BlockSpec((None, seq, WIDTH), lambda b, i: (b, 0, 0)),
                  pl.BlockSpec((None, seq, WIDTH), lambda b, i: (b, 0, 0))],
        out_specs=pl.BlockSpec((None, tq, WIDTH), lambda b, i: (b, i, 0)),
        out_shape=jax.ShapeDtypeStruct((batch, seq, WIDTH), BF16),
        compiler_params=_cparams(("parallel", "arbitrary")),
        name="stick_breaking",
    )(q3, k3, v3)
    return out.reshape(batch * seq, WIDTH)


def _rel_bucket(dist):
    exact = REL_BUCKETS // 2
    n = jnp.maximum(dist, 0)
    nf = jnp.maximum(n, 1).astype(F32)
    large = exact + (jnp.log(nf / exact) / math.log(REL_MAX_DIST / exact) * (REL_BUCKETS - exact)).astype(I32)
    large = jnp.clip(large, 0, REL_BUCKETS - 1)
    return jnp.where(n < exact, n, large)


def _swa_kernel(sink_ref, tab_ref, q_ref, kc_ref, kh_ref, vc_ref, vh_ref, pq_ref, pkc_ref, pkh_ref, o_ref):
    w = SWA_WINDOW
    step = pl.program_id(1)
    row = lax.broadcasted_iota(I32, (w, w), 0)
    col = lax.broadcasted_iota(I32, (w, w), 1)
    valid_c = col <= row
    valid_p = col > row
    tabs = [jnp.broadcast_to(tab_ref[hh:hh + 1, :], (w, LANES)) for hh in range(N_HEADS)]
    ones = jnp.ones((1, LANES), BF16)
    nsub = q_ref.shape[0] // w
    chains = [(r, hh) for r in range(nsub) for hh in range(N_HEADS)]

    def keys(ref, halo_ref, r, hh):
        sl = slice((hh // 2) * LANES, (hh // 2 + 1) * LANES)
        cur = ref[r * w:(r + 1) * w, sl]
        prev = ref[(r - 1) * w:r * w, sl] if r else halo_ref[:, sl]
        return cur, prev

    buckets = []
    for r in range(nsub):
        pq = pq_ref[r * w:(r + 1) * w, :]
        pk_prev = pkc_ref[:, (r - 1) * w:r * w] if r else pkh_ref[...]
        buckets.append((_rel_bucket(pq - pkc_ref[:, r * w:(r + 1) * w]), _rel_bucket(pq - pk_prev)))
    logits = []
    for r, hh in chains:
        qp = q_ref[r * w:(r + 1) * w, (hh // 2) * LANES:(hh // 2 + 1) * LANES]
        qh = jnp.where(_half_mask(hh % 2), qp, jnp.zeros_like(qp))
        kc, kp = keys(kc_ref, kh_ref, r, hh)
        logits.append((_dot_nt(qh, kc), _dot_nt(qh, kp)))
    masked = []
    for (r, hh), (lc, lp) in zip(chains, logits):
        lc = jnp.where(valid_c, lc + jnp.take_along_axis(tabs[hh], buckets[r][0], axis=1), NEG_BIG)
        lp = lp + jnp.take_along_axis(tabs[hh], buckets[r][1], axis=1)
        lp = jnp.where(valid_p if r else valid_p & (step > 0), lp, NEG_BIG)
        masked.append((lc, lp))
    maxes = [jnp.maximum(jnp.maximum(jnp.max(lc, axis=-1, keepdims=True), jnp.max(lp, axis=-1, keepdims=True)),
                         sink_ref[hh]) for (r, hh), (lc, lp) in zip(chains, masked)]
    probs = [(jnp.exp(lc - m).astype(BF16), jnp.exp(lp - m).astype(BF16)) for (lc, lp), m in zip(masked, maxes)]
    outs = {}
    for (r, hh), (ec, ep), m in zip(chains, probs, maxes):
        vc, vp = keys(vc_ref, vh_ref, r, hh)
        mine = _half_mask(hh % 2)
        acc = _dot(ec, jnp.where(mine, vc, ones)) + _dot(ep, jnp.where(mine, vp, ones))
        den = (acc[:, 0:1] if hh % 2 else acc[:, HEAD_DIM:HEAD_DIM + 1]) + jnp.exp(sink_ref[hh] - m)
        outs[(r, hh)] = acc / den
    for r in range(nsub):
        pairs = [jnp.where(_half_mask(0), outs[(r, 2 * p)], outs[(r, 2 * p + 1)]) for p in range(N_HEADS // 2)]
        o_ref[r * w:(r + 1) * w, :] = jnp.concatenate(pairs, axis=1).astype(o_ref.dtype)


def _swa_attention(q, k, v, positions, sinks, rel_table, batch, seq):
    w = SWA_WINDOW
    tq = SWA_TQ
    per = tq // w
    q3, k3, v3 = (t.reshape(batch, seq, WIDTH) for t in (q, k, v))
    pcol = positions.reshape(batch, seq, 1)
    prow = positions.reshape(batch, 1, seq)
    tab = jnp.zeros((N_HEADS, LANES), F32).at[:, :REL_BUCKETS].set(rel_table.astype(F32).T)
    cur = lambda b, n: (b, n, 0)
    halo = lambda b, n: (b, jnp.maximum(n * per - 1, 0), 0)
    out = pl.pallas_call(
        _swa_kernel,
        grid=(batch, seq // tq),
        in_specs=[pl.BlockSpec(memory_space=pltpu.SMEM), _const_spec((N_HEADS, LANES)),
                  pl.BlockSpec((None, tq, WIDTH), cur),
                  pl.BlockSpec((None, tq, WIDTH), cur), pl.BlockSpec((None, w, WIDTH), halo),
                  pl.BlockSpec((None, tq, WIDTH), cur), pl.BlockSpec((None, w, WIDTH), halo),
                  pl.BlockSpec((None, tq, 1), cur),
                  pl.BlockSpec((None, 1, tq), lambda b, n: (b, 0, n)),
                  pl.BlockSpec((None, 1, w), lambda b, n: (b, 0, jnp.maximum(n * per - 1, 0)))],
        out_specs=pl.BlockSpec((None, tq, WIDTH), cur),
        out_shape=jax.ShapeDtypeStruct((batch, seq, WIDTH), BF16),
        compiler_params=_cparams(("parallel", "arbitrary")),
        name="swa_attention",
    )(sinks.astype(F32), tab, q3, k3, k3, v3, v3, pcol, prow, prow)
    return out.reshape(batch * seq, WIDTH)


def _hgrn_kernel(hg_ref, lb_ref, nw_ref, o_ref, state_ref):
    c = HGRN_CHUNK
    blk = HGRN_BLOCK

    @pl.when(pl.program_id(1) == 0)
    def _():
        state_ref[...] = jnp.zeros_like(state_ref)

    r64 = lax.broadcasted_iota(I32, (c, c), 0)
    c64 = lax.broadcasted_iota(I32, (c, c), 1)
    incl = (c64 <= r64).astype(BF16)
    ra = lax.broadcasted_iota(I32, (WIDTH, WIDTH), 0) // HEAD_DIM
    ca = lax.broadcasted_iota(I32, (WIDTH, WIDTH), 1) // HEAD_DIM
    same_head = ra == ca
    seg = same_head.astype(BF16)
    ones_cols = jnp.ones((c, LANES), BF16)
    trow = lax.broadcasted_iota(I32, (blk, WIDTH), 0)
    caps = [jnp.where(trow >= s_i, 0.0, NEG_BIG) for s_i in range(blk)]
    lane_head = lax.broadcasted_iota(I32, (1, WIDTH), 1) // HEAD_DIM
    lb = lb_ref[...]
    nw = nw_ref[...]
    dn0 = (((0,), (0,)), ((), ()))

    for ch in range(hg_ref.shape[0] // c):
        rows = slice(ch * c, (ch + 1) * c)
        qraw = hg_ref[rows, 0:WIDTH]
        fraw = hg_ref[rows, WIDTH:2 * WIDTH]
        v = hg_ref[rows, 2 * WIDTH:3 * WIDTH]
        graw = hg_ref[rows, 3 * WIDTH:4 * WIDTH]
        qf = qraw * jax.nn.sigmoid(qraw)
        forget = lb + (1.0 - lb) * jax.nn.sigmoid(fraw)
        lf = jnp.log(forget)
        kk = 1.0 - forget
        gate = graw * jax.nn.sigmoid(graw)
        vb = v.astype(BF16)

        lf3 = _split3(lf)
        bc = _dot(incl, lf3[0]) + _dot(incl, lf3[1]) + _dot(incl, lf3[2])
        b_last = bc[c - 1:c, :]
        tot_col = sum(lax.dot_general(t, ones_cols, dn0, preferred_element_type=F32) for t in lf3)
        decay_col = jnp.exp(jnp.concatenate([tot_col, tot_col], axis=1))

        state = state_ref[...]
        o_inter = _dot((qf * jnp.exp(bc)).astype(BF16), state.astype(BF16))

        def before(qa, qb, ka, kb):
            ref = bc[kb - 1:kb, :]
            qt = qf[qa:qb] * jnp.exp(bc[qa:qb] - ref)
            kt = (kk[ka:kb] * jnp.exp(ref - bc[ka:kb])).astype(BF16)
            qs = jnp.concatenate([jnp.where(lane_head == hh, qt, 0.0) for hh in range(N_HEADS)], axis=0)
            att = _dot_nt(qs.astype(BF16), kt)
            mix = _dot(att.astype(BF16), vb[ka:kb])
            nq = qb - qa
            return sum(jnp.where(lane_head == hh, mix[hh * nq:(hh + 1) * nq], 0.0) for hh in range(N_HEADS))

        bc2 = bc * LOG2E
        key2 = bc2 - jnp.log2(kk)

        def inside(a):
            b2 = bc2[a:a + blk]
            qb_ = qf[a:a + blk]
            ws = []
            for s_i in range(blk):
                ws.append(qb_ * jnp.exp2(jnp.minimum(b2 - key2[a + s_i:a + s_i + 1, :], caps[s_i])))
            att = _dot(jnp.concatenate(ws, axis=0).astype(BF16), seg)
            return sum(att[s_i * blk:(s_i + 1) * blk] * v[a + s_i:a + s_i + 1, :] for s_i in range(blk))

        pieces = {a: [] for a in range(0, c, blk)}

        def cover(a, b):
            if b - a == blk:
                pieces[a].append(inside(a))
                return
            mid = (a + b) // 2
            cover(a, mid)
            cover(mid, b)
            res = before(mid, b, a, mid)
            for off in range(0, b - mid, blk):
                pieces[mid + off].append(res[off:off + blk])

        cover(0, c)
        o = o_inter + jnp.concatenate([sum(pieces[a]) for a in range(0, c, blk)], axis=0)

        khat = (kk * jnp.exp(b_last - bc)).astype(BF16)
        upd = lax.dot_general(khat, vb, dn0, preferred_element_type=F32)
        state_ref[...] = decay_col * state + jnp.where(same_head, upd, 0.0)

        o2 = _split3(o * o)
        ms = (_dot(o2[0], seg) + _dot(o2[1], seg)) * (1.0 / HEAD_DIM)
        o_ref[rows, :] = (o * lax.rsqrt(ms + EPS) * nw * gate).astype(o_ref.dtype)


def _hgrn(hg, lower_bound, norm_w, batch, seq):
    rows = HG_ROWS
    hg3 = hg.reshape(batch, seq, 4 * WIDTH)
    out = pl.pallas_call(
        _hgrn_kernel,
        grid=(batch, seq // rows),
        in_specs=[pl.BlockSpec((None, rows, 4 * WIDTH), lambda b, i: (b, i, 0)),
                  _const_spec((1, WIDTH)), _const_spec((1, WIDTH))],
        out_specs=pl.BlockSpec((None, rows, WIDTH), lambda b, i: (b, i, 0)),
        out_shape=jax.ShapeDtypeStruct((batch, seq, WIDTH), BF16),
        scratch_shapes=[pltpu.VMEM((WIDTH, WIDTH), F32)],
        compiler_params=_cparams(("parallel", "arbitrary")),
        name="hgrn2",
    )(hg3, lower_bound.reshape(1, WIDTH).astype(F32), norm_w.reshape(1, WIDTH).astype(F32))
    return out.reshape(batch * seq, WIDTH)


def _merge_body(x, y_refs, wg_ref, wb_ref, wo_ref, g, b):
    xb = x.astype(BF16)
    merged = jnp.zeros(x.shape, F32)
    for nbr, y_ref in enumerate(y_refs):
        gate = jax.nn.sigmoid(_dot(xb, wg_ref[:, nbr * D_MODEL:(nbr + 1) * D_MODEL]))
        merged = merged + gate * _dot(y_ref[...], wb_ref[nbr])
    y = _dot(merged.astype(BF16), wo_ref[...])
    return _layernorm(ALPHA * x + y, g, b)


def _memkv_kernel(m_ref, w_ref, k_ref, v_ref):
    kv = _dot(m_ref[...].astype(BF16), w_ref[...])
    k_ref[...] = kv[:, :WIDTH].astype(BF16)
    v_ref[...] = kv[:, WIDTH:].astype(BF16)


def _memkv(mem, wkv):
    batch, m, _ = mem.shape
    return pl.pallas_call(
        _memkv_kernel,
        grid=(batch,),
        in_specs=[pl.BlockSpec((None, m, D_MODEL), lambda b: (b, 0, 0)), _const_spec(wkv.shape)],
        out_specs=[pl.BlockSpec((None, m, WIDTH), lambda b: (b, 0, 0))] * 2,
        out_shape=[jax.ShapeDtypeStruct((batch, m, WIDTH), BF16)] * 2,
        compiler_params=_cparams(("parallel",)),
        name="mem_kv",
    )(mem, wkv)


def _xattn_body(x, wq_ref, k, v, wo_ref, g, b):
    q = _dot(x.astype(BF16), wq_ref[...]).astype(BF16)
    lane = lax.broadcasted_iota(I32, (1, WIDTH), 1) // HEAD_DIM
    heads = range(N_HEADS)
    ss = [_dot_nt(jnp.where(lane == hh, q, jnp.zeros_like(q)), k) for hh in heads]
    es = [jnp.exp(s - jnp.max(s, axis=-1, keepdims=True)) for s in ss]
    ps = [(e / jnp.sum(e, axis=-1, keepdims=True)).astype(BF16) for e in es]
    o = jnp.zeros((x.shape[0], WIDTH), F32)
    for hh in heads:
        o = o + jnp.where(lane == hh, _dot(ps[hh], v), 0.0)
    y = _dot(o.astype(BF16), wo_ref[...])
    return _layernorm(ALPHA * x + y, g, b)


def _merge_xattn_kernel(x_ref, y0_ref, y1_ref, y2_ref, y3_ref, wg_ref, wb_ref, wo_ref, g1_ref, b1_ref,
                        wq_ref, k_ref, v_ref, xwo_ref, g2_ref, b2_ref, o_ref):
    x1 = _merge_body(x_ref[...], (y0_ref, y1_ref, y2_ref, y3_ref), wg_ref, wb_ref, wo_ref,
                     g1_ref[...], b1_ref[...])
    o_ref[...] = _xattn_body(x1, wq_ref, k_ref[...], v_ref[...], xwo_ref, g2_ref[...], b2_ref[...])


def _merge_xattn(x2d, ys, wg, wb, wo, g1, b1, wq, k, v, xwo, g2, b2, batch, seq):
    tm = TM_A
    m = k.shape[1]
    per = seq // tm
    row = lambda w: pl.BlockSpec((tm, w), lambda bb, i: (bb * per + i, 0))
    kv_spec = pl.BlockSpec((None, m, WIDTH), lambda bb, i: (bb, 0, 0))
    vec = _const_spec((1, D_MODEL))
    return pl.pallas_call(
        _merge_xattn_kernel,
        grid=(batch, per),
        in_specs=[row(D_MODEL)] + [row(WIDTH)] * 4 +
                 [_const_spec(wg.shape), _const_spec(wb.shape), _const_spec(wo.shape), vec, vec,
                  _const_spec(wq.shape), kv_spec, kv_spec, _const_spec(xwo.shape), vec, vec],
        out_specs=row(D_MODEL),
        out_shape=jax.ShapeDtypeStruct((batch * seq, D_MODEL), F32),
        compiler_params=_cparams(("parallel", "parallel")),
        name="merge_xattn_ln",
    )(x2d, *ys, wg, wb, wo, g1.reshape(1, -1), b1.reshape(1, -1),
      wq, k, v, xwo, g2.reshape(1, -1), b2.reshape(1, -1))


def _ffn_kernel(x_ref, w13_ref, w2_ref, g_ref, b_ref, o_ref):
    x = x_ref[...]
    xb = x.astype(BF16)
    y = None
    for h in range(F_DENSE // TF_FFN):
        lo = h * TF_FFN
        a = _dot(xb, w13_ref[:, lo:lo + TF_FFN])
        gate = _dot(xb, w13_ref[:, F_DENSE + lo:F_DENSE + lo + TF_FFN])
        part = _dot((a * jax.nn.sigmoid(a) * gate).astype(BF16), w2_ref[lo:lo + TF_FFN, :])
        y = part if y is None else y + part
    o_ref[...] = _layernorm(ALPHA * x + y, g_ref[...], b_ref[...])


def _ffn(x2d, w13, w2, g, b):
    n = x2d.shape[0]
    tm = TM_FFN
    return pl.pallas_call(
        _ffn_kernel,
        grid=(n // tm,),
        in_specs=[pl.BlockSpec((tm, D_MODEL), lambda i: (i, 0)),
                  _const_spec(w13.shape), _const_spec(w2.shape),
                  _const_spec((1, D_MODEL)), _const_spec((1, D_MODEL))],
        out_specs=pl.BlockSpec((tm, D_MODEL), lambda i: (i, 0)),
        out_shape=jax.ShapeDtypeStruct((n, D_MODEL), F32),
        compiler_params=_cparams(("parallel",)),
        name="ffn_ln",
    )(x2d, w13, w2, g.reshape(1, -1), b.reshape(1, -1))


def _router_kernel(x_ref, r_ref, info_ref, wts_ref, cnt_ref, carry_ref):
    tm = x_ref.shape[0]

    @pl.when(pl.program_id(0) == 0)
    def _():
        carry_ref[...] = jnp.zeros_like(carry_ref)

    logits = jnp.dot(x_ref[...], r_ref[...], precision=lax.Precision.HIGHEST, preferred_element_type=F32)
    lane = lax.broadcasted_iota(I32, (tm, LANES), 1)
    lg = jnp.where(lane < N_EXPERTS, logits, -jnp.inf)
    m1 = jnp.max(lg, axis=-1, keepdims=True)
    i1 = jnp.min(jnp.where(lg == m1, lane, LANES), axis=-1, keepdims=True)
    lg2 = jnp.where(lane == i1, -jnp.inf, lg)
    m2 = jnp.max(lg2, axis=-1, keepdims=True)
    i2 = jnp.min(jnp.where(lg2 == m2, lane, LANES), axis=-1, keepdims=True)
    e = jnp.exp(m2 - m1)
    w1 = 1.0 / (1.0 + e)
    w2 = e / (1.0 + e)
    sel1 = lane == i1
    sel2 = lane == i2
    chosen = jnp.where(sel1 | sel2, 1.0, 0.0)
    row = lax.broadcasted_iota(I32, (tm, tm), 0)
    col = lax.broadcasted_iota(I32, (tm, tm), 1)
    before = (col < row).astype(BF16)
    ranks = _dot(before, chosen.astype(BF16)) + carry_ref[...]
    r1 = jnp.sum(jnp.where(sel1, ranks, 0.0), axis=-1, keepdims=True).astype(I32)
    r2 = jnp.sum(jnp.where(sel2, ranks, 0.0), axis=-1, keepdims=True).astype(I32)
    carry_ref[...] = carry_ref[...] + jnp.sum(chosen, axis=0, keepdims=True)
    info_ref[...] = jnp.where(lane == 0, i1, jnp.where(lane == 1, i2, jnp.where(lane == 2, r1,
                              jnp.where(lane == 3, r2, 0))))
    wts_ref[...] = jnp.where(lane == 0, w1, jnp.where(lane == 1, w2, 0.0))
    cnt_ref[...] = carry_ref[...]


def _router(x2d, router):
    n = x2d.shape[0]
    tm = TM_A
    r_pad = jnp.zeros((D_MODEL, LANES), F32).at[:, :N_EXPERTS].set(router.astype(F32))
    row = pl.BlockSpec((tm, LANES), lambda i: (i, 0))
    return pl.pallas_call(
        _router_kernel,
        grid=(n // tm,),
        in_specs=[pl.BlockSpec((tm, D_MODEL), lambda i: (i, 0)), _const_spec(r_pad.shape)],
        out_specs=[row, row, pl.BlockSpec((1, LANES), lambda i: (0, 0))],
        out_shape=[jax.ShapeDtypeStruct((n, LANES), I32), jax.ShapeDtypeStruct((n, LANES), F32),
                   jax.ShapeDtypeStruct((1, LANES), F32)],
        scratch_shapes=[pltpu.VMEM((1, LANES), F32)],
        compiler_params=_cparams(("arbitrary",)),
        name="moe_router",
    )(x2d, r_pad)


def _dispatch_kernel(pad_ref, d0_ref, d1_ref, x_ref, xb_hbm, stage_ref, sems):
    tm = x_ref.shape[0]
    i = pl.program_id(0)
    last = pl.num_programs(0) - 1
    slot = i % 2

    def wait_step(s):
        for _ in range(2):
            pltpu.make_async_copy(stage_ref.at[s], xb_hbm.at[pl.ds(0, tm), :], sems.at[s]).wait()

    @pl.when(i >= 2)
    def _():
        wait_step(slot)

    for s in range(2):
        @pl.when(slot == s)
        def _():
            stage_ref[s] = x_ref[...]

            def issue(r, c):
                for k in range(2):
                    pltpu.make_async_copy(stage_ref.at[s, pl.ds(r, 1), :],
                                          xb_hbm.at[pl.ds((d0_ref, d1_ref)[k][0, r], 1), :], sems.at[s]).start()
                return c
            lax.fori_loop(0, tm, issue, 0, unroll=8)

    @pl.when(i == last)
    def _():
        def fill(e, c):
            def one(s, c2):
                pltpu.make_async_copy(stage_ref.at[slot, pl.ds(0, 1), :], xb_hbm.at[pl.ds(s, 1), :],
                                      sems.at[2]).start()
                return c2

            def done(s, c2):
                pltpu.make_async_copy(stage_ref.at[slot, pl.ds(0, 1), :], xb_hbm.at[pl.ds(0, 1), :],
                                      sems.at[2]).wait()
                return c2
            lax.fori_loop(pad_ref[0, e], pad_ref[1, e], one, 0)
            lax.fori_loop(pad_ref[0, e], pad_ref[1, e], done, 0)
            return c
        lax.fori_loop(0, pad_ref.shape[1], fill, 0)
        wait_step(slot)

        @pl.when(last >= 1)
        def _():
            wait_step(1 - slot)


def _dispatch(x2d, dest, pads, nblk):
    n = x2d.shape[0]
    tm = TM_DISP
    nt = n // tm
    grid_spec = pltpu.PrefetchScalarGridSpec(
        num_scalar_prefetch=1,
        grid=(nt,),
        in_specs=[pl.BlockSpec((None, 1, tm), lambda i, pads: (i, 0, 0), memory_space=pltpu.SMEM),
                  pl.BlockSpec((None, 1, tm), lambda i, pads: (i, 0, 0), memory_space=pltpu.SMEM),
                  pl.BlockSpec((tm, D_MODEL), lambda i, pads: (i, 0))],
        out_specs=pl.BlockSpec(memory_space=pl.ANY),
        scratch_shapes=[pltpu.VMEM((2, tm, D_MODEL), F32), pltpu.SemaphoreType.DMA((3,))],
    )
    return pl.pallas_call(
        _dispatch_kernel,
        grid_spec=grid_spec,
        out_shape=jax.ShapeDtypeStruct((nblk * MOE_TB, D_MODEL), F32),
        compiler_params=_cparams(("arbitrary",), disable_bounds_checks=True),
        name="moe_dispatch",
    )(pads, dest[0].reshape(nt, 1, tm), dest[1].reshape(nt, 1, tm), x2d)


def _expert_kernel(nused_ref, bexp_ref, x_ref, w1_ref, w3_ref, w2_ref, o_ref, acc_ref):
    f = pl.program_id(1)

    @pl.when(pl.program_id(0) < nused_ref[0])
    def _():
        xb = x_ref[...].astype(BF16)
        a = _dot(xb, w1_ref[...])
        gate = _dot(xb, w3_ref[...])
        part = _dot((a * jax.nn.sigmoid(a) * gate).astype(BF16), w2_ref[...])

        @pl.when(f == 0)
        def _():
            acc_ref[...] = part

        @pl.when(f > 0)
        def _():
            acc_ref[...] += part

        @pl.when(f == pl.num_programs(1) - 1)
        def _():
            o_ref[...] = acc_ref[...]

    @pl.when(pl.program_id(0) >= nused_ref[0])
    def _():
        o_ref[...] = jnp.zeros_like(o_ref)


def _experts(xb, w13, w2, nused, blk_exp, nblk):
    tb, tf = MOE_TB, MOE_TF
    nf = F_EXPERT // tf
    w13t = w13.astype(BF16)

    def blk(i, nu):
        return jnp.maximum(jnp.minimum(i, nu[0] - 1), 0)

    def ftile(i, f, nu):
        return jnp.where(i < nu[0], f, nf - 1)

    grid_spec = pltpu.PrefetchScalarGridSpec(
        num_scalar_prefetch=2,
        grid=(nblk, nf),
        in_specs=[pl.BlockSpec((tb, D_MODEL), lambda i, f, nu, be: (blk(i, nu), 0)),
                  pl.BlockSpec((None, D_MODEL, tf), lambda i, f, nu, be: (be[blk(i, nu)], 0, ftile(i, f, nu))),
                  pl.BlockSpec((None, D_MODEL, tf), lambda i, f, nu, be: (be[blk(i, nu)], 0, nf + ftile(i, f, nu))),
                  pl.BlockSpec((None, tf, D_MODEL), lambda i, f, nu, be: (be[blk(i, nu)], ftile(i, f, nu), 0))],
        out_specs=pl.BlockSpec((tb, D_MODEL), lambda i, f, nu, be: (i, 0)),
        scratch_shapes=[pltpu.VMEM((tb, D_MODEL), F32)],
    )
    return pl.pallas_call(
        _expert_kernel,
        grid_spec=grid_spec,
        out_shape=jax.ShapeDtypeStruct((nblk * tb, D_MODEL), F32),
        compiler_params=_cparams(("arbitrary", "arbitrary")),
        name="moe_experts",
    )(nused, blk_exp, xb, w13t, w13t, w2.astype(BF16))


def _combine_kernel(d0_ref, d1_ref, n0_ref, n1_ref, y_hbm, x_ref, wts_ref, g_ref, b_ref, o_ref, buf_ref, sems):
    tm = x_ref.shape[0]
    i = pl.program_id(0)
    slot = i % 2

    def gather(idx_refs, s):
        def issue(r, c):
            for k in range(2):
                pltpu.make_async_copy(y_hbm.at[pl.ds(idx_refs[k][0, r], 1), :],
                                      buf_ref.at[s, k, pl.ds(r, 1), :], sems.at[s]).start()
            return c
        lax.fori_loop(0, tm, issue, 0, unroll=8)

    @pl.when(i == 0)
    def _():
        gather((d0_ref, d1_ref), 0)

    for s in range(2):
        @pl.when((i + 1 < pl.num_programs(0)) & (slot != s))
        def _():
            gather((n0_ref, n1_ref), s)

    for k in range(2):
        pltpu.make_async_copy(y_hbm.at[pl.ds(0, tm), :], buf_ref.at[slot, k], sems.at[slot]).wait()
    wts = wts_ref[...]
    y = wts[:, 0:1] * buf_ref[slot, 0] + wts[:, 1:2] * buf_ref[slot, 1]
    o_ref[...] = _layernorm(ALPHA * x_ref[...] + y, g_ref[...], b_ref[...])


def _combine(yb, dest, x2d, wts, g, b):
    n = x2d.shape[0]
    tm = TM_COMB
    nt = n // tm
    row = lambda w: pl.BlockSpec((tm, w), lambda i: (i, 0))
    cur = pl.BlockSpec((None, 1, tm), lambda i: (i, 0, 0), memory_space=pltpu.SMEM)
    nxt = pl.BlockSpec((None, 1, tm), lambda i: (jnp.minimum(i + 1, nt - 1), 0, 0), memory_space=pltpu.SMEM)
    d0, d1 = (d.reshape(nt, 1, tm) for d in dest)
    return pl.pallas_call(
        _combine_kernel,
        grid=(nt,),
        in_specs=[cur, cur, nxt, nxt, pl.BlockSpec(memory_space=pl.ANY), row(D_MODEL), row(LANES),
                  _const_spec((1, D_MODEL)), _const_spec((1, D_MODEL))],
        out_specs=row(D_MODEL),
        out_shape=jax.ShapeDtypeStruct((n, D_MODEL), F32),
        scratch_shapes=[pltpu.VMEM((2, 2, tm, D_MODEL), F32), pltpu.SemaphoreType.DMA((2,))],
        compiler_params=_cparams(("arbitrary",), disable_bounds_checks=True),
        name="moe_combine_ln",
    )(d0, d1, d0, d1, yb, x2d, wts, g.reshape(1, -1), b.reshape(1, -1))


def _moe(x2d, router, w13, w2, g, b):
    n = x2d.shape[0]
    tb = MOE_TB
    info, wts, cnt = _router(x2d, router)
    counts = cnt[0, :N_EXPERTS].astype(I32)
    padded = (counts + tb - 1) // tb * tb
    pend = jnp.cumsum(padded)
    pstart = pend - padded
    dest = tuple((sum(jnp.where(info[:, k] == e, pstart[e], 0) for e in range(N_EXPERTS)) + info[:, 2 + k]).astype(I32)
                 for k in range(2))
    nblk = -(-(2 * n + N_EXPERTS * (tb - 1)) // tb)
    pads = jnp.stack([jnp.append(pstart + counts, pend[-1]), jnp.append(pend, nblk * tb)]).astype(I32)
    nused = (pend[-1] // tb).astype(I32).reshape(1)
    first_row = jnp.arange(nblk, dtype=I32) * tb
    blk_exp = jnp.minimum(jnp.sum(pend[None, :] <= first_row[:, None], axis=1), N_EXPERTS - 1).astype(I32)
    xb = _dispatch(x2d, dest, pads, nblk)
    yb = _experts(xb, w13, w2, nused, blk_exp, nblk)
    return _combine(yb, dest, x2d, wts, g, b)


def kernel(x, mem, positions, rel_bias_table, hgrn_lb_logits, w_in, mla_q_norm, mla_w_uq, mla_kv_norm, mla_w_ukv, swa_sinks, hgrn_norm, w_branch, w_out, ln_g, ln_b, xa_wq, xa_wkv, xa_wo, ffn_w13, ffn_w2, moe_router, moe_w13, moe_w2):
    batch, seq, _ = x.shape
    n = batch * seq
    sm = jax.nn.softmax(hgrn_lb_logits.astype(F32), axis=0)
    lower_bounds = jnp.cumsum(sm, axis=0) - sm[0]
    ctab, stab = _rope_tables(positions)
    xc = x.reshape(n, D_MODEL)
    for l in range(DEPTH):
        wts = _inproj_weights(w_in[l], mla_w_uq[l], mla_w_ukv[l])
        mq, mk, mv, swq, swk, swv, hg, sbq, sbk, sbv = _inproj(xc, wts, ctab, stab, mla_q_norm[l], mla_kv_norm[l])
        y_mla = _mla_attention(mq, mk, mv, batch, seq)
        y_swa = _swa_attention(swq, swk, swv, positions, swa_sinks[l], rel_bias_table, batch, seq)
        y_hg = _hgrn(hg, lower_bounds[l], hgrn_norm[l], batch, seq)
        y_sb = _sb_attention(sbq, sbk, sbv, batch, seq)
        go = _IN_OFF['gates']
        mk_, mv_ = _memkv(mem, xa_wkv[l].astype(BF16))
        xc = _merge_xattn(xc, (y_mla, y_swa, y_hg, y_sb), w_in[l][:, go:].astype(BF16), w_branch[l].astype(BF16),
                          w_out[l].astype(BF16), ln_g[l, 0], ln_b[l, 0],
                          (xa_wq[l] * QK_SCALE).astype(BF16), mk_, mv_, xa_wo[l].astype(BF16),
                          ln_g[l, 1], ln_b[l, 1], batch, seq)
        if l % 2 == 0:
            xc = _ffn(xc, ffn_w13[l // 2].astype(BF16), ffn_w2[l // 2].astype(BF16), ln_g[l, 2], ln_b[l, 2])
        else:
            xc = _moe(xc, moe_router[l // 2], moe_w13[l // 2], moe_w2[l // 2],
                      ln_g[l, 2], ln_b[l, 2])
    return xc.reshape(batch, seq, D_MODEL)
```

```python
import functools
import math

import jax
import jax.numpy as jnp
from jax import lax
from jax.experimental import pallas as pl
from jax.experimental.pallas import tpu as pltpu

F32 = jnp.float32
BF16 = jnp.bfloat16
I32 = jnp.int32

D_MODEL = 1024
DEPTH = 2
EPS = 1e-5
NEG_BIG = -1e30
LANES = 128
HEAD_DIM = 64
N_HEADS = 4
WIDTH = N_HEADS * HEAD_DIM

MLA_Q_LORA = 256
MLA_KV_LORA = 128
MLA_NOPE = 64
MLA_ROPE = 32
ROPE_THETA = 10000.0
MLA_SCALE = (MLA_NOPE + MLA_ROPE) ** -0.5
LOG2E = math.log2(math.e)
QK_SCALE = HEAD_DIM ** -0.5

SB_RUN_FLOOR = -150.0
SWA_WINDOW = 128
REL_BUCKETS = 32
REL_MAX_DIST = 128
HGRN_CHUNK = 64
HGRN_BLOCK = 16
N_EXPERTS = 8
F_DENSE = 2816
F_EXPERT = 3584
ALPHA = (2 * DEPTH) ** 0.25

_IN_SPLITS = (('mla_cq', 256), ('mla_ckv', 128), ('mla_kr', 32), ('swa_q', 256), ('swa_k', 128),
              ('swa_v', 128), ('hgrn', 1024), ('sb_q', 256), ('sb_k', 256), ('sb_v', 256), ('gates', 4096))
_IN_OFF = {}
_o = 0
for _n, _w in _IN_SPLITS:
    _IN_OFF[_n] = _o
    _o += _w

_A_SPLITS = (('cq', 256), ('ckv', 128), ('kra', 128), ('krb', 128), ('swa_q', 256), ('swa_k', 256),
             ('swa_v', 256), ('hgrn', 1024), ('sb_q', 256), ('sb_k', 256), ('sb_v', 256))
_A_OFF = {}
_o = 0
for _n, _w in _A_SPLITS:
    _A_OFF[_n] = (_o, _o + _w)
    _o += _w
A_COLS = _o

TM_A = 512
TQ_ATT = 256
MLA_TQ = 512
MLA_TK = 512
MLA_WIDE = 4
MLA_GROUP = 4
SWA_TQ = 512
HG_ROWS = 256
TM_FFN = 512
TF_FFN = 1408
MOE_TB = 512
MOE_TF = 1792
TM_COMB = 256
TM_DISP = 512
VMEM_LIMIT = 56 * 1024 * 1024


def _cparams(sem, **kw):
    return pltpu.CompilerParams(dimension_semantics=sem, vmem_limit_bytes=VMEM_LIMIT, **kw)


def _const_spec(shape):
    nd = len(shape)
    return pl.BlockSpec(shape, lambda *_: (0,) * nd, pipeline_mode=pl.Buffered(1))


def _layernorm(v, g, b):
    mu = jnp.mean(v, axis=-1, keepdims=True)
    vc = v - mu
    var = jnp.mean(vc * vc, axis=-1, keepdims=True)
    return vc * lax.rsqrt(var + EPS) * g + b


def _dot(a, b):
    return jnp.dot(a, b, preferred_element_type=F32)


def _dot_nt(a, b):
    return lax.dot_general(a, b, (((1,), (1,)), ((), ())), preferred_element_type=F32)


def _split3(a):
    hi = a.astype(BF16)
    r = a - hi.astype(F32)
    mid = r.astype(BF16)
    lo = (r - mid.astype(F32)).astype(BF16)
    return hi, mid, lo


def _rope_kernel(pos_ref, freq_ref, c_ref, s_ref):
    lane = lax.broadcasted_iota(I32, pos_ref.shape, 1)
    ang = pos_ref[...] * freq_ref[...]
    rope = (lane >= MLA_NOPE) & (lane < MLA_NOPE + MLA_ROPE)
    first = lane < MLA_NOPE + MLA_ROPE // 2
    c_ref[...] = jnp.where(lane < MLA_NOPE, 1.0, jnp.where(rope, jnp.cos(ang), 0.0))
    sn = jnp.sin(ang)
    s_ref[...] = jnp.where(rope, jnp.where(first, -sn, sn), 0.0)


def _rope_tables(positions):
    n = positions.size
    half = MLA_ROPE // 2
    inv_freq = ROPE_THETA ** (-jnp.arange(half, dtype=F32) / half)
    freq = jnp.zeros((1, LANES), F32).at[0, MLA_NOPE:MLA_NOPE + MLA_ROPE].set(jnp.tile(inv_freq, 2))
    posb = jnp.broadcast_to(positions.reshape(n, 1).astype(F32), (n, LANES))
    tm = 1024
    return pl.pallas_call(
        _rope_kernel,
        grid=(n // tm,),
        in_specs=[pl.BlockSpec((tm, LANES), lambda i: (i, 0)), _const_spec((1, LANES))],
        out_specs=[pl.BlockSpec((tm, LANES), lambda i: (i, 0))] * 2,
        out_shape=[jax.ShapeDtypeStruct((n, LANES), F32)] * 2,
        compiler_params=_cparams(("parallel",)),
        name="rope_tables",
    )(posb, freq)


def _inproj_kernel(x_ref, w_ref, c_ref, s_ref, qn_ref, kvn_ref, wuqa_ref, wuqb_ref, wuk_ref, wuv_ref,
                   mq_ref, mk_ref, mv_ref, swq_ref, swk_ref, swv_ref, hg_ref, sbq_ref, sbk_ref, sbv_ref):
    h = _dot(x_ref[...].astype(BF16), w_ref[...])

    def cols(name):
        lo, hi = _A_OFF[name]
        return h[:, lo:hi]

    c = c_ref[...]
    s = s_ref[...]
    c4 = jnp.concatenate([c] * N_HEADS, axis=1)
    s4 = jnp.concatenate([s] * N_HEADS, axis=1)

    cq = cols('cq')
    cqn = (cq * lax.rsqrt(jnp.mean(cq * cq, axis=-1, keepdims=True) + EPS) * qn_ref[...]).astype(BF16)
    q = _dot(cqn, wuqa_ref[...]) * c4 + _dot(cqn, wuqb_ref[...]) * s4
    mq_ref[...] = (q * (MLA_SCALE * LOG2E)).astype(BF16)

    ckv = cols('ckv')
    ckvn = (ckv * lax.rsqrt(jnp.mean(ckv * ckv, axis=-1, keepdims=True) + EPS) * kvn_ref[...]).astype(BF16)
    krot = cols('kra') * c + cols('krb') * s
    mk_ref[...] = (_dot(ckvn, wuk_ref[...]) + jnp.concatenate([krot] * N_HEADS, axis=1)).astype(BF16)
    mv_ref[...] = _dot(ckvn, wuv_ref[...]).astype(BF16)

    swq_ref[...] = cols('swa_q').astype(BF16)
    swk_ref[...] = cols('swa_k').astype(BF16)
    swv_ref[...] = cols('swa_v').astype(BF16)
    hg_ref[...] = cols('hgrn')
    sbq_ref[...] = cols('sb_q').astype(BF16)
    sbk_ref[...] = cols('sb_k').astype(BF16)
    sbv_ref[...] = cols('sb_v').astype(BF16)


def _inproj_weights(w_in, w_uq, w_ukv):
    def seg(name, width):
        o = _IN_OFF[name]
        return w_in[:, o:o + width]

    kr = seg('mla_kr', MLA_ROPE)
    half = MLA_ROPE // 2
    z64 = jnp.zeros((D_MODEL, MLA_NOPE), F32)
    z32 = jnp.zeros((D_MODEL, LANES - MLA_NOPE - MLA_ROPE), F32)
    kra = jnp.concatenate([z64, kr, z32], axis=1)
    krb = jnp.concatenate([z64, kr[:, half:], kr[:, :half], z32], axis=1)
    swk = seg('swa_k', 128)
    swv = seg('swa_v', 128)
    dup = lambda t: jnp.concatenate([t[:, :64], t[:, :64], t[:, 64:], t[:, 64:]], axis=1)
    w_a = jnp.concatenate([
        seg('mla_cq', 256), seg('mla_ckv', 128), kra, krb,
        seg('swa_q', 256) * QK_SCALE, dup(swk), dup(swv),
        seg('hgrn', 1024), seg('sb_q', 256) * (QK_SCALE * LOG2E), seg('sb_k', 256), seg('sb_v', 256)], axis=1)

    qd = MLA_NOPE + MLA_ROPE
    zq = jnp.zeros((MLA_Q_LORA, LANES - qd), F32)
    zn = jnp.zeros((MLA_Q_LORA, MLA_NOPE), F32)
    qa, qb = [], []
    for hh in range(N_HEADS):
        nope = w_uq[:, hh * qd: hh * qd + MLA_NOPE]
        rope = w_uq[:, hh * qd + MLA_NOPE: (hh + 1) * qd]
        qa += [nope, rope, zq]
        qb += [zn, rope[:, half:], rope[:, :half], zq]
    wuqa = jnp.concatenate(qa, axis=1)
    wuqb = jnp.concatenate(qb, axis=1)
    lane = jnp.arange(N_HEADS * LANES) % LANES
    wuk = jnp.where(lane[None, :] < MLA_NOPE, w_ukv, 0.0)
    wuv = jnp.concatenate([w_ukv[:, hh * LANES + MLA_NOPE:(hh + 1) * LANES] for hh in range(N_HEADS)], axis=1)
    return tuple(t.astype(BF16) for t in (w_a, wuqa, wuqb, wuk, wuv))


def _inproj(x2d, wts, ctab, stab, q_norm, kv_norm):
    n = x2d.shape[0]
    w_a, wuqa, wuqb, wuk, wuv = wts
    tm = TM_A
    row = lambda w: pl.BlockSpec((tm, w), lambda i: (i, 0))
    out_w = (512, 512, 256, 256, 256, 256, 1024, 256, 256, 256)
    out_dt = (BF16, BF16, BF16, BF16, BF16, BF16, F32, BF16, BF16, BF16)
    return pl.pallas_call(
        _inproj_kernel,
        grid=(n // tm,),
        in_specs=[row(D_MODEL), _const_spec(w_a.shape), row(LANES), row(LANES),
                  _const_spec((1, MLA_Q_LORA)), _const_spec((1, MLA_KV_LORA)),
                  _const_spec(wuqa.shape), _const_spec(wuqb.shape), _const_spec(wuk.shape),
                  _const_spec(wuv.shape)],
        out_specs=[row(w) for w in out_w],
        out_shape=[jax.ShapeDtypeStruct((n, w), d) for w, d in zip(out_w, out_dt)],
        compiler_params=_cparams(("parallel",)),
        name="inproj",
    )(x2d, w_a, ctab, stab, q_norm.reshape(1, -1), kv_norm.reshape(1, -1), wuqa, wuqb, wuk, wuv)


def _half_mask(half):
    lane = lax.broadcasted_iota(I32, (1, LANES), 1)
    return (lane < HEAD_DIM) if half == 0 else (lane >= HEAD_DIM)


def _mla_kernel(q_ref, k_ref, v_ref, o_ref):
    tq = q_ref.shape[0]
    tk = MLA_TK
    nsub = tq // tk
    i = pl.program_id(1)
    row = lax.broadcasted_iota(I32, (tq, tk), 0)
    col = lax.broadcasted_iota(I32, (tq, tk), 1)
    ones = jnp.ones((1, LANES), BF16)

    def update(off, carry, heads, mask, width=tk):
        ss = [_dot_nt(q_ref[:, hh * LANES:(hh + 1) * LANES],
                      k_ref[pl.ds(off, width), hh * LANES:(hh + 1) * LANES]) for hh in heads]
        if mask is not None:
            ss = [jnp.where(mask, s, NEG_BIG) for s in ss]
        ms = [jnp.maximum(c[0], jnp.max(s, axis=-1, keepdims=True)) for c, s in zip(carry, ss)]
        pms = [jnp.exp2(s - m).astype(BF16) for s, m in zip(ss, ms)]
        new = []
        for n, hh in enumerate(heads):
            vb = v_ref[pl.ds(off, width), (hh // 2) * LANES:(hh // 2 + 1) * LANES]
            vb = jnp.where(_half_mask(hh % 2), vb, ones)
            m, acc = carry[n]
            new.append((ms[n], jnp.exp2(m - ms[n]) * acc + _dot(pms[n], vb)))
        return tuple(new)

    accs = []
    for g in range(0, N_HEADS, MLA_GROUP):
        heads = tuple(range(g, g + MLA_GROUP))
        init = tuple((jnp.full((tq, 1), NEG_BIG, F32), jnp.zeros((tq, LANES), F32)) for _ in heads)
        nkb = i * nsub
        wide = MLA_WIDE * tk
        carry = lax.fori_loop(
            0, nkb // MLA_WIDE,
            lambda j, c, heads=heads: update(pl.multiple_of(j * wide, wide), c, heads, None, wide), init)
        done = nkb // MLA_WIDE * MLA_WIDE
        rest = nkb - done
        carry = lax.cond(
            rest >= 2,
            lambda c, heads=heads: update(pl.multiple_of(done * tk, 2 * tk), c, heads, None, 2 * tk),
            lambda c: c, carry)
        carry = lax.cond(
            rest % 2 == 1,
            lambda c, heads=heads: update(pl.multiple_of((nkb - 1) * tk, tk), c, heads, None),
            lambda c: c, carry)
        for r in range(nsub):
            carry = update(pl.multiple_of(i * tq + r * tk, tk), carry, heads, col + r * tk <= row)
        accs += [c[1] for c in carry]
    outs = []
    for p in range(N_HEADS // 2):
        a0, a1 = accs[2 * p], accs[2 * p + 1]
        outs.append(jnp.where(_half_mask(0), a0 / a0[:, HEAD_DIM:HEAD_DIM + 1], a1 / a1[:, 0:1]))
    o_ref[...] = jnp.concatenate(outs, axis=1).astype(o_ref.dtype)


def _mla_attention(q, k, v, batch, seq):
    tq = MLA_TQ
    q3, k3, v3 = (t.reshape(batch, seq, t.shape[-1]) for t in (q, k, v))
    out = pl.pallas_call(
        _mla_kernel,
        grid=(batch, seq // tq),
        in_specs=[pl.BlockSpec((None, tq, 512), lambda b, i: (b, i, 0)),
                  pl.BlockSpec((None, seq, 512), lambda b, i: (b, 0, 0), pipeline_mode=pl.Buffered(1)),
                  pl.BlockSpec((None, seq, WIDTH), lambda b, i: (b, 0, 0), pipeline_mode=pl.Buffered(1))],
        out_specs=pl.BlockSpec((None, tq, WIDTH), lambda b, i: (b, i, 0)),
        out_shape=jax.ShapeDtypeStruct((batch, seq, WIDTH), BF16),
        compiler_params=_cparams(("parallel", "arbitrary")),
        name="mla_attention",
    )(q3, k3, v3)
    return out.reshape(batch * seq, WIDTH)


def _sb_kernel(q_ref, k_ref, v_ref, o_ref):
    tq = q_ref.shape[0]
    i = pl.program_id(1)
    row = lax.broadcasted_iota(I32, (tq, tq), 0)
    col = lax.broadcasted_iota(I32, (tq, tq), 1)
    strict = col < row
    later = (row > col).astype(BF16)
    qs = []
    for hh in range(N_HEADS):
        qp = q_ref[:, (hh // 2) * LANES:(hh // 2 + 1) * LANES]
        qs.append(jnp.where(_half_mask(hh % 2), qp, jnp.zeros_like(qp)))

    def block(j, carry, diag):
        off = pl.multiple_of(j * tq, tq)
        runs, accs = carry
        heads = range(N_HEADS)
        zs = [_dot_nt(qs[hh], k_ref[pl.ds(off, tq), (hh // 2) * LANES:(hh // 2 + 1) * LANES]) for hh in heads]
        lsps = [jnp.minimum(z, 0.0) - jnp.log2(1.0 + jnp.exp2(-jnp.abs(z))) for z in zs]
        lsns = [lsp - z for lsp, z in zip(lsps, zs)]
        if diag:
            lsns = [jnp.where(strict, t, 0.0) for t in lsns]
        his = [t.astype(BF16) for t in lsns]
        los = [(t - hi.astype(F32)).astype(BF16) for t, hi in zip(lsns, his)]
        rems = [_dot(hi, later) + _dot(lo, later) for hi, lo in zip(his, los)]
        args = [lsps[hh] + rems[hh] + runs[hh] for hh in heads]
        if diag:
            args = [jnp.where(strict, t, NEG_BIG) for t in args]
        probs = [jnp.exp2(t).astype(BF16) for t in args]
        new_runs = tuple(runs[hh] + rems[hh][:, 0:1] + lsns[hh][:, 0:1] for hh in heads)
        new_accs = list(accs)
        for hh in heads:
            p = hh // 2
            vb = v_ref[pl.ds(off, tq), p * LANES:(p + 1) * LANES]
            vb = jnp.where(_half_mask(hh % 2), vb, jnp.zeros_like(vb))
            new_accs[p] = new_accs[p] + _dot(probs[hh], vb)
        return new_runs, tuple(new_accs)

    init = (tuple(jnp.zeros((tq, 1), F32) for _ in range(N_HEADS)),
            tuple(jnp.zeros((tq, LANES), F32) for _ in range(N_HEADS // 2)))
    def still_active(runs):
        top = functools.reduce(jnp.maximum, runs)
        return (jnp.max(top) > SB_RUN_FLOOR).astype(I32)

    runs, accs = block(i, init, True)

    def cond(c):
        return (c[0] < i) & (c[1] > 0)

    def body(c):
        jj, _, runs, accs = c
        runs, accs = block(i - 1 - jj, (runs, accs), False)
        return jj + 1, still_active(runs), runs, accs

    _, _, _, accs = lax.while_loop(cond, body, (jnp.int32(0), still_active(runs), runs, accs))
    o_ref[...] = jnp.concatenate(accs, axis=1).astype(o_ref.dtype)


def _sb_attention(q, k, v, batch, seq):
    tq = TQ_ATT
    q3, k3, v3 = (t.reshape(batch, seq, WIDTH) for t in (q, k, v))
    out = pl.pallas_call(
        _sb_kernel,
        grid=(batch, seq // tq),
        in_specs=[pl.BlockSpec((None, tq, WIDTH), lambda b, i: (b, i, 0)),
                  pl.BlockSpec((None, seq, WIDTH), lambda b, i: (b, 0, 0)),
                  pl.BlockSpec((None, seq, WIDTH), lambda b, i: (b, 0, 0))],
        out_specs=pl.BlockSpec((None, tq, WIDTH), lambda b, i: (b, i, 0)),
        out_shape=jax.ShapeDtypeStruct((batch, seq, WIDTH), BF16),
        compiler_params=_cparams(("parallel", "arbitrary")),
        name="stick_breaking",
    )(q3, k3, v3)
    return out.reshape(batch * seq, WIDTH)


def _rel_bucket(dist):
    exact = REL_BUCKETS // 2
    n = jnp.maximum(dist, 0)
    nf = jnp.maximum(n, 1).astype(F32)
    large = exact + (jnp.log(nf / exact) / math.log(REL_MAX_DIST / exact) * (REL_BUCKETS - exact)).astype(I32)
    large = jnp.clip(large, 0, REL_BUCKETS - 1)
    return jnp.where(n < exact, n, large)


def _swa_kernel(sink_ref, tab_ref, q_ref, kc_ref, kh_ref, vc_ref, vh_ref, pq_ref, pkc_ref, pkh_ref, o_ref):
    w = SWA_WINDOW
    step = pl.program_id(1)
    row = lax.broadcasted_iota(I32, (w, w), 0)
    col = lax.broadcasted_iota(I32, (w, w), 1)
    valid_c = col <= row
    valid_p = col > row
    tabs = [jnp.broadcast_to(tab_ref[hh:hh + 1, :], (w, LANES)) for hh in range(N_HEADS)]
    ones = jnp.ones((1, LANES), BF16)
    nsub = q_ref.shape[0] // w
    chains = [(r, hh) for r in range(nsub) for hh in range(N_HEADS)]

    def keys(ref, halo_ref, r, hh):
        sl = slice((hh // 2) * LANES, (hh // 2 + 1) * LANES)
        cur = ref[r * w:(r + 1) * w, sl]
        prev = ref[(r - 1) * w:r * w, sl] if r else halo_ref[:, sl]
        return cur, prev

    buckets = []
    for r in range(nsub):
        pq = pq_ref[r * w:(r + 1) * w, :]
        pk_prev = pkc_ref[:, (r - 1) * w:r * w] if r else pkh_ref[...]
        buckets.append((_rel_bucket(pq - pkc_ref[:, r * w:(r + 1) * w]), _rel_bucket(pq - pk_prev)))
    logits = []
    for r, hh in chains:
        qp = q_ref[r * w:(r + 1) * w, (hh // 2) * LANES:(hh // 2 + 1) * LANES]
        qh = jnp.where(_half_mask(hh % 2), qp, jnp.zeros_like(qp))
        kc, kp = keys(kc_ref, kh_ref, r, hh)
        logits.append((_dot_nt(qh, kc), _dot_nt(qh, kp)))
    masked = []
    for (r, hh), (lc, lp) in zip(chains, logits):
        lc = jnp.where(valid_c, lc + jnp.take_along_axis(tabs[hh], buckets[r][0], axis=1), NEG_BIG)
        lp = lp + jnp.take_along_axis(tabs[hh], buckets[r][1], axis=1)
        lp = jnp.where(valid_p if r else valid_p & (step > 0), lp, NEG_BIG)
        masked.append((lc, lp))
    maxes = [jnp.maximum(jnp.maximum(jnp.max(lc, axis=-1, keepdims=True), jnp.max(lp, axis=-1, keepdims=True)),
                         sink_ref[hh]) for (r, hh), (lc, lp) in zip(chains, masked)]
    probs = [(jnp.exp(lc - m).astype(BF16), jnp.exp(lp - m).astype(BF16)) for (lc, lp), m in zip(masked, maxes)]
    outs = {}
    for (r, hh), (ec, ep), m in zip(chains, probs, maxes):
        vc, vp = keys(vc_ref, vh_ref, r, hh)
        mine = _half_mask(hh % 2)
        acc = _dot(ec, jnp.where(mine, vc, ones)) + _dot(ep, jnp.where(mine, vp, ones))
        den = (acc[:, 0:1] if hh % 2 else acc[:, HEAD_DIM:HEAD_DIM + 1]) + jnp.exp(sink_ref[hh] - m)
        outs[(r, hh)] = acc / den
    for r in range(nsub):
        pairs = [jnp.where(_half_mask(0), outs[(r, 2 * p)], outs[(r, 2 * p + 1)]) for p in range(N_HEADS // 2)]
        o_ref[r * w:(r + 1) * w, :] = jnp.concatenate(pairs, axis=1).astype(o_ref.dtype)


def _swa_attention(q, k, v, positions, sinks, rel_table, batch, seq):
    w = SWA_WINDOW
    tq = SWA_TQ
    per = tq // w
    q3, k3, v3 = (t.reshape(batch, seq, WIDTH) for t in (q, k, v))
    pcol = positions.reshape(batch, seq, 1)
    prow = positions.reshape(batch, 1, seq)
    tab = jnp.zeros((N_HEADS, LANES), F32).at[:, :REL_BUCKETS].set(rel_table.astype(F32).T)
    cur = lambda b, n: (b, n, 0)
    halo = lambda b, n: (b, jnp.maximum(n * per - 1, 0), 0)
    out = pl.pallas_call(
        _swa_kernel,
        grid=(batch, seq // tq),
        in_specs=[pl.BlockSpec(memory_space=pltpu.SMEM), _const_spec((N_HEADS, LANES)),
                  pl.BlockSpec((None, tq, WIDTH), cur),
                  pl.BlockSpec((None, tq, WIDTH), cur), pl.BlockSpec((None, w, WIDTH), halo),
                  pl.BlockSpec((None, tq, WIDTH), cur), pl.BlockSpec((None, w, WIDTH), halo),
                  pl.BlockSpec((None, tq, 1), cur),
                  pl.BlockSpec((None, 1, tq), lambda b, n: (b, 0, n)),
                  pl.BlockSpec((None, 1, w), lambda b, n: (b, 0, jnp.maximum(n * per - 1, 0)))],
        out_specs=pl.BlockSpec((None, tq, WIDTH), cur),
        out_shape=jax.ShapeDtypeStruct((batch, seq, WIDTH), BF16),
        compiler_params=_cparams(("parallel", "arbitrary")),
        name="swa_attention",
    )(sinks.astype(F32), tab, q3, k3, k3, v3, v3, pcol, prow, prow)
    return out.reshape(batch * seq, WIDTH)


def _hgrn_kernel(hg_ref, lb_ref, nw_ref, o_ref, state_ref):
    c = HGRN_CHUNK
    blk = HGRN_BLOCK

    @pl.when(pl.program_id(1) == 0)
    def _():
        state_ref[...] = jnp.zeros_like(state_ref)

    r64 = lax.broadcasted_iota(I32, (c, c), 0)
    c64 = lax.broadcasted_iota(I32, (c, c), 1)
    incl = (c64 <= r64).astype(BF16)
    ra = lax.broadcasted_iota(I32, (WIDTH, WIDTH), 0) // HEAD_DIM
    ca = lax.broadcasted_iota(I32, (WIDTH, WIDTH), 1) // HEAD_DIM
    same_head = ra == ca
    seg = same_head.astype(BF16)
    ones_cols = jnp.ones((c, LANES), BF16)
    trow = lax.broadcasted_iota(I32, (blk, WIDTH), 0)
    caps = [jnp.where(trow >= s_i, 0.0, NEG_BIG) for s_i in range(blk)]
    lane_head = lax.broadcasted_iota(I32, (1, WIDTH), 1) // HEAD_DIM
    lb = lb_ref[...]
    nw = nw_ref[...]
    dn0 = (((0,), (0,)), ((), ()))

    for ch in range(hg_ref.shape[0] // c):
        rows = slice(ch * c, (ch + 1) * c)
        qraw = hg_ref[rows, 0:WIDTH]
        fraw = hg_ref[rows, WIDTH:2 * WIDTH]
        v = hg_ref[rows, 2 * WIDTH:3 * WIDTH]
        graw = hg_ref[rows, 3 * WIDTH:4 * WIDTH]
        qf = qraw * jax.nn.sigmoid(qraw)
        forget = lb + (1.0 - lb) * jax.nn.sigmoid(fraw)
        lf = jnp.log(forget)
        kk = 1.0 - forget
        gate = graw * jax.nn.sigmoid(graw)
        vb = v.astype(BF16)

        lf3 = _split3(lf)
        bc = _dot(incl, lf3[0]) + _dot(incl, lf3[1]) + _dot(incl, lf3[2])
        b_last = bc[c - 1:c, :]
        tot_col = sum(lax.dot_general(t, ones_cols, dn0, preferred_element_type=F32) for t in lf3)
        decay_col = jnp.exp(jnp.concatenate([tot_col, tot_col], axis=1))

        state = state_ref[...]
        o_inter = _dot((qf * jnp.exp(bc)).astype(BF16), state.astype(BF16))

        def before(qa, qb, ka, kb):
            ref = bc[kb - 1:kb, :]
            qt = qf[qa:qb] * jnp.exp(bc[qa:qb] - ref)
            kt = (kk[ka:kb] * jnp.exp(ref - bc[ka:kb])).astype(BF16)
            qs = jnp.concatenate([jnp.where(lane_head == hh, qt, 0.0) for hh in range(N_HEADS)], axis=0)
            att = _dot_nt(qs.astype(BF16), kt)
            mix = _dot(att.astype(BF16), vb[ka:kb])
            nq = qb - qa
            return sum(jnp.where(lane_head == hh, mix[hh * nq:(hh + 1) * nq], 0.0) for hh in range(N_HEADS))

        bc2 = bc * LOG2E
        key2 = bc2 - jnp.log2(jnp.maximum(kk, 0.0))

        def inside(a):
            b2 = bc2[a:a + blk]
            qb_ = qf[a:a + blk]
            ws = []
            for s_i in range(blk):
                ws.append(qb_ * jnp.exp2(jnp.minimum(b2 - key2[a + s_i:a + s_i + 1, :], caps[s_i])))
            att = _dot(jnp.concatenate(ws, axis=0).astype(BF16), seg)
            return sum(att[s_i * blk:(s_i + 1) * blk] * v[a + s_i:a + s_i + 1, :] for s_i in range(blk))

        pieces = {a: [] for a in range(0, c, blk)}

        def cover(a, b):
            if b - a == blk:
                pieces[a].append(inside(a))
                return
            mid = (a + b) // 2
            cover(a, mid)
            cover(mid, b)
            res = before(mid, b, a, mid)
            for off in range(0, b - mid, blk):
                pieces[mid + off].append(res[off:off + blk])

        cover(0, c)
        o = o_inter + jnp.concatenate([sum(pieces[a]) for a in range(0, c, blk)], axis=0)

        khat = (kk * jnp.exp(b_last - bc)).astype(BF16)
        upd = lax.dot_general(khat, vb, dn0, preferred_element_type=F32)
        state_ref[...] = decay_col * state + jnp.where(same_head, upd, 0.0)

        o2 = _split3(o * o)
        ms = (_dot(o2[0], seg) + _dot(o2[1], seg)) * (1.0 / HEAD_DIM)
        o_ref[rows, :] = (o * lax.rsqrt(ms + EPS) * nw * gate).astype(o_ref.dtype)


def _hgrn(hg, lower_bound, norm_w, batch, seq):
    rows = HG_ROWS
    hg3 = hg.reshape(batch, seq, 4 * WIDTH)
    out = pl.pallas_call(
        _hgrn_kernel,
        grid=(batch, seq // rows),
        in_specs=[pl.BlockSpec((None, rows, 4 * WIDTH), lambda b, i: (b, i, 0)),
                  _const_spec((1, WIDTH)), _const_spec((1, WIDTH))],
        out_specs=pl.BlockSpec((None, rows, WIDTH), lambda b, i: (b, i, 0)),
        out_shape=jax.ShapeDtypeStruct((batch, seq, WIDTH), BF16),
        scratch_shapes=[pltpu.VMEM((WIDTH, WIDTH), F32)],
        compiler_params=_cparams(("parallel", "arbitrary")),
        name="hgrn2",
    )(hg3, lower_bound.reshape(1, WIDTH).astype(F32), norm_w.reshape(1, WIDTH).astype(F32))
    return out.reshape(batch * seq, WIDTH)


def _merge_body(x, y_refs, wg_ref, wb_ref, wo_ref, g, b):
    xb = x.astype(BF16)
    merged = jnp.zeros(x.shape, F32)
    for nbr, y_ref in enumerate(y_refs):
        gate = jax.nn.sigmoid(_dot(xb, wg_ref[:, nbr * D_MODEL:(nbr + 1) * D_MODEL]))
        merged = merged + gate * _dot(y_ref[...], wb_ref[nbr])
    y = _dot(merged.astype(BF16), wo_ref[...])
    return _layernorm(ALPHA * x + y, g, b)


def _memkv_kernel(m_ref, w_ref, k_ref, v_ref):
    kv = _dot(m_ref[...].astype(BF16), w_ref[...])
    k_ref[...] = kv[:, :WIDTH].astype(BF16)
    v_ref[...] = kv[:, WIDTH:].astype(BF16)


def _memkv(mem, wkv):
    batch, m, _ = mem.shape
    return pl.pallas_call(
        _memkv_kernel,
        grid=(batch,),
        in_specs=[pl.BlockSpec((None, m, D_MODEL), lambda b: (b, 0, 0)), _const_spec(wkv.shape)],
        out_specs=[pl.BlockSpec((None, m, WIDTH), lambda b: (b, 0, 0))] * 2,
        out_shape=[jax.ShapeDtypeStruct((batch, m, WIDTH), BF16)] * 2,
        compiler_params=_cparams(("parallel",)),
        name="mem_kv",
    )(mem, wkv)


def _xattn_body(x, wq_ref, k, v, wo_ref, g, b):
    q = _dot(x.astype(BF16), wq_ref[...]).astype(BF16)
    lane = lax.broadcasted_iota(I32, (1, WIDTH), 1) // HEAD_DIM
    heads = range(N_HEADS)
    ss = [_dot_nt(jnp.where(lane == hh, q, jnp.zeros_like(q)), k) for hh in heads]
    es = [jnp.exp(s - jnp.max(s, axis=-1, keepdims=True)) for s in ss]
    ps = [(e / jnp.sum(e, axis=-1, keepdims=True)).astype(BF16) for e in es]
    o = jnp.zeros((x.shape[0], WIDTH), F32)
    for hh in heads:
        o = o + jnp.where(lane == hh, _dot(ps[hh], v), 0.0)
    y = _dot(o.astype(BF16), wo_ref[...])
    return _layernorm(ALPHA * x + y, g, b)


def _merge_xattn_kernel(x_ref, y0_ref, y1_ref, y2_ref, y3_ref, wg_ref, wb_ref, wo_ref, g1_ref, b1_ref,
                        wq_ref, k_ref, v_ref, xwo_ref, g2_ref, b2_ref, o_ref):
    x1 = _merge_body(x_ref[...], (y0_ref, y1_ref, y2_ref, y3_ref), wg_ref, wb_ref, wo_ref,
                     g1_ref[...], b1_ref[...])
    o_ref[...] = _xattn_body(x1, wq_ref, k_ref[...], v_ref[...], xwo_ref, g2_ref[...], b2_ref[...])


def _merge_xattn(x2d, ys, wg, wb, wo, g1, b1, wq, k, v, xwo, g2, b2, batch, seq):
    tm = TM_A
    m = k.shape[1]
    per = seq // tm
    row = lambda w: pl.BlockSpec((tm, w), lambda bb, i: (bb * per + i, 0))
    kv_spec = pl.BlockSpec((None, m, WIDTH), lambda bb, i: (bb, 0, 0))
    vec = _const_spec((1, D_MODEL))
    return pl.pallas_call(
        _merge_xattn_kernel,
        grid=(batch, per),
        in_specs=[row(D_MODEL)] + [row(WIDTH)] * 4 +
                 [_const_spec(wg.shape), _const_spec(wb.shape), _const_spec(wo.shape), vec, vec,
                  _const_spec(wq.shape), kv_spec, kv_spec, _const_spec(xwo.shape), vec, vec],
        out_specs=row(D_MODEL),
        out_shape=jax.ShapeDtypeStruct((batch * seq, D_MODEL), F32),
        compiler_params=_cparams(("parallel", "parallel")),
        name="merge_xattn_ln",
    )(x2d, *ys, wg, wb, wo, g1.reshape(1, -1), b1.reshape(1, -1),
      wq, k, v, xwo, g2.reshape(1, -1), b2.reshape(1, -1))


def _ffn_kernel(x_ref, w13_ref, w2_ref, g_ref, b_ref, o_ref):
    x = x_ref[...]
    xb = x.astype(BF16)
    y = None
    for h in range(F_DENSE // TF_FFN):
        lo = h * TF_FFN
        a = _dot(xb, w13_ref[:, lo:lo + TF_FFN])
        gate = _dot(xb, w13_ref[:, F_DENSE + lo:F_DENSE + lo + TF_FFN])
        part = _dot((a * jax.nn.sigmoid(a) * gate).astype(BF16), w2_ref[lo:lo + TF_FFN, :])
        y = part if y is None else y + part
    o_ref[...] = _layernorm(ALPHA * x + y, g_ref[...], b_ref[...])


def _ffn(x2d, w13, w2, g, b):
    n = x2d.shape[0]
    tm = TM_FFN
    return pl.pallas_call(
        _ffn_kernel,
        grid=(n // tm,),
        in_specs=[pl.BlockSpec((tm, D_MODEL), lambda i: (i, 0)),
                  _const_spec(w13.shape), _const_spec(w2.shape),
                  _const_spec((1, D_MODEL)), _const_spec((1, D_MODEL))],
        out_specs=pl.BlockSpec((tm, D_MODEL), lambda i: (i, 0)),
        out_shape=jax.ShapeDtypeStruct((n, D_MODEL), F32),
        compiler_params=_cparams(("parallel",)),
        name="ffn_ln",
    )(x2d, w13, w2, g.reshape(1, -1), b.reshape(1, -1))


def _router_kernel(x_ref, r_ref, info_ref, wts_ref, cnt_ref, carry_ref):
    tm = x_ref.shape[0]

    @pl.when(pl.program_id(0) == 0)
    def _():
        carry_ref[...] = jnp.zeros_like(carry_ref)

    logits = jnp.dot(x_ref[...], r_ref[...], precision=lax.Precision.HIGHEST, preferred_element_type=F32)
    lane = lax.broadcasted_iota(I32, (tm, LANES), 1)
    lg = jnp.where(lane < N_EXPERTS, logits, -jnp.inf)
    m1 = jnp.max(lg, axis=-1, keepdims=True)
    i1 = jnp.min(jnp.where(lg == m1, lane, LANES), axis=-1, keepdims=True)
    lg2 = jnp.where(lane == i1, -jnp.inf, lg)
    m2 = jnp.max(lg2, axis=-1, keepdims=True)
    i2 = jnp.min(jnp.where(lg2 == m2, lane, LANES), axis=-1, keepdims=True)
    e = jnp.exp(m2 - m1)
    w1 = 1.0 / (1.0 + e)
    w2 = e / (1.0 + e)
    sel1 = lane == i1
    sel2 = lane == i2
    chosen = jnp.where(sel1 | sel2, 1.0, 0.0)
    row = lax.broadcasted_iota(I32, (tm, tm), 0)
    col = lax.broadcasted_iota(I32, (tm, tm), 1)
    before = (col < row).astype(BF16)
    ranks = _dot(before, chosen.astype(BF16)) + carry_ref[...]
    r1 = jnp.sum(jnp.where(sel1, ranks, 0.0), axis=-1, keepdims=True)
    r2 = jnp.sum(jnp.where(sel2, ranks, 0.0), axis=-1, keepdims=True)
    carry_ref[...] = carry_ref[...] + jnp.sum(chosen, axis=0, keepdims=True)
    info = jnp.where(lane == 0, i1.astype(F32), jnp.where(lane == 1, i2.astype(F32),
                     jnp.where(lane == 2, r1, jnp.where(lane == 3, r2, 0.0))))
    info_ref[...] = jnp.transpose(info)[:info_ref.shape[0], :]
    wts_ref[...] = jnp.where(lane == 0, w1, jnp.where(lane == 1, w2, 0.0))
    cnt_ref[...] = carry_ref[...]


def _router(x2d, router):
    n = x2d.shape[0]
    tm = TM_A
    r_pad = jnp.zeros((D_MODEL, LANES), F32).at[:, :N_EXPERTS].set(router.astype(F32))
    row = pl.BlockSpec((tm, LANES), lambda i: (i, 0))
    return pl.pallas_call(
        _router_kernel,
        grid=(n // tm,),
        in_specs=[pl.BlockSpec((tm, D_MODEL), lambda i: (i, 0)), _const_spec(r_pad.shape)],
        out_specs=[pl.BlockSpec((8, tm), lambda i: (0, i)), row, pl.BlockSpec((1, LANES), lambda i: (0, 0))],
        out_shape=[jax.ShapeDtypeStruct((8, n), F32), jax.ShapeDtypeStruct((n, LANES), F32),
                   jax.ShapeDtypeStruct((1, LANES), F32)],
        scratch_shapes=[pltpu.VMEM((1, LANES), F32)],
        compiler_params=_cparams(("arbitrary",)),
        name="moe_router",
    )(x2d, r_pad)


def _dispatch_kernel(pad_ref, d0_ref, d1_ref, x_ref, xb_hbm, stage_ref, sems):
    tm = x_ref.shape[0]
    i = pl.program_id(0)
    last = pl.num_programs(0) - 1
    slot = i % 2

    def wait_step(s):
        for _ in range(2):
            pltpu.make_async_copy(stage_ref.at[s], xb_hbm.at[pl.ds(0, tm), :], sems.at[s]).wait()

    @pl.when(i >= 2)
    def _():
        wait_step(slot)

    for s in range(2):
        @pl.when(slot == s)
        def _():
            stage_ref[s] = x_ref[...]

            def issue(r, c):
                for k in range(2):
                    pltpu.make_async_copy(stage_ref.at[s, pl.ds(r, 1), :],
                                          xb_hbm.at[pl.ds((d0_ref, d1_ref)[k][0, r], 1), :], sems.at[s]).start()
                return c
            lax.fori_loop(0, tm, issue, 0, unroll=8)

    @pl.when(i == last)
    def _():
        def fill(e, c):
            def one(s, c2):
                pltpu.make_async_copy(stage_ref.at[slot, pl.ds(0, 1), :], xb_hbm.at[pl.ds(s, 1), :],
                                      sems.at[2]).start()
                return c2

            def done(s, c2):
                pltpu.make_async_copy(stage_ref.at[slot, pl.ds(0, 1), :], xb_hbm.at[pl.ds(0, 1), :],
                                      sems.at[2]).wait()
                return c2
            lax.fori_loop(pad_ref[0, e], pad_ref[1, e], one, 0)
            lax.fori_loop(pad_ref[0, e], pad_ref[1, e], done, 0)
            return c
        lax.fori_loop(0, pad_ref.shape[1], fill, 0)
        wait_step(slot)

        @pl.when(last >= 1)
        def _():
            wait_step(1 - slot)


def _dispatch(x2d, dest, pads, nblk):
    n = x2d.shape[0]
    tm = TM_DISP
    nt = n // tm
    grid_spec = pltpu.PrefetchScalarGridSpec(
        num_scalar_prefetch=1,
        grid=(nt,),
        in_specs=[pl.BlockSpec((None, 1, tm), lambda i, pads: (i, 0, 0), memory_space=pltpu.SMEM),
                  pl.BlockSpec((None, 1, tm), lambda i, pads: (i, 0, 0), memory_space=pltpu.SMEM),
                  pl.BlockSpec((tm, D_MODEL), lambda i, pads: (i, 0))],
        out_specs=pl.BlockSpec(memory_space=pl.ANY),
        scratch_shapes=[pltpu.VMEM((2, tm, D_MODEL), F32), pltpu.SemaphoreType.DMA((3,))],
    )
    return pl.pallas_call(
        _dispatch_kernel,
        grid_spec=grid_spec,
        out_shape=jax.ShapeDtypeStruct((nblk * MOE_TB, D_MODEL), F32),
        compiler_params=_cparams(("arbitrary",), disable_bounds_checks=True),
        name="moe_dispatch",
    )(pads, dest[0].reshape(nt, 1, tm), dest[1].reshape(nt, 1, tm), x2d)


def _expert_kernel(nused_ref, bexp_ref, x_ref, w1_ref, w3_ref, w2_ref, o_ref, acc_ref):
    f = pl.program_id(1)

    @pl.when(pl.program_id(0) < nused_ref[0])
    def _():
        xb = x_ref[...].astype(BF16)
        a = _dot(xb, w1_ref[...])
        gate = _dot(xb, w3_ref[...])
        part = _dot((a * jax.nn.sigmoid(a) * gate).astype(BF16), w2_ref[...])

        @pl.when(f == 0)
        def _():
            acc_ref[...] = part

        @pl.when(f > 0)
        def _():
            acc_ref[...] += part

        @pl.when(f == pl.num_programs(1) - 1)
        def _():
            o_ref[...] = acc_ref[...]

    @pl.when(pl.program_id(0) >= nused_ref[0])
    def _():
        o_ref[...] = jnp.zeros_like(o_ref)


def _experts(xb, w13, w2, nused, blk_exp, nblk):
    tb, tf = MOE_TB, MOE_TF
    nf = F_EXPERT // tf
    w13t = w13.astype(BF16)

    def blk(i, nu):
        return jnp.maximum(jnp.minimum(i, nu[0] - 1), 0)

    def ftile(i, f, nu):
        return jnp.where(i < nu[0], f, nf - 1)

    grid_spec = pltpu.PrefetchScalarGridSpec(
        num_scalar_prefetch=2,
        grid=(nblk, nf),
        in_specs=[pl.BlockSpec((tb, D_MODEL), lambda i, f, nu, be: (blk(i, nu), 0)),
                  pl.BlockSpec((None, D_MODEL, tf), lambda i, f, nu, be: (be[blk(i, nu)], 0, ftile(i, f, nu))),
                  pl.BlockSpec((None, D_MODEL, tf), lambda i, f, nu, be: (be[blk(i, nu)], 0, nf + ftile(i, f, nu))),
                  pl.BlockSpec((None, tf, D_MODEL), lambda i, f, nu, be: (be[blk(i, nu)], ftile(i, f, nu), 0))],
        out_specs=pl.BlockSpec((tb, D_MODEL), lambda i, f, nu, be: (i, 0)),
        scratch_shapes=[pltpu.VMEM((tb, D_MODEL), F32)],
    )
    return pl.pallas_call(
        _expert_kernel,
        grid_spec=grid_spec,
        out_shape=jax.ShapeDtypeStruct((nblk * tb, D_MODEL), F32),
        compiler_params=_cparams(("arbitrary", "arbitrary")),
        name="moe_experts",
    )(nused, blk_exp, xb, w13t, w13t, w2.astype(BF16))


def _combine_kernel(d0_ref, d1_ref, n0_ref, n1_ref, y_hbm, x_ref, wts_ref, g_ref, b_ref, o_ref, buf_ref, sems):
    tm = x_ref.shape[0]
    i = pl.program_id(0)
    slot = i % 2

    def gather(idx_refs, s):
        def issue(r, c):
            for k in range(2):
                pltpu.make_async_copy(y_hbm.at[pl.ds(idx_refs[k][0, r], 1), :],
                                      buf_ref.at[s, k, pl.ds(r, 1), :], sems.at[s]).start()
            return c
        lax.fori_loop(0, tm, issue, 0, unroll=8)

    @pl.when(i == 0)
    def _():
        gather((d0_ref, d1_ref), 0)

    for s in range(2):
        @pl.when((i + 1 < pl.num_programs(0)) & (slot != s))
        def _():
            gather((n0_ref, n1_ref), s)

    for k in range(2):
        pltpu.make_async_copy(y_hbm.at[pl.ds(0, tm), :], buf_ref.at[slot, k], sems.at[slot]).wait()
    wts = wts_ref[...]
    y = wts[:, 0:1] * buf_ref[slot, 0] + wts[:, 1:2] * buf_ref[slot, 1]
    o_ref[...] = _layernorm(ALPHA * x_ref[...] + y, g_ref[...], b_ref[...])


def _combine(yb, dest, x2d, wts, g, b):
    n = x2d.shape[0]
    tm = TM_COMB
    nt = n // tm
    row = lambda w: pl.BlockSpec((tm, w), lambda i: (i, 0))
    cur = pl.BlockSpec((None, 1, tm), lambda i: (i, 0, 0), memory_space=pltpu.SMEM)
    nxt = pl.BlockSpec((None, 1, tm), lambda i: (jnp.minimum(i + 1, nt - 1), 0, 0), memory_space=pltpu.SMEM)
    d0, d1 = (d.reshape(nt, 1, tm) for d in dest)
    return pl.pallas_call(
        _combine_kernel,
        grid=(nt,),
        in_specs=[cur, cur, nxt, nxt, pl.BlockSpec(memory_space=pl.ANY), row(D_MODEL), row(LANES),
                  _const_spec((1, D_MODEL)), _const_spec((1, D_MODEL))],
        out_specs=row(D_MODEL),
        out_shape=jax.ShapeDtypeStruct((n, D_MODEL), F32),
        scratch_shapes=[pltpu.VMEM((2, 2, tm, D_MODEL), F32), pltpu.SemaphoreType.DMA((2,))],
        compiler_params=_cparams(("arbitrary",), disable_bounds_checks=True),
        name="moe_combine_ln",
    )(d0, d1, d0, d1, yb, x2d, wts, g.reshape(1, -1), b.reshape(1, -1))


def _moe(x2d, router, w13, w2, g, b):
    n = x2d.shape[0]
    tb = MOE_TB
    info, wts, cnt = _router(x2d, router)
    counts = cnt[0, :N_EXPERTS].astype(I32)
    padded = (counts + tb - 1) // tb * tb
    pend = jnp.cumsum(padded)
    pstart = pend - padded
    info = info.astype(I32)
    dest = tuple((sum(jnp.where(info[k] == e, pstart[e], 0) for e in range(N_EXPERTS)) + info[2 + k]).astype(I32)
                 for k in range(2))
    nblk = -(-(2 * n + N_EXPERTS * (tb - 1)) // tb)
    pads = jnp.stack([jnp.append(pstart + counts, pend[-1]), jnp.append(pend, nblk * tb)]).astype(I32)
    nused = (pend[-1] // tb).astype(I32).reshape(1)
    first_row = jnp.arange(nblk, dtype=I32) * tb
    blk_exp = jnp.minimum(jnp.sum(pend[None, :] <= first_row[:, None], axis=1), N_EXPERTS - 1).astype(I32)
    xb = _dispatch(x2d, dest, pads, nblk)
    yb = _experts(xb, w13, w2, nused, blk_exp, nblk)
    return _combine(yb, dest, x2d, wts, g, b)


def kernel(x, mem, positions, rel_bias_table, hgrn_lb_logits, w_in, mla_q_norm, mla_w_uq, mla_kv_norm, mla_w_ukv, swa_sinks, hgrn_norm, w_branch, w_out, ln_g, ln_b, xa_wq, xa_wkv, xa_wo, ffn_w13, ffn_w2, moe_router, moe_w13, moe_w2):
    batch, seq, _ = x.shape
    n = batch * seq
    sm = jax.nn.softmax(hgrn_lb_logits.astype(F32), axis=0)
    lower_bounds = jnp.cumsum(sm, axis=0) - sm[0]
    ctab, stab = _rope_tables(positions)
    xc = x.reshape(n, D_MODEL)
    for l in range(DEPTH):
        wts = _inproj_weights(w_in[l], mla_w_uq[l], mla_w_ukv[l])
        mq, mk, mv, swq, swk, swv, hg, sbq, sbk, sbv = _inproj(xc, wts, ctab, stab, mla_q_norm[l], mla_kv_norm[l])
        y_mla = _mla_attention(mq, mk, mv, batch, seq)
        y_swa = _swa_attention(swq, swk, swv, positions, swa_sinks[l], rel_bias_table, batch, seq)
        y_hg = _hgrn(hg, lower_bounds[l], hgrn_norm[l], batch, seq)
        y_sb = _sb_attention(sbq, sbk, sbv, batch, seq)
        go = _IN_OFF['gates']
        mk_, mv_ = _memkv(mem, xa_wkv[l].astype(BF16))
        xc = _merge_xattn(xc, (y_mla, y_swa, y_hg, y_sb), w_in[l][:, go:].astype(BF16), w_branch[l].astype(BF16),
                          w_out[l].astype(BF16), ln_g[l, 0], ln_b[l, 0],
                          (xa_wq[l] * QK_SCALE).astype(BF16), mk_, mv_, xa_wo[l].astype(BF16),
                          ln_g[l, 1], ln_b[l, 1], batch, seq)
        if l % 2 == 0:
            xc = _ffn(xc, ffn_w13[l // 2].astype(BF16), ffn_w2[l // 2].astype(BF16), ln_g[l, 2], ln_b[l, 2])
        else:
            xc = _moe(xc, moe_router[l // 2], moe_w13[l // 2], moe_w2[l // 2],
                      ln_g[l, 2], ln_b[l, 2])
    return xc.reshape(batch, seq, D_MODEL)
```

```python
import functools
import math

import jax
import jax.numpy as jnp
from jax import lax
from jax.experimental import pallas as pl
from jax.experimental.pallas import tpu as pltpu

F32 = jnp.float32
BF16 = jnp.bfloat16
I32 = jnp.int32

D_MODEL = 1024
DEPTH = 2
EPS = 1e-5
NEG_BIG = -1e30
LANES = 128
HEAD_DIM = 64
N_HEADS = 4
WIDTH = N_HEADS * HEAD_DIM

MLA_Q_LORA = 256
MLA_KV_LORA = 128
MLA_NOPE = 64
MLA_ROPE = 32
ROPE_THETA = 10000.0
MLA_SCALE = (MLA_NOPE + MLA_ROPE) ** -0.5
LOG2E = math.log2(math.e)
QK_SCALE = HEAD_DIM ** -0.5

SB_RUN_FLOOR = -150.0
SWA_WINDOW = 128
REL_BUCKETS = 32
REL_MAX_DIST = 128
HGRN_CHUNK = 64
HGRN_BLOCK = 16
N_EXPERTS = 8
F_DENSE = 2816
F_EXPERT = 3584
ALPHA = (2 * DEPTH) ** 0.25

_IN_SPLITS = (('mla_cq', 256), ('mla_ckv', 128), ('mla_kr', 32), ('swa_q', 256), ('swa_k', 128),
              ('swa_v', 128), ('hgrn', 1024), ('sb_q', 256), ('sb_k', 256), ('sb_v', 256), ('gates', 4096))
_IN_OFF = {}
_o = 0
for _n, _w in _IN_SPLITS:
    _IN_OFF[_n] = _o
    _o += _w

_A_SPLITS = (('cq', 256), ('ckv', 128), ('kra', 128), ('krb', 128), ('swa_q', 256), ('swa_k', 128),
             ('swa_v', 128), ('hgrn', 1024), ('sb_q', 256), ('sb_k', 256), ('sb_v', 256))
_A_OFF = {}
_o = 0
for _n, _w in _A_SPLITS:
    _A_OFF[_n] = (_o, _o + _w)
    _o += _w
A_COLS = _o

TM_A = 512
TQ_ATT = 256
MLA_TQ = 512
MLA_TK = 512
MLA_WIDE = 4
MLA_GROUP = 4
SWA_TQ = 512
HG_ROWS = 256
TM_FFN = 512
TF_FFN = 1408
MOE_TB = 512
MOE_TF = 1792
TM_ROUTER = 512
TM_COMB = 256
TM_DISP = 512
VMEM_LIMIT = 56 * 1024 * 1024


def _cparams(sem, **kw):
    return pltpu.CompilerParams(dimension_semantics=sem, vmem_limit_bytes=VMEM_LIMIT, **kw)


def _const_spec(shape):
    nd = len(shape)
    return pl.BlockSpec(shape, lambda *_: (0,) * nd, pipeline_mode=pl.Buffered(1))


def _layernorm(v, g, b):
    mu = jnp.mean(v, axis=-1, keepdims=True)
    vc = v - mu
    var = jnp.mean(vc * vc, axis=-1, keepdims=True)
    return vc * lax.rsqrt(var + EPS) * g + b


def _dot(a, b):
    return jnp.dot(a, b, preferred_element_type=F32)


def _dot_nt(a, b):
    return lax.dot_general(a, b, (((1,), (1,)), ((), ())), preferred_element_type=F32)


def _split3(a):
    hi = a.astype(BF16)
    r = a - hi.astype(F32)
    mid = r.astype(BF16)
    lo = (r - mid.astype(F32)).astype(BF16)
    return hi, mid, lo


def _rope_kernel(pos_ref, freq_ref, c_ref, s_ref):
    lane = lax.broadcasted_iota(I32, pos_ref.shape, 1)
    ang = pos_ref[...] * freq_ref[...]
    rope = (lane >= MLA_NOPE) & (lane < MLA_NOPE + MLA_ROPE)
    first = lane < MLA_NOPE + MLA_ROPE // 2
    c_ref[...] = jnp.where(lane < MLA_NOPE, 1.0, jnp.where(rope, jnp.cos(ang), 0.0))
    sn = jnp.sin(ang)
    s_ref[...] = jnp.where(rope, jnp.where(first, -sn, sn), 0.0)


def _rope_tables(positions):
    n = positions.size
    half = MLA_ROPE // 2
    inv_freq = ROPE_THETA ** (-jnp.arange(half, dtype=F32) / half)
    freq = jnp.zeros((1, LANES), F32).at[0, MLA_NOPE:MLA_NOPE + MLA_ROPE].set(jnp.tile(inv_freq, 2))
    posb = jnp.broadcast_to(positions.reshape(n, 1).astype(F32), (n, LANES))
    tm = 1024
    return pl.pallas_call(
        _rope_kernel,
        grid=(n // tm,),
        in_specs=[pl.BlockSpec((tm, LANES), lambda i: (i, 0)), _const_spec((1, LANES))],
        out_specs=[pl.BlockSpec((tm, LANES), lambda i: (i, 0))] * 2,
        out_shape=[jax.ShapeDtypeStruct((n, LANES), F32)] * 2,
        compiler_params=_cparams(("parallel",)),
        name="rope_tables",
    )(posb, freq)


def _inproj_kernel(x_ref, w_ref, c_ref, s_ref, qn_ref, kvn_ref, wuqa_ref, wuqb_ref, wuk_ref, wuv_ref,
                   mq_ref, mk_ref, mv_ref, swq_ref, swk_ref, swv_ref, hg_ref, sbq_ref, sbk_ref, sbv_ref):
    h = _dot(x_ref[...].astype(BF16), w_ref[...])

    def cols(name):
        lo, hi = _A_OFF[name]
        return h[:, lo:hi]

    c = c_ref[...]
    s = s_ref[...]
    c4 = jnp.concatenate([c] * N_HEADS, axis=1)
    s4 = jnp.concatenate([s] * N_HEADS, axis=1)

    cq = cols('cq')
    cqn = (cq * lax.rsqrt(jnp.mean(cq * cq, axis=-1, keepdims=True) + EPS) * qn_ref[...]).astype(BF16)
    q = _dot(cqn, wuqa_ref[...]) * c4 + _dot(cqn, wuqb_ref[...]) * s4
    mq_ref[...] = (q * (MLA_SCALE * LOG2E)).astype(BF16)

    ckv = cols('ckv')
    ckvn = (ckv * lax.rsqrt(jnp.mean(ckv * ckv, axis=-1, keepdims=True) + EPS) * kvn_ref[...]).astype(BF16)
    krot = cols('kra') * c + cols('krb') * s
    mk_ref[...] = (_dot(ckvn, wuk_ref[...]) + jnp.concatenate([krot] * N_HEADS, axis=1)).astype(BF16)
    mv_ref[...] = _dot(ckvn, wuv_ref[...]).astype(BF16)

    swq_ref[...] = cols('swa_q').astype(BF16)
    swk_ref[...] = cols('swa_k').astype(BF16)
    swv_ref[...] = cols('swa_v').astype(BF16)
    hg_ref[...] = cols('hgrn')
    sbq_ref[...] = cols('sb_q').astype(BF16)
    sbk_ref[...] = cols('sb_k').astype(BF16)
    sbv_ref[...] = cols('sb_v').astype(BF16)


def _inproj_weights(w_in, w_uq, w_ukv):
    def seg(name, width):
        o = _IN_OFF[name]
        return w_in[:, o:o + width]

    kr = seg('mla_kr', MLA_ROPE)
    half = MLA_ROPE // 2
    z64 = jnp.zeros((D_MODEL, MLA_NOPE), F32)
    z32 = jnp.zeros((D_MODEL, LANES - MLA_NOPE - MLA_ROPE), F32)
    kra = jnp.concatenate([z64, kr, z32], axis=1)
    krb = jnp.concatenate([z64, kr[:, half:], kr[:, :half], z32], axis=1)
    swq = seg('swa_q', 256).reshape(D_MODEL, N_HEADS, HEAD_DIM)[:, jnp.array([0, 2, 1, 3])].reshape(D_MODEL, WIDTH)
    w_a = jnp.concatenate([
        seg('mla_cq', 256), seg('mla_ckv', 128), kra, krb,
        swq * QK_SCALE, seg('swa_k', 128), seg('swa_v', 128),
        seg('hgrn', 1024), seg('sb_q', 256) * (QK_SCALE * LOG2E), seg('sb_k', 256), seg('sb_v', 256)], axis=1)

    qd = MLA_NOPE + MLA_ROPE
    zq = jnp.zeros((MLA_Q_LORA, LANES - qd), F32)
    zn = jnp.zeros((MLA_Q_LORA, MLA_NOPE), F32)
    qa, qb = [], []
    for hh in range(N_HEADS):
        nope = w_uq[:, hh * qd: hh * qd + MLA_NOPE]
        rope = w_uq[:, hh * qd + MLA_NOPE: (hh + 1) * qd]
        qa += [nope, rope, zq]
        qb += [zn, rope[:, half:], rope[:, :half], zq]
    wuqa = jnp.concatenate(qa, axis=1)
    wuqb = jnp.concatenate(qb, axis=1)
    lane = jnp.arange(N_HEADS * LANES) % LANES
    wuk = jnp.where(lane[None, :] < MLA_NOPE, w_ukv, 0.0)
    wuv = jnp.concatenate([w_ukv[:, hh * LANES + MLA_NOPE:(hh + 1) * LANES] for hh in range(N_HEADS)], axis=1)
    return tuple(t.astype(BF16) for t in (w_a, wuqa, wuqb, wuk, wuv))


def _inproj(x2d, wts, ctab, stab, q_norm, kv_norm):
    n = x2d.shape[0]
    w_a, wuqa, wuqb, wuk, wuv = wts
    tm = TM_A
    row = lambda w: pl.BlockSpec((tm, w), lambda i: (i, 0))
    out_w = (512, 512, 256, 256, 128, 128, 1024, 256, 256, 256)
    out_dt = (BF16, BF16, BF16, BF16, BF16, BF16, F32, BF16, BF16, BF16)
    return pl.pallas_call(
        _inproj_kernel,
        grid=(n // tm,),
        in_specs=[row(D_MODEL), _const_spec(w_a.shape), row(LANES), row(LANES),
                  _const_spec((1, MLA_Q_LORA)), _const_spec((1, MLA_KV_LORA)),
                  _const_spec(wuqa.shape), _const_spec(wuqb.shape), _const_spec(wuk.shape),
                  _const_spec(wuv.shape)],
        out_specs=[row(w) for w in out_w],
        out_shape=[jax.ShapeDtypeStruct((n, w), d) for w, d in zip(out_w, out_dt)],
        compiler_params=_cparams(("parallel",)),
        name="inproj",
    )(x2d, w_a, ctab, stab, q_norm.reshape(1, -1), kv_norm.reshape(1, -1), wuqa, wuqb, wuk, wuv)


def _half_mask(half):
    lane = lax.broadcasted_iota(I32, (1, LANES), 1)
    return (lane < HEAD_DIM) if half == 0 else (lane >= HEAD_DIM)


def _mla_kernel(q_ref, k_ref, v_ref, o_ref):
    tq = q_ref.shape[0]
    tk = MLA_TK
    nsub = tq // tk
    i = pl.program_id(1)
    row = lax.broadcasted_iota(I32, (tq, tk), 0)
    col = lax.broadcasted_iota(I32, (tq, tk), 1)
    ones = jnp.ones((1, LANES), BF16)

    def update(off, carry, heads, mask, width=tk):
        ss = [_dot_nt(q_ref[:, hh * LANES:(hh + 1) * LANES],
                      k_ref[pl.ds(off, width), hh * LANES:(hh + 1) * LANES]) for hh in heads]
        if mask is not None:
            ss = [jnp.where(mask, s, NEG_BIG) for s in ss]
        ms = [jnp.maximum(c[0], jnp.max(s, axis=-1, keepdims=True)) for c, s in zip(carry, ss)]
        pms = [jnp.exp2(s - m).astype(BF16) for s, m in zip(ss, ms)]
        new = []
        for n, hh in enumerate(heads):
            vb = v_ref[pl.ds(off, width), (hh // 2) * LANES:(hh // 2 + 1) * LANES]
            vb = jnp.where(_half_mask(hh % 2), vb, ones)
            m, acc = carry[n]
            new.append((ms[n], jnp.exp2(m - ms[n]) * acc + _dot(pms[n], vb)))
        return tuple(new)

    accs = []
    for g in range(0, N_HEADS, MLA_GROUP):
        heads = tuple(range(g, g + MLA_GROUP))
        init = tuple((jnp.full((tq, 1), NEG_BIG, F32), jnp.zeros((tq, LANES), F32)) for _ in heads)
        nkb = i * nsub
        wide = MLA_WIDE * tk
        carry = lax.fori_loop(
            0, nkb // MLA_WIDE,
            lambda j, c, heads=heads: update(pl.multiple_of(j * wide, wide), c, heads, None, wide), init)
        done = nkb // MLA_WIDE * MLA_WIDE
        rest = nkb - done
        carry = lax.cond(
            rest >= 2,
            lambda c, heads=heads: update(pl.multiple_of(done * tk, 2 * tk), c, heads, None, 2 * tk),
            lambda c: c, carry)
        carry = lax.cond(
            rest % 2 == 1,
            lambda c, heads=heads: update(pl.multiple_of((nkb - 1) * tk, tk), c, heads, None),
            lambda c: c, carry)
        for r in range(nsub):
            carry = update(pl.multiple_of(i * tq + r * tk, tk), carry, heads, col + r * tk <= row)
        accs += [c[1] for c in carry]
    outs = []
    for p in range(N_HEADS // 2):
        a0, a1 = accs[2 * p], accs[2 * p + 1]
        outs.append(jnp.where(_half_mask(0), a0 / a0[:, HEAD_DIM:HEAD_DIM + 1], a1 / a1[:, 0:1]))
    o_ref[...] = jnp.concatenate(outs, axis=1).astype(o_ref.dtype)


def _mla_attention(q, k, v, batch, seq):
    tq = MLA_TQ
    q3, k3, v3 = (t.reshape(batch, seq, t.shape[-1]) for t in (q, k, v))
    out = pl.pallas_call(
        _mla_kernel,
        grid=(batch, seq // tq),
        in_specs=[pl.BlockSpec((None, tq, 512), lambda b, i: (b, i, 0)),
                  pl.BlockSpec((None, seq, 512), lambda b, i: (b, 0, 0), pipeline_mode=pl.Buffered(1)),
                  pl.BlockSpec((None, seq, WIDTH), lambda b, i: (b, 0, 0), pipeline_mode=pl.Buffered(1))],
        out_specs=pl.BlockSpec((None, tq, WIDTH), lambda b, i: (b, i, 0)),
        out_shape=jax.ShapeDtypeStruct((batch, seq, WIDTH), BF16),
        compiler_params=_cparams(("parallel", "arbitrary")),
        name="mla_attention",
    )(q3, k3, v3)
    return out.reshape(batch * seq, WIDTH)


def _sb_kernel(q_ref, k_ref, v_ref, o_ref):
    tq = q_ref.shape[0]
    i = pl.program_id(1)
    row = lax.broadcasted_iota(I32, (tq, tq), 0)
    col = lax.broadcasted_iota(I32, (tq, tq), 1)
    strict = col < row
    later = (row > col).astype(BF16)
    qs = []
    for hh in range(N_HEADS):
        qp = q_ref[:, (hh // 2) * LANES:(hh // 2 + 1) * LANES]
        qs.append(jnp.where(_half_mask(hh % 2), qp, jnp.zeros_like(qp)))

    def block(j, carry, diag):
        off = pl.multiple_of(j * tq, tq)
        runs, accs = carry
        heads = range(N_HEADS)
        zs = [_dot_nt(qs[hh], k_ref[pl.ds(off, tq), (hh // 2) * LANES:(hh // 2 + 1) * LANES]) for hh in heads]
        lsps = [jnp.minimum(z, 0.0) - jnp.log2(1.0 + jnp.exp2(-jnp.abs(z))) for z in zs]
        lsns = [lsp - z for lsp, z in zip(lsps, zs)]
        if diag:
            lsns = [jnp.where(strict, t, 0.0) for t in lsns]
        his = [t.astype(BF16) for t in lsns]
        los = [(t - hi.astype(F32)).astype(BF16) for t, hi in zip(lsns, his)]
        rems = [_dot(hi, later) + _dot(lo, later) for hi, lo in zip(his, los)]
        args = [lsps[hh] + rems[hh] + runs[hh] for hh in heads]
        if diag:
            args = [jnp.where(strict, t, NEG_BIG) for t in args]
        probs = [jnp.exp2(t).astype(BF16) for t in args]
        new_runs = tuple(runs[hh] + rems[hh][:, 0:1] + lsns[hh][:, 0:1] for hh in heads)
        new_accs = list(accs)
        for hh in heads:
            p = hh // 2
            vb = v_ref[pl.ds(off, tq), p * LANES:(p + 1) * LANES]
            vb = jnp.where(_half_mask(hh % 2), vb, jnp.zeros_like(vb))
            new_accs[p] = new_accs[p] + _dot(probs[hh], vb)
        return new_runs, tuple(new_accs)

    init = (tuple(jnp.zeros((tq, 1), F32) for _ in range(N_HEADS)),
            tuple(jnp.zeros((tq, LANES), F32) for _ in range(N_HEADS // 2)))
    def still_active(runs):
        top = functools.reduce(jnp.maximum, runs)
        return (jnp.max(top) > SB_RUN_FLOOR).astype(I32)

    runs, accs = block(i, init, True)

    def cond(c):
        return (c[0] < i) & (c[1] > 0)

    def body(c):
        jj, _, runs, accs = c
        runs, accs = block(i - 1 - jj, (runs, accs), False)
        return jj + 1, still_active(runs), runs, accs

    _, _, _, accs = lax.while_loop(cond, body, (jnp.int32(0), still_active(runs), runs, accs))
    o_ref[...] = jnp.concatenate(accs, axis=1).astype(o_ref.dtype)


def _sb_attention(q, k, v, batch, seq):
    tq = TQ_ATT
    q3, k3, v3 = (t.reshape(batch, seq, WIDTH) for t in (q, k, v))
    out = pl.pallas_call(
        _sb_kernel,
        grid=(batch, seq // tq),
        in_specs=[pl.BlockSpec((None, tq, WIDTH), lambda b, i: (b, i, 0)),
                  pl.BlockSpec((None, seq, WIDTH), lambda b, i: (b, 0, 0)),
                  pl.BlockSpec((None, seq, WIDTH), lambda b, i: (b, 0, 0))],
        out_specs=pl.BlockSpec((None, tq, WIDTH), lambda b, i: (b, i, 0)),
        out_shape=jax.ShapeDtypeStruct((batch, seq, WIDTH), BF16),
        compiler_params=_cparams(("parallel", "arbitrary")),
        name="stick_breaking",
    )(q3, k3, v3)
    return out.reshape(batch * seq, WIDTH)


def _rel_bucket(dist):
    exact = REL_BUCKETS // 2
    n = jnp.maximum(dist, 0)
    nf = jnp.maximum(n, 1).astype(F32)
    large = exact + (jnp.log(nf / exact) / math.log(REL_MAX_DIST / exact) * (REL_BUCKETS - exact)).astype(I32)
    large = jnp.clip(large, 0, REL_BUCKETS - 1)
    return jnp.where(n < exact, n, large)


def _swa_kernel(sink_ref, tab_ref, q_ref, kc_ref, kh_ref, vc_ref, vh_ref, pq_ref, pkc_ref, pkh_ref, o_ref):
    w = SWA_WINDOW
    step = pl.program_id(1)
    row = lax.broadcasted_iota(I32, (w, w), 0)
    col = lax.broadcasted_iota(I32, (w, w), 1)
    valid_c = col <= row
    valid_p = col > row
    tabs = [jnp.broadcast_to(tab_ref[hh:hh + 1, :], (w, LANES)) for hh in range(N_HEADS)]
    ones = jnp.ones((1, LANES), BF16)
    nsub = q_ref.shape[0] // w
    chains = [(r, hh) for r in range(nsub) for hh in range(N_HEADS)]

    def real(hh):
        return (hh % 2) * 2 + hh // 2

    def keys(ref, halo_ref, r):
        cur = ref[r * w:(r + 1) * w, :]
        prev = ref[(r - 1) * w:r * w, :] if r else halo_ref[...]
        return cur, prev

    buckets = []
    for r in range(nsub):
        pq = pq_ref[r * w:(r + 1) * w, :]
        pk_prev = pkc_ref[:, (r - 1) * w:r * w] if r else pkh_ref[...]
        buckets.append((_rel_bucket(pq - pkc_ref[:, r * w:(r + 1) * w]), _rel_bucket(pq - pk_prev)))
    logits = []
    for r, hh in chains:
        qp = q_ref[r * w:(r + 1) * w, (hh // 2) * LANES:(hh // 2 + 1) * LANES]
        qh = jnp.where(_half_mask(hh % 2), qp, jnp.zeros_like(qp))
        kc, kp = keys(kc_ref, kh_ref, r)
        logits.append((_dot_nt(qh, kc), _dot_nt(qh, kp)))
    masked = []
    for (r, hh), (lc, lp) in zip(chains, logits):
        lc = jnp.where(valid_c, lc + jnp.take_along_axis(tabs[real(hh)], buckets[r][0], axis=1), NEG_BIG)
        lp = lp + jnp.take_along_axis(tabs[real(hh)], buckets[r][1], axis=1)
        lp = jnp.where(valid_p if r else valid_p & (step > 0), lp, NEG_BIG)
        masked.append((lc, lp))
    maxes = [jnp.maximum(jnp.maximum(jnp.max(lc, axis=-1, keepdims=True), jnp.max(lp, axis=-1, keepdims=True)),
                         sink_ref[real(hh)]) for (r, hh), (lc, lp) in zip(chains, masked)]
    probs = [(jnp.exp(lc - m).astype(BF16), jnp.exp(lp - m).astype(BF16)) for (lc, lp), m in zip(masked, maxes)]
    outs = {}
    for (r, hh), (ec, ep), m in zip(chains, probs, maxes):
        vc, vp = keys(vc_ref, vh_ref, r)
        mine = _half_mask(hh % 2)
        acc = _dot(ec, jnp.where(mine, vc, ones)) + _dot(ep, jnp.where(mine, vp, ones))
        den = (acc[:, 0:1] if hh % 2 else acc[:, HEAD_DIM:HEAD_DIM + 1]) + jnp.exp(sink_ref[real(hh)] - m)
        outs[(r, hh)] = acc / den
    for r in range(nsub):
        pairs = [jnp.where(_half_mask(0), outs[(r, 2 * p)], outs[(r, 2 * p + 1)]) for p in range(N_HEADS // 2)]
        o_ref[r * w:(r + 1) * w, :] = jnp.concatenate(pairs, axis=1).astype(o_ref.dtype)


def _swa_attention(q, k, v, positions, sinks, rel_table, batch, seq):
    w = SWA_WINDOW
    tq = SWA_TQ
    per = tq // w
    kvw = k.shape[-1]
    q3, k3, v3 = (t.reshape(batch, seq, t.shape[-1]) for t in (q, k, v))
    pcol = positions.reshape(batch, seq, 1)
    prow = positions.reshape(batch, 1, seq)
    tab = jnp.zeros((N_HEADS, LANES), F32).at[:, :REL_BUCKETS].set(rel_table.astype(F32).T)
    cur = lambda b, n: (b, n, 0)
    halo = lambda b, n: (b, jnp.maximum(n * per - 1, 0), 0)
    out = pl.pallas_call(
        _swa_kernel,
        grid=(batch, seq // tq),
        in_specs=[pl.BlockSpec(memory_space=pltpu.SMEM), _const_spec((N_HEADS, LANES)),
                  pl.BlockSpec((None, tq, WIDTH), cur),
                  pl.BlockSpec((None, tq, kvw), cur), pl.BlockSpec((None, w, kvw), halo),
                  pl.BlockSpec((None, tq, kvw), cur), pl.BlockSpec((None, w, kvw), halo),
                  pl.BlockSpec((None, tq, 1), cur),
                  pl.BlockSpec((None, 1, tq), lambda b, n: (b, 0, n)),
                  pl.BlockSpec((None, 1, w), lambda b, n: (b, 0, jnp.maximum(n * per - 1, 0)))],
        out_specs=pl.BlockSpec((None, tq, WIDTH), cur),
        out_shape=jax.ShapeDtypeStruct((batch, seq, WIDTH), BF16),
        compiler_params=_cparams(("parallel", "arbitrary")),
        name="swa_attention",
    )(sinks.astype(F32), tab, q3, k3, k3, v3, v3, pcol, prow, prow)
    return out.reshape(batch * seq, WIDTH)


def _hgrn_kernel(hg_ref, lb_ref, nw_ref, o_ref, state_ref):
    c = HGRN_CHUNK
    blk = HGRN_BLOCK

    @pl.when(pl.program_id(1) == 0)
    def _():
        state_ref[...] = jnp.zeros_like(state_ref)

    r64 = lax.broadcasted_iota(I32, (c, c), 0)
    c64 = lax.broadcasted_iota(I32, (c, c), 1)
    incl = (c64 <= r64).astype(BF16)
    ra = lax.broadcasted_iota(I32, (WIDTH, WIDTH), 0) // HEAD_DIM
    ca = lax.broadcasted_iota(I32, (WIDTH, WIDTH), 1) // HEAD_DIM
    same_head = ra == ca
    seg = same_head.astype(BF16)
    ones_cols = jnp.ones((c, LANES), BF16)
    trow = lax.broadcasted_iota(I32, (blk, WIDTH), 0)
    caps = [jnp.where(trow >= s_i, 0.0, NEG_BIG) for s_i in range(blk)]
    lane_head = lax.broadcasted_iota(I32, (1, WIDTH), 1) // HEAD_DIM
    lb = lb_ref[...]
    nw = nw_ref[...]
    dn0 = (((0,), (0,)), ((), ()))

    for ch in range(hg_ref.shape[0] // c):
        rows = slice(ch * c, (ch + 1) * c)
        qraw = hg_ref[rows, 0:WIDTH]
        fraw = hg_ref[rows, WIDTH:2 * WIDTH]
        v = hg_ref[rows, 2 * WIDTH:3 * WIDTH]
        graw = hg_ref[rows, 3 * WIDTH:4 * WIDTH]
        qf = qraw * jax.nn.sigmoid(qraw)
        forget = lb + (1.0 - lb) * jax.nn.sigmoid(fraw)
        lf = jnp.log(forget)
        kk = 1.0 - forget
        gate = graw * jax.nn.sigmoid(graw)
        vb = v.astype(BF16)

        lf3 = _split3(lf)
        bc = _dot(incl, lf3[0]) + _dot(incl, lf3[1]) + _dot(incl, lf3[2])
        b_last = bc[c - 1:c, :]
        tot_col = sum(lax.dot_general(t, ones_cols, dn0, preferred_element_type=F32) for t in lf3)
        decay_col = jnp.exp(jnp.concatenate([tot_col, tot_col], axis=1))

        state = state_ref[...]
        o_inter = _dot((qf * jnp.exp(bc)).astype(BF16), state.astype(BF16))

        def before(qa, qb, ka, kb):
            ref = bc[kb - 1:kb, :]
            qt = qf[qa:qb] * jnp.exp(bc[qa:qb] - ref)
            kt = (kk[ka:kb] * jnp.exp(ref - bc[ka:kb])).astype(BF16)
            qs = jnp.concatenate([jnp.where(lane_head == hh, qt, 0.0) for hh in range(N_HEADS)], axis=0)
            att = _dot_nt(qs.astype(BF16), kt)
            mix = _dot(att.astype(BF16), vb[ka:kb])
            nq = qb - qa
            return sum(jnp.where(lane_head == hh, mix[hh * nq:(hh + 1) * nq], 0.0) for hh in range(N_HEADS))

        bc2 = bc * LOG2E
        key2 = bc2 - jnp.log2(jnp.maximum(kk, 0.0))

        def inside(a):
            b2 = bc2[a:a + blk]
            qb_ = qf[a:a + blk]
            ws = []
            for s_i in range(blk):
                ws.append(qb_ * jnp.exp2(jnp.minimum(b2 - key2[a + s_i:a + s_i + 1, :], caps[s_i])))
            att = _dot(jnp.concatenate(ws, axis=0).astype(BF16), seg)
            return sum(att[s_i * blk:(s_i + 1) * blk] * v[a + s_i:a + s_i + 1, :] for s_i in range(blk))

        pieces = {a: [] for a in range(0, c, blk)}

        def cover(a, b):
            if b - a == blk:
                pieces[a].append(inside(a))
                return
            mid = (a + b) // 2
            cover(a, mid)
            cover(mid, b)
            res = before(mid, b, a, mid)
            for off in range(0, b - mid, blk):
                pieces[mid + off].append(res[off:off + blk])

        cover(0, c)
        o = o_inter + jnp.concatenate([sum(pieces[a]) for a in range(0, c, blk)], axis=0)

        khat = (kk * jnp.exp(b_last - bc)).astype(BF16)
        upd = lax.dot_general(khat, vb, dn0, preferred_element_type=F32)
        state_ref[...] = decay_col * state + jnp.where(same_head, upd, 0.0)

        o2 = _split3(o * o)
        ms = (_dot(o2[0], seg) + _dot(o2[1], seg)) * (1.0 / HEAD_DIM)
        o_ref[rows, :] = (o * lax.rsqrt(ms + EPS) * nw * gate).astype(o_ref.dtype)


def _hgrn(hg, lower_bound, norm_w, batch, seq):
    rows = HG_ROWS
    hg3 = hg.reshape(batch, seq, 4 * WIDTH)
    out = pl.pallas_call(
        _hgrn_kernel,
        grid=(batch, seq // rows),
        in_specs=[pl.BlockSpec((None, rows, 4 * WIDTH), lambda b, i: (b, i, 0)),
                  _const_spec((1, WIDTH)), _const_spec((1, WIDTH))],
        out_specs=pl.BlockSpec((None, rows, WIDTH), lambda b, i: (b, i, 0)),
        out_shape=jax.ShapeDtypeStruct((batch, seq, WIDTH), BF16),
        scratch_shapes=[pltpu.VMEM((WIDTH, WIDTH), F32)],
        compiler_params=_cparams(("parallel", "arbitrary")),
        name="hgrn2",
    )(hg3, lower_bound.reshape(1, WIDTH).astype(F32), norm_w.reshape(1, WIDTH).astype(F32))
    return out.reshape(batch * seq, WIDTH)


def _merge_body(x, y_refs, wg_ref, wb_ref, wo_ref, g, b):
    xb = x.astype(BF16)
    merged = jnp.zeros(x.shape, F32)
    for nbr, y_ref in enumerate(y_refs):
        gate = jax.nn.sigmoid(_dot(xb, wg_ref[:, nbr * D_MODEL:(nbr + 1) * D_MODEL]))
        merged = merged + gate * _dot(y_ref[...], wb_ref[nbr])
    y = _dot(merged.astype(BF16), wo_ref[...])
    return _layernorm(ALPHA * x + y, g, b)


def _memkv_kernel(m_ref, w_ref, k_ref, v_ref):
    kv = _dot(m_ref[...].astype(BF16), w_ref[...])
    k_ref[...] = kv[:, :WIDTH].astype(BF16)
    v_ref[...] = kv[:, WIDTH:].astype(BF16)


def _memkv(mem, wkv):
    batch, m, _ = mem.shape
    return pl.pallas_call(
        _memkv_kernel,
        grid=(batch,),
        in_specs=[pl.BlockSpec((None, m, D_MODEL), lambda b: (b, 0, 0)), _const_spec(wkv.shape)],
        out_specs=[pl.BlockSpec((None, m, WIDTH), lambda b: (b, 0, 0))] * 2,
        out_shape=[jax.ShapeDtypeStruct((batch, m, WIDTH), BF16)] * 2,
        compiler_params=_cparams(("parallel",)),
        name="mem_kv",
    )(mem, wkv)


def _xattn_body(x, wq_ref, k, v, wo_ref, g, b):
    q = _dot(x.astype(BF16), wq_ref[...]).astype(BF16)
    lane = lax.broadcasted_iota(I32, (1, WIDTH), 1) // HEAD_DIM
    heads = range(N_HEADS)
    ss = [_dot_nt(jnp.where(lane == hh, q, jnp.zeros_like(q)), k) for hh in heads]
    es = [jnp.exp(s - jnp.max(s, axis=-1, keepdims=True)) for s in ss]
    ps = [(e / jnp.sum(e, axis=-1, keepdims=True)).astype(BF16) for e in es]
    o = jnp.zeros((x.shape[0], WIDTH), F32)
    for hh in heads:
        o = o + jnp.where(lane == hh, _dot(ps[hh], v), 0.0)
    y = _dot(o.astype(BF16), wo_ref[...])
    return _layernorm(ALPHA * x + y, g, b)


def _merge_xattn_kernel(x_ref, y0_ref, y1_ref, y2_ref, y3_ref, wg_ref, wb_ref, wo_ref, g1_ref, b1_ref,
                        wq_ref, k_ref, v_ref, xwo_ref, g2_ref, b2_ref, o_ref):
    x1 = _merge_body(x_ref[...], (y0_ref, y1_ref, y2_ref, y3_ref), wg_ref, wb_ref, wo_ref,
                     g1_ref[...], b1_ref[...])
    o_ref[...] = _xattn_body(x1, wq_ref, k_ref[...], v_ref[...], xwo_ref, g2_ref[...], b2_ref[...])


def _merge_xattn(x2d, ys, wg, wb, wo, g1, b1, wq, k, v, xwo, g2, b2, batch, seq):
    tm = TM_A
    m = k.shape[1]
    per = seq // tm
    row = lambda w: pl.BlockSpec((tm, w), lambda bb, i: (bb * per + i, 0))
    kv_spec = pl.BlockSpec((None, m, WIDTH), lambda bb, i: (bb, 0, 0))
    vec = _const_spec((1, D_MODEL))
    return pl.pallas_call(
        _merge_xattn_kernel,
        grid=(batch, per),
        in_specs=[row(D_MODEL)] + [row(WIDTH)] * 4 +
                 [_const_spec(wg.shape), _const_spec(wb.shape), _const_spec(wo.shape), vec, vec,
                  _const_spec(wq.shape), kv_spec, kv_spec, _const_spec(xwo.shape), vec, vec],
        out_specs=row(D_MODEL),
        out_shape=jax.ShapeDtypeStruct((batch * seq, D_MODEL), F32),
        compiler_params=_cparams(("parallel", "parallel")),
        name="merge_xattn_ln",
    )(x2d, *ys, wg, wb, wo, g1.reshape(1, -1), b1.reshape(1, -1),
      wq, k, v, xwo, g2.reshape(1, -1), b2.reshape(1, -1))


def _ffn_kernel(x_ref, w13_ref, w2_ref, g_ref, b_ref, o_ref):
    x = x_ref[...]
    xb = x.astype(BF16)
    y = None
    for h in range(F_DENSE // TF_FFN):
        lo = h * TF_FFN
        a = _dot(xb, w13_ref[:, lo:lo + TF_FFN])
        gate = _dot(xb, w13_ref[:, F_DENSE + lo:F_DENSE + lo + TF_FFN])
        part = _dot((a * jax.nn.sigmoid(a) * gate).astype(BF16), w2_ref[lo:lo + TF_FFN, :])
        y = part if y is None else y + part
    o_ref[...] = _layernorm(ALPHA * x + y, g_ref[...], b_ref[...])


def _ffn(x2d, w13, w2, g, b):
    n = x2d.shape[0]
    tm = TM_FFN
    return pl.pallas_call(
        _ffn_kernel,
        grid=(n // tm,),
        in_specs=[pl.BlockSpec((tm, D_MODEL), lambda i: (i, 0)),
                  _const_spec(w13.shape), _const_spec(w2.shape),
                  _const_spec((1, D_MODEL)), _const_spec((1, D_MODEL))],
        out_specs=pl.BlockSpec((tm, D_MODEL), lambda i: (i, 0)),
        out_shape=jax.ShapeDtypeStruct((n, D_MODEL), F32),
        compiler_params=_cparams(("parallel",)),
        name="ffn_ln",
    )(x2d, w13, w2, g.reshape(1, -1), b.reshape(1, -1))


def _router_kernel(x_ref, r_ref, info_ref, wts_ref, cnt_ref, carry_ref):
    tm = x_ref.shape[0]

    @pl.when(pl.program_id(0) == 0)
    def _():
        carry_ref[...] = jnp.zeros_like(carry_ref)

    logits = jnp.dot(x_ref[...], r_ref[...], precision=lax.Precision.HIGHEST, preferred_element_type=F32)
    lane = lax.broadcasted_iota(I32, (tm, LANES), 1)
    lg = jnp.where(lane < N_EXPERTS, logits, -jnp.inf)
    m1 = jnp.max(lg, axis=-1, keepdims=True)
    i1 = jnp.min(jnp.where(lg == m1, lane, LANES), axis=-1, keepdims=True)
    lg2 = jnp.where(lane == i1, -jnp.inf, lg)
    m2 = jnp.max(lg2, axis=-1, keepdims=True)
    i2 = jnp.min(jnp.where(lg2 == m2, lane, LANES), axis=-1, keepdims=True)
    e = jnp.exp(m2 - m1)
    w1 = 1.0 / (1.0 + e)
    w2 = e / (1.0 + e)
    sel1 = lane == i1
    sel2 = lane == i2
    chosen = jnp.where(sel1 | sel2, 1.0, 0.0)
    row = lax.broadcasted_iota(I32, (tm, tm), 0)
    col = lax.broadcasted_iota(I32, (tm, tm), 1)
    before = (col < row).astype(BF16)
    ranks = _dot(before, chosen.astype(BF16)) + carry_ref[...]
    r1 = jnp.sum(jnp.where(sel1, ranks, 0.0), axis=-1, keepdims=True)
    r2 = jnp.sum(jnp.where(sel2, ranks, 0.0), axis=-1, keepdims=True)
    carry_ref[...] = carry_ref[...] + jnp.sum(chosen, axis=0, keepdims=True)
    info = jnp.where(lane == 0, i1.astype(F32), jnp.where(lane == 1, i2.astype(F32),
                     jnp.where(lane == 2, r1, jnp.where(lane == 3, r2, 0.0))))
    info_ref[...] = jnp.transpose(info)[:info_ref.shape[0], :]
    wts_ref[...] = jnp.where(lane == 0, w1, jnp.where(lane == 1, w2, 0.0))
    cnt_ref[...] = carry_ref[...]


def _router(x2d, router):
    n = x2d.shape[0]
    tm = TM_ROUTER
    r_pad = jnp.zeros((D_MODEL, LANES), F32).at[:, :N_EXPERTS].set(router.astype(F32))
    row = pl.BlockSpec((tm, LANES), lambda i: (i, 0))
    return pl.pallas_call(
        _router_kernel,
        grid=(n // tm,),
        in_specs=[pl.BlockSpec((tm, D_MODEL), lambda i: (i, 0)), _const_spec(r_pad.shape)],
        out_specs=[pl.BlockSpec((8, tm), lambda i: (0, i)), row, pl.BlockSpec((1, LANES), lambda i: (0, 0))],
        out_shape=[jax.ShapeDtypeStruct((8, n), F32), jax.ShapeDtypeStruct((n, LANES), F32),
                   jax.ShapeDtypeStruct((1, LANES), F32)],
        scratch_shapes=[pltpu.VMEM((1, LANES), F32)],
        compiler_params=_cparams(("arbitrary",)),
        name="moe_router",
    )(x2d, r_pad)


def _dispatch_kernel(pad_ref, d0_ref, d1_ref, x_ref, xb_hbm, stage_ref, sems):
    tm = x_ref.shape[0]
    i = pl.program_id(0)
    last = pl.num_programs(0) - 1
    slot = i % 2

    def wait_step(s):
        for _ in range(2):
            pltpu.make_async_copy(stage_ref.at[s], xb_hbm.at[pl.ds(0, tm), :], sems.at[s]).wait()

    @pl.when(i >= 2)
    def _():
        wait_step(slot)

    for s in range(2):
        @pl.when(slot == s)
        def _():
            stage_ref[s] = x_ref[...]

            def issue(r, c):
                for k in range(2):
                    pltpu.make_async_copy(stage_ref.at[s, pl.ds(r, 1), :],
                                          xb_hbm.at[pl.ds((d0_ref, d1_ref)[k][0, r], 1), :], sems.at[s]).start()
                return c
            lax.fori_loop(0, tm, issue, 0, unroll=8)

    @pl.when(i == last)
    def _():
        def fill(e, c):
            def one(s, c2):
                pltpu.make_async_copy(stage_ref.at[slot, pl.ds(0, 1), :], xb_hbm.at[pl.ds(s, 1), :],
                                      sems.at[2]).start()
                return c2

            def done(s, c2):
                pltpu.make_async_copy(stage_ref.at[slot, pl.ds(0, 1), :], xb_hbm.at[pl.ds(0, 1), :],
                                      sems.at[2]).wait()
                return c2
            lax.fori_loop(pad_ref[0, e], pad_ref[1, e], one, 0)
            lax.fori_loop(pad_ref[0, e], pad_ref[1, e], done, 0)
            return c
        lax.fori_loop(0, pad_ref.shape[1], fill, 0)
        wait_step(slot)

        @pl.when(last >= 1)
        def _():
            wait_step(1 - slot)


def _dispatch(x2d, dest, pads, nblk):
    n = x2d.shape[0]
    tm = TM_DISP
    nt = n // tm
    grid_spec = pltpu.PrefetchScalarGridSpec(
        num_scalar_prefetch=1,
        grid=(nt,),
        in_specs=[pl.BlockSpec((None, 1, tm), lambda i, pads: (i, 0, 0), memory_space=pltpu.SMEM),
                  pl.BlockSpec((None, 1, tm), lambda i, pads: (i, 0, 0), memory_space=pltpu.SMEM),
                  pl.BlockSpec((tm, D_MODEL), lambda i, pads: (i, 0))],
        out_specs=pl.BlockSpec(memory_space=pl.ANY),
        scratch_shapes=[pltpu.VMEM((2, tm, D_MODEL), F32), pltpu.SemaphoreType.DMA((3,))],
    )
    return pl.pallas_call(
        _dispatch_kernel,
        grid_spec=grid_spec,
        out_shape=jax.ShapeDtypeStruct((nblk * MOE_TB, D_MODEL), F32),
        compiler_params=_cparams(("arbitrary",), disable_bounds_checks=True),
        name="moe_dispatch",
    )(pads, dest[0].reshape(nt, 1, tm), dest[1].reshape(nt, 1, tm), x2d)


def _expert_kernel(nused_ref, bexp_ref, x_ref, w1_ref, w3_ref, w2_ref, o_ref, acc_ref):
    f = pl.program_id(1)

    @pl.when(pl.program_id(0) < nused_ref[0])
    def _():
        xb = x_ref[...].astype(BF16)
        a = _dot(xb, w1_ref[...])
        gate = _dot(xb, w3_ref[...])
        part = _dot((a * jax.nn.sigmoid(a) * gate).astype(BF16), w2_ref[...])

        @pl.when(f == 0)
        def _():
            acc_ref[...] = part

        @pl.when(f > 0)
        def _():
            acc_ref[...] += part

        @pl.when(f == pl.num_programs(1) - 1)
        def _():
            o_ref[...] = acc_ref[...]

    @pl.when(pl.program_id(0) >= nused_ref[0])
    def _():
        o_ref[...] = jnp.zeros_like(o_ref)


def _experts(xb, w13, w2, nused, blk_exp, nblk):
    tb, tf = MOE_TB, MOE_TF
    nf = F_EXPERT // tf
    w13t = w13.astype(BF16)

    def blk(i, nu):
        return jnp.maximum(jnp.minimum(i, nu[0] - 1), 0)

    def ftile(i, f, nu):
        return jnp.where(i < nu[0], f, nf - 1)

    grid_spec = pltpu.PrefetchScalarGridSpec(
        num_scalar_prefetch=2,
        grid=(nblk, nf),
        in_specs=[pl.BlockSpec((tb, D_MODEL), lambda i, f, nu, be: (blk(i, nu), 0)),
                  pl.BlockSpec((None, D_MODEL, tf), lambda i, f, nu, be: (be[blk(i, nu)], 0, ftile(i, f, nu))),
                  pl.BlockSpec((None, D_MODEL, tf), lambda i, f, nu, be: (be[blk(i, nu)], 0, nf + ftile(i, f, nu))),
                  pl.BlockSpec((None, tf, D_MODEL), lambda i, f, nu, be: (be[blk(i, nu)], ftile(i, f, nu), 0))],
        out_specs=pl.BlockSpec((tb, D_MODEL), lambda i, f, nu, be: (i, 0)),
        scratch_shapes=[pltpu.VMEM((tb, D_MODEL), F32)],
    )
    return pl.pallas_call(
        _expert_kernel,
        grid_spec=grid_spec,
        out_shape=jax.ShapeDtypeStruct((nblk * tb, D_MODEL), F32),
        compiler_params=_cparams(("arbitrary", "arbitrary")),
        name="moe_experts",
    )(nused, blk_exp, xb, w13t, w13t, w2.astype(BF16))


def _combine_kernel(d0_ref, d1_ref, n0_ref, n1_ref, y_hbm, x_ref, wts_ref, g_ref, b_ref, o_ref, buf_ref, sems):
    tm = x_ref.shape[0]
    i = pl.program_id(0)
    slot = i % 2

    def gather(idx_refs, s):
        def issue(r, c):
            for k in range(2):
                pltpu.make_async_copy(y_hbm.at[pl.ds(idx_refs[k][0, r], 1), :],
                                      buf_ref.at[s, k, pl.ds(r, 1), :], sems.at[s]).start()
            return c
        lax.fori_loop(0, tm, issue, 0, unroll=8)

    @pl.when(i == 0)
    def _():
        gather((d0_ref, d1_ref), 0)

    for s in range(2):
        @pl.when((i + 1 < pl.num_programs(0)) & (slot != s))
        def _():
            gather((n0_ref, n1_ref), s)

    for k in range(2):
        pltpu.make_async_copy(y_hbm.at[pl.ds(0, tm), :], buf_ref.at[slot, k], sems.at[slot]).wait()
    wts = wts_ref[...]
    y = wts[:, 0:1] * buf_ref[slot, 0] + wts[:, 1:2] * buf_ref[slot, 1]
    o_ref[...] = _layernorm(ALPHA * x_ref[...] + y, g_ref[...], b_ref[...])


def _combine(yb, dest, x2d, wts, g, b):
    n = x2d.shape[0]
    tm = TM_COMB
    nt = n // tm
    row = lambda w: pl.BlockSpec((tm, w), lambda i: (i, 0))
    cur = pl.BlockSpec((None, 1, tm), lambda i: (i, 0, 0), memory_space=pltpu.SMEM)
    nxt = pl.BlockSpec((None, 1, tm), lambda i: (jnp.minimum(i + 1, nt - 1), 0, 0), memory_space=pltpu.SMEM)
    d0, d1 = (d.reshape(nt, 1, tm) for d in dest)
    return pl.pallas_call(
        _combine_kernel,
        grid=(nt,),
        in_specs=[cur, cur, nxt, nxt, pl.BlockSpec(memory_space=pl.ANY), row(D_MODEL), row(LANES),
                  _const_spec((1, D_MODEL)), _const_spec((1, D_MODEL))],
        out_specs=row(D_MODEL),
        out_shape=jax.ShapeDtypeStruct((n, D_MODEL), F32),
        scratch_shapes=[pltpu.VMEM((2, 2, tm, D_MODEL), F32), pltpu.SemaphoreType.DMA((2,))],
        compiler_params=_cparams(("arbitrary",), disable_bounds_checks=True),
        name="moe_combine_ln",
    )(d0, d1, d0, d1, yb, x2d, wts, g.reshape(1, -1), b.reshape(1, -1))


def _moe(x2d, router, w13, w2, g, b):
    n = x2d.shape[0]
    tb = MOE_TB
    info, wts, cnt = _router(x2d, router)
    counts = cnt[0, :N_EXPERTS].astype(I32)
    padded = (counts + tb - 1) // tb * tb
    pend = jnp.cumsum(padded)
    pstart = pend - padded
    info = info.astype(I32)
    dest = tuple((sum(jnp.where(info[k] == e, pstart[e], 0) for e in range(N_EXPERTS)) + info[2 + k]).astype(I32)
                 for k in range(2))
    nblk = -(-(2 * n + N_EXPERTS * (tb - 1)) // tb)
    pads = jnp.stack([jnp.append(pstart + counts, pend[-1]), jnp.append(pend, nblk * tb)]).astype(I32)
    nused = (pend[-1] // tb).astype(I32).reshape(1)
    first_row = jnp.arange(nblk, dtype=I32) * tb
    blk_exp = jnp.minimum(jnp.sum(pend[None, :] <= first_row[:, None], axis=1), N_EXPERTS - 1).astype(I32)
    xb = _dispatch(x2d, dest, pads, nblk)
    yb = _experts(xb, w13, w2, nused, blk_exp, nblk)
    return _combine(yb, dest, x2d, wts, g, b)


def kernel(x, mem, positions, rel_bias_table, hgrn_lb_logits, w_in, mla_q_norm, mla_w_uq, mla_kv_norm, mla_w_ukv, swa_sinks, hgrn_norm, w_branch, w_out, ln_g, ln_b, xa_wq, xa_wkv, xa_wo, ffn_w13, ffn_w2, moe_router, moe_w13, moe_w2):
    batch, seq, _ = x.shape
    n = batch * seq
    sm = jax.nn.softmax(hgrn_lb_logits.astype(F32), axis=0)
    lower_bounds = jnp.cumsum(sm, axis=0) - sm[0]
    ctab, stab = _rope_tables(positions)
    xc = x.reshape(n, D_MODEL)
    for l in range(DEPTH):
        wts = _inproj_weights(w_in[l], mla_w_uq[l], mla_w_ukv[l])
        mq, mk, mv, swq, swk, swv, hg, sbq, sbk, sbv = _inproj(xc, wts, ctab, stab, mla_q_norm[l], mla_kv_norm[l])
        y_mla = _mla_attention(mq, mk, mv, batch, seq)
        y_swa = _swa_attention(swq, swk, swv, positions, swa_sinks[l], rel_bias_table, batch, seq)
        y_hg = _hgrn(hg, lower_bounds[l], hgrn_norm[l], batch, seq)
        y_sb = _sb_attention(sbq, sbk, sbv, batch, seq)
        go = _IN_OFF['gates']
        mk_, mv_ = _memkv(mem, xa_wkv[l].astype(BF16))
        wb = w_branch[l].at[1].set(
            w_branch[l][1].reshape(N_HEADS, HEAD_DIM, D_MODEL)[jnp.array([0, 2, 1, 3])].reshape(WIDTH, D_MODEL))
        xc = _merge_xattn(xc, (y_mla, y_swa, y_hg, y_sb), w_in[l][:, go:].astype(BF16), wb.astype(BF16),
                          w_out[l].astype(BF16), ln_g[l, 0], ln_b[l, 0],
                          (xa_wq[l] * QK_SCALE).astype(BF16), mk_, mv_, xa_wo[l].astype(BF16),
                          ln_g[l, 1], ln_b[l, 1], batch, seq)
        if l % 2 == 0:
            xc = _ffn(xc, ffn_w13[l // 2].astype(BF16), ffn_w2[l // 2].astype(BF16), ln_g[l, 2], ln_b[l, 2])
        else:
            xc = _moe(xc, moe_router[l // 2], moe_w13[l // 2], moe_w2[l // 2],
                      ln_g[l, 2], ln_b[l, 2])
    return xc.reshape(batch, seq, D_MODEL)
```

```python
import functools
import math

import jax
import jax.numpy as jnp
from jax import lax
from jax.experimental import pallas as pl
from jax.experimental.pallas import tpu as pltpu

F32 = jnp.float32
BF16 = jnp.bfloat16
I32 = jnp.int32

D_MODEL = 1024
DEPTH = 2
EPS = 1e-5
NEG_BIG = -1e30
LANES = 128
HEAD_DIM = 64
N_HEADS = 4
WIDTH = N_HEADS * HEAD_DIM

MLA_Q_LORA = 256
MLA_KV_LORA = 128
MLA_NOPE = 64
MLA_ROPE = 32
ROPE_THETA = 10000.0
MLA_SCALE = (MLA_NOPE + MLA_ROPE) ** -0.5
LOG2E = math.log2(math.e)
QK_SCALE = HEAD_DIM ** -0.5

SB_RUN_FLOOR = -150.0
SWA_WINDOW = 128
REL_BUCKETS = 32
REL_MAX_DIST = 128
HGRN_CHUNK = 64
HGRN_BLOCK = 16
N_EXPERTS = 8
F_DENSE = 2816
F_EXPERT = 3584
ALPHA = (2 * DEPTH) ** 0.25

_IN_SPLITS = (('mla_cq', 256), ('mla_ckv', 128), ('mla_kr', 32), ('swa_q', 256), ('swa_k', 128),
              ('swa_v', 128), ('hgrn', 1024), ('sb_q', 256), ('sb_k', 256), ('sb_v', 256), ('gates', 4096))
_IN_OFF = {}
_o = 0
for _n, _w in _IN_SPLITS:
    _IN_OFF[_n] = _o
    _o += _w

_A_SPLITS = (('cq', 256), ('ckv', 128), ('kra', 128), ('krb', 128), ('swa_q', 256), ('swa_k', 128),
             ('swa_v', 128), ('hgrn', 1024), ('sb_q', 256), ('sb_k', 256), ('sb_v', 256))
_A_OFF = {}
_o = 0
for _n, _w in _A_SPLITS:
    _A_OFF[_n] = (_o, _o + _w)
    _o += _w
A_COLS = _o

TM_A = 512
TQ_ATT = 256
MLA_TQ = 512
MLA_TK = 512
MLA_WIDE = 4
MLA_GROUP = 4
SWA_TQ = 512
HG_ROWS = 256
TM_FFN = 512
TF_FFN = 1408
MOE_TB = 512
MOE_TF = 1792
TM_ROUTER = 512
TM_COMB = 512
TM_DISP = 1024
VMEM_LIMIT = 56 * 1024 * 1024


def _cparams(sem, **kw):
    return pltpu.CompilerParams(dimension_semantics=sem, vmem_limit_bytes=VMEM_LIMIT, **kw)


def _const_spec(shape):
    nd = len(shape)
    return pl.BlockSpec(shape, lambda *_: (0,) * nd, pipeline_mode=pl.Buffered(1))


def _layernorm(v, g, b):
    mu = jnp.mean(v, axis=-1, keepdims=True)
    vc = v - mu
    var = jnp.mean(vc * vc, axis=-1, keepdims=True)
    return vc * lax.rsqrt(var + EPS) * g + b


def _dot(a, b):
    return jnp.dot(a, b, preferred_element_type=F32)


def _dot_nt(a, b):
    return lax.dot_general(a, b, (((1,), (1,)), ((), ())), preferred_element_type=F32)


def _split3(a):
    hi = a.astype(BF16)
    r = a - hi.astype(F32)
    mid = r.astype(BF16)
    lo = (r - mid.astype(F32)).astype(BF16)
    return hi, mid, lo


def _rope_kernel(pos_ref, freq_ref, c_ref, s_ref):
    lane = lax.broadcasted_iota(I32, pos_ref.shape, 1)
    ang = pos_ref[...] * freq_ref[...]
    rope = (lane >= MLA_NOPE) & (lane < MLA_NOPE + MLA_ROPE)
    first = lane < MLA_NOPE + MLA_ROPE // 2
    c_ref[...] = jnp.where(lane < MLA_NOPE, 1.0, jnp.where(rope, jnp.cos(ang), 0.0))
    sn = jnp.sin(ang)
    s_ref[...] = jnp.where(rope, jnp.where(first, -sn, sn), 0.0)


def _rope_tables(positions):
    n = positions.size
    half = MLA_ROPE // 2
    inv_freq = ROPE_THETA ** (-jnp.arange(half, dtype=F32) / half)
    freq = jnp.zeros((1, LANES), F32).at[0, MLA_NOPE:MLA_NOPE + MLA_ROPE].set(jnp.tile(inv_freq, 2))
    posb = jnp.broadcast_to(positions.reshape(n, 1).astype(F32), (n, LANES))
    tm = 1024
    return pl.pallas_call(
        _rope_kernel,
        grid=(n // tm,),
        in_specs=[pl.BlockSpec((tm, LANES), lambda i: (i, 0)), _const_spec((1, LANES))],
        out_specs=[pl.BlockSpec((tm, LANES), lambda i: (i, 0))] * 2,
        out_shape=[jax.ShapeDtypeStruct((n, LANES), F32)] * 2,
        compiler_params=_cparams(("parallel",)),
        name="rope_tables",
    )(posb, freq)


def _inproj_kernel(x_ref, w_ref, c_ref, s_ref, qn_ref, kvn_ref, wuqa_ref, wuqb_ref, wuk_ref, wuv_ref,
                   mq_ref, mk_ref, mv_ref, swq_ref, swk_ref, swv_ref, hg_ref, sbq_ref, sbk_ref, sbv_ref):
    h = _dot(x_ref[...].astype(BF16), w_ref[...])

    def cols(name):
        lo, hi = _A_OFF[name]
        return h[:, lo:hi]

    c = c_ref[...]
    s = s_ref[...]
    c4 = jnp.concatenate([c] * N_HEADS, axis=1)
    s4 = jnp.concatenate([s] * N_HEADS, axis=1)

    cq = cols('cq')
    cqn = (cq * lax.rsqrt(jnp.mean(cq * cq, axis=-1, keepdims=True) + EPS) * qn_ref[...]).astype(BF16)
    q = _dot(cqn, wuqa_ref[...]) * c4 + _dot(cqn, wuqb_ref[...]) * s4
    mq_ref[...] = (q * (MLA_SCALE * LOG2E)).astype(BF16)

    ckv = cols('ckv')
    ckvn = (ckv * lax.rsqrt(jnp.mean(ckv * ckv, axis=-1, keepdims=True) + EPS) * kvn_ref[...]).astype(BF16)
    krot = cols('kra') * c + cols('krb') * s
    mk_ref[...] = (_dot(ckvn, wuk_ref[...]) + jnp.concatenate([krot] * N_HEADS, axis=1)).astype(BF16)
    mv_ref[...] = _dot(ckvn, wuv_ref[...]).astype(BF16)

    swq_ref[...] = cols('swa_q').astype(BF16)
    swk_ref[...] = cols('swa_k').astype(BF16)
    swv_ref[...] = cols('swa_v').astype(BF16)
    hg_ref[...] = cols('hgrn')
    sbq_ref[...] = cols('sb_q').astype(BF16)
    sbk_ref[...] = cols('sb_k').astype(BF16)
    sbv_ref[...] = cols('sb_v').astype(BF16)


def _inproj_weights(w_in, w_uq, w_ukv):
    def seg(name, width):
        o = _IN_OFF[name]
        return w_in[:, o:o + width]

    kr = seg('mla_kr', MLA_ROPE)
    half = MLA_ROPE // 2
    z64 = jnp.zeros((D_MODEL, MLA_NOPE), F32)
    z32 = jnp.zeros((D_MODEL, LANES - MLA_NOPE - MLA_ROPE), F32)
    kra = jnp.concatenate([z64, kr, z32], axis=1)
    krb = jnp.concatenate([z64, kr[:, half:], kr[:, :half], z32], axis=1)
    swq = seg('swa_q', 256).reshape(D_MODEL, N_HEADS, HEAD_DIM)[:, jnp.array([0, 2, 1, 3])].reshape(D_MODEL, WIDTH)
    w_a = jnp.concatenate([
        seg('mla_cq', 256), seg('mla_ckv', 128), kra, krb,
        swq * QK_SCALE, seg('swa_k', 128), seg('swa_v', 128),
        seg('hgrn', 1024), seg('sb_q', 256) * (QK_SCALE * LOG2E), seg('sb_k', 256), seg('sb_v', 256)], axis=1)

    qd = MLA_NOPE + MLA_ROPE
    zq = jnp.zeros((MLA_Q_LORA, LANES - qd), F32)
    zn = jnp.zeros((MLA_Q_LORA, MLA_NOPE), F32)
    qa, qb = [], []
    for hh in range(N_HEADS):
        nope = w_uq[:, hh * qd: hh * qd + MLA_NOPE]
        rope = w_uq[:, hh * qd + MLA_NOPE: (hh + 1) * qd]
        qa += [nope, rope, zq]
        qb += [zn, rope[:, half:], rope[:, :half], zq]
    wuqa = jnp.concatenate(qa, axis=1)
    wuqb = jnp.concatenate(qb, axis=1)
    lane = jnp.arange(N_HEADS * LANES) % LANES
    wuk = jnp.where(lane[None, :] < MLA_NOPE, w_ukv, 0.0)
    wuv = jnp.concatenate([w_ukv[:, hh * LANES + MLA_NOPE:(hh + 1) * LANES] for hh in range(N_HEADS)], axis=1)
    return tuple(t.astype(BF16) for t in (w_a, wuqa, wuqb, wuk, wuv))


def _inproj(x2d, wts, ctab, stab, q_norm, kv_norm):
    n = x2d.shape[0]
    w_a, wuqa, wuqb, wuk, wuv = wts
    tm = TM_A
    row = lambda w: pl.BlockSpec((tm, w), lambda i: (i, 0))
    out_w = (512, 512, 256, 256, 128, 128, 1024, 256, 256, 256)
    out_dt = (BF16, BF16, BF16, BF16, BF16, BF16, F32, BF16, BF16, BF16)
    return pl.pallas_call(
        _inproj_kernel,
        grid=(n // tm,),
        in_specs=[row(D_MODEL), _const_spec(w_a.shape), row(LANES), row(LANES),
                  _const_spec((1, MLA_Q_LORA)), _const_spec((1, MLA_KV_LORA)),
                  _const_spec(wuqa.shape), _const_spec(wuqb.shape), _const_spec(wuk.shape),
                  _const_spec(wuv.shape)],
        out_specs=[row(w) for w in out_w],
        out_shape=[jax.ShapeDtypeStruct((n, w), d) for w, d in zip(out_w, out_dt)],
        compiler_params=_cparams(("parallel",)),
        name="inproj",
    )(x2d, w_a, ctab, stab, q_norm.reshape(1, -1), kv_norm.reshape(1, -1), wuqa, wuqb, wuk, wuv)


def _half_mask(half):
    lane = lax.broadcasted_iota(I32, (1, LANES), 1)
    return (lane < HEAD_DIM) if half == 0 else (lane >= HEAD_DIM)


def _mla_kernel(q_ref, k_ref, v_ref, o_ref):
    tq = q_ref.shape[0]
    tk = MLA_TK
    nsub = tq // tk
    i = pl.program_id(1)
    row = lax.broadcasted_iota(I32, (tq, tk), 0)
    col = lax.broadcasted_iota(I32, (tq, tk), 1)
    ones = jnp.ones((1, LANES), BF16)

    def update(off, carry, heads, mask, width=tk):
        ss = [_dot_nt(q_ref[:, hh * LANES:(hh + 1) * LANES],
                      k_ref[pl.ds(off, width), hh * LANES:(hh + 1) * LANES]) for hh in heads]
        if mask is not None:
            ss = [jnp.where(mask, s, NEG_BIG) for s in ss]
        ms = [jnp.maximum(c[0], jnp.max(s, axis=-1, keepdims=True)) for c, s in zip(carry, ss)]
        pms = [jnp.exp2(s - m).astype(BF16) for s, m in zip(ss, ms)]
        new = []
        for n, hh in enumerate(heads):
            vb = v_ref[pl.ds(off, width), (hh // 2) * LANES:(hh // 2 + 1) * LANES]
            vb = jnp.where(_half_mask(hh % 2), vb, ones)
            m, acc = carry[n]
            new.append((ms[n], jnp.exp2(m - ms[n]) * acc + _dot(pms[n], vb)))
        return tuple(new)

    accs = []
    for g in range(0, N_HEADS, MLA_GROUP):
        heads = tuple(range(g, g + MLA_GROUP))
        init = tuple((jnp.full((tq, 1), NEG_BIG, F32), jnp.zeros((tq, LANES), F32)) for _ in heads)
        nkb = i * nsub
        wide = MLA_WIDE * tk
        carry = lax.fori_loop(
            0, nkb // MLA_WIDE,
            lambda j, c, heads=heads: update(pl.multiple_of(j * wide, wide), c, heads, None, wide), init)
        done = nkb // MLA_WIDE * MLA_WIDE
        rest = nkb - done
        carry = lax.cond(
            rest >= 2,
            lambda c, heads=heads: update(pl.multiple_of(done * tk, 2 * tk), c, heads, None, 2 * tk),
            lambda c: c, carry)
        carry = lax.cond(
            rest % 2 == 1,
            lambda c, heads=heads: update(pl.multiple_of((nkb - 1) * tk, tk), c, heads, None),
            lambda c: c, carry)
        for r in range(nsub):
            carry = update(pl.multiple_of(i * tq + r * tk, tk), carry, heads, col + r * tk <= row)
        accs += [c[1] for c in carry]
    outs = []
    for p in range(N_HEADS // 2):
        a0, a1 = accs[2 * p], accs[2 * p + 1]
        outs.append(jnp.where(_half_mask(0), a0 / a0[:, HEAD_DIM:HEAD_DIM + 1], a1 / a1[:, 0:1]))
    o_ref[...] = jnp.concatenate(outs, axis=1).astype(o_ref.dtype)


def _mla_attention(q, k, v, batch, seq):
    tq = MLA_TQ
    q3, k3, v3 = (t.reshape(batch, seq, t.shape[-1]) for t in (q, k, v))
    out = pl.pallas_call(
        _mla_kernel,
        grid=(batch, seq // tq),
        in_specs=[pl.BlockSpec((None, tq, 512), lambda b, i: (b, i, 0)),
                  pl.BlockSpec((None, seq, 512), lambda b, i: (b, 0, 0), pipeline_mode=pl.Buffered(1)),
                  pl.BlockSpec((None, seq, WIDTH), lambda b, i: (b, 0, 0), pipeline_mode=pl.Buffered(1))],
        out_specs=pl.BlockSpec((None, tq, WIDTH), lambda b, i: (b, i, 0)),
        out_shape=jax.ShapeDtypeStruct((batch, seq, WIDTH), BF16),
        compiler_params=_cparams(("parallel", "arbitrary")),
        name="mla_attention",
    )(q3, k3, v3)
    return out.reshape(batch * seq, WIDTH)


def _sb_kernel(q_ref, k_ref, v_ref, o_ref):
    tq = q_ref.shape[0]
    i = pl.program_id(1)
    row = lax.broadcasted_iota(I32, (tq, tq), 0)
    col = lax.broadcasted_iota(I32, (tq, tq), 1)
    strict = col < row
    later = (row > col).astype(BF16)
    qs = []
    for hh in range(N_HEADS):
        qp = q_ref[:, (hh // 2) * LANES:(hh // 2 + 1) * LANES]
        qs.append(jnp.where(_half_mask(hh % 2), qp, jnp.zeros_like(qp)))

    def block(j, carry, diag):
        off = pl.multiple_of(j * tq, tq)
        runs, accs = carry
        heads = range(N_HEADS)
        zs = [_dot_nt(qs[hh], k_ref[pl.ds(off, tq), (hh // 2) * LANES:(hh // 2 + 1) * LANES]) for hh in heads]
        lsps = [jnp.minimum(z, 0.0) - jnp.log2(1.0 + jnp.exp2(-jnp.abs(z))) for z in zs]
        lsns = [lsp - z for lsp, z in zip(lsps, zs)]
        if diag:
            lsns = [jnp.where(strict, t, 0.0) for t in lsns]
        his = [t.astype(BF16) for t in lsns]
        los = [(t - hi.astype(F32)).astype(BF16) for t, hi in zip(lsns, his)]
        rems = [_dot(hi, later) + _dot(lo, later) for hi, lo in zip(his, los)]
        args = [lsps[hh] + rems[hh] + runs[hh] for hh in heads]
        if diag:
            args = [jnp.where(strict, t, NEG_BIG) for t in args]
        probs = [jnp.exp2(t).astype(BF16) for t in args]
        new_runs = tuple(runs[hh] + rems[hh][:, 0:1] + lsns[hh][:, 0:1] for hh in heads)
        new_accs = list(accs)
        for hh in heads:
            p = hh // 2
            vb = v_ref[pl.ds(off, tq), p * LANES:(p + 1) * LANES]
            vb = jnp.where(_half_mask(hh % 2), vb, jnp.zeros_like(vb))
            new_accs[p] = new_accs[p] + _dot(probs[hh], vb)
        return new_runs, tuple(new_accs)

    init = (tuple(jnp.zeros((tq, 1), F32) for _ in range(N_HEADS)),
            tuple(jnp.zeros((tq, LANES), F32) for _ in range(N_HEADS // 2)))
    def still_active(runs):
        top = functools.reduce(jnp.maximum, runs)
        return (jnp.max(top) > SB_RUN_FLOOR).astype(I32)

    runs, accs = block(i, init, True)

    def cond(c):
        return (c[0] < i) & (c[1] > 0)

    def body(c):
        jj, _, runs, accs = c
        runs, accs = block(i - 1 - jj, (runs, accs), False)
        return jj + 1, still_active(runs), runs, accs

    _, _, _, accs = lax.while_loop(cond, body, (jnp.int32(0), still_active(runs), runs, accs))
    o_ref[...] = jnp.concatenate(accs, axis=1).astype(o_ref.dtype)


def _sb_attention(q, k, v, batch, seq):
    tq = TQ_ATT
    q3, k3, v3 = (t.reshape(batch, seq, WIDTH) for t in (q, k, v))
    out = pl.pallas_call(
        _sb_kernel,
        grid=(batch, seq // tq),
        in_specs=[pl.BlockSpec((None, tq, WIDTH), lambda b, i: (b, i, 0)),
                  pl.BlockSpec((None, seq, WIDTH), lambda b, i: (b, 0, 0)),
                  pl.BlockSpec((None, seq, WIDTH), lambda b, i: (b, 0, 0))],
        out_specs=pl.BlockSpec((None, tq, WIDTH), lambda b, i: (b, i, 0)),
        out_shape=jax.ShapeDtypeStruct((batch, seq, WIDTH), BF16),
        compiler_params=_cparams(("parallel", "arbitrary")),
        name="stick_breaking",
    )(q3, k3, v3)
    return out.reshape(batch * seq, WIDTH)


def _rel_bucket(dist):
    exact = REL_BUCKETS // 2
    n = jnp.maximum(dist, 0)
    nf = jnp.maximum(n, 1).astype(F32)
    large = exact + (jnp.log(nf / exact) / math.log(REL_MAX_DIST / exact) * (REL_BUCKETS - exact)).astype(I32)
    large = jnp.clip(large, 0, REL_BUCKETS - 1)
    return jnp.where(n < exact, n, large)


def _swa_kernel(sink_ref, tab_ref, q_ref, kc_ref, kh_ref, vc_ref, vh_ref, pq_ref, pkc_ref, pkh_ref, o_ref):
    w = SWA_WINDOW
    step = pl.program_id(1)
    row = lax.broadcasted_iota(I32, (w, w), 0)
    col = lax.broadcasted_iota(I32, (w, w), 1)
    valid_c = col <= row
    valid_p = col > row
    tabs = [jnp.broadcast_to(tab_ref[hh:hh + 1, :], (w, LANES)) for hh in range(N_HEADS)]
    ones = jnp.ones((1, LANES), BF16)
    nsub = q_ref.shape[0] // w
    chains = [(r, hh) for r in range(nsub) for hh in range(N_HEADS)]

    def real(hh):
        return (hh % 2) * 2 + hh // 2

    def keys(ref, halo_ref, r):
        cur = ref[r * w:(r + 1) * w, :]
        prev = ref[(r - 1) * w:r * w, :] if r else halo_ref[...]
        return cur, prev

    buckets = []
    for r in range(nsub):
        pq = pq_ref[r * w:(r + 1) * w, :]
        pk_prev = pkc_ref[:, (r - 1) * w:r * w] if r else pkh_ref[...]
        buckets.append((_rel_bucket(pq - pkc_ref[:, r * w:(r + 1) * w]), _rel_bucket(pq - pk_prev)))
    logits = []
    for r, hh in chains:
        qp = q_ref[r * w:(r + 1) * w, (hh // 2) * LANES:(hh // 2 + 1) * LANES]
        qh = jnp.where(_half_mask(hh % 2), qp, jnp.zeros_like(qp))
        kc, kp = keys(kc_ref, kh_ref, r)
        logits.append((_dot_nt(qh, kc), _dot_nt(qh, kp)))
    masked = []
    for (r, hh), (lc, lp) in zip(chains, logits):
        lc = jnp.where(valid_c, lc + jnp.take_along_axis(tabs[real(hh)], buckets[r][0], axis=1), NEG_BIG)
        lp = lp + jnp.take_along_axis(tabs[real(hh)], buckets[r][1], axis=1)
        lp = jnp.where(valid_p if r else valid_p & (step > 0), lp, NEG_BIG)
        masked.append((lc, lp))
    maxes = [jnp.maximum(jnp.maximum(jnp.max(lc, axis=-1, keepdims=True), jnp.max(lp, axis=-1, keepdims=True)),
                         sink_ref[real(hh)]) for (r, hh), (lc, lp) in zip(chains, masked)]
    probs = [(jnp.exp(lc - m).astype(BF16), jnp.exp(lp - m).astype(BF16)) for (lc, lp), m in zip(masked, maxes)]
    outs = {}
    for (r, hh), (ec, ep), m in zip(chains, probs, maxes):
        vc, vp = keys(vc_ref, vh_ref, r)
        mine = _half_mask(hh % 2)
        acc = _dot(ec, jnp.where(mine, vc, ones)) + _dot(ep, jnp.where(mine, vp, ones))
        den = (acc[:, 0:1] if hh % 2 else acc[:, HEAD_DIM:HEAD_DIM + 1]) + jnp.exp(sink_ref[real(hh)] - m)
        outs[(r, hh)] = acc / den
    for r in range(nsub):
        pairs = [jnp.where(_half_mask(0), outs[(r, 2 * p)], outs[(r, 2 * p + 1)]) for p in range(N_HEADS // 2)]
        o_ref[r * w:(r + 1) * w, :] = jnp.concatenate(pairs, axis=1).astype(o_ref.dtype)


def _swa_attention(q, k, v, positions, sinks, rel_table, batch, seq):
    w = SWA_WINDOW
    tq = SWA_TQ
    per = tq // w
    kvw = k.shape[-1]
    q3, k3, v3 = (t.reshape(batch, seq, t.shape[-1]) for t in (q, k, v))
    pcol = positions.reshape(batch, seq, 1)
    prow = positions.reshape(batch, 1, seq)
    tab = jnp.zeros((N_HEADS, LANES), F32).at[:, :REL_BUCKETS].set(rel_table.astype(F32).T)
    cur = lambda b, n: (b, n, 0)
    halo = lambda b, n: (b, jnp.maximum(n * per - 1, 0), 0)
    out = pl.pallas_call(
        _swa_kernel,
        grid=(batch, seq // tq),
        in_specs=[pl.BlockSpec(memory_space=pltpu.SMEM), _const_spec((N_HEADS, LANES)),
                  pl.BlockSpec((None, tq, WIDTH), cur),
                  pl.BlockSpec((None, tq, kvw), cur), pl.BlockSpec((None, w, kvw), halo),
                  pl.BlockSpec((None, tq, kvw), cur), pl.BlockSpec((None, w, kvw), halo),
                  pl.BlockSpec((None, tq, 1), cur),
                  pl.BlockSpec((None, 1, tq), lambda b, n: (b, 0, n)),
                  pl.BlockSpec((None, 1, w), lambda b, n: (b, 0, jnp.maximum(n * per - 1, 0)))],
        out_specs=pl.BlockSpec((None, tq, WIDTH), cur),
        out_shape=jax.ShapeDtypeStruct((batch, seq, WIDTH), BF16),
        compiler_params=_cparams(("parallel", "arbitrary")),
        name="swa_attention",
    )(sinks.astype(F32), tab, q3, k3, k3, v3, v3, pcol, prow, prow)
    return out.reshape(batch * seq, WIDTH)


def _hgrn_kernel(hg_ref, lb_ref, nw_ref, o_ref, state_ref):
    c = HGRN_CHUNK
    blk = HGRN_BLOCK

    @pl.when(pl.program_id(1) == 0)
    def _():
        state_ref[...] = jnp.zeros_like(state_ref)

    r64 = lax.broadcasted_iota(I32, (c, c), 0)
    c64 = lax.broadcasted_iota(I32, (c, c), 1)
    incl = (c64 <= r64).astype(BF16)
    ra = lax.broadcasted_iota(I32, (WIDTH, WIDTH), 0) // HEAD_DIM
    ca = lax.broadcasted_iota(I32, (WIDTH, WIDTH), 1) // HEAD_DIM
    same_head = ra == ca
    seg = same_head.astype(BF16)
    ones_cols = jnp.ones((c, LANES), BF16)
    trow = lax.broadcasted_iota(I32, (blk, WIDTH), 0)
    caps = [jnp.where(trow >= s_i, 0.0, NEG_BIG) for s_i in range(blk)]
    lane_head = lax.broadcasted_iota(I32, (1, WIDTH), 1) // HEAD_DIM
    lb = lb_ref[...]
    nw = nw_ref[...]
    dn0 = (((0,), (0,)), ((), ()))

    for ch in range(hg_ref.shape[0] // c):
        rows = slice(ch * c, (ch + 1) * c)
        qraw = hg_ref[rows, 0:WIDTH]
        fraw = hg_ref[rows, WIDTH:2 * WIDTH]
        v = hg_ref[rows, 2 * WIDTH:3 * WIDTH]
        graw = hg_ref[rows, 3 * WIDTH:4 * WIDTH]
        qf = qraw * jax.nn.sigmoid(qraw)
        forget = lb + (1.0 - lb) * jax.nn.sigmoid(fraw)
        lf = jnp.log(forget)
        kk = 1.0 - forget
        gate = graw * jax.nn.sigmoid(graw)
        vb = v.astype(BF16)

        lf3 = _split3(lf)
        bc = _dot(incl, lf3[0]) + _dot(incl, lf3[1]) + _dot(incl, lf3[2])
        b_last = bc[c - 1:c, :]
        tot_col = sum(lax.dot_general(t, ones_cols, dn0, preferred_element_type=F32) for t in lf3)
        decay_col = jnp.exp(jnp.concatenate([tot_col, tot_col], axis=1))

        state = state_ref[...]
        o_inter = _dot((qf * jnp.exp(bc)).astype(BF16), state.astype(BF16))

        def before(qa, qb, ka, kb):
            ref = bc[kb - 1:kb, :]
            qt = qf[qa:qb] * jnp.exp(bc[qa:qb] - ref)
            kt = (kk[ka:kb] * jnp.exp(ref - bc[ka:kb])).astype(BF16)
            qs = jnp.concatenate([jnp.where(lane_head == hh, qt, 0.0) for hh in range(N_HEADS)], axis=0)
            att = _dot_nt(qs.astype(BF16), kt)
            mix = _dot(att.astype(BF16), vb[ka:kb])
            nq = qb - qa
            return sum(jnp.where(lane_head == hh, mix[hh * nq:(hh + 1) * nq], 0.0) for hh in range(N_HEADS))

        bc2 = bc * LOG2E
        key2 = bc2 - jnp.log2(jnp.maximum(kk, 0.0))

        def inside(a):
            b2 = bc2[a:a + blk]
            qb_ = qf[a:a + blk]
            ws = []
            for s_i in range(blk):
                ws.append(qb_ * jnp.exp2(jnp.minimum(b2 - key2[a + s_i:a + s_i + 1, :], caps[s_i])))
            att = _dot(jnp.concatenate(ws, axis=0).astype(BF16), seg)
            return sum(att[s_i * blk:(s_i + 1) * blk] * v[a + s_i:a + s_i + 1, :] for s_i in range(blk))

        pieces = {a: [] for a in range(0, c, blk)}

        def cover(a, b):
            if b - a == blk:
                pieces[a].append(inside(a))
                return
            mid = (a + b) // 2
            cover(a, mid)
            cover(mid, b)
            res = before(mid, b, a, mid)
            for off in range(0, b - mid, blk):
                pieces[mid + off].append(res[off:off + blk])

        cover(0, c)
        o = o_inter + jnp.concatenate([sum(pieces[a]) for a in range(0, c, blk)], axis=0)

        khat = (kk * jnp.exp(b_last - bc)).astype(BF16)
        upd = lax.dot_general(khat, vb, dn0, preferred_element_type=F32)
        state_ref[...] = decay_col * state + jnp.where(same_head, upd, 0.0)

        o2 = _split3(o * o)
        ms = (_dot(o2[0], seg) + _dot(o2[1], seg)) * (1.0 / HEAD_DIM)
        o_ref[rows, :] = (o * lax.rsqrt(ms + EPS) * nw * gate).astype(o_ref.dtype)


def _hgrn(hg, lower_bound, norm_w, batch, seq):
    rows = HG_ROWS
    hg3 = hg.reshape(batch, seq, 4 * WIDTH)
    out = pl.pallas_call(
        _hgrn_kernel,
        grid=(batch, seq // rows),
        in_specs=[pl.BlockSpec((None, rows, 4 * WIDTH), lambda b, i: (b, i, 0)),
                  _const_spec((1, WIDTH)), _const_spec((1, WIDTH))],
        out_specs=pl.BlockSpec((None, rows, WIDTH), lambda b, i: (b, i, 0)),
        out_shape=jax.ShapeDtypeStruct((batch, seq, WIDTH), BF16),
        scratch_shapes=[pltpu.VMEM((WIDTH, WIDTH), F32)],
        compiler_params=_cparams(("parallel", "arbitrary")),
        name="hgrn2",
    )(hg3, lower_bound.reshape(1, WIDTH).astype(F32), norm_w.reshape(1, WIDTH).astype(F32))
    return out.reshape(batch * seq, WIDTH)


def _merge_body(x, y_refs, wg_ref, wb_ref, wo_ref, g, b):
    xb = x.astype(BF16)
    merged = jnp.zeros(x.shape, F32)
    for nbr, y_ref in enumerate(y_refs):
        gate = jax.nn.sigmoid(_dot(xb, wg_ref[:, nbr * D_MODEL:(nbr + 1) * D_MODEL]))
        merged = merged + gate * _dot(y_ref[...], wb_ref[nbr])
    y = _dot(merged.astype(BF16), wo_ref[...])
    return _layernorm(ALPHA * x + y, g, b)


def _memkv_kernel(m_ref, w_ref, k_ref, v_ref):
    kv = _dot(m_ref[...].astype(BF16), w_ref[...])
    k_ref[...] = kv[:, :WIDTH].astype(BF16)
    v_ref[...] = kv[:, WIDTH:].astype(BF16)


def _memkv(mem, wkv):
    batch, m, _ = mem.shape
    return pl.pallas_call(
        _memkv_kernel,
        grid=(batch,),
        in_specs=[pl.BlockSpec((None, m, D_MODEL), lambda b: (b, 0, 0)), _const_spec(wkv.shape)],
        out_specs=[pl.BlockSpec((None, m, WIDTH), lambda b: (b, 0, 0))] * 2,
        out_shape=[jax.ShapeDtypeStruct((batch, m, WIDTH), BF16)] * 2,
        compiler_params=_cparams(("parallel",)),
        name="mem_kv",
    )(mem, wkv)


def _xattn_body(x, wq_ref, k, v, wo_ref, g, b):
    q = _dot(x.astype(BF16), wq_ref[...]).astype(BF16)
    lane = lax.broadcasted_iota(I32, (1, WIDTH), 1) // HEAD_DIM
    heads = range(N_HEADS)
    ss = [_dot_nt(jnp.where(lane == hh, q, jnp.zeros_like(q)), k) for hh in heads]
    es = [jnp.exp(s - jnp.max(s, axis=-1, keepdims=True)) for s in ss]
    ps = [(e / jnp.sum(e, axis=-1, keepdims=True)).astype(BF16) for e in es]
    o = jnp.zeros((x.shape[0], WIDTH), F32)
    for hh in heads:
        o = o + jnp.where(lane == hh, _dot(ps[hh], v), 0.0)
    y = _dot(o.astype(BF16), wo_ref[...])
    return _layernorm(ALPHA * x + y, g, b)


def _merge_xattn_kernel(x_ref, y0_ref, y1_ref, y2_ref, y3_ref, wg_ref, wb_ref, wo_ref, g1_ref, b1_ref,
                        wq_ref, k_ref, v_ref, xwo_ref, g2_ref, b2_ref, o_ref):
    x1 = _merge_body(x_ref[...], (y0_ref, y1_ref, y2_ref, y3_ref), wg_ref, wb_ref, wo_ref,
                     g1_ref[...], b1_ref[...])
    o_ref[...] = _xattn_body(x1, wq_ref, k_ref[...], v_ref[...], xwo_ref, g2_ref[...], b2_ref[...])


def _merge_xattn(x2d, ys, wg, wb, wo, g1, b1, wq, k, v, xwo, g2, b2, batch, seq):
    tm = TM_A
    m = k.shape[1]
    per = seq // tm
    row = lambda w: pl.BlockSpec((tm, w), lambda bb, i: (bb * per + i, 0))
    kv_spec = pl.BlockSpec((None, m, WIDTH), lambda bb, i: (bb, 0, 0))
    vec = _const_spec((1, D_MODEL))
    return pl.pallas_call(
        _merge_xattn_kernel,
        grid=(batch, per),
        in_specs=[row(D_MODEL)] + [row(WIDTH)] * 4 +
                 [_const_spec(wg.shape), _const_spec(wb.shape), _const_spec(wo.shape), vec, vec,
                  _const_spec(wq.shape), kv_spec, kv_spec, _const_spec(xwo.shape), vec, vec],
        out_specs=row(D_MODEL),
        out_shape=jax.ShapeDtypeStruct((batch * seq, D_MODEL), F32),
        compiler_params=_cparams(("parallel", "parallel")),
        name="merge_xattn_ln",
    )(x2d, *ys, wg, wb, wo, g1.reshape(1, -1), b1.reshape(1, -1),
      wq, k, v, xwo, g2.reshape(1, -1), b2.reshape(1, -1))


def _ffn_kernel(x_ref, w13_ref, w2_ref, g_ref, b_ref, o_ref):
    x = x_ref[...]
    xb = x.astype(BF16)
    y = None
    for h in range(F_DENSE // TF_FFN):
        lo = h * TF_FFN
        a = _dot(xb, w13_ref[:, lo:lo + TF_FFN])
        gate = _dot(xb, w13_ref[:, F_DENSE + lo:F_DENSE + lo + TF_FFN])
        part = _dot((a * jax.nn.sigmoid(a) * gate).astype(BF16), w2_ref[lo:lo + TF_FFN, :])
        y = part if y is None else y + part
    o_ref[...] = _layernorm(ALPHA * x + y, g_ref[...], b_ref[...])


def _ffn(x2d, w13, w2, g, b):
    n = x2d.shape[0]
    tm = TM_FFN
    return pl.pallas_call(
        _ffn_kernel,
        grid=(n // tm,),
        in_specs=[pl.BlockSpec((tm, D_MODEL), lambda i: (i, 0)),
                  _const_spec(w13.shape), _const_spec(w2.shape),
                  _const_spec((1, D_MODEL)), _const_spec((1, D_MODEL))],
        out_specs=pl.BlockSpec((tm, D_MODEL), lambda i: (i, 0)),
        out_shape=jax.ShapeDtypeStruct((n, D_MODEL), F32),
        compiler_params=_cparams(("parallel",)),
        name="ffn_ln",
    )(x2d, w13, w2, g.reshape(1, -1), b.reshape(1, -1))


def _router_kernel(x_ref, r_ref, info_ref, wts_ref, cnt_ref, carry_ref):
    tm = x_ref.shape[0]

    @pl.when(pl.program_id(0) == 0)
    def _():
        carry_ref[...] = jnp.zeros_like(carry_ref)

    logits = jnp.dot(x_ref[...], r_ref[...], precision=lax.Precision.HIGHEST, preferred_element_type=F32)
    lane = lax.broadcasted_iota(I32, (tm, LANES), 1)
    lg = jnp.where(lane < N_EXPERTS, logits, -jnp.inf)
    m1 = jnp.max(lg, axis=-1, keepdims=True)
    i1 = jnp.min(jnp.where(lg == m1, lane, LANES), axis=-1, keepdims=True)
    lg2 = jnp.where(lane == i1, -jnp.inf, lg)
    m2 = jnp.max(lg2, axis=-1, keepdims=True)
    i2 = jnp.min(jnp.where(lg2 == m2, lane, LANES), axis=-1, keepdims=True)
    e = jnp.exp(m2 - m1)
    w1 = 1.0 / (1.0 + e)
    w2 = e / (1.0 + e)
    sel1 = lane == i1
    sel2 = lane == i2
    chosen = jnp.where(sel1 | sel2, 1.0, 0.0)
    row = lax.broadcasted_iota(I32, (tm, tm), 0)
    col = lax.broadcasted_iota(I32, (tm, tm), 1)
    before = (col < row).astype(BF16)
    ranks = _dot(before, chosen.astype(BF16)) + carry_ref[...]
    r1 = jnp.sum(jnp.where(sel1, ranks, 0.0), axis=-1, keepdims=True)
    r2 = jnp.sum(jnp.where(sel2, ranks, 0.0), axis=-1, keepdims=True)
    carry_ref[...] = carry_ref[...] + jnp.sum(chosen, axis=0, keepdims=True)
    info = jnp.where(lane == 0, i1.astype(F32), jnp.where(lane == 1, i2.astype(F32),
                     jnp.where(lane == 2, r1, jnp.where(lane == 3, r2, 0.0))))
    info_ref[...] = jnp.transpose(info)[:info_ref.shape[0], :]
    wts_ref[...] = jnp.where(lane == 0, w1, jnp.where(lane == 1, w2, 0.0))
    cnt_ref[...] = carry_ref[...]


def _router(x2d, router):
    n = x2d.shape[0]
    tm = TM_ROUTER
    r_pad = jnp.zeros((D_MODEL, LANES), F32).at[:, :N_EXPERTS].set(router.astype(F32))
    row = pl.BlockSpec((tm, LANES), lambda i: (i, 0))
    return pl.pallas_call(
        _router_kernel,
        grid=(n // tm,),
        in_specs=[pl.BlockSpec((tm, D_MODEL), lambda i: (i, 0)), _const_spec(r_pad.shape)],
        out_specs=[pl.BlockSpec((8, tm), lambda i: (0, i)), row, pl.BlockSpec((1, LANES), lambda i: (0, 0))],
        out_shape=[jax.ShapeDtypeStruct((8, n), F32), jax.ShapeDtypeStruct((n, LANES), F32),
                   jax.ShapeDtypeStruct((1, LANES), F32)],
        scratch_shapes=[pltpu.VMEM((1, LANES), F32)],
        compiler_params=_cparams(("arbitrary",)),
        name="moe_router",
    )(x2d, r_pad)


def _dispatch_kernel(pad_ref, d0_ref, d1_ref, x_ref, xb_hbm, stage_ref, sems):
    tm = x_ref.shape[0]
    i = pl.program_id(0)
    last = pl.num_programs(0) - 1
    slot = i % 2

    def wait_step(s):
        for _ in range(2):
            pltpu.make_async_copy(stage_ref.at[s], xb_hbm.at[pl.ds(0, tm), :], sems.at[s]).wait()

    @pl.when(i >= 2)
    def _():
        wait_step(slot)

    for s in range(2):
        @pl.when(slot == s)
        def _():
            stage_ref[s] = x_ref[...]

            def issue(r, c):
                for k in range(2):
                    pltpu.make_async_copy(stage_ref.at[s, pl.ds(r, 1), :],
                                          xb_hbm.at[pl.ds((d0_ref, d1_ref)[k][0, r], 1), :], sems.at[s]).start()
                return c
            lax.fori_loop(0, tm, issue, 0, unroll=8)

    @pl.when(i == last)
    def _():
        def fill(e, c):
            def one(s, c2):
                pltpu.make_async_copy(stage_ref.at[slot, pl.ds(0, 1), :], xb_hbm.at[pl.ds(s, 1), :],
                                      sems.at[2]).start()
                return c2

            def done(s, c2):
                pltpu.make_async_copy(stage_ref.at[slot, pl.ds(0, 1), :], xb_hbm.at[pl.ds(0, 1), :],
                                      sems.at[2]).wait()
                return c2
            lax.fori_loop(pad_ref[0, e], pad_ref[1, e], one, 0)
            lax.fori_loop(pad_ref[0, e], pad_ref[1, e], done, 0)
            return c
        lax.fori_loop(0, pad_ref.shape[1], fill, 0)
        wait_step(slot)

        @pl.when(last >= 1)
        def _():
            wait_step(1 - slot)


def _dispatch(x2d, dest, pads, nblk):
    n = x2d.shape[0]
    tm = TM_DISP
    nt = n // tm
    grid_spec = pltpu.PrefetchScalarGridSpec(
        num_scalar_prefetch=1,
        grid=(nt,),
        in_specs=[pl.BlockSpec((None, 1, tm), lambda i, pads: (i, 0, 0), memory_space=pltpu.SMEM),
                  pl.BlockSpec((None, 1, tm), lambda i, pads: (i, 0, 0), memory_space=pltpu.SMEM),
                  pl.BlockSpec((tm, D_MODEL), lambda i, pads: (i, 0))],
        out_specs=pl.BlockSpec(memory_space=pl.ANY),
        scratch_shapes=[pltpu.VMEM((2, tm, D_MODEL), F32), pltpu.SemaphoreType.DMA((3,))],
    )
    return pl.pallas_call(
        _dispatch_kernel,
        grid_spec=grid_spec,
        out_shape=jax.ShapeDtypeStruct((nblk * MOE_TB, D_MODEL), F32),
        compiler_params=_cparams(("arbitrary",), disable_bounds_checks=True),
        name="moe_dispatch",
    )(pads, dest[0].reshape(nt, 1, tm), dest[1].reshape(nt, 1, tm), x2d)


def _expert_kernel(nused_ref, bexp_ref, x_ref, w1_ref, w3_ref, w2_ref, o_ref, acc_ref):
    f = pl.program_id(1)

    @pl.when(pl.program_id(0) < nused_ref[0])
    def _():
        xb = x_ref[...].astype(BF16)
        a = _dot(xb, w1_ref[...])
        gate = _dot(xb, w3_ref[...])
        part = _dot((a * jax.nn.sigmoid(a) * gate).astype(BF16), w2_ref[...])

        @pl.when(f == 0)
        def _():
            acc_ref[...] = part

        @pl.when(f > 0)
        def _():
            acc_ref[...] += part

        @pl.when(f == pl.num_programs(1) - 1)
        def _():
            o_ref[...] = acc_ref[...]

    @pl.when(pl.program_id(0) >= nused_ref[0])
    def _():
        o_ref[...] = jnp.zeros_like(o_ref)


def _experts(xb, w13, w2, nused, blk_exp, nblk):
    tb, tf = MOE_TB, MOE_TF
    nf = F_EXPERT // tf
    w13t = w13.astype(BF16)

    def blk(i, nu):
        return jnp.maximum(jnp.minimum(i, nu[0] - 1), 0)

    def ftile(i, f, nu):
        return jnp.where(i < nu[0], f, nf - 1)

    grid_spec = pltpu.PrefetchScalarGridSpec(
        num_scalar_prefetch=2,
        grid=(nblk, nf),
        in_specs=[pl.BlockSpec((tb, D_MODEL), lambda i, f, nu, be: (blk(i, nu), 0)),
                  pl.BlockSpec((None, D_MODEL, tf), lambda i, f, nu, be: (be[blk(i, nu)], 0, ftile(i, f, nu))),
                  pl.BlockSpec((None, D_MODEL, tf), lambda i, f, nu, be: (be[blk(i, nu)], 0, nf + ftile(i, f, nu))),
                  pl.BlockSpec((None, tf, D_MODEL), lambda i, f, nu, be: (be[blk(i, nu)], ftile(i, f, nu), 0))],
        out_specs=pl.BlockSpec((tb, D_MODEL), lambda i, f, nu, be: (i, 0)),
        scratch_shapes=[pltpu.VMEM((tb, D_MODEL), F32)],
    )
    return pl.pallas_call(
        _expert_kernel,
        grid_spec=grid_spec,
        out_shape=jax.ShapeDtypeStruct((nblk * tb, D_MODEL), F32),
        compiler_params=_cparams(("arbitrary", "arbitrary")),
        name="moe_experts",
    )(nused, blk_exp, xb, w13t, w13t, w2.astype(BF16))


def _combine_kernel(d0_ref, d1_ref, n0_ref, n1_ref, y_hbm, x_ref, wts_ref, g_ref, b_ref, o_ref, buf_ref, sems):
    tm = x_ref.shape[0]
    i = pl.program_id(0)
    slot = i % 2

    def gather(idx_refs, s):
        def issue(r, c):
            for k in range(2):
                pltpu.make_async_copy(y_hbm.at[pl.ds(idx_refs[k][0, r], 1), :],
                                      buf_ref.at[s, k, pl.ds(r, 1), :], sems.at[s]).start()
            return c
        lax.fori_loop(0, tm, issue, 0, unroll=8)

    @pl.when(i == 0)
    def _():
        gather((d0_ref, d1_ref), 0)

    for s in range(2):
        @pl.when((i + 1 < pl.num_programs(0)) & (slot != s))
        def _():
            gather((n0_ref, n1_ref), s)

    for k in range(2):
        pltpu.make_async_copy(y_hbm.at[pl.ds(0, tm), :], buf_ref.at[slot, k], sems.at[slot]).wait()
    wts = wts_ref[...]
    y = wts[:, 0:1] * buf_ref[slot, 0] + wts[:, 1:2] * buf_ref[slot, 1]
    o_ref[...] = _layernorm(ALPHA * x_ref[...] + y, g_ref[...], b_ref[...])


def _combine(yb, dest, x2d, wts, g, b):
    n = x2d.shape[0]
    tm = TM_COMB
    nt = n // tm
    row = lambda w: pl.BlockSpec((tm, w), lambda i: (i, 0))
    cur = pl.BlockSpec((None, 1, tm), lambda i: (i, 0, 0), memory_space=pltpu.SMEM)
    nxt = pl.BlockSpec((None, 1, tm), lambda i: (jnp.minimum(i + 1, nt - 1), 0, 0), memory_space=pltpu.SMEM)
    d0, d1 = (d.reshape(nt, 1, tm) for d in dest)
    return pl.pallas_call(
        _combine_kernel,
        grid=(nt,),
        in_specs=[cur, cur, nxt, nxt, pl.BlockSpec(memory_space=pl.ANY), row(D_MODEL), row(LANES),
                  _const_spec((1, D_MODEL)), _const_spec((1, D_MODEL))],
        out_specs=row(D_MODEL),
        out_shape=jax.ShapeDtypeStruct((n, D_MODEL), F32),
        scratch_shapes=[pltpu.VMEM((2, 2, tm, D_MODEL), F32), pltpu.SemaphoreType.DMA((2,))],
        compiler_params=_cparams(("arbitrary",), disable_bounds_checks=True),
        name="moe_combine_ln",
    )(d0, d1, d0, d1, yb, x2d, wts, g.reshape(1, -1), b.reshape(1, -1))


def _moe(x2d, router, w13, w2, g, b):
    n = x2d.shape[0]
    tb = MOE_TB
    info, wts, cnt = _router(x2d, router)
    counts = cnt[0, :N_EXPERTS].astype(I32)
    padded = (counts + tb - 1) // tb * tb
    pend = jnp.cumsum(padded)
    pstart = pend - padded
    info = info.astype(I32)
    dest = tuple((sum(jnp.where(info[k] == e, pstart[e], 0) for e in range(N_EXPERTS)) + info[2 + k]).astype(I32)
                 for k in range(2))
    nblk = -(-(2 * n + N_EXPERTS * (tb - 1)) // tb)
    pads = jnp.stack([jnp.append(pstart + counts, pend[-1]), jnp.append(pend, nblk * tb)]).astype(I32)
    nused = (pend[-1] // tb).astype(I32).reshape(1)
    first_row = jnp.arange(nblk, dtype=I32) * tb
    blk_exp = jnp.minimum(jnp.sum(pend[None, :] <= first_row[:, None], axis=1), N_EXPERTS - 1).astype(I32)
    xb = _dispatch(x2d, dest, pads, nblk)
    yb = _experts(xb, w13, w2, nused, blk_exp, nblk)
    return _combine(yb, dest, x2d, wts, g, b)


def kernel(x, mem, positions, rel_bias_table, hgrn_lb_logits, w_in, mla_q_norm, mla_w_uq, mla_kv_norm, mla_w_ukv, swa_sinks, hgrn_norm, w_branch, w_out, ln_g, ln_b, xa_wq, xa_wkv, xa_wo, ffn_w13, ffn_w2, moe_router, moe_w13, moe_w2):
    batch, seq, _ = x.shape
    n = batch * seq
    sm = jax.nn.softmax(hgrn_lb_logits.astype(F32), axis=0)
    lower_bounds = jnp.cumsum(sm, axis=0) - sm[0]
    ctab, stab = _rope_tables(positions)
    xc = x.reshape(n, D_MODEL)
    for l in range(DEPTH):
        wts = _inproj_weights(w_in[l], mla_w_uq[l], mla_w_ukv[l])
        mq, mk, mv, swq, swk, swv, hg, sbq, sbk, sbv = _inproj(xc, wts, ctab, stab, mla_q_norm[l], mla_kv_norm[l])
        y_mla = _mla_attention(mq, mk, mv, batch, seq)
        y_swa = _swa_attention(swq, swk, swv, positions, swa_sinks[l], rel_bias_table, batch, seq)
        y_hg = _hgrn(hg, lower_bounds[l], hgrn_norm[l], batch, seq)
        y_sb = _sb_attention(sbq, sbk, sbv, batch, seq)
        go = _IN_OFF['gates']
        mk_, mv_ = _memkv(mem, xa_wkv[l].astype(BF16))
        wb = w_branch[l].at[1].set(
            w_branch[l][1].reshape(N_HEADS, HEAD_DIM, D_MODEL)[jnp.array([0, 2, 1, 3])].reshape(WIDTH, D_MODEL))
        xc = _merge_xattn(xc, (y_mla, y_swa, y_hg, y_sb), w_in[l][:, go:].astype(BF16), wb.astype(BF16),
                          w_out[l].astype(BF16), ln_g[l, 0], ln_b[l, 0],
                          (xa_wq[l] * QK_SCALE).astype(BF16), mk_, mv_, xa_wo[l].astype(BF16),
                          ln_g[l, 1], ln_b[l, 1], batch, seq)
        if l % 2 == 0:
            xc = _ffn(xc, ffn_w13[l // 2].astype(BF16), ffn_w2[l // 2].astype(BF16), ln_g[l, 2], ln_b[l, 2])
        else:
            xc = _moe(xc, moe_router[l // 2], moe_w13[l // 2], moe_w2[l // 2],
                      ln_g[l, 2], ln_b[l, 2])
    return xc.reshape(batch, seq, D_MODEL)
```

```python
import functools
import math

import jax
import jax.numpy as jnp
from jax import lax
from jax.experimental import pallas as pl
from jax.experimental.pallas import tpu as pltpu

F32 = jnp.float32
BF16 = jnp.bfloat16
I32 = jnp.int32

D_MODEL = 1024
DEPTH = 2
EPS = 1e-5
NEG_BIG = -1e30
LANES = 128
HEAD_DIM = 64
N_HEADS = 4
WIDTH = N_HEADS * HEAD_DIM

MLA_Q_LORA = 256
MLA_KV_LORA = 128
MLA_NOPE = 64
MLA_ROPE = 32
ROPE_THETA = 10000.0
MLA_SCALE = (MLA_NOPE + MLA_ROPE) ** -0.5
LOG2E = math.log2(math.e)
QK_SCALE = HEAD_DIM ** -0.5

SB_RUN_FLOOR = -150.0
SWA_WINDOW = 128
REL_BUCKETS = 32
REL_MAX_DIST = 128
HGRN_CHUNK = 64
HGRN_BLOCK = 16
N_EXPERTS = 8
F_DENSE = 2816
F_EXPERT = 3584
ALPHA = (2 * DEPTH) ** 0.25

_IN_SPLITS = (('mla_cq', 256), ('mla_ckv', 128), ('mla_kr', 32), ('swa_q', 256), ('swa_k', 128),
              ('swa_v', 128), ('hgrn', 1024), ('sb_q', 256), ('sb_k', 256), ('sb_v', 256), ('gates', 4096))
_IN_OFF = {}
_o = 0
for _n, _w in _IN_SPLITS:
    _IN_OFF[_n] = _o
    _o += _w

_A_SPLITS = (('cq', 256), ('ckv', 128), ('kra', 128), ('krb', 128), ('swa_q', 256), ('swa_k', 128),
             ('swa_v', 128), ('hgrn', 1024), ('sb_q', 256), ('sb_k', 256), ('sb_v', 256))
_A_OFF = {}
_o = 0
for _n, _w in _A_SPLITS:
    _A_OFF[_n] = (_o, _o + _w)
    _o += _w
A_COLS = _o

TM_A = 512
TQ_ATT = 256
MLA_TQ = 512
MLA_TK = 512
MLA_WIDE = 4
MLA_GROUP = 4
SWA_TQ = 512
HG_ROWS = 256
TM_FFN = 512
TF_FFN = 1408
MOE_TB = 512
MOE_TF = 1792
TM_ROUTER = 512
TM_COMB = 512
TM_DISP = 1024
VMEM_LIMIT = 56 * 1024 * 1024
assert MLA_WIDE == 4 and MLA_TQ % MLA_TK == 0


def _cparams(sem, **kw):
    return pltpu.CompilerParams(dimension_semantics=sem, vmem_limit_bytes=VMEM_LIMIT, **kw)


def _const_spec(shape):
    nd = len(shape)
    return pl.BlockSpec(shape, lambda *_: (0,) * nd, pipeline_mode=pl.Buffered(1))


def _layernorm(v, g, b):
    mu = jnp.mean(v, axis=-1, keepdims=True)
    vc = v - mu
    var = jnp.mean(vc * vc, axis=-1, keepdims=True)
    return vc * lax.rsqrt(var + EPS) * g + b


def _dot(a, b):
    return jnp.dot(a, b, preferred_element_type=F32)


def _dot_nt(a, b):
    return lax.dot_general(a, b, (((1,), (1,)), ((), ())), preferred_element_type=F32)


def _split3(a):
    hi = a.astype(BF16)
    r = a - hi.astype(F32)
    mid = r.astype(BF16)
    lo = (r - mid.astype(F32)).astype(BF16)
    return hi, mid, lo


def _rope_kernel(pos_ref, freq_ref, c_ref, s_ref):
    lane = lax.broadcasted_iota(I32, pos_ref.shape, 1)
    ang = pos_ref[...] * freq_ref[...]
    rope = (lane >= MLA_NOPE) & (lane < MLA_NOPE + MLA_ROPE)
    first = lane < MLA_NOPE + MLA_ROPE // 2
    c_ref[...] = jnp.where(lane < MLA_NOPE, 1.0, jnp.where(rope, jnp.cos(ang), 0.0))
    sn = jnp.sin(ang)
    s_ref[...] = jnp.where(rope, jnp.where(first, -sn, sn), 0.0)


def _rope_tables(positions):
    n = positions.size
    half = MLA_ROPE // 2
    inv_freq = ROPE_THETA ** (-jnp.arange(half, dtype=F32) / half)
    freq = jnp.zeros((1, LANES), F32).at[0, MLA_NOPE:MLA_NOPE + MLA_ROPE].set(jnp.tile(inv_freq, 2))
    posb = jnp.broadcast_to(positions.reshape(n, 1).astype(F32), (n, LANES))
    tm = 1024
    return pl.pallas_call(
        _rope_kernel,
        grid=(n // tm,),
        in_specs=[pl.BlockSpec((tm, LANES), lambda i: (i, 0)), _const_spec((1, LANES))],
        out_specs=[pl.BlockSpec((tm, LANES), lambda i: (i, 0))] * 2,
        out_shape=[jax.ShapeDtypeStruct((n, LANES), F32)] * 2,
        compiler_params=_cparams(("parallel",)),
        name="rope_tables",
    )(posb, freq)


def _inproj_kernel(x_ref, w_ref, c_ref, s_ref, qn_ref, kvn_ref, wuqa_ref, wuqb_ref, wuk_ref, wuv_ref,
                   mq_ref, mk_ref, mv_ref, swq_ref, swk_ref, swv_ref, hg_ref, sbq_ref, sbk_ref, sbv_ref):
    h = _dot(x_ref[...].astype(BF16), w_ref[...])

    def cols(name):
        lo, hi = _A_OFF[name]
        return h[:, lo:hi]

    c = c_ref[...]
    s = s_ref[...]
    c4 = jnp.concatenate([c] * N_HEADS, axis=1)
    s4 = jnp.concatenate([s] * N_HEADS, axis=1)

    cq = cols('cq')
    cqn = (cq * lax.rsqrt(jnp.mean(cq * cq, axis=-1, keepdims=True) + EPS) * qn_ref[...]).astype(BF16)
    q = _dot(cqn, wuqa_ref[...]) * c4 + _dot(cqn, wuqb_ref[...]) * s4
    mq_ref[...] = (q * (MLA_SCALE * LOG2E)).astype(BF16)

    ckv = cols('ckv')
    ckvn = (ckv * lax.rsqrt(jnp.mean(ckv * ckv, axis=-1, keepdims=True) + EPS) * kvn_ref[...]).astype(BF16)
    krot = cols('kra') * c + cols('krb') * s
    mk_ref[...] = (_dot(ckvn, wuk_ref[...]) + jnp.concatenate([krot] * N_HEADS, axis=1)).astype(BF16)
    mv_ref[...] = _dot(ckvn, wuv_ref[...]).astype(BF16)

    swq_ref[...] = cols('swa_q').astype(BF16)
    swk_ref[...] = cols('swa_k').astype(BF16)
    swv_ref[...] = cols('swa_v').astype(BF16)
    hg_ref[...] = cols('hgrn')
    sbq_ref[...] = cols('sb_q').astype(BF16)
    sbk_ref[...] = cols('sb_k').astype(BF16)
    sbv_ref[...] = cols('sb_v').astype(BF16)


def _inproj_weights(w_in, w_uq, w_ukv):
    def seg(name, width):
        o = _IN_OFF[name]
        return w_in[:, o:o + width]

    kr = seg('mla_kr', MLA_ROPE)
    half = MLA_ROPE // 2
    z64 = jnp.zeros((D_MODEL, MLA_NOPE), F32)
    z32 = jnp.zeros((D_MODEL, LANES - MLA_NOPE - MLA_ROPE), F32)
    kra = jnp.concatenate([z64, kr, z32], axis=1)
    krb = jnp.concatenate([z64, kr[:, half:], kr[:, :half], z32], axis=1)
    swq = seg('swa_q', 256).reshape(D_MODEL, N_HEADS, HEAD_DIM)[:, jnp.array([0, 2, 1, 3])].reshape(D_MODEL, WIDTH)
    w_a = jnp.concatenate([
        seg('mla_cq', 256), seg('mla_ckv', 128), kra, krb,
        swq * QK_SCALE, seg('swa_k', 128), seg('swa_v', 128),
        seg('hgrn', 1024), seg('sb_q', 256) * (QK_SCALE * LOG2E), seg('sb_k', 256), seg('sb_v', 256)], axis=1)

    qd = MLA_NOPE + MLA_ROPE
    zq = jnp.zeros((MLA_Q_LORA, LANES - qd), F32)
    zn = jnp.zeros((MLA_Q_LORA, MLA_NOPE), F32)
    qa, qb = [], []
    for hh in range(N_HEADS):
        nope = w_uq[:, hh * qd: hh * qd + MLA_NOPE]
        rope = w_uq[:, hh * qd + MLA_NOPE: (hh + 1) * qd]
        qa += [nope, rope, zq]
        qb += [zn, rope[:, half:], rope[:, :half], zq]
    wuqa = jnp.concatenate(qa, axis=1)
    wuqb = jnp.concatenate(qb, axis=1)
    lane = jnp.arange(N_HEADS * LANES) % LANES
    wuk = jnp.where(lane[None, :] < MLA_NOPE, w_ukv, 0.0)
    wuv = jnp.concatenate([w_ukv[:, hh * LANES + MLA_NOPE:(hh + 1) * LANES] for hh in range(N_HEADS)], axis=1)
    return tuple(t.astype(BF16) for t in (w_a, wuqa, wuqb, wuk, wuv))


def _inproj(x2d, wts, ctab, stab, q_norm, kv_norm):
    n = x2d.shape[0]
    w_a, wuqa, wuqb, wuk, wuv = wts
    tm = TM_A
    row = lambda w: pl.BlockSpec((tm, w), lambda i: (i, 0))
    out_w = (512, 512, 256, 256, 128, 128, 1024, 256, 256, 256)
    out_dt = (BF16, BF16, BF16, BF16, BF16, BF16, F32, BF16, BF16, BF16)
    return pl.pallas_call(
        _inproj_kernel,
        grid=(n // tm,),
        in_specs=[row(D_MODEL), _const_spec(w_a.shape), row(LANES), row(LANES),
                  _const_spec((1, MLA_Q_LORA)), _const_spec((1, MLA_KV_LORA)),
                  _const_spec(wuqa.shape), _const_spec(wuqb.shape), _const_spec(wuk.shape),
                  _const_spec(wuv.shape)],
        out_specs=[row(w) for w in out_w],
        out_shape=[jax.ShapeDtypeStruct((n, w), d) for w, d in zip(out_w, out_dt)],
        compiler_params=_cparams(("parallel",)),
        name="inproj",
    )(x2d, w_a, ctab, stab, q_norm.reshape(1, -1), kv_norm.reshape(1, -1), wuqa, wuqb, wuk, wuv)


def _half_mask(half):
    lane = lax.broadcasted_iota(I32, (1, LANES), 1)
    return (lane < HEAD_DIM) if half == 0 else (lane >= HEAD_DIM)


def _mla_kernel(q_ref, k_ref, v_ref, o_ref):
    tq = q_ref.shape[0]
    tk = MLA_TK
    nsub = tq // tk
    i = pl.program_id(1)
    row = lax.broadcasted_iota(I32, (tq, tk), 0)
    col = lax.broadcasted_iota(I32, (tq, tk), 1)
    ones = jnp.ones((1, LANES), BF16)

    def update(off, carry, heads, mask, width=tk):
        ss = [_dot_nt(q_ref[:, hh * LANES:(hh + 1) * LANES],
                      k_ref[pl.ds(off, width), hh * LANES:(hh + 1) * LANES]) for hh in heads]
        if mask is not None:
            ss = [jnp.where(mask, s, NEG_BIG) for s in ss]
        ms = [jnp.maximum(c[0], jnp.max(s, axis=-1, keepdims=True)) for c, s in zip(carry, ss)]
        pms = [jnp.exp2(s - m).astype(BF16) for s, m in zip(ss, ms)]
        new = []
        for n, hh in enumerate(heads):
            vb = v_ref[pl.ds(off, width), (hh // 2) * LANES:(hh // 2 + 1) * LANES]
            vb = jnp.where(_half_mask(hh % 2), vb, ones)
            m, acc = carry[n]
            new.append((ms[n], jnp.exp2(m - ms[n]) * acc + _dot(pms[n], vb)))
        return tuple(new)

    accs = []
    for g in range(0, N_HEADS, MLA_GROUP):
        heads = tuple(range(g, g + MLA_GROUP))
        init = tuple((jnp.full((tq, 1), NEG_BIG, F32), jnp.zeros((tq, LANES), F32)) for _ in heads)
        nkb = i * nsub
        wide = MLA_WIDE * tk
        carry = lax.fori_loop(
            0, nkb // MLA_WIDE,
            lambda j, c, heads=heads: update(pl.multiple_of(j * wide, wide), c, heads, None, wide), init)
        done = nkb // MLA_WIDE * MLA_WIDE
        rest = nkb - done
        carry = lax.cond(
            rest >= 2,
            lambda c, heads=heads: update(pl.multiple_of(done * tk, 2 * tk), c, heads, None, 2 * tk),
            lambda c: c, carry)
        carry = lax.cond(
            rest % 2 == 1,
            lambda c, heads=heads: update(pl.multiple_of((nkb - 1) * tk, tk), c, heads, None),
            lambda c: c, carry)
        for r in range(nsub):
            carry = update(pl.multiple_of(i * tq + r * tk, tk), carry, heads, col + r * tk <= row)
        accs += [c[1] for c in carry]
    outs = []
    for p in range(N_HEADS // 2):
        a0, a1 = accs[2 * p], accs[2 * p + 1]
        outs.append(jnp.where(_half_mask(0), a0 / a0[:, HEAD_DIM:HEAD_DIM + 1], a1 / a1[:, 0:1]))
    o_ref[...] = jnp.concatenate(outs, axis=1).astype(o_ref.dtype)


def _mla_attention(q, k, v, batch, seq):
    tq = MLA_TQ
    q3, k3, v3 = (t.reshape(batch, seq, t.shape[-1]) for t in (q, k, v))
    out = pl.pallas_call(
        _mla_kernel,
        grid=(batch, seq // tq),
        in_specs=[pl.BlockSpec((None, tq, 512), lambda b, i: (b, i, 0)),
                  pl.BlockSpec((None, seq, 512), lambda b, i: (b, 0, 0), pipeline_mode=pl.Buffered(1)),
                  pl.BlockSpec((None, seq, WIDTH), lambda b, i: (b, 0, 0), pipeline_mode=pl.Buffered(1))],
        out_specs=pl.BlockSpec((None, tq, WIDTH), lambda b, i: (b, i, 0)),
        out_shape=jax.ShapeDtypeStruct((batch, seq, WIDTH), BF16),
        compiler_params=_cparams(("parallel", "arbitrary")),
        name="mla_attention",
    )(q3, k3, v3)
    return out.reshape(batch * seq, WIDTH)


def _sb_kernel(q_ref, k_ref, v_ref, o_ref):
    tq = q_ref.shape[0]
    i = pl.program_id(1)
    row = lax.broadcasted_iota(I32, (tq, tq), 0)
    col = lax.broadcasted_iota(I32, (tq, tq), 1)
    strict = col < row
    later = (row > col).astype(BF16)
    qs = []
    for hh in range(N_HEADS):
        qp = q_ref[:, (hh // 2) * LANES:(hh // 2 + 1) * LANES]
        qs.append(jnp.where(_half_mask(hh % 2), qp, jnp.zeros_like(qp)))

    def block(j, carry, diag):
        off = pl.multiple_of(j * tq, tq)
        runs, accs = carry
        heads = range(N_HEADS)
        zs = [_dot_nt(qs[hh], k_ref[pl.ds(off, tq), (hh // 2) * LANES:(hh // 2 + 1) * LANES]) for hh in heads]
        lsps = [jnp.minimum(z, 0.0) - jnp.log2(1.0 + jnp.exp2(-jnp.abs(z))) for z in zs]
        lsns = [lsp - z for lsp, z in zip(lsps, zs)]
        if diag:
            lsns = [jnp.where(strict, t, 0.0) for t in lsns]
        his = [t.astype(BF16) for t in lsns]
        los = [(t - hi.astype(F32)).astype(BF16) for t, hi in zip(lsns, his)]
        rems = [_dot(hi, later) + _dot(lo, later) for hi, lo in zip(his, los)]
        args = [lsps[hh] + rems[hh] + runs[hh] for hh in heads]
        if diag:
            args = [jnp.where(strict, t, NEG_BIG) for t in args]
        probs = [jnp.exp2(t).astype(BF16) for t in args]
        new_runs = tuple(runs[hh] + rems[hh][:, 0:1] + lsns[hh][:, 0:1] for hh in heads)
        new_accs = list(accs)
        for hh in heads:
            p = hh // 2
            vb = v_ref[pl.ds(off, tq), p * LANES:(p + 1) * LANES]
            vb = jnp.where(_half_mask(hh % 2), vb, jnp.zeros_like(vb))
            new_accs[p] = new_accs[p] + _dot(probs[hh], vb)
        return new_runs, tuple(new_accs)

    init = (tuple(jnp.zeros((tq, 1), F32) for _ in range(N_HEADS)),
            tuple(jnp.zeros((tq, LANES), F32) for _ in range(N_HEADS // 2)))
    def still_active(runs):
        top = functools.reduce(jnp.maximum, runs)
        return (jnp.max(top) > SB_RUN_FLOOR).astype(I32)

    runs, accs = block(i, init, True)

    def cond(c):
        return (c[0] < i) & (c[1] > 0)

    def body(c):
        jj, _, runs, accs = c
        runs, accs = block(i - 1 - jj, (runs, accs), False)
        return jj + 1, still_active(runs), runs, accs

    _, _, _, accs = lax.while_loop(cond, body, (jnp.int32(0), still_active(runs), runs, accs))
    o_ref[...] = jnp.concatenate(accs, axis=1).astype(o_ref.dtype)


def _sb_attention(q, k, v, batch, seq):
    tq = TQ_ATT
    q3, k3, v3 = (t.reshape(batch, seq, WIDTH) for t in (q, k, v))
    out = pl.pallas_call(
        _sb_kernel,
        grid=(batch, seq // tq),
        in_specs=[pl.BlockSpec((None, tq, WIDTH), lambda b, i: (b, i, 0)),
                  pl.BlockSpec((None, seq, WIDTH), lambda b, i: (b, 0, 0)),
                  pl.BlockSpec((None, seq, WIDTH), lambda b, i: (b, 0, 0))],
        out_specs=pl.BlockSpec((None, tq, WIDTH), lambda b, i: (b, i, 0)),
        out_shape=jax.ShapeDtypeStruct((batch, seq, WIDTH), BF16),
        compiler_params=_cparams(("parallel", "arbitrary")),
        name="stick_breaking",
    )(q3, k3, v3)
    return out.reshape(batch * seq, WIDTH)


def _rel_bucket(dist):
    exact = REL_BUCKETS // 2
    n = jnp.maximum(dist, 0)
    nf = jnp.maximum(n, 1).astype(F32)
    large = exact + (jnp.log(nf / exact) / math.log(REL_MAX_DIST / exact) * (REL_BUCKETS - exact)).astype(I32)
    large = jnp.clip(large, 0, REL_BUCKETS - 1)
    return jnp.where(n < exact, n, large)


def _swa_kernel(sink_ref, tab_ref, q_ref, kc_ref, kh_ref, vc_ref, vh_ref, pq_ref, pkc_ref, pkh_ref, o_ref):
    w = SWA_WINDOW
    step = pl.program_id(1)
    row = lax.broadcasted_iota(I32, (w, w), 0)
    col = lax.broadcasted_iota(I32, (w, w), 1)
    valid_c = col <= row
    valid_p = col > row
    tabs = [jnp.broadcast_to(tab_ref[hh:hh + 1, :], (w, LANES)) for hh in range(N_HEADS)]
    ones = jnp.ones((1, LANES), BF16)
    nsub = q_ref.shape[0] // w
    chains = [(r, hh) for r in range(nsub) for hh in range(N_HEADS)]

    def real(hh):
        return (hh % 2) * 2 + hh // 2

    def keys(ref, halo_ref, r):
        cur = ref[r * w:(r + 1) * w, :]
        prev = ref[(r - 1) * w:r * w, :] if r else halo_ref[...]
        return cur, prev

    buckets = []
    for r in range(nsub):
        pq = pq_ref[r * w:(r + 1) * w, :]
        pk_prev = pkc_ref[:, (r - 1) * w:r * w] if r else pkh_ref[...]
        buckets.append((_rel_bucket(pq - pkc_ref[:, r * w:(r + 1) * w]), _rel_bucket(pq - pk_prev)))
    logits = []
    for r, hh in chains:
        qp = q_ref[r * w:(r + 1) * w, (hh // 2) * LANES:(hh // 2 + 1) * LANES]
        qh = jnp.where(_half_mask(hh % 2), qp, jnp.zeros_like(qp))
        kc, kp = keys(kc_ref, kh_ref, r)
        logits.append((_dot_nt(qh, kc), _dot_nt(qh, kp)))
    masked = []
    for (r, hh), (lc, lp) in zip(chains, logits):
        lc = jnp.where(valid_c, lc + jnp.take_along_axis(tabs[real(hh)], buckets[r][0], axis=1), NEG_BIG)
        lp = lp + jnp.take_along_axis(tabs[real(hh)], buckets[r][1], axis=1)
        lp = jnp.where(valid_p if r else valid_p & (step > 0), lp, NEG_BIG)
        masked.append((lc, lp))
    maxes = [jnp.maximum(jnp.maximum(jnp.max(lc, axis=-1, keepdims=True), jnp.max(lp, axis=-1, keepdims=True)),
                         sink_ref[real(hh)]) for (r, hh), (lc, lp) in zip(chains, masked)]
    probs = [(jnp.exp(lc - m).astype(BF16), jnp.exp(lp - m).astype(BF16)) for (lc, lp), m in zip(masked, maxes)]
    outs = {}
    for (r, hh), (ec, ep), m in zip(chains, probs, maxes):
        vc, vp = keys(vc_ref, vh_ref, r)
        mine = _half_mask(hh % 2)
        acc = _dot(ec, jnp.where(mine, vc, ones)) + _dot(ep, jnp.where(mine, vp, ones))
        den = (acc[:, 0:1] if hh % 2 else acc[:, HEAD_DIM:HEAD_DIM + 1]) + jnp.exp(sink_ref[real(hh)] - m)
        outs[(r, hh)] = acc / den
    for r in range(nsub):
        pairs = [jnp.where(_half_mask(0), outs[(r, 2 * p)], outs[(r, 2 * p + 1)]) for p in range(N_HEADS // 2)]
        o_ref[r * w:(r + 1) * w, :] = jnp.concatenate(pairs, axis=1).astype(o_ref.dtype)


def _swa_attention(q, k, v, positions, sinks, rel_table, batch, seq):
    w = SWA_WINDOW
    tq = SWA_TQ
    per = tq // w
    kvw = k.shape[-1]
    q3, k3, v3 = (t.reshape(batch, seq, t.shape[-1]) for t in (q, k, v))
    pcol = positions.reshape(batch, seq, 1)
    prow = positions.reshape(batch, 1, seq)
    tab = jnp.zeros((N_HEADS, LANES), F32).at[:, :REL_BUCKETS].set(rel_table.astype(F32).T)
    cur = lambda b, n: (b, n, 0)
    halo = lambda b, n: (b, jnp.maximum(n * per - 1, 0), 0)
    out = pl.pallas_call(
        _swa_kernel,
        grid=(batch, seq // tq),
        in_specs=[pl.BlockSpec(memory_space=pltpu.SMEM), _const_spec((N_HEADS, LANES)),
                  pl.BlockSpec((None, tq, WIDTH), cur),
                  pl.BlockSpec((None, tq, kvw), cur), pl.BlockSpec((None, w, kvw), halo),
                  pl.BlockSpec((None, tq, kvw), cur), pl.BlockSpec((None, w, kvw), halo),
                  pl.BlockSpec((None, tq, 1), cur),
                  pl.BlockSpec((None, 1, tq), lambda b, n: (b, 0, n)),
                  pl.BlockSpec((None, 1, w), lambda b, n: (b, 0, jnp.maximum(n * per - 1, 0)))],
        out_specs=pl.BlockSpec((None, tq, WIDTH), cur),
        out_shape=jax.ShapeDtypeStruct((batch, seq, WIDTH), BF16),
        compiler_params=_cparams(("parallel", "arbitrary")),
        name="swa_attention",
    )(sinks.astype(F32), tab, q3, k3, k3, v3, v3, pcol, prow, prow)
    return out.reshape(batch * seq, WIDTH)


def _hgrn_kernel(hg_ref, lb_ref, nw_ref, o_ref, state_ref):
    c = HGRN_CHUNK
    blk = HGRN_BLOCK

    @pl.when(pl.program_id(1) == 0)
    def _():
        state_ref[...] = jnp.zeros_like(state_ref)

    r64 = lax.broadcasted_iota(I32, (c, c), 0)
    c64 = lax.broadcasted_iota(I32, (c, c), 1)
    incl = (c64 <= r64).astype(BF16)
    ra = lax.broadcasted_iota(I32, (WIDTH, WIDTH), 0) // HEAD_DIM
    ca = lax.broadcasted_iota(I32, (WIDTH, WIDTH), 1) // HEAD_DIM
    same_head = ra == ca
    seg = same_head.astype(BF16)
    ones_cols = jnp.ones((c, LANES), BF16)
    trow = lax.broadcasted_iota(I32, (blk, WIDTH), 0)
    caps = [jnp.where(trow >= s_i, 0.0, NEG_BIG) for s_i in range(blk)]
    lane_head = lax.broadcasted_iota(I32, (1, WIDTH), 1) // HEAD_DIM
    lb = lb_ref[...]
    nw = nw_ref[...]
    dn0 = (((0,), (0,)), ((), ()))

    for ch in range(hg_ref.shape[0] // c):
        rows = slice(ch * c, (ch + 1) * c)
        qraw = hg_ref[rows, 0:WIDTH]
        fraw = hg_ref[rows, WIDTH:2 * WIDTH]
        v = hg_ref[rows, 2 * WIDTH:3 * WIDTH]
        graw = hg_ref[rows, 3 * WIDTH:4 * WIDTH]
        qf = qraw * jax.nn.sigmoid(qraw)
        forget = lb + (1.0 - lb) * jax.nn.sigmoid(fraw)
        lf = jnp.log(forget)
        kk = 1.0 - forget
        gate = graw * jax.nn.sigmoid(graw)
        vb = v.astype(BF16)

        lf3 = _split3(lf)
        bc = _dot(incl, lf3[0]) + _dot(incl, lf3[1]) + _dot(incl, lf3[2])
        b_last = bc[c - 1:c, :]
        tot_col = sum(lax.dot_general(t, ones_cols, dn0, preferred_element_type=F32) for t in lf3)
        decay_col = jnp.exp(jnp.concatenate([tot_col, tot_col], axis=1))

        state = state_ref[...]
        o_inter = _dot((qf * jnp.exp(bc)).astype(BF16), state.astype(BF16))

        def before(qa, qb, ka, kb):
            ref = bc[kb - 1:kb, :]
            qt = qf[qa:qb] * jnp.exp(bc[qa:qb] - ref)
            kt = (kk[ka:kb] * jnp.exp(ref - bc[ka:kb])).astype(BF16)
            qs = jnp.concatenate([jnp.where(lane_head == hh, qt, 0.0) for hh in range(N_HEADS)], axis=0)
            att = _dot_nt(qs.astype(BF16), kt)
            mix = _dot(att.astype(BF16), vb[ka:kb])
            nq = qb - qa
            return sum(jnp.where(lane_head == hh, mix[hh * nq:(hh + 1) * nq], 0.0) for hh in range(N_HEADS))

        bc2 = bc * LOG2E
        key2 = bc2 - jnp.log2(jnp.maximum(kk, 0.0))

        def inside(a):
            b2 = bc2[a:a + blk]
            qb_ = qf[a:a + blk]
            ws = []
            for s_i in range(blk):
                ws.append(qb_ * jnp.exp2(jnp.minimum(b2 - key2[a + s_i:a + s_i + 1, :], caps[s_i])))
            att = _dot(jnp.concatenate(ws, axis=0).astype(BF16), seg)
            return sum(att[s_i * blk:(s_i + 1) * blk] * v[a + s_i:a + s_i + 1, :] for s_i in range(blk))

        pieces = {a: [] for a in range(0, c, blk)}

        def cover(a, b):
            if b - a == blk:
                pieces[a].append(inside(a))
                return
            mid = (a + b) // 2
            cover(a, mid)
            cover(mid, b)
            res = before(mid, b, a, mid)
            for off in range(0, b - mid, blk):
                pieces[mid + off].append(res[off:off + blk])

        cover(0, c)
        o = o_inter + jnp.concatenate([sum(pieces[a]) for a in range(0, c, blk)], axis=0)

        khat = (kk * jnp.exp(b_last - bc)).astype(BF16)
        upd = lax.dot_general(khat, vb, dn0, preferred_element_type=F32)
        state_ref[...] = decay_col * state + jnp.where(same_head, upd, 0.0)

        o2 = _split3(o * o)
        ms = (_dot(o2[0], seg) + _dot(o2[1], seg)) * (1.0 / HEAD_DIM)
        o_ref[rows, :] = (o * lax.rsqrt(ms + EPS) * nw * gate).astype(o_ref.dtype)


def _hgrn(hg, lower_bound, norm_w, batch, seq):
    rows = HG_ROWS
    hg3 = hg.reshape(batch, seq, 4 * WIDTH)
    out = pl.pallas_call(
        _hgrn_kernel,
        grid=(batch, seq // rows),
        in_specs=[pl.BlockSpec((None, rows, 4 * WIDTH), lambda b, i: (b, i, 0)),
                  _const_spec((1, WIDTH)), _const_spec((1, WIDTH))],
        out_specs=pl.BlockSpec((None, rows, WIDTH), lambda b, i: (b, i, 0)),
        out_shape=jax.ShapeDtypeStruct((batch, seq, WIDTH), BF16),
        scratch_shapes=[pltpu.VMEM((WIDTH, WIDTH), F32)],
        compiler_params=_cparams(("parallel", "arbitrary")),
        name="hgrn2",
    )(hg3, lower_bound.reshape(1, WIDTH).astype(F32), norm_w.reshape(1, WIDTH).astype(F32))
    return out.reshape(batch * seq, WIDTH)


def _merge_body(x, y_refs, wg_ref, wb_ref, wo_ref, g, b):
    xb = x.astype(BF16)
    merged = jnp.zeros(x.shape, F32)
    for nbr, y_ref in enumerate(y_refs):
        gate = jax.nn.sigmoid(_dot(xb, wg_ref[:, nbr * D_MODEL:(nbr + 1) * D_MODEL]))
        merged = merged + gate * _dot(y_ref[...], wb_ref[nbr])
    y = _dot(merged.astype(BF16), wo_ref[...])
    return _layernorm(ALPHA * x + y, g, b)


def _memkv_kernel(m_ref, w_ref, k_ref, v_ref):
    kv = _dot(m_ref[...].astype(BF16), w_ref[...])
    k_ref[...] = kv[:, :WIDTH].astype(BF16)
    v_ref[...] = kv[:, WIDTH:].astype(BF16)


def _memkv(mem, wkv):
    batch, m, _ = mem.shape
    return pl.pallas_call(
        _memkv_kernel,
        grid=(batch,),
        in_specs=[pl.BlockSpec((None, m, D_MODEL), lambda b: (b, 0, 0)), _const_spec(wkv.shape)],
        out_specs=[pl.BlockSpec((None, m, WIDTH), lambda b: (b, 0, 0))] * 2,
        out_shape=[jax.ShapeDtypeStruct((batch, m, WIDTH), BF16)] * 2,
        compiler_params=_cparams(("parallel",)),
        name="mem_kv",
    )(mem, wkv)


def _xattn_body(x, wq_ref, k, v, wo_ref, g, b):
    q = _dot(x.astype(BF16), wq_ref[...]).astype(BF16)
    lane = lax.broadcasted_iota(I32, (1, WIDTH), 1) // HEAD_DIM
    heads = range(N_HEADS)
    ss = [_dot_nt(jnp.where(lane == hh, q, jnp.zeros_like(q)), k) for hh in heads]
    es = [jnp.exp(s - jnp.max(s, axis=-1, keepdims=True)) for s in ss]
    ps = [(e / jnp.sum(e, axis=-1, keepdims=True)).astype(BF16) for e in es]
    o = jnp.zeros((x.shape[0], WIDTH), F32)
    for hh in heads:
        o = o + jnp.where(lane == hh, _dot(ps[hh], v), 0.0)
    y = _dot(o.astype(BF16), wo_ref[...])
    return _layernorm(ALPHA * x + y, g, b)


def _merge_xattn_kernel(x_ref, y0_ref, y1_ref, y2_ref, y3_ref, wg_ref, wb_ref, wo_ref, g1_ref, b1_ref,
                        wq_ref, k_ref, v_ref, xwo_ref, g2_ref, b2_ref, o_ref):
    x1 = _merge_body(x_ref[...], (y0_ref, y1_ref, y2_ref, y3_ref), wg_ref, wb_ref, wo_ref,
                     g1_ref[...], b1_ref[...])
    o_ref[...] = _xattn_body(x1, wq_ref, k_ref[...], v_ref[...], xwo_ref, g2_ref[...], b2_ref[...])


def _merge_xattn(x2d, ys, wg, wb, wo, g1, b1, wq, k, v, xwo, g2, b2, batch, seq):
    tm = TM_A
    m = k.shape[1]
    per = seq // tm
    row = lambda w: pl.BlockSpec((tm, w), lambda bb, i: (bb * per + i, 0))
    kv_spec = pl.BlockSpec((None, m, WIDTH), lambda bb, i: (bb, 0, 0))
    vec = _const_spec((1, D_MODEL))
    return pl.pallas_call(
        _merge_xattn_kernel,
        grid=(batch, per),
        in_specs=[row(D_MODEL)] + [row(WIDTH)] * 4 +
                 [_const_spec(wg.shape), _const_spec(wb.shape), _const_spec(wo.shape), vec, vec,
                  _const_spec(wq.shape), kv_spec, kv_spec, _const_spec(xwo.shape), vec, vec],
        out_specs=row(D_MODEL),
        out_shape=jax.ShapeDtypeStruct((batch * seq, D_MODEL), F32),
        compiler_params=_cparams(("parallel", "parallel")),
        name="merge_xattn_ln",
    )(x2d, *ys, wg, wb, wo, g1.reshape(1, -1), b1.reshape(1, -1),
      wq, k, v, xwo, g2.reshape(1, -1), b2.reshape(1, -1))


def _ffn_kernel(x_ref, w13_ref, w2_ref, g_ref, b_ref, o_ref):
    x = x_ref[...]
    xb = x.astype(BF16)
    y = None
    for h in range(F_DENSE // TF_FFN):
        lo = h * TF_FFN
        a = _dot(xb, w13_ref[:, lo:lo + TF_FFN])
        gate = _dot(xb, w13_ref[:, F_DENSE + lo:F_DENSE + lo + TF_FFN])
        part = _dot((a * jax.nn.sigmoid(a) * gate).astype(BF16), w2_ref[lo:lo + TF_FFN, :])
        y = part if y is None else y + part
    o_ref[...] = _layernorm(ALPHA * x + y, g_ref[...], b_ref[...])


def _ffn(x2d, w13, w2, g, b):
    n = x2d.shape[0]
    tm = TM_FFN
    return pl.pallas_call(
        _ffn_kernel,
        grid=(n // tm,),
        in_specs=[pl.BlockSpec((tm, D_MODEL), lambda i: (i, 0)),
                  _const_spec(w13.shape), _const_spec(w2.shape),
                  _const_spec((1, D_MODEL)), _const_spec((1, D_MODEL))],
        out_specs=pl.BlockSpec((tm, D_MODEL), lambda i: (i, 0)),
        out_shape=jax.ShapeDtypeStruct((n, D_MODEL), F32),
        compiler_params=_cparams(("parallel",)),
        name="ffn_ln",
    )(x2d, w13, w2, g.reshape(1, -1), b.reshape(1, -1))


def _router_kernel(x_ref, r_ref, info_ref, wts_ref, cnt_ref, carry_ref):
    tm = x_ref.shape[0]

    @pl.when(pl.program_id(0) == 0)
    def _():
        carry_ref[...] = jnp.zeros_like(carry_ref)

    logits = jnp.dot(x_ref[...], r_ref[...], precision=lax.Precision.HIGHEST, preferred_element_type=F32)
    lane = lax.broadcasted_iota(I32, (tm, LANES), 1)
    lg = jnp.where(lane < N_EXPERTS, logits, -jnp.inf)
    m1 = jnp.max(lg, axis=-1, keepdims=True)
    i1 = jnp.min(jnp.where(lg == m1, lane, LANES), axis=-1, keepdims=True)
    lg2 = jnp.where(lane == i1, -jnp.inf, lg)
    m2 = jnp.max(lg2, axis=-1, keepdims=True)
    i2 = jnp.min(jnp.where(lg2 == m2, lane, LANES), axis=-1, keepdims=True)
    e = jnp.exp(m2 - m1)
    w1 = 1.0 / (1.0 + e)
    w2 = e / (1.0 + e)
    sel1 = lane == i1
    sel2 = lane == i2
    chosen = jnp.where(sel1 | sel2, 1.0, 0.0)
    row = lax.broadcasted_iota(I32, (tm, tm), 0)
    col = lax.broadcasted_iota(I32, (tm, tm), 1)
    before = (col < row).astype(BF16)
    ranks = _dot(before, chosen.astype(BF16)) + carry_ref[...]
    r1 = jnp.sum(jnp.where(sel1, ranks, 0.0), axis=-1, keepdims=True)
    r2 = jnp.sum(jnp.where(sel2, ranks, 0.0), axis=-1, keepdims=True)
    carry_ref[...] = carry_ref[...] + jnp.sum(chosen, axis=0, keepdims=True)
    info = jnp.where(lane == 0, i1.astype(F32), jnp.where(lane == 1, i2.astype(F32),
                     jnp.where(lane == 2, r1, jnp.where(lane == 3, r2, 0.0))))
    info_ref[...] = jnp.transpose(info)[:info_ref.shape[0], :]
    wts_ref[...] = jnp.where(lane == 0, w1, jnp.where(lane == 1, w2, 0.0))
    cnt_ref[...] = carry_ref[...]


def _router(x2d, router):
    n = x2d.shape[0]
    tm = TM_ROUTER
    r_pad = jnp.zeros((D_MODEL, LANES), F32).at[:, :N_EXPERTS].set(router.astype(F32))
    row = pl.BlockSpec((tm, LANES), lambda i: (i, 0))
    return pl.pallas_call(
        _router_kernel,
        grid=(n // tm,),
        in_specs=[pl.BlockSpec((tm, D_MODEL), lambda i: (i, 0)), _const_spec(r_pad.shape)],
        out_specs=[pl.BlockSpec((8, tm), lambda i: (0, i)), row, pl.BlockSpec((1, LANES), lambda i: (0, 0))],
        out_shape=[jax.ShapeDtypeStruct((8, n), F32), jax.ShapeDtypeStruct((n, LANES), F32),
                   jax.ShapeDtypeStruct((1, LANES), F32)],
        scratch_shapes=[pltpu.VMEM((1, LANES), F32)],
        compiler_params=_cparams(("arbitrary",)),
        name="moe_router",
    )(x2d, r_pad)


def _dispatch_kernel(pad_ref, d0_ref, d1_ref, x_ref, xb_hbm, stage_ref, sems):
    tm = x_ref.shape[0]
    i = pl.program_id(0)
    last = pl.num_programs(0) - 1
    slot = i % 2

    def wait_step(s):
        for _ in range(2):
            pltpu.make_async_copy(stage_ref.at[s], xb_hbm.at[pl.ds(0, tm), :], sems.at[s]).wait()

    @pl.when(i >= 2)
    def _():
        wait_step(slot)

    for s in range(2):
        @pl.when(slot == s)
        def _():
            stage_ref[s] = x_ref[...]

            def issue(r, c):
                for k in range(2):
                    pltpu.make_async_copy(stage_ref.at[s, pl.ds(r, 1), :],
                                          xb_hbm.at[pl.ds((d0_ref, d1_ref)[k][0, r], 1), :], sems.at[s]).start()
                return c
            lax.fori_loop(0, tm, issue, 0, unroll=8)

    @pl.when(i == last)
    def _():
        def fill(e, c):
            def one(s, c2):
                pltpu.make_async_copy(stage_ref.at[slot, pl.ds(0, 1), :], xb_hbm.at[pl.ds(s, 1), :],
                                      sems.at[2]).start()
                return c2

            def done(s, c2):
                pltpu.make_async_copy(stage_ref.at[slot, pl.ds(0, 1), :], xb_hbm.at[pl.ds(0, 1), :],
                                      sems.at[2]).wait()
                return c2
            lax.fori_loop(pad_ref[0, e], pad_ref[1, e], one, 0)
            lax.fori_loop(pad_ref[0, e], pad_ref[1, e], done, 0)
            return c
        lax.fori_loop(0, pad_ref.shape[1], fill, 0)
        wait_step(slot)

        @pl.when(last >= 1)
        def _():
            wait_step(1 - slot)


def _dispatch(x2d, dest, pads, nblk):
    n = x2d.shape[0]
    tm = TM_DISP
    nt = n // tm
    grid_spec = pltpu.PrefetchScalarGridSpec(
        num_scalar_prefetch=1,
        grid=(nt,),
        in_specs=[pl.BlockSpec((None, 1, tm), lambda i, pads: (i, 0, 0), memory_space=pltpu.SMEM),
                  pl.BlockSpec((None, 1, tm), lambda i, pads: (i, 0, 0), memory_space=pltpu.SMEM),
                  pl.BlockSpec((tm, D_MODEL), lambda i, pads: (i, 0))],
        out_specs=pl.BlockSpec(memory_space=pl.ANY),
        scratch_shapes=[pltpu.VMEM((2, tm, D_MODEL), F32), pltpu.SemaphoreType.DMA((3,))],
    )
    return pl.pallas_call(
        _dispatch_kernel,
        grid_spec=grid_spec,
        out_shape=jax.ShapeDtypeStruct((nblk * MOE_TB, D_MODEL), F32),
        compiler_params=_cparams(("arbitrary",), disable_bounds_checks=True),
        name="moe_dispatch",
    )(pads, dest[0].reshape(nt, 1, tm), dest[1].reshape(nt, 1, tm), x2d)


def _expert_kernel(nused_ref, bexp_ref, x_ref, w1_ref, w3_ref, w2_ref, o_ref, acc_ref):
    f = pl.program_id(1)

    @pl.when(pl.program_id(0) < nused_ref[0])
    def _():
        xb = x_ref[...].astype(BF16)
        a = _dot(xb, w1_ref[...])
        gate = _dot(xb, w3_ref[...])
        part = _dot((a * jax.nn.sigmoid(a) * gate).astype(BF16), w2_ref[...])

        @pl.when(f == 0)
        def _():
            acc_ref[...] = part

        @pl.when(f > 0)
        def _():
            acc_ref[...] += part

        @pl.when(f == pl.num_programs(1) - 1)
        def _():
            o_ref[...] = acc_ref[...]

    @pl.when(pl.program_id(0) >= nused_ref[0])
    def _():
        o_ref[...] = jnp.zeros_like(o_ref)


def _experts(xb, w13, w2, nused, blk_exp, nblk):
    tb, tf = MOE_TB, MOE_TF
    nf = F_EXPERT // tf
    w13 = w13.astype(BF16)

    def blk(i, nu):
        return jnp.maximum(jnp.minimum(i, nu[0] - 1), 0)

    def ftile(i, f, nu):
        return jnp.where(i < nu[0], f, nf - 1)

    grid_spec = pltpu.PrefetchScalarGridSpec(
        num_scalar_prefetch=2,
        grid=(nblk, nf),
        in_specs=[pl.BlockSpec((tb, D_MODEL), lambda i, f, nu, be: (blk(i, nu), 0)),
                  pl.BlockSpec((None, D_MODEL, tf), lambda i, f, nu, be: (be[blk(i, nu)], 0, ftile(i, f, nu))),
                  pl.BlockSpec((None, D_MODEL, tf), lambda i, f, nu, be: (be[blk(i, nu)], 0, nf + ftile(i, f, nu))),
                  pl.BlockSpec((None, tf, D_MODEL), lambda i, f, nu, be: (be[blk(i, nu)], ftile(i, f, nu), 0))],
        out_specs=pl.BlockSpec((tb, D_MODEL), lambda i, f, nu, be: (i, 0)),
        scratch_shapes=[pltpu.VMEM((tb, D_MODEL), F32)],
    )
    return pl.pallas_call(
        _expert_kernel,
        grid_spec=grid_spec,
        out_shape=jax.ShapeDtypeStruct((nblk * tb, D_MODEL), F32),
        compiler_params=_cparams(("arbitrary", "arbitrary")),
        name="moe_experts",
    )(nused, blk_exp, xb, w13, w13, w2.astype(BF16))


def _combine_kernel(d0_ref, d1_ref, n0_ref, n1_ref, y_hbm, x_ref, wts_ref, g_ref, b_ref, o_ref, buf_ref, sems):
    tm = x_ref.shape[0]
    i = pl.program_id(0)
    slot = i % 2

    def gather(idx_refs, s):
        def issue(r, c):
            for k in range(2):
                pltpu.make_async_copy(y_hbm.at[pl.ds(idx_refs[k][0, r], 1), :],
                                      buf_ref.at[s, k, pl.ds(r, 1), :], sems.at[s]).start()
            return c
        lax.fori_loop(0, tm, issue, 0, unroll=8)

    @pl.when(i == 0)
    def _():
        gather((d0_ref, d1_ref), 0)

    for s in range(2):
        @pl.when((i + 1 < pl.num_programs(0)) & (slot != s))
        def _():
            gather((n0_ref, n1_ref), s)

    for k in range(2):
        pltpu.make_async_copy(y_hbm.at[pl.ds(0, tm), :], buf_ref.at[slot, k], sems.at[slot]).wait()
    wts = wts_ref[...]
    y = wts[:, 0:1] * buf_ref[slot, 0] + wts[:, 1:2] * buf_ref[slot, 1]
    o_ref[...] = _layernorm(ALPHA * x_ref[...] + y, g_ref[...], b_ref[...])


def _combine(yb, dest, x2d, wts, g, b):
    n = x2d.shape[0]
    tm = TM_COMB
    nt = n // tm
    row = lambda w: pl.BlockSpec((tm, w), lambda i: (i, 0))
    cur = pl.BlockSpec((None, 1, tm), lambda i: (i, 0, 0), memory_space=pltpu.SMEM)
    nxt = pl.BlockSpec((None, 1, tm), lambda i: (jnp.minimum(i + 1, nt - 1), 0, 0), memory_space=pltpu.SMEM)
    d0, d1 = (d.reshape(nt, 1, tm) for d in dest)
    return pl.pallas_call(
        _combine_kernel,
        grid=(nt,),
        in_specs=[cur, cur, nxt, nxt, pl.BlockSpec(memory_space=pl.ANY), row(D_MODEL), row(LANES),
                  _const_spec((1, D_MODEL)), _const_spec((1, D_MODEL))],
        out_specs=row(D_MODEL),
        out_shape=jax.ShapeDtypeStruct((n, D_MODEL), F32),
        scratch_shapes=[pltpu.VMEM((2, 2, tm, D_MODEL), F32), pltpu.SemaphoreType.DMA((2,))],
        compiler_params=_cparams(("arbitrary",), disable_bounds_checks=True),
        name="moe_combine_ln",
    )(d0, d1, d0, d1, yb, x2d, wts, g.reshape(1, -1), b.reshape(1, -1))


def _moe(x2d, router, w13, w2, g, b):
    n = x2d.shape[0]
    tb = MOE_TB
    info, wts, cnt = _router(x2d, router)
    counts = cnt[0, :N_EXPERTS].astype(I32)
    padded = (counts + tb - 1) // tb * tb
    pend = jnp.cumsum(padded)
    pstart = pend - padded
    info = info.astype(I32)
    dest = tuple((sum(jnp.where(info[k] == e, pstart[e], 0) for e in range(N_EXPERTS)) + info[2 + k]).astype(I32)
                 for k in range(2))
    nblk = -(-(2 * n + N_EXPERTS * (tb - 1)) // tb)
    pads = jnp.stack([jnp.append(pstart + counts, pend[-1]), jnp.append(pend, nblk * tb)]).astype(I32)
    nused = (pend[-1] // tb).astype(I32).reshape(1)
    first_row = jnp.arange(nblk, dtype=I32) * tb
    blk_exp = jnp.minimum(jnp.sum(pend[None, :] <= first_row[:, None], axis=1), N_EXPERTS - 1).astype(I32)
    xb = _dispatch(x2d, dest, pads, nblk)
    yb = _experts(xb, w13, w2, nused, blk_exp, nblk)
    return _combine(yb, dest, x2d, wts, g, b)


def kernel(x, mem, positions, rel_bias_table, hgrn_lb_logits, w_in, mla_q_norm, mla_w_uq, mla_kv_norm, mla_w_ukv, swa_sinks, hgrn_norm, w_branch, w_out, ln_g, ln_b, xa_wq, xa_wkv, xa_wo, ffn_w13, ffn_w2, moe_router, moe_w13, moe_w2):
    batch, seq, _ = x.shape
    n = batch * seq
    sm = jax.nn.softmax(hgrn_lb_logits.astype(F32), axis=0)
    lower_bounds = jnp.cumsum(sm, axis=0) - sm[0]
    ctab, stab = _rope_tables(positions)
    xc = x.reshape(n, D_MODEL)
    for l in range(DEPTH):
        wts = _inproj_weights(w_in[l], mla_w_uq[l], mla_w_ukv[l])
        mq, mk, mv, swq, swk, swv, hg, sbq, sbk, sbv = _inproj(xc, wts, ctab, stab, mla_q_norm[l], mla_kv_norm[l])
        y_mla = _mla_attention(mq, mk, mv, batch, seq)
        y_swa = _swa_attention(swq, swk, swv, positions, swa_sinks[l], rel_bias_table, batch, seq)
        y_hg = _hgrn(hg, lower_bounds[l], hgrn_norm[l], batch, seq)
        y_sb = _sb_attention(sbq, sbk, sbv, batch, seq)
        go = _IN_OFF['gates']
        mk_, mv_ = _memkv(mem, xa_wkv[l].astype(BF16))
        wb = w_branch[l].at[1].set(
            w_branch[l][1].reshape(N_HEADS, HEAD_DIM, D_MODEL)[jnp.array([0, 2, 1, 3])].reshape(WIDTH, D_MODEL))
        xc = _merge_xattn(xc, (y_mla, y_swa, y_hg, y_sb), w_in[l][:, go:].astype(BF16), wb.astype(BF16),
                          w_out[l].astype(BF16), ln_g[l, 0], ln_b[l, 0],
                          (xa_wq[l] * QK_SCALE).astype(BF16), mk_, mv_, xa_wo[l].astype(BF16),
                          ln_g[l, 1], ln_b[l, 1], batch, seq)
        if l % 2 == 0:
            xc = _ffn(xc, ffn_w13[l // 2].astype(BF16), ffn_w2[l // 2].astype(BF16), ln_g[l, 2], ln_b[l, 2])
        else:
            xc = _moe(xc, moe_router[l // 2], moe_w13[l // 2], moe_w2[l // 2],
                      ln_g[l, 2], ln_b[l, 2])
    return xc.reshape(batch, seq, D_MODEL)
```

```python
import functools
import math

import jax
import jax.numpy as jnp
from jax import lax
from jax.experimental import pallas as pl
from jax.experimental.pallas import tpu as pltpu

F32 = jnp.float32
BF16 = jnp.bfloat16
I32 = jnp.int32

D_MODEL = 1024
DEPTH = 2
EPS = 1e-5
NEG_BIG = -1e30
LANES = 128
HEAD_DIM = 64
N_HEADS = 4
WIDTH = N_HEADS * HEAD_DIM

MLA_Q_LORA = 256
MLA_KV_LORA = 128
MLA_NOPE = 64
MLA_ROPE = 32
ROPE_THETA = 10000.0
MLA_SCALE = (MLA_NOPE + MLA_ROPE) ** -0.5
LOG2E = math.log2(math.e)
QK_SCALE = HEAD_DIM ** -0.5

SB_RUN_FLOOR = -150.0
SWA_WINDOW = 128
REL_BUCKETS = 32
REL_MAX_DIST = 128
HGRN_CHUNK = 64
HGRN_BLOCK = 16
N_EXPERTS = 8
F_DENSE = 2816
F_EXPERT = 3584
ALPHA = (2 * DEPTH) ** 0.25

_IN_SPLITS = (('mla_cq', 256), ('mla_ckv', 128), ('mla_kr', 32), ('swa_q', 256), ('swa_k', 128),
              ('swa_v', 128), ('hgrn', 1024), ('sb_q', 256), ('sb_k', 256), ('sb_v', 256), ('gates', 4096))
_IN_OFF = {}
_o = 0
for _n, _w in _IN_SPLITS:
    _IN_OFF[_n] = _o
    _o += _w

_A_SPLITS = (('cq', 256), ('ckv', 128), ('kra', 128), ('krb', 128), ('swa_q', 256), ('swa_k', 128),
             ('swa_v', 128), ('hgrn', 1024), ('sb_q', 256), ('sb_k', 256), ('sb_v', 256))
_A_OFF = {}
_o = 0
for _n, _w in _A_SPLITS:
    _A_OFF[_n] = (_o, _o + _w)
    _o += _w
A_COLS = _o

TM_A = 512
TQ_ATT = 256
MLA_TQ = 512
MLA_TK = 512
MLA_WIDE = 4
MLA_GROUP = 4
SWA_TQ = 1024
HG_ROWS = 512
TM_FFN = 512
TF_FFN = 1408
MOE_TB = 512
MOE_TF = 1792
TM_ROUTER = 512
TM_COMB = 512
TM_DISP = 1024
VMEM_LIMIT = 56 * 1024 * 1024
assert MLA_WIDE == 4 and MLA_TQ % MLA_TK == 0


def _cparams(sem, **kw):
    return pltpu.CompilerParams(dimension_semantics=sem, vmem_limit_bytes=VMEM_LIMIT, **kw)


def _const_spec(shape):
    nd = len(shape)
    return pl.BlockSpec(shape, lambda *_: (0,) * nd, pipeline_mode=pl.Buffered(1))


def _layernorm(v, g, b):
    mu = jnp.mean(v, axis=-1, keepdims=True)
    vc = v - mu
    var = jnp.mean(vc * vc, axis=-1, keepdims=True)
    return vc * lax.rsqrt(var + EPS) * g + b


def _dot(a, b):
    return jnp.dot(a, b, preferred_element_type=F32)


def _dot_nt(a, b):
    return lax.dot_general(a, b, (((1,), (1,)), ((), ())), preferred_element_type=F32)


def _split3(a):
    hi = a.astype(BF16)
    r = a - hi.astype(F32)
    mid = r.astype(BF16)
    lo = (r - mid.astype(F32)).astype(BF16)
    return hi, mid, lo


def _rope_kernel(pos_ref, freq_ref, c_ref, s_ref):
    lane = lax.broadcasted_iota(I32, pos_ref.shape, 1)
    ang = pos_ref[...] * freq_ref[...]
    rope = (lane >= MLA_NOPE) & (lane < MLA_NOPE + MLA_ROPE)
    first = lane < MLA_NOPE + MLA_ROPE // 2
    c_ref[...] = jnp.where(lane < MLA_NOPE, 1.0, jnp.where(rope, jnp.cos(ang), 0.0))
    sn = jnp.sin(ang)
    s_ref[...] = jnp.where(rope, jnp.where(first, -sn, sn), 0.0)


def _rope_tables(positions):
    n = positions.size
    half = MLA_ROPE // 2
    inv_freq = ROPE_THETA ** (-jnp.arange(half, dtype=F32) / half)
    freq = jnp.zeros((1, LANES), F32).at[0, MLA_NOPE:MLA_NOPE + MLA_ROPE].set(jnp.tile(inv_freq, 2))
    posb = jnp.broadcast_to(positions.reshape(n, 1).astype(F32), (n, LANES))
    tm = 1024
    return pl.pallas_call(
        _rope_kernel,
        grid=(n // tm,),
        in_specs=[pl.BlockSpec((tm, LANES), lambda i: (i, 0)), _const_spec((1, LANES))],
        out_specs=[pl.BlockSpec((tm, LANES), lambda i: (i, 0))] * 2,
        out_shape=[jax.ShapeDtypeStruct((n, LANES), F32)] * 2,
        compiler_params=_cparams(("parallel",)),
        name="rope_tables",
    )(posb, freq)


def _inproj_kernel(x_ref, w_ref, c_ref, s_ref, qn_ref, kvn_ref, wuqa_ref, wuqb_ref, wuk_ref, wuv_ref,
                   mq_ref, mk_ref, mv_ref, swq_ref, swk_ref, swv_ref, hg_ref, sbq_ref, sbk_ref, sbv_ref):
    h = _dot(x_ref[...].astype(BF16), w_ref[...])

    def cols(name):
        lo, hi = _A_OFF[name]
        return h[:, lo:hi]

    c = c_ref[...]
    s = s_ref[...]
    c4 = jnp.concatenate([c] * N_HEADS, axis=1)
    s4 = jnp.concatenate([s] * N_HEADS, axis=1)

    cq = cols('cq')
    cqn = (cq * lax.rsqrt(jnp.mean(cq * cq, axis=-1, keepdims=True) + EPS) * qn_ref[...]).astype(BF16)
    q = _dot(cqn, wuqa_ref[...]) * c4 + _dot(cqn, wuqb_ref[...]) * s4
    mq_ref[...] = (q * (MLA_SCALE * LOG2E)).astype(BF16)

    ckv = cols('ckv')
    ckvn = (ckv * lax.rsqrt(jnp.mean(ckv * ckv, axis=-1, keepdims=True) + EPS) * kvn_ref[...]).astype(BF16)
    krot = cols('kra') * c + cols('krb') * s
    mk_ref[...] = (_dot(ckvn, wuk_ref[...]) + jnp.concatenate([krot] * N_HEADS, axis=1)).astype(BF16)
    mv_ref[...] = _dot(ckvn, wuv_ref[...]).astype(BF16)

    swq_ref[...] = cols('swa_q').astype(BF16)
    swk_ref[...] = cols('swa_k').astype(BF16)
    swv_ref[...] = cols('swa_v').astype(BF16)
    hg_ref[...] = cols('hgrn')
    sbq_ref[...] = cols('sb_q').astype(BF16)
    sbk_ref[...] = cols('sb_k').astype(BF16)
    sbv_ref[...] = cols('sb_v').astype(BF16)


def _inproj_weights(w_in, w_uq, w_ukv):
    def seg(name, width):
        o = _IN_OFF[name]
        return w_in[:, o:o + width]

    kr = seg('mla_kr', MLA_ROPE)
    half = MLA_ROPE // 2
    z64 = jnp.zeros((D_MODEL, MLA_NOPE), F32)
    z32 = jnp.zeros((D_MODEL, LANES - MLA_NOPE - MLA_ROPE), F32)
    kra = jnp.concatenate([z64, kr, z32], axis=1)
    krb = jnp.concatenate([z64, kr[:, half:], kr[:, :half], z32], axis=1)
    swq = seg('swa_q', 256).reshape(D_MODEL, N_HEADS, HEAD_DIM)[:, jnp.array([0, 2, 1, 3])].reshape(D_MODEL, WIDTH)
    w_a = jnp.concatenate([
        seg('mla_cq', 256), seg('mla_ckv', 128), kra, krb,
        swq * QK_SCALE, seg('swa_k', 128), seg('swa_v', 128),
        seg('hgrn', 1024), seg('sb_q', 256) * (QK_SCALE * LOG2E), seg('sb_k', 256), seg('sb_v', 256)], axis=1)

    qd = MLA_NOPE + MLA_ROPE
    zq = jnp.zeros((MLA_Q_LORA, LANES - qd), F32)
    zn = jnp.zeros((MLA_Q_LORA, MLA_NOPE), F32)
    qa, qb = [], []
    for hh in range(N_HEADS):
        nope = w_uq[:, hh * qd: hh * qd + MLA_NOPE]
        rope = w_uq[:, hh * qd + MLA_NOPE: (hh + 1) * qd]
        qa += [nope, rope, zq]
        qb += [zn, rope[:, half:], rope[:, :half], zq]
    wuqa = jnp.concatenate(qa, axis=1)
    wuqb = jnp.concatenate(qb, axis=1)
    lane = jnp.arange(N_HEADS * LANES) % LANES
    wuk = jnp.where(lane[None, :] < MLA_NOPE, w_ukv, 0.0)
    wuv = jnp.concatenate([w_ukv[:, hh * LANES + MLA_NOPE:(hh + 1) * LANES] for hh in range(N_HEADS)], axis=1)
    return tuple(t.astype(BF16) for t in (w_a, wuqa, wuqb, wuk, wuv))


def _inproj(x2d, wts, ctab, stab, q_norm, kv_norm):
    n = x2d.shape[0]
    w_a, wuqa, wuqb, wuk, wuv = wts
    tm = TM_A
    row = lambda w: pl.BlockSpec((tm, w), lambda i: (i, 0))
    out_w = (512, 512, 256, 256, 128, 128, 1024, 256, 256, 256)
    out_dt = (BF16, BF16, BF16, BF16, BF16, BF16, F32, BF16, BF16, BF16)
    return pl.pallas_call(
        _inproj_kernel,
        grid=(n // tm,),
        in_specs=[row(D_MODEL), _const_spec(w_a.shape), row(LANES), row(LANES),
                  _const_spec((1, MLA_Q_LORA)), _const_spec((1, MLA_KV_LORA)),
                  _const_spec(wuqa.shape), _const_spec(wuqb.shape), _const_spec(wuk.shape),
                  _const_spec(wuv.shape)],
        out_specs=[row(w) for w in out_w],
        out_shape=[jax.ShapeDtypeStruct((n, w), d) for w, d in zip(out_w, out_dt)],
        compiler_params=_cparams(("parallel",)),
        name="inproj",
    )(x2d, w_a, ctab, stab, q_norm.reshape(1, -1), kv_norm.reshape(1, -1), wuqa, wuqb, wuk, wuv)


def _half_mask(half):
    lane = lax.broadcasted_iota(I32, (1, LANES), 1)
    return (lane < HEAD_DIM) if half == 0 else (lane >= HEAD_DIM)


def _mla_kernel(q_ref, k_ref, v_ref, o_ref):
    tq = q_ref.shape[0]
    tk = MLA_TK
    nsub = tq // tk
    i = pl.program_id(1)
    row = lax.broadcasted_iota(I32, (tq, tk), 0)
    col = lax.broadcasted_iota(I32, (tq, tk), 1)
    ones = jnp.ones((1, LANES), BF16)

    def update(off, carry, heads, mask, width=tk):
        ss = [_dot_nt(q_ref[:, hh * LANES:(hh + 1) * LANES],
                      k_ref[pl.ds(off, width), hh * LANES:(hh + 1) * LANES]) for hh in heads]
        if mask is not None:
            ss = [jnp.where(mask, s, NEG_BIG) for s in ss]
        ms = [jnp.maximum(c[0], jnp.max(s, axis=-1, keepdims=True)) for c, s in zip(carry, ss)]
        pms = [jnp.exp2(s - m).astype(BF16) for s, m in zip(ss, ms)]
        new = []
        for n, hh in enumerate(heads):
            vb = v_ref[pl.ds(off, width), (hh // 2) * LANES:(hh // 2 + 1) * LANES]
            vb = jnp.where(_half_mask(hh % 2), vb, ones)
            m, acc = carry[n]
            new.append((ms[n], jnp.exp2(m - ms[n]) * acc + _dot(pms[n], vb)))
        return tuple(new)

    accs = []
    for g in range(0, N_HEADS, MLA_GROUP):
        heads = tuple(range(g, g + MLA_GROUP))
        init = tuple((jnp.full((tq, 1), NEG_BIG, F32), jnp.zeros((tq, LANES), F32)) for _ in heads)
        nkb = i * nsub
        wide = MLA_WIDE * tk
        carry = lax.fori_loop(
            0, nkb // MLA_WIDE,
            lambda j, c, heads=heads: update(pl.multiple_of(j * wide, wide), c, heads, None, wide), init)
        done = nkb // MLA_WIDE * MLA_WIDE
        rest = nkb - done
        carry = lax.cond(
            rest >= 2,
            lambda c, heads=heads: update(pl.multiple_of(done * tk, 2 * tk), c, heads, None, 2 * tk),
            lambda c: c, carry)
        carry = lax.cond(
            rest % 2 == 1,
            lambda c, heads=heads: update(pl.multiple_of((nkb - 1) * tk, tk), c, heads, None),
            lambda c: c, carry)
        for r in range(nsub):
            carry = update(pl.multiple_of(i * tq + r * tk, tk), carry, heads, col + r * tk <= row)
        accs += [c[1] for c in carry]
    outs = []
    for p in range(N_HEADS // 2):
        a0, a1 = accs[2 * p], accs[2 * p + 1]
        outs.append(jnp.where(_half_mask(0), a0 / a0[:, HEAD_DIM:HEAD_DIM + 1], a1 / a1[:, 0:1]))
    o_ref[...] = jnp.concatenate(outs, axis=1).astype(o_ref.dtype)


def _mla_attention(q, k, v, batch, seq):
    tq = MLA_TQ
    q3, k3, v3 = (t.reshape(batch, seq, t.shape[-1]) for t in (q, k, v))
    out = pl.pallas_call(
        _mla_kernel,
        grid=(batch, seq // tq),
        in_specs=[pl.BlockSpec((None, tq, 512), lambda b, i: (b, i, 0)),
                  pl.BlockSpec((None, seq, 512), lambda b, i: (b, 0, 0), pipeline_mode=pl.Buffered(1)),
                  pl.BlockSpec((None, seq, WIDTH), lambda b, i: (b, 0, 0), pipeline_mode=pl.Buffered(1))],
        out_specs=pl.BlockSpec((None, tq, WIDTH), lambda b, i: (b, i, 0)),
        out_shape=jax.ShapeDtypeStruct((batch, seq, WIDTH), BF16),
        compiler_params=_cparams(("parallel", "arbitrary")),
        name="mla_attention",
    )(q3, k3, v3)
    return out.reshape(batch * seq, WIDTH)


def _sb_kernel(q_ref, k_ref, v_ref, o_ref):
    tq = q_ref.shape[0]
    i = pl.program_id(1)
    row = lax.broadcasted_iota(I32, (tq, tq), 0)
    col = lax.broadcasted_iota(I32, (tq, tq), 1)
    strict = col < row
    later = (row > col).astype(BF16)
    qs = []
    for hh in range(N_HEADS):
        qp = q_ref[:, (hh // 2) * LANES:(hh // 2 + 1) * LANES]
        qs.append(jnp.where(_half_mask(hh % 2), qp, jnp.zeros_like(qp)))

    def block(j, carry, diag):
        off = pl.multiple_of(j * tq, tq)
        runs, accs = carry
        heads = range(N_HEADS)
        zs = [_dot_nt(qs[hh], k_ref[pl.ds(off, tq), (hh // 2) * LANES:(hh // 2 + 1) * LANES]) for hh in heads]
        lsps = [jnp.minimum(z, 0.0) - jnp.log2(1.0 + jnp.exp2(-jnp.abs(z))) for z in zs]
        lsns = [lsp - z for lsp, z in zip(lsps, zs)]
        if diag:
            lsns = [jnp.where(strict, t, 0.0) for t in lsns]
        his = [t.astype(BF16) for t in lsns]
        los = [(t - hi.astype(F32)).astype(BF16) for t, hi in zip(lsns, his)]
        rems = [_dot(hi, later) + _dot(lo, later) for hi, lo in zip(his, los)]
        args = [lsps[hh] + rems[hh] + runs[hh] for hh in heads]
        if diag:
            args = [jnp.where(strict, t, NEG_BIG) for t in args]
        probs = [jnp.exp2(t).astype(BF16) for t in args]
        new_runs = tuple(runs[hh] + rems[hh][:, 0:1] + lsns[hh][:, 0:1] for hh in heads)
        new_accs = list(accs)
        for hh in heads:
            p = hh // 2
            vb = v_ref[pl.ds(off, tq), p * LANES:(p + 1) * LANES]
            vb = jnp.where(_half_mask(hh % 2), vb, jnp.zeros_like(vb))
            new_accs[p] = new_accs[p] + _dot(probs[hh], vb)
        return new_runs, tuple(new_accs)

    init = (tuple(jnp.zeros((tq, 1), F32) for _ in range(N_HEADS)),
            tuple(jnp.zeros((tq, LANES), F32) for _ in range(N_HEADS // 2)))
    def still_active(runs):
        top = functools.reduce(jnp.maximum, runs)
        return (jnp.max(top) > SB_RUN_FLOOR).astype(I32)

    runs, accs = block(i, init, True)

    def cond(c):
        return (c[0] < i) & (c[1] > 0)

    def body(c):
        jj, _, runs, accs = c
        runs, accs = block(i - 1 - jj, (runs, accs), False)
        return jj + 1, still_active(runs), runs, accs

    _, _, _, accs = lax.while_loop(cond, body, (jnp.int32(0), still_active(runs), runs, accs))
    o_ref[...] = jnp.concatenate(accs, axis=1).astype(o_ref.dtype)


def _sb_attention(q, k, v, batch, seq):
    tq = TQ_ATT
    q3, k3, v3 = (t.reshape(batch, seq, WIDTH) for t in (q, k, v))
    out = pl.pallas_call(
        _sb_kernel,
        grid=(batch, seq // tq),
        in_specs=[pl.BlockSpec((None, tq, WIDTH), lambda b, i: (b, i, 0)),
                  pl.BlockSpec((None, seq, WIDTH), lambda b, i: (b, 0, 0)),
                  pl.BlockSpec((None, seq, WIDTH), lambda b, i: (b, 0, 0))],
        out_specs=pl.BlockSpec((None, tq, WIDTH), lambda b, i: (b, i, 0)),
        out_shape=jax.ShapeDtypeStruct((batch, seq, WIDTH), BF16),
        compiler_params=_cparams(("parallel", "arbitrary")),
        name="stick_breaking",
    )(q3, k3, v3)
    return out.reshape(batch * seq, WIDTH)


def _rel_bucket(dist):
    exact = REL_BUCKETS // 2
    n = jnp.maximum(dist, 0)
    nf = jnp.maximum(n, 1).astype(F32)
    large = exact + (jnp.log(nf / exact) / math.log(REL_MAX_DIST / exact) * (REL_BUCKETS - exact)).astype(I32)
    large = jnp.clip(large, 0, REL_BUCKETS - 1)
    return jnp.where(n < exact, n, large)


def _swa_kernel(sink_ref, tab_ref, q_ref, kc_ref, kh_ref, vc_ref, vh_ref, pq_ref, pkc_ref, pkh_ref, o_ref):
    w = SWA_WINDOW
    step = pl.program_id(1)
    row = lax.broadcasted_iota(I32, (w, w), 0)
    col = lax.broadcasted_iota(I32, (w, w), 1)
    valid_c = col <= row
    valid_p = col > row
    tabs = [jnp.broadcast_to(tab_ref[hh:hh + 1, :], (w, LANES)) for hh in range(N_HEADS)]
    ones = jnp.ones((1, LANES), BF16)
    nsub = q_ref.shape[0] // w
    chains = [(r, hh) for r in range(nsub) for hh in range(N_HEADS)]

    def real(hh):
        return (hh % 2) * 2 + hh // 2

    def keys(ref, halo_ref, r):
        cur = ref[r * w:(r + 1) * w, :]
        prev = ref[(r - 1) * w:r * w, :] if r else halo_ref[...]
        return cur, prev

    buckets = []
    for r in range(nsub):
        pq = pq_ref[r * w:(r + 1) * w, :]
        pk_prev = pkc_ref[:, (r - 1) * w:r * w] if r else pkh_ref[...]
        buckets.append((_rel_bucket(pq - pkc_ref[:, r * w:(r + 1) * w]), _rel_bucket(pq - pk_prev)))
    logits = []
    for r, hh in chains:
        qp = q_ref[r * w:(r + 1) * w, (hh // 2) * LANES:(hh // 2 + 1) * LANES]
        qh = jnp.where(_half_mask(hh % 2), qp, jnp.zeros_like(qp))
        kc, kp = keys(kc_ref, kh_ref, r)
        logits.append((_dot_nt(qh, kc), _dot_nt(qh, kp)))
    masked = []
    for (r, hh), (lc, lp) in zip(chains, logits):
        lc = jnp.where(valid_c, lc + jnp.take_along_axis(tabs[real(hh)], buckets[r][0], axis=1), NEG_BIG)
        lp = lp + jnp.take_along_axis(tabs[real(hh)], buckets[r][1], axis=1)
        lp = jnp.where(valid_p if r else valid_p & (step > 0), lp, NEG_BIG)
        masked.append((lc, lp))
    maxes = [jnp.maximum(jnp.maximum(jnp.max(lc, axis=-1, keepdims=True), jnp.max(lp, axis=-1, keepdims=True)),
                         sink_ref[real(hh)]) for (r, hh), (lc, lp) in zip(chains, masked)]
    probs = [(jnp.exp(lc - m).astype(BF16), jnp.exp(lp - m).astype(BF16)) for (lc, lp), m in zip(masked, maxes)]
    outs = {}
    for (r, hh), (ec, ep), m in zip(chains, probs, maxes):
        vc, vp = keys(vc_ref, vh_ref, r)
        mine = _half_mask(hh % 2)
        acc = _dot(ec, jnp.where(mine, vc, ones)) + _dot(ep, jnp.where(mine, vp, ones))
        den = (acc[:, 0:1] if hh % 2 else acc[:, HEAD_DIM:HEAD_DIM + 1]) + jnp.exp(sink_ref[real(hh)] - m)
        outs[(r, hh)] = acc / den
    for r in range(nsub):
        pairs = [jnp.where(_half_mask(0), outs[(r, 2 * p)], outs[(r, 2 * p + 1)]) for p in range(N_HEADS // 2)]
        o_ref[r * w:(r + 1) * w, :] = jnp.concatenate(pairs, axis=1).astype(o_ref.dtype)


def _swa_attention(q, k, v, positions, sinks, rel_table, batch, seq):
    w = SWA_WINDOW
    tq = SWA_TQ
    per = tq // w
    kvw = k.shape[-1]
    q3, k3, v3 = (t.reshape(batch, seq, t.shape[-1]) for t in (q, k, v))
    pcol = positions.reshape(batch, seq, 1)
    prow = positions.reshape(batch, 1, seq)
    tab = jnp.zeros((N_HEADS, LANES), F32).at[:, :REL_BUCKETS].set(rel_table.astype(F32).T)
    cur = lambda b, n: (b, n, 0)
    halo = lambda b, n: (b, jnp.maximum(n * per - 1, 0), 0)
    out = pl.pallas_call(
        _swa_kernel,
        grid=(batch, seq // tq),
        in_specs=[pl.BlockSpec(memory_space=pltpu.SMEM), _const_spec((N_HEADS, LANES)),
                  pl.BlockSpec((None, tq, WIDTH), cur),
                  pl.BlockSpec((None, tq, kvw), cur), pl.BlockSpec((None, w, kvw), halo),
                  pl.BlockSpec((None, tq, kvw), cur), pl.BlockSpec((None, w, kvw), halo),
                  pl.BlockSpec((None, tq, 1), cur),
                  pl.BlockSpec((None, 1, tq), lambda b, n: (b, 0, n)),
                  pl.BlockSpec((None, 1, w), lambda b, n: (b, 0, jnp.maximum(n * per - 1, 0)))],
        out_specs=pl.BlockSpec((None, tq, WIDTH), cur),
        out_shape=jax.ShapeDtypeStruct((batch, seq, WIDTH), BF16),
        compiler_params=_cparams(("parallel", "arbitrary")),
        name="swa_attention",
    )(sinks.astype(F32), tab, q3, k3, k3, v3, v3, pcol, prow, prow)
    return out.reshape(batch * seq, WIDTH)


def _hgrn_kernel(hg_ref, lb_ref, nw_ref, o_ref, state_ref):
    c = HGRN_CHUNK
    blk = HGRN_BLOCK

    @pl.when(pl.program_id(1) == 0)
    def _():
        state_ref[...] = jnp.zeros_like(state_ref)

    r64 = lax.broadcasted_iota(I32, (c, c), 0)
    c64 = lax.broadcasted_iota(I32, (c, c), 1)
    incl = (c64 <= r64).astype(BF16)
    ra = lax.broadcasted_iota(I32, (WIDTH, WIDTH), 0) // HEAD_DIM
    ca = lax.broadcasted_iota(I32, (WIDTH, WIDTH), 1) // HEAD_DIM
    same_head = ra == ca
    seg = same_head.astype(BF16)
    ones_cols = jnp.ones((c, LANES), BF16)
    trow = lax.broadcasted_iota(I32, (blk, WIDTH), 0)
    caps = [jnp.where(trow >= s_i, 0.0, NEG_BIG) for s_i in range(blk)]
    lane_head = lax.broadcasted_iota(I32, (1, WIDTH), 1) // HEAD_DIM
    lb = lb_ref[...]
    nw = nw_ref[...]
    dn0 = (((0,), (0,)), ((), ()))

    for ch in range(hg_ref.shape[0] // c):
        rows = slice(ch * c, (ch + 1) * c)
        qraw = hg_ref[rows, 0:WIDTH]
        fraw = hg_ref[rows, WIDTH:2 * WIDTH]
        v = hg_ref[rows, 2 * WIDTH:3 * WIDTH]
        graw = hg_ref[rows, 3 * WIDTH:4 * WIDTH]
        qf = qraw * jax.nn.sigmoid(qraw)
        forget = lb + (1.0 - lb) * jax.nn.sigmoid(fraw)
        lf = jnp.log(forget)
        kk = 1.0 - forget
        gate = graw * jax.nn.sigmoid(graw)
        vb = v.astype(BF16)

        lf3 = _split3(lf)
        bc = _dot(incl, lf3[0]) + _dot(incl, lf3[1]) + _dot(incl, lf3[2])
        b_last = bc[c - 1:c, :]
        tot_col = sum(lax.dot_general(t, ones_cols, dn0, preferred_element_type=F32) for t in lf3)
        decay_col = jnp.exp(jnp.concatenate([tot_col, tot_col], axis=1))

        state = state_ref[...]
        o_inter = _dot((qf * jnp.exp(bc)).astype(BF16), state.astype(BF16))

        def before(qa, qb, ka, kb):
            ref = bc[kb - 1:kb, :]
            qt = qf[qa:qb] * jnp.exp(bc[qa:qb] - ref)
            kt = (kk[ka:kb] * jnp.exp(ref - bc[ka:kb])).astype(BF16)
            qs = jnp.concatenate([jnp.where(lane_head == hh, qt, 0.0) for hh in range(N_HEADS)], axis=0)
            att = _dot_nt(qs.astype(BF16), kt)
            mix = _dot(att.astype(BF16), vb[ka:kb])
            nq = qb - qa
            return sum(jnp.where(lane_head == hh, mix[hh * nq:(hh + 1) * nq], 0.0) for hh in range(N_HEADS))

        bc2 = bc * LOG2E
        key2 = bc2 - jnp.log2(jnp.maximum(kk, 0.0))

        def inside(a):
            b2 = bc2[a:a + blk]
            qb_ = qf[a:a + blk]
            ws = []
            for s_i in range(blk):
                ws.append(qb_ * jnp.exp2(jnp.minimum(b2 - key2[a + s_i:a + s_i + 1, :], caps[s_i])))
            att = _dot(jnp.concatenate(ws, axis=0).astype(BF16), seg)
            return sum(att[s_i * blk:(s_i + 1) * blk] * v[a + s_i:a + s_i + 1, :] for s_i in range(blk))

        pieces = {a: [] for a in range(0, c, blk)}

        def cover(a, b):
            if b - a == blk:
                pieces[a].append(inside(a))
                return
            mid = (a + b) // 2
            cover(a, mid)
            cover(mid, b)
            res = before(mid, b, a, mid)
            for off in range(0, b - mid, blk):
                pieces[mid + off].append(res[off:off + blk])

        cover(0, c)
        o = o_inter + jnp.concatenate([sum(pieces[a]) for a in range(0, c, blk)], axis=0)

        khat = (kk * jnp.exp(b_last - bc)).astype(BF16)
        upd = lax.dot_general(khat, vb, dn0, preferred_element_type=F32)
        state_ref[...] = decay_col * state + jnp.where(same_head, upd, 0.0)

        o2 = _split3(o * o)
        ms = (_dot(o2[0], seg) + _dot(o2[1], seg)) * (1.0 / HEAD_DIM)
        o_ref[rows, :] = (o * lax.rsqrt(ms + EPS) * nw * gate).astype(o_ref.dtype)


def _hgrn(hg, lower_bound, norm_w, batch, seq):
    rows = HG_ROWS
    hg3 = hg.reshape(batch, seq, 4 * WIDTH)
    out = pl.pallas_call(
        _hgrn_kernel,
        grid=(batch, seq // rows),
        in_specs=[pl.BlockSpec((None, rows, 4 * WIDTH), lambda b, i: (b, i, 0)),
                  _const_spec((1, WIDTH)), _const_spec((1, WIDTH))],
        out_specs=pl.BlockSpec((None, rows, WIDTH), lambda b, i: (b, i, 0)),
        out_shape=jax.ShapeDtypeStruct((batch, seq, WIDTH), BF16),
        scratch_shapes=[pltpu.VMEM((WIDTH, WIDTH), F32)],
        compiler_params=_cparams(("parallel", "arbitrary")),
        name="hgrn2",
    )(hg3, lower_bound.reshape(1, WIDTH).astype(F32), norm_w.reshape(1, WIDTH).astype(F32))
    return out.reshape(batch * seq, WIDTH)


def _merge_body(x, y_refs, wg_ref, wb_ref, wo_ref, g, b):
    xb = x.astype(BF16)
    merged = jnp.zeros(x.shape, F32)
    for nbr, y_ref in enumerate(y_refs):
        gate = jax.nn.sigmoid(_dot(xb, wg_ref[:, nbr * D_MODEL:(nbr + 1) * D_MODEL]))
        merged = merged + gate * _dot(y_ref[...], wb_ref[nbr])
    y = _dot(merged.astype(BF16), wo_ref[...])
    return _layernorm(ALPHA * x + y, g, b)


def _memkv_kernel(m_ref, w_ref, k_ref, v_ref):
    kv = _dot(m_ref[...].astype(BF16), w_ref[...])
    k_ref[...] = kv[:, :WIDTH].astype(BF16)
    v_ref[...] = kv[:, WIDTH:].astype(BF16)


def _memkv(mem, wkv):
    batch, m, _ = mem.shape
    return pl.pallas_call(
        _memkv_kernel,
        grid=(batch,),
        in_specs=[pl.BlockSpec((None, m, D_MODEL), lambda b: (b, 0, 0)), _const_spec(wkv.shape)],
        out_specs=[pl.BlockSpec((None, m, WIDTH), lambda b: (b, 0, 0))] * 2,
        out_shape=[jax.ShapeDtypeStruct((batch, m, WIDTH), BF16)] * 2,
        compiler_params=_cparams(("parallel",)),
        name="mem_kv",
    )(mem, wkv)


def _xattn_body(x, wq_ref, k, v, wo_ref, g, b):
    q = _dot(x.astype(BF16), wq_ref[...]).astype(BF16)
    lane = lax.broadcasted_iota(I32, (1, WIDTH), 1) // HEAD_DIM
    heads = range(N_HEADS)
    ss = [_dot_nt(jnp.where(lane == hh, q, jnp.zeros_like(q)), k) for hh in heads]
    es = [jnp.exp(s - jnp.max(s, axis=-1, keepdims=True)) for s in ss]
    ps = [(e / jnp.sum(e, axis=-1, keepdims=True)).astype(BF16) for e in es]
    o = jnp.zeros((x.shape[0], WIDTH), F32)
    for hh in heads:
        o = o + jnp.where(lane == hh, _dot(ps[hh], v), 0.0)
    y = _dot(o.astype(BF16), wo_ref[...])
    return _layernorm(ALPHA * x + y, g, b)


def _merge_xattn_kernel(x_ref, y0_ref, y1_ref, y2_ref, y3_ref, wg_ref, wb_ref, wo_ref, g1_ref, b1_ref,
                        wq_ref, k_ref, v_ref, xwo_ref, g2_ref, b2_ref, o_ref):
    x1 = _merge_body(x_ref[...], (y0_ref, y1_ref, y2_ref, y3_ref), wg_ref, wb_ref, wo_ref,
                     g1_ref[...], b1_ref[...])
    o_ref[...] = _xattn_body(x1, wq_ref, k_ref[...], v_ref[...], xwo_ref, g2_ref[...], b2_ref[...])


def _merge_xattn(x2d, ys, wg, wb, wo, g1, b1, wq, k, v, xwo, g2, b2, batch, seq):
    tm = TM_A
    m = k.shape[1]
    per = seq // tm
    row = lambda w: pl.BlockSpec((tm, w), lambda bb, i: (bb * per + i, 0))
    kv_spec = pl.BlockSpec((None, m, WIDTH), lambda bb, i: (bb, 0, 0))
    vec = _const_spec((1, D_MODEL))
    return pl.pallas_call(
        _merge_xattn_kernel,
        grid=(batch, per),
        in_specs=[row(D_MODEL)] + [row(WIDTH)] * 4 +
                 [_const_spec(wg.shape), _const_spec(wb.shape), _const_spec(wo.shape), vec, vec,
                  _const_spec(wq.shape), kv_spec, kv_spec, _const_spec(xwo.shape), vec, vec],
        out_specs=row(D_MODEL),
        out_shape=jax.ShapeDtypeStruct((batch * seq, D_MODEL), F32),
        compiler_params=_cparams(("parallel", "parallel")),
        name="merge_xattn_ln",
    )(x2d, *ys, wg, wb, wo, g1.reshape(1, -1), b1.reshape(1, -1),
      wq, k, v, xwo, g2.reshape(1, -1), b2.reshape(1, -1))


def _ffn_kernel(x_ref, w13_ref, w2_ref, g_ref, b_ref, o_ref):
    x = x_ref[...]
    xb = x.astype(BF16)
    y = None
    for h in range(F_DENSE // TF_FFN):
        lo = h * TF_FFN
        a = _dot(xb, w13_ref[:, lo:lo + TF_FFN])
        gate = _dot(xb, w13_ref[:, F_DENSE + lo:F_DENSE + lo + TF_FFN])
        part = _dot((a * jax.nn.sigmoid(a) * gate).astype(BF16), w2_ref[lo:lo + TF_FFN, :])
        y = part if y is None else y + part
    o_ref[...] = _layernorm(ALPHA * x + y, g_ref[...], b_ref[...])


def _ffn(x2d, w13, w2, g, b):
    n = x2d.shape[0]
    tm = TM_FFN
    return pl.pallas_call(
        _ffn_kernel,
        grid=(n // tm,),
        in_specs=[pl.BlockSpec((tm, D_MODEL), lambda i: (i, 0)),
                  _const_spec(w13.shape), _const_spec(w2.shape),
                  _const_spec((1, D_MODEL)), _const_spec((1, D_MODEL))],
        out_specs=pl.BlockSpec((tm, D_MODEL), lambda i: (i, 0)),
        out_shape=jax.ShapeDtypeStruct((n, D_MODEL), F32),
        compiler_params=_cparams(("parallel",)),
        name="ffn_ln",
    )(x2d, w13, w2, g.reshape(1, -1), b.reshape(1, -1))


def _router_kernel(x_ref, r_ref, info_ref, wts_ref, cnt_ref, carry_ref):
    tm = x_ref.shape[0]

    @pl.when(pl.program_id(0) == 0)
    def _():
        carry_ref[...] = jnp.zeros_like(carry_ref)

    logits = jnp.dot(x_ref[...], r_ref[...], precision=lax.Precision.HIGHEST, preferred_element_type=F32)
    lane = lax.broadcasted_iota(I32, (tm, LANES), 1)
    lg = jnp.where(lane < N_EXPERTS, logits, -jnp.inf)
    m1 = jnp.max(lg, axis=-1, keepdims=True)
    i1 = jnp.min(jnp.where(lg == m1, lane, LANES), axis=-1, keepdims=True)
    lg2 = jnp.where(lane == i1, -jnp.inf, lg)
    m2 = jnp.max(lg2, axis=-1, keepdims=True)
    i2 = jnp.min(jnp.where(lg2 == m2, lane, LANES), axis=-1, keepdims=True)
    e = jnp.exp(m2 - m1)
    w1 = 1.0 / (1.0 + e)
    w2 = e / (1.0 + e)
    sel1 = lane == i1
    sel2 = lane == i2
    chosen = jnp.where(sel1 | sel2, 1.0, 0.0)
    row = lax.broadcasted_iota(I32, (tm, tm), 0)
    col = lax.broadcasted_iota(I32, (tm, tm), 1)
    before = (col < row).astype(BF16)
    ranks = _dot(before, chosen.astype(BF16)) + carry_ref[...]
    r1 = jnp.sum(jnp.where(sel1, ranks, 0.0), axis=-1, keepdims=True)
    r2 = jnp.sum(jnp.where(sel2, ranks, 0.0), axis=-1, keepdims=True)
    carry_ref[...] = carry_ref[...] + jnp.sum(chosen, axis=0, keepdims=True)
    info = jnp.where(lane == 0, i1.astype(F32), jnp.where(lane == 1, i2.astype(F32),
                     jnp.where(lane == 2, r1, jnp.where(lane == 3, r2, 0.0))))
    info_ref[...] = jnp.transpose(info)[:info_ref.shape[0], :]
    wts_ref[...] = jnp.where(lane == 0, w1, jnp.where(lane == 1, w2, 0.0))
    cnt_ref[...] = carry_ref[...]


def _router(x2d, router):
    n = x2d.shape[0]
    tm = TM_ROUTER
    r_pad = jnp.zeros((D_MODEL, LANES), F32).at[:, :N_EXPERTS].set(router.astype(F32))
    row = pl.BlockSpec((tm, LANES), lambda i: (i, 0))
    return pl.pallas_call(
        _router_kernel,
        grid=(n // tm,),
        in_specs=[pl.BlockSpec((tm, D_MODEL), lambda i: (i, 0)), _const_spec(r_pad.shape)],
        out_specs=[pl.BlockSpec((8, tm), lambda i: (0, i)), row, pl.BlockSpec((1, LANES), lambda i: (0, 0))],
        out_shape=[jax.ShapeDtypeStruct((8, n), F32), jax.ShapeDtypeStruct((n, LANES), F32),
                   jax.ShapeDtypeStruct((1, LANES), F32)],
        scratch_shapes=[pltpu.VMEM((1, LANES), F32)],
        compiler_params=_cparams(("arbitrary",)),
        name="moe_router",
    )(x2d, r_pad)


def _dispatch_kernel(pad_ref, d0_ref, d1_ref, x_ref, xb_hbm, stage_ref, sems):
    tm = x_ref.shape[0]
    i = pl.program_id(0)
    last = pl.num_programs(0) - 1
    slot = i % 2

    def wait_step(s):
        for _ in range(2):
            pltpu.make_async_copy(stage_ref.at[s], xb_hbm.at[pl.ds(0, tm), :], sems.at[s]).wait()

    @pl.when(i >= 2)
    def _():
        wait_step(slot)

    for s in range(2):
        @pl.when(slot == s)
        def _():
            stage_ref[s] = x_ref[...]

            def issue(r, c):
                for k in range(2):
                    pltpu.make_async_copy(stage_ref.at[s, pl.ds(r, 1), :],
                                          xb_hbm.at[pl.ds((d0_ref, d1_ref)[k][0, r], 1), :], sems.at[s]).start()
                return c
            lax.fori_loop(0, tm, issue, 0, unroll=8)

    @pl.when(i == last)
    def _():
        def fill(e, c):
            def one(s, c2):
                pltpu.make_async_copy(stage_ref.at[slot, pl.ds(0, 1), :], xb_hbm.at[pl.ds(s, 1), :],
                                      sems.at[2]).start()
                return c2

            def done(s, c2):
                pltpu.make_async_copy(stage_ref.at[slot, pl.ds(0, 1), :], xb_hbm.at[pl.ds(0, 1), :],
                                      sems.at[2]).wait()
                return c2
            lax.fori_loop(pad_ref[0, e], pad_ref[1, e], one, 0)
            lax.fori_loop(pad_ref[0, e], pad_ref[1, e], done, 0)
            return c
        lax.fori_loop(0, pad_ref.shape[1], fill, 0)
        wait_step(slot)

        @pl.when(last >= 1)
        def _():
            wait_step(1 - slot)


def _dispatch(x2d, dest, pads, nblk):
    n = x2d.shape[0]
    tm = TM_DISP
    nt = n // tm
    grid_spec = pltpu.PrefetchScalarGridSpec(
        num_scalar_prefetch=1,
        grid=(nt,),
        in_specs=[pl.BlockSpec((None, 1, tm), lambda i, pads: (i, 0, 0), memory_space=pltpu.SMEM),
                  pl.BlockSpec((None, 1, tm), lambda i, pads: (i, 0, 0), memory_space=pltpu.SMEM),
                  pl.BlockSpec((tm, D_MODEL), lambda i, pads: (i, 0))],
        out_specs=pl.BlockSpec(memory_space=pl.ANY),
        scratch_shapes=[pltpu.VMEM((2, tm, D_MODEL), F32), pltpu.SemaphoreType.DMA((3,))],
    )
    return pl.pallas_call(
        _dispatch_kernel,
        grid_spec=grid_spec,
        out_shape=jax.ShapeDtypeStruct((nblk * MOE_TB, D_MODEL), F32),
        compiler_params=_cparams(("arbitrary",), disable_bounds_checks=True),
        name="moe_dispatch",
    )(pads, dest[0].reshape(nt, 1, tm), dest[1].reshape(nt, 1, tm), x2d)


def _expert_kernel(nused_ref, bexp_ref, x_ref, w1_ref, w3_ref, w2_ref, o_ref, acc_ref):
    f = pl.program_id(1)

    @pl.when(pl.program_id(0) < nused_ref[0])
    def _():
        xb = x_ref[...].astype(BF16)
        a = _dot(xb, w1_ref[...])
        gate = _dot(xb, w3_ref[...])
        part = _dot((a * jax.nn.sigmoid(a) * gate).astype(BF16), w2_ref[...])

        @pl.when(f == 0)
        def _():
            acc_ref[...] = part

        @pl.when(f > 0)
        def _():
            acc_ref[...] += part

        @pl.when(f == pl.num_programs(1) - 1)
        def _():
            o_ref[...] = acc_ref[...]

    @pl.when(pl.program_id(0) >= nused_ref[0])
    def _():
        o_ref[...] = jnp.zeros_like(o_ref)


def _experts(xb, w13, w2, nused, blk_exp, nblk):
    tb, tf = MOE_TB, MOE_TF
    nf = F_EXPERT // tf
    w13 = w13.astype(BF16)

    def blk(i, nu):
        return jnp.maximum(jnp.minimum(i, nu[0] - 1), 0)

    def ftile(i, f, nu):
        return jnp.where(i < nu[0], f, nf - 1)

    grid_spec = pltpu.PrefetchScalarGridSpec(
        num_scalar_prefetch=2,
        grid=(nblk, nf),
        in_specs=[pl.BlockSpec((tb, D_MODEL), lambda i, f, nu, be: (blk(i, nu), 0)),
                  pl.BlockSpec((None, D_MODEL, tf), lambda i, f, nu, be: (be[blk(i, nu)], 0, ftile(i, f, nu))),
                  pl.BlockSpec((None, D_MODEL, tf), lambda i, f, nu, be: (be[blk(i, nu)], 0, nf + ftile(i, f, nu))),
                  pl.BlockSpec((None, tf, D_MODEL), lambda i, f, nu, be: (be[blk(i, nu)], ftile(i, f, nu), 0))],
        out_specs=pl.BlockSpec((tb, D_MODEL), lambda i, f, nu, be: (i, 0)),
        scratch_shapes=[pltpu.VMEM((tb, D_MODEL), F32)],
    )
    return pl.pallas_call(
        _expert_kernel,
        grid_spec=grid_spec,
        out_shape=jax.ShapeDtypeStruct((nblk * tb, D_MODEL), F32),
        compiler_params=_cparams(("arbitrary", "arbitrary")),
        name="moe_experts",
    )(nused, blk_exp, xb, w13, w13, w2.astype(BF16))


def _combine_kernel(d0_ref, d1_ref, n0_ref, n1_ref, y_hbm, x_ref, wts_ref, g_ref, b_ref, o_ref, buf_ref, sems):
    tm = x_ref.shape[0]
    i = pl.program_id(0)
    slot = i % 2

    def gather(idx_refs, s):
        def issue(r, c):
            for k in range(2):
                pltpu.make_async_copy(y_hbm.at[pl.ds(idx_refs[k][0, r], 1), :],
                                      buf_ref.at[s, k, pl.ds(r, 1), :], sems.at[s]).start()
            return c
        lax.fori_loop(0, tm, issue, 0, unroll=8)

    @pl.when(i == 0)
    def _():
        gather((d0_ref, d1_ref), 0)

    for s in range(2):
        @pl.when((i + 1 < pl.num_programs(0)) & (slot != s))
        def _():
            gather((n0_ref, n1_ref), s)

    for k in range(2):
        pltpu.make_async_copy(y_hbm.at[pl.ds(0, tm), :], buf_ref.at[slot, k], sems.at[slot]).wait()
    wts = wts_ref[...]
    y = wts[:, 0:1] * buf_ref[slot, 0] + wts[:, 1:2] * buf_ref[slot, 1]
    o_ref[...] = _layernorm(ALPHA * x_ref[...] + y, g_ref[...], b_ref[...])


def _combine(yb, dest, x2d, wts, g, b):
    n = x2d.shape[0]
    tm = TM_COMB
    nt = n // tm
    row = lambda w: pl.BlockSpec((tm, w), lambda i: (i, 0))
    cur = pl.BlockSpec((None, 1, tm), lambda i: (i, 0, 0), memory_space=pltpu.SMEM)
    nxt = pl.BlockSpec((None, 1, tm), lambda i: (jnp.minimum(i + 1, nt - 1), 0, 0), memory_space=pltpu.SMEM)
    d0, d1 = (d.reshape(nt, 1, tm) for d in dest)
    return pl.pallas_call(
        _combine_kernel,
        grid=(nt,),
        in_specs=[cur, cur, nxt, nxt, pl.BlockSpec(memory_space=pl.ANY), row(D_MODEL), row(LANES),
                  _const_spec((1, D_MODEL)), _const_spec((1, D_MODEL))],
        out_specs=row(D_MODEL),
        out_shape=jax.ShapeDtypeStruct((n, D_MODEL), F32),
        scratch_shapes=[pltpu.VMEM((2, 2, tm, D_MODEL), F32), pltpu.SemaphoreType.DMA((2,))],
        compiler_params=_cparams(("arbitrary",), disable_bounds_checks=True),
        name="moe_combine_ln",
    )(d0, d1, d0, d1, yb, x2d, wts, g.reshape(1, -1), b.reshape(1, -1))


def _moe(x2d, router, w13, w2, g, b):
    n = x2d.shape[0]
    tb = MOE_TB
    info, wts, cnt = _router(x2d, router)
    counts = cnt[0, :N_EXPERTS].astype(I32)
    padded = (counts + tb - 1) // tb * tb
    pend = jnp.cumsum(padded)
    pstart = pend - padded
    info = info.astype(I32)
    dest = tuple((sum(jnp.where(info[k] == e, pstart[e], 0) for e in range(N_EXPERTS)) + info[2 + k]).astype(I32)
                 for k in range(2))
    nblk = -(-(2 * n + N_EXPERTS * (tb - 1)) // tb)
    pads = jnp.stack([jnp.append(pstart + counts, pend[-1]), jnp.append(pend, nblk * tb)]).astype(I32)
    nused = (pend[-1] // tb).astype(I32).reshape(1)
    first_row = jnp.arange(nblk, dtype=I32) * tb
    blk_exp = jnp.minimum(jnp.sum(pend[None, :] <= first_row[:, None], axis=1), N_EXPERTS - 1).astype(I32)
    xb = _dispatch(x2d, dest, pads, nblk)
    yb = _experts(xb, w13, w2, nused, blk_exp, nblk)
    return _combine(yb, dest, x2d, wts, g, b)


def kernel(x, mem, positions, rel_bias_table, hgrn_lb_logits, w_in, mla_q_norm, mla_w_uq, mla_kv_norm, mla_w_ukv, swa_sinks, hgrn_norm, w_branch, w_out, ln_g, ln_b, xa_wq, xa_wkv, xa_wo, ffn_w13, ffn_w2, moe_router, moe_w13, moe_w2):
    batch, seq, _ = x.shape
    n = batch * seq
    sm = jax.nn.softmax(hgrn_lb_logits.astype(F32), axis=0)
    lower_bounds = jnp.cumsum(sm, axis=0) - sm[0]
    ctab, stab = _rope_tables(positions)
    xc = x.reshape(n, D_MODEL)
    for l in range(DEPTH):
        wts = _inproj_weights(w_in[l], mla_w_uq[l], mla_w_ukv[l])
        mq, mk, mv, swq, swk, swv, hg, sbq, sbk, sbv = _inproj(xc, wts, ctab, stab, mla_q_norm[l], mla_kv_norm[l])
        y_mla = _mla_attention(mq, mk, mv, batch, seq)
        y_swa = _swa_attention(swq, swk, swv, positions, swa_sinks[l], rel_bias_table, batch, seq)
        y_hg = _hgrn(hg, lower_bounds[l], hgrn_norm[l], batch, seq)
        y_sb = _sb_attention(sbq, sbk, sbv, batch, seq)
        go = _IN_OFF['gates']
        mk_, mv_ = _memkv(mem, xa_wkv[l].astype(BF16))
        wb = w_branch[l].at[1].set(
            w_branch[l][1].reshape(N_HEADS, HEAD_DIM, D_MODEL)[jnp.array([0, 2, 1, 3])].reshape(WIDTH, D_MODEL))
        xc = _merge_xattn(xc, (y_mla, y_swa, y_hg, y_sb), w_in[l][:, go:].astype(BF16), wb.astype(BF16),
                          w_out[l].astype(BF16), ln_g[l, 0], ln_b[l, 0],
                          (xa_wq[l] * QK_SCALE).astype(BF16), mk_, mv_, xa_wo[l].astype(BF16),
                          ln_g[l, 1], ln_b[l, 1], batch, seq)
        if l % 2 == 0:
            xc = _ffn(xc, ffn_w13[l // 2].astype(BF16), ffn_w2[l // 2].astype(BF16), ln_g[l, 2], ln_b[l, 2])
        else:
            xc = _moe(xc, moe_router[l // 2], moe_w13[l // 2], moe_w2[l // 2],
                      ln_g[l, 2], ln_b[l, 2])
    return xc.reshape(batch, seq, D_MODEL)
```

```python
import functools
import math

import jax
import jax.numpy as jnp
from jax import lax
from jax.experimental import pallas as pl
from jax.experimental.pallas import tpu as pltpu

F32 = jnp.float32
BF16 = jnp.bfloat16
I32 = jnp.int32

D_MODEL = 1024
DEPTH = 2
EPS = 1e-5
NEG_BIG = -1e30
LANES = 128
HEAD_DIM = 64
N_HEADS = 4
WIDTH = N_HEADS * HEAD_DIM

MLA_Q_LORA = 256
MLA_KV_LORA = 128
MLA_NOPE = 64
MLA_ROPE = 32
ROPE_THETA = 10000.0
MLA_SCALE = (MLA_NOPE + MLA_ROPE) ** -0.5
LOG2E = math.log2(math.e)
QK_SCALE = HEAD_DIM ** -0.5

SB_RUN_FLOOR = -150.0
SWA_WINDOW = 128
REL_BUCKETS = 32
REL_MAX_DIST = 128
HGRN_CHUNK = 64
HGRN_BLOCK = 16
N_EXPERTS = 8
F_DENSE = 2816
F_EXPERT = 3584
ALPHA = (2 * DEPTH) ** 0.25

_IN_SPLITS = (('mla_cq', 256), ('mla_ckv', 128), ('mla_kr', 32), ('swa_q', 256), ('swa_k', 128),
              ('swa_v', 128), ('hgrn', 1024), ('sb_q', 256), ('sb_k', 256), ('sb_v', 256), ('gates', 4096))
_IN_OFF = {}
_o = 0
for _n, _w in _IN_SPLITS:
    _IN_OFF[_n] = _o
    _o += _w

_A_SPLITS = (('cq', 256), ('ckv', 128), ('kra', 128), ('krb', 128), ('swa_q', 256), ('swa_k', 128),
             ('swa_v', 128), ('hgrn', 1024), ('sb_q', 256), ('sb_k', 256), ('sb_v', 256))
_A_OFF = {}
_o = 0
for _n, _w in _A_SPLITS:
    _A_OFF[_n] = (_o, _o + _w)
    _o += _w
A_COLS = _o

TM_INPROJ = 1024
TM_A = 512
TQ_ATT = 256
MLA_TQ = 512
MLA_TK = 512
MLA_WIDE = 4
MLA_GROUP = 4
SWA_TQ = 1024
HG_ROWS = 512
TM_FFN = 1024
TF_FFN = 256
MOE_TB = 512
MOE_TF = 1792
TM_ROUTER = 512
TM_COMB = 512
TM_DISP = 1024
VMEM_LIMIT = 56 * 1024 * 1024
assert MLA_WIDE == 4 and MLA_TQ % MLA_TK == 0


def _cparams(sem, **kw):
    return pltpu.CompilerParams(dimension_semantics=sem, vmem_limit_bytes=VMEM_LIMIT, **kw)


def _const_spec(shape):
    nd = len(shape)
    return pl.BlockSpec(shape, lambda *_: (0,) * nd, pipeline_mode=pl.Buffered(1))


def _layernorm(v, g, b):
    mu = jnp.mean(v, axis=-1, keepdims=True)
    vc = v - mu
    var = jnp.mean(vc * vc, axis=-1, keepdims=True)
    return vc * lax.rsqrt(var + EPS) * g + b


def _dot(a, b):
    return jnp.dot(a, b, preferred_element_type=F32)


def _dot_nt(a, b):
    return lax.dot_general(a, b, (((1,), (1,)), ((), ())), preferred_element_type=F32)


def _split3(a):
    hi = a.astype(BF16)
    r = a - hi.astype(F32)
    mid = r.astype(BF16)
    lo = (r - mid.astype(F32)).astype(BF16)
    return hi, mid, lo


def _rope_kernel(pos_ref, freq_ref, c_ref, s_ref):
    lane = lax.broadcasted_iota(I32, pos_ref.shape, 1)
    ang = pos_ref[...] * freq_ref[...]
    rope = (lane >= MLA_NOPE) & (lane < MLA_NOPE + MLA_ROPE)
    first = lane < MLA_NOPE + MLA_ROPE // 2
    c_ref[...] = jnp.where(lane < MLA_NOPE, 1.0, jnp.where(rope, jnp.cos(ang), 0.0))
    sn = jnp.sin(ang)
    s_ref[...] = jnp.where(rope, jnp.where(first, -sn, sn), 0.0)


def _rope_tables(positions):
    n = positions.size
    half = MLA_ROPE // 2
    inv_freq = ROPE_THETA ** (-jnp.arange(half, dtype=F32) / half)
    freq = jnp.zeros((1, LANES), F32).at[0, MLA_NOPE:MLA_NOPE + MLA_ROPE].set(jnp.tile(inv_freq, 2))
    posb = jnp.broadcast_to(positions.reshape(n, 1).astype(F32), (n, LANES))
    tm = 1024
    return pl.pallas_call(
        _rope_kernel,
        grid=(n // tm,),
        in_specs=[pl.BlockSpec((tm, LANES), lambda i: (i, 0)), _const_spec((1, LANES))],
        out_specs=[pl.BlockSpec((tm, LANES), lambda i: (i, 0))] * 2,
        out_shape=[jax.ShapeDtypeStruct((n, LANES), F32)] * 2,
        compiler_params=_cparams(("parallel",)),
        name="rope_tables",
    )(posb, freq)


def _inproj_kernel(x_ref, w_ref, c_ref, s_ref, qn_ref, kvn_ref, wuqa_ref, wuqb_ref, wuk_ref, wuv_ref,
                   mq_ref, mk_ref, mv_ref, swq_ref, swk_ref, swv_ref, hg_ref, sbq_ref, sbk_ref, sbv_ref):
    h = _dot(x_ref[...].astype(BF16), w_ref[...])

    def cols(name):
        lo, hi = _A_OFF[name]
        return h[:, lo:hi]

    c = c_ref[...]
    s = s_ref[...]
    c4 = jnp.concatenate([c] * N_HEADS, axis=1)
    s4 = jnp.concatenate([s] * N_HEADS, axis=1)

    cq = cols('cq')
    cqn = (cq * lax.rsqrt(jnp.mean(cq * cq, axis=-1, keepdims=True) + EPS) * qn_ref[...]).astype(BF16)
    q = _dot(cqn, wuqa_ref[...]) * c4 + _dot(cqn, wuqb_ref[...]) * s4
    mq_ref[...] = (q * (MLA_SCALE * LOG2E)).astype(BF16)

    ckv = cols('ckv')
    ckvn = (ckv * lax.rsqrt(jnp.mean(ckv * ckv, axis=-1, keepdims=True) + EPS) * kvn_ref[...]).astype(BF16)
    krot = cols('kra') * c + cols('krb') * s
    mk_ref[...] = (_dot(ckvn, wuk_ref[...]) + jnp.concatenate([krot] * N_HEADS, axis=1)).astype(BF16)
    mv_ref[...] = _dot(ckvn, wuv_ref[...]).astype(BF16)

    swq_ref[...] = cols('swa_q').astype(BF16)
    swk_ref[...] = cols('swa_k').astype(BF16)
    swv_ref[...] = cols('swa_v').astype(BF16)
    hg_ref[...] = cols('hgrn')
    sbq_ref[...] = cols('sb_q').astype(BF16)
    sbk_ref[...] = cols('sb_k').astype(BF16)
    sbv_ref[...] = cols('sb_v').astype(BF16)


def _inproj_weights(w_in, w_uq, w_ukv):
    def seg(name, width):
        o = _IN_OFF[name]
        return w_in[:, o:o + width]

    kr = seg('mla_kr', MLA_ROPE)
    half = MLA_ROPE // 2
    z64 = jnp.zeros((D_MODEL, MLA_NOPE), F32)
    z32 = jnp.zeros((D_MODEL, LANES - MLA_NOPE - MLA_ROPE), F32)
    kra = jnp.concatenate([z64, kr, z32], axis=1)
    krb = jnp.concatenate([z64, kr[:, half:], kr[:, :half], z32], axis=1)
    swq = seg('swa_q', 256).reshape(D_MODEL, N_HEADS, HEAD_DIM)[:, jnp.array([0, 2, 1, 3])].reshape(D_MODEL, WIDTH)
    w_a = jnp.concatenate([
        seg('mla_cq', 256), seg('mla_ckv', 128), kra, krb,
        swq * QK_SCALE, seg('swa_k', 128), seg('swa_v', 128),
        seg('hgrn', 1024), seg('sb_q', 256) * (QK_SCALE * LOG2E), seg('sb_k', 256), seg('sb_v', 256)], axis=1)

    qd = MLA_NOPE + MLA_ROPE
    zq = jnp.zeros((MLA_Q_LORA, LANES - qd), F32)
    zn = jnp.zeros((MLA_Q_LORA, MLA_NOPE), F32)
    qa, qb = [], []
    for hh in range(N_HEADS):
        nope = w_uq[:, hh * qd: hh * qd + MLA_NOPE]
        rope = w_uq[:, hh * qd + MLA_NOPE: (hh + 1) * qd]
        qa += [nope, rope, zq]
        qb += [zn, rope[:, half:], rope[:, :half], zq]
    wuqa = jnp.concatenate(qa, axis=1)
    wuqb = jnp.concatenate(qb, axis=1)
    lane = jnp.arange(N_HEADS * LANES) % LANES
    wuk = jnp.where(lane[None, :] < MLA_NOPE, w_ukv, 0.0)
    wuv = jnp.concatenate([w_ukv[:, hh * LANES + MLA_NOPE:(hh + 1) * LANES] for hh in range(N_HEADS)], axis=1)
    return tuple(t.astype(BF16) for t in (w_a, wuqa, wuqb, wuk, wuv))


def _inproj(x2d, wts, ctab, stab, q_norm, kv_norm):
    n = x2d.shape[0]
    w_a, wuqa, wuqb, wuk, wuv = wts
    tm = TM_INPROJ
    row = lambda w: pl.BlockSpec((tm, w), lambda i: (i, 0))
    out_w = (512, 512, 256, 256, 128, 128, 1024, 256, 256, 256)
    out_dt = (BF16, BF16, BF16, BF16, BF16, BF16, F32, BF16, BF16, BF16)
    return pl.pallas_call(
        _inproj_kernel,
        grid=(n // tm,),
        in_specs=[row(D_MODEL), _const_spec(w_a.shape), row(LANES), row(LANES),
                  _const_spec((1, MLA_Q_LORA)), _const_spec((1, MLA_KV_LORA)),
                  _const_spec(wuqa.shape), _const_spec(wuqb.shape), _const_spec(wuk.shape),
                  _const_spec(wuv.shape)],
        out_specs=[row(w) for w in out_w],
        out_shape=[jax.ShapeDtypeStruct((n, w), d) for w, d in zip(out_w, out_dt)],
        compiler_params=_cparams(("parallel",)),
        name="inproj",
    )(x2d, w_a, ctab, stab, q_norm.reshape(1, -1), kv_norm.reshape(1, -1), wuqa, wuqb, wuk, wuv)


def _half_mask(half):
    lane = lax.broadcasted_iota(I32, (1, LANES), 1)
    return (lane < HEAD_DIM) if half == 0 else (lane >= HEAD_DIM)


def _mla_kernel(q_ref, k_ref, v_ref, o_ref):
    tq = q_ref.shape[0]
    tk = MLA_TK
    nsub = tq // tk
    i = pl.program_id(1)
    row = lax.broadcasted_iota(I32, (tq, tk), 0)
    col = lax.broadcasted_iota(I32, (tq, tk), 1)
    ones = jnp.ones((1, LANES), BF16)

    def update(off, carry, heads, mask, width=tk):
        ss = [_dot_nt(q_ref[:, hh * LANES:(hh + 1) * LANES],
                      k_ref[pl.ds(off, width), hh * LANES:(hh + 1) * LANES]) for hh in heads]
        if mask is not None:
            ss = [jnp.where(mask, s, NEG_BIG) for s in ss]
        ms = [jnp.maximum(c[0], jnp.max(s, axis=-1, keepdims=True)) for c, s in zip(carry, ss)]
        pms = [jnp.exp2(s - m).astype(BF16) for s, m in zip(ss, ms)]
        new = []
        for n, hh in enumerate(heads):
            vb = v_ref[pl.ds(off, width), (hh // 2) * LANES:(hh // 2 + 1) * LANES]
            vb = jnp.where(_half_mask(hh % 2), vb, ones)
            m, acc = carry[n]
            new.append((ms[n], jnp.exp2(m - ms[n]) * acc + _dot(pms[n], vb)))
        return tuple(new)

    accs = []
    for g in range(0, N_HEADS, MLA_GROUP):
        heads = tuple(range(g, g + MLA_GROUP))
        init = tuple((jnp.full((tq, 1), NEG_BIG, F32), jnp.zeros((tq, LANES), F32)) for _ in heads)
        nkb = i * nsub
        wide = MLA_WIDE * tk
        carry = lax.fori_loop(
            0, nkb // MLA_WIDE,
            lambda j, c, heads=heads: update(pl.multiple_of(j * wide, wide), c, heads, None, wide), init)
        done = nkb // MLA_WIDE * MLA_WIDE
        rest = nkb - done
        carry = lax.cond(
            rest >= 2,
            lambda c, heads=heads: update(pl.multiple_of(done * tk, 2 * tk), c, heads, None, 2 * tk),
            lambda c: c, carry)
        carry = lax.cond(
            rest % 2 == 1,
            lambda c, heads=heads: update(pl.multiple_of((nkb - 1) * tk, tk), c, heads, None),
            lambda c: c, carry)
        for r in range(nsub):
            carry = update(pl.multiple_of(i * tq + r * tk, tk), carry, heads, col + r * tk <= row)
        accs += [c[1] for c in carry]
    outs = []
    for p in range(N_HEADS // 2):
        a0, a1 = accs[2 * p], accs[2 * p + 1]
        outs.append(jnp.where(_half_mask(0), a0 / a0[:, HEAD_DIM:HEAD_DIM + 1], a1 / a1[:, 0:1]))
    o_ref[...] = jnp.concatenate(outs, axis=1).astype(o_ref.dtype)


def _mla_attention(q, k, v, batch, seq):
    tq = MLA_TQ
    q3, k3, v3 = (t.reshape(batch, seq, t.shape[-1]) for t in (q, k, v))
    out = pl.pallas_call(
        _mla_kernel,
        grid=(batch, seq // tq),
        in_specs=[pl.BlockSpec((None, tq, 512), lambda b, i: (b, i, 0)),
                  pl.BlockSpec((None, seq, 512), lambda b, i: (b, 0, 0), pipeline_mode=pl.Buffered(1)),
                  pl.BlockSpec((None, seq, WIDTH), lambda b, i: (b, 0, 0), pipeline_mode=pl.Buffered(1))],
        out_specs=pl.BlockSpec((None, tq, WIDTH), lambda b, i: (b, i, 0)),
        out_shape=jax.ShapeDtypeStruct((batch, seq, WIDTH), BF16),
        compiler_params=_cparams(("parallel", "arbitrary")),
        name="mla_attention",
    )(q3, k3, v3)
    return out.reshape(batch * seq, WIDTH)


def _sb_kernel(q_ref, k_ref, v_ref, o_ref):
    tq = q_ref.shape[0]
    i = pl.program_id(1)
    row = lax.broadcasted_iota(I32, (tq, tq), 0)
    col = lax.broadcasted_iota(I32, (tq, tq), 1)
    strict = col < row
    later = (row > col).astype(BF16)
    qs = []
    for hh in range(N_HEADS):
        qp = q_ref[:, (hh // 2) * LANES:(hh // 2 + 1) * LANES]
        qs.append(jnp.where(_half_mask(hh % 2), qp, jnp.zeros_like(qp)))

    def block(j, carry, diag):
        off = pl.multiple_of(j * tq, tq)
        runs, accs = carry
        heads = range(N_HEADS)
        zs = [_dot_nt(qs[hh], k_ref[pl.ds(off, tq), (hh // 2) * LANES:(hh // 2 + 1) * LANES]) for hh in heads]
        lsps = [jnp.minimum(z, 0.0) - jnp.log2(1.0 + jnp.exp2(-jnp.abs(z))) for z in zs]
        lsns = [lsp - z for lsp, z in zip(lsps, zs)]
        if diag:
            lsns = [jnp.where(strict, t, 0.0) for t in lsns]
        his = [t.astype(BF16) for t in lsns]
        los = [(t - hi.astype(F32)).astype(BF16) for t, hi in zip(lsns, his)]
        rems = [_dot(hi, later) + _dot(lo, later) for hi, lo in zip(his, los)]
        args = [lsps[hh] + rems[hh] + runs[hh] for hh in heads]
        if diag:
            args = [jnp.where(strict, t, NEG_BIG) for t in args]
        probs = [jnp.exp2(t).astype(BF16) for t in args]
        new_runs = tuple(runs[hh] + rems[hh][:, 0:1] + lsns[hh][:, 0:1] for hh in heads)
        new_accs = list(accs)
        for hh in heads:
            p = hh // 2
            vb = v_ref[pl.ds(off, tq), p * LANES:(p + 1) * LANES]
            vb = jnp.where(_half_mask(hh % 2), vb, jnp.zeros_like(vb))
            new_accs[p] = new_accs[p] + _dot(probs[hh], vb)
        return new_runs, tuple(new_accs)

    init = (tuple(jnp.zeros((tq, 1), F32) for _ in range(N_HEADS)),
            tuple(jnp.zeros((tq, LANES), F32) for _ in range(N_HEADS // 2)))
    def still_active(runs):
        top = functools.reduce(jnp.maximum, runs)
        return (jnp.max(top) > SB_RUN_FLOOR).astype(I32)

    runs, accs = block(i, init, True)

    def cond(c):
        return (c[0] < i) & (c[1] > 0)

    def body(c):
        jj, _, runs, accs = c
        runs, accs = block(i - 1 - jj, (runs, accs), False)
        return jj + 1, still_active(runs), runs, accs

    _, _, _, accs = lax.while_loop(cond, body, (jnp.int32(0), still_active(runs), runs, accs))
    o_ref[...] = jnp.concatenate(accs, axis=1).astype(o_ref.dtype)


def _sb_attention(q, k, v, batch, seq):
    tq = TQ_ATT
    q3, k3, v3 = (t.reshape(batch, seq, WIDTH) for t in (q, k, v))
    out = pl.pallas_call(
        _sb_kernel,
        grid=(batch, seq // tq),
        in_specs=[pl.BlockSpec((None, tq, WIDTH), lambda b, i: (b, i, 0)),
                  pl.BlockSpec((None, seq, WIDTH), lambda b, i: (b, 0, 0)),
                  pl.BlockSpec((None, seq, WIDTH), lambda b, i: (b, 0, 0))],
        out_specs=pl.BlockSpec((None, tq, WIDTH), lambda b, i: (b, i, 0)),
        out_shape=jax.ShapeDtypeStruct((batch, seq, WIDTH), BF16),
        compiler_params=_cparams(("parallel", "arbitrary")),
        name="stick_breaking",
    )(q3, k3, v3)
    return out.reshape(batch * seq, WIDTH)


def _rel_bucket(dist):
    exact = REL_BUCKETS // 2
    n = jnp.maximum(dist, 0)
    nf = jnp.maximum(n, 1).astype(F32)
    large = exact + (jnp.log(nf / exact) / math.log(REL_MAX_DIST / exact) * (REL_BUCKETS - exact)).astype(I32)
    large = jnp.clip(large, 0, REL_BUCKETS - 1)
    return jnp.where(n < exact, n, large)


def _swa_kernel(sink_ref, tab_ref, q_ref, kc_ref, kh_ref, vc_ref, vh_ref, pq_ref, pkc_ref, pkh_ref, o_ref):
    w = SWA_WINDOW
    step = pl.program_id(1)
    row = lax.broadcasted_iota(I32, (w, w), 0)
    col = lax.broadcasted_iota(I32, (w, w), 1)
    valid_c = col <= row
    valid_p = col > row
    tabs = [jnp.broadcast_to(tab_ref[hh:hh + 1, :], (w, LANES)) for hh in range(N_HEADS)]
    ones = jnp.ones((1, LANES), BF16)
    nsub = q_ref.shape[0] // w
    chains = [(r, hh) for r in range(nsub) for hh in range(N_HEADS)]

    def real(hh):
        return (hh % 2) * 2 + hh // 2

    def keys(ref, halo_ref, r):
        cur = ref[r * w:(r + 1) * w, :]
        prev = ref[(r - 1) * w:r * w, :] if r else halo_ref[...]
        return cur, prev

    buckets = []
    for r in range(nsub):
        pq = pq_ref[r * w:(r + 1) * w, :]
        pk_prev = pkc_ref[:, (r - 1) * w:r * w] if r else pkh_ref[...]
        buckets.append((_rel_bucket(pq - pkc_ref[:, r * w:(r + 1) * w]), _rel_bucket(pq - pk_prev)))
    logits = []
    for r, hh in chains:
        qp = q_ref[r * w:(r + 1) * w, (hh // 2) * LANES:(hh // 2 + 1) * LANES]
        qh = jnp.where(_half_mask(hh % 2), qp, jnp.zeros_like(qp))
        kc, kp = keys(kc_ref, kh_ref, r)
        logits.append((_dot_nt(qh, kc), _dot_nt(qh, kp)))
    masked = []
    for (r, hh), (lc, lp) in zip(chains, logits):
        lc = jnp.where(valid_c, lc + jnp.take_along_axis(tabs[real(hh)], buckets[r][0], axis=1), NEG_BIG)
        lp = lp + jnp.take_along_axis(tabs[real(hh)], buckets[r][1], axis=1)
        lp = jnp.where(valid_p if r else valid_p & (step > 0), lp, NEG_BIG)
        masked.append((lc, lp))
    maxes = [jnp.maximum(jnp.maximum(jnp.max(lc, axis=-1, keepdims=True), jnp.max(lp, axis=-1, keepdims=True)),
                         sink_ref[real(hh)]) for (r, hh), (lc, lp) in zip(chains, masked)]
    probs = [(jnp.exp(lc - m).astype(BF16), jnp.exp(lp - m).astype(BF16)) for (lc, lp), m in zip(masked, maxes)]
    outs = {}
    for (r, hh), (ec, ep), m in zip(chains, probs, maxes):
        vc, vp = keys(vc_ref, vh_ref, r)
        mine = _half_mask(hh % 2)
        acc = _dot(ec, jnp.where(mine, vc, ones)) + _dot(ep, jnp.where(mine, vp, ones))
        den = (acc[:, 0:1] if hh % 2 else acc[:, HEAD_DIM:HEAD_DIM + 1]) + jnp.exp(sink_ref[real(hh)] - m)
        outs[(r, hh)] = acc / den
    for r in range(nsub):
        pairs = [jnp.where(_half_mask(0), outs[(r, 2 * p)], outs[(r, 2 * p + 1)]) for p in range(N_HEADS // 2)]
        o_ref[r * w:(r + 1) * w, :] = jnp.concatenate(pairs, axis=1).astype(o_ref.dtype)


def _swa_attention(q, k, v, positions, sinks, rel_table, batch, seq):
    w = SWA_WINDOW
    tq = SWA_TQ
    per = tq // w
    kvw = k.shape[-1]
    q3, k3, v3 = (t.reshape(batch, seq, t.shape[-1]) for t in (q, k, v))
    pcol = positions.reshape(batch, seq, 1)
    prow = positions.reshape(batch, 1, seq)
    tab = jnp.zeros((N_HEADS, LANES), F32).at[:, :REL_BUCKETS].set(rel_table.astype(F32).T)
    cur = lambda b, n: (b, n, 0)
    halo = lambda b, n: (b, jnp.maximum(n * per - 1, 0), 0)
    out = pl.pallas_call(
        _swa_kernel,
        grid=(batch, seq // tq),
        in_specs=[pl.BlockSpec(memory_space=pltpu.SMEM), _const_spec((N_HEADS, LANES)),
                  pl.BlockSpec((None, tq, WIDTH), cur),
                  pl.BlockSpec((None, tq, kvw), cur), pl.BlockSpec((None, w, kvw), halo),
                  pl.BlockSpec((None, tq, kvw), cur), pl.BlockSpec((None, w, kvw), halo),
                  pl.BlockSpec((None, tq, 1), cur),
                  pl.BlockSpec((None, 1, tq), lambda b, n: (b, 0, n)),
                  pl.BlockSpec((None, 1, w), lambda b, n: (b, 0, jnp.maximum(n * per - 1, 0)))],
        out_specs=pl.BlockSpec((None, tq, WIDTH), cur),
        out_shape=jax.ShapeDtypeStruct((batch, seq, WIDTH), BF16),
        compiler_params=_cparams(("parallel", "arbitrary")),
        name="swa_attention",
    )(sinks.astype(F32), tab, q3, k3, k3, v3, v3, pcol, prow, prow)
    return out.reshape(batch * seq, WIDTH)


def _hgrn_kernel(hg_ref, lb_ref, nw_ref, o_ref, state_ref):
    c = HGRN_CHUNK
    blk = HGRN_BLOCK

    @pl.when(pl.program_id(1) == 0)
    def _():
        state_ref[...] = jnp.zeros_like(state_ref)

    r64 = lax.broadcasted_iota(I32, (c, c), 0)
    c64 = lax.broadcasted_iota(I32, (c, c), 1)
    incl = (c64 <= r64).astype(BF16)
    ra = lax.broadcasted_iota(I32, (WIDTH, WIDTH), 0) // HEAD_DIM
    ca = lax.broadcasted_iota(I32, (WIDTH, WIDTH), 1) // HEAD_DIM
    same_head = ra == ca
    seg = same_head.astype(BF16)
    ones_cols = jnp.ones((c, LANES), BF16)
    trow = lax.broadcasted_iota(I32, (blk, WIDTH), 0)
    caps = [jnp.where(trow >= s_i, 0.0, NEG_BIG) for s_i in range(blk)]
    lane_head = lax.broadcasted_iota(I32, (1, WIDTH), 1) // HEAD_DIM
    lb = lb_ref[...]
    nw = nw_ref[...]
    dn0 = (((0,), (0,)), ((), ()))

    for ch in range(hg_ref.shape[0] // c):
        rows = slice(ch * c, (ch + 1) * c)
        qraw = hg_ref[rows, 0:WIDTH]
        fraw = hg_ref[rows, WIDTH:2 * WIDTH]
        v = hg_ref[rows, 2 * WIDTH:3 * WIDTH]
        graw = hg_ref[rows, 3 * WIDTH:4 * WIDTH]
        qf = qraw * jax.nn.sigmoid(qraw)
        forget = lb + (1.0 - lb) * jax.nn.sigmoid(fraw)
        lf = jnp.log(forget)
        kk = 1.0 - forget
        gate = graw * jax.nn.sigmoid(graw)
        vb = v.astype(BF16)

        lf3 = _split3(lf)
        bc = _dot(incl, lf3[0]) + _dot(incl, lf3[1]) + _dot(incl, lf3[2])
        b_last = bc[c - 1:c, :]
        tot_col = sum(lax.dot_general(t, ones_cols, dn0, preferred_element_type=F32) for t in lf3)
        decay_col = jnp.exp(jnp.concatenate([tot_col, tot_col], axis=1))

        state = state_ref[...]
        o_inter = _dot((qf * jnp.exp(bc)).astype(BF16), state.astype(BF16))

        def before(qa, qb, ka, kb):
            ref = bc[kb - 1:kb, :]
            qt = qf[qa:qb] * jnp.exp(bc[qa:qb] - ref)
            kt = (kk[ka:kb] * jnp.exp(ref - bc[ka:kb])).astype(BF16)
            qs = jnp.concatenate([jnp.where(lane_head == hh, qt, 0.0) for hh in range(N_HEADS)], axis=0)
            att = _dot_nt(qs.astype(BF16), kt)
            mix = _dot(att.astype(BF16), vb[ka:kb])
            nq = qb - qa
            return sum(jnp.where(lane_head == hh, mix[hh * nq:(hh + 1) * nq], 0.0) for hh in range(N_HEADS))

        bc2 = bc * LOG2E
        key2 = bc2 - jnp.log2(jnp.maximum(kk, 0.0))

        def inside(a):
            b2 = bc2[a:a + blk]
            qb_ = qf[a:a + blk]
            ws = []
            for s_i in range(blk):
                ws.append(qb_ * jnp.exp2(jnp.minimum(b2 - key2[a + s_i:a + s_i + 1, :], caps[s_i])))
            att = _dot(jnp.concatenate(ws, axis=0).astype(BF16), seg)
            return sum(att[s_i * blk:(s_i + 1) * blk] * v[a + s_i:a + s_i + 1, :] for s_i in range(blk))

        pieces = {a: [] for a in range(0, c, blk)}

        def cover(a, b):
            if b - a == blk:
                pieces[a].append(inside(a))
                return
            mid = (a + b) // 2
            cover(a, mid)
            cover(mid, b)
            res = before(mid, b, a, mid)
            for off in range(0, b - mid, blk):
                pieces[mid + off].append(res[off:off + blk])

        cover(0, c)
        o = o_inter + jnp.concatenate([sum(pieces[a]) for a in range(0, c, blk)], axis=0)

        khat = (kk * jnp.exp(b_last - bc)).astype(BF16)
        upd = lax.dot_general(khat, vb, dn0, preferred_element_type=F32)
        state_ref[...] = decay_col * state + jnp.where(same_head, upd, 0.0)

        o2 = _split3(o * o)
        ms = (_dot(o2[0], seg) + _dot(o2[1], seg)) * (1.0 / HEAD_DIM)
        o_ref[rows, :] = (o * lax.rsqrt(ms + EPS) * nw * gate).astype(o_ref.dtype)


def _hgrn(hg, lower_bound, norm_w, batch, seq):
    rows = HG_ROWS
    hg3 = hg.reshape(batch, seq, 4 * WIDTH)
    out = pl.pallas_call(
        _hgrn_kernel,
        grid=(batch, seq // rows),
        in_specs=[pl.BlockSpec((None, rows, 4 * WIDTH), lambda b, i: (b, i, 0)),
                  _const_spec((1, WIDTH)), _const_spec((1, WIDTH))],
        out_specs=pl.BlockSpec((None, rows, WIDTH), lambda b, i: (b, i, 0)),
        out_shape=jax.ShapeDtypeStruct((batch, seq, WIDTH), BF16),
        scratch_shapes=[pltpu.VMEM((WIDTH, WIDTH), F32)],
        compiler_params=_cparams(("parallel", "arbitrary")),
        name="hgrn2",
    )(hg3, lower_bound.reshape(1, WIDTH).astype(F32), norm_w.reshape(1, WIDTH).astype(F32))
    return out.reshape(batch * seq, WIDTH)


def _merge_body(x, y_refs, wg_ref, wb_ref, wo_ref, g, b):
    xb = x.astype(BF16)
    merged = jnp.zeros(x.shape, F32)
    for nbr, y_ref in enumerate(y_refs):
        gate = jax.nn.sigmoid(_dot(xb, wg_ref[:, nbr * D_MODEL:(nbr + 1) * D_MODEL]))
        merged = merged + gate * _dot(y_ref[...], wb_ref[nbr])
    y = _dot(merged.astype(BF16), wo_ref[...])
    return _layernorm(ALPHA * x + y, g, b)


def _memkv_kernel(m_ref, w_ref, k_ref, v_ref):
    kv = _dot(m_ref[...].astype(BF16), w_ref[...])
    k_ref[...] = kv[:, :WIDTH].astype(BF16)
    v_ref[...] = kv[:, WIDTH:].astype(BF16)


def _memkv(mem, wkv):
    batch, m, _ = mem.shape
    return pl.pallas_call(
        _memkv_kernel,
        grid=(batch,),
        in_specs=[pl.BlockSpec((None, m, D_MODEL), lambda b: (b, 0, 0)), _const_spec(wkv.shape)],
        out_specs=[pl.BlockSpec((None, m, WIDTH), lambda b: (b, 0, 0))] * 2,
        out_shape=[jax.ShapeDtypeStruct((batch, m, WIDTH), BF16)] * 2,
        compiler_params=_cparams(("parallel",)),
        name="mem_kv",
    )(mem, wkv)


def _xattn_body(x, wq_ref, k, v, wo_ref, g, b):
    q = _dot(x.astype(BF16), wq_ref[...]).astype(BF16)
    lane = lax.broadcasted_iota(I32, (1, WIDTH), 1) // HEAD_DIM
    heads = range(N_HEADS)
    ss = [_dot_nt(jnp.where(lane == hh, q, jnp.zeros_like(q)), k) for hh in heads]
    es = [jnp.exp(s - jnp.max(s, axis=-1, keepdims=True)) for s in ss]
    ps = [(e / jnp.sum(e, axis=-1, keepdims=True)).astype(BF16) for e in es]
    o = jnp.zeros((x.shape[0], WIDTH), F32)
    for hh in heads:
        o = o + jnp.where(lane == hh, _dot(ps[hh], v), 0.0)
    y = _dot(o.astype(BF16), wo_ref[...])
    return _layernorm(ALPHA * x + y, g, b)


def _merge_xattn_kernel(x_ref, y0_ref, y1_ref, y2_ref, y3_ref, wg_ref, wb_ref, wo_ref, g1_ref, b1_ref,
                        wq_ref, k_ref, v_ref, xwo_ref, g2_ref, b2_ref, o_ref):
    x1 = _merge_body(x_ref[...], (y0_ref, y1_ref, y2_ref, y3_ref), wg_ref, wb_ref, wo_ref,
                     g1_ref[...], b1_ref[...])
    o_ref[...] = _xattn_body(x1, wq_ref, k_ref[...], v_ref[...], xwo_ref, g2_ref[...], b2_ref[...])


def _merge_xattn(x2d, ys, wg, wb, wo, g1, b1, wq, k, v, xwo, g2, b2, batch, seq):
    tm = TM_A
    m = k.shape[1]
    per = seq // tm
    row = lambda w: pl.BlockSpec((tm, w), lambda bb, i: (bb * per + i, 0))
    kv_spec = pl.BlockSpec((None, m, WIDTH), lambda bb, i: (bb, 0, 0))
    vec = _const_spec((1, D_MODEL))
    return pl.pallas_call(
        _merge_xattn_kernel,
        grid=(batch, per),
        in_specs=[row(D_MODEL)] + [row(WIDTH)] * 4 +
                 [_const_spec(wg.shape), _const_spec(wb.shape), _const_spec(wo.shape), vec, vec,
                  _const_spec(wq.shape), kv_spec, kv_spec, _const_spec(xwo.shape), vec, vec],
        out_specs=row(D_MODEL),
        out_shape=jax.ShapeDtypeStruct((batch * seq, D_MODEL), F32),
        compiler_params=_cparams(("parallel", "parallel")),
        name="merge_xattn_ln",
    )(x2d, *ys, wg, wb, wo, g1.reshape(1, -1), b1.reshape(1, -1),
      wq, k, v, xwo, g2.reshape(1, -1), b2.reshape(1, -1))


def _ffn_kernel(x_ref, w13_ref, w2_ref, g_ref, b_ref, o_ref):
    x = x_ref[...]
    xb = x.astype(BF16)
    y = None
    for h in range(F_DENSE // TF_FFN):
        lo = h * TF_FFN
        a = _dot(xb, w13_ref[:, lo:lo + TF_FFN])
        gate = _dot(xb, w13_ref[:, F_DENSE + lo:F_DENSE + lo + TF_FFN])
        part = _dot((a * jax.nn.sigmoid(a) * gate).astype(BF16), w2_ref[lo:lo + TF_FFN, :])
        y = part if y is None else y + part
    o_ref[...] = _layernorm(ALPHA * x + y, g_ref[...], b_ref[...])


def _ffn(x2d, w13, w2, g, b):
    n = x2d.shape[0]
    tm = TM_FFN
    return pl.pallas_call(
        _ffn_kernel,
        grid=(n // tm,),
        in_specs=[pl.BlockSpec((tm, D_MODEL), lambda i: (i, 0)),
                  _const_spec(w13.shape), _const_spec(w2.shape),
                  _const_spec((1, D_MODEL)), _const_spec((1, D_MODEL))],
        out_specs=pl.BlockSpec((tm, D_MODEL), lambda i: (i, 0)),
        out_shape=jax.ShapeDtypeStruct((n, D_MODEL), F32),
        compiler_params=_cparams(("parallel",)),
        name="ffn_ln",
    )(x2d, w13, w2, g.reshape(1, -1), b.reshape(1, -1))


def _router_kernel(x_ref, r_ref, info_ref, wts_ref, cnt_ref, carry_ref):
    tm = x_ref.shape[0]

    @pl.when(pl.program_id(0) == 0)
    def _():
        carry_ref[...] = jnp.zeros_like(carry_ref)

    logits = jnp.dot(x_ref[...], r_ref[...], precision=lax.Precision.HIGHEST, preferred_element_type=F32)
    lane = lax.broadcasted_iota(I32, (tm, LANES), 1)
    lg = jnp.where(lane < N_EXPERTS, logits, -jnp.inf)
    m1 = jnp.max(lg, axis=-1, keepdims=True)
    i1 = jnp.min(jnp.where(lg == m1, lane, LANES), axis=-1, keepdims=True)
    lg2 = jnp.where(lane == i1, -jnp.inf, lg)
    m2 = jnp.max(lg2, axis=-1, keepdims=True)
    i2 = jnp.min(jnp.where(lg2 == m2, lane, LANES), axis=-1, keepdims=True)
    e = jnp.exp(m2 - m1)
    w1 = 1.0 / (1.0 + e)
    w2 = e / (1.0 + e)
    sel1 = lane == i1
    sel2 = lane == i2
    chosen = jnp.where(sel1 | sel2, 1.0, 0.0)
    row = lax.broadcasted_iota(I32, (tm, tm), 0)
    col = lax.broadcasted_iota(I32, (tm, tm), 1)
    before = (col < row).astype(BF16)
    ranks = _dot(before, chosen.astype(BF16)) + carry_ref[...]
    r1 = jnp.sum(jnp.where(sel1, ranks, 0.0), axis=-1, keepdims=True)
    r2 = jnp.sum(jnp.where(sel2, ranks, 0.0), axis=-1, keepdims=True)
    carry_ref[...] = carry_ref[...] + jnp.sum(chosen, axis=0, keepdims=True)
    info = jnp.where(lane == 0, i1.astype(F32), jnp.where(lane == 1, i2.astype(F32),
                     jnp.where(lane == 2, r1, jnp.where(lane == 3, r2, 0.0))))
    info_ref[...] = jnp.transpose(info)[:info_ref.shape[0], :]
    wts_ref[...] = jnp.where(lane == 0, w1, jnp.where(lane == 1, w2, 0.0))
    cnt_ref[...] = carry_ref[...]


def _router(x2d, router):
    n = x2d.shape[0]
    tm = TM_ROUTER
    r_pad = jnp.zeros((D_MODEL, LANES), F32).at[:, :N_EXPERTS].set(router.astype(F32))
    row = pl.BlockSpec((tm, LANES), lambda i: (i, 0))
    return pl.pallas_call(
        _router_kernel,
        grid=(n // tm,),
        in_specs=[pl.BlockSpec((tm, D_MODEL), lambda i: (i, 0)), _const_spec(r_pad.shape)],
        out_specs=[pl.BlockSpec((8, tm), lambda i: (0, i)), row, pl.BlockSpec((1, LANES), lambda i: (0, 0))],
        out_shape=[jax.ShapeDtypeStruct((8, n), F32), jax.ShapeDtypeStruct((n, LANES), F32),
                   jax.ShapeDtypeStruct((1, LANES), F32)],
        scratch_shapes=[pltpu.VMEM((1, LANES), F32)],
        compiler_params=_cparams(("arbitrary",)),
        name="moe_router",
    )(x2d, r_pad)


def _dispatch_kernel(pad_ref, d0_ref, d1_ref, x_ref, xb_hbm, stage_ref, sems):
    tm = x_ref.shape[0]
    i = pl.program_id(0)
    last = pl.num_programs(0) - 1
    slot = i % 2

    def wait_step(s):
        for _ in range(2):
            pltpu.make_async_copy(stage_ref.at[s], xb_hbm.at[pl.ds(0, tm), :], sems.at[s]).wait()

    @pl.when(i >= 2)
    def _():
        wait_step(slot)

    for s in range(2):
        @pl.when(slot == s)
        def _():
            stage_ref[s] = x_ref[...]

            def issue(r, c):
                for k in range(2):
                    pltpu.make_async_copy(stage_ref.at[s, pl.ds(r, 1), :],
                                          xb_hbm.at[pl.ds((d0_ref, d1_ref)[k][0, r], 1), :], sems.at[s]).start()
                return c
            lax.fori_loop(0, tm, issue, 0, unroll=8)

    @pl.when(i == last)
    def _():
        def fill(e, c):
            def one(s, c2):
                pltpu.make_async_copy(stage_ref.at[slot, pl.ds(0, 1), :], xb_hbm.at[pl.ds(s, 1), :],
                                      sems.at[2]).start()
                return c2

            def done(s, c2):
                pltpu.make_async_copy(stage_ref.at[slot, pl.ds(0, 1), :], xb_hbm.at[pl.ds(0, 1), :],
                                      sems.at[2]).wait()
                return c2
            lax.fori_loop(pad_ref[0, e], pad_ref[1, e], one, 0)
            lax.fori_loop(pad_ref[0, e], pad_ref[1, e], done, 0)
            return c
        lax.fori_loop(0, pad_ref.shape[1], fill, 0)
        wait_step(slot)

        @pl.when(last >= 1)
        def _():
            wait_step(1 - slot)


def _dispatch(x2d, dest, pads, nblk):
    n = x2d.shape[0]
    tm = TM_DISP
    nt = n // tm
    grid_spec = pltpu.PrefetchScalarGridSpec(
        num_scalar_prefetch=1,
        grid=(nt,),
        in_specs=[pl.BlockSpec((None, 1, tm), lambda i, pads: (i, 0, 0), memory_space=pltpu.SMEM),
                  pl.BlockSpec((None, 1, tm), lambda i, pads: (i, 0, 0), memory_space=pltpu.SMEM),
                  pl.BlockSpec((tm, D_MODEL), lambda i, pads: (i, 0))],
        out_specs=pl.BlockSpec(memory_space=pl.ANY),
        scratch_shapes=[pltpu.VMEM((2, tm, D_MODEL), F32), pltpu.SemaphoreType.DMA((3,))],
    )
    return pl.pallas_call(
        _dispatch_kernel,
        grid_spec=grid_spec,
        out_shape=jax.ShapeDtypeStruct((nblk * MOE_TB, D_MODEL), F32),
        compiler_params=_cparams(("arbitrary",), disable_bounds_checks=True),
        name="moe_dispatch",
    )(pads, dest[0].reshape(nt, 1, tm), dest[1].reshape(nt, 1, tm), x2d)


def _expert_kernel(nused_ref, bexp_ref, x_ref, w1_ref, w3_ref, w2_ref, o_ref, acc_ref):
    f = pl.program_id(1)

    @pl.when(pl.program_id(0) < nused_ref[0])
    def _():
        xb = x_ref[...].astype(BF16)
        a = _dot(xb, w1_ref[...])
        gate = _dot(xb, w3_ref[...])
        part = _dot((a * jax.nn.sigmoid(a) * gate).astype(BF16), w2_ref[...])

        @pl.when(f == 0)
        def _():
            acc_ref[...] = part

        @pl.when(f > 0)
        def _():
            acc_ref[...] += part

        @pl.when(f == pl.num_programs(1) - 1)
        def _():
            o_ref[...] = acc_ref[...]

    @pl.when(pl.program_id(0) >= nused_ref[0])
    def _():
        o_ref[...] = jnp.zeros_like(o_ref)


def _experts(xb, w13, w2, nused, blk_exp, nblk):
    tb, tf = MOE_TB, MOE_TF
    nf = F_EXPERT // tf
    w13 = w13.astype(BF16)

    def blk(i, nu):
        return jnp.maximum(jnp.minimum(i, nu[0] - 1), 0)

    def ftile(i, f, nu):
        return jnp.where(i < nu[0], f, nf - 1)

    grid_spec = pltpu.PrefetchScalarGridSpec(
        num_scalar_prefetch=2,
        grid=(nblk, nf),
        in_specs=[pl.BlockSpec((tb, D_MODEL), lambda i, f, nu, be: (blk(i, nu), 0)),
                  pl.BlockSpec((None, D_MODEL, tf), lambda i, f, nu, be: (be[blk(i, nu)], 0, ftile(i, f, nu))),
                  pl.BlockSpec((None, D_MODEL, tf), lambda i, f, nu, be: (be[blk(i, nu)], 0, nf + ftile(i, f, nu))),
                  pl.BlockSpec((None, tf, D_MODEL), lambda i, f, nu, be: (be[blk(i, nu)], ftile(i, f, nu), 0))],
        out_specs=pl.BlockSpec((tb, D_MODEL), lambda i, f, nu, be: (i, 0)),
        scratch_shapes=[pltpu.VMEM((tb, D_MODEL), F32)],
    )
    return pl.pallas_call(
        _expert_kernel,
        grid_spec=grid_spec,
        out_shape=jax.ShapeDtypeStruct((nblk * tb, D_MODEL), F32),
        compiler_params=_cparams(("arbitrary", "arbitrary")),
        name="moe_experts",
    )(nused, blk_exp, xb, w13, w13, w2.astype(BF16))


def _combine_kernel(d0_ref, d1_ref, n0_ref, n1_ref, y_hbm, x_ref, wts_ref, g_ref, b_ref, o_ref, buf_ref, sems):
    tm = x_ref.shape[0]
    i = pl.program_id(0)
    slot = i % 2

    def gather(idx_refs, s):
        def issue(r, c):
            for k in range(2):
                pltpu.make_async_copy(y_hbm.at[pl.ds(idx_refs[k][0, r], 1), :],
                                      buf_ref.at[s, k, pl.ds(r, 1), :], sems.at[s]).start()
            return c
        lax.fori_loop(0, tm, issue, 0, unroll=8)

    @pl.when(i == 0)
    def _():
        gather((d0_ref, d1_ref), 0)

    for s in range(2):
        @pl.when((i + 1 < pl.num_programs(0)) & (slot != s))
        def _():
            gather((n0_ref, n1_ref), s)

    for k in range(2):
        pltpu.make_async_copy(y_hbm.at[pl.ds(0, tm), :], buf_ref.at[slot, k], sems.at[slot]).wait()
    wts = wts_ref[...]
    y = wts[:, 0:1] * buf_ref[slot, 0] + wts[:, 1:2] * buf_ref[slot, 1]
    o_ref[...] = _layernorm(ALPHA * x_ref[...] + y, g_ref[...], b_ref[...])


def _combine(yb, dest, x2d, wts, g, b):
    n = x2d.shape[0]
    tm = TM_COMB
    nt = n // tm
    row = lambda w: pl.BlockSpec((tm, w), lambda i: (i, 0))
    cur = pl.BlockSpec((None, 1, tm), lambda i: (i, 0, 0), memory_space=pltpu.SMEM)
    nxt = pl.BlockSpec((None, 1, tm), lambda i: (jnp.minimum(i + 1, nt - 1), 0, 0), memory_space=pltpu.SMEM)
    d0, d1 = (d.reshape(nt, 1, tm) for d in dest)
    return pl.pallas_call(
        _combine_kernel,
        grid=(nt,),
        in_specs=[cur, cur, nxt, nxt, pl.BlockSpec(memory_space=pl.ANY), row(D_MODEL), row(LANES),
                  _const_spec((1, D_MODEL)), _const_spec((1, D_MODEL))],
        out_specs=row(D_MODEL),
        out_shape=jax.ShapeDtypeStruct((n, D_MODEL), F32),
        scratch_shapes=[pltpu.VMEM((2, 2, tm, D_MODEL), F32), pltpu.SemaphoreType.DMA((2,))],
        compiler_params=_cparams(("arbitrary",), disable_bounds_checks=True),
        name="moe_combine_ln",
    )(d0, d1, d0, d1, yb, x2d, wts, g.reshape(1, -1), b.reshape(1, -1))


def _moe(x2d, router, w13, w2, g, b):
    n = x2d.shape[0]
    tb = MOE_TB
    info, wts, cnt = _router(x2d, router)
    counts = cnt[0, :N_EXPERTS].astype(I32)
    padded = (counts + tb - 1) // tb * tb
    pend = jnp.cumsum(padded)
    pstart = pend - padded
    info = info.astype(I32)
    dest = tuple((sum(jnp.where(info[k] == e, pstart[e], 0) for e in range(N_EXPERTS)) + info[2 + k]).astype(I32)
                 for k in range(2))
    nblk = -(-(2 * n + N_EXPERTS * (tb - 1)) // tb)
    pads = jnp.stack([jnp.append(pstart + counts, pend[-1]), jnp.append(pend, nblk * tb)]).astype(I32)
    nused = (pend[-1] // tb).astype(I32).reshape(1)
    first_row = jnp.arange(nblk, dtype=I32) * tb
    blk_exp = jnp.minimum(jnp.sum(pend[None, :] <= first_row[:, None], axis=1), N_EXPERTS - 1).astype(I32)
    xb = _dispatch(x2d, dest, pads, nblk)
    yb = _experts(xb, w13, w2, nused, blk_exp, nblk)
    return _combine(yb, dest, x2d, wts, g, b)


def kernel(x, mem, positions, rel_bias_table, hgrn_lb_logits, w_in, mla_q_norm, mla_w_uq, mla_kv_norm, mla_w_ukv, swa_sinks, hgrn_norm, w_branch, w_out, ln_g, ln_b, xa_wq, xa_wkv, xa_wo, ffn_w13, ffn_w2, moe_router, moe_w13, moe_w2):
    batch, seq, _ = x.shape
    n = batch * seq
    sm = jax.nn.softmax(hgrn_lb_logits.astype(F32), axis=0)
    lower_bounds = jnp.cumsum(sm, axis=0) - sm[0]
    ctab, stab = _rope_tables(positions)
    xc = x.reshape(n, D_MODEL)
    for l in range(DEPTH):
        wts = _inproj_weights(w_in[l], mla_w_uq[l], mla_w_ukv[l])
        mq, mk, mv, swq, swk, swv, hg, sbq, sbk, sbv = _inproj(xc, wts, ctab, stab, mla_q_norm[l], mla_kv_norm[l])
        y_mla = _mla_attention(mq, mk, mv, batch, seq)
        y_swa = _swa_attention(swq, swk, swv, positions, swa_sinks[l], rel_bias_table, batch, seq)
        y_hg = _hgrn(hg, lower_bounds[l], hgrn_norm[l], batch, seq)
        y_sb = _sb_attention(sbq, sbk, sbv, batch, seq)
        go = _IN_OFF['gates']
        mk_, mv_ = _memkv(mem, xa_wkv[l].astype(BF16))
        wb = w_branch[l].at[1].set(
            w_branch[l][1].reshape(N_HEADS, HEAD_DIM, D_MODEL)[jnp.array([0, 2, 1, 3])].reshape(WIDTH, D_MODEL))
        xc = _merge_xattn(xc, (y_mla, y_swa, y_hg, y_sb), w_in[l][:, go:].astype(BF16), wb.astype(BF16),
                          w_out[l].astype(BF16), ln_g[l, 0], ln_b[l, 0],
                          (xa_wq[l] * QK_SCALE).astype(BF16), mk_, mv_, xa_wo[l].astype(BF16),
                          ln_g[l, 1], ln_b[l, 1], batch, seq)
        if l % 2 == 0:
            xc = _ffn(xc, ffn_w13[l // 2].astype(BF16), ffn_w2[l // 2].astype(BF16), ln_g[l, 2], ln_b[l, 2])
        else:
            xc = _moe(xc, moe_router[l // 2], moe_w13[l // 2], moe_w2[l // 2],
                      ln_g[l, 2], ln_b[l, 2])
    return xc.reshape(batch, seq, D_MODEL)
```

```python
import functools
import math

import jax
import jax.numpy as jnp
from jax import lax
from jax.experimental import pallas as pl
from jax.experimental.pallas import tpu as pltpu

F32 = jnp.float32
BF16 = jnp.bfloat16
I32 = jnp.int32

D_MODEL = 1024
DEPTH = 2
EPS = 1e-5
NEG_BIG = -1e30
LANES = 128
HEAD_DIM = 64
N_HEADS = 4
WIDTH = N_HEADS * HEAD_DIM

MLA_Q_LORA = 256
MLA_KV_LORA = 128
MLA_NOPE = 64
MLA_ROPE = 32
ROPE_THETA = 10000.0
MLA_SCALE = (MLA_NOPE + MLA_ROPE) ** -0.5
LOG2E = math.log2(math.e)
QK_SCALE = HEAD_DIM ** -0.5

SB_RUN_FLOOR = -150.0
SWA_WINDOW = 128
REL_BUCKETS = 32
REL_MAX_DIST = 128
HGRN_CHUNK = 64
HGRN_BLOCK = 16
N_EXPERTS = 8
F_DENSE = 2816
F_EXPERT = 3584
ALPHA = (2 * DEPTH) ** 0.25

_IN_SPLITS = (('mla_cq', 256), ('mla_ckv', 128), ('mla_kr', 32), ('swa_q', 256), ('swa_k', 128),
              ('swa_v', 128), ('hgrn', 1024), ('sb_q', 256), ('sb_k', 256), ('sb_v', 256), ('gates', 4096))
_IN_OFF = {}
_o = 0
for _n, _w in _IN_SPLITS:
    _IN_OFF[_n] = _o
    _o += _w

_A_SPLITS = (('cq', 256), ('ckv', 128), ('kra', 128), ('krb', 128), ('swa_q', 256), ('swa_k', 128),
             ('swa_v', 128), ('hgrn', 1024), ('sb_q', 256), ('sb_k', 256), ('sb_v', 256))
_A_OFF = {}
_o = 0
for _n, _w in _A_SPLITS:
    _A_OFF[_n] = (_o, _o + _w)
    _o += _w
A_COLS = _o

TM_INPROJ = 1024
TM_A = 512
TQ_ATT = 256
MLA_TQ = 512
MLA_TK = 512
MLA_WIDE = 4
MLA_GROUP = 4
SWA_TQ = 1024
HG_ROWS = 512
TM_FFN = 1024
TF_FFN = 256
MOE_TB = 1024
MOE_TF = 896
TM_ROUTER = 512
TM_COMB = 512
TM_DISP = 1024
VMEM_LIMIT = 56 * 1024 * 1024
assert MLA_WIDE == 4 and MLA_TQ % MLA_TK == 0


def _cparams(sem, **kw):
    return pltpu.CompilerParams(dimension_semantics=sem, vmem_limit_bytes=VMEM_LIMIT, **kw)


def _const_spec(shape):
    nd = len(shape)
    return pl.BlockSpec(shape, lambda *_: (0,) * nd, pipeline_mode=pl.Buffered(1))


def _layernorm(v, g, b):
    mu = jnp.mean(v, axis=-1, keepdims=True)
    vc = v - mu
    var = jnp.mean(vc * vc, axis=-1, keepdims=True)
    return vc * lax.rsqrt(var + EPS) * g + b


def _dot(a, b):
    return jnp.dot(a, b, preferred_element_type=F32)


def _dot_nt(a, b):
    return lax.dot_general(a, b, (((1,), (1,)), ((), ())), preferred_element_type=F32)


def _split3(a):
    hi = a.astype(BF16)
    r = a - hi.astype(F32)
    mid = r.astype(BF16)
    lo = (r - mid.astype(F32)).astype(BF16)
    return hi, mid, lo


def _rope_kernel(pos_ref, freq_ref, c_ref, s_ref):
    lane = lax.broadcasted_iota(I32, pos_ref.shape, 1)
    ang = pos_ref[...] * freq_ref[...]
    rope = (lane >= MLA_NOPE) & (lane < MLA_NOPE + MLA_ROPE)
    first = lane < MLA_NOPE + MLA_ROPE // 2
    c_ref[...] = jnp.where(lane < MLA_NOPE, 1.0, jnp.where(rope, jnp.cos(ang), 0.0))
    sn = jnp.sin(ang)
    s_ref[...] = jnp.where(rope, jnp.where(first, -sn, sn), 0.0)


def _rope_tables(positions):
    n = positions.size
    half = MLA_ROPE // 2
    inv_freq = ROPE_THETA ** (-jnp.arange(half, dtype=F32) / half)
    freq = jnp.zeros((1, LANES), F32).at[0, MLA_NOPE:MLA_NOPE + MLA_ROPE].set(jnp.tile(inv_freq, 2))
    posb = jnp.broadcast_to(positions.reshape(n, 1).astype(F32), (n, LANES))
    tm = 1024
    return pl.pallas_call(
        _rope_kernel,
        grid=(n // tm,),
        in_specs=[pl.BlockSpec((tm, LANES), lambda i: (i, 0)), _const_spec((1, LANES))],
        out_specs=[pl.BlockSpec((tm, LANES), lambda i: (i, 0))] * 2,
        out_shape=[jax.ShapeDtypeStruct((n, LANES), F32)] * 2,
        compiler_params=_cparams(("parallel",)),
        name="rope_tables",
    )(posb, freq)


def _inproj_kernel(x_ref, w_ref, c_ref, s_ref, qn_ref, kvn_ref, wuqa_ref, wuqb_ref, wuk_ref, wuv_ref,
                   mq_ref, mk_ref, mv_ref, swq_ref, swk_ref, swv_ref, hg_ref, sbq_ref, sbk_ref, sbv_ref):
    h = _dot(x_ref[...].astype(BF16), w_ref[...])

    def cols(name):
        lo, hi = _A_OFF[name]
        return h[:, lo:hi]

    c = c_ref[...]
    s = s_ref[...]
    c4 = jnp.concatenate([c] * N_HEADS, axis=1)
    s4 = jnp.concatenate([s] * N_HEADS, axis=1)

    cq = cols('cq')
    cqn = (cq * lax.rsqrt(jnp.mean(cq * cq, axis=-1, keepdims=True) + EPS) * qn_ref[...]).astype(BF16)
    q = _dot(cqn, wuqa_ref[...]) * c4 + _dot(cqn, wuqb_ref[...]) * s4
    mq_ref[...] = (q * (MLA_SCALE * LOG2E)).astype(BF16)

    ckv = cols('ckv')
    ckvn = (ckv * lax.rsqrt(jnp.mean(ckv * ckv, axis=-1, keepdims=True) + EPS) * kvn_ref[...]).astype(BF16)
    krot = cols('kra') * c + cols('krb') * s
    mk_ref[...] = (_dot(ckvn, wuk_ref[...]) + jnp.concatenate([krot] * N_HEADS, axis=1)).astype(BF16)
    mv_ref[...] = _dot(ckvn, wuv_ref[...]).astype(BF16)

    swq_ref[...] = cols('swa_q').astype(BF16)
    swk_ref[...] = cols('swa_k').astype(BF16)
    swv_ref[...] = cols('swa_v').astype(BF16)
    hg_ref[...] = cols('hgrn')
    sbq_ref[...] = cols('sb_q').astype(BF16)
    sbk_ref[...] = cols('sb_k').astype(BF16)
    sbv_ref[...] = cols('sb_v').astype(BF16)


def _inproj_weights(w_in, w_uq, w_ukv):
    def seg(name, width):
        o = _IN_OFF[name]
        return w_in[:, o:o + width]

    kr = seg('mla_kr', MLA_ROPE)
    half = MLA_ROPE // 2
    z64 = jnp.zeros((D_MODEL, MLA_NOPE), F32)
    z32 = jnp.zeros((D_MODEL, LANES - MLA_NOPE - MLA_ROPE), F32)
    kra = jnp.concatenate([z64, kr, z32], axis=1)
    krb = jnp.concatenate([z64, kr[:, half:], kr[:, :half], z32], axis=1)
    swq = seg('swa_q', 256).reshape(D_MODEL, N_HEADS, HEAD_DIM)[:, jnp.array([0, 2, 1, 3])].reshape(D_MODEL, WIDTH)
    w_a = jnp.concatenate([
        seg('mla_cq', 256), seg('mla_ckv', 128), kra, krb,
        swq * QK_SCALE, seg('swa_k', 128), seg('swa_v', 128),
        seg('hgrn', 1024), seg('sb_q', 256) * (QK_SCALE * LOG2E), seg('sb_k', 256), seg('sb_v', 256)], axis=1)

    qd = MLA_NOPE + MLA_ROPE
    zq = jnp.zeros((MLA_Q_LORA, LANES - qd), F32)
    zn = jnp.zeros((MLA_Q_LORA, MLA_NOPE), F32)
    qa, qb = [], []
    for hh in range(N_HEADS):
        nope = w_uq[:, hh * qd: hh * qd + MLA_NOPE]
        rope = w_uq[:, hh * qd + MLA_NOPE: (hh + 1) * qd]
        qa += [nope, rope, zq]
        qb += [zn, rope[:, half:], rope[:, :half], zq]
    wuqa = jnp.concatenate(qa, axis=1)
    wuqb = jnp.concatenate(qb, axis=1)
    lane = jnp.arange(N_HEADS * LANES) % LANES
    wuk = jnp.where(lane[None, :] < MLA_NOPE, w_ukv, 0.0)
    wuv = jnp.concatenate([w_ukv[:, hh * LANES + MLA_NOPE:(hh + 1) * LANES] for hh in range(N_HEADS)], axis=1)
    return tuple(t.astype(BF16) for t in (w_a, wuqa, wuqb, wuk, wuv))


def _inproj(x2d, wts, ctab, stab, q_norm, kv_norm):
    n = x2d.shape[0]
    w_a, wuqa, wuqb, wuk, wuv = wts
    tm = TM_INPROJ
    row = lambda w: pl.BlockSpec((tm, w), lambda i: (i, 0))
    out_w = (512, 512, 256, 256, 128, 128, 1024, 256, 256, 256)
    out_dt = (BF16, BF16, BF16, BF16, BF16, BF16, F32, BF16, BF16, BF16)
    return pl.pallas_call(
        _inproj_kernel,
        grid=(n // tm,),
        in_specs=[row(D_MODEL), _const_spec(w_a.shape), row(LANES), row(LANES),
                  _const_spec((1, MLA_Q_LORA)), _const_spec((1, MLA_KV_LORA)),
                  _const_spec(wuqa.shape), _const_spec(wuqb.shape), _const_spec(wuk.shape),
                  _const_spec(wuv.shape)],
        out_specs=[row(w) for w in out_w],
        out_shape=[jax.ShapeDtypeStruct((n, w), d) for w, d in zip(out_w, out_dt)],
        compiler_params=_cparams(("parallel",)),
        name="inproj",
    )(x2d, w_a, ctab, stab, q_norm.reshape(1, -1), kv_norm.reshape(1, -1), wuqa, wuqb, wuk, wuv)


def _half_mask(half):
    lane = lax.broadcasted_iota(I32, (1, LANES), 1)
    return (lane < HEAD_DIM) if half == 0 else (lane >= HEAD_DIM)


def _mla_kernel(q_ref, k_ref, v_ref, o_ref):
    tq = q_ref.shape[0]
    tk = MLA_TK
    nsub = tq // tk
    i = pl.program_id(1)
    row = lax.broadcasted_iota(I32, (tq, tk), 0)
    col = lax.broadcasted_iota(I32, (tq, tk), 1)
    ones = jnp.ones((1, LANES), BF16)

    def update(off, carry, heads, mask, width=tk):
        ss = [_dot_nt(q_ref[:, hh * LANES:(hh + 1) * LANES],
                      k_ref[pl.ds(off, width), hh * LANES:(hh + 1) * LANES]) for hh in heads]
        if mask is not None:
            ss = [jnp.where(mask, s, NEG_BIG) for s in ss]
        ms = [jnp.maximum(c[0], jnp.max(s, axis=-1, keepdims=True)) for c, s in zip(carry, ss)]
        pms = [jnp.exp2(s - m).astype(BF16) for s, m in zip(ss, ms)]
        new = []
        for n, hh in enumerate(heads):
            vb = v_ref[pl.ds(off, width), (hh // 2) * LANES:(hh // 2 + 1) * LANES]
            vb = jnp.where(_half_mask(hh % 2), vb, ones)
            m, acc = carry[n]
            new.append((ms[n], jnp.exp2(m - ms[n]) * acc + _dot(pms[n], vb)))
        return tuple(new)

    accs = []
    for g in range(0, N_HEADS, MLA_GROUP):
        heads = tuple(range(g, g + MLA_GROUP))
        init = tuple((jnp.full((tq, 1), NEG_BIG, F32), jnp.zeros((tq, LANES), F32)) for _ in heads)
        nkb = i * nsub
        wide = MLA_WIDE * tk
        carry = lax.fori_loop(
            0, nkb // MLA_WIDE,
            lambda j, c, heads=heads: update(pl.multiple_of(j * wide, wide), c, heads, None, wide), init)
        done = nkb // MLA_WIDE * MLA_WIDE
        rest = nkb - done
        carry = lax.cond(
            rest >= 2,
            lambda c, heads=heads: update(pl.multiple_of(done * tk, 2 * tk), c, heads, None, 2 * tk),
            lambda c: c, carry)
        carry = lax.cond(
            rest % 2 == 1,
            lambda c, heads=heads: update(pl.multiple_of((nkb - 1) * tk, tk), c, heads, None),
            lambda c: c, carry)
        for r in range(nsub):
            carry = update(pl.multiple_of(i * tq + r * tk, tk), carry, heads, col + r * tk <= row)
        accs += [c[1] for c in carry]
    outs = []
    for p in range(N_HEADS // 2):
        a0, a1 = accs[2 * p], accs[2 * p + 1]
        outs.append(jnp.where(_half_mask(0), a0 / a0[:, HEAD_DIM:HEAD_DIM + 1], a1 / a1[:, 0:1]))
    o_ref[...] = jnp.concatenate(outs, axis=1).astype(o_ref.dtype)


def _mla_attention(q, k, v, batch, seq):
    tq = MLA_TQ
    q3, k3, v3 = (t.reshape(batch, seq, t.shape[-1]) for t in (q, k, v))
    out = pl.pallas_call(
        _mla_kernel,
        grid=(batch, seq // tq),
        in_specs=[pl.BlockSpec((None, tq, 512), lambda b, i: (b, i, 0)),
                  pl.BlockSpec((None, seq, 512), lambda b, i: (b, 0, 0), pipeline_mode=pl.Buffered(1)),
                  pl.BlockSpec((None, seq, WIDTH), lambda b, i: (b, 0, 0), pipeline_mode=pl.Buffered(1))],
        out_specs=pl.BlockSpec((None, tq, WIDTH), lambda b, i: (b, i, 0)),
        out_shape=jax.ShapeDtypeStruct((batch, seq, WIDTH), BF16),
        compiler_params=_cparams(("parallel", "arbitrary")),
        name="mla_attention",
    )(q3, k3, v3)
    return out.reshape(batch * seq, WIDTH)


def _sb_kernel(q_ref, k_ref, v_ref, o_ref):
    tq = q_ref.shape[0]
    i = pl.program_id(1)
    row = lax.broadcasted_iota(I32, (tq, tq), 0)
    col = lax.broadcasted_iota(I32, (tq, tq), 1)
    strict = col < row
    later = (row > col).astype(BF16)
    qs = []
    for hh in range(N_HEADS):
        qp = q_ref[:, (hh // 2) * LANES:(hh // 2 + 1) * LANES]
        qs.append(jnp.where(_half_mask(hh % 2), qp, jnp.zeros_like(qp)))

    def block(j, carry, diag):
        off = pl.multiple_of(j * tq, tq)
        runs, accs = carry
        heads = range(N_HEADS)
        zs = [_dot_nt(qs[hh], k_ref[pl.ds(off, tq), (hh // 2) * LANES:(hh // 2 + 1) * LANES]) for hh in heads]
        lsps = [jnp.minimum(z, 0.0) - jnp.log2(1.0 + jnp.exp2(-jnp.abs(z))) for z in zs]
        lsns = [lsp - z for lsp, z in zip(lsps, zs)]
        if diag:
            lsns = [jnp.where(strict, t, 0.0) for t in lsns]
        his = [t.astype(BF16) for t in lsns]
        los = [(t - hi.astype(F32)).astype(BF16) for t, hi in zip(lsns, his)]
        rems = [_dot(hi, later) + _dot(lo, later) for hi, lo in zip(his, los)]
        args = [lsps[hh] + rems[hh] + runs[hh] for hh in heads]
        if diag:
            args = [jnp.where(strict, t, NEG_BIG) for t in args]
        probs = [jnp.exp2(t).astype(BF16) for t in args]
        new_runs = tuple(runs[hh] + rems[hh][:, 0:1] + lsns[hh][:, 0:1] for hh in heads)
        new_accs = list(accs)
        for hh in heads:
            p = hh // 2
            vb = v_ref[pl.ds(off, tq), p * LANES:(p + 1) * LANES]
            vb = jnp.where(_half_mask(hh % 2), vb, jnp.zeros_like(vb))
            new_accs[p] = new_accs[p] + _dot(probs[hh], vb)
        return new_runs, tuple(new_accs)

    init = (tuple(jnp.zeros((tq, 1), F32) for _ in range(N_HEADS)),
            tuple(jnp.zeros((tq, LANES), F32) for _ in range(N_HEADS // 2)))
    def still_active(runs):
        top = functools.reduce(jnp.maximum, runs)
        return (jnp.max(top) > SB_RUN_FLOOR).astype(I32)

    runs, accs = block(i, init, True)

    def cond(c):
        return (c[0] < i) & (c[1] > 0)

    def body(c):
        jj, _, runs, accs = c
        runs, accs = block(i - 1 - jj, (runs, accs), False)
        return jj + 1, still_active(runs), runs, accs

    _, _, _, accs = lax.while_loop(cond, body, (jnp.int32(0), still_active(runs), runs, accs))
    o_ref[...] = jnp.concatenate(accs, axis=1).astype(o_ref.dtype)


def _sb_attention(q, k, v, batch, seq):
    tq = TQ_ATT
    q3, k3, v3 = (t.reshape(batch, seq, WIDTH) for t in (q, k, v))
    out = pl.pallas_call(
        _sb_kernel,
        grid=(batch, seq // tq),
        in_specs=[pl.BlockSpec((None, tq, WIDTH), lambda b, i: (b, i, 0)),
                  pl.BlockSpec((None, seq, WIDTH), lambda b, i: (b, 0, 0)),
                  pl.BlockSpec((None, seq, WIDTH), lambda b, i: (b, 0, 0))],
        out_specs=pl.BlockSpec((None, tq, WIDTH), lambda b, i: (b, i, 0)),
        out_shape=jax.ShapeDtypeStruct((batch, seq, WIDTH), BF16),
        compiler_params=_cparams(("parallel", "arbitrary")),
        name="stick_breaking",
    )(q3, k3, v3)
    return out.reshape(batch * seq, WIDTH)


def _rel_bucket(dist):
    exact = REL_BUCKETS // 2
    n = jnp.maximum(dist, 0)
    nf = jnp.maximum(n, 1).astype(F32)
    large = exact + (jnp.log(nf / exact) / math.log(REL_MAX_DIST / exact) * (REL_BUCKETS - exact)).astype(I32)
    large = jnp.clip(large, 0, REL_BUCKETS - 1)
    return jnp.where(n < exact, n, large)


def _swa_kernel(sink_ref, tab_ref, q_ref, kc_ref, kh_ref, vc_ref, vh_ref, pq_ref, pkc_ref, pkh_ref, o_ref):
    w = SWA_WINDOW
    step = pl.program_id(1)
    row = lax.broadcasted_iota(I32, (w, w), 0)
    col = lax.broadcasted_iota(I32, (w, w), 1)
    valid_c = col <= row
    valid_p = col > row
    tabs = [jnp.broadcast_to(tab_ref[hh:hh + 1, :], (w, LANES)) for hh in range(N_HEADS)]
    ones = jnp.ones((1, LANES), BF16)
    nsub = q_ref.shape[0] // w
    chains = [(r, hh) for r in range(nsub) for hh in range(N_HEADS)]

    def real(hh):
        return (hh % 2) * 2 + hh // 2

    def keys(ref, halo_ref, r):
        cur = ref[r * w:(r + 1) * w, :]
        prev = ref[(r - 1) * w:r * w, :] if r else halo_ref[...]
        return cur, prev

    buckets = []
    for r in range(nsub):
        pq = pq_ref[r * w:(r + 1) * w, :]
        pk_prev = pkc_ref[:, (r - 1) * w:r * w] if r else pkh_ref[...]
        buckets.append((_rel_bucket(pq - pkc_ref[:, r * w:(r + 1) * w]), _rel_bucket(pq - pk_prev)))
    logits = []
    for r, hh in chains:
        qp = q_ref[r * w:(r + 1) * w, (hh // 2) * LANES:(hh // 2 + 1) * LANES]
        qh = jnp.where(_half_mask(hh % 2), qp, jnp.zeros_like(qp))
        kc, kp = keys(kc_ref, kh_ref, r)
        logits.append((_dot_nt(qh, kc), _dot_nt(qh, kp)))
    masked = []
    for (r, hh), (lc, lp) in zip(chains, logits):
        lc = jnp.where(valid_c, lc + jnp.take_along_axis(tabs[real(hh)], buckets[r][0], axis=1), NEG_BIG)
        lp = lp + jnp.take_along_axis(tabs[real(hh)], buckets[r][1], axis=1)
        lp = jnp.where(valid_p if r else valid_p & (step > 0), lp, NEG_BIG)
        masked.append((lc, lp))
    maxes = [jnp.maximum(jnp.maximum(jnp.max(lc, axis=-1, keepdims=True), jnp.max(lp, axis=-1, keepdims=True)),
                         sink_ref[real(hh)]) for (r, hh), (lc, lp) in zip(chains, masked)]
    probs = [(jnp.exp(lc - m).astype(BF16), jnp.exp(lp - m).astype(BF16)) for (lc, lp), m in zip(masked, maxes)]
    outs = {}
    for (r, hh), (ec, ep), m in zip(chains, probs, maxes):
        vc, vp = keys(vc_ref, vh_ref, r)
        mine = _half_mask(hh % 2)
        acc = _dot(ec, jnp.where(mine, vc, ones)) + _dot(ep, jnp.where(mine, vp, ones))
        den = (acc[:, 0:1] if hh % 2 else acc[:, HEAD_DIM:HEAD_DIM + 1]) + jnp.exp(sink_ref[real(hh)] - m)
        outs[(r, hh)] = acc / den
    for r in range(nsub):
        pairs = [jnp.where(_half_mask(0), outs[(r, 2 * p)], outs[(r, 2 * p + 1)]) for p in range(N_HEADS // 2)]
        o_ref[r * w:(r + 1) * w, :] = jnp.concatenate(pairs, axis=1).astype(o_ref.dtype)


def _swa_attention(q, k, v, positions, sinks, rel_table, batch, seq):
    w = SWA_WINDOW
    tq = SWA_TQ
    per = tq // w
    kvw = k.shape[-1]
    q3, k3, v3 = (t.reshape(batch, seq, t.shape[-1]) for t in (q, k, v))
    pcol = positions.reshape(batch, seq, 1)
    prow = positions.reshape(batch, 1, seq)
    tab = jnp.zeros((N_HEADS, LANES), F32).at[:, :REL_BUCKETS].set(rel_table.astype(F32).T)
    cur = lambda b, n: (b, n, 0)
    halo = lambda b, n: (b, jnp.maximum(n * per - 1, 0), 0)
    out = pl.pallas_call(
        _swa_kernel,
        grid=(batch, seq // tq),
        in_specs=[pl.BlockSpec(memory_space=pltpu.SMEM), _const_spec((N_HEADS, LANES)),
                  pl.BlockSpec((None, tq, WIDTH), cur),
                  pl.BlockSpec((None, tq, kvw), cur), pl.BlockSpec((None, w, kvw), halo),
                  pl.BlockSpec((None, tq, kvw), cur), pl.BlockSpec((None, w, kvw), halo),
                  pl.BlockSpec((None, tq, 1), cur),
                  pl.BlockSpec((None, 1, tq), lambda b, n: (b, 0, n)),
                  pl.BlockSpec((None, 1, w), lambda b, n: (b, 0, jnp.maximum(n * per - 1, 0)))],
        out_specs=pl.BlockSpec((None, tq, WIDTH), cur),
        out_shape=jax.ShapeDtypeStruct((batch, seq, WIDTH), BF16),
        compiler_params=_cparams(("parallel", "arbitrary")),
        name="swa_attention",
    )(sinks.astype(F32), tab, q3, k3, k3, v3, v3, pcol, prow, prow)
    return out.reshape(batch * seq, WIDTH)


def _hgrn_kernel(hg_ref, lb_ref, nw_ref, o_ref, state_ref):
    c = HGRN_CHUNK
    blk = HGRN_BLOCK

    @pl.when(pl.program_id(1) == 0)
    def _():
        state_ref[...] = jnp.zeros_like(state_ref)

    r64 = lax.broadcasted_iota(I32, (c, c), 0)
    c64 = lax.broadcasted_iota(I32, (c, c), 1)
    incl = (c64 <= r64).astype(BF16)
    ra = lax.broadcasted_iota(I32, (WIDTH, WIDTH), 0) // HEAD_DIM
    ca = lax.broadcasted_iota(I32, (WIDTH, WIDTH), 1) // HEAD_DIM
    same_head = ra == ca
    seg = same_head.astype(BF16)
    ones_cols = jnp.ones((c, LANES), BF16)
    trow = lax.broadcasted_iota(I32, (blk, WIDTH), 0)
    caps = [jnp.where(trow >= s_i, 0.0, NEG_BIG) for s_i in range(blk)]
    lane_head = lax.broadcasted_iota(I32, (1, WIDTH), 1) // HEAD_DIM
    lb = lb_ref[...]
    nw = nw_ref[...]
    dn0 = (((0,), (0,)), ((), ()))

    for ch in range(hg_ref.shape[0] // c):
        rows = slice(ch * c, (ch + 1) * c)
        qraw = hg_ref[rows, 0:WIDTH]
        fraw = hg_ref[rows, WIDTH:2 * WIDTH]
        v = hg_ref[rows, 2 * WIDTH:3 * WIDTH]
        graw = hg_ref[rows, 3 * WIDTH:4 * WIDTH]
        qf = qraw * jax.nn.sigmoid(qraw)
        forget = lb + (1.0 - lb) * jax.nn.sigmoid(fraw)
        lf = jnp.log(forget)
        kk = 1.0 - forget
        gate = graw * jax.nn.sigmoid(graw)
        vb = v.astype(BF16)

        lf3 = _split3(lf)
        bc = _dot(incl, lf3[0]) + _dot(incl, lf3[1]) + _dot(incl, lf3[2])
        b_last = bc[c - 1:c, :]
        tot_col = sum(lax.dot_general(t, ones_cols, dn0, preferred_element_type=F32) for t in lf3)
        decay_col = jnp.exp(jnp.concatenate([tot_col, tot_col], axis=1))

        state = state_ref[...]
        o_inter = _dot((qf * jnp.exp(bc)).astype(BF16), state.astype(BF16))

        def before(qa, qb, ka, kb):
            ref = bc[kb - 1:kb, :]
            qt = qf[qa:qb] * jnp.exp(bc[qa:qb] - ref)
            kt = (kk[ka:kb] * jnp.exp(ref - bc[ka:kb])).astype(BF16)
            qs = jnp.concatenate([jnp.where(lane_head == hh, qt, 0.0) for hh in range(N_HEADS)], axis=0)
            att = _dot_nt(qs.astype(BF16), kt)
            mix = _dot(att.astype(BF16), vb[ka:kb])
            nq = qb - qa
            return sum(jnp.where(lane_head == hh, mix[hh * nq:(hh + 1) * nq], 0.0) for hh in range(N_HEADS))

        bc2 = bc * LOG2E
        key2 = bc2 - jnp.log2(jnp.maximum(kk, 0.0))

        def inside(a):
            b2 = bc2[a:a + blk]
            qb_ = qf[a:a + blk]
            ws = []
            for s_i in range(blk):
                ws.append(qb_ * jnp.exp2(jnp.minimum(b2 - key2[a + s_i:a + s_i + 1, :], caps[s_i])))
            att = _dot(jnp.concatenate(ws, axis=0).astype(BF16), seg)
            return sum(att[s_i * blk:(s_i + 1) * blk] * v[a + s_i:a + s_i + 1, :] for s_i in range(blk))

        pieces = {a: [] for a in range(0, c, blk)}

        def cover(a, b):
            if b - a == blk:
                pieces[a].append(inside(a))
                return
            mid = (a + b) // 2
            cover(a, mid)
            cover(mid, b)
            res = before(mid, b, a, mid)
            for off in range(0, b - mid, blk):
                pieces[mid + off].append(res[off:off + blk])

        cover(0, c)
        o = o_inter + jnp.concatenate([sum(pieces[a]) for a in range(0, c, blk)], axis=0)

        khat = (kk * jnp.exp(b_last - bc)).astype(BF16)
        upd = lax.dot_general(khat, vb, dn0, preferred_element_type=F32)
        state_ref[...] = decay_col * state + jnp.where(same_head, upd, 0.0)

        o2 = _split3(o * o)
        ms = (_dot(o2[0], seg) + _dot(o2[1], seg)) * (1.0 / HEAD_DIM)
        o_ref[rows, :] = (o * lax.rsqrt(ms + EPS) * nw * gate).astype(o_ref.dtype)


def _hgrn(hg, lower_bound, norm_w, batch, seq):
    rows = HG_ROWS
    hg3 = hg.reshape(batch, seq, 4 * WIDTH)
    out = pl.pallas_call(
        _hgrn_kernel,
        grid=(batch, seq // rows),
        in_specs=[pl.BlockSpec((None, rows, 4 * WIDTH), lambda b, i: (b, i, 0)),
                  _const_spec((1, WIDTH)), _const_spec((1, WIDTH))],
        out_specs=pl.BlockSpec((None, rows, WIDTH), lambda b, i: (b, i, 0)),
        out_shape=jax.ShapeDtypeStruct((batch, seq, WIDTH), BF16),
        scratch_shapes=[pltpu.VMEM((WIDTH, WIDTH), F32)],
        compiler_params=_cparams(("parallel", "arbitrary")),
        name="hgrn2",
    )(hg3, lower_bound.reshape(1, WIDTH).astype(F32), norm_w.reshape(1, WIDTH).astype(F32))
    return out.reshape(batch * seq, WIDTH)


def _merge_body(x, y_refs, wg_ref, wb_ref, wo_ref, g, b):
    xb = x.astype(BF16)
    merged = jnp.zeros(x.shape, F32)
    for nbr, y_ref in enumerate(y_refs):
        gate = jax.nn.sigmoid(_dot(xb, wg_ref[:, nbr * D_MODEL:(nbr + 1) * D_MODEL]))
        merged = merged + gate * _dot(y_ref[...], wb_ref[nbr])
    y = _dot(merged.astype(BF16), wo_ref[...])
    return _layernorm(ALPHA * x + y, g, b)


def _memkv_kernel(m_ref, w_ref, k_ref, v_ref):
    kv = _dot(m_ref[...].astype(BF16), w_ref[...])
    k_ref[...] = kv[:, :WIDTH].astype(BF16)
    v_ref[...] = kv[:, WIDTH:].astype(BF16)


def _memkv(mem, wkv):
    batch, m, _ = mem.shape
    return pl.pallas_call(
        _memkv_kernel,
        grid=(batch,),
        in_specs=[pl.BlockSpec((None, m, D_MODEL), lambda b: (b, 0, 0)), _const_spec(wkv.shape)],
        out_specs=[pl.BlockSpec((None, m, WIDTH), lambda b: (b, 0, 0))] * 2,
        out_shape=[jax.ShapeDtypeStruct((batch, m, WIDTH), BF16)] * 2,
        compiler_params=_cparams(("parallel",)),
        name="mem_kv",
    )(mem, wkv)


def _xattn_body(x, wq_ref, k, v, wo_ref, g, b):
    q = _dot(x.astype(BF16), wq_ref[...]).astype(BF16)
    lane = lax.broadcasted_iota(I32, (1, WIDTH), 1) // HEAD_DIM
    heads = range(N_HEADS)
    ss = [_dot_nt(jnp.where(lane == hh, q, jnp.zeros_like(q)), k) for hh in heads]
    es = [jnp.exp(s - jnp.max(s, axis=-1, keepdims=True)) for s in ss]
    ps = [(e / jnp.sum(e, axis=-1, keepdims=True)).astype(BF16) for e in es]
    o = jnp.zeros((x.shape[0], WIDTH), F32)
    for hh in heads:
        o = o + jnp.where(lane == hh, _dot(ps[hh], v), 0.0)
    y = _dot(o.astype(BF16), wo_ref[...])
    return _layernorm(ALPHA * x + y, g, b)


def _merge_xattn_kernel(x_ref, y0_ref, y1_ref, y2_ref, y3_ref, wg_ref, wb_ref, wo_ref, g1_ref, b1_ref,
                        wq_ref, k_ref, v_ref, xwo_ref, g2_ref, b2_ref, o_ref):
    x1 = _merge_body(x_ref[...], (y0_ref, y1_ref, y2_ref, y3_ref), wg_ref, wb_ref, wo_ref,
                     g1_ref[...], b1_ref[...])
    o_ref[...] = _xattn_body(x1, wq_ref, k_ref[...], v_ref[...], xwo_ref, g2_ref[...], b2_ref[...])


def _merge_xattn(x2d, ys, wg, wb, wo, g1, b1, wq, k, v, xwo, g2, b2, batch, seq):
    tm = TM_A
    m = k.shape[1]
    per = seq // tm
    row = lambda w: pl.BlockSpec((tm, w), lambda bb, i: (bb * per + i, 0))
    kv_spec = pl.BlockSpec((None, m, WIDTH), lambda bb, i: (bb, 0, 0))
    vec = _const_spec((1, D_MODEL))
    return pl.pallas_call(
        _merge_xattn_kernel,
        grid=(batch, per),
        in_specs=[row(D_MODEL)] + [row(WIDTH)] * 4 +
                 [_const_spec(wg.shape), _const_spec(wb.shape), _const_spec(wo.shape), vec, vec,
                  _const_spec(wq.shape), kv_spec, kv_spec, _const_spec(xwo.shape), vec, vec],
        out_specs=row(D_MODEL),
        out_shape=jax.ShapeDtypeStruct((batch * seq, D_MODEL), F32),
        compiler_params=_cparams(("parallel", "parallel")),
        name="merge_xattn_ln",
    )(x2d, *ys, wg, wb, wo, g1.reshape(1, -1), b1.reshape(1, -1),
      wq, k, v, xwo, g2.reshape(1, -1), b2.reshape(1, -1))


def _ffn_kernel(x_ref, w13_ref, w2_ref, g_ref, b_ref, o_ref):
    x = x_ref[...]
    xb = x.astype(BF16)
    y = None
    for h in range(F_DENSE // TF_FFN):
        lo = h * TF_FFN
        a = _dot(xb, w13_ref[:, lo:lo + TF_FFN])
        gate = _dot(xb, w13_ref[:, F_DENSE + lo:F_DENSE + lo + TF_FFN])
        part = _dot((a * jax.nn.sigmoid(a) * gate).astype(BF16), w2_ref[lo:lo + TF_FFN, :])
        y = part if y is None else y + part
    o_ref[...] = _layernorm(ALPHA * x + y, g_ref[...], b_ref[...])


def _ffn(x2d, w13, w2, g, b):
    n = x2d.shape[0]
    tm = TM_FFN
    return pl.pallas_call(
        _ffn_kernel,
        grid=(n // tm,),
        in_specs=[pl.BlockSpec((tm, D_MODEL), lambda i: (i, 0)),
                  _const_spec(w13.shape), _const_spec(w2.shape),
                  _const_spec((1, D_MODEL)), _const_spec((1, D_MODEL))],
        out_specs=pl.BlockSpec((tm, D_MODEL), lambda i: (i, 0)),
        out_shape=jax.ShapeDtypeStruct((n, D_MODEL), F32),
        compiler_params=_cparams(("parallel",)),
        name="ffn_ln",
    )(x2d, w13, w2, g.reshape(1, -1), b.reshape(1, -1))


def _router_kernel(x_ref, r_ref, info_ref, wts_ref, cnt_ref, carry_ref):
    tm = x_ref.shape[0]

    @pl.when(pl.program_id(0) == 0)
    def _():
        carry_ref[...] = jnp.zeros_like(carry_ref)

    logits = jnp.dot(x_ref[...], r_ref[...], precision=lax.Precision.HIGHEST, preferred_element_type=F32)
    lane = lax.broadcasted_iota(I32, (tm, LANES), 1)
    lg = jnp.where(lane < N_EXPERTS, logits, -jnp.inf)
    m1 = jnp.max(lg, axis=-1, keepdims=True)
    i1 = jnp.min(jnp.where(lg == m1, lane, LANES), axis=-1, keepdims=True)
    lg2 = jnp.where(lane == i1, -jnp.inf, lg)
    m2 = jnp.max(lg2, axis=-1, keepdims=True)
    i2 = jnp.min(jnp.where(lg2 == m2, lane, LANES), axis=-1, keepdims=True)
    e = jnp.exp(m2 - m1)
    w1 = 1.0 / (1.0 + e)
    w2 = e / (1.0 + e)
    sel1 = lane == i1
    sel2 = lane == i2
    chosen = jnp.where(sel1 | sel2, 1.0, 0.0)
    row = lax.broadcasted_iota(I32, (tm, tm), 0)
    col = lax.broadcasted_iota(I32, (tm, tm), 1)
    before = (col < row).astype(BF16)
    ranks = _dot(before, chosen.astype(BF16)) + carry_ref[...]
    r1 = jnp.sum(jnp.where(sel1, ranks, 0.0), axis=-1, keepdims=True)
    r2 = jnp.sum(jnp.where(sel2, ranks, 0.0), axis=-1, keepdims=True)
    carry_ref[...] = carry_ref[...] + jnp.sum(chosen, axis=0, keepdims=True)
    info = jnp.where(lane == 0, i1.astype(F32), jnp.where(lane == 1, i2.astype(F32),
                     jnp.where(lane == 2, r1, jnp.where(lane == 3, r2, 0.0))))
    info_ref[...] = jnp.transpose(info)[:info_ref.shape[0], :]
    wts_ref[...] = jnp.where(lane == 0, w1, jnp.where(lane == 1, w2, 0.0))
    cnt_ref[...] = carry_ref[...]


def _router(x2d, router):
    n = x2d.shape[0]
    tm = TM_ROUTER
    r_pad = jnp.zeros((D_MODEL, LANES), F32).at[:, :N_EXPERTS].set(router.astype(F32))
    row = pl.BlockSpec((tm, LANES), lambda i: (i, 0))
    return pl.pallas_call(
        _router_kernel,
        grid=(n // tm,),
        in_specs=[pl.BlockSpec((tm, D_MODEL), lambda i: (i, 0)), _const_spec(r_pad.shape)],
        out_specs=[pl.BlockSpec((8, tm), lambda i: (0, i)), row, pl.BlockSpec((1, LANES), lambda i: (0, 0))],
        out_shape=[jax.ShapeDtypeStruct((8, n), F32), jax.ShapeDtypeStruct((n, LANES), F32),
                   jax.ShapeDtypeStruct((1, LANES), F32)],
        scratch_shapes=[pltpu.VMEM((1, LANES), F32)],
        compiler_params=_cparams(("arbitrary",)),
        name="moe_router",
    )(x2d, r_pad)


def _dispatch_kernel(pad_ref, d0_ref, d1_ref, x_ref, xb_hbm, stage_ref, sems):
    tm = x_ref.shape[0]
    i = pl.program_id(0)
    last = pl.num_programs(0) - 1
    slot = i % 2

    def wait_step(s):
        for _ in range(2):
            pltpu.make_async_copy(stage_ref.at[s], xb_hbm.at[pl.ds(0, tm), :], sems.at[s]).wait()

    @pl.when(i >= 2)
    def _():
        wait_step(slot)

    for s in range(2):
        @pl.when(slot == s)
        def _():
            stage_ref[s] = x_ref[...]

            def issue(r, c):
                for k in range(2):
                    pltpu.make_async_copy(stage_ref.at[s, pl.ds(r, 1), :],
                                          xb_hbm.at[pl.ds((d0_ref, d1_ref)[k][0, r], 1), :], sems.at[s]).start()
                return c
            lax.fori_loop(0, tm, issue, 0, unroll=8)

    @pl.when(i == last)
    def _():
        def fill(e, c):
            def one(s, c2):
                pltpu.make_async_copy(stage_ref.at[slot, pl.ds(0, 1), :], xb_hbm.at[pl.ds(s, 1), :],
                                      sems.at[2]).start()
                return c2

            def done(s, c2):
                pltpu.make_async_copy(stage_ref.at[slot, pl.ds(0, 1), :], xb_hbm.at[pl.ds(0, 1), :],
                                      sems.at[2]).wait()
                return c2
            lax.fori_loop(pad_ref[0, e], pad_ref[1, e], one, 0)
            lax.fori_loop(pad_ref[0, e], pad_ref[1, e], done, 0)
            return c
        lax.fori_loop(0, pad_ref.shape[1], fill, 0)
        wait_step(slot)

        @pl.when(last >= 1)
        def _():
            wait_step(1 - slot)


def _dispatch(x2d, dest, pads, nblk):
    n = x2d.shape[0]
    tm = TM_DISP
    nt = n // tm
    grid_spec = pltpu.PrefetchScalarGridSpec(
        num_scalar_prefetch=1,
        grid=(nt,),
        in_specs=[pl.BlockSpec((None, 1, tm), lambda i, pads: (i, 0, 0), memory_space=pltpu.SMEM),
                  pl.BlockSpec((None, 1, tm), lambda i, pads: (i, 0, 0), memory_space=pltpu.SMEM),
                  pl.BlockSpec((tm, D_MODEL), lambda i, pads: (i, 0))],
        out_specs=pl.BlockSpec(memory_space=pl.ANY),
        scratch_shapes=[pltpu.VMEM((2, tm, D_MODEL), F32), pltpu.SemaphoreType.DMA((3,))],
    )
    return pl.pallas_call(
        _dispatch_kernel,
        grid_spec=grid_spec,
        out_shape=jax.ShapeDtypeStruct((nblk * MOE_TB, D_MODEL), F32),
        compiler_params=_cparams(("arbitrary",), disable_bounds_checks=True),
        name="moe_dispatch",
    )(pads, dest[0].reshape(nt, 1, tm), dest[1].reshape(nt, 1, tm), x2d)


def _expert_kernel(nused_ref, bexp_ref, x_ref, w1_ref, w3_ref, w2_ref, o_ref, acc_ref):
    f = pl.program_id(1)

    @pl.when(pl.program_id(0) < nused_ref[0])
    def _():
        xb = x_ref[...].astype(BF16)
        a = _dot(xb, w1_ref[...])
        gate = _dot(xb, w3_ref[...])
        part = _dot((a * jax.nn.sigmoid(a) * gate).astype(BF16), w2_ref[...])

        @pl.when(f == 0)
        def _():
            acc_ref[...] = part

        @pl.when(f > 0)
        def _():
            acc_ref[...] += part

        @pl.when(f == pl.num_programs(1) - 1)
        def _():
            o_ref[...] = acc_ref[...]

    @pl.when(pl.program_id(0) >= nused_ref[0])
    def _():
        o_ref[...] = jnp.zeros_like(o_ref)


def _experts(xb, w13, w2, nused, blk_exp, nblk):
    tb, tf = MOE_TB, MOE_TF
    nf = F_EXPERT // tf
    w13 = w13.astype(BF16)

    def blk(i, nu):
        return jnp.maximum(jnp.minimum(i, nu[0] - 1), 0)

    def ftile(i, f, nu):
        return jnp.where(i < nu[0], f, nf - 1)

    grid_spec = pltpu.PrefetchScalarGridSpec(
        num_scalar_prefetch=2,
        grid=(nblk, nf),
        in_specs=[pl.BlockSpec((tb, D_MODEL), lambda i, f, nu, be: (blk(i, nu), 0)),
                  pl.BlockSpec((None, D_MODEL, tf), lambda i, f, nu, be: (be[blk(i, nu)], 0, ftile(i, f, nu))),
                  pl.BlockSpec((None, D_MODEL, tf), lambda i, f, nu, be: (be[blk(i, nu)], 0, nf + ftile(i, f, nu))),
                  pl.BlockSpec((None, tf, D_MODEL), lambda i, f, nu, be: (be[blk(i, nu)], ftile(i, f, nu), 0))],
        out_specs=pl.BlockSpec((tb, D_MODEL), lambda i, f, nu, be: (i, 0)),
        scratch_shapes=[pltpu.VMEM((tb, D_MODEL), F32)],
    )
    return pl.pallas_call(
        _expert_kernel,
        grid_spec=grid_spec,
        out_shape=jax.ShapeDtypeStruct((nblk * tb, D_MODEL), F32),
        compiler_params=_cparams(("arbitrary", "arbitrary")),
        name="moe_experts",
    )(nused, blk_exp, xb, w13, w13, w2.astype(BF16))


def _combine_kernel(d0_ref, d1_ref, n0_ref, n1_ref, y_hbm, x_ref, wts_ref, g_ref, b_ref, o_ref, buf_ref, sems):
    tm = x_ref.shape[0]
    i = pl.program_id(0)
    slot = i % 2

    def gather(idx_refs, s):
        def issue(r, c):
            for k in range(2):
                pltpu.make_async_copy(y_hbm.at[pl.ds(idx_refs[k][0, r], 1), :],
                                      buf_ref.at[s, k, pl.ds(r, 1), :], sems.at[s]).start()
            return c
        lax.fori_loop(0, tm, issue, 0, unroll=8)

    @pl.when(i == 0)
    def _():
        gather((d0_ref, d1_ref), 0)

    for s in range(2):
        @pl.when((i + 1 < pl.num_programs(0)) & (slot != s))
        def _():
            gather((n0_ref, n1_ref), s)

    for k in range(2):
        pltpu.make_async_copy(y_hbm.at[pl.ds(0, tm), :], buf_ref.at[slot, k], sems.at[slot]).wait()
    wts = wts_ref[...]
    y = wts[:, 0:1] * buf_ref[slot, 0] + wts[:, 1:2] * buf_ref[slot, 1]
    o_ref[...] = _layernorm(ALPHA * x_ref[...] + y, g_ref[...], b_ref[...])


def _combine(yb, dest, x2d, wts, g, b):
    n = x2d.shape[0]
    tm = TM_COMB
    nt = n // tm
    row = lambda w: pl.BlockSpec((tm, w), lambda i: (i, 0))
    cur = pl.BlockSpec((None, 1, tm), lambda i: (i, 0, 0), memory_space=pltpu.SMEM)
    nxt = pl.BlockSpec((None, 1, tm), lambda i: (jnp.minimum(i + 1, nt - 1), 0, 0), memory_space=pltpu.SMEM)
    d0, d1 = (d.reshape(nt, 1, tm) for d in dest)
    return pl.pallas_call(
        _combine_kernel,
        grid=(nt,),
        in_specs=[cur, cur, nxt, nxt, pl.BlockSpec(memory_space=pl.ANY), row(D_MODEL), row(LANES),
                  _const_spec((1, D_MODEL)), _const_spec((1, D_MODEL))],
        out_specs=row(D_MODEL),
        out_shape=jax.ShapeDtypeStruct((n, D_MODEL), F32),
        scratch_shapes=[pltpu.VMEM((2, 2, tm, D_MODEL), F32), pltpu.SemaphoreType.DMA((2,))],
        compiler_params=_cparams(("arbitrary",), disable_bounds_checks=True),
        name="moe_combine_ln",
    )(d0, d1, d0, d1, yb, x2d, wts, g.reshape(1, -1), b.reshape(1, -1))


def _moe(x2d, router, w13, w2, g, b):
    n = x2d.shape[0]
    tb = MOE_TB
    info, wts, cnt = _router(x2d, router)
    counts = cnt[0, :N_EXPERTS].astype(I32)
    padded = (counts + tb - 1) // tb * tb
    pend = jnp.cumsum(padded)
    pstart = pend - padded
    info = info.astype(I32)
    dest = tuple((sum(jnp.where(info[k] == e, pstart[e], 0) for e in range(N_EXPERTS)) + info[2 + k]).astype(I32)
                 for k in range(2))
    nblk = -(-(2 * n + N_EXPERTS * (tb - 1)) // tb)
    pads = jnp.stack([jnp.append(pstart + counts, pend[-1]), jnp.append(pend, nblk * tb)]).astype(I32)
    nused = (pend[-1] // tb).astype(I32).reshape(1)
    first_row = jnp.arange(nblk, dtype=I32) * tb
    blk_exp = jnp.minimum(jnp.sum(pend[None, :] <= first_row[:, None], axis=1), N_EXPERTS - 1).astype(I32)
    xb = _dispatch(x2d, dest, pads, nblk)
    yb = _experts(xb, w13, w2, nused, blk_exp, nblk)
    return _combine(yb, dest, x2d, wts, g, b)


def kernel(x, mem, positions, rel_bias_table, hgrn_lb_logits, w_in, mla_q_norm, mla_w_uq, mla_kv_norm, mla_w_ukv, swa_sinks, hgrn_norm, w_branch, w_out, ln_g, ln_b, xa_wq, xa_wkv, xa_wo, ffn_w13, ffn_w2, moe_router, moe_w13, moe_w2):
    batch, seq, _ = x.shape
    n = batch * seq
    sm = jax.nn.softmax(hgrn_lb_logits.astype(F32), axis=0)
    lower_bounds = jnp.cumsum(sm, axis=0) - sm[0]
    ctab, stab = _rope_tables(positions)
    xc = x.reshape(n, D_MODEL)
    for l in range(DEPTH):
        wts = _inproj_weights(w_in[l], mla_w_uq[l], mla_w_ukv[l])
        mq, mk, mv, swq, swk, swv, hg, sbq, sbk, sbv = _inproj(xc, wts, ctab, stab, mla_q_norm[l], mla_kv_norm[l])
        y_mla = _mla_attention(mq, mk, mv, batch, seq)
        y_swa = _swa_attention(swq, swk, swv, positions, swa_sinks[l], rel_bias_table, batch, seq)
        y_hg = _hgrn(hg, lower_bounds[l], hgrn_norm[l], batch, seq)
        y_sb = _sb_attention(sbq, sbk, sbv, batch, seq)
        go = _IN_OFF['gates']
        mk_, mv_ = _memkv(mem, xa_wkv[l].astype(BF16))
        wb = w_branch[l].at[1].set(
            w_branch[l][1].reshape(N_HEADS, HEAD_DIM, D_MODEL)[jnp.array([0, 2, 1, 3])].reshape(WIDTH, D_MODEL))
        xc = _merge_xattn(xc, (y_mla, y_swa, y_hg, y_sb), w_in[l][:, go:].astype(BF16), wb.astype(BF16),
                          w_out[l].astype(BF16), ln_g[l, 0], ln_b[l, 0],
                          (xa_wq[l] * QK_SCALE).astype(BF16), mk_, mv_, xa_wo[l].astype(BF16),
                          ln_g[l, 1], ln_b[l, 1], batch, seq)
        if l % 2 == 0:
            xc = _ffn(xc, ffn_w13[l // 2].astype(BF16), ffn_w2[l // 2].astype(BF16), ln_g[l, 2], ln_b[l, 2])
        else:
            xc = _moe(xc, moe_router[l // 2], moe_w13[l // 2], moe_w2[l // 2],
                      ln_g[l, 2], ln_b[l, 2])
    return xc.reshape(batch, seq, D_MODEL)
```

```python
import functools
import math

import jax
import jax.numpy as jnp
from jax import lax
from jax.experimental import pallas as pl
from jax.experimental.pallas import tpu as pltpu

F32 = jnp.float32
BF16 = jnp.bfloat16
I32 = jnp.int32

D_MODEL = 1024
DEPTH = 2
EPS = 1e-5
NEG_BIG = -1e30
LANES = 128
HEAD_DIM = 64
N_HEADS = 4
WIDTH = N_HEADS * HEAD_DIM

MLA_Q_LORA = 256
MLA_KV_LORA = 128
MLA_NOPE = 64
MLA_ROPE = 32
ROPE_THETA = 10000.0
MLA_SCALE = (MLA_NOPE + MLA_ROPE) ** -0.5
LOG2E = math.log2(math.e)
QK_SCALE = HEAD_DIM ** -0.5

SB_RUN_FLOOR = -150.0
SWA_WINDOW = 128
REL_BUCKETS = 32
REL_MAX_DIST = 128
HGRN_CHUNK = 64
HGRN_BLOCK = 16
N_EXPERTS = 8
F_DENSE = 2816
F_EXPERT = 3584
ALPHA = (2 * DEPTH) ** 0.25

_IN_SPLITS = (('mla_cq', 256), ('mla_ckv', 128), ('mla_kr', 32), ('swa_q', 256), ('swa_k', 128),
              ('swa_v', 128), ('hgrn', 1024), ('sb_q', 256), ('sb_k', 256), ('sb_v', 256), ('gates', 4096))
_IN_OFF = {}
_o = 0
for _n, _w in _IN_SPLITS:
    _IN_OFF[_n] = _o
    _o += _w

_A_SPLITS = (('cq', 256), ('ckv', 128), ('kra', 128), ('krb', 128), ('swa_q', 256), ('swa_k', 128),
             ('swa_v', 128), ('hgrn', 1024), ('sb_q', 256), ('sb_k', 256), ('sb_v', 256))
_A_OFF = {}
_o = 0
for _n, _w in _A_SPLITS:
    _A_OFF[_n] = (_o, _o + _w)
    _o += _w
A_COLS = _o

TM_INPROJ = 1024
TM_A = 1024
TN_MERGE = 256
TQ_ATT = 256
MLA_TQ = 512
MLA_TK = 512
MLA_WIDE = 4
MLA_GROUP = 4
SWA_TQ = 1024
HG_ROWS = 512
TM_FFN = 1024
TF_FFN = 256
MOE_TB = 512
MOE_TF = 1792
TM_ROUTER = 512
TM_COMB = 512
TM_DISP = 1024
VMEM_LIMIT = 56 * 1024 * 1024
assert MLA_WIDE == 4 and MLA_TQ % MLA_TK == 0


def _cparams(sem, **kw):
    return pltpu.CompilerParams(dimension_semantics=sem, vmem_limit_bytes=VMEM_LIMIT, **kw)


def _const_spec(shape):
    nd = len(shape)
    return pl.BlockSpec(shape, lambda *_: (0,) * nd, pipeline_mode=pl.Buffered(1))


def _layernorm(v, g, b):
    mu = jnp.mean(v, axis=-1, keepdims=True)
    vc = v - mu
    var = jnp.mean(vc * vc, axis=-1, keepdims=True)
    return vc * lax.rsqrt(var + EPS) * g + b


def _dot(a, b):
    return jnp.dot(a, b, preferred_element_type=F32)


def _dot_nt(a, b):
    return lax.dot_general(a, b, (((1,), (1,)), ((), ())), preferred_element_type=F32)


def _split3(a):
    hi = a.astype(BF16)
    r = a - hi.astype(F32)
    mid = r.astype(BF16)
    lo = (r - mid.astype(F32)).astype(BF16)
    return hi, mid, lo


def _rope_kernel(pos_ref, freq_ref, c_ref, s_ref):
    lane = lax.broadcasted_iota(I32, pos_ref.shape, 1)
    ang = pos_ref[...] * freq_ref[...]
    rope = (lane >= MLA_NOPE) & (lane < MLA_NOPE + MLA_ROPE)
    first = lane < MLA_NOPE + MLA_ROPE // 2
    c_ref[...] = jnp.where(lane < MLA_NOPE, 1.0, jnp.where(rope, jnp.cos(ang), 0.0))
    sn = jnp.sin(ang)
    s_ref[...] = jnp.where(rope, jnp.where(first, -sn, sn), 0.0)


def _rope_tables(positions):
    n = positions.size
    half = MLA_ROPE // 2
    inv_freq = ROPE_THETA ** (-jnp.arange(half, dtype=F32) / half)
    freq = jnp.zeros((1, LANES), F32).at[0, MLA_NOPE:MLA_NOPE + MLA_ROPE].set(jnp.tile(inv_freq, 2))
    posb = jnp.broadcast_to(positions.reshape(n, 1).astype(F32), (n, LANES))
    tm = 1024
    return pl.pallas_call(
        _rope_kernel,
        grid=(n // tm,),
        in_specs=[pl.BlockSpec((tm, LANES), lambda i: (i, 0)), _const_spec((1, LANES))],
        out_specs=[pl.BlockSpec((tm, LANES), lambda i: (i, 0))] * 2,
        out_shape=[jax.ShapeDtypeStruct((n, LANES), F32)] * 2,
        compiler_params=_cparams(("parallel",)),
        name="rope_tables",
    )(posb, freq)


def _inproj_kernel(x_ref, w_ref, c_ref, s_ref, qn_ref, kvn_ref, wuqa_ref, wuqb_ref, wuk_ref, wuv_ref,
                   mq_ref, mk_ref, mv_ref, swq_ref, swk_ref, swv_ref, hg_ref, sbq_ref, sbk_ref, sbv_ref):
    h = _dot(x_ref[...].astype(BF16), w_ref[...])

    def cols(name):
        lo, hi = _A_OFF[name]
        return h[:, lo:hi]

    c = c_ref[...]
    s = s_ref[...]
    c4 = jnp.concatenate([c] * N_HEADS, axis=1)
    s4 = jnp.concatenate([s] * N_HEADS, axis=1)

    cq = cols('cq')
    cqn = (cq * lax.rsqrt(jnp.mean(cq * cq, axis=-1, keepdims=True) + EPS) * qn_ref[...]).astype(BF16)
    q = _dot(cqn, wuqa_ref[...]) * c4 + _dot(cqn, wuqb_ref[...]) * s4
    mq_ref[...] = (q * (MLA_SCALE * LOG2E)).astype(BF16)

    ckv = cols('ckv')
    ckvn = (ckv * lax.rsqrt(jnp.mean(ckv * ckv, axis=-1, keepdims=True) + EPS) * kvn_ref[...]).astype(BF16)
    krot = cols('kra') * c + cols('krb') * s
    mk_ref[...] = (_dot(ckvn, wuk_ref[...]) + jnp.concatenate([krot] * N_HEADS, axis=1)).astype(BF16)
    mv_ref[...] = _dot(ckvn, wuv_ref[...]).astype(BF16)

    swq_ref[...] = cols('swa_q').astype(BF16)
    swk_ref[...] = cols('swa_k').astype(BF16)
    swv_ref[...] = cols('swa_v').astype(BF16)
    hg_ref[...] = cols('hgrn')
    sbq_ref[...] = cols('sb_q').astype(BF16)
    sbk_ref[...] = cols('sb_k').astype(BF16)
    sbv_ref[...] = cols('sb_v').astype(BF16)


def _inproj_weights(w_in, w_uq, w_ukv):
    def seg(name, width):
        o = _IN_OFF[name]
        return w_in[:, o:o + width]

    kr = seg('mla_kr', MLA_ROPE)
    half = MLA_ROPE // 2
    z64 = jnp.zeros((D_MODEL, MLA_NOPE), F32)
    z32 = jnp.zeros((D_MODEL, LANES - MLA_NOPE - MLA_ROPE), F32)
    kra = jnp.concatenate([z64, kr, z32], axis=1)
    krb = jnp.concatenate([z64, kr[:, half:], kr[:, :half], z32], axis=1)
    swq = seg('swa_q', 256).reshape(D_MODEL, N_HEADS, HEAD_DIM)[:, jnp.array([0, 2, 1, 3])].reshape(D_MODEL, WIDTH)
    w_a = jnp.concatenate([
        seg('mla_cq', 256), seg('mla_ckv', 128), kra, krb,
        swq * QK_SCALE, seg('swa_k', 128), seg('swa_v', 128),
        seg('hgrn', 1024), seg('sb_q', 256) * (QK_SCALE * LOG2E), seg('sb_k', 256), seg('sb_v', 256)], axis=1)

    qd = MLA_NOPE + MLA_ROPE
    zq = jnp.zeros((MLA_Q_LORA, LANES - qd), F32)
    zn = jnp.zeros((MLA_Q_LORA, MLA_NOPE), F32)
    qa, qb = [], []
    for hh in range(N_HEADS):
        nope = w_uq[:, hh * qd: hh * qd + MLA_NOPE]
        rope = w_uq[:, hh * qd + MLA_NOPE: (hh + 1) * qd]
        qa += [nope, rope, zq]
        qb += [zn, rope[:, half:], rope[:, :half], zq]
    wuqa = jnp.concatenate(qa, axis=1)
    wuqb = jnp.concatenate(qb, axis=1)
    lane = jnp.arange(N_HEADS * LANES) % LANES
    wuk = jnp.where(lane[None, :] < MLA_NOPE, w_ukv, 0.0)
    wuv = jnp.concatenate([w_ukv[:, hh * LANES + MLA_NOPE:(hh + 1) * LANES] for hh in range(N_HEADS)], axis=1)
    return tuple(t.astype(BF16) for t in (w_a, wuqa, wuqb, wuk, wuv))


def _inproj(x2d, wts, ctab, stab, q_norm, kv_norm):
    n = x2d.shape[0]
    w_a, wuqa, wuqb, wuk, wuv = wts
    tm = TM_INPROJ
    row = lambda w: pl.BlockSpec((tm, w), lambda i: (i, 0))
    out_w = (512, 512, 256, 256, 128, 128, 1024, 256, 256, 256)
    out_dt = (BF16, BF16, BF16, BF16, BF16, BF16, F32, BF16, BF16, BF16)
    return pl.pallas_call(
        _inproj_kernel,
        grid=(n // tm,),
        in_specs=[row(D_MODEL), _const_spec(w_a.shape), row(LANES), row(LANES),
                  _const_spec((1, MLA_Q_LORA)), _const_spec((1, MLA_KV_LORA)),
                  _const_spec(wuqa.shape), _const_spec(wuqb.shape), _const_spec(wuk.shape),
                  _const_spec(wuv.shape)],
        out_specs=[row(w) for w in out_w],
        out_shape=[jax.ShapeDtypeStruct((n, w), d) for w, d in zip(out_w, out_dt)],
        compiler_params=_cparams(("parallel",)),
        name="inproj",
    )(x2d, w_a, ctab, stab, q_norm.reshape(1, -1), kv_norm.reshape(1, -1), wuqa, wuqb, wuk, wuv)


def _half_mask(half):
    lane = lax.broadcasted_iota(I32, (1, LANES), 1)
    return (lane < HEAD_DIM) if half == 0 else (lane >= HEAD_DIM)


def _mla_kernel(q_ref, k_ref, v_ref, o_ref):
    tq = q_ref.shape[0]
    tk = MLA_TK
    nsub = tq // tk
    i = pl.program_id(1)
    row = lax.broadcasted_iota(I32, (tq, tk), 0)
    col = lax.broadcasted_iota(I32, (tq, tk), 1)
    ones = jnp.ones((1, LANES), BF16)

    def update(off, carry, heads, mask, width=tk):
        ss = [_dot_nt(q_ref[:, hh * LANES:(hh + 1) * LANES],
                      k_ref[pl.ds(off, width), hh * LANES:(hh + 1) * LANES]) for hh in heads]
        if mask is not None:
            ss = [jnp.where(mask, s, NEG_BIG) for s in ss]
        ms = [jnp.maximum(c[0], jnp.max(s, axis=-1, keepdims=True)) for c, s in zip(carry, ss)]
        pms = [jnp.exp2(s - m).astype(BF16) for s, m in zip(ss, ms)]
        new = []
        for n, hh in enumerate(heads):
            vb = v_ref[pl.ds(off, width), (hh // 2) * LANES:(hh // 2 + 1) * LANES]
            vb = jnp.where(_half_mask(hh % 2), vb, ones)
            m, acc = carry[n]
            new.append((ms[n], jnp.exp2(m - ms[n]) * acc + _dot(pms[n], vb)))
        return tuple(new)

    accs = []
    for g in range(0, N_HEADS, MLA_GROUP):
        heads = tuple(range(g, g + MLA_GROUP))
        init = tuple((jnp.full((tq, 1), NEG_BIG, F32), jnp.zeros((tq, LANES), F32)) for _ in heads)
        nkb = i * nsub
        wide = MLA_WIDE * tk
        carry = lax.fori_loop(
            0, nkb // MLA_WIDE,
            lambda j, c, heads=heads: update(pl.multiple_of(j * wide, wide), c, heads, None, wide), init)
        done = nkb // MLA_WIDE * MLA_WIDE
        rest = nkb - done
        carry = lax.cond(
            rest >= 2,
            lambda c, heads=heads: update(pl.multiple_of(done * tk, 2 * tk), c, heads, None, 2 * tk),
            lambda c: c, carry)
        carry = lax.cond(
            rest % 2 == 1,
            lambda c, heads=heads: update(pl.multiple_of((nkb - 1) * tk, tk), c, heads, None),
            lambda c: c, carry)
        for r in range(nsub):
            carry = update(pl.multiple_of(i * tq + r * tk, tk), carry, heads, col + r * tk <= row)
        accs += [c[1] for c in carry]
    outs = []
    for p in range(N_HEADS // 2):
        a0, a1 = accs[2 * p], accs[2 * p + 1]
        outs.append(jnp.where(_half_mask(0), a0 / a0[:, HEAD_DIM:HEAD_DIM + 1], a1 / a1[:, 0:1]))
    o_ref[...] = jnp.concatenate(outs, axis=1).astype(o_ref.dtype)


def _mla_attention(q, k, v, batch, seq):
    tq = MLA_TQ
    q3, k3, v3 = (t.reshape(batch, seq, t.shape[-1]) for t in (q, k, v))
    out = pl.pallas_call(
        _mla_kernel,
        grid=(batch, seq // tq),
        in_specs=[pl.BlockSpec((None, tq, 512), lambda b, i: (b, i, 0)),
                  pl.BlockSpec((None, seq, 512), lambda b, i: (b, 0, 0), pipeline_mode=pl.Buffered(1)),
                  pl.BlockSpec((None, seq, WIDTH), lambda b, i: (b, 0, 0), pipeline_mode=pl.Buffered(1))],
        out_specs=pl.BlockSpec((None, tq, WIDTH), lambda b, i: (b, i, 0)),
        out_shape=jax.ShapeDtypeStruct((batch, seq, WIDTH), BF16),
        compiler_params=_cparams(("parallel", "arbitrary")),
        name="mla_attention",
    )(q3, k3, v3)
    return out.reshape(batch * seq, WIDTH)


def _sb_kernel(q_ref, k_ref, v_ref, o_ref):
    tq = q_ref.shape[0]
    i = pl.program_id(1)
    row = lax.broadcasted_iota(I32, (tq, tq), 0)
    col = lax.broadcasted_iota(I32, (tq, tq), 1)
    strict = col < row
    later = (row > col).astype(BF16)
    qs = []
    for hh in range(N_HEADS):
        qp = q_ref[:, (hh // 2) * LANES:(hh // 2 + 1) * LANES]
        qs.append(jnp.where(_half_mask(hh % 2), qp, jnp.zeros_like(qp)))

    def block(j, carry, diag):
        off = pl.multiple_of(j * tq, tq)
        runs, accs = carry
        heads = range(N_HEADS)
        zs = [_dot_nt(qs[hh], k_ref[pl.ds(off, tq), (hh // 2) * LANES:(hh // 2 + 1) * LANES]) for hh in heads]
        lsps = [jnp.minimum(z, 0.0) - jnp.log2(1.0 + jnp.exp2(-jnp.abs(z))) for z in zs]
        lsns = [lsp - z for lsp, z in zip(lsps, zs)]
        if diag:
            lsns = [jnp.where(strict, t, 0.0) for t in lsns]
        his = [t.astype(BF16) for t in lsns]
        los = [(t - hi.astype(F32)).astype(BF16) for t, hi in zip(lsns, his)]
        rems = [_dot(hi, later) + _dot(lo, later) for hi, lo in zip(his, los)]
        args = [lsps[hh] + rems[hh] + runs[hh] for hh in heads]
        if diag:
            args = [jnp.where(strict, t, NEG_BIG) for t in args]
        probs = [jnp.exp2(t).astype(BF16) for t in args]
        new_runs = tuple(runs[hh] + rems[hh][:, 0:1] + lsns[hh][:, 0:1] for hh in heads)
        new_accs = list(accs)
        for hh in heads:
            p = hh // 2
            vb = v_ref[pl.ds(off, tq), p * LANES:(p + 1) * LANES]
            vb = jnp.where(_half_mask(hh % 2), vb, jnp.zeros_like(vb))
            new_accs[p] = new_accs[p] + _dot(probs[hh], vb)
        return new_runs, tuple(new_accs)

    init = (tuple(jnp.zeros((tq, 1), F32) for _ in range(N_HEADS)),
            tuple(jnp.zeros((tq, LANES), F32) for _ in range(N_HEADS // 2)))
    def still_active(runs):
        top = functools.reduce(jnp.maximum, runs)
        return (jnp.max(top) > SB_RUN_FLOOR).astype(I32)

    runs, accs = block(i, init, True)

    def cond(c):
        return (c[0] < i) & (c[1] > 0)

    def body(c):
        jj, _, runs, accs = c
        runs, accs = block(i - 1 - jj, (runs, accs), False)
        return jj + 1, still_active(runs), runs, accs

    _, _, _, accs = lax.while_loop(cond, body, (jnp.int32(0), still_active(runs), runs, accs))
    o_ref[...] = jnp.concatenate(accs, axis=1).astype(o_ref.dtype)


def _sb_attention(q, k, v, batch, seq):
    tq = TQ_ATT
    q3, k3, v3 = (t.reshape(batch, seq, WIDTH) for t in (q, k, v))
    out = pl.pallas_call(
        _sb_kernel,
        grid=(batch, seq // tq),
        in_specs=[pl.BlockSpec((None, tq, WIDTH), lambda b, i: (b, i, 0)),
                  pl.BlockSpec((None, seq, WIDTH), lambda b, i: (b, 0, 0)),
                  pl.BlockSpec((None, seq, WIDTH), lambda b, i: (b, 0, 0))],
        out_specs=pl.BlockSpec((None, tq, WIDTH), lambda b, i: (b, i, 0)),
        out_shape=jax.ShapeDtypeStruct((batch, seq, WIDTH), BF16),
        compiler_params=_cparams(("parallel", "arbitrary")),
        name="stick_breaking",
    )(q3, k3, v3)
    return out.reshape(batch * seq, WIDTH)


def _rel_bucket(dist):
    exact = REL_BUCKETS // 2
    n = jnp.maximum(dist, 0)
    nf = jnp.maximum(n, 1).astype(F32)
    large = exact + (jnp.log(nf / exact) / math.log(REL_MAX_DIST / exact) * (REL_BUCKETS - exact)).astype(I32)
    large = jnp.clip(large, 0, REL_BUCKETS - 1)
    return jnp.where(n < exact, n, large)


def _swa_kernel(sink_ref, tab_ref, q_ref, kc_ref, kh_ref, vc_ref, vh_ref, pq_ref, pkc_ref, pkh_ref, o_ref):
    w = SWA_WINDOW
    step = pl.program_id(1)
    row = lax.broadcasted_iota(I32, (w, w), 0)
    col = lax.broadcasted_iota(I32, (w, w), 1)
    valid_c = col <= row
    valid_p = col > row
    tabs = [jnp.broadcast_to(tab_ref[hh:hh + 1, :], (w, LANES)) for hh in range(N_HEADS)]
    ones = jnp.ones((1, LANES), BF16)
    nsub = q_ref.shape[0] // w
    chains = [(r, hh) for r in range(nsub) for hh in range(N_HEADS)]

    def real(hh):
        return (hh % 2) * 2 + hh // 2

    def keys(ref, halo_ref, r):
        cur = ref[r * w:(r + 1) * w, :]
        prev = ref[(r - 1) * w:r * w, :] if r else halo_ref[...]
        return cur, prev

    buckets = []
    for r in range(nsub):
        pq = pq_ref[r * w:(r + 1) * w, :]
        pk_prev = pkc_ref[:, (r - 1) * w:r * w] if r else pkh_ref[...]
        buckets.append((_rel_bucket(pq - pkc_ref[:, r * w:(r + 1) * w]), _rel_bucket(pq - pk_prev)))
    logits = []
    for r, hh in chains:
        qp = q_ref[r * w:(r + 1) * w, (hh // 2) * LANES:(hh // 2 + 1) * LANES]
        qh = jnp.where(_half_mask(hh % 2), qp, jnp.zeros_like(qp))
        kc, kp = keys(kc_ref, kh_ref, r)
        logits.append((_dot_nt(qh, kc), _dot_nt(qh, kp)))
    masked = []
    for (r, hh), (lc, lp) in zip(chains, logits):
        lc = jnp.where(valid_c, lc + jnp.take_along_axis(tabs[real(hh)], buckets[r][0], axis=1), NEG_BIG)
        lp = lp + jnp.take_along_axis(tabs[real(hh)], buckets[r][1], axis=1)
        lp = jnp.where(valid_p if r else valid_p & (step > 0), lp, NEG_BIG)
        masked.append((lc, lp))
    maxes = [jnp.maximum(jnp.maximum(jnp.max(lc, axis=-1, keepdims=True), jnp.max(lp, axis=-1, keepdims=True)),
                         sink_ref[real(hh)]) for (r, hh), (lc, lp) in zip(chains, masked)]
    probs = [(jnp.exp(lc - m).astype(BF16), jnp.exp(lp - m).astype(BF16)) for (lc, lp), m in zip(masked, maxes)]
    outs = {}
    for (r, hh), (ec, ep), m in zip(chains, probs, maxes):
        vc, vp = keys(vc_ref, vh_ref, r)
        mine = _half_mask(hh % 2)
        acc = _dot(ec, jnp.where(mine, vc, ones)) + _dot(ep, jnp.where(mine, vp, ones))
        den = (acc[:, 0:1] if hh % 2 else acc[:, HEAD_DIM:HEAD_DIM + 1]) + jnp.exp(sink_ref[real(hh)] - m)
        outs[(r, hh)] = acc / den
    for r in range(nsub):
        pairs = [jnp.where(_half_mask(0), outs[(r, 2 * p)], outs[(r, 2 * p + 1)]) for p in range(N_HEADS // 2)]
        o_ref[r * w:(r + 1) * w, :] = jnp.concatenate(pairs, axis=1).astype(o_ref.dtype)


def _swa_attention(q, k, v, positions, sinks, rel_table, batch, seq):
    w = SWA_WINDOW
    tq = SWA_TQ
    per = tq // w
    kvw = k.shape[-1]
    q3, k3, v3 = (t.reshape(batch, seq, t.shape[-1]) for t in (q, k, v))
    pcol = positions.reshape(batch, seq, 1)
    prow = positions.reshape(batch, 1, seq)
    tab = jnp.zeros((N_HEADS, LANES), F32).at[:, :REL_BUCKETS].set(rel_table.astype(F32).T)
    cur = lambda b, n: (b, n, 0)
    halo = lambda b, n: (b, jnp.maximum(n * per - 1, 0), 0)
    out = pl.pallas_call(
        _swa_kernel,
        grid=(batch, seq // tq),
        in_specs=[pl.BlockSpec(memory_space=pltpu.SMEM), _const_spec((N_HEADS, LANES)),
                  pl.BlockSpec((None, tq, WIDTH), cur),
                  pl.BlockSpec((None, tq, kvw), cur), pl.BlockSpec((None, w, kvw), halo),
                  pl.BlockSpec((None, tq, kvw), cur), pl.BlockSpec((None, w, kvw), halo),
                  pl.BlockSpec((None, tq, 1), cur),
                  pl.BlockSpec((None, 1, tq), lambda b, n: (b, 0, n)),
                  pl.BlockSpec((None, 1, w), lambda b, n: (b, 0, jnp.maximum(n * per - 1, 0)))],
        out_specs=pl.BlockSpec((None, tq, WIDTH), cur),
        out_shape=jax.ShapeDtypeStruct((batch, seq, WIDTH), BF16),
        compiler_params=_cparams(("parallel", "arbitrary")),
        name="swa_attention",
    )(sinks.astype(F32), tab, q3, k3, k3, v3, v3, pcol, prow, prow)
    return out.reshape(batch * seq, WIDTH)


def _hgrn_kernel(hg_ref, lb_ref, nw_ref, o_ref, state_ref):
    c = HGRN_CHUNK
    blk = HGRN_BLOCK

    @pl.when(pl.program_id(1) == 0)
    def _():
        state_ref[...] = jnp.zeros_like(state_ref)

    r64 = lax.broadcasted_iota(I32, (c, c), 0)
    c64 = lax.broadcasted_iota(I32, (c, c), 1)
    incl = (c64 <= r64).astype(BF16)
    ra = lax.broadcasted_iota(I32, (WIDTH, WIDTH), 0) // HEAD_DIM
    ca = lax.broadcasted_iota(I32, (WIDTH, WIDTH), 1) // HEAD_DIM
    same_head = ra == ca
    seg = same_head.astype(BF16)
    ones_cols = jnp.ones((c, LANES), BF16)
    trow = lax.broadcasted_iota(I32, (blk, WIDTH), 0)
    caps = [jnp.where(trow >= s_i, 0.0, NEG_BIG) for s_i in range(blk)]
    lane_head = lax.broadcasted_iota(I32, (1, WIDTH), 1) // HEAD_DIM
    lb = lb_ref[...]
    nw = nw_ref[...]
    dn0 = (((0,), (0,)), ((), ()))

    for ch in range(hg_ref.shape[0] // c):
        rows = slice(ch * c, (ch + 1) * c)
        qraw = hg_ref[rows, 0:WIDTH]
        fraw = hg_ref[rows, WIDTH:2 * WIDTH]
        v = hg_ref[rows, 2 * WIDTH:3 * WIDTH]
        graw = hg_ref[rows, 3 * WIDTH:4 * WIDTH]
        qf = qraw * jax.nn.sigmoid(qraw)
        forget = lb + (1.0 - lb) * jax.nn.sigmoid(fraw)
        lf = jnp.log(forget)
        kk = 1.0 - forget
        gate = graw * jax.nn.sigmoid(graw)
        vb = v.astype(BF16)

        lf3 = _split3(lf)
        bc = _dot(incl, lf3[0]) + _dot(incl, lf3[1]) + _dot(incl, lf3[2])
        b_last = bc[c - 1:c, :]
        tot_col = sum(lax.dot_general(t, ones_cols, dn0, preferred_element_type=F32) for t in lf3)
        decay_col = jnp.exp(jnp.concatenate([tot_col, tot_col], axis=1))

        state = state_ref[...]
        o_inter = _dot((qf * jnp.exp(bc)).astype(BF16), state.astype(BF16))

        def before(qa, qb, ka, kb):
            ref = bc[kb - 1:kb, :]
            qt = qf[qa:qb] * jnp.exp(bc[qa:qb] - ref)
            kt = (kk[ka:kb] * jnp.exp(ref - bc[ka:kb])).astype(BF16)
            qs = jnp.concatenate([jnp.where(lane_head == hh, qt, 0.0) for hh in range(N_HEADS)], axis=0)
            att = _dot_nt(qs.astype(BF16), kt)
            mix = _dot(att.astype(BF16), vb[ka:kb])
            nq = qb - qa
            return sum(jnp.where(lane_head == hh, mix[hh * nq:(hh + 1) * nq], 0.0) for hh in range(N_HEADS))

        bc2 = bc * LOG2E
        key2 = bc2 - jnp.log2(jnp.maximum(kk, 0.0))

        def inside(a):
            b2 = bc2[a:a + blk]
            qb_ = qf[a:a + blk]
            ws = []
            for s_i in range(blk):
                ws.append(qb_ * jnp.exp2(jnp.minimum(b2 - key2[a + s_i:a + s_i + 1, :], caps[s_i])))
            att = _dot(jnp.concatenate(ws, axis=0).astype(BF16), seg)
            return sum(att[s_i * blk:(s_i + 1) * blk] * v[a + s_i:a + s_i + 1, :] for s_i in range(blk))

        pieces = {a: [] for a in range(0, c, blk)}

        def cover(a, b):
            if b - a == blk:
                pieces[a].append(inside(a))
                return
            mid = (a + b) // 2
            cover(a, mid)
            cover(mid, b)
            res = before(mid, b, a, mid)
            for off in range(0, b - mid, blk):
                pieces[mid + off].append(res[off:off + blk])

        cover(0, c)
        o = o_inter + jnp.concatenate([sum(pieces[a]) for a in range(0, c, blk)], axis=0)

        khat = (kk * jnp.exp(b_last - bc)).astype(BF16)
        upd = lax.dot_general(khat, vb, dn0, preferred_element_type=F32)
        state_ref[...] = decay_col * state + jnp.where(same_head, upd, 0.0)

        o2 = _split3(o * o)
        ms = (_dot(o2[0], seg) + _dot(o2[1], seg)) * (1.0 / HEAD_DIM)
        o_ref[rows, :] = (o * lax.rsqrt(ms + EPS) * nw * gate).astype(o_ref.dtype)


def _hgrn(hg, lower_bound, norm_w, batch, seq):
    rows = HG_ROWS
    hg3 = hg.reshape(batch, seq, 4 * WIDTH)
    out = pl.pallas_call(
        _hgrn_kernel,
        grid=(batch, seq // rows),
        in_specs=[pl.BlockSpec((None, rows, 4 * WIDTH), lambda b, i: (b, i, 0)),
                  _const_spec((1, WIDTH)), _const_spec((1, WIDTH))],
        out_specs=pl.BlockSpec((None, rows, WIDTH), lambda b, i: (b, i, 0)),
        out_shape=jax.ShapeDtypeStruct((batch, seq, WIDTH), BF16),
        scratch_shapes=[pltpu.VMEM((WIDTH, WIDTH), F32)],
        compiler_params=_cparams(("parallel", "arbitrary")),
        name="hgrn2",
    )(hg3, lower_bound.reshape(1, WIDTH).astype(F32), norm_w.reshape(1, WIDTH).astype(F32))
    return out.reshape(batch * seq, WIDTH)


def _merge_body(x, y_refs, wg_ref, wb_ref, wo_ref, g, b):
    xb = x.astype(BF16)
    ys = [y_ref[...] for y_ref in y_refs]
    cols = []
    for lo in range(0, D_MODEL, TN_MERGE):
        part = None
        for nbr, yb in enumerate(ys):
            gate = jax.nn.sigmoid(_dot(xb, wg_ref[:, nbr * D_MODEL + lo:nbr * D_MODEL + lo + TN_MERGE]))
            term = gate * _dot(yb, wb_ref[nbr, :, lo:lo + TN_MERGE])
            part = term if part is None else part + term
        cols.append(part.astype(BF16))
    y = _dot(jnp.concatenate(cols, axis=1), wo_ref[...])
    return _layernorm(ALPHA * x + y, g, b)


def _memkv_kernel(m_ref, w_ref, k_ref, v_ref):
    kv = _dot(m_ref[...].astype(BF16), w_ref[...])
    k_ref[...] = kv[:, :WIDTH].astype(BF16)
    v_ref[...] = kv[:, WIDTH:].astype(BF16)


def _memkv(mem, wkv):
    batch, m, _ = mem.shape
    return pl.pallas_call(
        _memkv_kernel,
        grid=(batch,),
        in_specs=[pl.BlockSpec((None, m, D_MODEL), lambda b: (b, 0, 0)), _const_spec(wkv.shape)],
        out_specs=[pl.BlockSpec((None, m, WIDTH), lambda b: (b, 0, 0))] * 2,
        out_shape=[jax.ShapeDtypeStruct((batch, m, WIDTH), BF16)] * 2,
        compiler_params=_cparams(("parallel",)),
        name="mem_kv",
    )(mem, wkv)


def _xattn_body(x, wq_ref, k, v, wo_ref, g, b):
    q = _dot(x.astype(BF16), wq_ref[...]).astype(BF16)
    lane = lax.broadcasted_iota(I32, (1, WIDTH), 1) // HEAD_DIM
    heads = range(N_HEADS)
    ss = [_dot_nt(jnp.where(lane == hh, q, jnp.zeros_like(q)), k) for hh in heads]
    es = [jnp.exp(s - jnp.max(s, axis=-1, keepdims=True)) for s in ss]
    ps = [(e / jnp.sum(e, axis=-1, keepdims=True)).astype(BF16) for e in es]
    o = jnp.zeros((x.shape[0], WIDTH), F32)
    for hh in heads:
        o = o + jnp.where(lane == hh, _dot(ps[hh], v), 0.0)
    y = _dot(o.astype(BF16), wo_ref[...])
    return _layernorm(ALPHA * x + y, g, b)


def _merge_xattn_kernel(x_ref, y0_ref, y1_ref, y2_ref, y3_ref, wg_ref, wb_ref, wo_ref, g1_ref, b1_ref,
                        wq_ref, k_ref, v_ref, xwo_ref, g2_ref, b2_ref, o_ref):
    x1 = _merge_body(x_ref[...], (y0_ref, y1_ref, y2_ref, y3_ref), wg_ref, wb_ref, wo_ref,
                     g1_ref[...], b1_ref[...])
    o_ref[...] = _xattn_body(x1, wq_ref, k_ref[...], v_ref[...], xwo_ref, g2_ref[...], b2_ref[...])


def _merge_xattn(x2d, ys, wg, wb, wo, g1, b1, wq, k, v, xwo, g2, b2, batch, seq):
    tm = TM_A
    m = k.shape[1]
    per = seq // tm
    row = lambda w: pl.BlockSpec((tm, w), lambda bb, i: (bb * per + i, 0))
    kv_spec = pl.BlockSpec((None, m, WIDTH), lambda bb, i: (bb, 0, 0))
    vec = _const_spec((1, D_MODEL))
    return pl.pallas_call(
        _merge_xattn_kernel,
        grid=(batch, per),
        in_specs=[row(D_MODEL)] + [row(WIDTH)] * 4 +
                 [_const_spec(wg.shape), _const_spec(wb.shape), _const_spec(wo.shape), vec, vec,
                  _const_spec(wq.shape), kv_spec, kv_spec, _const_spec(xwo.shape), vec, vec],
        out_specs=row(D_MODEL),
        out_shape=jax.ShapeDtypeStruct((batch * seq, D_MODEL), F32),
        compiler_params=_cparams(("parallel", "parallel")),
        name="merge_xattn_ln",
    )(x2d, *ys, wg, wb, wo, g1.reshape(1, -1), b1.reshape(1, -1),
      wq, k, v, xwo, g2.reshape(1, -1), b2.reshape(1, -1))


def _ffn_kernel(x_ref, w13_ref, w2_ref, g_ref, b_ref, o_ref):
    x = x_ref[...]
    xb = x.astype(BF16)
    y = None
    for h in range(F_DENSE // TF_FFN):
        lo = h * TF_FFN
        a = _dot(xb, w13_ref[:, lo:lo + TF_FFN])
        gate = _dot(xb, w13_ref[:, F_DENSE + lo:F_DENSE + lo + TF_FFN])
        part = _dot((a * jax.nn.sigmoid(a) * gate).astype(BF16), w2_ref[lo:lo + TF_FFN, :])
        y = part if y is None else y + part
    o_ref[...] = _layernorm(ALPHA * x + y, g_ref[...], b_ref[...])


def _ffn(x2d, w13, w2, g, b):
    n = x2d.shape[0]
    tm = TM_FFN
    return pl.pallas_call(
        _ffn_kernel,
        grid=(n // tm,),
        in_specs=[pl.BlockSpec((tm, D_MODEL), lambda i: (i, 0)),
                  _const_spec(w13.shape), _const_spec(w2.shape),
                  _const_spec((1, D_MODEL)), _const_spec((1, D_MODEL))],
        out_specs=pl.BlockSpec((tm, D_MODEL), lambda i: (i, 0)),
        out_shape=jax.ShapeDtypeStruct((n, D_MODEL), F32),
        compiler_params=_cparams(("parallel",)),
        name="ffn_ln",
    )(x2d, w13, w2, g.reshape(1, -1), b.reshape(1, -1))


def _router_kernel(x_ref, r_ref, info_ref, wts_ref, cnt_ref, carry_ref):
    tm = x_ref.shape[0]

    @pl.when(pl.program_id(0) == 0)
    def _():
        carry_ref[...] = jnp.zeros_like(carry_ref)

    logits = jnp.dot(x_ref[...], r_ref[...], precision=lax.Precision.HIGHEST, preferred_element_type=F32)
    lane = lax.broadcasted_iota(I32, (tm, LANES), 1)
    lg = jnp.where(lane < N_EXPERTS, logits, -jnp.inf)
    m1 = jnp.max(lg, axis=-1, keepdims=True)
    i1 = jnp.min(jnp.where(lg == m1, lane, LANES), axis=-1, keepdims=True)
    lg2 = jnp.where(lane == i1, -jnp.inf, lg)
    m2 = jnp.max(lg2, axis=-1, keepdims=True)
    i2 = jnp.min(jnp.where(lg2 == m2, lane, LANES), axis=-1, keepdims=True)
    e = jnp.exp(m2 - m1)
    w1 = 1.0 / (1.0 + e)
    w2 = e / (1.0 + e)
    sel1 = lane == i1
    sel2 = lane == i2
    chosen = jnp.where(sel1 | sel2, 1.0, 0.0)
    row = lax.broadcasted_iota(I32, (tm, tm), 0)
    col = lax.broadcasted_iota(I32, (tm, tm), 1)
    before = (col < row).astype(BF16)
    ranks = _dot(before, chosen.astype(BF16)) + carry_ref[...]
    r1 = jnp.sum(jnp.where(sel1, ranks, 0.0), axis=-1, keepdims=True)
    r2 = jnp.sum(jnp.where(sel2, ranks, 0.0), axis=-1, keepdims=True)
    carry_ref[...] = carry_ref[...] + jnp.sum(chosen, axis=0, keepdims=True)
    info = jnp.where(lane == 0, i1.astype(F32), jnp.where(lane == 1, i2.astype(F32),
                     jnp.where(lane == 2, r1, jnp.where(lane == 3, r2, 0.0))))
    info_ref[...] = jnp.transpose(info)[:info_ref.shape[0], :]
    wts_ref[...] = jnp.where(lane == 0, w1, jnp.where(lane == 1, w2, 0.0))
    cnt_ref[...] = carry_ref[...]


def _router(x2d, router):
    n = x2d.shape[0]
    tm = TM_ROUTER
    r_pad = jnp.zeros((D_MODEL, LANES), F32).at[:, :N_EXPERTS].set(router.astype(F32))
    row = pl.BlockSpec((tm, LANES), lambda i: (i, 0))
    return pl.pallas_call(
        _router_kernel,
        grid=(n // tm,),
        in_specs=[pl.BlockSpec((tm, D_MODEL), lambda i: (i, 0)), _const_spec(r_pad.shape)],
        out_specs=[pl.BlockSpec((8, tm), lambda i: (0, i)), row, pl.BlockSpec((1, LANES), lambda i: (0, 0))],
        out_shape=[jax.ShapeDtypeStruct((8, n), F32), jax.ShapeDtypeStruct((n, LANES), F32),
                   jax.ShapeDtypeStruct((1, LANES), F32)],
        scratch_shapes=[pltpu.VMEM((1, LANES), F32)],
        compiler_params=_cparams(("arbitrary",)),
        name="moe_router",
    )(x2d, r_pad)


def _dispatch_kernel(pad_ref, d0_ref, d1_ref, x_ref, xb_hbm, stage_ref, sems):
    tm = x_ref.shape[0]
    i = pl.program_id(0)
    last = pl.num_programs(0) - 1
    slot = i % 2

    def wait_step(s):
        for _ in range(2):
            pltpu.make_async_copy(stage_ref.at[s], xb_hbm.at[pl.ds(0, tm), :], sems.at[s]).wait()

    @pl.when(i >= 2)
    def _():
        wait_step(slot)

    for s in range(2):
        @pl.when(slot == s)
        def _():
            stage_ref[s] = x_ref[...]

            def issue(r, c):
                for k in range(2):
                    pltpu.make_async_copy(stage_ref.at[s, pl.ds(r, 1), :],
                                          xb_hbm.at[pl.ds((d0_ref, d1_ref)[k][0, r], 1), :], sems.at[s]).start()
                return c
            lax.fori_loop(0, tm, issue, 0, unroll=8)

    @pl.when(i == last)
    def _():
        def fill(e, c):
            def one(s, c2):
                pltpu.make_async_copy(stage_ref.at[slot, pl.ds(0, 1), :], xb_hbm.at[pl.ds(s, 1), :],
                                      sems.at[2]).start()
                return c2

            def done(s, c2):
                pltpu.make_async_copy(stage_ref.at[slot, pl.ds(0, 1), :], xb_hbm.at[pl.ds(0, 1), :],
                                      sems.at[2]).wait()
                return c2
            lax.fori_loop(pad_ref[0, e], pad_ref[1, e], one, 0)
            lax.fori_loop(pad_ref[0, e], pad_ref[1, e], done, 0)
            return c
        lax.fori_loop(0, pad_ref.shape[1], fill, 0)
        wait_step(slot)

        @pl.when(last >= 1)
        def _():
            wait_step(1 - slot)


def _dispatch(x2d, dest, pads, nblk):
    n = x2d.shape[0]
    tm = TM_DISP
    nt = n // tm
    grid_spec = pltpu.PrefetchScalarGridSpec(
        num_scalar_prefetch=1,
        grid=(nt,),
        in_specs=[pl.BlockSpec((None, 1, tm), lambda i, pads: (i, 0, 0), memory_space=pltpu.SMEM),
                  pl.BlockSpec((None, 1, tm), lambda i, pads: (i, 0, 0), memory_space=pltpu.SMEM),
                  pl.BlockSpec((tm, D_MODEL), lambda i, pads: (i, 0))],
        out_specs=pl.BlockSpec(memory_space=pl.ANY),
        scratch_shapes=[pltpu.VMEM((2, tm, D_MODEL), F32), pltpu.SemaphoreType.DMA((3,))],
    )
    return pl.pallas_call(
        _dispatch_kernel,
        grid_spec=grid_spec,
        out_shape=jax.ShapeDtypeStruct((nblk * MOE_TB, D_MODEL), F32),
        compiler_params=_cparams(("arbitrary",), disable_bounds_checks=True),
        name="moe_dispatch",
    )(pads, dest[0].reshape(nt, 1, tm), dest[1].reshape(nt, 1, tm), x2d)


def _expert_kernel(nused_ref, bexp_ref, x_ref, w1_ref, w3_ref, w2_ref, o_ref, acc_ref):
    f = pl.program_id(1)

    @pl.when(pl.program_id(0) < nused_ref[0])
    def _():
        xb = x_ref[...].astype(BF16)
        a = _dot(xb, w1_ref[...])
        gate = _dot(xb, w3_ref[...])
        part = _dot((a * jax.nn.sigmoid(a) * gate).astype(BF16), w2_ref[...])

        @pl.when(f == 0)
        def _():
            acc_ref[...] = part

        @pl.when(f > 0)
        def _():
            acc_ref[...] += part

        @pl.when(f == pl.num_programs(1) - 1)
        def _():
            o_ref[...] = acc_ref[...]

    @pl.when(pl.program_id(0) >= nused_ref[0])
    def _():
        o_ref[...] = jnp.zeros_like(o_ref)


def _experts(xb, w13, w2, nused, blk_exp, nblk):
    tb, tf = MOE_TB, MOE_TF
    nf = F_EXPERT // tf
    w13 = w13.astype(BF16)

    def blk(i, nu):
        return jnp.maximum(jnp.minimum(i, nu[0] - 1), 0)

    def ftile(i, f, nu):
        return jnp.where(i < nu[0], f, nf - 1)

    grid_spec = pltpu.PrefetchScalarGridSpec(
        num_scalar_prefetch=2,
        grid=(nblk, nf),
        in_specs=[pl.BlockSpec((tb, D_MODEL), lambda i, f, nu, be: (blk(i, nu), 0)),
                  pl.BlockSpec((None, D_MODEL, tf), lambda i, f, nu, be: (be[blk(i, nu)], 0, ftile(i, f, nu))),
                  pl.BlockSpec((None, D_MODEL, tf), lambda i, f, nu, be: (be[blk(i, nu)], 0, nf + ftile(i, f, nu))),
                  pl.BlockSpec((None, tf, D_MODEL), lambda i, f, nu, be: (be[blk(i, nu)], ftile(i, f, nu), 0))],
        out_specs=pl.BlockSpec((tb, D_MODEL), lambda i, f, nu, be: (i, 0)),
        scratch_shapes=[pltpu.VMEM((tb, D_MODEL), F32)],
    )
    return pl.pallas_call(
        _expert_kernel,
        grid_spec=grid_spec,
        out_shape=jax.ShapeDtypeStruct((nblk * tb, D_MODEL), F32),
        compiler_params=_cparams(("arbitrary", "arbitrary")),
        name="moe_experts",
    )(nused, blk_exp, xb, w13, w13, w2.astype(BF16))


def _combine_kernel(d0_ref, d1_ref, n0_ref, n1_ref, y_hbm, x_ref, wts_ref, g_ref, b_ref, o_ref, buf_ref, sems):
    tm = x_ref.shape[0]
    i = pl.program_id(0)
    slot = i % 2

    def gather(idx_refs, s):
        def issue(r, c):
            for k in range(2):
                pltpu.make_async_copy(y_hbm.at[pl.ds(idx_refs[k][0, r], 1), :],
                                      buf_ref.at[s, k, pl.ds(r, 1), :], sems.at[s]).start()
            return c
        lax.fori_loop(0, tm, issue, 0, unroll=8)

    @pl.when(i == 0)
    def _():
        gather((d0_ref, d1_ref), 0)

    for s in range(2):
        @pl.when((i + 1 < pl.num_programs(0)) & (slot != s))
        def _():
            gather((n0_ref, n1_ref), s)

    for k in range(2):
        pltpu.make_async_copy(y_hbm.at[pl.ds(0, tm), :], buf_ref.at[slot, k], sems.at[slot]).wait()
    wts = wts_ref[...]
    y = wts[:, 0:1] * buf_ref[slot, 0] + wts[:, 1:2] * buf_ref[slot, 1]
    o_ref[...] = _layernorm(ALPHA * x_ref[...] + y, g_ref[...], b_ref[...])


def _combine(yb, dest, x2d, wts, g, b):
    n = x2d.shape[0]
    tm = TM_COMB
    nt = n // tm
    row = lambda w: pl.BlockSpec((tm, w), lambda i: (i, 0))
    cur = pl.BlockSpec((None, 1, tm), lambda i: (i, 0, 0), memory_space=pltpu.SMEM)
    nxt = pl.BlockSpec((None, 1, tm), lambda i: (jnp.minimum(i + 1, nt - 1), 0, 0), memory_space=pltpu.SMEM)
    d0, d1 = (d.reshape(nt, 1, tm) for d in dest)
    return pl.pallas_call(
        _combine_kernel,
        grid=(nt,),
        in_specs=[cur, cur, nxt, nxt, pl.BlockSpec(memory_space=pl.ANY), row(D_MODEL), row(LANES),
                  _const_spec((1, D_MODEL)), _const_spec((1, D_MODEL))],
        out_specs=row(D_MODEL),
        out_shape=jax.ShapeDtypeStruct((n, D_MODEL), F32),
        scratch_shapes=[pltpu.VMEM((2, 2, tm, D_MODEL), F32), pltpu.SemaphoreType.DMA((2,))],
        compiler_params=_cparams(("arbitrary",), disable_bounds_checks=True),
        name="moe_combine_ln",
    )(d0, d1, d0, d1, yb, x2d, wts, g.reshape(1, -1), b.reshape(1, -1))


def _moe(x2d, router, w13, w2, g, b):
    n = x2d.shape[0]
    tb = MOE_TB
    info, wts, cnt = _router(x2d, router)
    counts = cnt[0, :N_EXPERTS].astype(I32)
    padded = (counts + tb - 1) // tb * tb
    pend = jnp.cumsum(padded)
    pstart = pend - padded
    info = info.astype(I32)
    dest = tuple((sum(jnp.where(info[k] == e, pstart[e], 0) for e in range(N_EXPERTS)) + info[2 + k]).astype(I32)
                 for k in range(2))
    nblk = -(-(2 * n + N_EXPERTS * (tb - 1)) // tb)
    pads = jnp.stack([jnp.append(pstart + counts, pend[-1]), jnp.append(pend, nblk * tb)]).astype(I32)
    nused = (pend[-1] // tb).astype(I32).reshape(1)
    first_row = jnp.arange(nblk, dtype=I32) * tb
    blk_exp = jnp.minimum(jnp.sum(pend[None, :] <= first_row[:, None], axis=1), N_EXPERTS - 1).astype(I32)
    xb = _dispatch(x2d, dest, pads, nblk)
    yb = _experts(xb, w13, w2, nused, blk_exp, nblk)
    return _combine(yb, dest, x2d, wts, g, b)


def kernel(x, mem, positions, rel_bias_table, hgrn_lb_logits, w_in, mla_q_norm, mla_w_uq, mla_kv_norm, mla_w_ukv, swa_sinks, hgrn_norm, w_branch, w_out, ln_g, ln_b, xa_wq, xa_wkv, xa_wo, ffn_w13, ffn_w2, moe_router, moe_w13, moe_w2):
    batch, seq, _ = x.shape
    n = batch * seq
    sm = jax.nn.softmax(hgrn_lb_logits.astype(F32), axis=0)
    lower_bounds = jnp.cumsum(sm, axis=0) - sm[0]
    ctab, stab = _rope_tables(positions)
    xc = x.reshape(n, D_MODEL)
    for l in range(DEPTH):
        wts = _inproj_weights(w_in[l], mla_w_uq[l], mla_w_ukv[l])
        mq, mk, mv, swq, swk, swv, hg, sbq, sbk, sbv = _inproj(xc, wts, ctab, stab, mla_q_norm[l], mla_kv_norm[l])
        y_mla = _mla_attention(mq, mk, mv, batch, seq)
        y_swa = _swa_attention(swq, swk, swv, positions, swa_sinks[l], rel_bias_table, batch, seq)
        y_hg = _hgrn(hg, lower_bounds[l], hgrn_norm[l], batch, seq)
        y_sb = _sb_attention(sbq, sbk, sbv, batch, seq)
        go = _IN_OFF['gates']
        mk_, mv_ = _memkv(mem, xa_wkv[l].astype(BF16))
        wb = w_branch[l].at[1].set(
            w_branch[l][1].reshape(N_HEADS, HEAD_DIM, D_MODEL)[jnp.array([0, 2, 1, 3])].reshape(WIDTH, D_MODEL))
        xc = _merge_xattn(xc, (y_mla, y_swa, y_hg, y_sb), w_in[l][:, go:].astype(BF16), wb.astype(BF16),
                          w_out[l].astype(BF16), ln_g[l, 0], ln_b[l, 0],
                          (xa_wq[l] * QK_SCALE).astype(BF16), mk_, mv_, xa_wo[l].astype(BF16),
                          ln_g[l, 1], ln_b[l, 1], batch, seq)
        if l % 2 == 0:
            xc = _ffn(xc, ffn_w13[l // 2].astype(BF16), ffn_w2[l // 2].astype(BF16), ln_g[l, 2], ln_b[l, 2])
        else:
            xc = _moe(xc, moe_router[l // 2], moe_w13[l // 2], moe_w2[l // 2],
                      ln_g[l, 2], ln_b[l, 2])
    return xc.reshape(batch, seq, D_MODEL)
```

```python
import functools
import math

import jax
import jax.numpy as jnp
from jax import lax
from jax.experimental import pallas as pl
from jax.experimental.pallas import tpu as pltpu

F32 = jnp.float32
BF16 = jnp.bfloat16
I32 = jnp.int32

D_MODEL = 1024
DEPTH = 2
EPS = 1e-5
NEG_BIG = -1e30
LANES = 128
HEAD_DIM = 64
N_HEADS = 4
WIDTH = N_HEADS * HEAD_DIM

MLA_Q_LORA = 256
MLA_KV_LORA = 128
MLA_NOPE = 64
MLA_ROPE = 32
ROPE_THETA = 10000.0
MLA_SCALE = (MLA_NOPE + MLA_ROPE) ** -0.5
LOG2E = math.log2(math.e)
QK_SCALE = HEAD_DIM ** -0.5

SB_RUN_FLOOR = -150.0
SWA_WINDOW = 128
REL_BUCKETS = 32
REL_MAX_DIST = 128
HGRN_CHUNK = 64
HGRN_BLOCK = 16
N_EXPERTS = 8
F_DENSE = 2816
F_EXPERT = 3584
ALPHA = (2 * DEPTH) ** 0.25

_IN_SPLITS = (('mla_cq', 256), ('mla_ckv', 128), ('mla_kr', 32), ('swa_q', 256), ('swa_k', 128),
              ('swa_v', 128), ('hgrn', 1024), ('sb_q', 256), ('sb_k', 256), ('sb_v', 256), ('gates', 4096))
_IN_OFF = {}
_o = 0
for _n, _w in _IN_SPLITS:
    _IN_OFF[_n] = _o
    _o += _w

_A_SPLITS = (('cq', 256), ('ckv', 128), ('kra', 128), ('krb', 128), ('swa_q', 256), ('swa_k', 128),
             ('swa_v', 128), ('hgrn', 1024), ('sb_q', 256), ('sb_k', 256), ('sb_v', 256))
_A_OFF = {}
_o = 0
for _n, _w in _A_SPLITS:
    _A_OFF[_n] = (_o, _o + _w)
    _o += _w
A_COLS = _o

TM_INPROJ = 1024
TM_A = 1024
TN_MERGE = 256
TQ_ATT = 256
MLA_TQ = 512
MLA_TK = 512
MLA_WIDE = 4
MLA_GROUP = 4
SWA_TQ = 1024
HG_ROWS = 512
TM_FFN = 1024
TF_FFN = 256
MOE_TB = 512
MOE_TF = 1792
TM_ROUTER = 512
TM_COMB = 512
TM_DISP = 1024
VMEM_LIMIT = 56 * 1024 * 1024
assert MLA_WIDE == 4 and MLA_TQ % MLA_TK == 0


def _cparams(sem, **kw):
    return pltpu.CompilerParams(dimension_semantics=sem, vmem_limit_bytes=VMEM_LIMIT, **kw)


def _const_spec(shape):
    nd = len(shape)
    return pl.BlockSpec(shape, lambda *_: (0,) * nd, pipeline_mode=pl.Buffered(1))


def _layernorm(v, g, b):
    mu = jnp.mean(v, axis=-1, keepdims=True)
    vc = v - mu
    var = jnp.mean(vc * vc, axis=-1, keepdims=True)
    return vc * lax.rsqrt(var + EPS) * g + b


def _dot(a, b):
    return jnp.dot(a, b, preferred_element_type=F32)


def _dot_nt(a, b):
    return lax.dot_general(a, b, (((1,), (1,)), ((), ())), preferred_element_type=F32)


def _split3(a):
    hi = a.astype(BF16)
    r = a - hi.astype(F32)
    mid = r.astype(BF16)
    lo = (r - mid.astype(F32)).astype(BF16)
    return hi, mid, lo


def _rope_kernel(pos_ref, freq_ref, c_ref, s_ref):
    lane = lax.broadcasted_iota(I32, pos_ref.shape, 1)
    ang = pos_ref[...] * freq_ref[...]
    rope = (lane >= MLA_NOPE) & (lane < MLA_NOPE + MLA_ROPE)
    first = lane < MLA_NOPE + MLA_ROPE // 2
    c_ref[...] = jnp.where(lane < MLA_NOPE, 1.0, jnp.where(rope, jnp.cos(ang), 0.0))
    sn = jnp.sin(ang)
    s_ref[...] = jnp.where(rope, jnp.where(first, -sn, sn), 0.0)


def _rope_tables(positions):
    n = positions.size
    half = MLA_ROPE // 2
    inv_freq = ROPE_THETA ** (-jnp.arange(half, dtype=F32) / half)
    freq = jnp.zeros((1, LANES), F32).at[0, MLA_NOPE:MLA_NOPE + MLA_ROPE].set(jnp.tile(inv_freq, 2))
    posb = jnp.broadcast_to(positions.reshape(n, 1).astype(F32), (n, LANES))
    tm = 1024
    return pl.pallas_call(
        _rope_kernel,
        grid=(n // tm,),
        in_specs=[pl.BlockSpec((tm, LANES), lambda i: (i, 0)), _const_spec((1, LANES))],
        out_specs=[pl.BlockSpec((tm, LANES), lambda i: (i, 0))] * 2,
        out_shape=[jax.ShapeDtypeStruct((n, LANES), F32)] * 2,
        compiler_params=_cparams(("parallel",)),
        name="rope_tables",
    )(posb, freq)


def _inproj_kernel(x_ref, w_ref, c_ref, s_ref, qn_ref, kvn_ref, wuqa_ref, wuqb_ref, wuk_ref, wuv_ref,
                   mq_ref, mk_ref, mv_ref, swq_ref, swk_ref, swv_ref, hg_ref, sbq_ref, sbk_ref, sbv_ref):
    h = _dot(x_ref[...].astype(BF16), w_ref[...])

    def cols(name):
        lo, hi = _A_OFF[name]
        return h[:, lo:hi]

    c = c_ref[...]
    s = s_ref[...]
    c4 = jnp.concatenate([c] * N_HEADS, axis=1)
    s4 = jnp.concatenate([s] * N_HEADS, axis=1)

    cq = cols('cq')
    cqn = (cq * lax.rsqrt(jnp.mean(cq * cq, axis=-1, keepdims=True) + EPS) * qn_ref[...]).astype(BF16)
    q = _dot(cqn, wuqa_ref[...]) * c4 + _dot(cqn, wuqb_ref[...]) * s4
    mq_ref[...] = (q * (MLA_SCALE * LOG2E)).astype(BF16)

    ckv = cols('ckv')
    ckvn = (ckv * lax.rsqrt(jnp.mean(ckv * ckv, axis=-1, keepdims=True) + EPS) * kvn_ref[...]).astype(BF16)
    krot = cols('kra') * c + cols('krb') * s
    mk_ref[...] = (_dot(ckvn, wuk_ref[...]) + jnp.concatenate([krot] * N_HEADS, axis=1)).astype(BF16)
    mv_ref[...] = _dot(ckvn, wuv_ref[...]).astype(BF16)

    swq_ref[...] = cols('swa_q').astype(BF16)
    swk_ref[...] = cols('swa_k').astype(BF16)
    swv_ref[...] = cols('swa_v').astype(BF16)
    hg_ref[...] = cols('hgrn')
    sbq_ref[...] = cols('sb_q').astype(BF16)
    sbk_ref[...] = cols('sb_k').astype(BF16)
    sbv_ref[...] = cols('sb_v').astype(BF16)


def _inproj_weights(w_in, w_uq, w_ukv):
    def seg(name, width):
        o = _IN_OFF[name]
        return w_in[:, o:o + width]

    kr = seg('mla_kr', MLA_ROPE)
    half = MLA_ROPE // 2
    z64 = jnp.zeros((D_MODEL, MLA_NOPE), F32)
    z32 = jnp.zeros((D_MODEL, LANES - MLA_NOPE - MLA_ROPE), F32)
    kra = jnp.concatenate([z64, kr, z32], axis=1)
    krb = jnp.concatenate([z64, kr[:, half:], kr[:, :half], z32], axis=1)
    swq = seg('swa_q', 256).reshape(D_MODEL, N_HEADS, HEAD_DIM)[:, jnp.array([0, 2, 1, 3])].reshape(D_MODEL, WIDTH)
    w_a = jnp.concatenate([
        seg('mla_cq', 256), seg('mla_ckv', 128), kra, krb,
        swq * QK_SCALE, seg('swa_k', 128), seg('swa_v', 128),
        seg('hgrn', 1024), seg('sb_q', 256) * (QK_SCALE * LOG2E), seg('sb_k', 256), seg('sb_v', 256)], axis=1)

    qd = MLA_NOPE + MLA_ROPE
    zq = jnp.zeros((MLA_Q_LORA, LANES - qd), F32)
    zn = jnp.zeros((MLA_Q_LORA, MLA_NOPE), F32)
    qa, qb = [], []
    for hh in range(N_HEADS):
        nope = w_uq[:, hh * qd: hh * qd + MLA_NOPE]
        rope = w_uq[:, hh * qd + MLA_NOPE: (hh + 1) * qd]
        qa += [nope, rope, zq]
        qb += [zn, rope[:, half:], rope[:, :half], zq]
    wuqa = jnp.concatenate(qa, axis=1)
    wuqb = jnp.concatenate(qb, axis=1)
    lane = jnp.arange(N_HEADS * LANES) % LANES
    wuk = jnp.where(lane[None, :] < MLA_NOPE, w_ukv, 0.0)
    wuv = jnp.concatenate([w_ukv[:, hh * LANES + MLA_NOPE:(hh + 1) * LANES] for hh in range(N_HEADS)], axis=1)
    return tuple(t.astype(BF16) for t in (w_a, wuqa, wuqb, wuk, wuv))


def _inproj(x2d, wts, ctab, stab, q_norm, kv_norm):
    n = x2d.shape[0]
    w_a, wuqa, wuqb, wuk, wuv = wts
    tm = TM_INPROJ
    row = lambda w: pl.BlockSpec((tm, w), lambda i: (i, 0))
    out_w = (512, 512, 256, 256, 128, 128, 1024, 256, 256, 256)
    out_dt = (BF16, BF16, BF16, BF16, BF16, BF16, F32, BF16, BF16, BF16)
    return pl.pallas_call(
        _inproj_kernel,
        grid=(n // tm,),
        in_specs=[row(D_MODEL), _const_spec(w_a.shape), row(LANES), row(LANES),
                  _const_spec((1, MLA_Q_LORA)), _const_spec((1, MLA_KV_LORA)),
                  _const_spec(wuqa.shape), _const_spec(wuqb.shape), _const_spec(wuk.shape),
                  _const_spec(wuv.shape)],
        out_specs=[row(w) for w in out_w],
        out_shape=[jax.ShapeDtypeStruct((n, w), d) for w, d in zip(out_w, out_dt)],
        compiler_params=_cparams(("parallel",)),
        name="inproj",
    )(x2d, w_a, ctab, stab, q_norm.reshape(1, -1), kv_norm.reshape(1, -1), wuqa, wuqb, wuk, wuv)


def _half_mask(half):
    lane = lax.broadcasted_iota(I32, (1, LANES), 1)
    return (lane < HEAD_DIM) if half == 0 else (lane >= HEAD_DIM)


def _mla_kernel(q_ref, k_ref, v_ref, o_ref):
    tq = q_ref.shape[0]
    tk = MLA_TK
    nsub = tq // tk
    i = pl.program_id(1)
    row = lax.broadcasted_iota(I32, (tq, tk), 0)
    col = lax.broadcasted_iota(I32, (tq, tk), 1)
    ones = jnp.ones((1, LANES), BF16)

    def update(off, carry, heads, mask, width=tk):
        ss = [_dot_nt(q_ref[:, hh * LANES:(hh + 1) * LANES],
                      k_ref[pl.ds(off, width), hh * LANES:(hh + 1) * LANES]) for hh in heads]
        if mask is not None:
            ss = [jnp.where(mask, s, NEG_BIG) for s in ss]
        ms = [jnp.maximum(c[0], jnp.max(s, axis=-1, keepdims=True)) for c, s in zip(carry, ss)]
        pms = [jnp.exp2(s - m).astype(BF16) for s, m in zip(ss, ms)]
        new = []
        for n, hh in enumerate(heads):
            vb = v_ref[pl.ds(off, width), (hh // 2) * LANES:(hh // 2 + 1) * LANES]
            vb = jnp.where(_half_mask(hh % 2), vb, ones)
            m, acc = carry[n]
            new.append((ms[n], jnp.exp2(m - ms[n]) * acc + _dot(pms[n], vb)))
        return tuple(new)

    accs = []
    for g in range(0, N_HEADS, MLA_GROUP):
        heads = tuple(range(g, g + MLA_GROUP))
        init = tuple((jnp.full((tq, 1), NEG_BIG, F32), jnp.zeros((tq, LANES), F32)) for _ in heads)
        nkb = i * nsub
        wide = MLA_WIDE * tk
        carry = lax.fori_loop(
            0, nkb // MLA_WIDE,
            lambda j, c, heads=heads: update(pl.multiple_of(j * wide, wide), c, heads, None, wide), init)
        done = nkb // MLA_WIDE * MLA_WIDE
        rest = nkb - done
        carry = lax.cond(
            rest >= 2,
            lambda c, heads=heads: update(pl.multiple_of(done * tk, 2 * tk), c, heads, None, 2 * tk),
            lambda c: c, carry)
        carry = lax.cond(
            rest % 2 == 1,
            lambda c, heads=heads: update(pl.multiple_of((nkb - 1) * tk, tk), c, heads, None),
            lambda c: c, carry)
        for r in range(nsub):
            carry = update(pl.multiple_of(i * tq + r * tk, tk), carry, heads, col + r * tk <= row)
        accs += [c[1] for c in carry]
    outs = []
    for p in range(N_HEADS // 2):
        a0, a1 = accs[2 * p], accs[2 * p + 1]
        outs.append(jnp.where(_half_mask(0), a0 / a0[:, HEAD_DIM:HEAD_DIM + 1], a1 / a1[:, 0:1]))
    o_ref[...] = jnp.concatenate(outs, axis=1).astype(o_ref.dtype)


def _mla_attention(q, k, v, batch, seq):
    tq = MLA_TQ
    q3, k3, v3 = (t.reshape(batch, seq, t.shape[-1]) for t in (q, k, v))
    out = pl.pallas_call(
        _mla_kernel,
        grid=(batch, seq // tq),
        in_specs=[pl.BlockSpec((None, tq, 512), lambda b, i: (b, i, 0)),
                  pl.BlockSpec((None, seq, 512), lambda b, i: (b, 0, 0), pipeline_mode=pl.Buffered(1)),
                  pl.BlockSpec((None, seq, WIDTH), lambda b, i: (b, 0, 0), pipeline_mode=pl.Buffered(1))],
        out_specs=pl.BlockSpec((None, tq, WIDTH), lambda b, i: (b, i, 0)),
        out_shape=jax.ShapeDtypeStruct((batch, seq, WIDTH), BF16),
        compiler_params=_cparams(("parallel", "arbitrary")),
        name="mla_attention",
    )(q3, k3, v3)
    return out.reshape(batch * seq, WIDTH)


def _sb_kernel(q_ref, k_ref, v_ref, o_ref):
    tq = q_ref.shape[0]
    i = pl.program_id(1)
    row = lax.broadcasted_iota(I32, (tq, tq), 0)
    col = lax.broadcasted_iota(I32, (tq, tq), 1)
    strict = col < row
    later = (row > col).astype(BF16)
    qs = []
    for hh in range(N_HEADS):
        qp = q_ref[:, (hh // 2) * LANES:(hh // 2 + 1) * LANES]
        qs.append(jnp.where(_half_mask(hh % 2), qp, jnp.zeros_like(qp)))

    def block(j, carry, diag):
        off = pl.multiple_of(j * tq, tq)
        runs, accs = carry
        heads = range(N_HEADS)
        zs = [_dot_nt(qs[hh], k_ref[pl.ds(off, tq), (hh // 2) * LANES:(hh // 2 + 1) * LANES]) for hh in heads]
        lsps = [jnp.minimum(z, 0.0) - jnp.log2(1.0 + jnp.exp2(-jnp.abs(z))) for z in zs]
        lsns = [lsp - z for lsp, z in zip(lsps, zs)]
        if diag:
            lsns = [jnp.where(strict, t, 0.0) for t in lsns]
        his = [t.astype(BF16) for t in lsns]
        los = [(t - hi.astype(F32)).astype(BF16) for t, hi in zip(lsns, his)]
        rems = [_dot(hi, later) + _dot(lo, later) for hi, lo in zip(his, los)]
        args = [lsps[hh] + rems[hh] + runs[hh] for hh in heads]
        if diag:
            args = [jnp.where(strict, t, NEG_BIG) for t in args]
        probs = [jnp.exp2(t).astype(BF16) for t in args]
        new_runs = tuple(runs[hh] + rems[hh][:, 0:1] + lsns[hh][:, 0:1] for hh in heads)
        new_accs = list(accs)
        for hh in heads:
            p = hh // 2
            vb = v_ref[pl.ds(off, tq), p * LANES:(p + 1) * LANES]
            vb = jnp.where(_half_mask(hh % 2), vb, jnp.zeros_like(vb))
            new_accs[p] = new_accs[p] + _dot(probs[hh], vb)
        return new_runs, tuple(new_accs)

    init = (tuple(jnp.zeros((tq, 1), F32) for _ in range(N_HEADS)),
            tuple(jnp.zeros((tq, LANES), F32) for _ in range(N_HEADS // 2)))
    def still_active(runs):
        top = functools.reduce(jnp.maximum, runs)
        return (jnp.max(top) > SB_RUN_FLOOR).astype(I32)

    runs, accs = block(i, init, True)

    def cond(c):
        return (c[0] < i) & (c[1] > 0)

    def body(c):
        jj, _, runs, accs = c
        runs, accs = block(i - 1 - jj, (runs, accs), False)
        return jj + 1, still_active(runs), runs, accs

    _, _, _, accs = lax.while_loop(cond, body, (jnp.int32(0), still_active(runs), runs, accs))
    o_ref[...] = jnp.concatenate(accs, axis=1).astype(o_ref.dtype)


def _sb_attention(q, k, v, batch, seq):
    tq = TQ_ATT
    q3, k3, v3 = (t.reshape(batch, seq, WIDTH) for t in (q, k, v))
    out = pl.pallas_call(
        _sb_kernel,
        grid=(batch, seq // tq),
        in_specs=[pl.BlockSpec((None, tq, WIDTH), lambda b, i: (b, i, 0)),
                  pl.BlockSpec((None, seq, WIDTH), lambda b, i: (b, 0, 0)),
                  pl.BlockSpec((None, seq, WIDTH), lambda b, i: (b, 0, 0))],
        out_specs=pl.BlockSpec((None, tq, WIDTH), lambda b, i: (b, i, 0)),
        out_shape=jax.ShapeDtypeStruct((batch, seq, WIDTH), BF16),
        compiler_params=_cparams(("parallel", "arbitrary")),
        name="stick_breaking",
    )(q3, k3, v3)
    return out.reshape(batch * seq, WIDTH)


def _rel_bucket(dist):
    exact = REL_BUCKETS // 2
    n = jnp.maximum(dist, 0)
    nf = jnp.maximum(n, 1).astype(F32)
    large = exact + (jnp.log(nf / exact) / math.log(REL_MAX_DIST / exact) * (REL_BUCKETS - exact)).astype(I32)
    large = jnp.clip(large, 0, REL_BUCKETS - 1)
    return jnp.where(n < exact, n, large)


def _swa_kernel(sink_ref, tab_ref, q_ref, kc_ref, kh_ref, vc_ref, vh_ref, pq_ref, pkc_ref, pkh_ref, o_ref):
    w = SWA_WINDOW
    step = pl.program_id(1)
    row = lax.broadcasted_iota(I32, (w, w), 0)
    col = lax.broadcasted_iota(I32, (w, w), 1)
    valid_c = col <= row
    valid_p = col > row
    tabs = [jnp.broadcast_to(tab_ref[hh:hh + 1, :], (w, LANES)) for hh in range(N_HEADS)]
    ones = jnp.ones((1, LANES), BF16)
    nsub = q_ref.shape[0] // w
    chains = [(r, hh) for r in range(nsub) for hh in range(N_HEADS)]

    def real(hh):
        return (hh % 2) * 2 + hh // 2

    def keys(ref, halo_ref, r):
        cur = ref[r * w:(r + 1) * w, :]
        prev = ref[(r - 1) * w:r * w, :] if r else halo_ref[...]
        return cur, prev

    buckets = []
    for r in range(nsub):
        pq = pq_ref[r * w:(r + 1) * w, :]
        pk_prev = pkc_ref[:, (r - 1) * w:r * w] if r else pkh_ref[...]
        buckets.append((_rel_bucket(pq - pkc_ref[:, r * w:(r + 1) * w]), _rel_bucket(pq - pk_prev)))
    logits = []
    for r, hh in chains:
        qp = q_ref[r * w:(r + 1) * w, (hh // 2) * LANES:(hh // 2 + 1) * LANES]
        qh = jnp.where(_half_mask(hh % 2), qp, jnp.zeros_like(qp))
        kc, kp = keys(kc_ref, kh_ref, r)
        logits.append((_dot_nt(qh, kc), _dot_nt(qh, kp)))
    masked = []
    for (r, hh), (lc, lp) in zip(chains, logits):
        lc = jnp.where(valid_c, lc + jnp.take_along_axis(tabs[real(hh)], buckets[r][0], axis=1), NEG_BIG)
        lp = lp + jnp.take_along_axis(tabs[real(hh)], buckets[r][1], axis=1)
        lp = jnp.where(valid_p if r else valid_p & (step > 0), lp, NEG_BIG)
        masked.append((lc, lp))
    maxes = [jnp.maximum(jnp.maximum(jnp.max(lc, axis=-1, keepdims=True), jnp.max(lp, axis=-1, keepdims=True)),
                         sink_ref[real(hh)]) for (r, hh), (lc, lp) in zip(chains, masked)]
    probs = [(jnp.exp(lc - m).astype(BF16), jnp.exp(lp - m).astype(BF16)) for (lc, lp), m in zip(masked, maxes)]
    outs = {}
    for (r, hh), (ec, ep), m in zip(chains, probs, maxes):
        vc, vp = keys(vc_ref, vh_ref, r)
        mine = _half_mask(hh % 2)
        acc = _dot(ec, jnp.where(mine, vc, ones)) + _dot(ep, jnp.where(mine, vp, ones))
        den = (acc[:, 0:1] if hh % 2 else acc[:, HEAD_DIM:HEAD_DIM + 1]) + jnp.exp(sink_ref[real(hh)] - m)
        outs[(r, hh)] = acc / den
    for r in range(nsub):
        pairs = [jnp.where(_half_mask(0), outs[(r, 2 * p)], outs[(r, 2 * p + 1)]) for p in range(N_HEADS // 2)]
        o_ref[r * w:(r + 1) * w, :] = jnp.concatenate(pairs, axis=1).astype(o_ref.dtype)


def _swa_attention(q, k, v, positions, sinks, rel_table, batch, seq):
    w = SWA_WINDOW
    tq = SWA_TQ
    per = tq // w
    kvw = k.shape[-1]
    q3, k3, v3 = (t.reshape(batch, seq, t.shape[-1]) for t in (q, k, v))
    pcol = positions.reshape(batch, seq, 1)
    prow = positions.reshape(batch, 1, seq)
    tab = jnp.zeros((N_HEADS, LANES), F32).at[:, :REL_BUCKETS].set(rel_table.astype(F32).T)
    cur = lambda b, n: (b, n, 0)
    halo = lambda b, n: (b, jnp.maximum(n * per - 1, 0), 0)
    out = pl.pallas_call(
        _swa_kernel,
        grid=(batch, seq // tq),
        in_specs=[pl.BlockSpec(memory_space=pltpu.SMEM), _const_spec((N_HEADS, LANES)),
                  pl.BlockSpec((None, tq, WIDTH), cur),
                  pl.BlockSpec((None, tq, kvw), cur), pl.BlockSpec((None, w, kvw), halo),
                  pl.BlockSpec((None, tq, kvw), cur), pl.BlockSpec((None, w, kvw), halo),
                  pl.BlockSpec((None, tq, 1), cur),
                  pl.BlockSpec((None, 1, tq), lambda b, n: (b, 0, n)),
                  pl.BlockSpec((None, 1, w), lambda b, n: (b, 0, jnp.maximum(n * per - 1, 0)))],
        out_specs=pl.BlockSpec((None, tq, WIDTH), cur),
        out_shape=jax.ShapeDtypeStruct((batch, seq, WIDTH), BF16),
        compiler_params=_cparams(("parallel", "arbitrary")),
        name="swa_attention",
    )(sinks.astype(F32), tab, q3, k3, k3, v3, v3, pcol, prow, prow)
    return out.reshape(batch * seq, WIDTH)


def _hgrn_kernel(hg_ref, lb_ref, nw_ref, o_ref, state_ref):
    c = HGRN_CHUNK
    blk = HGRN_BLOCK

    @pl.when(pl.program_id(1) == 0)
    def _():
        state_ref[...] = jnp.zeros_like(state_ref)

    r64 = lax.broadcasted_iota(I32, (c, c), 0)
    c64 = lax.broadcasted_iota(I32, (c, c), 1)
    incl = (c64 <= r64).astype(BF16)
    ra = lax.broadcasted_iota(I32, (WIDTH, WIDTH), 0) // HEAD_DIM
    ca = lax.broadcasted_iota(I32, (WIDTH, WIDTH), 1) // HEAD_DIM
    same_head = ra == ca
    seg = same_head.astype(BF16)
    ones_cols = jnp.ones((c, LANES), BF16)
    trow = lax.broadcasted_iota(I32, (blk, WIDTH), 0)
    caps = [jnp.where(trow >= s_i, 0.0, NEG_BIG) for s_i in range(blk)]
    lane_head = lax.broadcasted_iota(I32, (1, WIDTH), 1) // HEAD_DIM
    lb = lb_ref[...]
    nw = nw_ref[...]
    dn0 = (((0,), (0,)), ((), ()))

    for ch in range(hg_ref.shape[0] // c):
        rows = slice(ch * c, (ch + 1) * c)
        qraw = hg_ref[rows, 0:WIDTH]
        fraw = hg_ref[rows, WIDTH:2 * WIDTH]
        v = hg_ref[rows, 2 * WIDTH:3 * WIDTH]
        graw = hg_ref[rows, 3 * WIDTH:4 * WIDTH]
        qf = qraw * jax.nn.sigmoid(qraw)
        forget = lb + (1.0 - lb) * jax.nn.sigmoid(fraw)
        lf = jnp.log(forget)
        kk = 1.0 - forget
        gate = graw * jax.nn.sigmoid(graw)
        vb = v.astype(BF16)

        lf3 = _split3(lf)
        bc = _dot(incl, lf3[0]) + _dot(incl, lf3[1]) + _dot(incl, lf3[2])
        b_last = bc[c - 1:c, :]
        tot_col = sum(lax.dot_general(t, ones_cols, dn0, preferred_element_type=F32) for t in lf3)
        decay_col = jnp.exp(jnp.concatenate([tot_col, tot_col], axis=1))

        state = state_ref[...]
        o_inter = _dot((qf * jnp.exp(bc)).astype(BF16), state.astype(BF16))

        def before(qa, qb, ka, kb):
            ref = bc[kb - 1:kb, :]
            qt = qf[qa:qb] * jnp.exp(bc[qa:qb] - ref)
            kt = (kk[ka:kb] * jnp.exp(ref - bc[ka:kb])).astype(BF16)
            qs = jnp.concatenate([jnp.where(lane_head == hh, qt, 0.0) for hh in range(N_HEADS)], axis=0)
            att = _dot_nt(qs.astype(BF16), kt)
            mix = _dot(att.astype(BF16), vb[ka:kb])
            nq = qb - qa
            return sum(jnp.where(lane_head == hh, mix[hh * nq:(hh + 1) * nq], 0.0) for hh in range(N_HEADS))

        bc2 = bc * LOG2E
        key2 = bc2 - jnp.log2(jnp.maximum(kk, 0.0))

        def inside(a):
            b2 = bc2[a:a + blk]
            qb_ = qf[a:a + blk]
            ws = []
            for s_i in range(blk):
                ws.append(qb_ * jnp.exp2(jnp.minimum(b2 - key2[a + s_i:a + s_i + 1, :], caps[s_i])))
            att = _dot(jnp.concatenate(ws, axis=0).astype(BF16), seg)
            return sum(att[s_i * blk:(s_i + 1) * blk] * v[a + s_i:a + s_i + 1, :] for s_i in range(blk))

        pieces = {a: [] for a in range(0, c, blk)}

        def cover(a, b):
            if b - a == blk:
                pieces[a].append(inside(a))
                return
            mid = (a + b) // 2
            cover(a, mid)
            cover(mid, b)
            res = before(mid, b, a, mid)
            for off in range(0, b - mid, blk):
                pieces[mid + off].append(res[off:off + blk])

        cover(0, c)
        o = o_inter + jnp.concatenate([sum(pieces[a]) for a in range(0, c, blk)], axis=0)

        khat = (kk * jnp.exp(b_last - bc)).astype(BF16)
        upd = lax.dot_general(khat, vb, dn0, preferred_element_type=F32)
        state_ref[...] = decay_col * state + jnp.where(same_head, upd, 0.0)

        o2 = _split3(o * o)
        ms = (_dot(o2[0], seg) + _dot(o2[1], seg)) * (1.0 / HEAD_DIM)
        o_ref[rows, :] = (o * lax.rsqrt(ms + EPS) * nw * gate).astype(o_ref.dtype)


def _hgrn(hg, lower_bound, norm_w, batch, seq):
    rows = HG_ROWS
    hg3 = hg.reshape(batch, seq, 4 * WIDTH)
    out = pl.pallas_call(
        _hgrn_kernel,
        grid=(batch, seq // rows),
        in_specs=[pl.BlockSpec((None, rows, 4 * WIDTH), lambda b, i: (b, i, 0)),
                  _const_spec((1, WIDTH)), _const_spec((1, WIDTH))],
        out_specs=pl.BlockSpec((None, rows, WIDTH), lambda b, i: (b, i, 0)),
        out_shape=jax.ShapeDtypeStruct((batch, seq, WIDTH), BF16),
        scratch_shapes=[pltpu.VMEM((WIDTH, WIDTH), F32)],
        compiler_params=_cparams(("parallel", "arbitrary")),
        name="hgrn2",
    )(hg3, lower_bound.reshape(1, WIDTH).astype(F32), norm_w.reshape(1, WIDTH).astype(F32))
    return out.reshape(batch * seq, WIDTH)


def _merge_body(x, y_refs, wg_ref, wb_ref, wo_ref, g, b):
    xb = x.astype(BF16)
    ys = [y_ref[...] for y_ref in y_refs]
    cols = []
    for lo in range(0, D_MODEL, TN_MERGE):
        part = None
        for nbr, yb in enumerate(ys):
            gate = jax.nn.sigmoid(_dot(xb, wg_ref[:, nbr * D_MODEL + lo:nbr * D_MODEL + lo + TN_MERGE]))
            term = gate * _dot(yb, wb_ref[nbr, :, lo:lo + TN_MERGE])
            part = term if part is None else part + term
        cols.append(part.astype(BF16))
    y = _dot(jnp.concatenate(cols, axis=1), wo_ref[...])
    return _layernorm(ALPHA * x + y, g, b)


def _memkv_kernel(m_ref, w_ref, k_ref, v_ref):
    kv = _dot(m_ref[...].astype(BF16), w_ref[...])
    k_ref[...] = kv[:, :WIDTH].astype(BF16)
    v_ref[...] = kv[:, WIDTH:].astype(BF16)


def _memkv(mem, wkv):
    batch, m, _ = mem.shape
    return pl.pallas_call(
        _memkv_kernel,
        grid=(batch,),
        in_specs=[pl.BlockSpec((None, m, D_MODEL), lambda b: (b, 0, 0)), _const_spec(wkv.shape)],
        out_specs=[pl.BlockSpec((None, m, WIDTH), lambda b: (b, 0, 0))] * 2,
        out_shape=[jax.ShapeDtypeStruct((batch, m, WIDTH), BF16)] * 2,
        compiler_params=_cparams(("parallel",)),
        name="mem_kv",
    )(mem, wkv)


def _xattn_body(x, wq_ref, k, v, wo_ref, g, b):
    q = _dot(x.astype(BF16), wq_ref[...]).astype(BF16)
    lane = lax.broadcasted_iota(I32, (1, WIDTH), 1) // HEAD_DIM
    heads = range(N_HEADS)
    ss = [_dot_nt(jnp.where(lane == hh, q, jnp.zeros_like(q)), k) for hh in heads]
    es = [jnp.exp(s - jnp.max(s, axis=-1, keepdims=True)) for s in ss]
    ps = [(e / jnp.sum(e, axis=-1, keepdims=True)).astype(BF16) for e in es]
    o = jnp.zeros((x.shape[0], WIDTH), F32)
    for hh in heads:
        o = o + jnp.where(lane == hh, _dot(ps[hh], v), 0.0)
    y = _dot(o.astype(BF16), wo_ref[...])
    return _layernorm(ALPHA * x + y, g, b)


def _merge_xattn_kernel(x_ref, y0_ref, y1_ref, y2_ref, y3_ref, wg_ref, wb_ref, wo_ref, g1_ref, b1_ref,
                        wq_ref, k_ref, v_ref, xwo_ref, g2_ref, b2_ref, o_ref):
    x1 = _merge_body(x_ref[...], (y0_ref, y1_ref, y2_ref, y3_ref), wg_ref, wb_ref, wo_ref,
                     g1_ref[...], b1_ref[...])
    o_ref[...] = _xattn_body(x1, wq_ref, k_ref[...], v_ref[...], xwo_ref, g2_ref[...], b2_ref[...])


def _merge_xattn(x2d, ys, wg, wb, wo, g1, b1, wq, k, v, xwo, g2, b2, batch, seq):
    tm = TM_A
    m = k.shape[1]
    per = seq // tm
    row = lambda w: pl.BlockSpec((tm, w), lambda bb, i: (bb * per + i, 0))
    kv_spec = pl.BlockSpec((None, m, WIDTH), lambda bb, i: (bb, 0, 0))
    vec = _const_spec((1, D_MODEL))
    return pl.pallas_call(
        _merge_xattn_kernel,
        grid=(batch, per),
        in_specs=[row(D_MODEL)] + [row(WIDTH)] * 4 +
                 [_const_spec(wg.shape), _const_spec(wb.shape), _const_spec(wo.shape), vec, vec,
                  _const_spec(wq.shape), kv_spec, kv_spec, _const_spec(xwo.shape), vec, vec],
        out_specs=row(D_MODEL),
        out_shape=jax.ShapeDtypeStruct((batch * seq, D_MODEL), F32),
        compiler_params=_cparams(("parallel", "parallel")),
        name="merge_xattn_ln",
    )(x2d, *ys, wg, wb, wo, g1.reshape(1, -1), b1.reshape(1, -1),
      wq, k, v, xwo, g2.reshape(1, -1), b2.reshape(1, -1))


def _ffn_kernel(x_ref, w13_ref, w2_ref, g_ref, b_ref, o_ref):
    x = x_ref[...]
    xb = x.astype(BF16)
    y = None
    for h in range(F_DENSE // TF_FFN):
        lo = h * TF_FFN
        a = _dot(xb, w13_ref[:, lo:lo + TF_FFN])
        gate = _dot(xb, w13_ref[:, F_DENSE + lo:F_DENSE + lo + TF_FFN])
        part = _dot((a * jax.nn.sigmoid(a) * gate).astype(BF16), w2_ref[lo:lo + TF_FFN, :])
        y = part if y is None else y + part
    o_ref[...] = _layernorm(ALPHA * x + y, g_ref[...], b_ref[...])


def _ffn(x2d, w13, w2, g, b):
    n = x2d.shape[0]
    tm = TM_FFN
    return pl.pallas_call(
        _ffn_kernel,
        grid=(n // tm,),
        in_specs=[pl.BlockSpec((tm, D_MODEL), lambda i: (i, 0)),
                  _const_spec(w13.shape), _const_spec(w2.shape),
                  _const_spec((1, D_MODEL)), _const_spec((1, D_MODEL))],
        out_specs=pl.BlockSpec((tm, D_MODEL), lambda i: (i, 0)),
        out_shape=jax.ShapeDtypeStruct((n, D_MODEL), F32),
        compiler_params=_cparams(("parallel",)),
        name="ffn_ln",
    )(x2d, w13, w2, g.reshape(1, -1), b.reshape(1, -1))


def _router_kernel(x_ref, r_ref, info_ref, wts_ref, cnt_ref, carry_ref):
    tm = x_ref.shape[0]

    @pl.when(pl.program_id(0) == 0)
    def _():
        carry_ref[...] = jnp.zeros_like(carry_ref)

    logits = jnp.dot(x_ref[...], r_ref[...], precision=lax.Precision.HIGHEST, preferred_element_type=F32)
    lane = lax.broadcasted_iota(I32, (tm, LANES), 1)
    lg = jnp.where(lane < N_EXPERTS, logits, -jnp.inf)
    m1 = jnp.max(lg, axis=-1, keepdims=True)
    i1 = jnp.min(jnp.where(lg == m1, lane, LANES), axis=-1, keepdims=True)
    lg2 = jnp.where(lane == i1, -jnp.inf, lg)
    m2 = jnp.max(lg2, axis=-1, keepdims=True)
    i2 = jnp.min(jnp.where(lg2 == m2, lane, LANES), axis=-1, keepdims=True)
    e = jnp.exp(m2 - m1)
    w1 = 1.0 / (1.0 + e)
    w2 = e / (1.0 + e)
    sel1 = lane == i1
    sel2 = lane == i2
    chosen = jnp.where(sel1 | sel2, 1.0, 0.0)
    row = lax.broadcasted_iota(I32, (tm, tm), 0)
    col = lax.broadcasted_iota(I32, (tm, tm), 1)
    before = (col < row).astype(BF16)
    ranks = _dot(before, chosen.astype(BF16)) + carry_ref[...]
    r1 = jnp.sum(jnp.where(sel1, ranks, 0.0), axis=-1, keepdims=True)
    r2 = jnp.sum(jnp.where(sel2, ranks, 0.0), axis=-1, keepdims=True)
    carry_ref[...] = carry_ref[...] + jnp.sum(chosen, axis=0, keepdims=True)
    info = jnp.where(lane == 0, i1.astype(F32), jnp.where(lane == 1, i2.astype(F32),
                     jnp.where(lane == 2, r1, jnp.where(lane == 3, r2, 0.0))))
    info_ref[...] = jnp.transpose(info)[:info_ref.shape[0], :]
    wts_ref[...] = jnp.where(lane == 0, w1, jnp.where(lane == 1, w2, 0.0))
    cnt_ref[...] = carry_ref[...]


def _router(x2d, router):
    n = x2d.shape[0]
    tm = TM_ROUTER
    r_pad = jnp.zeros((D_MODEL, LANES), F32).at[:, :N_EXPERTS].set(router.astype(F32))
    row = pl.BlockSpec((tm, LANES), lambda i: (i, 0))
    return pl.pallas_call(
        _router_kernel,
        grid=(n // tm,),
        in_specs=[pl.BlockSpec((tm, D_MODEL), lambda i: (i, 0)), _const_spec(r_pad.shape)],
        out_specs=[pl.BlockSpec((8, tm), lambda i: (0, i)), row, pl.BlockSpec((1, LANES), lambda i: (0, 0))],
        out_shape=[jax.ShapeDtypeStruct((8, n), F32), jax.ShapeDtypeStruct((n, LANES), F32),
                   jax.ShapeDtypeStruct((1, LANES), F32)],
        scratch_shapes=[pltpu.VMEM((1, LANES), F32)],
        compiler_params=_cparams(("arbitrary",)),
        name="moe_router",
    )(x2d, r_pad)


def _dispatch_kernel(pad_ref, d0_ref, d1_ref, x_ref, xb_hbm, stage_ref, sems):
    tm = x_ref.shape[0]
    i = pl.program_id(0)
    last = pl.num_programs(0) - 1
    slot = i % 2

    def wait_step(s):
        for _ in range(2):
            pltpu.make_async_copy(stage_ref.at[s], xb_hbm.at[pl.ds(0, tm), :], sems.at[s]).wait()

    @pl.when(i >= 2)
    def _():
        wait_step(slot)

    for s in range(2):
        @pl.when(slot == s)
        def _():
            stage_ref[s] = x_ref[...]

            def issue(r, c):
                for k in range(2):
                    pltpu.make_async_copy(stage_ref.at[s, pl.ds(r, 1), :],
                                          xb_hbm.at[pl.ds((d0_ref, d1_ref)[k][0, r], 1), :],
                                          sems.at[s]).start(priority=k)
                return c
            lax.fori_loop(0, tm, issue, 0, unroll=8)

    @pl.when(i == last)
    def _():
        def fill(e, c):
            def one(s, c2):
                pltpu.make_async_copy(stage_ref.at[slot, pl.ds(0, 1), :], xb_hbm.at[pl.ds(s, 1), :],
                                      sems.at[2]).start()
                return c2

            def done(s, c2):
                pltpu.make_async_copy(stage_ref.at[slot, pl.ds(0, 1), :], xb_hbm.at[pl.ds(0, 1), :],
                                      sems.at[2]).wait()
                return c2
            lax.fori_loop(pad_ref[0, e], pad_ref[1, e], one, 0)
            lax.fori_loop(pad_ref[0, e], pad_ref[1, e], done, 0)
            return c
        lax.fori_loop(0, pad_ref.shape[1], fill, 0)
        wait_step(slot)

        @pl.when(last >= 1)
        def _():
            wait_step(1 - slot)


def _dispatch(x2d, dest, pads, nblk):
    n = x2d.shape[0]
    tm = TM_DISP
    nt = n // tm
    grid_spec = pltpu.PrefetchScalarGridSpec(
        num_scalar_prefetch=1,
        grid=(nt,),
        in_specs=[pl.BlockSpec((None, 1, tm), lambda i, pads: (i, 0, 0), memory_space=pltpu.SMEM),
                  pl.BlockSpec((None, 1, tm), lambda i, pads: (i, 0, 0), memory_space=pltpu.SMEM),
                  pl.BlockSpec((tm, D_MODEL), lambda i, pads: (i, 0))],
        out_specs=pl.BlockSpec(memory_space=pl.ANY),
        scratch_shapes=[pltpu.VMEM((2, tm, D_MODEL), F32), pltpu.SemaphoreType.DMA((3,))],
    )
    return pl.pallas_call(
        _dispatch_kernel,
        grid_spec=grid_spec,
        out_shape=jax.ShapeDtypeStruct((nblk * MOE_TB, D_MODEL), F32),
        compiler_params=_cparams(("arbitrary",), disable_bounds_checks=True),
        name="moe_dispatch",
    )(pads, dest[0].reshape(nt, 1, tm), dest[1].reshape(nt, 1, tm), x2d)


def _expert_kernel(nused_ref, bexp_ref, x_ref, w1_ref, w3_ref, w2_ref, o_ref, acc_ref):
    f = pl.program_id(1)

    @pl.when(pl.program_id(0) < nused_ref[0])
    def _():
        xb = x_ref[...].astype(BF16)
        a = _dot(xb, w1_ref[...])
        gate = _dot(xb, w3_ref[...])
        part = _dot((a * jax.nn.sigmoid(a) * gate).astype(BF16), w2_ref[...])

        @pl.when(f == 0)
        def _():
            acc_ref[...] = part

        @pl.when(f > 0)
        def _():
            acc_ref[...] += part

        @pl.when(f == pl.num_programs(1) - 1)
        def _():
            o_ref[...] = acc_ref[...]

    @pl.when(pl.program_id(0) >= nused_ref[0])
    def _():
        o_ref[...] = jnp.zeros_like(o_ref)


def _experts(xb, w13, w2, nused, blk_exp, nblk):
    tb, tf = MOE_TB, MOE_TF
    nf = F_EXPERT // tf
    w13 = w13.astype(BF16)

    def blk(i, nu):
        return jnp.maximum(jnp.minimum(i, nu[0] - 1), 0)

    def ftile(i, f, nu):
        return jnp.where(i < nu[0], f, nf - 1)

    grid_spec = pltpu.PrefetchScalarGridSpec(
        num_scalar_prefetch=2,
        grid=(nblk, nf),
        in_specs=[pl.BlockSpec((tb, D_MODEL), lambda i, f, nu, be: (blk(i, nu), 0)),
                  pl.BlockSpec((None, D_MODEL, tf), lambda i, f, nu, be: (be[blk(i, nu)], 0, ftile(i, f, nu))),
                  pl.BlockSpec((None, D_MODEL, tf), lambda i, f, nu, be: (be[blk(i, nu)], 0, nf + ftile(i, f, nu))),
                  pl.BlockSpec((None, tf, D_MODEL), lambda i, f, nu, be: (be[blk(i, nu)], ftile(i, f, nu), 0))],
        out_specs=pl.BlockSpec((tb, D_MODEL), lambda i, f, nu, be: (i, 0)),
        scratch_shapes=[pltpu.VMEM((tb, D_MODEL), F32)],
    )
    return pl.pallas_call(
        _expert_kernel,
        grid_spec=grid_spec,
        out_shape=jax.ShapeDtypeStruct((nblk * tb, D_MODEL), F32),
        compiler_params=_cparams(("arbitrary", "arbitrary")),
        name="moe_experts",
    )(nused, blk_exp, xb, w13, w13, w2.astype(BF16))


def _combine_kernel(d0_ref, d1_ref, n0_ref, n1_ref, y_hbm, x_ref, wts_ref, g_ref, b_ref, o_ref, buf_ref, sems):
    tm = x_ref.shape[0]
    i = pl.program_id(0)
    slot = i % 2

    def gather(idx_refs, s):
        def issue(r, c):
            for k in range(2):
                pltpu.make_async_copy(y_hbm.at[pl.ds(idx_refs[k][0, r], 1), :],
                                      buf_ref.at[s, k, pl.ds(r, 1), :],
                                      sems.at[s]).start(priority=k)
            return c
        lax.fori_loop(0, tm, issue, 0, unroll=8)

    @pl.when(i == 0)
    def _():
        gather((d0_ref, d1_ref), 0)

    for s in range(2):
        @pl.when((i + 1 < pl.num_programs(0)) & (slot != s))
        def _():
            gather((n0_ref, n1_ref), s)

    for k in range(2):
        pltpu.make_async_copy(y_hbm.at[pl.ds(0, tm), :], buf_ref.at[slot, k], sems.at[slot]).wait()
    wts = wts_ref[...]
    y = wts[:, 0:1] * buf_ref[slot, 0] + wts[:, 1:2] * buf_ref[slot, 1]
    o_ref[...] = _layernorm(ALPHA * x_ref[...] + y, g_ref[...], b_ref[...])


def _combine(yb, dest, x2d, wts, g, b):
    n = x2d.shape[0]
    tm = TM_COMB
    nt = n // tm
    row = lambda w: pl.BlockSpec((tm, w), lambda i: (i, 0))
    cur = pl.BlockSpec((None, 1, tm), lambda i: (i, 0, 0), memory_space=pltpu.SMEM)
    nxt = pl.BlockSpec((None, 1, tm), lambda i: (jnp.minimum(i + 1, nt - 1), 0, 0), memory_space=pltpu.SMEM)
    d0, d1 = (d.reshape(nt, 1, tm) for d in dest)
    return pl.pallas_call(
        _combine_kernel,
        grid=(nt,),
        in_specs=[cur, cur, nxt, nxt, pl.BlockSpec(memory_space=pl.ANY), row(D_MODEL), row(LANES),
                  _const_spec((1, D_MODEL)), _const_spec((1, D_MODEL))],
        out_specs=row(D_MODEL),
        out_shape=jax.ShapeDtypeStruct((n, D_MODEL), F32),
        scratch_shapes=[pltpu.VMEM((2, 2, tm, D_MODEL), F32), pltpu.SemaphoreType.DMA((2,))],
        compiler_params=_cparams(("arbitrary",), disable_bounds_checks=True),
        name="moe_combine_ln",
    )(d0, d1, d0, d1, yb, x2d, wts, g.reshape(1, -1), b.reshape(1, -1))


def _moe(x2d, router, w13, w2, g, b):
    n = x2d.shape[0]
    tb = MOE_TB
    info, wts, cnt = _router(x2d, router)
    counts = cnt[0, :N_EXPERTS].astype(I32)
    padded = (counts + tb - 1) // tb * tb
    pend = jnp.cumsum(padded)
    pstart = pend - padded
    info = info.astype(I32)
    dest = tuple((sum(jnp.where(info[k] == e, pstart[e], 0) for e in range(N_EXPERTS)) + info[2 + k]).astype(I32)
                 for k in range(2))
    nblk = -(-(2 * n + N_EXPERTS * (tb - 1)) // tb)
    pads = jnp.stack([jnp.append(pstart + counts, pend[-1]), jnp.append(pend, nblk * tb)]).astype(I32)
    nused = (pend[-1] // tb).astype(I32).reshape(1)
    first_row = jnp.arange(nblk, dtype=I32) * tb
    blk_exp = jnp.minimum(jnp.sum(pend[None, :] <= first_row[:, None], axis=1), N_EXPERTS - 1).astype(I32)
    xb = _dispatch(x2d, dest, pads, nblk)
    yb = _experts(xb, w13, w2, nused, blk_exp, nblk)
    return _combine(yb, dest, x2d, wts, g, b)


def kernel(x, mem, positions, rel_bias_table, hgrn_lb_logits, w_in, mla_q_norm, mla_w_uq, mla_kv_norm, mla_w_ukv, swa_sinks, hgrn_norm, w_branch, w_out, ln_g, ln_b, xa_wq, xa_wkv, xa_wo, ffn_w13, ffn_w2, moe_router, moe_w13, moe_w2):
    batch, seq, _ = x.shape
    n = batch * seq
    sm = jax.nn.softmax(hgrn_lb_logits.astype(F32), axis=0)
    lower_bounds = jnp.cumsum(sm, axis=0) - sm[0]
    ctab, stab = _rope_tables(positions)
    xc = x.reshape(n, D_MODEL)
    for l in range(DEPTH):
        wts = _inproj_weights(w_in[l], mla_w_uq[l], mla_w_ukv[l])
        mq, mk, mv, swq, swk, swv, hg, sbq, sbk, sbv = _inproj(xc, wts, ctab, stab, mla_q_norm[l], mla_kv_norm[l])
        y_mla = _mla_attention(mq, mk, mv, batch, seq)
        y_swa = _swa_attention(swq, swk, swv, positions, swa_sinks[l], rel_bias_table, batch, seq)
        y_hg = _hgrn(hg, lower_bounds[l], hgrn_norm[l], batch, seq)
        y_sb = _sb_attention(sbq, sbk, sbv, batch, seq)
        go = _IN_OFF['gates']
        mk_, mv_ = _memkv(mem, xa_wkv[l].astype(BF16))
        wb = w_branch[l].at[1].set(
            w_branch[l][1].reshape(N_HEADS, HEAD_DIM, D_MODEL)[jnp.array([0, 2, 1, 3])].reshape(WIDTH, D_MODEL))
        xc = _merge_xattn(xc, (y_mla, y_swa, y_hg, y_sb), w_in[l][:, go:].astype(BF16), wb.astype(BF16),
                          w_out[l].astype(BF16), ln_g[l, 0], ln_b[l, 0],
                          (xa_wq[l] * QK_SCALE).astype(BF16), mk_, mv_, xa_wo[l].astype(BF16),
                          ln_g[l, 1], ln_b[l, 1], batch, seq)
        if l % 2 == 0:
            xc = _ffn(xc, ffn_w13[l // 2].astype(BF16), ffn_w2[l // 2].astype(BF16), ln_g[l, 2], ln_b[l, 2])
        else:
            xc = _moe(xc, moe_router[l // 2], moe_w13[l // 2], moe_w2[l // 2],
                      ln_g[l, 2], ln_b[l, 2])
    return xc.reshape(batch, seq, D_MODEL)
```
